```python
import math
import jax, jax.numpy as jnp
from jax import lax
import numpy as np

D_MODEL = 1024
BATCH = 8
SEQ = 16384
DEPTH = 2

N_MIXERS = 2
N_CONV = (DEPTH + 1) // 2
N_ATTN = DEPTH // 2
CONV_WIDTH = 3
HEAD_DIM = 64
N_HEADS = D_MODEL // HEAD_DIM
Q_BLOCK = 128
RMS_EPS = 1e-6

kernel_name = "hybrid_shortconv_forgetting_attention"


def _rmsnorm(x, g):
    xf = x.astype(jnp.float32)
    inv = lax.rsqrt(jnp.mean(xf * xf, axis=-1, keepdims=True) + RMS_EPS)
    return (xf * inv * g.astype(jnp.float32)).astype(x.dtype)


def _short_conv_layer(x, norm_g, w_in, conv_w, w_out):
    h = _rmsnorm(x, norm_g)
    proj = jnp.einsum('bsd,de->bse', h, w_in)
    b_g, c_g, xin, z = jnp.split(proj, 4, axis=-1)
    u = c_g * xin
    y = lax.conv_general_dilated(
        u, conv_w[:, None, :].astype(u.dtype),
        window_strides=(1,), padding=[(CONV_WIDTH - 1, 0)],
        dimension_numbers=('NWC', 'WIO', 'NWC'),
        feature_group_count=D_MODEL)
    y = b_g * y * jax.nn.silu(z)
    return x + jnp.einsum('bse,ed->bsd', y, w_out)


def _forgetting_attention(q, k, v, log_f):
    S = q.shape[2]
    c = jnp.cumsum(log_f, axis=-1)
    scale = 1.0 / math.sqrt(HEAD_DIM)
    kpos = jnp.arange(S)
    neg = jnp.finfo(jnp.float32).min

    def block(i):
        start = i * Q_BLOCK
        qb = lax.dynamic_slice_in_dim(q, start, Q_BLOCK, axis=2)
        cb = lax.dynamic_slice_in_dim(c, start, Q_BLOCK, axis=2)
        s = jnp.einsum('bhqd,bhkd->bhqk', qb, k) * scale
        s = s + cb[..., :, None] - c[..., None, :]
        qpos = start + jnp.arange(Q_BLOCK)
        s = jnp.where(kpos[None, :] <= qpos[:, None], s, neg)
        p = jax.nn.softmax(s, axis=-1)
        return jnp.einsum('bhqk,bhkd->bhqd', p, v)

    out = lax.map(block, jnp.arange(S // Q_BLOCK))
    nb, b, h, qn, dh = out.shape
    return jnp.transpose(out, (1, 2, 0, 3, 4)).reshape(b, h, nb * qn, dh)


def _attn_layer(x, norm_g, w_in, b_f, q_g, k_g, w_out):
    B, S, D = x.shape
    h = _rmsnorm(x, norm_g)
    proj = jnp.einsum('bsd,de->bse', h, w_in)
    q = proj[..., 0 * D:1 * D]
    k = proj[..., 1 * D:2 * D]
    v = proj[..., 2 * D:3 * D]
    z = proj[..., 3 * D:4 * D]
    f_logit = proj[..., 4 * D:].astype(jnp.float32) + b_f.astype(jnp.float32)
    to_heads = lambda t: jnp.transpose(t.reshape(B, S, N_HEADS, HEAD_DIM), (0, 2, 1, 3))
    q = _rmsnorm(to_heads(q), q_g).astype(jnp.float32)
    k = _rmsnorm(to_heads(k), k_g).astype(jnp.float32)
    v = to_heads(v).astype(jnp.float32)
    log_f = jnp.transpose(jax.nn.log_sigmoid(f_logit), (0, 2, 1))
    o = _forgetting_attention(q, k, v, log_f)
    o = jnp.transpose(o, (0, 2, 1, 3)).reshape(B, S, D).astype(x.dtype)
    o = o * jax.nn.silu(z)
    return x + jnp.einsum('bse,ed->bsd', o, w_out)


def _fwd_setup_inputs(seed: int = 0) -> dict:
    key = jax.random.key(seed)
    ks = jax.random.split(key, 12)
    D = D_MODEL
    s_d = D ** -0.5
    x = jax.random.normal(ks[0], (BATCH, SEQ, D), jnp.float32)
    conv_norm_g = 1.0 + 0.02 * jax.random.normal(ks[1], (N_CONV, D), jnp.float32)
    conv_w_in = jax.random.normal(ks[2], (N_CONV, D, 4 * D), jnp.float32) * s_d
    conv_w = jax.random.normal(ks[3], (N_CONV, CONV_WIDTH, D), jnp.float32) * (CONV_WIDTH ** -0.5)
    conv_w_out = jax.random.normal(ks[4], (N_CONV, D, D), jnp.float32) * s_d
    attn_norm_g = 1.0 + 0.02 * jax.random.normal(ks[5], (N_ATTN, D), jnp.float32)
    attn_w_in = jax.random.normal(ks[6], (N_ATTN, D, 4 * D + N_HEADS), jnp.float32) * s_d
    attn_b_f = jax.random.uniform(ks[7], (N_ATTN, N_HEADS), jnp.float32, 1.0, 4.0)
    attn_q_norm_g = 1.0 + 0.02 * jax.random.normal(ks[8], (N_ATTN, HEAD_DIM), jnp.float32)
    attn_k_norm_g = 1.0 + 0.02 * jax.random.normal(ks[9], (N_ATTN, HEAD_DIM), jnp.float32)
    attn_w_out = jax.random.normal(ks[10], (N_ATTN, D, D), jnp.float32) * s_d
    return {"x": x, "conv_norm_g": conv_norm_g, "conv_w_in": conv_w_in, "conv_w": conv_w,
            "conv_w_out": conv_w_out, "attn_norm_g": attn_norm_g, "attn_w_in": attn_w_in,
            "attn_b_f": attn_b_f, "attn_q_norm_g": attn_q_norm_g, "attn_k_norm_g": attn_k_norm_g,
            "attn_w_out": attn_w_out}


def _fwd_reference(x, conv_norm_g, conv_w_in, conv_w, conv_w_out, attn_norm_g, attn_w_in,
              attn_b_f, attn_q_norm_g, attn_k_norm_g, attn_w_out):
    for i in range(DEPTH):
        j = i // N_MIXERS
        if i % N_MIXERS == 0:
            x = _short_conv_layer(x, conv_norm_g[j], conv_w_in[j], conv_w[j], conv_w_out[j])
        else:
            x = _attn_layer(x, attn_norm_g[j], attn_w_in[j], attn_b_f[j],
                            attn_q_norm_g[j], attn_k_norm_g[j], attn_w_out[j])
    return x


import jax as _jax
import jax.numpy as _jnp

TWIN_FORMAT = 'train_step'
FWD_PARAMS = ['x', 'conv_norm_g', 'conv_w_in', 'conv_w', 'conv_w_out', 'attn_norm_g', 'attn_w_in', 'attn_b_f', 'attn_q_norm_g', 'attn_k_norm_g', 'attn_w_out']
TWIN_WEIGHTS = ['conv_norm_g', 'conv_w_in', 'conv_w', 'conv_w_out', 'attn_norm_g', 'attn_w_in', 'attn_b_f', 'attn_q_norm_g', 'attn_k_norm_g', 'attn_w_out']
TWIN_DIFF_INPUT = 'x'
TWIN_INPUTS = ['x', 'conv_norm_g', 'conv_w_in', 'conv_w', 'conv_w_out', 'attn_norm_g', 'attn_w_in', 'attn_b_f', 'attn_q_norm_g', 'attn_k_norm_g', 'attn_w_out', 'loss_target', 'm_conv_norm_g', 'm_conv_w_in', 'm_conv_w', 'm_conv_w_out', 'm_attn_norm_g', 'm_attn_w_in', 'm_attn_b_f', 'm_attn_q_norm_g', 'm_attn_k_norm_g', 'm_attn_w_out', 'v_conv_norm_g', 'v_conv_w_in', 'v_conv_w', 'v_conv_w_out', 'v_attn_norm_g', 'v_attn_w_in', 'v_attn_b_f', 'v_attn_q_norm_g', 'v_attn_k_norm_g', 'v_attn_w_out']
TWIN_OUTPUTS = ['loss', 'grad_x', 'grad_conv_norm_g', 'grad_conv_w_in', 'grad_conv_w', 'grad_conv_w_out', 'grad_attn_norm_g', 'grad_attn_w_in', 'grad_attn_b_f', 'grad_attn_q_norm_g', 'grad_attn_k_norm_g', 'grad_attn_w_out', 'delta_conv_norm_g', 'delta_conv_w_in', 'delta_conv_w', 'delta_conv_w_out', 'delta_attn_norm_g', 'delta_attn_w_in', 'delta_attn_b_f', 'delta_attn_q_norm_g', 'delta_attn_k_norm_g', 'delta_attn_w_out', 'new_m_conv_norm_g', 'new_m_conv_w_in', 'new_m_conv_w', 'new_m_conv_w_out', 'new_m_attn_norm_g', 'new_m_attn_w_in', 'new_m_attn_b_f', 'new_m_attn_q_norm_g', 'new_m_attn_k_norm_g', 'new_m_attn_w_out', 'new_v_conv_norm_g', 'new_v_conv_w_in', 'new_v_conv_w', 'new_v_conv_w_out', 'new_v_attn_norm_g', 'new_v_attn_w_in', 'new_v_attn_b_f', 'new_v_attn_q_norm_g', 'new_v_attn_k_norm_g', 'new_v_attn_w_out']
TWIN_LEAF_KINDS = {'loss': 'loss', 'grad_x': 'grad_x', 'grad_conv_norm_g': 'grad_w', 'grad_conv_w_in': 'grad_w', 'grad_conv_w': 'grad_w', 'grad_conv_w_out': 'grad_w', 'grad_attn_norm_g': 'grad_w', 'grad_attn_w_in': 'grad_w', 'grad_attn_b_f': 'grad_w', 'grad_attn_q_norm_g': 'grad_w', 'grad_attn_k_norm_g': 'grad_w', 'grad_attn_w_out': 'grad_w', 'delta_conv_norm_g': 'delta_w', 'delta_conv_w_in': 'delta_w', 'delta_conv_w': 'delta_w', 'delta_conv_w_out': 'delta_w', 'delta_attn_norm_g': 'delta_w', 'delta_attn_w_in': 'delta_w', 'delta_attn_b_f': 'delta_w', 'delta_attn_q_norm_g': 'delta_w', 'delta_attn_k_norm_g': 'delta_w', 'delta_attn_w_out': 'delta_w', 'new_m_conv_norm_g': 'new_m', 'new_m_conv_w_in': 'new_m', 'new_m_conv_w': 'new_m', 'new_m_conv_w_out': 'new_m', 'new_m_attn_norm_g': 'new_m', 'new_m_attn_w_in': 'new_m', 'new_m_attn_b_f': 'new_m', 'new_m_attn_q_norm_g': 'new_m', 'new_m_attn_k_norm_g': 'new_m', 'new_m_attn_w_out': 'new_m', 'new_v_conv_norm_g': 'new_v', 'new_v_conv_w_in': 'new_v', 'new_v_conv_w': 'new_v', 'new_v_conv_w_out': 'new_v', 'new_v_attn_norm_g': 'new_v', 'new_v_attn_w_in': 'new_v', 'new_v_attn_b_f': 'new_v', 'new_v_attn_q_norm_g': 'new_v', 'new_v_attn_k_norm_g': 'new_v', 'new_v_attn_w_out': 'new_v'}


def _forward(args):
    return _fwd_reference(*[args[k] for k in FWD_PARAMS])


def _output_shape():
    def fwd():
        inp = _fwd_setup_inputs(0)
        return _fwd_reference(*[inp[k] for k in FWD_PARAMS])
    out = _jax.eval_shape(fwd)
    return out.shape, out.dtype

N_MICROBATCH = 1
ADAM_LR = 0.001
ADAM_B1 = 0.9
ADAM_B2 = 0.999
ADAM_EPS = 1e-08
ADAM_WD = 0.01
ADAM_STEP = 10
PER_EXAMPLE_BATCH_AXIS = {'x': 0, 'loss_target': 0}
SHARED_INPUTS = []
_WEIGHT_DTYPES = {'conv_norm_g': _jnp.float32, 'conv_w_in': _jnp.float32, 'conv_w': _jnp.float32, 'conv_w_out': _jnp.float32, 'attn_norm_g': _jnp.float32, 'attn_w_in': _jnp.float32, 'attn_b_f': _jnp.float32, 'attn_q_norm_g': _jnp.float32, 'attn_k_norm_g': _jnp.float32, 'attn_w_out': _jnp.float32}
MOMENT_SCALE = {'conv_norm_g': 1.828672e+02, 'conv_w_in': 1.315134e+00, 'conv_w': 2.434347e+01, 'conv_w_out': 1.048463e+00, 'attn_norm_g': 1.423117e+01, 'attn_w_in': 1.820583e-01, 'attn_b_f': 1.328993e+02, 'attn_q_norm_g': 3.832679e+01, 'attn_k_norm_g': 3.825052e+01, 'attn_w_out': 1.648612e-01}


def _to_microbatches(a, axis):
    t = _jnp.moveaxis(a, axis, 0)
    t = t.reshape((N_MICROBATCH, t.shape[0] // N_MICROBATCH) + t.shape[1:])
    return _jnp.moveaxis(t, 1, axis + 1)


def setup_inputs(seed: int = 0) -> dict:
    inp = _fwd_setup_inputs(seed)
    key = _jax.random.fold_in(_jax.random.key(seed), 7919)
    shape, _ = _output_shape()
    out = dict(inp)
    out["loss_target"] = _jax.random.normal(_jax.random.fold_in(key, 0), shape, _jnp.float32)
    for i, name in enumerate(TWIN_WEIGHTS):
        w = inp[name].astype(_jnp.float32)
        if MOMENT_SCALE is None:
            s = _jnp.sqrt(_jnp.mean(_jnp.square(w)) + 1e-30)
        else:
            s = MOMENT_SCALE[name]
        km, kv = _jax.random.split(_jax.random.fold_in(key, i + 1))
        out[name] = w
        out["m_" + name] = s * _jax.random.normal(km, w.shape, _jnp.float32)
        out["v_" + name] = (s * s) * _jax.random.uniform(kv, w.shape, _jnp.float32, 0.5, 1.5)
    if N_MICROBATCH > 1:
        for name, axis in PER_EXAMPLE_BATCH_AXIS.items():
            out[name] = _to_microbatches(out[name], axis)
    return {'x': out['x'], 'conv_norm_g': out['conv_norm_g'], 'conv_w_in': out['conv_w_in'], 'conv_w': out['conv_w'], 'conv_w_out': out['conv_w_out'], 'attn_norm_g': out['attn_norm_g'], 'attn_w_in': out['attn_w_in'], 'attn_b_f': out['attn_b_f'], 'attn_q_norm_g': out['attn_q_norm_g'], 'attn_k_norm_g': out['attn_k_norm_g'], 'attn_w_out': out['attn_w_out'], 'loss_target': out['loss_target'], 'm_conv_norm_g': out['m_conv_norm_g'], 'm_conv_w_in': out['m_conv_w_in'], 'm_conv_w': out['m_conv_w'], 'm_conv_w_out': out['m_conv_w_out'], 'm_attn_norm_g': out['m_attn_norm_g'], 'm_attn_w_in': out['m_attn_w_in'], 'm_attn_b_f': out['m_attn_b_f'], 'm_attn_q_norm_g': out['m_attn_q_norm_g'], 'm_attn_k_norm_g': out['m_attn_k_norm_g'], 'm_attn_w_out': out['m_attn_w_out'], 'v_conv_norm_g': out['v_conv_norm_g'], 'v_conv_w_in': out['v_conv_w_in'], 'v_conv_w': out['v_conv_w'], 'v_conv_w_out': out['v_conv_w_out'], 'v_attn_norm_g': out['v_attn_norm_g'], 'v_attn_w_in': out['v_attn_w_in'], 'v_attn_b_f': out['v_attn_b_f'], 'v_attn_q_norm_g': out['v_attn_q_norm_g'], 'v_attn_k_norm_g': out['v_attn_k_norm_g'], 'v_attn_w_out': out['v_attn_w_out']}


def _loss(weights, diff, rest, loss_target):
    with _jax.named_scope("forward"):
        args = {**rest, TWIN_DIFF_INPUT: diff, **{k: w.astype(_WEIGHT_DTYPES[k]) for k, w in weights.items()}}
        y = _forward(args)
    with _jax.named_scope("loss_head"):
        err = _jnp.square(y.astype(_jnp.float32) - loss_target)
        return 0.5 * _jnp.sum(_jnp.mean(err, axis=-1)) if err.ndim else 0.5 * err


def _adamw(w, g, m, v):
    m = ADAM_B1 * m + (1.0 - ADAM_B1) * g
    v = ADAM_B2 * v + (1.0 - ADAM_B2) * _jnp.square(g)
    m_hat = m / (1.0 - ADAM_B1 ** ADAM_STEP)
    v_hat = v / (1.0 - ADAM_B2 ** ADAM_STEP)
    delta = -ADAM_LR * (m_hat / (_jnp.sqrt(v_hat) + ADAM_EPS) + ADAM_WD * w)
    return delta, m, v


def reference(x, conv_norm_g, conv_w_in, conv_w, conv_w_out, attn_norm_g, attn_w_in, attn_b_f, attn_q_norm_g, attn_k_norm_g, attn_w_out, loss_target, m_conv_norm_g, m_conv_w_in, m_conv_w, m_conv_w_out, m_attn_norm_g, m_attn_w_in, m_attn_b_f, m_attn_q_norm_g, m_attn_k_norm_g, m_attn_w_out, v_conv_norm_g, v_conv_w_in, v_conv_w, v_conv_w_out, v_attn_norm_g, v_attn_w_in, v_attn_b_f, v_attn_q_norm_g, v_attn_k_norm_g, v_attn_w_out):
    given = dict(x=x, conv_norm_g=conv_norm_g, conv_w_in=conv_w_in, conv_w=conv_w, conv_w_out=conv_w_out, attn_norm_g=attn_norm_g, attn_w_in=attn_w_in, attn_b_f=attn_b_f, attn_q_norm_g=attn_q_norm_g, attn_k_norm_g=attn_k_norm_g, attn_w_out=attn_w_out, loss_target=loss_target, m_conv_norm_g=m_conv_norm_g, m_conv_w_in=m_conv_w_in, m_conv_w=m_conv_w, m_conv_w_out=m_conv_w_out, m_attn_norm_g=m_attn_norm_g, m_attn_w_in=m_attn_w_in, m_attn_b_f=m_attn_b_f, m_attn_q_norm_g=m_attn_q_norm_g, m_attn_k_norm_g=m_attn_k_norm_g, m_attn_w_out=m_attn_w_out, v_conv_norm_g=v_conv_norm_g, v_conv_w_in=v_conv_w_in, v_conv_w=v_conv_w, v_conv_w_out=v_conv_w_out, v_attn_norm_g=v_attn_norm_g, v_attn_w_in=v_attn_w_in, v_attn_b_f=v_attn_b_f, v_attn_q_norm_g=v_attn_q_norm_g, v_attn_k_norm_g=v_attn_k_norm_g, v_attn_w_out=v_attn_w_out)
    weights = {n: given[n] for n in TWIN_WEIGHTS}
    shared = {n: given[n] for n in SHARED_INPUTS}
    per_example = {n: given[n] for n in ['x']}
    grad_fn = _jax.value_and_grad(_loss, argnums=(0, 1))

    def one_microbatch(ex, loss_target):
        ex = dict(ex)
        diff = ex.pop(TWIN_DIFF_INPUT)
        return grad_fn(weights, diff, {**shared, **ex}, loss_target)

    if N_MICROBATCH == 1:
        loss, (grad_w, grad_x) = one_microbatch(per_example, given["loss_target"])
    else:
        def body(carry, xs):
            loss_sum, grad_sum = carry
            l_k, (gw_k, gx_k) = one_microbatch(xs[0], xs[1])
            with _jax.named_scope("update"):
                return (loss_sum + l_k, _jax.tree.map(_jnp.add, grad_sum, gw_k)), gx_k

        init = (_jnp.zeros((), _jnp.float32), _jax.tree.map(_jnp.zeros_like, weights))
        (loss, grad_w), grad_x = _jax.lax.scan(body, init, (per_example, given["loss_target"]))
    with _jax.named_scope("update"):
        delta_w, new_m, new_v = {}, {}, {}
        for n in TWIN_WEIGHTS:
            delta_w[n], new_m[n], new_v[n] = _adamw(weights[n], grad_w[n], given["m_" + n], given["v_" + n])
    return (loss, grad_x, *[grad_w[n] for n in TWIN_WEIGHTS], *[delta_w[n] for n in TWIN_WEIGHTS],
            *[new_m[n] for n in TWIN_WEIGHTS], *[new_v[n] for n in TWIN_WEIGHTS])
```

```python
import functools

import jax
import jax.numpy as jnp
from jax import lax
from jax.experimental import pallas as pl
from jax.experimental.pallas import tpu as pltpu

F32 = jnp.float32
BF16 = jnp.bfloat16
HEAD_DIM = 64
LANES = 128
AUG_A = 64
AUG_B = 67
RMS_EPS = 1e-6
NEG = -1e30
Q_SCALE = 0.125
ROW_TILE = 256
TN_ROWS = 512
VMEM_LIMIT = 56 << 20
ADAM_LR, ADAM_B1, ADAM_B2, ADAM_EPS, ADAM_WD, ADAM_STEP = 0.001, 0.9, 0.999, 1e-08, 0.01, 10
MESH = pl.DeviceIdType.MESH
ANY = pl.BlockSpec(memory_space=pl.ANY)


def _lane():
    return lax.broadcasted_iota(jnp.int32, (1, LANES), 1)


def _split3(x):
    hi = x.astype(BF16).astype(F32)
    r = x - hi
    mid = r.astype(BF16).astype(F32)
    lo = (r - mid).astype(BF16).astype(F32)
    return hi, mid, lo


def _put(base, lane, start, parts):
    for j, p in enumerate(parts):
        base = jnp.where(lane == start + j, p, base)
    return base


def _col(x, lane, idx):
    return jnp.sum(jnp.where(lane == idx, x, 0.0), axis=1, keepdims=True)


def _head_tile(ref, hd, lane):
    j = hd // 2
    t = ref[:, LANES * j:LANES * (j + 1)]
    if hd % 2:
        t = pltpu.roll(t, HEAD_DIM, 1)
    return jnp.where(lane < HEAD_DIM, t, 0.0)


def _pair_tile(even, odd):
    return even + pltpu.roll(odd, HEAD_DIM, 1)


def _sigmoid(x):
    return 1.0 / (1.0 + jnp.exp(-x))


def _dot(a, b):
    return jnp.dot(a, b, preferred_element_type=F32)


def _dot_nt(a, b):
    return lax.dot_general(a, b, (((1,), (1,)), ((), ())), preferred_element_type=F32)


def _dot_tn(a, b):
    return lax.dot_general(a, b, (((0,), (0,)), ((), ())), preferred_element_type=F32)


def _dot01(tri, x):
    hi, mid, lo = _split3(x)
    return _dot(tri, hi.astype(BF16)) + _dot(tri, mid.astype(BF16)) + _dot(tri, lo.astype(BF16))


def _rms_bwd(dh, x, g):
    inv = lax.rsqrt(jnp.mean(x * x, axis=-1, keepdims=True) + RMS_EPS)
    xh = x * inv
    dxn = dh * g
    dx = inv * (dxn - xh * jnp.mean(dxn * xh, axis=-1, keepdims=True))
    return dx, jnp.sum(dh * xh, axis=0, keepdims=True)


def _head_rms_bwd(dn, t, g):
    inv = lax.rsqrt(jnp.sum(t * t, axis=1, keepdims=True) * (1.0 / HEAD_DIM) + RMS_EPS)
    th = t * inv
    gd = dn * g
    d = inv * (gd - th * (jnp.sum(gd * th, axis=1, keepdims=True) * (1.0 / HEAD_DIM)))
    return d, jnp.sum(dn * th, axis=0, keepdims=True)


def _params(n_grid):
    return pltpu.CompilerParams(dimension_semantics=("arbitrary",) * n_grid, vmem_limit_bytes=VMEM_LIMIT)


def _rows(tm, cols, rev=None):
    if rev is None:
        return pl.BlockSpec((tm, cols), lambda i: (i, 0))
    return pl.BlockSpec((tm, cols), lambda i: (rev - i, 0))


def _whole(shape):
    return pl.BlockSpec(shape, lambda *_: (0,) * len(shape))


def _conv_fwd(x, g1, w_in, conv_w, w_out):
    S, D = x.shape
    tm = min(ROW_TILE, S)

    def body(x_ref, g_ref, win_ref, cw_ref, wout_ref, proj_ref, h_ref, yc_ref, y_ref, x1_ref, prev_u):
        i = pl.program_id(0)
        xv = x_ref[...]
        inv = lax.rsqrt(jnp.mean(xv * xv, axis=-1, keepdims=True) + RMS_EPS)
        h = (xv * inv * g_ref[...]).astype(BF16)
        h_ref[...] = h
        for j in range(4):
            proj_ref[:, j * D:(j + 1) * D] = _dot(h, win_ref[:, j * D:(j + 1) * D])

        @pl.when(i == 0)
        def _():
            prev_u[...] = jnp.zeros((tm, D), F32)

        u = proj_ref[:, D:2 * D] * proj_ref[:, 2 * D:3 * D]
        pu = prev_u[...]
        row = lax.broadcasted_iota(jnp.int32, (tm, 1), 0)
        u1 = jnp.where(row < 1, pltpu.roll(pu, 1, 0), pltpu.roll(u, 1, 0))
        u2 = jnp.where(row < 2, pltpu.roll(pu, 2, 0), pltpu.roll(u, 2, 0))
        prev_u[...] = u
        w = cw_ref[...]
        yc = w[2:3] * u + w[1:2] * u1 + w[0:1] * u2
        yc_ref[...] = yc
        z = proj_ref[:, 3 * D:4 * D]
        y = (proj_ref[:, 0:D] * yc * (z * _sigmoid(z))).astype(BF16)
        y_ref[...] = y
        x1_ref[...] = xv + _dot(y, wout_ref[...])

    return pl.pallas_call(
        body, name="conv_fwd", grid=(S // tm,),
        in_specs=[_rows(tm, D), _whole((1, D)), _whole((D, 4 * D)), _whole((3, D)), _whole((D, D))],
        out_specs=[_rows(tm, 4 * D), _rows(tm, D), _rows(tm, D), _rows(tm, D), _rows(tm, D)],
        out_shape=[jax.ShapeDtypeStruct((S, 4 * D), F32), jax.ShapeDtypeStruct((S, D), BF16),
                   jax.ShapeDtypeStruct((S, D), F32), jax.ShapeDtypeStruct((S, D), BF16),
                   jax.ShapeDtypeStruct((S, D), F32)],
        scratch_shapes=[pltpu.VMEM((tm, D), F32)],
        compiler_params=_params(1),
    )(x, g1, w_in, conv_w, w_out)


def _conv_bwd(dx1, x, g1, w_in, w_out, conv_w, proj, yc):
    S, D = x.shape
    tm = min(ROW_TILE, S)
    last = S // tm - 1

    def body(dx1_ref, x_ref, g_ref, win_ref, wout_ref, cw_ref, proj_ref, yc_ref,
             dproj_ref, dx_ref, dx1b_ref, dg_ref, dcw_ref, next_d):
        @pl.when(pl.program_id(0) == 0)
        def _():
            dg_ref[...] = jnp.zeros((1, D), F32)
            dcw_ref[...] = jnp.zeros((3, D), F32)
            next_d[...] = jnp.zeros((tm, D), F32)

        dx1v = dx1_ref[...]
        dx1b = dx1v.astype(BF16)
        dx1b_ref[...] = dx1b
        dy = _dot_nt(dx1b, wout_ref[...])
        b = proj_ref[:, 0:D]
        c = proj_ref[:, D:2 * D]
        xin = proj_ref[:, 2 * D:3 * D]
        z = proj_ref[:, 3 * D:4 * D]
        sg = _sigmoid(z)
        sz = z * sg
        ycv = yc_ref[...]
        d0 = dy * b * sz
        dproj_ref[:, 0:D] = (dy * ycv * sz).astype(BF16)
        dproj_ref[:, 3 * D:4 * D] = (dy * b * ycv * (sg * (1.0 + z * (1.0 - sg)))).astype(BF16)
        nd = next_d[...]
        row = lax.broadcasted_iota(jnp.int32, (tm, 1), 0)
        d1 = jnp.where(row >= tm - 1, pltpu.roll(nd, tm - 1, 0), pltpu.roll(d0, tm - 1, 0))
        d2 = jnp.where(row >= tm - 2, pltpu.roll(nd, tm - 2, 0), pltpu.roll(d0, tm - 2, 0))
        next_d[...] = d0
        w = cw_ref[...]
        du = w[2:3] * d0 + w[1:2] * d1 + w[0:1] * d2
        u = c * xin
        dcw_ref[2:3, :] += jnp.sum(d0 * u, axis=0, keepdims=True)
        dcw_ref[1:2, :] += jnp.sum(d1 * u, axis=0, keepdims=True)
        dcw_ref[0:1, :] += jnp.sum(d2 * u, axis=0, keepdims=True)
        dproj_ref[:, D:2 * D] = (du * xin).astype(BF16)
        dproj_ref[:, 2 * D:3 * D] = (du * c).astype(BF16)
        dh = _dot_nt(dproj_ref[...], win_ref[...])
        dxn, dg = _rms_bwd(dh, x_ref[...], g_ref[...])
        dx_ref[...] = dx1v + dxn
        dg_ref[...] += dg

    return pl.pallas_call(
        body, name="conv_bwd", grid=(S // tm,),
        in_specs=[_rows(tm, D, last), _rows(tm, D, last), _whole((1, D)), _whole((D, 4 * D)), _whole((D, D)),
                  _whole((3, D)), _rows(tm, 4 * D, last), _rows(tm, D, last)],
        out_specs=[_rows(tm, 4 * D, last), _rows(tm, D, last), _rows(tm, D, last), _whole((1, D)), _whole((3, D))],
        out_shape=[jax.ShapeDtypeStruct((S, 4 * D), BF16), jax.ShapeDtypeStruct((S, D), F32),
                   jax.ShapeDtypeStruct((S, D), BF16), jax.ShapeDtypeStruct((1, D), F32),
                   jax.ShapeDtypeStruct((3, D), F32)],
        scratch_shapes=[pltpu.VMEM((tm, D), F32)],
        compiler_params=_params(1),
    )(dx1, x, g1, w_in, w_out, conv_w, proj, yc)


def _attn_front(x1, g2, w, wf, bf, gq, gk):
    S, D = x1.shape
    H = D // HEAD_DIM
    tm = min(ROW_TILE, S)
    tri = (lax.broadcasted_iota(jnp.int32, (tm, tm), 1) <= lax.broadcasted_iota(jnp.int32, (tm, tm), 0)).astype(BF16)

    def body(x_ref, g_ref, w_ref, wf_ref, bf_ref, gq_ref, gk_ref, tri_ref,
             h_ref, qraw_ref, kraw_ref, z_ref, f_ref, c_ref, qa_ref, ka_ref, va_ref, vt_ref, carry, v_s):
        @pl.when(pl.program_id(0) == 0)
        def _():
            carry[...] = jnp.zeros((8, LANES), F32)

        xv = x_ref[...]
        inv = lax.rsqrt(jnp.mean(xv * xv, axis=-1, keepdims=True) + RMS_EPS)
        h = (xv * inv * g_ref[...]).astype(BF16)
        h_ref[...] = h
        qraw_ref[...] = _dot(h, w_ref[:, 0:D])
        kraw_ref[...] = _dot(h, w_ref[:, D:2 * D])
        v_s[...] = _dot(h, w_ref[:, 2 * D:3 * D])
        z_ref[...] = _dot(h, w_ref[:, 3 * D:4 * D])
        lane = _lane()
        f = _dot(h, wf_ref[...]) + bf_ref[...]
        f_ref[...] = f
        logf = jnp.where(lane < H, jnp.minimum(f, 0.0) - jnp.log(1.0 + jnp.exp(-jnp.abs(f))), 0.0)
        cs = _dot01(tri_ref[...], logf) + carry[0:1, :]
        c_ref[...] = cs
        carry[...] = jnp.broadcast_to(cs[tm - 1:tm, :], (8, LANES))
        one_a = (lane >= AUG_A) & (lane < AUG_A + 3)
        one_b = (lane >= AUG_B) & (lane < AUG_B + 3)
        for hd in range(H):
            sl = slice(LANES * hd, LANES * (hd + 1))
            ch = _col(cs, lane, hd)
            qt = _head_tile(qraw_ref, hd, lane)
            qn = qt * lax.rsqrt(jnp.sum(qt * qt, axis=1, keepdims=True) * (1.0 / HEAD_DIM) + RMS_EPS) * gq_ref[...]
            qa = _put(jnp.where(one_b, 1.0, qn * Q_SCALE), lane, AUG_A, _split3(ch))
            qa_ref[:, sl] = qa.astype(BF16)
            kt = _head_tile(kraw_ref, hd, lane)
            kn = kt * lax.rsqrt(jnp.sum(kt * kt, axis=1, keepdims=True) * (1.0 / HEAD_DIM) + RMS_EPS) * gk_ref[...]
            ka = _put(jnp.where(one_a, 1.0, kn), lane, AUG_B, _split3(-ch))
            ka_ref[:, sl] = ka.astype(BF16)
            va = jnp.where(one_a, 1.0, _head_tile(v_s, hd, lane))
            va_ref[:, sl] = va.astype(BF16)
            vt_ref[hd] = va.T.astype(BF16)

    nb = S // tm
    return pl.pallas_call(
        body, name="attn_front", grid=(nb,),
        in_specs=[_rows(tm, D), _whole((1, D)), _whole((D, 4 * D)), _whole((D, LANES)), _whole((1, LANES)),
                  _whole((1, LANES)), _whole((1, LANES)), _whole((tm, tm))],
        out_specs=[_rows(tm, D), _rows(tm, D), _rows(tm, D), _rows(tm, D), _rows(tm, LANES), _rows(tm, LANES),
                   _rows(tm, H * LANES), _rows(tm, H * LANES), _rows(tm, H * LANES),
                   pl.BlockSpec((H, None, LANES, tm), lambda i: (0, i, 0, 0))],
        out_shape=[jax.ShapeDtypeStruct((S, D), BF16), jax.ShapeDtypeStruct((S, D), F32),
                   jax.ShapeDtypeStruct((S, D), F32), jax.ShapeDtypeStruct((S, D), F32),
                   jax.ShapeDtypeStruct((S, LANES), F32), jax.ShapeDtypeStruct((S, LANES), F32),
                   jax.ShapeDtypeStruct((S, H * LANES), BF16), jax.ShapeDtypeStruct((S, H * LANES), BF16),
                   jax.ShapeDtypeStruct((S, H * LANES), BF16), jax.ShapeDtypeStruct((H, nb, LANES, tm), BF16)],
        scratch_shapes=[pltpu.VMEM((8, LANES), F32), pltpu.VMEM((tm, D), F32)],
        compiler_params=_params(1),
    )(x1, g2, w, wf, bf, gq, gk, tri)


def _attn_fwd(qa, ka, vt):
    S = qa.shape[0]
    H = qa.shape[1] // LANES
    nb, T = vt.shape[1], vt.shape[3]

    def body(q_ref, k_ref, vt_ref, o_ref, m_ref, acc_ref):
        qi = pl.program_id(1)
        q = q_ref[...]
        m_ref[...] = jnp.full((8, T), NEG, F32)
        acc_ref[...] = jnp.zeros((LANES, T), F32)
        causal = lax.broadcasted_iota(jnp.int32, (T, T), 0) <= lax.broadcasted_iota(jnp.int32, (T, T), 1)

        def step(ki, masked):
            kb = k_ref[pl.ds(pl.multiple_of(ki * T, T), T), :]
            st = _dot_nt(kb, q)
            if masked:
                st = jnp.where(causal, st, NEG)
            m_old = m_ref[0:1, :]
            m_new = jnp.maximum(m_old, jnp.max(st, axis=0, keepdims=True))
            p = jnp.exp(st - m_new).astype(BF16)
            acc_ref[...] = acc_ref[...] * jnp.exp(m_old - m_new) + _dot(vt_ref[ki], p)
            m_ref[...] = jnp.broadcast_to(m_new, (8, T))

        def loop(ki, carry):
            step(ki, False)
            return carry

        lax.fori_loop(0, qi, loop, 0)
        step(qi, True)
        acc = acc_ref[...]
        l = acc[AUG_A:AUG_A + 1, :]
        lse = m_ref[0:1, :] + jnp.log(l)
        feat = lax.broadcasted_iota(jnp.int32, (LANES, 1), 0)
        o_ref[...] = jnp.where(feat == AUG_A + 1, lse, acc / l).T

    return pl.pallas_call(
        body, name="attn_fwd", grid=(H, nb),
        in_specs=[pl.BlockSpec((T, LANES), lambda h, i: (i, h)), pl.BlockSpec((S, LANES), lambda h, i: (0, h)),
                  pl.BlockSpec((None, nb, LANES, T), lambda h, i: (h, 0, 0, 0))],
        out_specs=pl.BlockSpec((T, LANES), lambda h, i: (i, h)),
        out_shape=jax.ShapeDtypeStruct((S, H * LANES), F32),
        scratch_shapes=[pltpu.VMEM((8, T), F32), pltpu.VMEM((LANES, T), F32)],
        compiler_params=_params(2),
    )(qa, ka, vt)


def _attn_out(o_aug, z, x1, target, w_out, qa, c):
    S, D = x1.shape
    H = D // HEAD_DIM
    tm = min(ROW_TILE, S)

    def body(o_ref, z_ref, x1_ref, t_ref, w_ref, q_ref, c_ref,
             dx2_ref, dx2b_ref, o2b_ref, dz_ref, doa_ref, qa2_ref, loss_ref, oc_s, do_s):
        @pl.when(pl.program_id(0) == 0)
        def _():
            loss_ref[...] = jnp.zeros((1, LANES), F32)

        lane = _lane()
        lses = []
        for j in range(H // 2):
            oe = o_ref[:, 2 * LANES * j:2 * LANES * j + LANES]
            oo = o_ref[:, 2 * LANES * j + LANES:2 * LANES * (j + 1)]
            lses += [_col(oe, lane, AUG_A + 1), _col(oo, lane, AUG_A + 1)]
            oc_s[:, LANES * j:LANES * (j + 1)] = _pair_tile(jnp.where(lane < HEAD_DIM, oe, 0.0),
                                                            jnp.where(lane < HEAD_DIM, oo, 0.0))
        oc = oc_s[...]
        zv = z_ref[...]
        sg = _sigmoid(zv)
        sz = zv * sg
        o2 = (oc * sz).astype(BF16)
        o2b_ref[...] = o2
        e = x1_ref[...] + _dot(o2, w_ref[...]) - t_ref[...]
        sq = jnp.sum(jnp.sum(e * e, axis=1, keepdims=True), axis=0, keepdims=True)
        loss_ref[...] += jnp.broadcast_to(sq * (0.5 / D), (1, LANES))
        dx2 = e * (1.0 / D)
        dx2_ref[...] = dx2
        dx2b = dx2.astype(BF16)
        dx2b_ref[...] = dx2b
        do2 = _dot_nt(dx2b, w_ref[...])
        dz_ref[...] = do2 * oc * (sg * (1.0 + zv * (1.0 - sg)))
        do_s[...] = do2 * sz
        cs = c_ref[...]
        for hd in range(H):
            sl = slice(LANES * hd, LANES * (hd + 1))
            dt = _head_tile(do_s, hd, lane)
            delta = jnp.sum(dt * _head_tile(oc_s, hd, lane), axis=1, keepdims=True)
            doa_ref[:, sl] = _put(dt, lane, AUG_A, _split3(-delta)).astype(BF16)
            rq = _col(cs, lane, hd) - lses[hd]
            qa2_ref[:, sl] = _put(q_ref[:, sl].astype(F32), lane, AUG_A, _split3(rq)).astype(BF16)

    return pl.pallas_call(
        body, name="attn_out", grid=(S // tm,),
        in_specs=[_rows(tm, H * LANES), _rows(tm, D), _rows(tm, D), _rows(tm, D), _whole((D, D)),
                  _rows(tm, H * LANES), _rows(tm, LANES)],
        out_specs=[_rows(tm, D), _rows(tm, D), _rows(tm, D), _rows(tm, D), _rows(tm, H * LANES),
                   _rows(tm, H * LANES), _whole((1, LANES))],
        out_shape=[jax.ShapeDtypeStruct((S, D), F32), jax.ShapeDtypeStruct((S, D), BF16),
                   jax.ShapeDtypeStruct((S, D), BF16), jax.ShapeDtypeStruct((S, D), F32),
                   jax.ShapeDtypeStruct((S, H * LANES), BF16), jax.ShapeDtypeStruct((S, H * LANES), BF16),
                   jax.ShapeDtypeStruct((1, LANES), F32)],
        scratch_shapes=[pltpu.VMEM((tm, D), F32), pltpu.VMEM((tm, D), F32)],
        compiler_params=_params(1),
    )(o_aug, z, x1, target, w_out, qa, c)


def _attn_bwd(qa2, doa, ka, va, T):
    S = qa2.shape[0]
    H = qa2.shape[1] // LANES
    nb = S // T

    def body(q_ref, do_ref, k_ref, v_ref, dq_ref, dk_ref, dv_ref, dk_acc, dv_acc):
        ki = pl.program_id(1)

        @pl.when(ki == 0)
        def _():
            dq_ref[...] = jnp.zeros((nb, LANES, T), F32)

        kb = k_ref[...]
        vb = v_ref[...]
        kt = kb.astype(F32).T.astype(BF16)
        dk_acc[...] = jnp.zeros((T, LANES), F32)
        dv_acc[...] = jnp.zeros((T, LANES), F32)
        causal = lax.broadcasted_iota(jnp.int32, (T, T), 0) <= lax.broadcasted_iota(jnp.int32, (T, T), 1)

        def step(qi, masked):
            q0 = pl.multiple_of(qi * T, T)
            qb = q_ref[pl.ds(q0, T), :]
            dob = do_ref[pl.ds(q0, T), :]
            st = _dot_nt(kb, qb)
            if masked:
                st = jnp.where(causal, st, NEG)
            p = jnp.exp(st)
            ds = (p * _dot_nt(vb, dob)).astype(BF16)
            dv_acc[...] += _dot(p.astype(BF16), dob)
            dk_acc[...] += _dot(ds, qb)
            dq_ref[qi] += _dot(kt, ds)

        def loop(qi, carry):
            step(qi, False)
            return carry

        step(ki, True)
        lax.fori_loop(ki + 1, nb, loop, 0)
        dk_ref[...] = dk_acc[...]
        dv_ref[...] = dv_acc[...]

    return pl.pallas_call(
        body, name="attn_bwd", grid=(H, nb),
        in_specs=[pl.BlockSpec((S, LANES), lambda h, i: (0, h)), pl.BlockSpec((S, LANES), lambda h, i: (0, h)),
                  pl.BlockSpec((T, LANES), lambda h, i: (i, h)), pl.BlockSpec((T, LANES), lambda h, i: (i, h))],
        out_specs=[pl.BlockSpec((None, nb, LANES, T), lambda h, i: (h, 0, 0, 0)),
                   pl.BlockSpec((T, LANES), lambda h, i: (i, h)), pl.BlockSpec((T, LANES), lambda h, i: (i, h))],
        out_shape=[jax.ShapeDtypeStruct((H, nb, LANES, T), F32), jax.ShapeDtypeStruct((S, H * LANES), F32),
                   jax.ShapeDtypeStruct((S, H * LANES), F32)],
        scratch_shapes=[pltpu.VMEM((T, LANES), F32), pltpu.VMEM((T, LANES), F32)],
        compiler_params=_params(2),
    )(qa2, doa, ka, va)


def _attn_proj_bwd(dqt, dka, dva, qraw, kraw, dz, f, gq, gk, w, wf, x1, g2, dx2):
    S, D = x1.shape
    H = D // HEAD_DIM
    tm = dqt.shape[3]
    last = S // tm - 1
    tri = (lax.broadcasted_iota(jnp.int32, (tm, tm), 1) >= lax.broadcasted_iota(jnp.int32, (tm, tm), 0)).astype(BF16)

    def body(dqt_ref, dk_ref, dv_ref, q_ref, k_ref, dz_ref, f_ref, gq_ref, gk_ref, w_ref, wf_ref, x1_ref, g2_ref,
             dx2_ref, tri_ref, dproj_ref, dx1_ref, dg2_ref, small_ref, carry):
        @pl.when(pl.program_id(0) == 0)
        def _():
            dg2_ref[...] = jnp.zeros((1, D), F32)
            small_ref[...] = jnp.zeros((8, LANES), F32)
            carry[...] = jnp.zeros((8, LANES), F32)

        lane = _lane()
        dcs = jnp.zeros((tm, LANES), F32)
        dgq = jnp.zeros((1, LANES), F32)
        dgk = jnp.zeros((1, LANES), F32)
        for j in range(H // 2):
            dq2, dk2, dv2 = [], [], []
            for hd in (2 * j, 2 * j + 1):
                sl = slice(LANES * hd, LANES * (hd + 1))
                dqf = dqt_ref[hd].T
                dqn = jnp.where(lane < HEAD_DIM, dqf * Q_SCALE, 0.0)
                d, dg = _head_rms_bwd(dqn, _head_tile(q_ref, hd, lane), gq_ref[...])
                dq2.append(d)
                dgq = dgq + dg
                dkt = dk_ref[:, sl]
                dcs = dcs + jnp.where(lane == hd, _col(dqf, lane, AUG_A) - _col(dkt, lane, AUG_B), 0.0)
                d, dg = _head_rms_bwd(jnp.where(lane < HEAD_DIM, dkt, 0.0), _head_tile(k_ref, hd, lane), gk_ref[...])
                dk2.append(d)
                dgk = dgk + dg
                dv2.append(jnp.where(lane < HEAD_DIM, dv_ref[:, sl], 0.0))
            for part, pair in enumerate((dq2, dk2, dv2)):
                dproj_ref[:, part * D + LANES * j:part * D + LANES * (j + 1)] = _pair_tile(*pair).astype(BF16)
        dproj_ref[:, 3 * D:4 * D] = dz_ref[...].astype(BF16)
        dlogf = _dot01(tri_ref[...], dcs) + carry[0:1, :]
        carry[...] = jnp.broadcast_to(dlogf[0:1, :], (8, LANES))
        df = dlogf * (1.0 / (1.0 + jnp.exp(f_ref[...])))
        dproj_ref[:, 4 * D:4 * D + LANES] = df.astype(BF16)
        small_ref[0:1, :] += jnp.sum(df, axis=0, keepdims=True)
        small_ref[1:2, :] += dgq
        small_ref[2:3, :] += dgk
        dh = _dot_nt(dproj_ref[:, 0:4 * D], w_ref[...]) + _dot_nt(dproj_ref[:, 4 * D:4 * D + LANES], wf_ref[...])
        dxn, dg = _rms_bwd(dh, x1_ref[...], g2_ref[...])
        dx1_ref[...] = dx2_ref[...] + dxn
        dg2_ref[...] += dg

    W = 4 * D + LANES
    return pl.pallas_call(
        body, name="attn_proj_bwd", grid=(S // tm,),
        in_specs=[pl.BlockSpec((H, None, LANES, tm), lambda i: (0, last - i, 0, 0)),
                  _rows(tm, H * LANES, last), _rows(tm, H * LANES, last), _rows(tm, D, last), _rows(tm, D, last),
                  _rows(tm, D, last), _rows(tm, LANES, last), _whole((1, LANES)), _whole((1, LANES)),
                  _whole((D, 4 * D)), _whole((D, LANES)), _rows(tm, D, last), _whole((1, D)), _rows(tm, D, last),
                  _whole((tm, tm))],
        out_specs=[_rows(tm, W, last), _rows(tm, D, last), _whole((1, D)), _whole((8, LANES))],
        out_shape=[jax.ShapeDtypeStruct((S, W), BF16), jax.ShapeDtypeStruct((S, D), F32),
                   jax.ShapeDtypeStruct((1, D), F32), jax.ShapeDtypeStruct((8, LANES), F32)],
        scratch_shapes=[pltpu.VMEM((8, LANES), F32)],
        compiler_params=_params(1),
    )(dqt, dka, dva, qraw, kraw, dz, f, gq, gk, w, wf, x1, g2, dx2, tri)


def _matmul_tn(a, b, col0, n, tn, name):
    S, M = a.shape
    ts = min(TN_ROWS, S)
    off = col0 // tn

    def body(a_ref, b_ref, o_ref):
        @pl.when(pl.program_id(1) == 0)
        def _():
            o_ref[...] = jnp.zeros((M, tn), F32)

        o_ref[...] += _dot_tn(a_ref[...], b_ref[...])

    return pl.pallas_call(
        body, name=name, grid=(n // tn, S // ts),
        in_specs=[pl.BlockSpec((ts, M), lambda j, s: (s, 0)), pl.BlockSpec((ts, tn), lambda j, s: (s, off + j))],
        out_specs=pl.BlockSpec((M, tn), lambda j, s: (0, j)),
        out_shape=jax.ShapeDtypeStruct((M, n), F32),
        compiler_params=_params(2),
    )(a, b)


def _adamw(w, g, m, v, name):
    r, c = w.shape
    tr = ROW_TILE if r % ROW_TILE == 0 else r

    def body(w_ref, g_ref, m_ref, v_ref, d_ref, m2_ref, v2_ref):
        gv = g_ref[...]
        m2 = ADAM_B1 * m_ref[...] + (1.0 - ADAM_B1) * gv
        v2 = ADAM_B2 * v_ref[...] + (1.0 - ADAM_B2) * (gv * gv)
        m2_ref[...] = m2
        v2_ref[...] = v2
        m_hat = m2 / (1.0 - ADAM_B1 ** ADAM_STEP)
        v_hat = v2 / (1.0 - ADAM_B2 ** ADAM_STEP)
        d_ref[...] = -ADAM_LR * (m_hat / (jnp.sqrt(v_hat) + ADAM_EPS) + ADAM_WD * w_ref[...])

    spec = _rows(tr, c)
    return pl.pallas_call(
        body, name=name, grid=(r // tr,), in_specs=[spec] * 4, out_specs=[spec] * 3,
        out_shape=[jax.ShapeDtypeStruct((r, c), F32)] * 3, compiler_params=_params(1),
    )(w, g, m, v)


def _local_step(x, target, g1, w_in, conv_w, w_out, g2, wa_in, b_f, gq, gk, wa_out):
    S, D = x.shape
    H = D // HEAD_DIM
    w_qkvz = wa_in[:, :4 * D]
    wf = jnp.pad(wa_in[:, 4 * D:], ((0, 0), (0, LANES - H)))
    bf = jnp.pad(b_f, ((0, 0), (0, LANES - H)))
    gq128 = jnp.pad(gq, ((0, 0), (0, LANES - HEAD_DIM)))
    gk128 = jnp.pad(gk, ((0, 0), (0, LANES - HEAD_DIM)))

    proj, h1, yc, y, x1 = _conv_fwd(x, g1, w_in, conv_w, w_out)
    h2, qraw, kraw, z, f, c, qa, ka, va, vt = _attn_front(x1, g2, w_qkvz, wf, bf, gq128, gk128)
    o_aug = _attn_fwd(qa, ka, vt)
    dx2, dx2b, o2b, dz, doa, qa2, loss = _attn_out(o_aug, z, x1, target, wa_out, qa, c)
    dqt, dka, dva = _attn_bwd(qa2, doa, ka, va, vt.shape[3])
    dproj2, dx1, dg2, small = _attn_proj_bwd(dqt, dka, dva, qraw, kraw, dz, f, gq128, gk128, w_qkvz, wf, x1, g2, dx2)
    dproj1, dx, dx1b, dg1, dcw = _conv_bwd(dx1, x, g1, w_in, w_out, conv_w, proj, yc)

    tn = min(1024, D)
    dwa_out = _matmul_tn(o2b, dx2b, 0, D, tn, "dw_attn_out")
    dwa_in = jnp.concatenate([_matmul_tn(h2, dproj2, 0, 4 * D, tn, "dw_attn_in"),
                              _matmul_tn(h2, dproj2, 4 * D, LANES, LANES, "dw_attn_f")[:, :H]], axis=1)
    dw_out = _matmul_tn(y, dx1b, 0, D, tn, "dw_conv_out")
    dw_in = _matmul_tn(h1, dproj1, 0, 4 * D, tn, "dw_conv_in")
    grads = dict(conv_norm_g=dg1, conv_w_in=dw_in, conv_w=dcw, conv_w_out=dw_out, attn_norm_g=dg2,
                 attn_w_in=dwa_in, attn_b_f=small[0:1, :H], attn_q_norm_g=small[1:2, :HEAD_DIM],
                 attn_k_norm_g=small[2:3, :HEAD_DIM], attn_w_out=dwa_out)
    return loss[0, 0], dx, grads


def _coords():
    return lax.axis_index("x"), lax.axis_index("y"), lax.axis_index("c")


def _at(ref, idx):
    return ref.at[idx] if idx else ref


def _other_chips(x, y):
    return [(1 - x, y), (x, 1 - y), (1 - x, 1 - y)]


def _all_gather(halved, whole):
    nh, nw = len(halved), len(whole)

    def body(*refs):
        src_h, src_w = refs[:nh], refs[nh:nh + nw]
        out_h, out_w = refs[nh + nw:2 * nh + nw], refs[2 * nh + nw:2 * (nh + nw)]
        send_h, recv_h, send_w, recv_w, local_sem = refs[2 * (nh + nw):]
        x, y, c = _coords()
        mine = 2 * x + y
        sibling = (x, y, 1 - c)
        chips = _other_chips(x, y)

        def copy_h(a, k, chip, half, to, src=None):
            dst = out_h[a].at[chip, half]
            return pltpu.make_async_remote_copy(src_ref=dst if src is None else src, dst_ref=dst,
                                                send_sem=send_h.at[a, k], recv_sem=recv_h.at[a, k],
                                                device_id=to, device_id_type=MESH)

        def copy_w(a, k, chip, to):
            return pltpu.make_async_remote_copy(src_ref=src_w[a], dst_ref=out_w[a].at[chip],
                                                send_sem=send_w.at[a, k], recv_sem=recv_w.at[a, k],
                                                device_id=to, device_id_type=MESH)

        local = [pltpu.make_async_copy(src_h[a], out_h[a].at[mine], local_sem.at[a]) for a in range(nh)]
        local += [pltpu.make_async_copy(src_w[a], out_w[a].at[mine], local_sem.at[nh + a]) for a in range(nw)]
        for cp in local:
            cp.start()
        first = [copy_h(a, k, mine, c, (*chip, c), src=src_h[a].at[c]) for a in range(nh) for k, chip in enumerate(chips)]
        first += [copy_w(a, k, mine, (*chip, c)) for a in range(nw) for k, chip in enumerate(chips)]
        for cp in first:
            cp.start()
        passed = []
        for a in range(nh):
            for k, (px, py) in enumerate(chips):
                copy_h(a, k, 2 * px + py, c, (x, y, c)).wait_recv()
                cp = copy_h(a, 3 + k, 2 * px + py, c, sibling)
                cp.start()
                passed.append(cp)
        for a in range(nh):
            for k, (px, py) in enumerate(chips):
                copy_h(a, 3 + k, 2 * px + py, 1 - c, (x, y, c)).wait_recv()
        for a in range(nw):
            for k, (px, py) in enumerate(chips):
                copy_w(a, k, 2 * px + py, (x, y, c)).wait_recv()
        for cp in first + passed:
            cp.wait_send()
        for cp in local:
            cp.wait()

    out_shape = [jax.ShapeDtypeStruct((4,) + a.shape, a.dtype) for a in list(halved) + list(whole)]
    return pl.pallas_call(
        body, name="gather_weights", in_specs=[ANY] * (nh + nw), out_specs=[ANY] * (nh + nw), out_shape=out_shape,
        scratch_shapes=[pltpu.SemaphoreType.DMA((nh, 6)), pltpu.SemaphoreType.DMA((nh, 6)),
                        pltpu.SemaphoreType.DMA((nw, 3)), pltpu.SemaphoreType.DMA((nw, 3)),
                        pltpu.SemaphoreType.DMA((nh + nw,))],
    )(*halved, *whole)


def _exchange(name, srcs, lands, copies, local_copies):
    ns, nl, n, nloc = len(srcs), len(lands), len(copies), len(local_copies)

    def body(*refs):
        src, land = refs[:ns], refs[ns:ns + nl]
        send, recv, local_sem = refs[ns + nl:]
        me = _coords()
        started = []
        for k, (si, s_at, li, l_at, ci) in enumerate(local_copies):
            cp = pltpu.make_async_copy(_at(src[si], s_at(*me)), _at(land[li], l_at(*me)), local_sem.at[k])
            cp.start()
            started.append(cp)
        remote = []
        for k, (si, s_at, li, l_at, peer) in enumerate(copies):
            cp = pltpu.make_async_remote_copy(src_ref=_at(src[si], s_at(*me)), dst_ref=_at(land[li], l_at(*me)),
                                              send_sem=send.at[k], recv_sem=recv.at[k],
                                              device_id=peer(*me), device_id_type=MESH)
            cp.start()
            remote.append(cp)
        for cp in remote:
            cp.wait()
        for cp in started:
            cp.wait()

    return pl.pallas_call(
        body, name=name, in_specs=[ANY] * ns, out_specs=[ANY] * nl, out_shape=list(lands),
        scratch_shapes=[pltpu.SemaphoreType.DMA((n,)), pltpu.SemaphoreType.DMA((n,)),
                        pltpu.SemaphoreType.DMA((max(nloc, 1),))],
    )(*srcs)


def _add_pairs(a, b, name):
    _, r, cols = a.shape
    tr = ROW_TILE if r % ROW_TILE == 0 else r

    def body(a_ref, b_ref, o_ref, ob_ref):
        s = a_ref[...] + b_ref[...]
        o_ref[...] = s
        ob_ref[...] = s.astype(BF16)

    spec = pl.BlockSpec((None, tr, cols), lambda j, i: (j, i, 0))
    return pl.pallas_call(
        body, name=name, grid=(4, r // tr), in_specs=[spec, spec], out_specs=[spec, spec],
        out_shape=[jax.ShapeDtypeStruct(a.shape, F32), jax.ShapeDtypeStruct(a.shape, BF16)],
        compiler_params=_params(2),
    )(a, b)


def _sum_chips(own, landed, mine, name):
    _, r, cols = landed.shape
    tr = ROW_TILE if r % ROW_TILE == 0 else r

    def body(mine_ref, own_ref, land_ref, o_ref):
        acc = None
        for j in range(4):
            term = jnp.where(mine_ref[0] == j, own_ref[...], land_ref[j].astype(F32))
            acc = term if acc is None else acc + term
        o_ref[...] = acc

    return pl.pallas_call(
        body, name=name,
        grid_spec=pltpu.PrefetchScalarGridSpec(
            num_scalar_prefetch=1, grid=(r // tr,),
            in_specs=[pl.BlockSpec((tr, cols), lambda i, m: (i, 0)), pl.BlockSpec((4, tr, cols), lambda i, m: (0, i, 0))],
            out_specs=pl.BlockSpec((tr, cols), lambda i, m: (i, 0))),
        out_shape=jax.ShapeDtypeStruct((r, cols), F32), compiler_params=_params(1),
    )(mine, own, landed)


def _sum_devices(landed, name):
    def body(l_ref, o_ref):
        acc = l_ref[0]
        for j in range(1, 8):
            acc = acc + l_ref[j]
        o_ref[...] = acc

    return pl.pallas_call(body, name=name, out_shape=jax.ShapeDtypeStruct(landed.shape[1:], F32))(landed)


def _reduce_gradients(big, small):
    nb = len(big)
    x, y, c = _coords()
    mine = 2 * x + y
    flips = [(fx, fy, fc) for fx in (0, 1) for fy in (0, 1) for fc in (0, 1) if fx or fy or fc]

    def flip(fx, fy, fc):
        return lambda x, y, c: (x ^ fx, y ^ fy, c ^ fc)

    copies = [(a, (lambda j: lambda x, y, c: (j, 1 - c))(j), a, (lambda j: lambda x, y, c: (j,))(j), flip(0, 0, 1))
              for a in range(nb) for j in range(4)]
    copies += [(nb, lambda x, y, c: (), nb, lambda x, y, c: (4 * x + 2 * y + c,), flip(*f)) for f in flips]
    lands = [jax.ShapeDtypeStruct((4,) + g.shape[2:], F32) for g in big] + [jax.ShapeDtypeStruct((8,) + small.shape, F32)]
    local = [(nb, lambda x, y, c: (), nb, lambda x, y, c: (4 * x + 2 * y + c,), None)]
    landed = _exchange("swap_halves", list(big) + [small], lands, copies, local)
    small_sum = _sum_devices(landed[nb], "sum_small")

    chip_f32, chip_bf16 = [], []
    for a in range(nb):
        kept = lax.dynamic_index_in_dim(big[a], c, axis=1, keepdims=False)
        s, sb = _add_pairs(kept, landed[a], f"add_cores_{a}")
        chip_f32.append(s)
        chip_bf16.append(sb)

    chip_flips = [(1, 0), (0, 1), (1, 1)]
    copies = [(a, (lambda f: lambda x, y, c: (2 * (x ^ f[0]) + (y ^ f[1]),))(f), a, lambda x, y, c: (2 * x + y,),
               flip(f[0], f[1], 0)) for a in range(nb) for f in chip_flips]
    local = [(a, lambda x, y, c: (2 * x + y,), a, lambda x, y, c: (2 * x + y,), None) for a in range(nb)]
    lands = [jax.ShapeDtypeStruct(g.shape, BF16) for g in chip_bf16]
    landed = _exchange("send_chip_sums", chip_bf16, lands, copies, local)
    mine_arr = jnp.reshape(mine, (1,)).astype(jnp.int32)
    totals = [_sum_chips(lax.dynamic_index_in_dim(chip_f32[a], mine, axis=0, keepdims=False), landed[a], mine_arr,
                         f"sum_chips_{a}") for a in range(nb)]

    copies = [(a, lambda x, y, c: (), a, lambda x, y, c: (c,), flip(0, 0, 1)) for a in range(nb)]
    local = [(a, lambda x, y, c: (), a, lambda x, y, c: (c,), None) for a in range(nb)]
    lands = [jax.ShapeDtypeStruct((2,) + t.shape, F32) for t in totals]
    return _exchange("swap_sums", totals, lands, copies, local), small_sum


def kernel(x, conv_norm_g, conv_w_in, conv_w, conv_w_out, attn_norm_g, attn_w_in, attn_b_f, attn_q_norm_g, attn_k_norm_g, attn_w_out, loss_target, m_conv_norm_g, m_conv_w_in, m_conv_w, m_conv_w_out, m_attn_norm_g, m_attn_w_in, m_attn_b_f, m_attn_q_norm_g, m_attn_k_norm_g, m_attn_w_out, v_conv_norm_g, v_conv_w_in, v_conv_w, v_conv_w_out, v_attn_norm_g, v_attn_w_in, v_attn_b_f, v_attn_q_norm_g, v_attn_k_norm_g, v_attn_w_out):
    xi, yi, _ = _coords()
    chip = 2 * xi + yi
    D = x.shape[2]
    H = D // HEAD_DIM
    names = ["conv_norm_g", "conv_w_in", "conv_w", "conv_w_out", "attn_norm_g", "attn_w_in", "attn_b_f",
             "attn_q_norm_g", "attn_k_norm_g", "attn_w_out"]
    weights = dict(zip(names, [conv_norm_g, conv_w_in, conv_w, conv_w_out, attn_norm_g, attn_w_in, attn_b_f,
                               attn_q_norm_g, attn_k_norm_g, attn_w_out]))
    m_in = dict(zip(names, [m_conv_norm_g, m_conv_w_in, m_conv_w, m_conv_w_out, m_attn_norm_g, m_attn_w_in,
                            m_attn_b_f, m_attn_q_norm_g, m_attn_k_norm_g, m_attn_w_out]))
    v_in = dict(zip(names, [v_conv_norm_g, v_conv_w_in, v_conv_w, v_conv_w_out, v_attn_norm_g, v_attn_w_in,
                            v_attn_b_f, v_attn_q_norm_g, v_attn_k_norm_g, v_attn_w_out]))
    weights = {k: w[0] for k, w in weights.items()}
    m_in = {k: w[0] for k, w in m_in.items()}
    v_in = {k: w[0] for k, w in v_in.items()}

    big_names = ["conv_w_in", "attn_w_in", "conv_w_out", "attn_w_out"]
    halved = [weights[k].astype(BF16).reshape(2, weights[k].shape[0] // 2, weights[k].shape[1]) for k in big_names]
    q = D // 4
    small_w = jnp.concatenate([weights["conv_w"], weights["attn_norm_g"][None, :], jnp.zeros((4, q), F32)], axis=0)
    g_in, ga_in, g_out, ga_out, g_small = _all_gather(halved, [small_w])
    w_in = g_in.reshape(4, D, D).transpose(1, 0, 2).reshape(D, 4 * D)
    wa_in = ga_in.reshape(4, D, D + H // 4).transpose(1, 0, 2).reshape(D, 4 * D + H)
    w_out = g_out.reshape(D, D)
    wa_out = ga_out.reshape(D, D)
    conv_w_full = g_small[:, 0:3, :].transpose(1, 0, 2).reshape(3, D)
    attn_g_full = g_small[:, 3, :].reshape(1, D)

    loss_part, grad_x, grads = _local_step(x[0], loss_target[0], weights["conv_norm_g"][None, :], w_in, conv_w_full,
                                           w_out, attn_g_full, wa_in, weights["attn_b_f"][None, :],
                                           weights["attn_q_norm_g"][None, :], weights["attn_k_norm_g"][None, :], wa_out)
    loss = lax.psum(loss_part, ("x", "y", "c"))

    big = [grads["conv_w_in"].reshape(D, 4, D).transpose(1, 0, 2).reshape(4, 2, D // 2, D),
           grads["attn_w_in"].reshape(D, 4, D + H // 4).transpose(1, 0, 2).reshape(4, 2, D // 2, D + H // 4),
           grads["conv_w_out"].reshape(4, 2, D // 8, D), grads["attn_w_out"].reshape(4, 2, D // 8, D)]
    tail = jnp.concatenate([grads["attn_b_f"], grads["attn_q_norm_g"], grads["attn_k_norm_g"],
                            jnp.zeros((1, D - H - 2 * HEAD_DIM), F32)], axis=1)
    small = jnp.concatenate([grads["conv_norm_g"], grads["conv_w"], grads["attn_norm_g"], tail,
                             jnp.zeros((2, D), F32)], axis=0)
    reduced, small_sum = _reduce_gradients(big, small)
    final = {k: r.reshape(weights[k].shape) for k, r in zip(big_names, reduced)}
    final["conv_norm_g"] = small_sum[0]
    final["conv_w"] = lax.dynamic_slice_in_dim(small_sum[1:4], chip * q, q, axis=1)
    final["attn_norm_g"] = lax.dynamic_slice_in_dim(small_sum[4], chip * q, q, axis=0)
    final["attn_b_f"] = small_sum[5, :H]
    final["attn_q_norm_g"] = small_sum[5, H:H + HEAD_DIM]
    final["attn_k_norm_g"] = small_sum[5, H + HEAD_DIM:H + 2 * HEAD_DIM]

    delta, new_m, new_v = {}, {}, {}
    for k in names:
        shape = weights[k].shape
        as2d = (lambda a: a.reshape(1, -1)) if len(shape) == 1 else (lambda a: a)
        d, m2, v2 = _adamw(as2d(weights[k]), as2d(final[k]), as2d(m_in[k]), as2d(v_in[k]), "adamw_" + k)
        delta[k], new_m[k], new_v[k] = d.reshape(shape), m2.reshape(shape), v2.reshape(shape)
    lead = lambda a: a[None]
    return (loss, grad_x[None], *[lead(final[k]) for k in names], *[lead(delta[k]) for k in names],
            *[lead(new_m[k]) for k in names], *[lead(new_v[k]) for k in names])
```

```python
import functools

import jax
import jax.numpy as jnp
from jax import lax
from jax.experimental import pallas as pl
from jax.experimental.pallas import tpu as pltpu

F32 = jnp.float32
BF16 = jnp.bfloat16
HEAD_DIM = 64
LANES = 128
AUG_A = 64
AUG_B = 67
RMS_EPS = 1e-6
NEG = -1e30
Q_SCALE = 0.125
ROW_TILE = 256
ATT_GROUP = 4
SKIP_LOG = 120.0
TN_ROWS = 512
VMEM_LIMIT = 56 << 20
ADAM_LR, ADAM_B1, ADAM_B2, ADAM_EPS, ADAM_WD, ADAM_STEP = 0.001, 0.9, 0.999, 1e-08, 0.01, 10
MESH = pl.DeviceIdType.MESH
ANY = pl.BlockSpec(memory_space=pl.ANY)


def _lane():
    return lax.broadcasted_iota(jnp.int32, (1, LANES), 1)


def _split3(x):
    hi = x.astype(BF16).astype(F32)
    r = x - hi
    mid = r.astype(BF16).astype(F32)
    lo = (r - mid).astype(BF16).astype(F32)
    return hi, mid, lo


def _put(base, lane, start, parts):
    for j, p in enumerate(parts):
        base = jnp.where(lane == start + j, p, base)
    return base


def _col(x, lane, idx):
    return jnp.sum(jnp.where(lane == idx, x, 0.0), axis=1, keepdims=True)


def _head_tile(ref, hd, lane):
    j = hd // 2
    t = ref[:, LANES * j:LANES * (j + 1)]
    if hd % 2:
        t = pltpu.roll(t, HEAD_DIM, 1)
    return jnp.where(lane < HEAD_DIM, t, 0.0)


def _pair_tile(even, odd):
    return even + pltpu.roll(odd, HEAD_DIM, 1)


def _sigmoid(x):
    return 1.0 / (1.0 + jnp.exp(-x))


def _dot(a, b):
    return jnp.dot(a, b, preferred_element_type=F32)


def _dot_nt(a, b):
    return lax.dot_general(a, b, (((1,), (1,)), ((), ())), preferred_element_type=F32)


def _dot_tn(a, b):
    return lax.dot_general(a, b, (((0,), (0,)), ((), ())), preferred_element_type=F32)


def _dot01(tri, x):
    hi, mid, lo = _split3(x)
    return _dot(tri, hi.astype(BF16)) + _dot(tri, mid.astype(BF16)) + _dot(tri, lo.astype(BF16))


def _rms_bwd(dh, x, g):
    inv = lax.rsqrt(jnp.mean(x * x, axis=-1, keepdims=True) + RMS_EPS)
    xh = x * inv
    dxn = dh * g
    dx = inv * (dxn - xh * jnp.mean(dxn * xh, axis=-1, keepdims=True))
    return dx, jnp.sum(dh * xh, axis=0, keepdims=True)


def _head_rms_bwd(dn, t, g):
    inv = lax.rsqrt(jnp.sum(t * t, axis=1, keepdims=True) * (1.0 / HEAD_DIM) + RMS_EPS)
    th = t * inv
    gd = dn * g
    d = inv * (gd - th * (jnp.sum(gd * th, axis=1, keepdims=True) * (1.0 / HEAD_DIM)))
    return d, jnp.sum(dn * th, axis=0, keepdims=True)


def _params(n_grid):
    return pltpu.CompilerParams(dimension_semantics=("arbitrary",) * n_grid, vmem_limit_bytes=VMEM_LIMIT)


def _rows(tm, cols, rev=None):
    if rev is None:
        return pl.BlockSpec((tm, cols), lambda i: (i, 0))
    return pl.BlockSpec((tm, cols), lambda i: (rev - i, 0))


def _whole(shape):
    return pl.BlockSpec(shape, lambda *_: (0,) * len(shape))


def _conv_fwd(x, g1, w_in, conv_w, w_out):
    S, D = x.shape
    tm = min(ROW_TILE, S)

    def body(x_ref, g_ref, win_ref, cw_ref, wout_ref, proj_ref, h_ref, yc_ref, y_ref, x1_ref, prev_u):
        i = pl.program_id(0)
        xv = x_ref[...]
        inv = lax.rsqrt(jnp.mean(xv * xv, axis=-1, keepdims=True) + RMS_EPS)
        h = (xv * inv * g_ref[...]).astype(BF16)
        h_ref[...] = h
        for j in range(4):
            proj_ref[:, j * D:(j + 1) * D] = _dot(h, win_ref[:, j * D:(j + 1) * D])

        @pl.when(i == 0)
        def _():
            prev_u[...] = jnp.zeros((tm, D), F32)

        u = proj_ref[:, D:2 * D] * proj_ref[:, 2 * D:3 * D]
        pu = prev_u[...]
        row = lax.broadcasted_iota(jnp.int32, (tm, 1), 0)
        u1 = jnp.where(row < 1, pltpu.roll(pu, 1, 0), pltpu.roll(u, 1, 0))
        u2 = jnp.where(row < 2, pltpu.roll(pu, 2, 0), pltpu.roll(u, 2, 0))
        prev_u[...] = u
        w = cw_ref[...]
        yc = w[2:3] * u + w[1:2] * u1 + w[0:1] * u2
        yc_ref[...] = yc
        z = proj_ref[:, 3 * D:4 * D]
        y = (proj_ref[:, 0:D] * yc * (z * _sigmoid(z))).astype(BF16)
        y_ref[...] = y
        x1_ref[...] = xv + _dot(y, wout_ref[...])

    return pl.pallas_call(
        body, name="conv_fwd", grid=(S // tm,),
        in_specs=[_rows(tm, D), _whole((1, D)), _whole((D, 4 * D)), _whole((3, D)), _whole((D, D))],
        out_specs=[_rows(tm, 4 * D), _rows(tm, D), _rows(tm, D), _rows(tm, D), _rows(tm, D)],
        out_shape=[jax.ShapeDtypeStruct((S, 4 * D), F32), jax.ShapeDtypeStruct((S, D), BF16),
                   jax.ShapeDtypeStruct((S, D), F32), jax.ShapeDtypeStruct((S, D), BF16),
                   jax.ShapeDtypeStruct((S, D), F32)],
        scratch_shapes=[pltpu.VMEM((tm, D), F32)],
        compiler_params=_params(1),
    )(x, g1, w_in, conv_w, w_out)


def _conv_bwd(dx1, x, g1, w_in, w_out, conv_w, proj, yc):
    S, D = x.shape
    tm = min(ROW_TILE, S)
    last = S // tm - 1

    def body(dx1_ref, x_ref, g_ref, win_ref, wout_ref, cw_ref, proj_ref, yc_ref,
             dproj_ref, dx_ref, dx1b_ref, dg_ref, dcw_ref, next_d):
        @pl.when(pl.program_id(0) == 0)
        def _():
            dg_ref[...] = jnp.zeros((1, D), F32)
            dcw_ref[...] = jnp.zeros((3, D), F32)
            next_d[...] = jnp.zeros((tm, D), F32)

        dx1v = dx1_ref[...]
        dx1b = dx1v.astype(BF16)
        dx1b_ref[...] = dx1b
        dy = _dot_nt(dx1b, wout_ref[...])
        b = proj_ref[:, 0:D]
        c = proj_ref[:, D:2 * D]
        xin = proj_ref[:, 2 * D:3 * D]
        z = proj_ref[:, 3 * D:4 * D]
        sg = _sigmoid(z)
        sz = z * sg
        ycv = yc_ref[...]
        d0 = dy * b * sz
        dproj_ref[:, 0:D] = (dy * ycv * sz).astype(BF16)
        dproj_ref[:, 3 * D:4 * D] = (dy * b * ycv * (sg * (1.0 + z * (1.0 - sg)))).astype(BF16)
        nd = next_d[...]
        row = lax.broadcasted_iota(jnp.int32, (tm, 1), 0)
        d1 = jnp.where(row >= tm - 1, pltpu.roll(nd, tm - 1, 0), pltpu.roll(d0, tm - 1, 0))
        d2 = jnp.where(row >= tm - 2, pltpu.roll(nd, tm - 2, 0), pltpu.roll(d0, tm - 2, 0))
        next_d[...] = d0
        w = cw_ref[...]
        du = w[2:3] * d0 + w[1:2] * d1 + w[0:1] * d2
        u = c * xin
        dcw_ref[2:3, :] += jnp.sum(d0 * u, axis=0, keepdims=True)
        dcw_ref[1:2, :] += jnp.sum(d1 * u, axis=0, keepdims=True)
        dcw_ref[0:1, :] += jnp.sum(d2 * u, axis=0, keepdims=True)
        dproj_ref[:, D:2 * D] = (du * xin).astype(BF16)
        dproj_ref[:, 2 * D:3 * D] = (du * c).astype(BF16)
        dh = _dot_nt(dproj_ref[...], win_ref[...])
        dxn, dg = _rms_bwd(dh, x_ref[...], g_ref[...])
        dx_ref[...] = dx1v + dxn
        dg_ref[...] += dg

    return pl.pallas_call(
        body, name="conv_bwd", grid=(S // tm,),
        in_specs=[_rows(tm, D, last), _rows(tm, D, last), _whole((1, D)), _whole((D, 4 * D)), _whole((D, D)),
                  _whole((3, D)), _rows(tm, 4 * D, last), _rows(tm, D, last)],
        out_specs=[_rows(tm, 4 * D, last), _rows(tm, D, last), _rows(tm, D, last), _whole((1, D)), _whole((3, D))],
        out_shape=[jax.ShapeDtypeStruct((S, 4 * D), BF16), jax.ShapeDtypeStruct((S, D), F32),
                   jax.ShapeDtypeStruct((S, D), BF16), jax.ShapeDtypeStruct((1, D), F32),
                   jax.ShapeDtypeStruct((3, D), F32)],
        scratch_shapes=[pltpu.VMEM((tm, D), F32)],
        compiler_params=_params(1),
    )(dx1, x, g1, w_in, w_out, conv_w, proj, yc)


def _attn_front(x1, g2, w, wf, bf, gq, gk):
    S, D = x1.shape
    H = D // HEAD_DIM
    tm = min(ROW_TILE, S)
    tri = (lax.broadcasted_iota(jnp.int32, (tm, tm), 1) <= lax.broadcasted_iota(jnp.int32, (tm, tm), 0)).astype(BF16)

    def body(x_ref, g_ref, w_ref, wf_ref, bf_ref, gq_ref, gk_ref, tri_ref,
             h_ref, qraw_ref, kraw_ref, z_ref, f_ref, c_ref, qa_ref, ka_ref, va_ref, vt_ref, carry, v_s):
        @pl.when(pl.program_id(0) == 0)
        def _():
            carry[...] = jnp.zeros((8, LANES), F32)

        xv = x_ref[...]
        inv = lax.rsqrt(jnp.mean(xv * xv, axis=-1, keepdims=True) + RMS_EPS)
        h = (xv * inv * g_ref[...]).astype(BF16)
        h_ref[...] = h
        qraw_ref[...] = _dot(h, w_ref[:, 0:D])
        kraw_ref[...] = _dot(h, w_ref[:, D:2 * D])
        v_s[...] = _dot(h, w_ref[:, 2 * D:3 * D])
        z_ref[...] = _dot(h, w_ref[:, 3 * D:4 * D])
        lane = _lane()
        f = _dot(h, wf_ref[...]) + bf_ref[...]
        f_ref[...] = f
        logf = jnp.where(lane < H, jnp.minimum(f, 0.0) - jnp.log(1.0 + jnp.exp(-jnp.abs(f))), 0.0)
        cs = _dot01(tri_ref[...], logf) + carry[0:1, :]
        c_ref[...] = cs
        carry[...] = jnp.broadcast_to(cs[tm - 1:tm, :], (8, LANES))
        one_a = (lane >= AUG_A) & (lane < AUG_A + 3)
        one_b = (lane >= AUG_B) & (lane < AUG_B + 3)
        for hd in range(H):
            sl = slice(LANES * hd, LANES * (hd + 1))
            ch = _col(cs, lane, hd)
            qt = _head_tile(qraw_ref, hd, lane)
            qn = qt * lax.rsqrt(jnp.sum(qt * qt, axis=1, keepdims=True) * (1.0 / HEAD_DIM) + RMS_EPS) * gq_ref[...]
            qa = _put(jnp.where(one_b, 1.0, qn * Q_SCALE), lane, AUG_A, _split3(ch))
            qa_ref[:, sl] = qa.astype(BF16)
            kt = _head_tile(kraw_ref, hd, lane)
            kn = kt * lax.rsqrt(jnp.sum(kt * kt, axis=1, keepdims=True) * (1.0 / HEAD_DIM) + RMS_EPS) * gk_ref[...]
            ka = _put(jnp.where(one_a, 1.0, kn), lane, AUG_B, _split3(-ch))
            ka_ref[:, sl] = ka.astype(BF16)
            va = jnp.where(one_a, 1.0, _head_tile(v_s, hd, lane))
            va_ref[:, sl] = va.astype(BF16)
            vt_ref[hd] = va.T.astype(BF16)

    nb = S // tm
    return pl.pallas_call(
        body, name="attn_front", grid=(nb,),
        in_specs=[_rows(tm, D), _whole((1, D)), _whole((D, 4 * D)), _whole((D, LANES)), _whole((1, LANES)),
                  _whole((1, LANES)), _whole((1, LANES)), _whole((tm, tm))],
        out_specs=[_rows(tm, D), _rows(tm, D), _rows(tm, D), _rows(tm, D), _rows(tm, LANES), _rows(tm, LANES),
                   _rows(tm, H * LANES), _rows(tm, H * LANES), _rows(tm, H * LANES),
                   pl.BlockSpec((H, None, LANES, tm), lambda i: (0, i, 0, 0))],
        out_shape=[jax.ShapeDtypeStruct((S, D), BF16), jax.ShapeDtypeStruct((S, D), F32),
                   jax.ShapeDtypeStruct((S, D), F32), jax.ShapeDtypeStruct((S, D), F32),
                   jax.ShapeDtypeStruct((S, LANES), F32), jax.ShapeDtypeStruct((S, LANES), F32),
                   jax.ShapeDtypeStruct((S, H * LANES), BF16), jax.ShapeDtypeStruct((S, H * LANES), BF16),
                   jax.ShapeDtypeStruct((S, H * LANES), BF16), jax.ShapeDtypeStruct((H, nb, LANES, tm), BF16)],
        scratch_shapes=[pltpu.VMEM((8, LANES), F32), pltpu.VMEM((tm, D), F32)],
        compiler_params=_params(1),
    )(x1, g2, w, wf, bf, gq, gk, tri)


def _skip_tables(c, gq, gk, T, G):
    nb = c.shape[0] // T
    bound = 8.0 * jnp.max(jnp.abs(gq)) * jnp.max(jnp.abs(gk))
    first, last = c[0::T, :], c[T - 1::T, :]
    idx = jnp.arange(nb)
    need = (last[None, :, :] <= first[:, None, :] + (SKIP_LOG + 2.0 * bound)) & (idx[None, :, None] < idx[:, None, None])
    need = need | (idx[None, :, None] == idx[:, None, None])
    kstart = jnp.argmax(need, axis=1)
    qend = nb - 1 - jnp.argmax(need[::-1], axis=0)
    kstart = jnp.min(kstart.reshape(nb // G, G, -1), axis=1)
    qend = jnp.max(qend.reshape(nb // G, G, -1), axis=1)
    return kstart.T.astype(jnp.int32), qend.T.astype(jnp.int32)


def _attn_fwd(kstart, qa, ka, vt):
    S = qa.shape[0]
    H = qa.shape[1] // LANES
    nb, T = vt.shape[1], vt.shape[3]
    G = nb // kstart.shape[1]
    W = G * T

    def body(ks_ref, q_ref, k_ref, vt_ref, o_ref, m_ref, acc_ref):
        h, g = pl.program_id(0), pl.program_id(1)
        q = q_ref[...]
        m_ref[...] = jnp.full((8, W), NEG, F32)
        acc_ref[...] = jnp.zeros((LANES, W), F32)

        def update(st, vtb, lo):
            m_old = m_ref[0:1, lo:]
            m_new = jnp.maximum(m_old, jnp.max(st, axis=0, keepdims=True))
            p = jnp.exp(st - m_new).astype(BF16)
            acc_ref[:, lo:] = acc_ref[:, lo:] * jnp.exp(m_old - m_new) + _dot(vtb, p)
            m_ref[:, lo:] = jnp.broadcast_to(m_new, (8, W - lo))

        def loop(ki, carry):
            kb = k_ref[pl.ds(pl.multiple_of(ki * T, T), T), :]
            update(_dot_nt(kb, q), vt_ref[ki], 0)
            return carry

        lax.fori_loop(ks_ref[h, g], g * G, loop, 0)
        for j in range(G):
            ki = g * G + j
            kb = k_ref[pl.ds(pl.multiple_of(ki * T, T), T), :]
            st = _dot_nt(kb, q[j * T:, :])
            causal = lax.broadcasted_iota(jnp.int32, st.shape, 0) <= lax.broadcasted_iota(jnp.int32, st.shape, 1)
            update(jnp.where(causal, st, NEG), vt_ref[ki], j * T)
        acc = acc_ref[...]
        l = acc[AUG_A:AUG_A + 1, :]
        lse = m_ref[0:1, :] + jnp.log(l)
        feat = lax.broadcasted_iota(jnp.int32, (LANES, 1), 0)
        o_ref[...] = jnp.where(feat == AUG_A + 1, lse, acc / l).T

    return pl.pallas_call(
        body, name="attn_fwd",
        grid_spec=pltpu.PrefetchScalarGridSpec(
            num_scalar_prefetch=1, grid=(H, nb // G),
            in_specs=[pl.BlockSpec((W, LANES), lambda h, i, ks: (i, h)),
                      pl.BlockSpec((S, LANES), lambda h, i, ks: (0, h)),
                      pl.BlockSpec((None, nb, LANES, T), lambda h, i, ks: (h, 0, 0, 0))],
            out_specs=pl.BlockSpec((W, LANES), lambda h, i, ks: (i, h)),
            scratch_shapes=[pltpu.VMEM((8, W), F32), pltpu.VMEM((LANES, W), F32)]),
        out_shape=jax.ShapeDtypeStruct((S, H * LANES), F32),
        compiler_params=_params(2),
    )(kstart, qa, ka, vt)


def _attn_out(o_aug, z, x1, target, w_out, qa, c):
    S, D = x1.shape
    H = D // HEAD_DIM
    tm = min(ROW_TILE, S)

    def body(o_ref, z_ref, x1_ref, t_ref, w_ref, q_ref, c_ref,
             dx2_ref, dx2b_ref, o2b_ref, dz_ref, doa_ref, qa2_ref, loss_ref, oc_s, do_s):
        @pl.when(pl.program_id(0) == 0)
        def _():
            loss_ref[...] = jnp.zeros((1, LANES), F32)

        lane = _lane()
        lses = []
        for j in range(H // 2):
            oe = o_ref[:, 2 * LANES * j:2 * LANES * j + LANES]
            oo = o_ref[:, 2 * LANES * j + LANES:2 * LANES * (j + 1)]
            lses += [_col(oe, lane, AUG_A + 1), _col(oo, lane, AUG_A + 1)]
            oc_s[:, LANES * j:LANES * (j + 1)] = _pair_tile(jnp.where(lane < HEAD_DIM, oe, 0.0),
                                                            jnp.where(lane < HEAD_DIM, oo, 0.0))
        oc = oc_s[...]
        zv = z_ref[...]
        sg = _sigmoid(zv)
        sz = zv * sg
        o2 = (oc * sz).astype(BF16)
        o2b_ref[...] = o2
        e = x1_ref[...] + _dot(o2, w_ref[...]) - t_ref[...]
        sq = jnp.sum(jnp.sum(e * e, axis=1, keepdims=True), axis=0, keepdims=True)
        loss_ref[...] += jnp.broadcast_to(sq * (0.5 / D), (1, LANES))
        dx2 = e * (1.0 / D)
        dx2_ref[...] = dx2
        dx2b = dx2.astype(BF16)
        dx2b_ref[...] = dx2b
        do2 = _dot_nt(dx2b, w_ref[...])
        dz_ref[...] = do2 * oc * (sg * (1.0 + zv * (1.0 - sg)))
        do_s[...] = do2 * sz
        cs = c_ref[...]
        for hd in range(H):
            sl = slice(LANES * hd, LANES * (hd + 1))
            dt = _head_tile(do_s, hd, lane)
            delta = jnp.sum(dt * _head_tile(oc_s, hd, lane), axis=1, keepdims=True)
            doa_ref[:, sl] = _put(dt, lane, AUG_A, _split3(-delta)).astype(BF16)
            rq = _col(cs, lane, hd) - lses[hd]
            qa2_ref[:, sl] = _put(q_ref[:, sl].astype(F32), lane, AUG_A, _split3(rq)).astype(BF16)

    return pl.pallas_call(
        body, name="attn_out", grid=(S // tm,),
        in_specs=[_rows(tm, H * LANES), _rows(tm, D), _rows(tm, D), _rows(tm, D), _whole((D, D)),
                  _rows(tm, H * LANES), _rows(tm, LANES)],
        out_specs=[_rows(tm, D), _rows(tm, D), _rows(tm, D), _rows(tm, D), _rows(tm, H * LANES),
                   _rows(tm, H * LANES), _whole((1, LANES))],
        out_shape=[jax.ShapeDtypeStruct((S, D), F32), jax.ShapeDtypeStruct((S, D), BF16),
                   jax.ShapeDtypeStruct((S, D), BF16), jax.ShapeDtypeStruct((S, D), F32),
                   jax.ShapeDtypeStruct((S, H * LANES), BF16), jax.ShapeDtypeStruct((S, H * LANES), BF16),
                   jax.ShapeDtypeStruct((1, LANES), F32)],
        scratch_shapes=[pltpu.VMEM((tm, D), F32), pltpu.VMEM((tm, D), F32)],
        compiler_params=_params(1),
    )(o_aug, z, x1, target, w_out, qa, c)


def _attn_bwd(qend, qa2, doa, ka, va, T):
    S = qa2.shape[0]
    H = qa2.shape[1] // LANES
    nb = S // T
    G = nb // qend.shape[1]
    W = G * T

    def body(qe_ref, q_ref, do_ref, k_ref, v_ref, dq_ref, dk_ref, dv_ref, dk_acc, dv_acc):
        h, g = pl.program_id(0), pl.program_id(1)

        @pl.when(g == 0)
        def _():
            dq_ref[...] = jnp.zeros((nb, LANES, T), F32)

        kb = k_ref[...]
        vb = v_ref[...]
        kt = kb.astype(F32).T.astype(BF16)
        dk_acc[...] = jnp.zeros((W, LANES), F32)
        dv_acc[...] = jnp.zeros((W, LANES), F32)

        def step(qi, rows, masked):
            q0 = pl.multiple_of(qi * T, T)
            qb = q_ref[pl.ds(q0, T), :]
            dob = do_ref[pl.ds(q0, T), :]
            st = _dot_nt(kb[:rows], qb)
            if masked:
                key = lax.broadcasted_iota(jnp.int32, (rows, T), 0)
                st = jnp.where(key <= lax.broadcasted_iota(jnp.int32, (rows, T), 1) + (rows - T), st, NEG)
            p = jnp.exp(st)
            ds = (p * _dot_nt(vb[:rows], dob)).astype(BF16)
            dv_acc[0:rows, :] += _dot(p.astype(BF16), dob)
            dk_acc[0:rows, :] += _dot(ds, qb)
            dq_ref[qi] += _dot(kt[:, :rows], ds)

        def loop(qi, carry):
            step(qi, W, False)
            return carry

        for j in range(G):
            step(g * G + j, (j + 1) * T, True)
        lax.fori_loop(g * G + G, qe_ref[h, g] + 1, loop, 0)
        dk_ref[...] = dk_acc[...]
        dv_ref[...] = dv_acc[...]

    return pl.pallas_call(
        body, name="attn_bwd",
        grid_spec=pltpu.PrefetchScalarGridSpec(
            num_scalar_prefetch=1, grid=(H, nb // G),
            in_specs=[pl.BlockSpec((S, LANES), lambda h, i, qe: (0, h)), pl.BlockSpec((S, LANES), lambda h, i, qe: (0, h)),
                      pl.BlockSpec((W, LANES), lambda h, i, qe: (i, h)), pl.BlockSpec((W, LANES), lambda h, i, qe: (i, h))],
            out_specs=[pl.BlockSpec((None, nb, LANES, T), lambda h, i, qe: (h, 0, 0, 0)),
                       pl.BlockSpec((W, LANES), lambda h, i, qe: (i, h)), pl.BlockSpec((W, LANES), lambda h, i, qe: (i, h))],
            scratch_shapes=[pltpu.VMEM((W, LANES), F32), pltpu.VMEM((W, LANES), F32)]),
        out_shape=[jax.ShapeDtypeStruct((H, nb, LANES, T), F32), jax.ShapeDtypeStruct((S, H * LANES), F32),
                   jax.ShapeDtypeStruct((S, H * LANES), F32)],
        compiler_params=_params(2),
    )(qend, qa2, doa, ka, va)


def _attn_proj_bwd(dqt, dka, dva, qraw, kraw, dz, f, gq, gk, w, wf, x1, g2, dx2):
    S, D = x1.shape
    H = D // HEAD_DIM
    tm = dqt.shape[3]
    last = S // tm - 1
    tri = (lax.broadcasted_iota(jnp.int32, (tm, tm), 1) >= lax.broadcasted_iota(jnp.int32, (tm, tm), 0)).astype(BF16)

    def body(dqt_ref, dk_ref, dv_ref, q_ref, k_ref, dz_ref, f_ref, gq_ref, gk_ref, w_ref, wf_ref, x1_ref, g2_ref,
             dx2_ref, tri_ref, dproj_ref, dx1_ref, dg2_ref, small_ref, carry):
        @pl.when(pl.program_id(0) == 0)
        def _():
            dg2_ref[...] = jnp.zeros((1, D), F32)
            small_ref[...] = jnp.zeros((8, LANES), F32)
            carry[...] = jnp.zeros((8, LANES), F32)

        lane = _lane()
        dcs = jnp.zeros((tm, LANES), F32)
        dgq = jnp.zeros((1, LANES), F32)
        dgk = jnp.zeros((1, LANES), F32)
        for j in range(H // 2):
            dq2, dk2, dv2 = [], [], []
            for hd in (2 * j, 2 * j + 1):
                sl = slice(LANES * hd, LANES * (hd + 1))
                dqf = dqt_ref[hd].T
                dqn = jnp.where(lane < HEAD_DIM, dqf * Q_SCALE, 0.0)
                d, dg = _head_rms_bwd(dqn, _head_tile(q_ref, hd, lane), gq_ref[...])
                dq2.append(d)
                dgq = dgq + dg
                dkt = dk_ref[:, sl]
                dcs = dcs + jnp.where(lane == hd, _col(dqf, lane, AUG_A) - _col(dkt, lane, AUG_B), 0.0)
                d, dg = _head_rms_bwd(jnp.where(lane < HEAD_DIM, dkt, 0.0), _head_tile(k_ref, hd, lane), gk_ref[...])
                dk2.append(d)
                dgk = dgk + dg
                dv2.append(jnp.where(lane < HEAD_DIM, dv_ref[:, sl], 0.0))
            for part, pair in enumerate((dq2, dk2, dv2)):
                dproj_ref[:, part * D + LANES * j:part * D + LANES * (j + 1)] = _pair_tile(*pair).astype(BF16)
        dproj_ref[:, 3 * D:4 * D] = dz_ref[...].astype(BF16)
        dlogf = _dot01(tri_ref[...], dcs) + carry[0:1, :]
        carry[...] = jnp.broadcast_to(dlogf[0:1, :], (8, LANES))
        df = dlogf * (1.0 / (1.0 + jnp.exp(f_ref[...])))
        dproj_ref[:, 4 * D:4 * D + LANES] = df.astype(BF16)
        small_ref[0:1, :] += jnp.sum(df, axis=0, keepdims=True)
        small_ref[1:2, :] += dgq
        small_ref[2:3, :] += dgk
        dh = _dot_nt(dproj_ref[:, 0:4 * D], w_ref[...]) + _dot_nt(dproj_ref[:, 4 * D:4 * D + LANES], wf_ref[...])
        dxn, dg = _rms_bwd(dh, x1_ref[...], g2_ref[...])
        dx1_ref[...] = dx2_ref[...] + dxn
        dg2_ref[...] += dg

    W = 4 * D + LANES
    return pl.pallas_call(
        body, name="attn_proj_bwd", grid=(S // tm,),
        in_specs=[pl.BlockSpec((H, None, LANES, tm), lambda i: (0, last - i, 0, 0)),
                  _rows(tm, H * LANES, last), _rows(tm, H * LANES, last), _rows(tm, D, last), _rows(tm, D, last),
                  _rows(tm, D, last), _rows(tm, LANES, last), _whole((1, LANES)), _whole((1, LANES)),
                  _whole((D, 4 * D)), _whole((D, LANES)), _rows(tm, D, last), _whole((1, D)), _rows(tm, D, last),
                  _whole((tm, tm))],
        out_specs=[_rows(tm, W, last), _rows(tm, D, last), _whole((1, D)), _whole((8, LANES))],
        out_shape=[jax.ShapeDtypeStruct((S, W), BF16), jax.ShapeDtypeStruct((S, D), F32),
                   jax.ShapeDtypeStruct((1, D), F32), jax.ShapeDtypeStruct((8, LANES), F32)],
        scratch_shapes=[pltpu.VMEM((8, LANES), F32)],
        compiler_params=_params(1),
    )(dqt, dka, dva, qraw, kraw, dz, f, gq, gk, w, wf, x1, g2, dx2, tri)


def _matmul_tn(a, b, col0, n, tn, name):
    S, M = a.shape
    ts = min(TN_ROWS, S)
    off = col0 // tn

    def body(a_ref, b_ref, o_ref):
        @pl.when(pl.program_id(1) == 0)
        def _():
            o_ref[...] = jnp.zeros((M, tn), F32)

        o_ref[...] += _dot_tn(a_ref[...], b_ref[...])

    return pl.pallas_call(
        body, name=name, grid=(n // tn, S // ts),
        in_specs=[pl.BlockSpec((ts, M), lambda j, s: (s, 0)), pl.BlockSpec((ts, tn), lambda j, s: (s, off + j))],
        out_specs=pl.BlockSpec((M, tn), lambda j, s: (0, j)),
        out_shape=jax.ShapeDtypeStruct((M, n), F32),
        compiler_params=_params(2),
    )(a, b)


def _adamw(w, g, m, v, name):
    r, c = w.shape
    tr = ROW_TILE if r % ROW_TILE == 0 else r

    def body(w_ref, g_ref, m_ref, v_ref, d_ref, m2_ref, v2_ref):
        gv = g_ref[...]
        m2 = ADAM_B1 * m_ref[...] + (1.0 - ADAM_B1) * gv
        v2 = ADAM_B2 * v_ref[...] + (1.0 - ADAM_B2) * (gv * gv)
        m2_ref[...] = m2
        v2_ref[...] = v2
        m_hat = m2 / (1.0 - ADAM_B1 ** ADAM_STEP)
        v_hat = v2 / (1.0 - ADAM_B2 ** ADAM_STEP)
        d_ref[...] = -ADAM_LR * (m_hat / (jnp.sqrt(v_hat) + ADAM_EPS) + ADAM_WD * w_ref[...])

    spec = _rows(tr, c)
    return pl.pallas_call(
        body, name=name, grid=(r // tr,), in_specs=[spec] * 4, out_specs=[spec] * 3,
        out_shape=[jax.ShapeDtypeStruct((r, c), F32)] * 3, compiler_params=_params(1),
    )(w, g, m, v)


def _local_step(x, target, g1, w_in, conv_w, w_out, g2, wa_in, b_f, gq, gk, wa_out):
    S, D = x.shape
    H = D // HEAD_DIM
    w_qkvz = wa_in[:, :4 * D]
    wf = jnp.pad(wa_in[:, 4 * D:], ((0, 0), (0, LANES - H)))
    bf = jnp.pad(b_f, ((0, 0), (0, LANES - H)))
    gq128 = jnp.pad(gq, ((0, 0), (0, LANES - HEAD_DIM)))
    gk128 = jnp.pad(gk, ((0, 0), (0, LANES - HEAD_DIM)))

    proj, h1, yc, y, x1 = _conv_fwd(x, g1, w_in, conv_w, w_out)
    h2, qraw, kraw, z, f, c, qa, ka, va, vt = _attn_front(x1, g2, w_qkvz, wf, bf, gq128, gk128)
    T = vt.shape[3]
    kstart, qend = _skip_tables(c[:, :H], gq, gk, T, min(ATT_GROUP, S // T))
    o_aug = _attn_fwd(kstart, qa, ka, vt)
    dx2, dx2b, o2b, dz, doa, qa2, loss = _attn_out(o_aug, z, x1, target, wa_out, qa, c)
    dqt, dka, dva = _attn_bwd(qend, qa2, doa, ka, va, T)
    dproj2, dx1, dg2, small = _attn_proj_bwd(dqt, dka, dva, qraw, kraw, dz, f, gq128, gk128, w_qkvz, wf, x1, g2, dx2)
    dproj1, dx, dx1b, dg1, dcw = _conv_bwd(dx1, x, g1, w_in, w_out, conv_w, proj, yc)

    tn = min(1024, D)
    dwa_out = _matmul_tn(o2b, dx2b, 0, D, tn, "dw_attn_out")
    dwa_in = jnp.concatenate([_matmul_tn(h2, dproj2, 0, 4 * D, tn, "dw_attn_in"),
                              _matmul_tn(h2, dproj2, 4 * D, LANES, LANES, "dw_attn_f")[:, :H]], axis=1)
    dw_out = _matmul_tn(y, dx1b, 0, D, tn, "dw_conv_out")
    dw_in = _matmul_tn(h1, dproj1, 0, 4 * D, tn, "dw_conv_in")
    grads = dict(conv_norm_g=dg1, conv_w_in=dw_in, conv_w=dcw, conv_w_out=dw_out, attn_norm_g=dg2,
                 attn_w_in=dwa_in, attn_b_f=small[0:1, :H], attn_q_norm_g=small[1:2, :HEAD_DIM],
                 attn_k_norm_g=small[2:3, :HEAD_DIM], attn_w_out=dwa_out)
    return loss[0, 0], dx, grads


def _coords():
    return lax.axis_index("x"), lax.axis_index("y"), lax.axis_index("c")


def _at(ref, idx):
    return ref.at[idx] if idx else ref


def _other_chips(x, y):
    return [(1 - x, y), (x, 1 - y), (1 - x, 1 - y)]


def _all_gather(halved, whole):
    nh, nw = len(halved), len(whole)

    def body(*refs):
        src_h, src_w = refs[:nh], refs[nh:nh + nw]
        out_h, out_w = refs[nh + nw:2 * nh + nw], refs[2 * nh + nw:2 * (nh + nw)]
        send_h, recv_h, send_w, recv_w, local_sem = refs[2 * (nh + nw):]
        x, y, c = _coords()
        mine = 2 * x + y
        sibling = (x, y, 1 - c)
        chips = _other_chips(x, y)

        def copy_h(a, k, chip, half, to, src=None):
            dst = out_h[a].at[chip, half]
            return pltpu.make_async_remote_copy(src_ref=dst if src is None else src, dst_ref=dst,
                                                send_sem=send_h.at[a, k], recv_sem=recv_h.at[a, k],
                                                device_id=to, device_id_type=MESH)

        def copy_w(a, k, chip, to):
            return pltpu.make_async_remote_copy(src_ref=src_w[a], dst_ref=out_w[a].at[chip],
                                                send_sem=send_w.at[a, k], recv_sem=recv_w.at[a, k],
                                                device_id=to, device_id_type=MESH)

        local = [pltpu.make_async_copy(src_h[a], out_h[a].at[mine], local_sem.at[a]) for a in range(nh)]
        local += [pltpu.make_async_copy(src_w[a], out_w[a].at[mine], local_sem.at[nh + a]) for a in range(nw)]
        for cp in local:
            cp.start()
        first = [copy_h(a, k, mine, c, (*chip, c), src=src_h[a].at[c]) for a in range(nh) for k, chip in enumerate(chips)]
        first += [copy_w(a, k, mine, (*chip, c)) for a in range(nw) for k, chip in enumerate(chips)]
        for cp in first:
            cp.start()
        passed = []
        for a in range(nh):
            for k, (px, py) in enumerate(chips):
                copy_h(a, k, 2 * px + py, c, (x, y, c)).wait_recv()
                cp = copy_h(a, 3 + k, 2 * px + py, c, sibling)
                cp.start()
                passed.append(cp)
        for a in range(nh):
            for k, (px, py) in enumerate(chips):
                copy_h(a, 3 + k, 2 * px + py, 1 - c, (x, y, c)).wait_recv()
        for a in range(nw):
            for k, (px, py) in enumerate(chips):
                copy_w(a, k, 2 * px + py, (x, y, c)).wait_recv()
        for cp in first + passed:
            cp.wait_send()
        for cp in local:
            cp.wait()

    out_shape = [jax.ShapeDtypeStruct((4,) + a.shape, a.dtype) for a in list(halved) + list(whole)]
    return pl.pallas_call(
        body, name="gather_weights", in_specs=[ANY] * (nh + nw), out_specs=[ANY] * (nh + nw), out_shape=out_shape,
        scratch_shapes=[pltpu.SemaphoreType.DMA((nh, 6)), pltpu.SemaphoreType.DMA((nh, 6)),
                        pltpu.SemaphoreType.DMA((nw, 3)), pltpu.SemaphoreType.DMA((nw, 3)),
                        pltpu.SemaphoreType.DMA((nh + nw,))],
    )(*halved, *whole)


def _exchange(name, srcs, lands, copies, local_copies):
    ns, nl, n, nloc = len(srcs), len(lands), len(copies), len(local_copies)

    def body(*refs):
        src, land = refs[:ns], refs[ns:ns + nl]
        send, recv, local_sem = refs[ns + nl:]
        me = _coords()
        started = []
        for k, (si, s_at, li, l_at, ci) in enumerate(local_copies):
            cp = pltpu.make_async_copy(_at(src[si], s_at(*me)), _at(land[li], l_at(*me)), local_sem.at[k])
            cp.start()
            started.append(cp)
        remote = []
        for k, (si, s_at, li, l_at, peer) in enumerate(copies):
            cp = pltpu.make_async_remote_copy(src_ref=_at(src[si], s_at(*me)), dst_ref=_at(land[li], l_at(*me)),
                                              send_sem=send.at[k], recv_sem=recv.at[k],
                                              device_id=peer(*me), device_id_type=MESH)
            cp.start()
            remote.append(cp)
        for cp in remote:
            cp.wait()
        for cp in started:
            cp.wait()

    return pl.pallas_call(
        body, name=name, in_specs=[ANY] * ns, out_specs=[ANY] * nl, out_shape=list(lands),
        scratch_shapes=[pltpu.SemaphoreType.DMA((n,)), pltpu.SemaphoreType.DMA((n,)),
                        pltpu.SemaphoreType.DMA((max(nloc, 1),))],
    )(*srcs)


def _add_pairs(a, b, name):
    _, r, cols = a.shape
    tr = ROW_TILE if r % ROW_TILE == 0 else r

    def body(a_ref, b_ref, o_ref, ob_ref):
        s = a_ref[...] + b_ref[...]
        o_ref[...] = s
        ob_ref[...] = s.astype(BF16)

    spec = pl.BlockSpec((None, tr, cols), lambda j, i: (j, i, 0))
    return pl.pallas_call(
        body, name=name, grid=(4, r // tr), in_specs=[spec, spec], out_specs=[spec, spec],
        out_shape=[jax.ShapeDtypeStruct(a.shape, F32), jax.ShapeDtypeStruct(a.shape, BF16)],
        compiler_params=_params(2),
    )(a, b)


def _sum_chips(own, landed, mine, name):
    _, r, cols = landed.shape
    tr = ROW_TILE if r % ROW_TILE == 0 else r

    def body(mine_ref, own_ref, land_ref, o_ref):
        acc = None
        for j in range(4):
            term = jnp.where(mine_ref[0] == j, own_ref[...], land_ref[j].astype(F32))
            acc = term if acc is None else acc + term
        o_ref[...] = acc

    return pl.pallas_call(
        body, name=name,
        grid_spec=pltpu.PrefetchScalarGridSpec(
            num_scalar_prefetch=1, grid=(r // tr,),
            in_specs=[pl.BlockSpec((tr, cols), lambda i, m: (i, 0)), pl.BlockSpec((4, tr, cols), lambda i, m: (0, i, 0))],
            out_specs=pl.BlockSpec((tr, cols), lambda i, m: (i, 0))),
        out_shape=jax.ShapeDtypeStruct((r, cols), F32), compiler_params=_params(1),
    )(mine, own, landed)


def _sum_devices(landed, name):
    def body(l_ref, o_ref):
        acc = l_ref[0]
        for j in range(1, 8):
            acc = acc + l_ref[j]
        o_ref[...] = acc

    return pl.pallas_call(body, name=name, out_shape=jax.ShapeDtypeStruct(landed.shape[1:], F32))(landed)


def _reduce_gradients(big, small):
    nb = len(big)
    x, y, c = _coords()
    mine = 2 * x + y
    flips = [(fx, fy, fc) for fx in (0, 1) for fy in (0, 1) for fc in (0, 1) if fx or fy or fc]

    def flip(fx, fy, fc):
        return lambda x, y, c: (x ^ fx, y ^ fy, c ^ fc)

    copies = [(a, (lambda j: lambda x, y, c: (j, 1 - c))(j), a, (lambda j: lambda x, y, c: (j,))(j), flip(0, 0, 1))
              for a in range(nb) for j in range(4)]
    copies += [(nb, lambda x, y, c: (), nb, lambda x, y, c: (4 * x + 2 * y + c,), flip(*f)) for f in flips]
    lands = [jax.ShapeDtypeStruct((4,) + g.shape[2:], F32) for g in big] + [jax.ShapeDtypeStruct((8,) + small.shape, F32)]
    local = [(nb, lambda x, y, c: (), nb, lambda x, y, c: (4 * x + 2 * y + c,), None)]
    landed = _exchange("swap_halves", list(big) + [small], lands, copies, local)
    small_sum = _sum_devices(landed[nb], "sum_small")

    chip_f32, chip_bf16 = [], []
    for a in range(nb):
        kept = lax.dynamic_index_in_dim(big[a], c, axis=1, keepdims=False)
        s, sb = _add_pairs(kept, landed[a], f"add_cores_{a}")
        chip_f32.append(s)
        chip_bf16.append(sb)

    chip_flips = [(1, 0), (0, 1), (1, 1)]
    copies = [(a, (lambda f: lambda x, y, c: (2 * (x ^ f[0]) + (y ^ f[1]),))(f), a, lambda x, y, c: (2 * x + y,),
               flip(f[0], f[1], 0)) for a in range(nb) for f in chip_flips]
    local = [(a, lambda x, y, c: (2 * x + y,), a, lambda x, y, c: (2 * x + y,), None) for a in range(nb)]
    lands = [jax.ShapeDtypeStruct(g.shape, BF16) for g in chip_bf16]
    landed = _exchange("send_chip_sums", chip_bf16, lands, copies, local)
    mine_arr = jnp.reshape(mine, (1,)).astype(jnp.int32)
    totals = [_sum_chips(lax.dynamic_index_in_dim(chip_f32[a], mine, axis=0, keepdims=False), landed[a], mine_arr,
                         f"sum_chips_{a}") for a in range(nb)]

    copies = [(a, lambda x, y, c: (), a, lambda x, y, c: (c,), flip(0, 0, 1)) for a in range(nb)]
    local = [(a, lambda x, y, c: (), a, lambda x, y, c: (c,), None) for a in range(nb)]
    lands = [jax.ShapeDtypeStruct((2,) + t.shape, F32) for t in totals]
    return _exchange("swap_sums", totals, lands, copies, local), small_sum


def kernel(x, conv_norm_g, conv_w_in, conv_w, conv_w_out, attn_norm_g, attn_w_in, attn_b_f, attn_q_norm_g, attn_k_norm_g, attn_w_out, loss_target, m_conv_norm_g, m_conv_w_in, m_conv_w, m_conv_w_out, m_attn_norm_g, m_attn_w_in, m_attn_b_f, m_attn_q_norm_g, m_attn_k_norm_g, m_attn_w_out, v_conv_norm_g, v_conv_w_in, v_conv_w, v_conv_w_out, v_attn_norm_g, v_attn_w_in, v_attn_b_f, v_attn_q_norm_g, v_attn_k_norm_g, v_attn_w_out):
    xi, yi, _ = _coords()
    chip = 2 * xi + yi
    D = x.shape[2]
    H = D // HEAD_DIM
    names = ["conv_norm_g", "conv_w_in", "conv_w", "conv_w_out", "attn_norm_g", "attn_w_in", "attn_b_f",
             "attn_q_norm_g", "attn_k_norm_g", "attn_w_out"]
    weights = dict(zip(names, [conv_norm_g, conv_w_in, conv_w, conv_w_out, attn_norm_g, attn_w_in, attn_b_f,
                               attn_q_norm_g, attn_k_norm_g, attn_w_out]))
    m_in = dict(zip(names, [m_conv_norm_g, m_conv_w_in, m_conv_w, m_conv_w_out, m_attn_norm_g, m_attn_w_in,
                            m_attn_b_f, m_attn_q_norm_g, m_attn_k_norm_g, m_attn_w_out]))
    v_in = dict(zip(names, [v_conv_norm_g, v_conv_w_in, v_conv_w, v_conv_w_out, v_attn_norm_g, v_attn_w_in,
                            v_attn_b_f, v_attn_q_norm_g, v_attn_k_norm_g, v_attn_w_out]))
    weights = {k: w[0] for k, w in weights.items()}
    m_in = {k: w[0] for k, w in m_in.items()}
    v_in = {k: w[0] for k, w in v_in.items()}

    big_names = ["conv_w_in", "attn_w_in", "conv_w_out", "attn_w_out"]
    halved = [weights[k].astype(BF16).reshape(2, weights[k].shape[0] // 2, weights[k].shape[1]) for k in big_names]
    q = D // 4
    small_w = jnp.concatenate([weights["conv_w"], weights["attn_norm_g"][None, :], jnp.zeros((4, q), F32)], axis=0)
    g_in, ga_in, g_out, ga_out, g_small = _all_gather(halved, [small_w])
    w_in = g_in.reshape(4, D, D).transpose(1, 0, 2).reshape(D, 4 * D)
    wa_in = ga_in.reshape(4, D, D + H // 4).transpose(1, 0, 2).reshape(D, 4 * D + H)
    w_out = g_out.reshape(D, D)
    wa_out = ga_out.reshape(D, D)
    conv_w_full = g_small[:, 0:3, :].transpose(1, 0, 2).reshape(3, D)
    attn_g_full = g_small[:, 3, :].reshape(1, D)

    loss_part, grad_x, grads = _local_step(x[0], loss_target[0], weights["conv_norm_g"][None, :], w_in, conv_w_full,
                                           w_out, attn_g_full, wa_in, weights["attn_b_f"][None, :],
                                           weights["attn_q_norm_g"][None, :], weights["attn_k_norm_g"][None, :], wa_out)
    loss = lax.psum(loss_part, ("x", "y", "c"))

    big = [grads["conv_w_in"].reshape(D, 4, D).transpose(1, 0, 2).reshape(4, 2, D // 2, D),
           grads["attn_w_in"].reshape(D, 4, D + H // 4).transpose(1, 0, 2).reshape(4, 2, D // 2, D + H // 4),
           grads["conv_w_out"].reshape(4, 2, D // 8, D), grads["attn_w_out"].reshape(4, 2, D // 8, D)]
    tail = jnp.concatenate([grads["attn_b_f"], grads["attn_q_norm_g"], grads["attn_k_norm_g"],
                            jnp.zeros((1, D - H - 2 * HEAD_DIM), F32)], axis=1)
    small = jnp.concatenate([grads["conv_norm_g"], grads["conv_w"], grads["attn_norm_g"], tail,
                             jnp.zeros((2, D), F32)], axis=0)
    reduced, small_sum = _reduce_gradients(big, small)
    final = {k: r.reshape(weights[k].shape) for k, r in zip(big_names, reduced)}
    final["conv_norm_g"] = small_sum[0]
    final["conv_w"] = lax.dynamic_slice_in_dim(small_sum[1:4], chip * q, q, axis=1)
    final["attn_norm_g"] = lax.dynamic_slice_in_dim(small_sum[4], chip * q, q, axis=0)
    final["attn_b_f"] = small_sum[5, :H]
    final["attn_q_norm_g"] = small_sum[5, H:H + HEAD_DIM]
    final["attn_k_norm_g"] = small_sum[5, H + HEAD_DIM:H + 2 * HEAD_DIM]

    delta, new_m, new_v = {}, {}, {}
    for k in names:
        shape = weights[k].shape
        as2d = (lambda a: a.reshape(1, -1)) if len(shape) == 1 else (lambda a: a)
        d, m2, v2 = _adamw(as2d(weights[k]), as2d(final[k]), as2d(m_in[k]), as2d(v_in[k]), "adamw_" + k)
        delta[k], new_m[k], new_v[k] = d.reshape(shape), m2.reshape(shape), v2.reshape(shape)
    lead = lambda a: a[None]
    return (loss, grad_x[None], *[lead(final[k]) for k in names], *[lead(delta[k]) for k in names],
            *[lead(new_m[k]) for k in names], *[lead(new_v[k]) for k in names])
```

```python
import functools

import jax
import jax.numpy as jnp
from jax import lax
from jax.experimental import pallas as pl
from jax.experimental.pallas import tpu as pltpu

F32 = jnp.float32
BF16 = jnp.bfloat16
HEAD_DIM = 64
LANES = 128
AUG_A = 64
AUG_B = 67
RMS_EPS = 1e-6
NEG = -1e30
Q_SCALE = 0.125
ROW_TILE = 256
ATT_GROUP = 4
SKIP_LOG = 106.0
PLAIN_EXP_MAX = 60.0
TN_ROWS = 512
VMEM_LIMIT = 56 << 20
ADAM_LR, ADAM_B1, ADAM_B2, ADAM_EPS, ADAM_WD, ADAM_STEP = 0.001, 0.9, 0.999, 1e-08, 0.01, 10
MESH = pl.DeviceIdType.MESH
ANY = pl.BlockSpec(memory_space=pl.ANY)


def _lane():
    return lax.broadcasted_iota(jnp.int32, (1, LANES), 1)


def _split3(x):
    hi = x.astype(BF16).astype(F32)
    r = x - hi
    mid = r.astype(BF16).astype(F32)
    lo = (r - mid).astype(BF16).astype(F32)
    return hi, mid, lo


def _put(base, lane, start, parts):
    for j, p in enumerate(parts):
        base = jnp.where(lane == start + j, p, base)
    return base


def _col(x, lane, idx):
    return jnp.sum(jnp.where(lane == idx, x, 0.0), axis=1, keepdims=True)


def _head_tile(ref, hd, lane):
    j = hd // 2
    t = ref[:, LANES * j:LANES * (j + 1)]
    if hd % 2:
        t = pltpu.roll(t, HEAD_DIM, 1)
    return jnp.where(lane < HEAD_DIM, t, 0.0)


def _pair_tile(even, odd):
    return even + pltpu.roll(odd, HEAD_DIM, 1)


def _sigmoid(x):
    return 1.0 / (1.0 + jnp.exp(-x))


def _dot(a, b):
    return jnp.dot(a, b, preferred_element_type=F32)


def _dot_nt(a, b):
    return lax.dot_general(a, b, (((1,), (1,)), ((), ())), preferred_element_type=F32)


def _dot_tn(a, b):
    return lax.dot_general(a, b, (((0,), (0,)), ((), ())), preferred_element_type=F32)


def _dot01(tri, x):
    hi, mid, lo = _split3(x)
    return _dot(tri, hi.astype(BF16)) + _dot(tri, mid.astype(BF16)) + _dot(tri, lo.astype(BF16))


def _rms_bwd(dh, x, g):
    inv = lax.rsqrt(jnp.mean(x * x, axis=-1, keepdims=True) + RMS_EPS)
    xh = x * inv
    dxn = dh * g
    dx = inv * (dxn - xh * jnp.mean(dxn * xh, axis=-1, keepdims=True))
    return dx, jnp.sum(dh * xh, axis=0, keepdims=True)


def _head_rms_bwd(dn, t, g):
    inv = lax.rsqrt(jnp.sum(t * t, axis=1, keepdims=True) * (1.0 / HEAD_DIM) + RMS_EPS)
    th = t * inv
    gd = dn * g
    d = inv * (gd - th * (jnp.sum(gd * th, axis=1, keepdims=True) * (1.0 / HEAD_DIM)))
    return d, jnp.sum(dn * th, axis=0, keepdims=True)


def _params(n_grid):
    return pltpu.CompilerParams(dimension_semantics=("arbitrary",) * n_grid, vmem_limit_bytes=VMEM_LIMIT)


def _rows(tm, cols, rev=None):
    if rev is None:
        return pl.BlockSpec((tm, cols), lambda i: (i, 0))
    return pl.BlockSpec((tm, cols), lambda i: (rev - i, 0))


def _whole(shape):
    return pl.BlockSpec(shape, lambda *_: (0,) * len(shape))


def _conv_fwd(x, g1, w_in, conv_w, w_out):
    S, D = x.shape
    tm = min(ROW_TILE, S)

    def body(x_ref, g_ref, win_ref, cw_ref, wout_ref, proj_ref, h_ref, yc_ref, y_ref, x1_ref, prev_u):
        i = pl.program_id(0)
        xv = x_ref[...]
        inv = lax.rsqrt(jnp.mean(xv * xv, axis=-1, keepdims=True) + RMS_EPS)
        h = (xv * inv * g_ref[...]).astype(BF16)
        h_ref[...] = h
        for j in range(4):
            proj_ref[:, j * D:(j + 1) * D] = _dot(h, win_ref[:, j * D:(j + 1) * D])

        @pl.when(i == 0)
        def _():
            prev_u[...] = jnp.zeros((tm, D), F32)

        u = proj_ref[:, D:2 * D] * proj_ref[:, 2 * D:3 * D]
        pu = prev_u[...]
        row = lax.broadcasted_iota(jnp.int32, (tm, 1), 0)
        u1 = jnp.where(row < 1, pltpu.roll(pu, 1, 0), pltpu.roll(u, 1, 0))
        u2 = jnp.where(row < 2, pltpu.roll(pu, 2, 0), pltpu.roll(u, 2, 0))
        prev_u[...] = u
        w = cw_ref[...]
        yc = w[2:3] * u + w[1:2] * u1 + w[0:1] * u2
        yc_ref[...] = yc
        z = proj_ref[:, 3 * D:4 * D]
        y = (proj_ref[:, 0:D] * yc * (z * _sigmoid(z))).astype(BF16)
        y_ref[...] = y
        x1_ref[...] = xv + _dot(y, wout_ref[...])

    return pl.pallas_call(
        body, name="conv_fwd", grid=(S // tm,),
        in_specs=[_rows(tm, D), _whole((1, D)), _whole((D, 4 * D)), _whole((3, D)), _whole((D, D))],
        out_specs=[_rows(tm, 4 * D), _rows(tm, D), _rows(tm, D), _rows(tm, D), _rows(tm, D)],
        out_shape=[jax.ShapeDtypeStruct((S, 4 * D), F32), jax.ShapeDtypeStruct((S, D), BF16),
                   jax.ShapeDtypeStruct((S, D), F32), jax.ShapeDtypeStruct((S, D), BF16),
                   jax.ShapeDtypeStruct((S, D), F32)],
        scratch_shapes=[pltpu.VMEM((tm, D), F32)],
        compiler_params=_params(1),
    )(x, g1, w_in, conv_w, w_out)


def _conv_bwd(dx1, x, g1, w_in, w_out, conv_w, proj, yc):
    S, D = x.shape
    tm = min(ROW_TILE, S)
    last = S // tm - 1

    def body(dx1_ref, x_ref, g_ref, win_ref, wout_ref, cw_ref, proj_ref, yc_ref,
             dproj_ref, dx_ref, dx1b_ref, dg_ref, dcw_ref, next_d):
        @pl.when(pl.program_id(0) == 0)
        def _():
            dg_ref[...] = jnp.zeros((1, D), F32)
            dcw_ref[...] = jnp.zeros((3, D), F32)
            next_d[...] = jnp.zeros((tm, D), F32)

        dx1v = dx1_ref[...]
        dx1b = dx1v.astype(BF16)
        dx1b_ref[...] = dx1b
        dy = _dot_nt(dx1b, wout_ref[...])
        b = proj_ref[:, 0:D]
        c = proj_ref[:, D:2 * D]
        xin = proj_ref[:, 2 * D:3 * D]
        z = proj_ref[:, 3 * D:4 * D]
        sg = _sigmoid(z)
        sz = z * sg
        ycv = yc_ref[...]
        d0 = dy * b * sz
        dproj_ref[:, 0:D] = (dy * ycv * sz).astype(BF16)
        dproj_ref[:, 3 * D:4 * D] = (dy * b * ycv * (sg * (1.0 + z * (1.0 - sg)))).astype(BF16)
        nd = next_d[...]
        row = lax.broadcasted_iota(jnp.int32, (tm, 1), 0)
        d1 = jnp.where(row >= tm - 1, pltpu.roll(nd, tm - 1, 0), pltpu.roll(d0, tm - 1, 0))
        d2 = jnp.where(row >= tm - 2, pltpu.roll(nd, tm - 2, 0), pltpu.roll(d0, tm - 2, 0))
        next_d[...] = d0
        w = cw_ref[...]
        du = w[2:3] * d0 + w[1:2] * d1 + w[0:1] * d2
        u = c * xin
        dcw_ref[2:3, :] += jnp.sum(d0 * u, axis=0, keepdims=True)
        dcw_ref[1:2, :] += jnp.sum(d1 * u, axis=0, keepdims=True)
        dcw_ref[0:1, :] += jnp.sum(d2 * u, axis=0, keepdims=True)
        dproj_ref[:, D:2 * D] = (du * xin).astype(BF16)
        dproj_ref[:, 2 * D:3 * D] = (du * c).astype(BF16)
        dh = _dot_nt(dproj_ref[...], win_ref[...])
        dxn, dg = _rms_bwd(dh, x_ref[...], g_ref[...])
        dx_ref[...] = dx1v + dxn
        dg_ref[...] += dg

    return pl.pallas_call(
        body, name="conv_bwd", grid=(S // tm,),
        in_specs=[_rows(tm, D, last), _rows(tm, D, last), _whole((1, D)), _whole((D, 4 * D)), _whole((D, D)),
                  _whole((3, D)), _rows(tm, 4 * D, last), _rows(tm, D, last)],
        out_specs=[_rows(tm, 4 * D, last), _rows(tm, D, last), _rows(tm, D, last), _whole((1, D)), _whole((3, D))],
        out_shape=[jax.ShapeDtypeStruct((S, 4 * D), BF16), jax.ShapeDtypeStruct((S, D), F32),
                   jax.ShapeDtypeStruct((S, D), BF16), jax.ShapeDtypeStruct((1, D), F32),
                   jax.ShapeDtypeStruct((3, D), F32)],
        scratch_shapes=[pltpu.VMEM((tm, D), F32)],
        compiler_params=_params(1),
    )(dx1, x, g1, w_in, w_out, conv_w, proj, yc)


def _attn_front(x1, g2, w, wf, bf, gq, gk):
    S, D = x1.shape
    H = D // HEAD_DIM
    tm = min(ROW_TILE, S)
    tri = (lax.broadcasted_iota(jnp.int32, (tm, tm), 1) <= lax.broadcasted_iota(jnp.int32, (tm, tm), 0)).astype(BF16)

    def body(x_ref, g_ref, w_ref, wf_ref, bf_ref, gq_ref, gk_ref, tri_ref,
             h_ref, qraw_ref, kraw_ref, z_ref, f_ref, c_ref, qa_ref, ka_ref, va_ref, vt_ref, carry, v_s):
        @pl.when(pl.program_id(0) == 0)
        def _():
            carry[...] = jnp.zeros((8, LANES), F32)

        xv = x_ref[...]
        inv = lax.rsqrt(jnp.mean(xv * xv, axis=-1, keepdims=True) + RMS_EPS)
        h = (xv * inv * g_ref[...]).astype(BF16)
        h_ref[...] = h
        qraw_ref[...] = _dot(h, w_ref[:, 0:D])
        kraw_ref[...] = _dot(h, w_ref[:, D:2 * D])
        v_s[...] = _dot(h, w_ref[:, 2 * D:3 * D])
        z_ref[...] = _dot(h, w_ref[:, 3 * D:4 * D])
        lane = _lane()
        f = _dot(h, wf_ref[...]) + bf_ref[...]
        f_ref[...] = f
        logf = jnp.where(lane < H, jnp.minimum(f, 0.0) - jnp.log(1.0 + jnp.exp(-jnp.abs(f))), 0.0)
        cs = _dot01(tri_ref[...], logf) + carry[0:1, :]
        c_ref[...] = cs
        carry[...] = jnp.broadcast_to(cs[tm - 1:tm, :], (8, LANES))
        one_a = (lane >= AUG_A) & (lane < AUG_A + 3)
        one_b = (lane >= AUG_B) & (lane < AUG_B + 3)
        for hd in range(H):
            sl = slice(LANES * hd, LANES * (hd + 1))
            ch = _col(cs, lane, hd)
            qt = _head_tile(qraw_ref, hd, lane)
            qn = qt * lax.rsqrt(jnp.sum(qt * qt, axis=1, keepdims=True) * (1.0 / HEAD_DIM) + RMS_EPS) * gq_ref[...]
            kt = _head_tile(kraw_ref, hd, lane)
            kn = kt * lax.rsqrt(jnp.sum(kt * kt, axis=1, keepdims=True) * (1.0 / HEAD_DIM) + RMS_EPS) * gk_ref[...]
            diag = jnp.sum(qn * kn, axis=1, keepdims=True) * Q_SCALE
            qa = _put(jnp.where(one_b, 1.0, qn * Q_SCALE), lane, AUG_A, _split3(ch - diag))
            qa_ref[:, sl] = qa.astype(BF16)
            ka = _put(jnp.where(one_a, 1.0, kn), lane, AUG_B, _split3(-ch))
            ka_ref[:, sl] = ka.astype(BF16)
            va = jnp.where(one_a, 1.0, _head_tile(v_s, hd, lane))
            va_ref[:, sl] = va.astype(BF16)
            vt_ref[hd] = va.T.astype(BF16)

    nb = S // tm
    return pl.pallas_call(
        body, name="attn_front", grid=(nb,),
        in_specs=[_rows(tm, D), _whole((1, D)), _whole((D, 4 * D)), _whole((D, LANES)), _whole((1, LANES)),
                  _whole((1, LANES)), _whole((1, LANES)), _whole((tm, tm))],
        out_specs=[_rows(tm, D), _rows(tm, D), _rows(tm, D), _rows(tm, D), _rows(tm, LANES), _rows(tm, LANES),
                   _rows(tm, H * LANES), _rows(tm, H * LANES), _rows(tm, H * LANES),
                   pl.BlockSpec((H, None, LANES, tm), lambda i: (0, i, 0, 0))],
        out_shape=[jax.ShapeDtypeStruct((S, D), BF16), jax.ShapeDtypeStruct((S, D), F32),
                   jax.ShapeDtypeStruct((S, D), F32), jax.ShapeDtypeStruct((S, D), F32),
                   jax.ShapeDtypeStruct((S, LANES), F32), jax.ShapeDtypeStruct((S, LANES), F32),
                   jax.ShapeDtypeStruct((S, H * LANES), BF16), jax.ShapeDtypeStruct((S, H * LANES), BF16),
                   jax.ShapeDtypeStruct((S, H * LANES), BF16), jax.ShapeDtypeStruct((H, nb, LANES, tm), BF16)],
        scratch_shapes=[pltpu.VMEM((8, LANES), F32), pltpu.VMEM((tm, D), F32)],
        compiler_params=_params(1),
    )(x1, g2, w, wf, bf, gq, gk, tri)


def _skip_tables(c, gq, gk, T, G):
    nb = c.shape[0] // T
    bound = 8.0 * jnp.max(jnp.abs(gq)) * jnp.max(jnp.abs(gk))
    first, last = c[0::T, :], c[T - 1::T, :]
    idx = jnp.arange(nb)
    need = (last[None, :, :] <= first[:, None, :] + (SKIP_LOG + 2.0 * bound)) & (idx[None, :, None] < idx[:, None, None])
    need = need | (idx[None, :, None] == idx[:, None, None])
    kstart = jnp.argmax(need, axis=1)
    qend = nb - 1 - jnp.argmax(need[::-1], axis=0)
    kstart = jnp.min(kstart.reshape(nb // G, G, -1), axis=1)
    qend = jnp.max(qend.reshape(nb // G, G, -1), axis=1)
    return kstart.T.astype(jnp.int32), qend.T.astype(jnp.int32), bound


def _attn_fwd(kstart, qa, ka, vt, online_max):
    S = qa.shape[0]
    H = qa.shape[1] // LANES
    nb, T = vt.shape[1], vt.shape[3]
    G = nb // kstart.shape[1]
    W = G * T

    def finish(acc, shift, o_ref):
        l = acc[AUG_A:AUG_A + 1, :]
        feat = lax.broadcasted_iota(jnp.int32, (LANES, 1), 0)
        o_ref[...] = jnp.where(feat == AUG_A + 1, shift + jnp.log(l), acc / l).T

    def causal(st):
        return jnp.where(lax.broadcasted_iota(jnp.int32, st.shape, 0) <= lax.broadcasted_iota(jnp.int32, st.shape, 1),
                         st, NEG)

    def fast_body(ks_ref, q_ref, k_ref, vt_ref, o_ref, acc_ref, sa_ref, sb_ref):
        h, g = pl.program_id(0), pl.program_id(1)
        q = q_ref[...]
        acc_ref[...] = jnp.zeros((LANES, W), F32)

        def scores(ki, lo):
            return _dot_nt(k_ref[pl.ds(pl.multiple_of(ki * T, T), 2 * T), :], q[lo * T:, :])

        def weighted(ki, p):
            return _dot(vt_ref[ki], p[:T]) + _dot(vt_ref[ki + 1], p[T:])

        first = ks_ref[h, g]
        first = first - ((g * G - first) & 1)
        steps = (g * G - first) // 2
        sa_ref[...] = scores(first, 0)

        def advance(ki, cur_ref, next_ref):
            p = jnp.exp(cur_ref[...]).astype(BF16)
            next_ref[...] = scores(ki + 2, 0)
            acc_ref[...] += weighted(ki, p)

        def loop(i, carry):
            advance(first + 4 * i, sa_ref, sb_ref)
            advance(first + 4 * i + 2, sb_ref, sa_ref)
            return carry

        lax.fori_loop(0, steps // 2, loop, 0)

        @pl.when(steps % 2 == 1)
        def _():
            advance(g * G - 2, sa_ref, sb_ref)
            acc_ref[...] += weighted(g * G, jnp.exp(causal(sb_ref[...])).astype(BF16))

        @pl.when(steps % 2 == 0)
        def _():
            acc_ref[...] += weighted(g * G, jnp.exp(causal(sa_ref[...])).astype(BF16))

        for j in range(2, G, 2):
            p = jnp.exp(causal(scores(g * G + j, j))).astype(BF16)
            acc_ref[:, j * T:] += weighted(g * G + j, p)
        finish(acc_ref[...], 0.0, o_ref)

    def online_body(ks_ref, q_ref, k_ref, vt_ref, o_ref, acc_ref, m_ref):
        h, g = pl.program_id(0), pl.program_id(1)
        q = q_ref[...]
        m_ref[...] = jnp.full((8, W), NEG, F32)
        acc_ref[...] = jnp.zeros((LANES, W), F32)

        def update(st, vtb, lo):
            m_old = m_ref[0:1, lo:]
            m_new = jnp.maximum(m_old, jnp.max(st, axis=0, keepdims=True))
            p = jnp.exp(st - m_new).astype(BF16)
            acc_ref[:, lo:] = acc_ref[:, lo:] * jnp.exp(m_old - m_new) + _dot(vtb, p)
            m_ref[:, lo:] = jnp.broadcast_to(m_new, (8, W - lo))

        def loop(ki, carry):
            kb = k_ref[pl.ds(pl.multiple_of(ki * T, T), T), :]
            update(_dot_nt(kb, q), vt_ref[ki], 0)
            return carry

        lax.fori_loop(ks_ref[h, g], g * G, loop, 0)
        for j in range(G):
            ki = g * G + j
            kb = k_ref[pl.ds(pl.multiple_of(ki * T, T), T), :]
            update(causal(_dot_nt(kb, q[j * T:, :])), vt_ref[ki], j * T)
        finish(acc_ref[...], m_ref[0:1, :], o_ref)

    return pl.pallas_call(
        online_body if online_max else fast_body, name="attn_fwd_online" if online_max else "attn_fwd",
        grid_spec=pltpu.PrefetchScalarGridSpec(
            num_scalar_prefetch=1, grid=(H, nb // G),
            in_specs=[pl.BlockSpec((W, LANES), lambda h, i, ks: (i, h)),
                      pl.BlockSpec((S, LANES), lambda h, i, ks: (0, h)),
                      pl.BlockSpec((None, nb, LANES, T), lambda h, i, ks: (h, 0, 0, 0))],
            out_specs=pl.BlockSpec((W, LANES), lambda h, i, ks: (i, h)),
            scratch_shapes=[pltpu.VMEM((LANES, W), F32)] + (
                [pltpu.VMEM((8, W), F32)] if online_max else [pltpu.VMEM((2 * T, W), F32)] * 2)),
        out_shape=jax.ShapeDtypeStruct((S, H * LANES), F32),
        compiler_params=_params(2),
    )(kstart, qa, ka, vt)


def _attn_out(o_aug, z, x1, target, w_out, qa):
    S, D = x1.shape
    H = D // HEAD_DIM
    tm = min(ROW_TILE, S)

    def body(o_ref, z_ref, x1_ref, t_ref, w_ref, q_ref,
             dx2_ref, dx2b_ref, o2b_ref, dz_ref, doa_ref, qa2_ref, loss_ref, oc_s, do_s):
        @pl.when(pl.program_id(0) == 0)
        def _():
            loss_ref[...] = jnp.zeros((1, LANES), F32)

        lane = _lane()
        lses = []
        for j in range(H // 2):
            oe = o_ref[:, 2 * LANES * j:2 * LANES * j + LANES]
            oo = o_ref[:, 2 * LANES * j + LANES:2 * LANES * (j + 1)]
            lses += [_col(oe, lane, AUG_A + 1), _col(oo, lane, AUG_A + 1)]
            oc_s[:, LANES * j:LANES * (j + 1)] = _pair_tile(jnp.where(lane < HEAD_DIM, oe, 0.0),
                                                            jnp.where(lane < HEAD_DIM, oo, 0.0))
        oc = oc_s[...]
        zv = z_ref[...]
        sg = _sigmoid(zv)
        sz = zv * sg
        o2 = (oc * sz).astype(BF16)
        o2b_ref[...] = o2
        e = x1_ref[...] + _dot(o2, w_ref[...]) - t_ref[...]
        sq = jnp.sum(jnp.sum(e * e, axis=1, keepdims=True), axis=0, keepdims=True)
        loss_ref[...] += jnp.broadcast_to(sq * (0.5 / D), (1, LANES))
        dx2 = e * (1.0 / D)
        dx2_ref[...] = dx2
        dx2b = dx2.astype(BF16)
        dx2b_ref[...] = dx2b
        do2 = _dot_nt(dx2b, w_ref[...])
        dz_ref[...] = do2 * oc * (sg * (1.0 + zv * (1.0 - sg)))
        do_s[...] = do2 * sz
        one_a = (lane >= AUG_A) & (lane < AUG_A + 3)
        for hd in range(H):
            sl = slice(LANES * hd, LANES * (hd + 1))
            dt = _head_tile(do_s, hd, lane)
            delta = jnp.sum(dt * _head_tile(oc_s, hd, lane), axis=1, keepdims=True)
            doa_ref[:, sl] = _put(dt, lane, AUG_A, _split3(-delta)).astype(BF16)
            qt = q_ref[:, sl].astype(F32)
            rq = jnp.sum(jnp.where(one_a, qt, 0.0), axis=1, keepdims=True) - lses[hd]
            qa2_ref[:, sl] = _put(qt, lane, AUG_A, _split3(rq)).astype(BF16)

    return pl.pallas_call(
        body, name="attn_out", grid=(S // tm,),
        in_specs=[_rows(tm, H * LANES), _rows(tm, D), _rows(tm, D), _rows(tm, D), _whole((D, D)),
                  _rows(tm, H * LANES)],
        out_specs=[_rows(tm, D), _rows(tm, D), _rows(tm, D), _rows(tm, D), _rows(tm, H * LANES),
                   _rows(tm, H * LANES), _whole((1, LANES))],
        out_shape=[jax.ShapeDtypeStruct((S, D), F32), jax.ShapeDtypeStruct((S, D), BF16),
                   jax.ShapeDtypeStruct((S, D), BF16), jax.ShapeDtypeStruct((S, D), F32),
                   jax.ShapeDtypeStruct((S, H * LANES), BF16), jax.ShapeDtypeStruct((S, H * LANES), BF16),
                   jax.ShapeDtypeStruct((1, LANES), F32)],
        scratch_shapes=[pltpu.VMEM((tm, D), F32), pltpu.VMEM((tm, D), F32)],
        compiler_params=_params(1),
    )(o_aug, z, x1, target, w_out, qa)


def _attn_bwd(qend, qa2, doa, ka, va, T):
    S = qa2.shape[0]
    H = qa2.shape[1] // LANES
    nb = S // T
    G = nb // qend.shape[1]
    W = G * T

    def body(qe_ref, q_ref, do_ref, k_ref, v_ref, dq_ref, dk_ref, dv_ref, dk_acc, dv_acc):
        h, g = pl.program_id(0), pl.program_id(1)

        @pl.when(g == 0)
        def _():
            dq_ref[...] = jnp.zeros((nb, LANES, T), F32)

        kb = k_ref[...]
        vb = v_ref[...]
        kt = kb.astype(F32).T.astype(BF16)
        dk_acc[...] = jnp.zeros((W, LANES), F32)
        dv_acc[...] = jnp.zeros((W, LANES), F32)

        def step(qi, rows, masked):
            q0 = pl.multiple_of(qi * T, T)
            qb = q_ref[pl.ds(q0, 2 * T), :]
            dob = do_ref[pl.ds(q0, 2 * T), :]
            st = _dot_nt(kb[:rows], qb)
            if masked:
                key = lax.broadcasted_iota(jnp.int32, st.shape, 0)
                st = jnp.where(key <= lax.broadcasted_iota(jnp.int32, st.shape, 1) + (rows - 2 * T), st, NEG)
            p = jnp.exp(st)
            ds = (p * _dot_nt(vb[:rows], dob)).astype(BF16)
            dv_acc[0:rows, :] += _dot(p.astype(BF16), dob)
            dk_acc[0:rows, :] += _dot(ds, qb)
            dqt = _dot(kt[:, :rows], ds)
            dq_ref[qi] += dqt[:, :T]
            dq_ref[qi + 1] += dqt[:, T:]

        def loop(i, carry):
            step(g * G + G + 2 * i, W, False)
            return carry

        for j in range(0, G, 2):
            step(g * G + j, (j + 2) * T, True)
        lax.fori_loop(0, (qe_ref[h, g] - (g * G + G) + 2) // 2, loop, 0)
        dk_ref[...] = dk_acc[...]
        dv_ref[...] = dv_acc[...]

    return pl.pallas_call(
        body, name="attn_bwd",
        grid_spec=pltpu.PrefetchScalarGridSpec(
            num_scalar_prefetch=1, grid=(H, nb // G),
            in_specs=[pl.BlockSpec((S, LANES), lambda h, i, qe: (0, h)), pl.BlockSpec((S, LANES), lambda h, i, qe: (0, h)),
                      pl.BlockSpec((W, LANES), lambda h, i, qe: (i, h)), pl.BlockSpec((W, LANES), lambda h, i, qe: (i, h))],
            out_specs=[pl.BlockSpec((None, nb, LANES, T), lambda h, i, qe: (h, 0, 0, 0)),
                       pl.BlockSpec((W, LANES), lambda h, i, qe: (i, h)), pl.BlockSpec((W, LANES), lambda h, i, qe: (i, h))],
            scratch_shapes=[pltpu.VMEM((W, LANES), F32), pltpu.VMEM((W, LANES), F32)]),
        out_shape=[jax.ShapeDtypeStruct((H, nb, LANES, T), F32), jax.ShapeDtypeStruct((S, H * LANES), F32),
                   jax.ShapeDtypeStruct((S, H * LANES), F32)],
        compiler_params=_params(2),
    )(qend, qa2, doa, ka, va)


def _attn_proj_bwd(dqt, dka, dva, qraw, kraw, dz, f, gq, gk, w, wf, x1, g2, dx2):
    S, D = x1.shape
    H = D // HEAD_DIM
    tm = dqt.shape[3]
    last = S // tm - 1
    tri = (lax.broadcasted_iota(jnp.int32, (tm, tm), 1) >= lax.broadcasted_iota(jnp.int32, (tm, tm), 0)).astype(BF16)

    def body(dqt_ref, dk_ref, dv_ref, q_ref, k_ref, dz_ref, f_ref, gq_ref, gk_ref, w_ref, wf_ref, x1_ref, g2_ref,
             dx2_ref, tri_ref, dproj_ref, dx1_ref, dg2_ref, small_ref, carry):
        @pl.when(pl.program_id(0) == 0)
        def _():
            dg2_ref[...] = jnp.zeros((1, D), F32)
            small_ref[...] = jnp.zeros((8, LANES), F32)
            carry[...] = jnp.zeros((8, LANES), F32)

        lane = _lane()
        dcs = jnp.zeros((tm, LANES), F32)
        dgq = jnp.zeros((1, LANES), F32)
        dgk = jnp.zeros((1, LANES), F32)
        for j in range(H // 2):
            dq2, dk2, dv2 = [], [], []
            for hd in (2 * j, 2 * j + 1):
                sl = slice(LANES * hd, LANES * (hd + 1))
                dqf = dqt_ref[hd].T
                dqn = jnp.where(lane < HEAD_DIM, dqf * Q_SCALE, 0.0)
                d, dg = _head_rms_bwd(dqn, _head_tile(q_ref, hd, lane), gq_ref[...])
                dq2.append(d)
                dgq = dgq + dg
                dkt = dk_ref[:, sl]
                dcs = dcs + jnp.where(lane == hd, _col(dqf, lane, AUG_A) - _col(dkt, lane, AUG_B), 0.0)
                d, dg = _head_rms_bwd(jnp.where(lane < HEAD_DIM, dkt, 0.0), _head_tile(k_ref, hd, lane), gk_ref[...])
                dk2.append(d)
                dgk = dgk + dg
                dv2.append(jnp.where(lane < HEAD_DIM, dv_ref[:, sl], 0.0))
            for part, pair in enumerate((dq2, dk2, dv2)):
                dproj_ref[:, part * D + LANES * j:part * D + LANES * (j + 1)] = _pair_tile(*pair).astype(BF16)
        dproj_ref[:, 3 * D:4 * D] = dz_ref[...].astype(BF16)
        dlogf = _dot01(tri_ref[...], dcs) + carry[0:1, :]
        carry[...] = jnp.broadcast_to(dlogf[0:1, :], (8, LANES))
        df = dlogf * (1.0 / (1.0 + jnp.exp(f_ref[...])))
        dproj_ref[:, 4 * D:4 * D + LANES] = df.astype(BF16)
        small_ref[0:1, :] += jnp.sum(df, axis=0, keepdims=True)
        small_ref[1:2, :] += dgq
        small_ref[2:3, :] += dgk
        dh = _dot_nt(dproj_ref[:, 0:4 * D], w_ref[...]) + _dot_nt(dproj_ref[:, 4 * D:4 * D + LANES], wf_ref[...])
        dxn, dg = _rms_bwd(dh, x1_ref[...], g2_ref[...])
        dx1_ref[...] = dx2_ref[...] + dxn
        dg2_ref[...] += dg

    W = 4 * D + LANES
    return pl.pallas_call(
        body, name="attn_proj_bwd", grid=(S // tm,),
        in_specs=[pl.BlockSpec((H, None, LANES, tm), lambda i: (0, last - i, 0, 0)),
                  _rows(tm, H * LANES, last), _rows(tm, H * LANES, last), _rows(tm, D, last), _rows(tm, D, last),
                  _rows(tm, D, last), _rows(tm, LANES, last), _whole((1, LANES)), _whole((1, LANES)),
                  _whole((D, 4 * D)), _whole((D, LANES)), _rows(tm, D, last), _whole((1, D)), _rows(tm, D, last),
                  _whole((tm, tm))],
        out_specs=[_rows(tm, W, last), _rows(tm, D, last), _whole((1, D)), _whole((8, LANES))],
        out_shape=[jax.ShapeDtypeStruct((S, W), BF16), jax.ShapeDtypeStruct((S, D), F32),
                   jax.ShapeDtypeStruct((1, D), F32), jax.ShapeDtypeStruct((8, LANES), F32)],
        scratch_shapes=[pltpu.VMEM((8, LANES), F32)],
        compiler_params=_params(1),
    )(dqt, dka, dva, qraw, kraw, dz, f, gq, gk, w, wf, x1, g2, dx2, tri)


def _matmul_tn(a, b, col0, n, tn, name):
    S, M = a.shape
    ts = min(TN_ROWS, S)
    off = col0 // tn

    def body(a_ref, b_ref, o_ref):
        @pl.when(pl.program_id(1) == 0)
        def _():
            o_ref[...] = jnp.zeros((M, tn), F32)

        o_ref[...] += _dot_tn(a_ref[...], b_ref[...])

    return pl.pallas_call(
        body, name=name, grid=(n // tn, S // ts),
        in_specs=[pl.BlockSpec((ts, M), lambda j, s: (s, 0)), pl.BlockSpec((ts, tn), lambda j, s: (s, off + j))],
        out_specs=pl.BlockSpec((M, tn), lambda j, s: (0, j)),
        out_shape=jax.ShapeDtypeStruct((M, n), F32),
        compiler_params=_params(2),
    )(a, b)


def _adamw(w, g, m, v, name):
    r, c = w.shape
    tr = ROW_TILE if r % ROW_TILE == 0 else r

    def body(w_ref, g_ref, m_ref, v_ref, d_ref, m2_ref, v2_ref):
        gv = g_ref[...]
        m2 = ADAM_B1 * m_ref[...] + (1.0 - ADAM_B1) * gv
        v2 = ADAM_B2 * v_ref[...] + (1.0 - ADAM_B2) * (gv * gv)
        m2_ref[...] = m2
        v2_ref[...] = v2
        m_hat = m2 / (1.0 - ADAM_B1 ** ADAM_STEP)
        v_hat = v2 / (1.0 - ADAM_B2 ** ADAM_STEP)
        d_ref[...] = -ADAM_LR * (m_hat / (jnp.sqrt(v_hat) + ADAM_EPS) + ADAM_WD * w_ref[...])

    spec = _rows(tr, c)
    return pl.pallas_call(
        body, name=name, grid=(r // tr,), in_specs=[spec] * 4, out_specs=[spec] * 3,
        out_shape=[jax.ShapeDtypeStruct((r, c), F32)] * 3, compiler_params=_params(1),
    )(w, g, m, v)


def _local_step(x, target, g1, w_in, conv_w, w_out, g2, wa_in, b_f, gq, gk, wa_out):
    S, D = x.shape
    H = D // HEAD_DIM
    w_qkvz = wa_in[:, :4 * D]
    wf = jnp.pad(wa_in[:, 4 * D:], ((0, 0), (0, LANES - H)))
    bf = jnp.pad(b_f, ((0, 0), (0, LANES - H)))
    gq128 = jnp.pad(gq, ((0, 0), (0, LANES - HEAD_DIM)))
    gk128 = jnp.pad(gk, ((0, 0), (0, LANES - HEAD_DIM)))

    proj, h1, yc, y, x1 = _conv_fwd(x, g1, w_in, conv_w, w_out)
    h2, qraw, kraw, z, f, c, qa, ka, va, vt = _attn_front(x1, g2, w_qkvz, wf, bf, gq128, gk128)
    T = vt.shape[3]
    kstart, qend, bound = _skip_tables(c[:, :H], gq, gk, T, min(ATT_GROUP, S // T))
    o_aug = lax.cond(2.0 * bound <= PLAIN_EXP_MAX, functools.partial(_attn_fwd, online_max=False),
                     functools.partial(_attn_fwd, online_max=True), kstart, qa, ka, vt)
    dx2, dx2b, o2b, dz, doa, qa2, loss = _attn_out(o_aug, z, x1, target, wa_out, qa)
    dqt, dka, dva = _attn_bwd(qend, qa2, doa, ka, va, T)
    dproj2, dx1, dg2, small = _attn_proj_bwd(dqt, dka, dva, qraw, kraw, dz, f, gq128, gk128, w_qkvz, wf, x1, g2, dx2)
    dproj1, dx, dx1b, dg1, dcw = _conv_bwd(dx1, x, g1, w_in, w_out, conv_w, proj, yc)

    tn = min(1024, D)
    dwa_out = _matmul_tn(o2b, dx2b, 0, D, tn, "dw_attn_out")
    dwa_in = jnp.concatenate([_matmul_tn(h2, dproj2, 0, 4 * D, tn, "dw_attn_in"),
                              _matmul_tn(h2, dproj2, 4 * D, LANES, LANES, "dw_attn_f")[:, :H]], axis=1)
    dw_out = _matmul_tn(y, dx1b, 0, D, tn, "dw_conv_out")
    dw_in = _matmul_tn(h1, dproj1, 0, 4 * D, tn, "dw_conv_in")
    grads = dict(conv_norm_g=dg1, conv_w_in=dw_in, conv_w=dcw, conv_w_out=dw_out, attn_norm_g=dg2,
                 attn_w_in=dwa_in, attn_b_f=small[0:1, :H], attn_q_norm_g=small[1:2, :HEAD_DIM],
                 attn_k_norm_g=small[2:3, :HEAD_DIM], attn_w_out=dwa_out)
    return loss[0, 0], dx, grads


def _coords():
    return lax.axis_index("x"), lax.axis_index("y"), lax.axis_index("c")


def _at(ref, idx):
    return ref.at[idx] if idx else ref


def _other_chips(x, y):
    return [(1 - x, y), (x, 1 - y), (1 - x, 1 - y)]


def _all_gather(halved, whole):
    nh, nw = len(halved), len(whole)

    def body(*refs):
        src_h, src_w = refs[:nh], refs[nh:nh + nw]
        out_h, out_w = refs[nh + nw:2 * nh + nw], refs[2 * nh + nw:2 * (nh + nw)]
        send_h, recv_h, send_w, recv_w, local_sem = refs[2 * (nh + nw):]
        x, y, c = _coords()
        mine = 2 * x + y
        sibling = (x, y, 1 - c)
        chips = _other_chips(x, y)

        def copy_h(a, k, chip, half, to, src=None):
            dst = out_h[a].at[chip, half]
            return pltpu.make_async_remote_copy(src_ref=dst if src is None else src, dst_ref=dst,
                                                send_sem=send_h.at[a, k], recv_sem=recv_h.at[a, k],
                                                device_id=to, device_id_type=MESH)

        def copy_w(a, k, chip, to):
            return pltpu.make_async_remote_copy(src_ref=src_w[a], dst_ref=out_w[a].at[chip],
                                                send_sem=send_w.at[a, k], recv_sem=recv_w.at[a, k],
                                                device_id=to, device_id_type=MESH)

        local = [pltpu.make_async_copy(src_h[a], out_h[a].at[mine], local_sem.at[a]) for a in range(nh)]
        local += [pltpu.make_async_copy(src_w[a], out_w[a].at[mine], local_sem.at[nh + a]) for a in range(nw)]
        for cp in local:
            cp.start()
        first = [copy_h(a, k, mine, c, (*chip, c), src=src_h[a].at[c]) for a in range(nh) for k, chip in enumerate(chips)]
        first += [copy_w(a, k, mine, (*chip, c)) for a in range(nw) for k, chip in enumerate(chips)]
        for cp in first:
            cp.start()
        passed = []
        for a in range(nh):
            for k, (px, py) in enumerate(chips):
                copy_h(a, k, 2 * px + py, c, (x, y, c)).wait_recv()
                cp = copy_h(a, 3 + k, 2 * px + py, c, sibling)
                cp.start()
                passed.append(cp)
        for a in range(nh):
            for k, (px, py) in enumerate(chips):
                copy_h(a, 3 + k, 2 * px + py, 1 - c, (x, y, c)).wait_recv()
        for a in range(nw):
            for k, (px, py) in enumerate(chips):
                copy_w(a, k, 2 * px + py, (x, y, c)).wait_recv()
        for cp in first + passed:
            cp.wait_send()
        for cp in local:
            cp.wait()

    out_shape = [jax.ShapeDtypeStruct((4,) + a.shape, a.dtype) for a in list(halved) + list(whole)]
    return pl.pallas_call(
        body, name="gather_weights", in_specs=[ANY] * (nh + nw), out_specs=[ANY] * (nh + nw), out_shape=out_shape,
        scratch_shapes=[pltpu.SemaphoreType.DMA((nh, 6)), pltpu.SemaphoreType.DMA((nh, 6)),
                        pltpu.SemaphoreType.DMA((nw, 3)), pltpu.SemaphoreType.DMA((nw, 3)),
                        pltpu.SemaphoreType.DMA((nh + nw,))],
    )(*halved, *whole)


def _exchange(name, srcs, lands, copies, local_copies):
    ns, nl, n, nloc = len(srcs), len(lands), len(copies), len(local_copies)

    def body(*refs):
        src, land = refs[:ns], refs[ns:ns + nl]
        send, recv, local_sem = refs[ns + nl:]
        me = _coords()
        started = []
        for k, (si, s_at, li, l_at, ci) in enumerate(local_copies):
            cp = pltpu.make_async_copy(_at(src[si], s_at(*me)), _at(land[li], l_at(*me)), local_sem.at[k])
            cp.start()
            started.append(cp)
        remote = []
        for k, (si, s_at, li, l_at, peer) in enumerate(copies):
            cp = pltpu.make_async_remote_copy(src_ref=_at(src[si], s_at(*me)), dst_ref=_at(land[li], l_at(*me)),
                                              send_sem=send.at[k], recv_sem=recv.at[k],
                                              device_id=peer(*me), device_id_type=MESH)
            cp.start()
            remote.append(cp)
        for cp in remote:
            cp.wait()
        for cp in started:
            cp.wait()

    return pl.pallas_call(
        body, name=name, in_specs=[ANY] * ns, out_specs=[ANY] * nl, out_shape=list(lands),
        scratch_shapes=[pltpu.SemaphoreType.DMA((n,)), pltpu.SemaphoreType.DMA((n,)),
                        pltpu.SemaphoreType.DMA((max(nloc, 1),))],
    )(*srcs)


def _add_pairs(a, b, name):
    _, r, cols = a.shape
    tr = ROW_TILE if r % ROW_TILE == 0 else r

    def body(a_ref, b_ref, o_ref, ob_ref):
        s = a_ref[...] + b_ref[...]
        o_ref[...] = s
        ob_ref[...] = s.astype(BF16)

    spec = pl.BlockSpec((None, tr, cols), lambda j, i: (j, i, 0))
    return pl.pallas_call(
        body, name=name, grid=(4, r // tr), in_specs=[spec, spec], out_specs=[spec, spec],
        out_shape=[jax.ShapeDtypeStruct(a.shape, F32), jax.ShapeDtypeStruct(a.shape, BF16)],
        compiler_params=_params(2),
    )(a, b)


def _sum_chips(own, landed, mine, name):
    _, r, cols = landed.shape
    tr = ROW_TILE if r % ROW_TILE == 0 else r

    def body(mine_ref, own_ref, land_ref, o_ref):
        acc = None
        for j in range(4):
            term = jnp.where(mine_ref[0] == j, own_ref[...], land_ref[j].astype(F32))
            acc = term if acc is None else acc + term
        o_ref[...] = acc

    return pl.pallas_call(
        body, name=name,
        grid_spec=pltpu.PrefetchScalarGridSpec(
            num_scalar_prefetch=1, grid=(r // tr,),
            in_specs=[pl.BlockSpec((tr, cols), lambda i, m: (i, 0)), pl.BlockSpec((4, tr, cols), lambda i, m: (0, i, 0))],
            out_specs=pl.BlockSpec((tr, cols), lambda i, m: (i, 0))),
        out_shape=jax.ShapeDtypeStruct((r, cols), F32), compiler_params=_params(1),
    )(mine, own, landed)


def _sum_devices(landed, name):
    def body(l_ref, o_ref):
        acc = l_ref[0]
        for j in range(1, 8):
            acc = acc + l_ref[j]
        o_ref[...] = acc

    return pl.pallas_call(body, name=name, out_shape=jax.ShapeDtypeStruct(landed.shape[1:], F32))(landed)


def _reduce_gradients(big, small):
    nb = len(big)
    x, y, c = _coords()
    mine = 2 * x + y
    flips = [(fx, fy, fc) for fx in (0, 1) for fy in (0, 1) for fc in (0, 1) if fx or fy or fc]

    def flip(fx, fy, fc):
        return lambda x, y, c: (x ^ fx, y ^ fy, c ^ fc)

    copies = [(a, (lambda j: lambda x, y, c: (j, 1 - c))(j), a, (lambda j: lambda x, y, c: (j,))(j), flip(0, 0, 1))
              for a in range(nb) for j in range(4)]
    copies += [(nb, lambda x, y, c: (), nb, lambda x, y, c: (4 * x + 2 * y + c,), flip(*f)) for f in flips]
    lands = [jax.ShapeDtypeStruct((4,) + g.shape[2:], F32) for g in big] + [jax.ShapeDtypeStruct((8,) + small.shape, F32)]
    local = [(nb, lambda x, y, c: (), nb, lambda x, y, c: (4 * x + 2 * y + c,), None)]
    landed = _exchange("swap_halves", list(big) + [small], lands, copies, local)
    small_sum = _sum_devices(landed[nb], "sum_small")

    chip_f32, chip_bf16 = [], []
    for a in range(nb):
        kept = lax.dynamic_index_in_dim(big[a], c, axis=1, keepdims=False)
        s, sb = _add_pairs(kept, landed[a], f"add_cores_{a}")
        chip_f32.append(s)
        chip_bf16.append(sb)

    chip_flips = [(1, 0), (0, 1), (1, 1)]
    copies = [(a, (lambda f: lambda x, y, c: (2 * (x ^ f[0]) + (y ^ f[1]),))(f), a, lambda x, y, c: (2 * x + y,),
               flip(f[0], f[1], 0)) for a in range(nb) for f in chip_flips]
    local = [(a, lambda x, y, c: (2 * x + y,), a, lambda x, y, c: (2 * x + y,), None) for a in range(nb)]
    lands = [jax.ShapeDtypeStruct(g.shape, BF16) for g in chip_bf16]
    landed = _exchange("send_chip_sums", chip_bf16, lands, copies, local)
    mine_arr = jnp.reshape(mine, (1,)).astype(jnp.int32)
    totals = [_sum_chips(lax.dynamic_index_in_dim(chip_f32[a], mine, axis=0, keepdims=False), landed[a], mine_arr,
                         f"sum_chips_{a}") for a in range(nb)]

    copies = [(a, lambda x, y, c: (), a, lambda x, y, c: (c,), flip(0, 0, 1)) for a in range(nb)]
    local = [(a, lambda x, y, c: (), a, lambda x, y, c: (c,), None) for a in range(nb)]
    lands = [jax.ShapeDtypeStruct((2,) + t.shape, F32) for t in totals]
    return _exchange("swap_sums", totals, lands, copies, local), small_sum


def kernel(x, conv_norm_g, conv_w_in, conv_w, conv_w_out, attn_norm_g, attn_w_in, attn_b_f, attn_q_norm_g, attn_k_norm_g, attn_w_out, loss_target, m_conv_norm_g, m_conv_w_in, m_conv_w, m_conv_w_out, m_attn_norm_g, m_attn_w_in, m_attn_b_f, m_attn_q_norm_g, m_attn_k_norm_g, m_attn_w_out, v_conv_norm_g, v_conv_w_in, v_conv_w, v_conv_w_out, v_attn_norm_g, v_attn_w_in, v_attn_b_f, v_attn_q_norm_g, v_attn_k_norm_g, v_attn_w_out):
    xi, yi, _ = _coords()
    chip = 2 * xi + yi
    D = x.shape[2]
    H = D // HEAD_DIM
    names = ["conv_norm_g", "conv_w_in", "conv_w", "conv_w_out", "attn_norm_g", "attn_w_in", "attn_b_f",
             "attn_q_norm_g", "attn_k_norm_g", "attn_w_out"]
    weights = dict(zip(names, [conv_norm_g, conv_w_in, conv_w, conv_w_out, attn_norm_g, attn_w_in, attn_b_f,
                               attn_q_norm_g, attn_k_norm_g, attn_w_out]))
    m_in = dict(zip(names, [m_conv_norm_g, m_conv_w_in, m_conv_w, m_conv_w_out, m_attn_norm_g, m_attn_w_in,
                            m_attn_b_f, m_attn_q_norm_g, m_attn_k_norm_g, m_attn_w_out]))
    v_in = dict(zip(names, [v_conv_norm_g, v_conv_w_in, v_conv_w, v_conv_w_out, v_attn_norm_g, v_attn_w_in,
                            v_attn_b_f, v_attn_q_norm_g, v_attn_k_norm_g, v_attn_w_out]))
    weights = {k: w[0] for k, w in weights.items()}
    m_in = {k: w[0] for k, w in m_in.items()}
    v_in = {k: w[0] for k, w in v_in.items()}

    big_names = ["conv_w_in", "attn_w_in", "conv_w_out", "attn_w_out"]
    halved = [weights[k].astype(BF16).reshape(2, weights[k].shape[0] // 2, weights[k].shape[1]) for k in big_names]
    q = D // 4
    small_w = jnp.concatenate([weights["conv_w"], weights["attn_norm_g"][None, :], jnp.zeros((4, q), F32)], axis=0)
    g_in, ga_in, g_out, ga_out, g_small = _all_gather(halved, [small_w])
    w_in = g_in.reshape(4, D, D).transpose(1, 0, 2).reshape(D, 4 * D)
    wa_in = ga_in.reshape(4, D, D + H // 4).transpose(1, 0, 2).reshape(D, 4 * D + H)
    w_out = g_out.reshape(D, D)
    wa_out = ga_out.reshape(D, D)
    conv_w_full = g_small[:, 0:3, :].transpose(1, 0, 2).reshape(3, D)
    attn_g_full = g_small[:, 3, :].reshape(1, D)

    loss_part, grad_x, grads = _local_step(x[0], loss_target[0], weights["conv_norm_g"][None, :], w_in, conv_w_full,
                                           w_out, attn_g_full, wa_in, weights["attn_b_f"][None, :],
                                           weights["attn_q_norm_g"][None, :], weights["attn_k_norm_g"][None, :], wa_out)
    loss = lax.psum(loss_part, ("x", "y", "c"))

    big = [grads["conv_w_in"].reshape(D, 4, D).transpose(1, 0, 2).reshape(4, 2, D // 2, D),
           grads["attn_w_in"].reshape(D, 4, D + H // 4).transpose(1, 0, 2).reshape(4, 2, D // 2, D + H // 4),
           grads["conv_w_out"].reshape(4, 2, D // 8, D), grads["attn_w_out"].reshape(4, 2, D // 8, D)]
    tail = jnp.concatenate([grads["attn_b_f"], grads["attn_q_norm_g"], grads["attn_k_norm_g"],
                            jnp.zeros((1, D - H - 2 * HEAD_DIM), F32)], axis=1)
    small = jnp.concatenate([grads["conv_norm_g"], grads["conv_w"], grads["attn_norm_g"], tail,
                             jnp.zeros((2, D), F32)], axis=0)
    reduced, small_sum = _reduce_gradients(big, small)
    final = {k: r.reshape(weights[k].shape) for k, r in zip(big_names, reduced)}
    final["conv_norm_g"] = small_sum[0]
    final["conv_w"] = lax.dynamic_slice_in_dim(small_sum[1:4], chip * q, q, axis=1)
    final["attn_norm_g"] = lax.dynamic_slice_in_dim(small_sum[4], chip * q, q, axis=0)
    final["attn_b_f"] = small_sum[5, :H]
    final["attn_q_norm_g"] = small_sum[5, H:H + HEAD_DIM]
    final["attn_k_norm_g"] = small_sum[5, H + HEAD_DIM:H + 2 * HEAD_DIM]

    delta, new_m, new_v = {}, {}, {}
    for k in names:
        shape = weights[k].shape
        as2d = (lambda a: a.reshape(1, -1)) if len(shape) == 1 else (lambda a: a)
        d, m2, v2 = _adamw(as2d(weights[k]), as2d(final[k]), as2d(m_in[k]), as2d(v_in[k]), "adamw_" + k)
        delta[k], new_m[k], new_v[k] = d.reshape(shape), m2.reshape(shape), v2.reshape(shape)
    lead = lambda a: a[None]
    return (loss, grad_x[None], *[lead(final[k]) for k in names], *[lead(delta[k]) for k in names],
            *[lead(new_m[k]) for k in names], *[lead(new_v[k]) for k in names])
```

```python
import functools

import jax
import jax.numpy as jnp
from jax import lax
from jax.experimental import pallas as pl
from jax.experimental.pallas import tpu as pltpu

F32 = jnp.float32
BF16 = jnp.bfloat16
HEAD_DIM = 64
LANES = 128
AUG_A = 64
AUG_B = 67
RMS_EPS = 1e-6
NEG = -1e30
Q_SCALE = 0.125
ROW_TILE = 256
ATT_GROUP = 4
SKIP_LOG = 106.0
PLAIN_EXP_MAX = 60.0
TN_ROWS = 1024
VMEM_LIMIT = 56 << 20
ADAM_LR, ADAM_B1, ADAM_B2, ADAM_EPS, ADAM_WD, ADAM_STEP = 0.001, 0.9, 0.999, 1e-08, 0.01, 10
MESH = pl.DeviceIdType.MESH
ANY = pl.BlockSpec(memory_space=pl.ANY)


def _lane():
    return lax.broadcasted_iota(jnp.int32, (1, LANES), 1)


def _split3(x):
    hi = x.astype(BF16).astype(F32)
    r = x - hi
    mid = r.astype(BF16).astype(F32)
    lo = (r - mid).astype(BF16).astype(F32)
    return hi, mid, lo


def _put(base, lane, start, parts):
    for j, p in enumerate(parts):
        base = jnp.where(lane == start + j, p, base)
    return base


def _col(x, lane, idx):
    return jnp.sum(jnp.where(lane == idx, x, 0.0), axis=1, keepdims=True)


def _head_tile(ref, hd, lane):
    j = hd // 2
    t = ref[:, LANES * j:LANES * (j + 1)]
    if hd % 2:
        t = pltpu.roll(t, HEAD_DIM, 1)
    return jnp.where(lane < HEAD_DIM, t, 0.0)


def _pair_tile(even, odd):
    return even + pltpu.roll(odd, HEAD_DIM, 1)


def _sigmoid(x):
    return 1.0 / (1.0 + jnp.exp(-x))


def _dot(a, b):
    return jnp.dot(a, b, preferred_element_type=F32)


def _dot_nt(a, b):
    return lax.dot_general(a, b, (((1,), (1,)), ((), ())), preferred_element_type=F32)


def _dot_tn(a, b):
    return lax.dot_general(a, b, (((0,), (0,)), ((), ())), preferred_element_type=F32)


def _dot01(tri, x):
    hi, mid, lo = _split3(x)
    return _dot(tri, hi.astype(BF16)) + _dot(tri, mid.astype(BF16)) + _dot(tri, lo.astype(BF16))


def _rms_bwd(dh, x, g):
    inv = lax.rsqrt(jnp.mean(x * x, axis=-1, keepdims=True) + RMS_EPS)
    xh = x * inv
    dxn = dh * g
    dx = inv * (dxn - xh * jnp.mean(dxn * xh, axis=-1, keepdims=True))
    return dx, jnp.sum(dh * xh, axis=0, keepdims=True)


def _head_rms_bwd(dn, t, g):
    inv = lax.rsqrt(jnp.sum(t * t, axis=1, keepdims=True) * (1.0 / HEAD_DIM) + RMS_EPS)
    th = t * inv
    gd = dn * g
    d = inv * (gd - th * (jnp.sum(gd * th, axis=1, keepdims=True) * (1.0 / HEAD_DIM)))
    return d, jnp.sum(dn * th, axis=0, keepdims=True)


def _params(n_grid):
    return pltpu.CompilerParams(dimension_semantics=("arbitrary",) * n_grid, vmem_limit_bytes=VMEM_LIMIT)


def _rows(tm, cols, rev=None):
    if rev is None:
        return pl.BlockSpec((tm, cols), lambda i: (i, 0))
    return pl.BlockSpec((tm, cols), lambda i: (rev - i, 0))


def _whole(shape):
    return pl.BlockSpec(shape, lambda *_: (0,) * len(shape))


def _conv_fwd(x, g1, w_in, conv_w, w_out):
    S, D = x.shape
    tm = min(ROW_TILE, S)

    def body(x_ref, g_ref, win_ref, cw_ref, wout_ref, proj_ref, h_ref, yc_ref, y_ref, x1_ref, prev_u):
        i = pl.program_id(0)
        xv = x_ref[...]
        inv = lax.rsqrt(jnp.mean(xv * xv, axis=-1, keepdims=True) + RMS_EPS)
        h = (xv * inv * g_ref[...]).astype(BF16)
        h_ref[...] = h
        for j in range(4):
            proj_ref[:, j * D:(j + 1) * D] = _dot(h, win_ref[j])

        @pl.when(i == 0)
        def _():
            prev_u[...] = jnp.zeros((tm, D), F32)

        u = proj_ref[:, D:2 * D] * proj_ref[:, 2 * D:3 * D]
        pu = prev_u[...]
        row = lax.broadcasted_iota(jnp.int32, (tm, 1), 0)
        u1 = jnp.where(row < 1, pltpu.roll(pu, 1, 0), pltpu.roll(u, 1, 0))
        u2 = jnp.where(row < 2, pltpu.roll(pu, 2, 0), pltpu.roll(u, 2, 0))
        prev_u[...] = u
        w = cw_ref[...]
        yc = w[2:3] * u + w[1:2] * u1 + w[0:1] * u2
        yc_ref[...] = yc
        z = proj_ref[:, 3 * D:4 * D]
        y = (proj_ref[:, 0:D] * yc * (z * _sigmoid(z))).astype(BF16)
        y_ref[...] = y
        x1_ref[...] = xv + _dot(y, wout_ref[...])

    return pl.pallas_call(
        body, name="conv_fwd", grid=(S // tm,),
        in_specs=[_rows(tm, D), _whole((1, D)), _whole((4, D, D)), _whole((3, D)), _whole((D, D))],
        out_specs=[_rows(tm, 4 * D), _rows(tm, D), _rows(tm, D), _rows(tm, D), _rows(tm, D)],
        out_shape=[jax.ShapeDtypeStruct((S, 4 * D), F32), jax.ShapeDtypeStruct((S, D), BF16),
                   jax.ShapeDtypeStruct((S, D), F32), jax.ShapeDtypeStruct((S, D), BF16),
                   jax.ShapeDtypeStruct((S, D), F32)],
        scratch_shapes=[pltpu.VMEM((tm, D), F32)],
        compiler_params=_params(1),
    )(x, g1, w_in, conv_w, w_out)


def _conv_bwd(dx1, x, g1, w_in, w_out, conv_w, proj, yc):
    S, D = x.shape
    tm = min(ROW_TILE, S)
    last = S // tm - 1

    def body(dx1_ref, x_ref, g_ref, win_ref, wout_ref, cw_ref, proj_ref, yc_ref,
             dproj_ref, dx_ref, dx1b_ref, dg_ref, dcw_ref, next_d):
        @pl.when(pl.program_id(0) == 0)
        def _():
            dg_ref[...] = jnp.zeros((1, D), F32)
            dcw_ref[...] = jnp.zeros((3, D), F32)
            next_d[...] = jnp.zeros((tm, D), F32)

        dx1v = dx1_ref[...]
        dx1b = dx1v.astype(BF16)
        dx1b_ref[...] = dx1b
        dy = _dot_nt(dx1b, wout_ref[...])
        b = proj_ref[:, 0:D]
        c = proj_ref[:, D:2 * D]
        xin = proj_ref[:, 2 * D:3 * D]
        z = proj_ref[:, 3 * D:4 * D]
        sg = _sigmoid(z)
        sz = z * sg
        ycv = yc_ref[...]
        d0 = dy * b * sz
        dproj_ref[:, 0:D] = (dy * ycv * sz).astype(BF16)
        dproj_ref[:, 3 * D:4 * D] = (dy * b * ycv * (sg * (1.0 + z * (1.0 - sg)))).astype(BF16)
        nd = next_d[...]
        row = lax.broadcasted_iota(jnp.int32, (tm, 1), 0)
        d1 = jnp.where(row >= tm - 1, pltpu.roll(nd, tm - 1, 0), pltpu.roll(d0, tm - 1, 0))
        d2 = jnp.where(row >= tm - 2, pltpu.roll(nd, tm - 2, 0), pltpu.roll(d0, tm - 2, 0))
        next_d[...] = d0
        w = cw_ref[...]
        du = w[2:3] * d0 + w[1:2] * d1 + w[0:1] * d2
        u = c * xin
        dcw_ref[2:3, :] += jnp.sum(d0 * u, axis=0, keepdims=True)
        dcw_ref[1:2, :] += jnp.sum(d1 * u, axis=0, keepdims=True)
        dcw_ref[0:1, :] += jnp.sum(d2 * u, axis=0, keepdims=True)
        dproj_ref[:, D:2 * D] = (du * xin).astype(BF16)
        dproj_ref[:, 2 * D:3 * D] = (du * c).astype(BF16)
        dh = _dot_nt(dproj_ref[:, 0:D], win_ref[0])
        for j in range(1, 4):
            dh = dh + _dot_nt(dproj_ref[:, j * D:(j + 1) * D], win_ref[j])
        dxn, dg = _rms_bwd(dh, x_ref[...], g_ref[...])
        dx_ref[...] = dx1v + dxn
        dg_ref[...] += dg

    return pl.pallas_call(
        body, name="conv_bwd", grid=(S // tm,),
        in_specs=[_rows(tm, D, last), _rows(tm, D, last), _whole((1, D)), _whole((4, D, D)), _whole((D, D)),
                  _whole((3, D)), _rows(tm, 4 * D, last), _rows(tm, D, last)],
        out_specs=[_rows(tm, 4 * D, last), _rows(tm, D, last), _rows(tm, D, last), _whole((1, D)), _whole((3, D))],
        out_shape=[jax.ShapeDtypeStruct((S, 4 * D), BF16), jax.ShapeDtypeStruct((S, D), F32),
                   jax.ShapeDtypeStruct((S, D), BF16), jax.ShapeDtypeStruct((1, D), F32),
                   jax.ShapeDtypeStruct((3, D), F32)],
        scratch_shapes=[pltpu.VMEM((tm, D), F32)],
        compiler_params=_params(1),
    )(dx1, x, g1, w_in, w_out, conv_w, proj, yc)


def _attn_front(x1, g2, w, wf, bf, gq, gk):
    S, D = x1.shape
    H = D // HEAD_DIM
    tm = min(ROW_TILE, S)
    tri = (lax.broadcasted_iota(jnp.int32, (tm, tm), 1) <= lax.broadcasted_iota(jnp.int32, (tm, tm), 0)).astype(BF16)

    def body(x_ref, g_ref, w_ref, wf_ref, bf_ref, gq_ref, gk_ref, tri_ref,
             h_ref, qraw_ref, kraw_ref, z_ref, f_ref, c_ref, qa_ref, ka_ref, va_ref, vt_ref, carry, v_s):
        @pl.when(pl.program_id(0) == 0)
        def _():
            carry[...] = jnp.zeros((8, LANES), F32)

        xv = x_ref[...]
        inv = lax.rsqrt(jnp.mean(xv * xv, axis=-1, keepdims=True) + RMS_EPS)
        h = (xv * inv * g_ref[...]).astype(BF16)
        h_ref[...] = h
        qraw_ref[...] = _dot(h, w_ref[:, 0:D])
        kraw_ref[...] = _dot(h, w_ref[:, D:2 * D])
        v_s[...] = _dot(h, w_ref[:, 2 * D:3 * D])
        z_ref[...] = _dot(h, w_ref[:, 3 * D:4 * D])
        lane = _lane()
        f = _dot(h, wf_ref[...]) + bf_ref[...]
        f_ref[...] = f
        logf = jnp.where(lane < H, jnp.minimum(f, 0.0) - jnp.log(1.0 + jnp.exp(-jnp.abs(f))), 0.0)
        cs = _dot01(tri_ref[...], logf) + carry[0:1, :]
        c_ref[...] = cs
        carry[...] = jnp.broadcast_to(cs[tm - 1:tm, :], (8, LANES))
        one_a = (lane >= AUG_A) & (lane < AUG_A + 3)
        one_b = (lane >= AUG_B) & (lane < AUG_B + 3)
        for hd in range(H):
            sl = slice(LANES * hd, LANES * (hd + 1))
            ch = _col(cs, lane, hd)
            qt = _head_tile(qraw_ref, hd, lane)
            qn = qt * lax.rsqrt(jnp.sum(qt * qt, axis=1, keepdims=True) * (1.0 / HEAD_DIM) + RMS_EPS) * gq_ref[...]
            kt = _head_tile(kraw_ref, hd, lane)
            kn = kt * lax.rsqrt(jnp.sum(kt * kt, axis=1, keepdims=True) * (1.0 / HEAD_DIM) + RMS_EPS) * gk_ref[...]
            diag = jnp.sum(qn * kn, axis=1, keepdims=True) * Q_SCALE
            qa = _put(jnp.where(one_b, 1.0, qn * Q_SCALE), lane, AUG_A, _split3(ch - diag))
            qa_ref[:, sl] = qa.astype(BF16)
            ka = _put(jnp.where(one_a, 1.0, kn), lane, AUG_B, _split3(-ch))
            ka_ref[:, sl] = ka.astype(BF16)
            va = jnp.where(one_a, 1.0, _head_tile(v_s, hd, lane))
            va_ref[:, sl] = va.astype(BF16)
            vt_ref[hd] = va.T.astype(BF16)

    nb = S // tm
    return pl.pallas_call(
        body, name="attn_front", grid=(nb,),
        in_specs=[_rows(tm, D), _whole((1, D)), _whole((D, 4 * D)), _whole((D, LANES)), _whole((1, LANES)),
                  _whole((1, LANES)), _whole((1, LANES)), _whole((tm, tm))],
        out_specs=[_rows(tm, D), _rows(tm, D), _rows(tm, D), _rows(tm, D), _rows(tm, LANES), _rows(tm, LANES),
                   _rows(tm, H * LANES), _rows(tm, H * LANES), _rows(tm, H * LANES),
                   pl.BlockSpec((H, None, LANES, tm), lambda i: (0, i, 0, 0))],
        out_shape=[jax.ShapeDtypeStruct((S, D), BF16), jax.ShapeDtypeStruct((S, D), F32),
                   jax.ShapeDtypeStruct((S, D), F32), jax.ShapeDtypeStruct((S, D), F32),
                   jax.ShapeDtypeStruct((S, LANES), F32), jax.ShapeDtypeStruct((S, LANES), F32),
                   jax.ShapeDtypeStruct((S, H * LANES), BF16), jax.ShapeDtypeStruct((S, H * LANES), BF16),
                   jax.ShapeDtypeStruct((S, H * LANES), BF16), jax.ShapeDtypeStruct((H, nb, LANES, tm), BF16)],
        scratch_shapes=[pltpu.VMEM((8, LANES), F32), pltpu.VMEM((tm, D), F32)],
        compiler_params=_params(1),
    )(x1, g2, w, wf, bf, gq, gk, tri)


def _skip_tables(c, gq, gk, T, G):
    nb = c.shape[0] // T
    bound = 8.0 * jnp.max(jnp.abs(gq)) * jnp.max(jnp.abs(gk))
    first, last = c[0::T, :], c[T - 1::T, :]
    idx = jnp.arange(nb)
    need = (last[None, :, :] <= first[:, None, :] + (SKIP_LOG + 2.0 * bound)) & (idx[None, :, None] < idx[:, None, None])
    need = need | (idx[None, :, None] == idx[:, None, None])
    kstart = jnp.argmax(need, axis=1)
    qend = nb - 1 - jnp.argmax(need[::-1], axis=0)
    kstart = jnp.min(kstart.reshape(nb // G, G, -1), axis=1)
    qend = jnp.max(qend.reshape(nb // G, G, -1), axis=1)
    return kstart.T.astype(jnp.int32), qend.T.astype(jnp.int32), bound


def _attn_fwd(kstart, qa, ka, vt, online_max):
    S = qa.shape[0]
    H = qa.shape[1] // LANES
    nb, T = vt.shape[1], vt.shape[3]
    G = nb // kstart.shape[1]
    W = G * T

    def finish(acc, shift, o_ref):
        l = acc[AUG_A:AUG_A + 1, :]
        feat = lax.broadcasted_iota(jnp.int32, (LANES, 1), 0)
        o_ref[...] = jnp.where(feat == AUG_A + 1, shift + jnp.log(l), acc / l).T

    def causal(st):
        return jnp.where(lax.broadcasted_iota(jnp.int32, st.shape, 0) <= lax.broadcasted_iota(jnp.int32, st.shape, 1),
                         st, NEG)

    def fast_body(ks_ref, q_ref, k_ref, vt_ref, o_ref, acc_ref, sa_ref, sb_ref):
        h, g = pl.program_id(0), pl.program_id(1)
        q = q_ref[...]
        acc_ref[...] = jnp.zeros((LANES, W), F32)

        def scores(ki, lo):
            return _dot_nt(k_ref[pl.ds(pl.multiple_of(ki * T, T), 2 * T), :], q[lo * T:, :])

        def weighted(ki, p):
            return _dot(vt_ref[ki], p[:T]) + _dot(vt_ref[ki + 1], p[T:])

        first = ks_ref[h, g]
        first = first - ((g * G - first) & 1)
        steps = (g * G - first) // 2
        sa_ref[...] = scores(first, 0)

        def advance(ki, cur_ref, next_ref):
            p = jnp.exp(cur_ref[...]).astype(BF16)
            next_ref[...] = scores(ki + 2, 0)
            acc_ref[...] += weighted(ki, p)

        def loop(i, carry):
            advance(first + 4 * i, sa_ref, sb_ref)
            advance(first + 4 * i + 2, sb_ref, sa_ref)
            return carry

        lax.fori_loop(0, steps // 2, loop, 0)

        @pl.when(steps % 2 == 1)
        def _():
            advance(g * G - 2, sa_ref, sb_ref)
            acc_ref[...] += weighted(g * G, jnp.exp(causal(sb_ref[...])).astype(BF16))

        @pl.when(steps % 2 == 0)
        def _():
            acc_ref[...] += weighted(g * G, jnp.exp(causal(sa_ref[...])).astype(BF16))

        for j in range(2, G, 2):
            p = jnp.exp(causal(scores(g * G + j, j))).astype(BF16)
            acc_ref[:, j * T:] += weighted(g * G + j, p)
        finish(acc_ref[...], 0.0, o_ref)

    def online_body(ks_ref, q_ref, k_ref, vt_ref, o_ref, acc_ref, m_ref):
        h, g = pl.program_id(0), pl.program_id(1)
        q = q_ref[...]
        m_ref[...] = jnp.full((8, W), NEG, F32)
        acc_ref[...] = jnp.zeros((LANES, W), F32)

        def update(st, vtb, lo):
            m_old = m_ref[0:1, lo:]
            m_new = jnp.maximum(m_old, jnp.max(st, axis=0, keepdims=True))
            p = jnp.exp(st - m_new).astype(BF16)
            acc_ref[:, lo:] = acc_ref[:, lo:] * jnp.exp(m_old - m_new) + _dot(vtb, p)
            m_ref[:, lo:] = jnp.broadcast_to(m_new, (8, W - lo))

        def loop(ki, carry):
            kb = k_ref[pl.ds(pl.multiple_of(ki * T, T), T), :]
            update(_dot_nt(kb, q), vt_ref[ki], 0)
            return carry

        lax.fori_loop(ks_ref[h, g], g * G, loop, 0)
        for j in range(G):
            ki = g * G + j
            kb = k_ref[pl.ds(pl.multiple_of(ki * T, T), T), :]
            update(causal(_dot_nt(kb, q[j * T:, :])), vt_ref[ki], j * T)
        finish(acc_ref[...], m_ref[0:1, :], o_ref)

    return pl.pallas_call(
        online_body if online_max else fast_body, name="attn_fwd_online" if online_max else "attn_fwd",
        grid_spec=pltpu.PrefetchScalarGridSpec(
            num_scalar_prefetch=1, grid=(H, nb // G),
            in_specs=[pl.BlockSpec((W, LANES), lambda h, i, ks: (i, h)),
                      pl.BlockSpec((S, LANES), lambda h, i, ks: (0, h)),
                      pl.BlockSpec((None, nb, LANES, T), lambda h, i, ks: (h, 0, 0, 0))],
            out_specs=pl.BlockSpec((W, LANES), lambda h, i, ks: (i, h)),
            scratch_shapes=[pltpu.VMEM((LANES, W), F32)] + (
                [pltpu.VMEM((8, W), F32)] if online_max else [pltpu.VMEM((2 * T, W), F32)] * 2)),
        out_shape=jax.ShapeDtypeStruct((S, H * LANES), F32),
        compiler_params=_params(2),
    )(kstart, qa, ka, vt)


def _attn_out(o_aug, z, x1, target, w_out, qa):
    S, D = x1.shape
    H = D // HEAD_DIM
    tm = min(ROW_TILE, S)

    def body(o_ref, z_ref, x1_ref, t_ref, w_ref, q_ref,
             dx2_ref, dx2b_ref, o2b_ref, dz_ref, doa_ref, qa2_ref, loss_ref, oc_s, do_s):
        @pl.when(pl.program_id(0) == 0)
        def _():
            loss_ref[...] = jnp.zeros((1, LANES), F32)

        lane = _lane()
        lses = []
        for j in range(H // 2):
            oe = o_ref[:, 2 * LANES * j:2 * LANES * j + LANES]
            oo = o_ref[:, 2 * LANES * j + LANES:2 * LANES * (j + 1)]
            lses += [_col(oe, lane, AUG_A + 1), _col(oo, lane, AUG_A + 1)]
            oc_s[:, LANES * j:LANES * (j + 1)] = _pair_tile(jnp.where(lane < HEAD_DIM, oe, 0.0),
                                                            jnp.where(lane < HEAD_DIM, oo, 0.0))
        oc = oc_s[...]
        zv = z_ref[...]
        sg = _sigmoid(zv)
        sz = zv * sg
        o2 = (oc * sz).astype(BF16)
        o2b_ref[...] = o2
        e = x1_ref[...] + _dot(o2, w_ref[...]) - t_ref[...]
        sq = jnp.sum(jnp.sum(e * e, axis=1, keepdims=True), axis=0, keepdims=True)
        loss_ref[...] += jnp.broadcast_to(sq * (0.5 / D), (1, LANES))
        dx2 = e * (1.0 / D)
        dx2_ref[...] = dx2
        dx2b = dx2.astype(BF16)
        dx2b_ref[...] = dx2b
        do2 = _dot_nt(dx2b, w_ref[...])
        dz_ref[...] = do2 * oc * (sg * (1.0 + zv * (1.0 - sg)))
        do_s[...] = do2 * sz
        one_a = (lane >= AUG_A) & (lane < AUG_A + 3)
        for hd in range(H):
            sl = slice(LANES * hd, LANES * (hd + 1))
            dt = _head_tile(do_s, hd, lane)
            delta = jnp.sum(dt * _head_tile(oc_s, hd, lane), axis=1, keepdims=True)
            doa_ref[:, sl] = _put(dt, lane, AUG_A, _split3(-delta)).astype(BF16)
            qt = q_ref[:, sl].astype(F32)
            rq = jnp.sum(jnp.where(one_a, qt, 0.0), axis=1, keepdims=True) - lses[hd]
            qa2_ref[:, sl] = _put(qt, lane, AUG_A, _split3(rq)).astype(BF16)

    return pl.pallas_call(
        body, name="attn_out", grid=(S // tm,),
        in_specs=[_rows(tm, H * LANES), _rows(tm, D), _rows(tm, D), _rows(tm, D), _whole((D, D)),
                  _rows(tm, H * LANES)],
        out_specs=[_rows(tm, D), _rows(tm, D), _rows(tm, D), _rows(tm, D), _rows(tm, H * LANES),
                   _rows(tm, H * LANES), _whole((1, LANES))],
        out_shape=[jax.ShapeDtypeStruct((S, D), F32), jax.ShapeDtypeStruct((S, D), BF16),
                   jax.ShapeDtypeStruct((S, D), BF16), jax.ShapeDtypeStruct((S, D), F32),
                   jax.ShapeDtypeStruct((S, H * LANES), BF16), jax.ShapeDtypeStruct((S, H * LANES), BF16),
                   jax.ShapeDtypeStruct((1, LANES), F32)],
        scratch_shapes=[pltpu.VMEM((tm, D), F32), pltpu.VMEM((tm, D), F32)],
        compiler_params=_params(1),
    )(o_aug, z, x1, target, w_out, qa)


def _attn_bwd(qend, qa2, doa, ka, va, T):
    S = qa2.shape[0]
    H = qa2.shape[1] // LANES
    nb = S // T
    G = nb // qend.shape[1]
    W = G * T

    def body(qe_ref, q_ref, do_ref, k_ref, v_ref, dq_ref, dk_ref, dv_ref, dk_acc, dv_acc):
        h, g = pl.program_id(0), pl.program_id(1)

        @pl.when(g == 0)
        def _():
            dq_ref[...] = jnp.zeros((nb, LANES, T), F32)

        kb = k_ref[...]
        vb = v_ref[...]
        kt = kb.astype(F32).T.astype(BF16)
        dk_acc[...] = jnp.zeros((W, LANES), F32)
        dv_acc[...] = jnp.zeros((W, LANES), F32)

        def step(qi, rows, masked):
            q0 = pl.multiple_of(qi * T, T)
            qb = q_ref[pl.ds(q0, 2 * T), :]
            dob = do_ref[pl.ds(q0, 2 * T), :]
            st = _dot_nt(kb[:rows], qb)
            if masked:
                key = lax.broadcasted_iota(jnp.int32, st.shape, 0)
                st = jnp.where(key <= lax.broadcasted_iota(jnp.int32, st.shape, 1) + (rows - 2 * T), st, NEG)
            p = jnp.exp(st)
            ds = (p * _dot_nt(vb[:rows], dob)).astype(BF16)
            dv_acc[0:rows, :] += _dot(p.astype(BF16), dob)
            dk_acc[0:rows, :] += _dot(ds, qb)
            dqt = _dot(kt[:, :rows], ds)
            dq_ref[qi] += dqt[:, :T]
            dq_ref[qi + 1] += dqt[:, T:]

        def loop(i, carry):
            step(g * G + G + 2 * i, W, False)
            return carry

        for j in range(0, G, 2):
            step(g * G + j, (j + 2) * T, True)
        lax.fori_loop(0, (qe_ref[h, g] - (g * G + G) + 2) // 2, loop, 0)
        dk_ref[...] = dk_acc[...]
        dv_ref[...] = dv_acc[...]

    return pl.pallas_call(
        body, name="attn_bwd",
        grid_spec=pltpu.PrefetchScalarGridSpec(
            num_scalar_prefetch=1, grid=(H, nb // G),
            in_specs=[pl.BlockSpec((S, LANES), lambda h, i, qe: (0, h)), pl.BlockSpec((S, LANES), lambda h, i, qe: (0, h)),
                      pl.BlockSpec((W, LANES), lambda h, i, qe: (i, h)), pl.BlockSpec((W, LANES), lambda h, i, qe: (i, h))],
            out_specs=[pl.BlockSpec((None, nb, LANES, T), lambda h, i, qe: (h, 0, 0, 0)),
                       pl.BlockSpec((W, LANES), lambda h, i, qe: (i, h)), pl.BlockSpec((W, LANES), lambda h, i, qe: (i, h))],
            scratch_shapes=[pltpu.VMEM((W, LANES), F32), pltpu.VMEM((W, LANES), F32)]),
        out_shape=[jax.ShapeDtypeStruct((H, nb, LANES, T), F32), jax.ShapeDtypeStruct((S, H * LANES), F32),
                   jax.ShapeDtypeStruct((S, H * LANES), F32)],
        compiler_params=_params(2),
    )(qend, qa2, doa, ka, va)


def _attn_proj_bwd(dqt, dka, dva, qraw, kraw, dz, f, gq, gk, w, wf, x1, g2, dx2):
    S, D = x1.shape
    H = D // HEAD_DIM
    tm = dqt.shape[3]
    last = S // tm - 1
    tri = (lax.broadcasted_iota(jnp.int32, (tm, tm), 1) >= lax.broadcasted_iota(jnp.int32, (tm, tm), 0)).astype(BF16)

    def body(dqt_ref, dk_ref, dv_ref, q_ref, k_ref, dz_ref, f_ref, gq_ref, gk_ref, w_ref, wf_ref, x1_ref, g2_ref,
             dx2_ref, tri_ref, dproj_ref, dx1_ref, dg2_ref, small_ref, carry):
        @pl.when(pl.program_id(0) == 0)
        def _():
            dg2_ref[...] = jnp.zeros((1, D), F32)
            small_ref[...] = jnp.zeros((8, LANES), F32)
            carry[...] = jnp.zeros((8, LANES), F32)

        lane = _lane()
        dcs = jnp.zeros((tm, LANES), F32)
        dgq = jnp.zeros((1, LANES), F32)
        dgk = jnp.zeros((1, LANES), F32)
        for j in range(H // 2):
            dq2, dk2, dv2 = [], [], []
            for hd in (2 * j, 2 * j + 1):
                sl = slice(LANES * hd, LANES * (hd + 1))
                dqf = dqt_ref[hd].T
                dqn = jnp.where(lane < HEAD_DIM, dqf * Q_SCALE, 0.0)
                d, dg = _head_rms_bwd(dqn, _head_tile(q_ref, hd, lane), gq_ref[...])
                dq2.append(d)
                dgq = dgq + dg
                dkt = dk_ref[:, sl]
                dcs = dcs + jnp.where(lane == hd, _col(dqf, lane, AUG_A) - _col(dkt, lane, AUG_B), 0.0)
                d, dg = _head_rms_bwd(jnp.where(lane < HEAD_DIM, dkt, 0.0), _head_tile(k_ref, hd, lane), gk_ref[...])
                dk2.append(d)
                dgk = dgk + dg
                dv2.append(jnp.where(lane < HEAD_DIM, dv_ref[:, sl], 0.0))
            for part, pair in enumerate((dq2, dk2, dv2)):
                dproj_ref[:, part * D + LANES * j:part * D + LANES * (j + 1)] = _pair_tile(*pair).astype(BF16)
        dproj_ref[:, 3 * D:4 * D] = dz_ref[...].astype(BF16)
        dlogf = _dot01(tri_ref[...], dcs) + carry[0:1, :]
        carry[...] = jnp.broadcast_to(dlogf[0:1, :], (8, LANES))
        df = dlogf * (1.0 / (1.0 + jnp.exp(f_ref[...])))
        dproj_ref[:, 4 * D:4 * D + LANES] = df.astype(BF16)
        small_ref[0:1, :] += jnp.sum(df, axis=0, keepdims=True)
        small_ref[1:2, :] += dgq
        small_ref[2:3, :] += dgk
        dh = _dot_nt(dproj_ref[:, 0:4 * D], w_ref[...]) + _dot_nt(dproj_ref[:, 4 * D:4 * D + LANES], wf_ref[...])
        dxn, dg = _rms_bwd(dh, x1_ref[...], g2_ref[...])
        dx1_ref[...] = dx2_ref[...] + dxn
        dg2_ref[...] += dg

    W = 4 * D + LANES
    return pl.pallas_call(
        body, name="attn_proj_bwd", grid=(S // tm,),
        in_specs=[pl.BlockSpec((H, None, LANES, tm), lambda i: (0, last - i, 0, 0)),
                  _rows(tm, H * LANES, last), _rows(tm, H * LANES, last), _rows(tm, D, last), _rows(tm, D, last),
                  _rows(tm, D, last), _rows(tm, LANES, last), _whole((1, LANES)), _whole((1, LANES)),
                  _whole((D, 4 * D)), _whole((D, LANES)), _rows(tm, D, last), _whole((1, D)), _rows(tm, D, last),
                  _whole((tm, tm))],
        out_specs=[_rows(tm, W, last), _rows(tm, D, last), _whole((1, D)), _whole((8, LANES))],
        out_shape=[jax.ShapeDtypeStruct((S, W), BF16), jax.ShapeDtypeStruct((S, D), F32),
                   jax.ShapeDtypeStruct((1, D), F32), jax.ShapeDtypeStruct((8, LANES), F32)],
        scratch_shapes=[pltpu.VMEM((8, LANES), F32)],
        compiler_params=_params(1),
    )(dqt, dka, dva, qraw, kraw, dz, f, gq, gk, w, wf, x1, g2, dx2, tri)


def _matmul_tn(a, b, col0, n, tn, name, stacked=False):
    S, M = a.shape
    ts = min(TN_ROWS, S)
    off = col0 // tn

    def body(a_ref, b_ref, o_ref):
        @pl.when(pl.program_id(1) == 0)
        def _():
            o_ref[...] = jnp.zeros((M, tn), F32)

        o_ref[...] += _dot_tn(a_ref[...], b_ref[...])

    if stacked:
        out_spec, out_shape = pl.BlockSpec((None, M, tn), lambda j, s: (j, 0, 0)), (n // tn, M, tn)
    else:
        out_spec, out_shape = pl.BlockSpec((M, tn), lambda j, s: (0, j)), (M, n)
    return pl.pallas_call(
        body, name=name, grid=(n // tn, S // ts),
        in_specs=[pl.BlockSpec((ts, M), lambda j, s: (s, 0)), pl.BlockSpec((ts, tn), lambda j, s: (s, off + j))],
        out_specs=out_spec, out_shape=jax.ShapeDtypeStruct(out_shape, F32),
        compiler_params=_params(2),
    )(a, b)


def _adamw(w, g, m, v, name):
    r, c = w.shape
    tr = ROW_TILE if r % ROW_TILE == 0 else r

    def body(w_ref, g_ref, m_ref, v_ref, d_ref, m2_ref, v2_ref):
        gv = g_ref[...]
        m2 = ADAM_B1 * m_ref[...] + (1.0 - ADAM_B1) * gv
        v2 = ADAM_B2 * v_ref[...] + (1.0 - ADAM_B2) * (gv * gv)
        m2_ref[...] = m2
        v2_ref[...] = v2
        m_hat = m2 / (1.0 - ADAM_B1 ** ADAM_STEP)
        v_hat = v2 / (1.0 - ADAM_B2 ** ADAM_STEP)
        d_ref[...] = -ADAM_LR * (m_hat / (jnp.sqrt(v_hat) + ADAM_EPS) + ADAM_WD * w_ref[...])

    spec = _rows(tr, c)
    return pl.pallas_call(
        body, name=name, grid=(r // tr,), in_specs=[spec] * 4, out_specs=[spec] * 3,
        out_shape=[jax.ShapeDtypeStruct((r, c), F32)] * 3, compiler_params=_params(1),
    )(w, g, m, v)


def _local_step(x, target, g1, w_in, conv_w, w_out, g2, wa_in, b_f, gq, gk, wa_out):
    S, D = x.shape
    H = D // HEAD_DIM
    w_qkvz = wa_in[:, :4 * D]
    wf = jnp.pad(wa_in[:, 4 * D:], ((0, 0), (0, LANES - H)))
    bf = jnp.pad(b_f, ((0, 0), (0, LANES - H)))
    gq128 = jnp.pad(gq, ((0, 0), (0, LANES - HEAD_DIM)))
    gk128 = jnp.pad(gk, ((0, 0), (0, LANES - HEAD_DIM)))

    proj, h1, yc, y, x1 = _conv_fwd(x, g1, w_in, conv_w, w_out)
    h2, qraw, kraw, z, f, c, qa, ka, va, vt = _attn_front(x1, g2, w_qkvz, wf, bf, gq128, gk128)
    T = vt.shape[3]
    kstart, qend, bound = _skip_tables(c[:, :H], gq, gk, T, min(ATT_GROUP, S // T))
    o_aug = lax.cond(2.0 * bound <= PLAIN_EXP_MAX, functools.partial(_attn_fwd, online_max=False),
                     functools.partial(_attn_fwd, online_max=True), kstart, qa, ka, vt)
    dx2, dx2b, o2b, dz, doa, qa2, loss = _attn_out(o_aug, z, x1, target, wa_out, qa)
    dqt, dka, dva = _attn_bwd(qend, qa2, doa, ka, va, T)
    dproj2, dx1, dg2, small = _attn_proj_bwd(dqt, dka, dva, qraw, kraw, dz, f, gq128, gk128, w_qkvz, wf, x1, g2, dx2)
    dproj1, dx, dx1b, dg1, dcw = _conv_bwd(dx1, x, g1, w_in, w_out, conv_w, proj, yc)

    tn = min(1024, D)
    dwa_out = _matmul_tn(o2b, dx2b, 0, D, tn, "dw_attn_out")
    dwa_in = jnp.concatenate([_matmul_tn(h2, dproj2, 0, 4 * D, tn, "dw_attn_in"),
                              _matmul_tn(h2, dproj2, 4 * D, LANES, LANES, "dw_attn_f")[:, :H]], axis=1)
    dw_out = _matmul_tn(y, dx1b, 0, D, tn, "dw_conv_out")
    dw_in = _matmul_tn(h1, dproj1, 0, 4 * D, D, "dw_conv_in", stacked=True)
    grads = dict(conv_norm_g=dg1, conv_w_in=dw_in, conv_w=dcw, conv_w_out=dw_out, attn_norm_g=dg2,
                 attn_w_in=dwa_in, attn_b_f=small[0:1, :H], attn_q_norm_g=small[1:2, :HEAD_DIM],
                 attn_k_norm_g=small[2:3, :HEAD_DIM], attn_w_out=dwa_out)
    return loss[0, 0], dx, grads


def _coords():
    return lax.axis_index("x"), lax.axis_index("y"), lax.axis_index("c")


def _at(ref, idx):
    return ref.at[idx] if idx else ref


def _other_chips(x, y):
    return [(1 - x, y), (x, 1 - y), (1 - x, 1 - y)]


def _all_gather(halved, whole):
    nh, nw = len(halved), len(whole)

    def body(*refs):
        src_h, src_w = refs[:nh], refs[nh:nh + nw]
        out_h, out_w = refs[nh + nw:2 * nh + nw], refs[2 * nh + nw:2 * (nh + nw)]
        send_h, recv_h, send_w, recv_w = refs[2 * (nh + nw):]
        x, y, c = _coords()
        mine = 2 * x + y
        sibling = (x, y, 1 - c)
        chips = _other_chips(x, y)

        def copy_h(a, k, chip, half, to, src=None):
            dst = out_h[a].at[chip, half]
            return pltpu.make_async_remote_copy(src_ref=dst if src is None else src, dst_ref=dst,
                                                send_sem=send_h.at[a, k], recv_sem=recv_h.at[a, k],
                                                device_id=to, device_id_type=MESH)

        def copy_w(a, k, chip, to):
            return pltpu.make_async_remote_copy(src_ref=src_w[a], dst_ref=out_w[a].at[chip],
                                                send_sem=send_w.at[a, k], recv_sem=recv_w.at[a, k],
                                                device_id=to, device_id_type=MESH)

        first = [copy_h(a, k, mine, c, (*chip, c), src=src_h[a].at[c]) for a in range(nh) for k, chip in enumerate(chips)]
        first += [copy_w(a, k, mine, (*chip, c)) for a in range(nw) for k, chip in enumerate(chips)]
        for cp in first:
            cp.start()
        passed = []
        for a in range(nh):
            for k, (px, py) in enumerate(chips):
                copy_h(a, k, 2 * px + py, c, (x, y, c)).wait_recv()
                cp = copy_h(a, 3 + k, 2 * px + py, c, sibling)
                cp.start()
                passed.append(cp)
        for a in range(nh):
            for k, (px, py) in enumerate(chips):
                copy_h(a, 3 + k, 2 * px + py, 1 - c, (x, y, c)).wait_recv()
        for a in range(nw):
            for k, (px, py) in enumerate(chips):
                copy_w(a, k, 2 * px + py, (x, y, c)).wait_recv()
        for cp in first + passed:
            cp.wait_send()

    out_shape = [jax.ShapeDtypeStruct((4,) + a.shape, a.dtype) for a in list(halved) + list(whole)]
    gathered = pl.pallas_call(
        body, name="gather_weights", in_specs=[ANY] * (nh + nw), out_specs=[ANY] * (nh + nw), out_shape=out_shape,
        scratch_shapes=[pltpu.SemaphoreType.DMA((nh, 6)), pltpu.SemaphoreType.DMA((nh, 6)),
                        pltpu.SemaphoreType.DMA((nw, 3)), pltpu.SemaphoreType.DMA((nw, 3))],
    )(*halved, *whole)
    x, y, _ = _coords()
    return [lax.dynamic_update_index_in_dim(g, a, 2 * x + y, axis=0) for g, a in zip(gathered, list(halved) + list(whole))]


def _exchange(name, srcs, lands, copies, local_copies):
    ns, nl, n, nloc = len(srcs), len(lands), len(copies), len(local_copies)

    def body(*refs):
        src, land = refs[:ns], refs[ns:ns + nl]
        send, recv, local_sem = refs[ns + nl:]
        me = _coords()
        started = []
        for k, (si, s_at, li, l_at, ci) in enumerate(local_copies):
            cp = pltpu.make_async_copy(_at(src[si], s_at(*me)), _at(land[li], l_at(*me)), local_sem.at[k])
            cp.start()
            started.append(cp)
        remote = []
        for k, (si, s_at, li, l_at, peer) in enumerate(copies):
            cp = pltpu.make_async_remote_copy(src_ref=_at(src[si], s_at(*me)), dst_ref=_at(land[li], l_at(*me)),
                                              send_sem=send.at[k], recv_sem=recv.at[k],
                                              device_id=peer(*me), device_id_type=MESH)
            cp.start()
            remote.append(cp)
        for cp in remote:
            cp.wait()
        for cp in started:
            cp.wait()

    return pl.pallas_call(
        body, name=name, in_specs=[ANY] * ns, out_specs=[ANY] * nl, out_shape=list(lands),
        scratch_shapes=[pltpu.SemaphoreType.DMA((n,)), pltpu.SemaphoreType.DMA((n,)),
                        pltpu.SemaphoreType.DMA((max(nloc, 1),))],
    )(*srcs)


def _add_pairs(a, b, name):
    _, r, cols = a.shape
    tr = ROW_TILE if r % ROW_TILE == 0 else r

    def body(a_ref, b_ref, o_ref, ob_ref):
        s = a_ref[...] + b_ref[...]
        o_ref[...] = s
        ob_ref[...] = s.astype(BF16)

    spec = pl.BlockSpec((None, tr, cols), lambda j, i: (j, i, 0))
    return pl.pallas_call(
        body, name=name, grid=(4, r // tr), in_specs=[spec, spec], out_specs=[spec, spec],
        out_shape=[jax.ShapeDtypeStruct(a.shape, F32), jax.ShapeDtypeStruct(a.shape, BF16)],
        compiler_params=_params(2),
    )(a, b)


def _sum_chips(own, landed, name):
    _, r, cols = landed.shape
    tr = ROW_TILE if r % ROW_TILE == 0 else r

    def body(own_ref, land_ref, o_ref):
        acc = own_ref[...]
        for j in range(3):
            acc = acc + land_ref[j].astype(F32)
        o_ref[...] = acc

    return pl.pallas_call(
        body, name=name, grid=(r // tr,),
        in_specs=[_rows(tr, cols), pl.BlockSpec((3, tr, cols), lambda i: (0, i, 0))], out_specs=_rows(tr, cols),
        out_shape=jax.ShapeDtypeStruct((r, cols), F32), compiler_params=_params(1),
    )(own, landed)


def _sum_devices(landed, name):
    def body(l_ref, o_ref):
        acc = l_ref[0]
        for j in range(1, 8):
            acc = acc + l_ref[j]
        o_ref[...] = acc

    return pl.pallas_call(body, name=name, out_shape=jax.ShapeDtypeStruct(landed.shape[1:], F32))(landed)


def _reduce_gradients(big, small):
    nb = len(big)
    x, y, c = _coords()
    mine = 2 * x + y
    flips = [(fx, fy, fc) for fx in (0, 1) for fy in (0, 1) for fc in (0, 1) if fx or fy or fc]

    def flip(fx, fy, fc):
        return lambda x, y, c: (x ^ fx, y ^ fy, c ^ fc)

    copies = [(a, (lambda j: lambda x, y, c: (j, 1 - c))(j), a, (lambda j: lambda x, y, c: (j,))(j), flip(0, 0, 1))
              for a in range(nb) for j in range(4)]
    copies += [(nb, lambda x, y, c: (), nb, lambda x, y, c: (4 * x + 2 * y + c,), flip(*f)) for f in flips]
    lands = [jax.ShapeDtypeStruct((4,) + g.shape[2:], F32) for g in big] + [jax.ShapeDtypeStruct((8,) + small.shape, F32)]
    local = [(nb, lambda x, y, c: (), nb, lambda x, y, c: (4 * x + 2 * y + c,), None)]
    landed = _exchange("swap_halves", list(big) + [small], lands, copies, local)
    small_sum = _sum_devices(landed[nb], "sum_small")

    chip_f32, chip_bf16 = [], []
    for a in range(nb):
        kept = lax.dynamic_index_in_dim(big[a], c, axis=1, keepdims=False)
        s, sb = _add_pairs(kept, landed[a], f"add_cores_{a}")
        chip_f32.append(s)
        chip_bf16.append(sb)

    chip_flips = [(1, 0), (0, 1), (1, 1)]
    copies = [(a, (lambda f: lambda x, y, c: (2 * (x ^ f[0]) + (y ^ f[1]),))(f), a, (lambda k: lambda x, y, c: (k,))(k),
               flip(f[0], f[1], 0)) for a in range(nb) for k, f in enumerate(chip_flips)]
    lands = [jax.ShapeDtypeStruct((3,) + g.shape[1:], BF16) for g in chip_bf16]
    landed = _exchange("send_chip_sums", chip_bf16, lands, copies, [])
    totals = [_sum_chips(lax.dynamic_index_in_dim(chip_f32[a], mine, axis=0, keepdims=False), landed[a],
                         f"sum_chips_{a}") for a in range(nb)]

    copies = [(a, lambda x, y, c: (), a, lambda x, y, c: (), flip(0, 0, 1)) for a in range(nb)]
    lands = [jax.ShapeDtypeStruct(t.shape, F32) for t in totals]
    landed = _exchange("swap_sums", totals, lands, copies, [])
    return [jnp.stack([jnp.where(c == 0, t, l), jnp.where(c == 0, l, t)]) for t, l in zip(totals, landed)], small_sum


def kernel(x, conv_norm_g, conv_w_in, conv_w, conv_w_out, attn_norm_g, attn_w_in, attn_b_f, attn_q_norm_g, attn_k_norm_g, attn_w_out, loss_target, m_conv_norm_g, m_conv_w_in, m_conv_w, m_conv_w_out, m_attn_norm_g, m_attn_w_in, m_attn_b_f, m_attn_q_norm_g, m_attn_k_norm_g, m_attn_w_out, v_conv_norm_g, v_conv_w_in, v_conv_w, v_conv_w_out, v_attn_norm_g, v_attn_w_in, v_attn_b_f, v_attn_q_norm_g, v_attn_k_norm_g, v_attn_w_out):
    xi, yi, _ = _coords()
    chip = 2 * xi + yi
    D = x.shape[2]
    H = D // HEAD_DIM
    names = ["conv_norm_g", "conv_w_in", "conv_w", "conv_w_out", "attn_norm_g", "attn_w_in", "attn_b_f",
             "attn_q_norm_g", "attn_k_norm_g", "attn_w_out"]
    weights = dict(zip(names, [conv_norm_g, conv_w_in, conv_w, conv_w_out, attn_norm_g, attn_w_in, attn_b_f,
                               attn_q_norm_g, attn_k_norm_g, attn_w_out]))
    m_in = dict(zip(names, [m_conv_norm_g, m_conv_w_in, m_conv_w, m_conv_w_out, m_attn_norm_g, m_attn_w_in,
                            m_attn_b_f, m_attn_q_norm_g, m_attn_k_norm_g, m_attn_w_out]))
    v_in = dict(zip(names, [v_conv_norm_g, v_conv_w_in, v_conv_w, v_conv_w_out, v_attn_norm_g, v_attn_w_in,
                            v_attn_b_f, v_attn_q_norm_g, v_attn_k_norm_g, v_attn_w_out]))
    weights = {k: w[0] for k, w in weights.items()}
    m_in = {k: w[0] for k, w in m_in.items()}
    v_in = {k: w[0] for k, w in v_in.items()}

    big_names = ["conv_w_in", "attn_w_in", "conv_w_out", "attn_w_out"]
    halved = [weights[k].astype(BF16).reshape(2, weights[k].shape[0] // 2, weights[k].shape[1]) for k in big_names]
    q = D // 4
    small_w = jnp.concatenate([weights["conv_w"], weights["attn_norm_g"][None, :], jnp.zeros((4, q), F32)], axis=0)
    g_in, ga_in, g_out, ga_out, g_small = _all_gather(halved, [small_w])
    w_in = g_in.reshape(4, D, D)
    wa_in = ga_in.reshape(4, D, D + H // 4).transpose(1, 0, 2).reshape(D, 4 * D + H)
    w_out = g_out.reshape(D, D)
    wa_out = ga_out.reshape(D, D)
    conv_w_full = g_small[:, 0:3, :].transpose(1, 0, 2).reshape(3, D)
    attn_g_full = g_small[:, 3, :].reshape(1, D)

    loss_part, grad_x, grads = _local_step(x[0], loss_target[0], weights["conv_norm_g"][None, :], w_in, conv_w_full,
                                           w_out, attn_g_full, wa_in, weights["attn_b_f"][None, :],
                                           weights["attn_q_norm_g"][None, :], weights["attn_k_norm_g"][None, :], wa_out)
    loss = lax.psum(loss_part, ("x", "y", "c"))

    big = [grads["conv_w_in"].reshape(4, 2, D // 2, D),
           grads["attn_w_in"].reshape(D, 4, D + H // 4).transpose(1, 0, 2).reshape(4, 2, D // 2, D + H // 4),
           grads["conv_w_out"].reshape(4, 2, D // 8, D), grads["attn_w_out"].reshape(4, 2, D // 8, D)]
    tail = jnp.concatenate([grads["attn_b_f"], grads["attn_q_norm_g"], grads["attn_k_norm_g"],
                            jnp.zeros((1, D - H - 2 * HEAD_DIM), F32)], axis=1)
    small = jnp.concatenate([grads["conv_norm_g"], grads["conv_w"], grads["attn_norm_g"], tail,
                             jnp.zeros((2, D), F32)], axis=0)
    reduced, small_sum = _reduce_gradients(big, small)
    final = {k: r.reshape(weights[k].shape) for k, r in zip(big_names, reduced)}
    final["conv_norm_g"] = small_sum[0]
    final["conv_w"] = lax.dynamic_slice_in_dim(small_sum[1:4], chip * q, q, axis=1)
    final["attn_norm_g"] = lax.dynamic_slice_in_dim(small_sum[4], chip * q, q, axis=0)
    final["attn_b_f"] = small_sum[5, :H]
    final["attn_q_norm_g"] = small_sum[5, H:H + HEAD_DIM]
    final["attn_k_norm_g"] = small_sum[5, H + HEAD_DIM:H + 2 * HEAD_DIM]

    delta, new_m, new_v = {}, {}, {}
    for k in names:
        shape = weights[k].shape
        as2d = (lambda a: a.reshape(1, -1)) if len(shape) == 1 else (lambda a: a)
        d, m2, v2 = _adamw(as2d(weights[k]), as2d(final[k]), as2d(m_in[k]), as2d(v_in[k]), "adamw_" + k)
        delta[k], new_m[k], new_v[k] = d.reshape(shape), m2.reshape(shape), v2.reshape(shape)
    lead = lambda a: a[None]
    return (loss, grad_x[None], *[lead(final[k]) for k in names], *[lead(delta[k]) for k in names],
            *[lead(new_m[k]) for k in names], *[lead(new_v[k]) for k in names])
```

```python
import functools

import jax
import jax.numpy as jnp
from jax import lax
from jax.experimental import pallas as pl
from jax.experimental.pallas import tpu as pltpu

F32 = jnp.float32
BF16 = jnp.bfloat16
HEAD_DIM = 64
LANES = 128
RMS_EPS = 1e-6
NEG = -1e30
Q_SCALE = 0.125
ROW_TILE = 256
ATT_GROUP = 4
SKIP_LOG = 106.0
PLAIN_EXP_MAX = 60.0
TN_ROWS = 1024
VMEM_LIMIT = 56 << 20
ADAM_LR, ADAM_B1, ADAM_B2, ADAM_EPS, ADAM_WD, ADAM_STEP = 0.001, 0.9, 0.999, 1e-08, 0.01, 10
MESH = pl.DeviceIdType.MESH
ANY = pl.BlockSpec(memory_space=pl.ANY)


def _lane():
    return lax.broadcasted_iota(jnp.int32, (1, LANES), 1)


def _split3(x):
    hi = x.astype(BF16).astype(F32)
    r = x - hi
    mid = r.astype(BF16).astype(F32)
    lo = (r - mid).astype(BF16).astype(F32)
    return hi, mid, lo


def _put(base, lane, start, parts):
    for j, p in enumerate(parts):
        base = jnp.where(lane == start + j, p, base)
    return base


def _col(x, lane, idx):
    return jnp.sum(jnp.where(lane == idx, x, 0.0), axis=1, keepdims=True)


def _feat(parity):
    return HEAD_DIM * parity


def _aug(parity):
    return HEAD_DIM * (1 - parity)


def _own(lane, parity):
    return (lane >= _feat(parity)) & (lane < _feat(parity) + HEAD_DIM)


def _head_tile(ref, hd, lane):
    j = hd // 2
    return jnp.where(_own(lane, hd % 2), ref[:, LANES * j:LANES * (j + 1)], 0.0)


def _pair_tile(even, odd, lane):
    return jnp.where(lane < HEAD_DIM, even, odd)


def _sigmoid(x):
    return 1.0 / (1.0 + jnp.exp(-x))


def _dot(a, b):
    return jnp.dot(a, b, preferred_element_type=F32)


def _dot_nt(a, b):
    return lax.dot_general(a, b, (((1,), (1,)), ((), ())), preferred_element_type=F32)


def _dot_tn(a, b):
    return lax.dot_general(a, b, (((0,), (0,)), ((), ())), preferred_element_type=F32)


def _dot01(tri, x):
    hi, mid, lo = _split3(x)
    return _dot(tri, hi.astype(BF16)) + _dot(tri, mid.astype(BF16)) + _dot(tri, lo.astype(BF16))


def _rms_bwd(dh, x, g):
    inv = lax.rsqrt(jnp.mean(x * x, axis=-1, keepdims=True) + RMS_EPS)
    xh = x * inv
    dxn = dh * g
    dx = inv * (dxn - xh * jnp.mean(dxn * xh, axis=-1, keepdims=True))
    return dx, jnp.sum(dh * xh, axis=0, keepdims=True)


def _head_rms_bwd(dn, t, g):
    inv = lax.rsqrt(jnp.sum(t * t, axis=1, keepdims=True) * (1.0 / HEAD_DIM) + RMS_EPS)
    th = t * inv
    gd = dn * g
    d = inv * (gd - th * (jnp.sum(gd * th, axis=1, keepdims=True) * (1.0 / HEAD_DIM)))
    return d, jnp.sum(dn * th, axis=0, keepdims=True)


def _params(n_grid):
    return pltpu.CompilerParams(dimension_semantics=("arbitrary",) * n_grid, vmem_limit_bytes=VMEM_LIMIT)


def _rows(tm, cols, rev=None):
    if rev is None:
        return pl.BlockSpec((tm, cols), lambda i: (i, 0))
    return pl.BlockSpec((tm, cols), lambda i: (rev - i, 0))


def _whole(shape):
    return pl.BlockSpec(shape, lambda *_: (0,) * len(shape))


def _conv_fwd(x, g1, w_in, conv_w, w_out):
    S, D = x.shape
    tm = min(ROW_TILE, S)

    def body(x_ref, g_ref, win_ref, cw_ref, wout_ref, proj_ref, h_ref, yc_ref, y_ref, x1_ref, prev_u):
        i = pl.program_id(0)
        xv = x_ref[...]
        inv = lax.rsqrt(jnp.mean(xv * xv, axis=-1, keepdims=True) + RMS_EPS)
        h = (xv * inv * g_ref[...]).astype(BF16)
        h_ref[...] = h
        for j in range(4):
            proj_ref[:, j * D:(j + 1) * D] = _dot(h, win_ref[j])

        @pl.when(i == 0)
        def _():
            prev_u[...] = jnp.zeros((tm, D), F32)

        u = proj_ref[:, D:2 * D] * proj_ref[:, 2 * D:3 * D]
        pu = prev_u[...]
        row = lax.broadcasted_iota(jnp.int32, (tm, 1), 0)
        u1 = jnp.where(row < 1, pltpu.roll(pu, 1, 0), pltpu.roll(u, 1, 0))
        u2 = jnp.where(row < 2, pltpu.roll(pu, 2, 0), pltpu.roll(u, 2, 0))
        prev_u[...] = u
        w = cw_ref[...]
        yc = w[2:3] * u + w[1:2] * u1 + w[0:1] * u2
        yc_ref[...] = yc
        z = proj_ref[:, 3 * D:4 * D]
        y = (proj_ref[:, 0:D] * yc * (z * _sigmoid(z))).astype(BF16)
        y_ref[...] = y
        x1_ref[...] = xv + _dot(y, wout_ref[...])

    return pl.pallas_call(
        body, name="conv_fwd", grid=(S // tm,),
        in_specs=[_rows(tm, D), _whole((1, D)), _whole((4, D, D)), _whole((3, D)), _whole((D, D))],
        out_specs=[_rows(tm, 4 * D), _rows(tm, D), _rows(tm, D), _rows(tm, D), _rows(tm, D)],
        out_shape=[jax.ShapeDtypeStruct((S, 4 * D), F32), jax.ShapeDtypeStruct((S, D), BF16),
                   jax.ShapeDtypeStruct((S, D), F32), jax.ShapeDtypeStruct((S, D), BF16),
                   jax.ShapeDtypeStruct((S, D), F32)],
        scratch_shapes=[pltpu.VMEM((tm, D), F32)],
        compiler_params=_params(1),
    )(x, g1, w_in, conv_w, w_out)


def _conv_bwd(dx1, x, g1, w_in, w_out, conv_w, proj, yc):
    S, D = x.shape
    tm = min(ROW_TILE, S)
    last = S // tm - 1

    def body(dx1_ref, x_ref, g_ref, win_ref, wout_ref, cw_ref, proj_ref, yc_ref,
             dproj_ref, dx_ref, dx1b_ref, dg_ref, dcw_ref, next_d):
        @pl.when(pl.program_id(0) == 0)
        def _():
            dg_ref[...] = jnp.zeros((1, D), F32)
            dcw_ref[...] = jnp.zeros((3, D), F32)
            next_d[...] = jnp.zeros((tm, D), F32)

        dx1v = dx1_ref[...]
        dx1b = dx1v.astype(BF16)
        dx1b_ref[...] = dx1b
        dy = _dot_nt(dx1b, wout_ref[...])
        b = proj_ref[:, 0:D]
        c = proj_ref[:, D:2 * D]
        xin = proj_ref[:, 2 * D:3 * D]
        z = proj_ref[:, 3 * D:4 * D]
        sg = _sigmoid(z)
        sz = z * sg
        ycv = yc_ref[...]
        d0 = dy * b * sz
        dproj_ref[:, 0:D] = (dy * ycv * sz).astype(BF16)
        dproj_ref[:, 3 * D:4 * D] = (dy * b * ycv * (sg * (1.0 + z * (1.0 - sg)))).astype(BF16)
        nd = next_d[...]
        row = lax.broadcasted_iota(jnp.int32, (tm, 1), 0)
        d1 = jnp.where(row >= tm - 1, pltpu.roll(nd, tm - 1, 0), pltpu.roll(d0, tm - 1, 0))
        d2 = jnp.where(row >= tm - 2, pltpu.roll(nd, tm - 2, 0), pltpu.roll(d0, tm - 2, 0))
        next_d[...] = d0
        w = cw_ref[...]
        du = w[2:3] * d0 + w[1:2] * d1 + w[0:1] * d2
        u = c * xin
        dcw_ref[2:3, :] += jnp.sum(d0 * u, axis=0, keepdims=True)
        dcw_ref[1:2, :] += jnp.sum(d1 * u, axis=0, keepdims=True)
        dcw_ref[0:1, :] += jnp.sum(d2 * u, axis=0, keepdims=True)
        dproj_ref[:, D:2 * D] = (du * xin).astype(BF16)
        dproj_ref[:, 2 * D:3 * D] = (du * c).astype(BF16)
        dh = _dot_nt(dproj_ref[:, 0:D], win_ref[0])
        for j in range(1, 4):
            dh = dh + _dot_nt(dproj_ref[:, j * D:(j + 1) * D], win_ref[j])
        dxn, dg = _rms_bwd(dh, x_ref[...], g_ref[...])
        dx_ref[...] = dx1v + dxn
        dg_ref[...] += dg

    return pl.pallas_call(
        body, name="conv_bwd", grid=(S // tm,),
        in_specs=[_rows(tm, D, last), _rows(tm, D, last), _whole((1, D)), _whole((4, D, D)), _whole((D, D)),
                  _whole((3, D)), _rows(tm, 4 * D, last), _rows(tm, D, last)],
        out_specs=[_rows(tm, 4 * D, last), _rows(tm, D, last), _rows(tm, D, last), _whole((1, D)), _whole((3, D))],
        out_shape=[jax.ShapeDtypeStruct((S, 4 * D), BF16), jax.ShapeDtypeStruct((S, D), F32),
                   jax.ShapeDtypeStruct((S, D), BF16), jax.ShapeDtypeStruct((1, D), F32),
                   jax.ShapeDtypeStruct((3, D), F32)],
        scratch_shapes=[pltpu.VMEM((tm, D), F32)],
        compiler_params=_params(1),
    )(dx1, x, g1, w_in, w_out, conv_w, proj, yc)


def _attn_front(x1, g2, w, wf, bf, gq, gk):
    S, D = x1.shape
    H = D // HEAD_DIM
    tm = min(ROW_TILE, S)
    tri = (lax.broadcasted_iota(jnp.int32, (tm, tm), 1) <= lax.broadcasted_iota(jnp.int32, (tm, tm), 0)).astype(BF16)

    def body(x_ref, g_ref, w_ref, wf_ref, bf_ref, gq_ref, gk_ref, tri_ref,
             h_ref, qh_ref, kh_ref, z_ref, f_ref, c_ref, qa_ref, ka_ref, va_ref, vt_ref, carry, v_s, qraw_ref, kraw_ref):
        @pl.when(pl.program_id(0) == 0)
        def _():
            carry[...] = jnp.zeros((8, LANES), F32)

        xv = x_ref[...]
        inv = lax.rsqrt(jnp.mean(xv * xv, axis=-1, keepdims=True) + RMS_EPS)
        h = (xv * inv * g_ref[...]).astype(BF16)
        h_ref[...] = h
        qraw_ref[...] = _dot(h, w_ref[:, 0:D])
        kraw_ref[...] = _dot(h, w_ref[:, D:2 * D])
        v_s[...] = _dot(h, w_ref[:, 2 * D:3 * D])
        z_ref[...] = _dot(h, w_ref[:, 3 * D:4 * D])
        lane = _lane()
        f = _dot(h, wf_ref[...]) + bf_ref[...]
        f_ref[...] = f
        logf = jnp.where(lane < H, jnp.minimum(f, 0.0) - jnp.log(1.0 + jnp.exp(-jnp.abs(f))), 0.0)
        cs = _dot01(tri_ref[...], logf) + carry[0:1, :]
        c_ref[...] = cs
        carry[...] = jnp.broadcast_to(cs[tm - 1:tm, :], (8, LANES))
        for hd in range(H):
            sl = slice(LANES * hd, LANES * (hd + 1))
            a = _aug(hd % 2)
            one_a = (lane >= a) & (lane < a + 3)
            one_b = (lane >= a + 3) & (lane < a + 6)
            ch = _col(cs, lane, hd)
            qt = _head_tile(qraw_ref, hd, lane)
            qh_ref[hd] = qt
            qn = qt * lax.rsqrt(jnp.sum(qt * qt, axis=1, keepdims=True) * (1.0 / HEAD_DIM) + RMS_EPS) * gq_ref[...]
            kt = _head_tile(kraw_ref, hd, lane)
            kh_ref[hd] = kt
            kn = kt * lax.rsqrt(jnp.sum(kt * kt, axis=1, keepdims=True) * (1.0 / HEAD_DIM) + RMS_EPS) * gk_ref[...]
            diag = jnp.sum(qn * kn, axis=1, keepdims=True) * Q_SCALE
            qa = _put(jnp.where(one_b, 1.0, qn * Q_SCALE), lane, a, _split3(ch - diag))
            qa_ref[:, sl] = qa.astype(BF16)
            ka = _put(jnp.where(one_a, 1.0, kn), lane, a + 3, _split3(-ch))
            ka_ref[:, sl] = ka.astype(BF16)
            va = jnp.where(one_a, 1.0, _head_tile(v_s, hd, lane))
            va_ref[:, sl] = va.astype(BF16)
            vt_ref[hd] = va.T.astype(BF16)

    nb = S // tm
    heads = pl.BlockSpec((H, tm, LANES), lambda i: (0, i, 0))
    return pl.pallas_call(
        body, name="attn_front", grid=(nb,),
        in_specs=[_rows(tm, D), _whole((1, D)), _whole((D, 4 * D)), _whole((D, LANES)), _whole((1, LANES)),
                  _whole((1, LANES)), _whole((1, LANES)), _whole((tm, tm))],
        out_specs=[_rows(tm, D), heads, heads, _rows(tm, D), _rows(tm, LANES), _rows(tm, LANES),
                   _rows(tm, H * LANES), _rows(tm, H * LANES), _rows(tm, H * LANES),
                   pl.BlockSpec((H, None, LANES, tm), lambda i: (0, i, 0, 0))],
        out_shape=[jax.ShapeDtypeStruct((S, D), BF16), jax.ShapeDtypeStruct((H, S, LANES), F32),
                   jax.ShapeDtypeStruct((H, S, LANES), F32), jax.ShapeDtypeStruct((S, D), F32),
                   jax.ShapeDtypeStruct((S, LANES), F32), jax.ShapeDtypeStruct((S, LANES), F32),
                   jax.ShapeDtypeStruct((S, H * LANES), BF16), jax.ShapeDtypeStruct((S, H * LANES), BF16),
                   jax.ShapeDtypeStruct((S, H * LANES), BF16), jax.ShapeDtypeStruct((H, nb, LANES, tm), BF16)],
        scratch_shapes=[pltpu.VMEM((8, LANES), F32), pltpu.VMEM((tm, D), F32), pltpu.VMEM((tm, D), F32),
                        pltpu.VMEM((tm, D), F32)],
        compiler_params=_params(1),
    )(x1, g2, w, wf, bf, gq, gk, tri)


def _skip_tables(c, gq, gk, T, G):
    nb = c.shape[0] // T
    bound = 8.0 * jnp.max(jnp.abs(gq)) * jnp.max(jnp.abs(gk))
    first, last = c[0::T, :], c[T - 1::T, :]
    idx = jnp.arange(nb)
    need = (last[None, :, :] <= first[:, None, :] + (SKIP_LOG + 2.0 * bound)) & (idx[None, :, None] < idx[:, None, None])
    need = need | (idx[None, :, None] == idx[:, None, None])
    kstart = jnp.argmax(need, axis=1)
    qend = nb - 1 - jnp.argmax(need[::-1], axis=0)
    kstart = jnp.min(kstart.reshape(nb // G, G, -1), axis=1)
    qend = jnp.max(qend.reshape(nb // G, G, -1), axis=1)
    return kstart.T.astype(jnp.int32), qend.T.astype(jnp.int32), bound


def _attn_fwd(kstart, qa, ka, vt, online_max):
    S = qa.shape[0]
    H = qa.shape[1] // LANES
    nb, T = vt.shape[1], vt.shape[3]
    G = nb // kstart.shape[1]
    W = G * T

    def finish(acc, shift, o_ref):
        a = _aug(pl.program_id(0) % 2)
        feat = lax.broadcasted_iota(jnp.int32, (LANES, 1), 0)
        l = jnp.sum(jnp.where(feat == a, acc, 0.0), axis=0, keepdims=True)
        o_ref[...] = jnp.where(feat == a + 1, shift + jnp.log(l), acc / l).T

    def causal(st):
        return jnp.where(lax.broadcasted_iota(jnp.int32, st.shape, 0) <= lax.broadcasted_iota(jnp.int32, st.shape, 1),
                         st, NEG)

    def fast_body(ks_ref, q_ref, k_ref, vt_ref, o_ref, acc_ref, sa_ref, sb_ref):
        h, g = pl.program_id(0), pl.program_id(1)
        q = q_ref[...]
        acc_ref[...] = jnp.zeros((LANES, W), F32)

        def scores(ki, lo):
            return _dot_nt(k_ref[pl.ds(pl.multiple_of(ki * T, T), 2 * T), :], q[lo * T:, :])

        def weighted(ki, p):
            return _dot(vt_ref[ki], p[:T]) + _dot(vt_ref[ki + 1], p[T:])

        first = ks_ref[h, g]
        first = first - ((g * G - first) & 1)
        steps = (g * G - first) // 2
        sa_ref[...] = scores(first, 0)

        def advance(ki, cur_ref, next_ref):
            p = jnp.exp(cur_ref[...]).astype(BF16)
            next_ref[...] = scores(ki + 2, 0)
            acc_ref[...] += weighted(ki, p)

        def loop(i, carry):
            advance(first + 4 * i, sa_ref, sb_ref)
            advance(first + 4 * i + 2, sb_ref, sa_ref)
            return carry

        lax.fori_loop(0, steps // 2, loop, 0)

        @pl.when(steps % 2 == 1)
        def _():
            advance(g * G - 2, sa_ref, sb_ref)
            acc_ref[...] += weighted(g * G, jnp.exp(causal(sb_ref[...])).astype(BF16))

        @pl.when(steps % 2 == 0)
        def _():
            acc_ref[...] += weighted(g * G, jnp.exp(causal(sa_ref[...])).astype(BF16))

        for j in range(2, G, 2):
            p = jnp.exp(causal(scores(g * G + j, j))).astype(BF16)
            acc_ref[:, j * T:] += weighted(g * G + j, p)
        finish(acc_ref[...], 0.0, o_ref)

    def online_body(ks_ref, q_ref, k_ref, vt_ref, o_ref, acc_ref, m_ref):
        h, g = pl.program_id(0), pl.program_id(1)
        q = q_ref[...]
        m_ref[...] = jnp.full((8, W), NEG, F32)
        acc_ref[...] = jnp.zeros((LANES, W), F32)

        def update(st, vtb, lo):
            m_old = m_ref[0:1, lo:]
            m_new = jnp.maximum(m_old, jnp.max(st, axis=0, keepdims=True))
            p = jnp.exp(st - m_new).astype(BF16)
            acc_ref[:, lo:] = acc_ref[:, lo:] * jnp.exp(m_old - m_new) + _dot(vtb, p)
            m_ref[:, lo:] = jnp.broadcast_to(m_new, (8, W - lo))

        def loop(ki, carry):
            kb = k_ref[pl.ds(pl.multiple_of(ki * T, T), T), :]
            update(_dot_nt(kb, q), vt_ref[ki], 0)
            return carry

        lax.fori_loop(ks_ref[h, g], g * G, loop, 0)
        for j in range(G):
            ki = g * G + j
            kb = k_ref[pl.ds(pl.multiple_of(ki * T, T), T), :]
            update(causal(_dot_nt(kb, q[j * T:, :])), vt_ref[ki], j * T)
        finish(acc_ref[...], m_ref[0:1, :], o_ref)

    return pl.pallas_call(
        online_body if online_max else fast_body, name="attn_fwd_online" if online_max else "attn_fwd",
        grid_spec=pltpu.PrefetchScalarGridSpec(
            num_scalar_prefetch=1, grid=(H, nb // G),
            in_specs=[pl.BlockSpec((W, LANES), lambda h, i, ks: (i, h)),
                      pl.BlockSpec((S, LANES), lambda h, i, ks: (0, h)),
                      pl.BlockSpec((None, nb, LANES, T), lambda h, i, ks: (h, 0, 0, 0))],
            out_specs=pl.BlockSpec((W, LANES), lambda h, i, ks: (i, h)),
            scratch_shapes=[pltpu.VMEM((LANES, W), F32)] + (
                [pltpu.VMEM((8, W), F32)] if online_max else [pltpu.VMEM((2 * T, W), F32)] * 2)),
        out_shape=jax.ShapeDtypeStruct((S, H * LANES), F32),
        compiler_params=_params(2),
    )(kstart, qa, ka, vt)


def _attn_out(o_aug, z, x1, target, w_out, qa):
    S, D = x1.shape
    H = D // HEAD_DIM
    tm = min(ROW_TILE, S)

    def body(o_ref, z_ref, x1_ref, t_ref, w_ref, q_ref,
             dx2_ref, dx2b_ref, o2b_ref, dz_ref, doa_ref, qa2_ref, loss_ref, oc_s, do_s):
        @pl.when(pl.program_id(0) == 0)
        def _():
            loss_ref[...] = jnp.zeros((1, LANES), F32)

        lane = _lane()
        lses = []
        for j in range(H // 2):
            oe = o_ref[:, 2 * LANES * j:2 * LANES * j + LANES]
            oo = o_ref[:, 2 * LANES * j + LANES:2 * LANES * (j + 1)]
            lses += [_col(oe, lane, _aug(0) + 1), _col(oo, lane, _aug(1) + 1)]
            oc_s[:, LANES * j:LANES * (j + 1)] = _pair_tile(oe, oo, lane)
        oc = oc_s[...]
        zv = z_ref[...]
        sg = _sigmoid(zv)
        sz = zv * sg
        o2 = (oc * sz).astype(BF16)
        o2b_ref[...] = o2
        e = x1_ref[...] + _dot(o2, w_ref[...]) - t_ref[...]
        sq = jnp.sum(jnp.sum(e * e, axis=1, keepdims=True), axis=0, keepdims=True)
        loss_ref[...] += jnp.broadcast_to(sq * (0.5 / D), (1, LANES))
        dx2 = e * (1.0 / D)
        dx2_ref[...] = dx2
        dx2b = dx2.astype(BF16)
        dx2b_ref[...] = dx2b
        do2 = _dot_nt(dx2b, w_ref[...])
        dz_ref[...] = do2 * oc * (sg * (1.0 + zv * (1.0 - sg)))
        do_s[...] = do2 * sz
        for hd in range(H):
            sl = slice(LANES * hd, LANES * (hd + 1))
            a = _aug(hd % 2)
            one_a = (lane >= a) & (lane < a + 3)
            dt = _head_tile(do_s, hd, lane)
            delta = jnp.sum(dt * _head_tile(oc_s, hd, lane), axis=1, keepdims=True)
            doa_ref[:, sl] = _put(dt, lane, a, _split3(-delta)).astype(BF16)
            qt = q_ref[:, sl].astype(F32)
            rq = jnp.sum(jnp.where(one_a, qt, 0.0), axis=1, keepdims=True) - lses[hd]
            qa2_ref[:, sl] = _put(qt, lane, a, _split3(rq)).astype(BF16)

    return pl.pallas_call(
        body, name="attn_out", grid=(S // tm,),
        in_specs=[_rows(tm, H * LANES), _rows(tm, D), _rows(tm, D), _rows(tm, D), _whole((D, D)),
                  _rows(tm, H * LANES)],
        out_specs=[_rows(tm, D), _rows(tm, D), _rows(tm, D), _rows(tm, D), _rows(tm, H * LANES),
                   _rows(tm, H * LANES), _whole((1, LANES))],
        out_shape=[jax.ShapeDtypeStruct((S, D), F32), jax.ShapeDtypeStruct((S, D), BF16),
                   jax.ShapeDtypeStruct((S, D), BF16), jax.ShapeDtypeStruct((S, D), F32),
                   jax.ShapeDtypeStruct((S, H * LANES), BF16), jax.ShapeDtypeStruct((S, H * LANES), BF16),
                   jax.ShapeDtypeStruct((1, LANES), F32)],
        scratch_shapes=[pltpu.VMEM((tm, D), F32), pltpu.VMEM((tm, D), F32)],
        compiler_params=_params(1),
    )(o_aug, z, x1, target, w_out, qa)


def _attn_bwd(qend, qa2, doa, ka, va, T):
    S = qa2.shape[0]
    H = qa2.shape[1] // LANES
    nb = S // T
    G = nb // qend.shape[1]
    W = G * T

    def body(qe_ref, q_ref, do_ref, k_ref, v_ref, dq_ref, dk_ref, dv_ref, dk_acc, dv_acc):
        h, g = pl.program_id(0), pl.program_id(1)

        @pl.when(g == 0)
        def _():
            dq_ref[...] = jnp.zeros((nb, LANES, T), F32)

        kb = k_ref[...]
        vb = v_ref[...]
        kt = kb.astype(F32).T.astype(BF16)
        dk_acc[...] = jnp.zeros((W, LANES), F32)
        dv_acc[...] = jnp.zeros((W, LANES), F32)

        def step(qi, rows, masked):
            q0 = pl.multiple_of(qi * T, T)
            qb = q_ref[pl.ds(q0, 2 * T), :]
            dob = do_ref[pl.ds(q0, 2 * T), :]
            st = _dot_nt(kb[:rows], qb)
            if masked:
                key = lax.broadcasted_iota(jnp.int32, st.shape, 0)
                st = jnp.where(key <= lax.broadcasted_iota(jnp.int32, st.shape, 1) + (rows - 2 * T), st, NEG)
            p = jnp.exp(st)
            ds = (p * _dot_nt(vb[:rows], dob)).astype(BF16)
            dv_acc[0:rows, :] += _dot(p.astype(BF16), dob)
            dk_acc[0:rows, :] += _dot(ds, qb)
            dqt = _dot(kt[:, :rows], ds)
            dq_ref[qi] += dqt[:, :T]
            dq_ref[qi + 1] += dqt[:, T:]

        def loop(i, carry):
            step(g * G + G + 2 * i, W, False)
            return carry

        for j in range(0, G, 2):
            step(g * G + j, (j + 2) * T, True)
        lax.fori_loop(0, (qe_ref[h, g] - (g * G + G) + 2) // 2, loop, 0)
        dk_ref[...] = dk_acc[...]
        dv_ref[...] = dv_acc[...]

    return pl.pallas_call(
        body, name="attn_bwd",
        grid_spec=pltpu.PrefetchScalarGridSpec(
            num_scalar_prefetch=1, grid=(H, nb // G),
            in_specs=[pl.BlockSpec((S, LANES), lambda h, i, qe: (0, h)), pl.BlockSpec((S, LANES), lambda h, i, qe: (0, h)),
                      pl.BlockSpec((W, LANES), lambda h, i, qe: (i, h)), pl.BlockSpec((W, LANES), lambda h, i, qe: (i, h))],
            out_specs=[pl.BlockSpec((None, nb, LANES, T), lambda h, i, qe: (h, 0, 0, 0)),
                       pl.BlockSpec((None, W, LANES), lambda h, i, qe: (h, i, 0)),
                       pl.BlockSpec((None, W, LANES), lambda h, i, qe: (h, i, 0))],
            scratch_shapes=[pltpu.VMEM((W, LANES), F32), pltpu.VMEM((W, LANES), F32)]),
        out_shape=[jax.ShapeDtypeStruct((H, nb, LANES, T), F32), jax.ShapeDtypeStruct((H, S, LANES), F32),
                   jax.ShapeDtypeStruct((H, S, LANES), F32)],
        compiler_params=_params(2),
    )(qend, qa2, doa, ka, va)


def _attn_proj_bwd(dqt, dka, dva, qraw, kraw, dz, f, gq, gk, w, wf, x1, g2, dx2):
    S, D = x1.shape
    H = D // HEAD_DIM
    tm = dqt.shape[3]
    last = S // tm - 1
    tri = (lax.broadcasted_iota(jnp.int32, (tm, tm), 1) >= lax.broadcasted_iota(jnp.int32, (tm, tm), 0)).astype(BF16)

    def body(dqt_ref, dk_ref, dv_ref, q_ref, k_ref, dz_ref, f_ref, gq_ref, gk_ref, w_ref, wf_ref, x1_ref, g2_ref,
             dx2_ref, tri_ref, dproj_ref, dx1_ref, dg2_ref, small_ref, carry, pairs):
        @pl.when(pl.program_id(0) == 0)
        def _():
            dg2_ref[...] = jnp.zeros((1, D), F32)
            small_ref[...] = jnp.zeros((8, LANES), F32)
            carry[...] = jnp.zeros((8, LANES), F32)

        lane = _lane()

        def head_pair(j, acc):
            dcs, dgq, dgk = acc
            dq2, dk2, dv2 = [], [], []
            for parity in (0, 1):
                hd = 2 * j + parity
                own, a = _own(lane, parity), _aug(parity)
                dqf = dqt_ref[hd].T
                dqn = jnp.where(own, dqf * Q_SCALE, 0.0)
                d, dg = _head_rms_bwd(dqn, q_ref[hd], gq_ref[...])
                dq2.append(d)
                dgq = dgq + dg
                dkt = dk_ref[hd]
                dcs = dcs + jnp.where(lane == hd, _col(dqf, lane, a) - _col(dkt, lane, a + 3), 0.0)
                d, dg = _head_rms_bwd(jnp.where(own, dkt, 0.0), k_ref[hd], gk_ref[...])
                dk2.append(d)
                dgk = dgk + dg
                dv2.append(dv_ref[hd])
            for part, pair in enumerate((dq2, dk2, dv2)):
                pairs[part, j] = _pair_tile(*pair, lane).astype(BF16)
            return dcs, dgq, dgk

        zero = jnp.zeros((1, LANES), F32)
        dcs, dgq, dgk = lax.fori_loop(0, H // 2, head_pair, (jnp.zeros((tm, LANES), F32), zero, zero))
        for part in range(3):
            for j in range(H // 2):
                dproj_ref[:, part * D + LANES * j:part * D + LANES * (j + 1)] = pairs[part, j]
        dproj_ref[:, 3 * D:4 * D] = dz_ref[...].astype(BF16)
        dlogf = _dot01(tri_ref[...], dcs) + carry[0:1, :]
        carry[...] = jnp.broadcast_to(dlogf[0:1, :], (8, LANES))
        df = dlogf * (1.0 / (1.0 + jnp.exp(f_ref[...])))
        dproj_ref[:, 4 * D:4 * D + LANES] = df.astype(BF16)
        small_ref[0:1, :] += jnp.sum(df, axis=0, keepdims=True)
        small_ref[1:2, :] += dgq
        small_ref[2:3, :] += dgk
        dh = _dot_nt(dproj_ref[:, 0:4 * D], w_ref[...]) + _dot_nt(dproj_ref[:, 4 * D:4 * D + LANES], wf_ref[...])
        dxn, dg = _rms_bwd(dh, x1_ref[...], g2_ref[...])
        dx1_ref[...] = dx2_ref[...] + dxn
        dg2_ref[...] += dg

    W = 4 * D + LANES
    heads = pl.BlockSpec((H, tm, LANES), lambda i: (0, last - i, 0))
    return pl.pallas_call(
        body, name="attn_proj_bwd", grid=(S // tm,),
        in_specs=[pl.BlockSpec((H, None, LANES, tm), lambda i: (0, last - i, 0, 0)),
                  heads, heads, heads, heads,
                  _rows(tm, D, last), _rows(tm, LANES, last), _whole((1, LANES)), _whole((1, LANES)),
                  _whole((D, 4 * D)), _whole((D, LANES)), _rows(tm, D, last), _whole((1, D)), _rows(tm, D, last),
                  _whole((tm, tm))],
        out_specs=[_rows(tm, W, last), _rows(tm, D, last), _whole((1, D)), _whole((8, LANES))],
        out_shape=[jax.ShapeDtypeStruct((S, W), BF16), jax.ShapeDtypeStruct((S, D), F32),
                   jax.ShapeDtypeStruct((1, D), F32), jax.ShapeDtypeStruct((8, LANES), F32)],
        scratch_shapes=[pltpu.VMEM((8, LANES), F32), pltpu.VMEM((3, H // 2, tm, LANES), BF16)],
        compiler_params=_params(1),
    )(dqt, dka, dva, qraw, kraw, dz, f, gq, gk, w, wf, x1, g2, dx2, tri)


def _matmul_tn(a, b, col0, n, tn, name, stacked=False):
    S, M = a.shape
    ts = min(TN_ROWS, S)
    off = col0 // tn

    def body(a_ref, b_ref, o_ref):
        @pl.when(pl.program_id(1) == 0)
        def _():
            o_ref[...] = jnp.zeros((M, tn), F32)

        o_ref[...] += _dot_tn(a_ref[...], b_ref[...])

    if stacked:
        out_spec, out_shape = pl.BlockSpec((None, M, tn), lambda j, s: (j, 0, 0)), (n // tn, M, tn)
    else:
        out_spec, out_shape = pl.BlockSpec((M, tn), lambda j, s: (0, j)), (M, n)
    return pl.pallas_call(
        body, name=name, grid=(n // tn, S // ts),
        in_specs=[pl.BlockSpec((ts, M), lambda j, s: (s, 0)), pl.BlockSpec((ts, tn), lambda j, s: (s, off + j))],
        out_specs=out_spec, out_shape=jax.ShapeDtypeStruct(out_shape, F32),
        compiler_params=_params(2),
    )(a, b)


def _adamw(w, g, m, v, name):
    r, c = w.shape
    tr = ROW_TILE if r % ROW_TILE == 0 else r

    def body(w_ref, g_ref, m_ref, v_ref, d_ref, m2_ref, v2_ref):
        gv = g_ref[...]
        m2 = ADAM_B1 * m_ref[...] + (1.0 - ADAM_B1) * gv
        v2 = ADAM_B2 * v_ref[...] + (1.0 - ADAM_B2) * (gv * gv)
        m2_ref[...] = m2
        v2_ref[...] = v2
        m_hat = m2 / (1.0 - ADAM_B1 ** ADAM_STEP)
        v_hat = v2 / (1.0 - ADAM_B2 ** ADAM_STEP)
        d_ref[...] = -ADAM_LR * (m_hat / (jnp.sqrt(v_hat) + ADAM_EPS) + ADAM_WD * w_ref[...])

    spec = _rows(tr, c)
    return pl.pallas_call(
        body, name=name, grid=(r // tr,), in_specs=[spec] * 4, out_specs=[spec] * 3,
        out_shape=[jax.ShapeDtypeStruct((r, c), F32)] * 3, compiler_params=_params(1),
    )(w, g, m, v)


def _local_step(x, target, g1, w_in, conv_w, w_out, g2, wa_in, b_f, gq, gk, wa_out):
    S, D = x.shape
    H = D // HEAD_DIM
    w_qkvz = wa_in[:, :4 * D]
    wf = jnp.pad(wa_in[:, 4 * D:], ((0, 0), (0, LANES - H)))
    bf = jnp.pad(b_f, ((0, 0), (0, LANES - H)))
    gq128 = jnp.concatenate([gq, gq], axis=1)
    gk128 = jnp.concatenate([gk, gk], axis=1)

    proj, h1, yc, y, x1 = _conv_fwd(x, g1, w_in, conv_w, w_out)
    h2, qraw, kraw, z, f, c, qa, ka, va, vt = _attn_front(x1, g2, w_qkvz, wf, bf, gq128, gk128)
    T = vt.shape[3]
    kstart, qend, bound = _skip_tables(c[:, :H], gq, gk, T, min(ATT_GROUP, S // T))
    o_aug = lax.cond(2.0 * bound <= PLAIN_EXP_MAX, functools.partial(_attn_fwd, online_max=False),
                     functools.partial(_attn_fwd, online_max=True), kstart, qa, ka, vt)
    dx2, dx2b, o2b, dz, doa, qa2, loss = _attn_out(o_aug, z, x1, target, wa_out, qa)
    dqt, dka, dva = _attn_bwd(qend, qa2, doa, ka, va, T)
    dproj2, dx1, dg2, small = _attn_proj_bwd(dqt, dka, dva, qraw, kraw, dz, f, gq128, gk128, w_qkvz, wf, x1, g2, dx2)
    dproj1, dx, dx1b, dg1, dcw = _conv_bwd(dx1, x, g1, w_in, w_out, conv_w, proj, yc)

    tn = min(1024, D)
    dwa_out = _matmul_tn(o2b, dx2b, 0, D, tn, "dw_attn_out")
    dwa_in = jnp.concatenate([_matmul_tn(h2, dproj2, 0, 4 * D, tn, "dw_attn_in"),
                              _matmul_tn(h2, dproj2, 4 * D, LANES, LANES, "dw_attn_f")[:, :H]], axis=1)
    dw_out = _matmul_tn(y, dx1b, 0, D, tn, "dw_conv_out")
    dw_in = _matmul_tn(h1, dproj1, 0, 4 * D, D, "dw_conv_in", stacked=True)
    grads = dict(conv_norm_g=dg1, conv_w_in=dw_in, conv_w=dcw, conv_w_out=dw_out, attn_norm_g=dg2,
                 attn_w_in=dwa_in, attn_b_f=small[0:1, :H],
                 attn_q_norm_g=small[1:2, :HEAD_DIM] + small[1:2, HEAD_DIM:],
                 attn_k_norm_g=small[2:3, :HEAD_DIM] + small[2:3, HEAD_DIM:], attn_w_out=dwa_out)
    return loss[0, 0], dx, grads


def _coords():
    return lax.axis_index("x"), lax.axis_index("y"), lax.axis_index("c")


def _at(ref, idx):
    return ref.at[idx] if idx else ref


def _other_chips(x, y):
    return [(1 - x, y), (x, 1 - y), (1 - x, 1 - y)]


def _all_gather(halved, whole):
    nh, nw = len(halved), len(whole)

    def body(*refs):
        src_h, src_w = refs[:nh], refs[nh:nh + nw]
        out_h, out_w = refs[nh + nw:2 * nh + nw], refs[2 * nh + nw:2 * (nh + nw)]
        send_h, recv_h, send_w, recv_w = refs[2 * (nh + nw):]
        x, y, c = _coords()
        mine = 2 * x + y
        sibling = (x, y, 1 - c)
        chips = _other_chips(x, y)

        def copy_h(a, k, chip, half, to, src=None):
            dst = out_h[a].at[chip, half]
            return pltpu.make_async_remote_copy(src_ref=dst if src is None else src, dst_ref=dst,
                                                send_sem=send_h.at[a, k], recv_sem=recv_h.at[a, k],
                                                device_id=to, device_id_type=MESH)

        def copy_w(a, k, chip, to):
            return pltpu.make_async_remote_copy(src_ref=src_w[a], dst_ref=out_w[a].at[chip],
                                                send_sem=send_w.at[a, k], recv_sem=recv_w.at[a, k],
                                                device_id=to, device_id_type=MESH)

        first = [copy_h(a, k, mine, c, (*chip, c), src=src_h[a].at[c]) for a in range(nh) for k, chip in enumerate(chips)]
        first += [copy_w(a, k, mine, (*chip, c)) for a in range(nw) for k, chip in enumerate(chips)]
        for cp in first:
            cp.start()
        passed = []
        for a in range(nh):
            for k, (px, py) in enumerate(chips):
                copy_h(a, k, 2 * px + py, c, (x, y, c)).wait_recv()
                cp = copy_h(a, 3 + k, 2 * px + py, c, sibling)
                cp.start()
                passed.append(cp)
        for a in range(nh):
            for k, (px, py) in enumerate(chips):
                copy_h(a, 3 + k, 2 * px + py, 1 - c, (x, y, c)).wait_recv()
        for a in range(nw):
            for k, (px, py) in enumerate(chips):
                copy_w(a, k, 2 * px + py, (x, y, c)).wait_recv()
        for cp in first + passed:
            cp.wait_send()

    out_shape = [jax.ShapeDtypeStruct((4,) + a.shape, a.dtype) for a in list(halved) + list(whole)]
    gathered = pl.pallas_call(
        body, name="gather_weights", in_specs=[ANY] * (nh + nw), out_specs=[ANY] * (nh + nw), out_shape=out_shape,
        scratch_shapes=[pltpu.SemaphoreType.DMA((nh, 6)), pltpu.SemaphoreType.DMA((nh, 6)),
                        pltpu.SemaphoreType.DMA((nw, 3)), pltpu.SemaphoreType.DMA((nw, 3))],
    )(*halved, *whole)
    x, y, _ = _coords()
    return [lax.dynamic_update_index_in_dim(g, a, 2 * x + y, axis=0) for g, a in zip(gathered, list(halved) + list(whole))]


def _exchange(name, srcs, lands, copies, local_copies):
    ns, nl, n, nloc = len(srcs), len(lands), len(copies), len(local_copies)

    def body(*refs):
        src, land = refs[:ns], refs[ns:ns + nl]
        send, recv, local_sem = refs[ns + nl:]
        me = _coords()
        started = []
        for k, (si, s_at, li, l_at, ci) in enumerate(local_copies):
            cp = pltpu.make_async_copy(_at(src[si], s_at(*me)), _at(land[li], l_at(*me)), local_sem.at[k])
            cp.start()
            started.append(cp)
        remote = []
        for k, (si, s_at, li, l_at, peer) in enumerate(copies):
            cp = pltpu.make_async_remote_copy(src_ref=_at(src[si], s_at(*me)), dst_ref=_at(land[li], l_at(*me)),
                                              send_sem=send.at[k], recv_sem=recv.at[k],
                                              device_id=peer(*me), device_id_type=MESH)
            cp.start()
            remote.append(cp)
        for cp in remote:
            cp.wait()
        for cp in started:
            cp.wait()

    return pl.pallas_call(
        body, name=name, in_specs=[ANY] * ns, out_specs=[ANY] * nl, out_shape=list(lands),
        scratch_shapes=[pltpu.SemaphoreType.DMA((n,)), pltpu.SemaphoreType.DMA((n,)),
                        pltpu.SemaphoreType.DMA((max(nloc, 1),))],
    )(*srcs)


def _add_pairs(a, b, name):
    _, r, cols = a.shape
    tr = ROW_TILE if r % ROW_TILE == 0 else r

    def body(a_ref, b_ref, o_ref, ob_ref):
        s = a_ref[...] + b_ref[...]
        o_ref[...] = s
        ob_ref[...] = s.astype(BF16)

    spec = pl.BlockSpec((None, tr, cols), lambda j, i: (j, i, 0))
    return pl.pallas_call(
        body, name=name, grid=(4, r // tr), in_specs=[spec, spec], out_specs=[spec, spec],
        out_shape=[jax.ShapeDtypeStruct(a.shape, F32), jax.ShapeDtypeStruct(a.shape, BF16)],
        compiler_params=_params(2),
    )(a, b)


def _sum_chips(own, landed, name):
    _, r, cols = landed.shape
    tr = ROW_TILE if r % ROW_TILE == 0 else r

    def body(own_ref, land_ref, o_ref):
        acc = own_ref[...]
        for j in range(3):
            acc = acc + land_ref[j].astype(F32)
        o_ref[...] = acc

    return pl.pallas_call(
        body, name=name, grid=(r // tr,),
        in_specs=[_rows(tr, cols), pl.BlockSpec((3, tr, cols), lambda i: (0, i, 0))], out_specs=_rows(tr, cols),
        out_shape=jax.ShapeDtypeStruct((r, cols), F32), compiler_params=_params(1),
    )(own, landed)


def _sum_devices(landed, name):
    def body(l_ref, o_ref):
        acc = l_ref[0]
        for j in range(1, 8):
            acc = acc + l_ref[j]
        o_ref[...] = acc

    return pl.pallas_call(body, name=name, out_shape=jax.ShapeDtypeStruct(landed.shape[1:], F32))(landed)


def _reduce_gradients(big, small):
    nb = len(big)
    x, y, c = _coords()
    mine = 2 * x + y
    flips = [(fx, fy, fc) for fx in (0, 1) for fy in (0, 1) for fc in (0, 1) if fx or fy or fc]

    def flip(fx, fy, fc):
        return lambda x, y, c: (x ^ fx, y ^ fy, c ^ fc)

    copies = [(a, (lambda j: lambda x, y, c: (j, 1 - c))(j), a, (lambda j: lambda x, y, c: (j,))(j), flip(0, 0, 1))
              for a in range(nb) for j in range(4)]
    copies += [(nb, lambda x, y, c: (), nb, lambda x, y, c: (4 * x + 2 * y + c,), flip(*f)) for f in flips]
    lands = [jax.ShapeDtypeStruct((4,) + g.shape[2:], F32) for g in big] + [jax.ShapeDtypeStruct((8,) + small.shape, F32)]
    local = [(nb, lambda x, y, c: (), nb, lambda x, y, c: (4 * x + 2 * y + c,), None)]
    landed = _exchange("swap_halves", list(big) + [small], lands, copies, local)
    small_sum = _sum_devices(landed[nb], "sum_small")

    chip_f32, chip_bf16 = [], []
    for a in range(nb):
        kept = lax.dynamic_index_in_dim(big[a], c, axis=1, keepdims=False)
        s, sb = _add_pairs(kept, landed[a], f"add_cores_{a}")
        chip_f32.append(s)
        chip_bf16.append(sb)

    chip_flips = [(1, 0), (0, 1), (1, 1)]
    copies = [(a, (lambda f: lambda x, y, c: (2 * (x ^ f[0]) + (y ^ f[1]),))(f), a, (lambda k: lambda x, y, c: (k,))(k),
               flip(f[0], f[1], 0)) for a in range(nb) for k, f in enumerate(chip_flips)]
    lands = [jax.ShapeDtypeStruct((3,) + g.shape[1:], BF16) for g in chip_bf16]
    landed = _exchange("send_chip_sums", chip_bf16, lands, copies, [])
    totals = [_sum_chips(lax.dynamic_index_in_dim(chip_f32[a], mine, axis=0, keepdims=False), landed[a],
                         f"sum_chips_{a}") for a in range(nb)]

    copies = [(a, lambda x, y, c: (), a, lambda x, y, c: (), flip(0, 0, 1)) for a in range(nb)]
    lands = [jax.ShapeDtypeStruct(t.shape, F32) for t in totals]
    landed = _exchange("swap_sums", totals, lands, copies, [])
    return [jnp.stack([jnp.where(c == 0, t, l), jnp.where(c == 0, l, t)]) for t, l in zip(totals, landed)], small_sum


def kernel(x, conv_norm_g, conv_w_in, conv_w, conv_w_out, attn_norm_g, attn_w_in, attn_b_f, attn_q_norm_g, attn_k_norm_g, attn_w_out, loss_target, m_conv_norm_g, m_conv_w_in, m_conv_w, m_conv_w_out, m_attn_norm_g, m_attn_w_in, m_attn_b_f, m_attn_q_norm_g, m_attn_k_norm_g, m_attn_w_out, v_conv_norm_g, v_conv_w_in, v_conv_w, v_conv_w_out, v_attn_norm_g, v_attn_w_in, v_attn_b_f, v_attn_q_norm_g, v_attn_k_norm_g, v_attn_w_out):
    xi, yi, _ = _coords()
    chip = 2 * xi + yi
    D = x.shape[2]
    H = D // HEAD_DIM
    names = ["conv_norm_g", "conv_w_in", "conv_w", "conv_w_out", "attn_norm_g", "attn_w_in", "attn_b_f",
             "attn_q_norm_g", "attn_k_norm_g", "attn_w_out"]
    weights = dict(zip(names, [conv_norm_g, conv_w_in, conv_w, conv_w_out, attn_norm_g, attn_w_in, attn_b_f,
                               attn_q_norm_g, attn_k_norm_g, attn_w_out]))
    m_in = dict(zip(names, [m_conv_norm_g, m_conv_w_in, m_conv_w, m_conv_w_out, m_attn_norm_g, m_attn_w_in,
                            m_attn_b_f, m_attn_q_norm_g, m_attn_k_norm_g, m_attn_w_out]))
    v_in = dict(zip(names, [v_conv_norm_g, v_conv_w_in, v_conv_w, v_conv_w_out, v_attn_norm_g, v_attn_w_in,
                            v_attn_b_f, v_attn_q_norm_g, v_attn_k_norm_g, v_attn_w_out]))
    weights = {k: w[0] for k, w in weights.items()}
    m_in = {k: w[0] for k, w in m_in.items()}
    v_in = {k: w[0] for k, w in v_in.items()}

    big_names = ["conv_w_in", "attn_w_in", "conv_w_out", "attn_w_out"]
    halved = [weights[k].astype(BF16).reshape(2, weights[k].shape[0] // 2, weights[k].shape[1]) for k in big_names]
    q = D // 4
    small_w = jnp.concatenate([weights["conv_w"], weights["attn_norm_g"][None, :], jnp.zeros((4, q), F32)], axis=0)
    g_in, ga_in, g_out, ga_out, g_small = _all_gather(halved, [small_w])
    w_in = g_in.reshape(4, D, D)
    wa_in = ga_in.reshape(4, D, D + H // 4).transpose(1, 0, 2).reshape(D, 4 * D + H)
    w_out = g_out.reshape(D, D)
    wa_out = ga_out.reshape(D, D)
    conv_w_full = g_small[:, 0:3, :].transpose(1, 0, 2).reshape(3, D)
    attn_g_full = g_small[:, 3, :].reshape(1, D)

    loss_part, grad_x, grads = _local_step(x[0], loss_target[0], weights["conv_norm_g"][None, :], w_in, conv_w_full,
                                           w_out, attn_g_full, wa_in, weights["attn_b_f"][None, :],
                                           weights["attn_q_norm_g"][None, :], weights["attn_k_norm_g"][None, :], wa_out)
    loss = lax.psum(loss_part, ("x", "y", "c"))

    big = [grads["conv_w_in"].reshape(4, 2, D // 2, D),
           grads["attn_w_in"].reshape(D, 4, D + H // 4).transpose(1, 0, 2).reshape(4, 2, D // 2, D + H // 4),
           grads["conv_w_out"].reshape(4, 2, D // 8, D), grads["attn_w_out"].reshape(4, 2, D // 8, D)]
    tail = jnp.concatenate([grads["attn_b_f"], grads["attn_q_norm_g"], grads["attn_k_norm_g"],
                            jnp.zeros((1, D - H - 2 * HEAD_DIM), F32)], axis=1)
    small = jnp.concatenate([grads["conv_norm_g"], grads["conv_w"], grads["attn_norm_g"], tail,
                             jnp.zeros((2, D), F32)], axis=0)
    reduced, small_sum = _reduce_gradients(big, small)
    final = {k: r.reshape(weights[k].shape) for k, r in zip(big_names, reduced)}
    final["conv_norm_g"] = small_sum[0]
    final["conv_w"] = lax.dynamic_slice_in_dim(small_sum[1:4], chip * q, q, axis=1)
    final["attn_norm_g"] = lax.dynamic_slice_in_dim(small_sum[4], chip * q, q, axis=0)
    final["attn_b_f"] = small_sum[5, :H]
    final["attn_q_norm_g"] = small_sum[5, H:H + HEAD_DIM]
    final["attn_k_norm_g"] = small_sum[5, H + HEAD_DIM:H + 2 * HEAD_DIM]

    delta, new_m, new_v = {}, {}, {}
    for k in names:
        shape = weights[k].shape
        as2d = (lambda a: a.reshape(1, -1)) if len(shape) == 1 else (lambda a: a)
        d, m2, v2 = _adamw(as2d(weights[k]), as2d(final[k]), as2d(m_in[k]), as2d(v_in[k]), "adamw_" + k)
        delta[k], new_m[k], new_v[k] = d.reshape(shape), m2.reshape(shape), v2.reshape(shape)
    lead = lambda a: a[None]
    return (loss, grad_x[None], *[lead(final[k]) for k in names], *[lead(delta[k]) for k in names],
            *[lead(new_m[k]) for k in names], *[lead(new_v[k]) for k in names])
```

```python
import functools

import jax
import jax.numpy as jnp
from jax import lax
from jax.experimental import pallas as pl
from jax.experimental.pallas import tpu as pltpu

F32 = jnp.float32
BF16 = jnp.bfloat16
HEAD_DIM = 64
LANES = 128
RMS_EPS = 1e-6
NEG = -1e30
Q_SCALE = 0.125
ROW_TILE = 256
ATT_GROUP = 4
SKIP_LOG = 106.0
PLAIN_EXP_MAX = 60.0
TN_ROWS = 1024
VMEM_LIMIT = 56 << 20
ADAM_LR, ADAM_B1, ADAM_B2, ADAM_EPS, ADAM_WD, ADAM_STEP = 0.001, 0.9, 0.999, 1e-08, 0.01, 10
MESH = pl.DeviceIdType.MESH
ANY = pl.BlockSpec(memory_space=pl.ANY)


def _lane():
    return lax.broadcasted_iota(jnp.int32, (1, LANES), 1)


def _split3(x):
    hi = x.astype(BF16).astype(F32)
    r = x - hi
    mid = r.astype(BF16).astype(F32)
    lo = (r - mid).astype(BF16).astype(F32)
    return hi, mid, lo


def _put(base, lane, start, parts):
    for j, p in enumerate(parts):
        base = jnp.where(lane == start + j, p, base)
    return base


def _col(x, lane, idx):
    return jnp.sum(jnp.where(lane == idx, x, 0.0), axis=1, keepdims=True)


def _feat(parity):
    return HEAD_DIM * parity


def _aug(parity):
    return HEAD_DIM * (1 - parity)


def _own(lane, parity):
    return (lane >= _feat(parity)) & (lane < _feat(parity) + HEAD_DIM)


def _head_tile(ref, hd, lane):
    j = hd // 2
    return jnp.where(_own(lane, hd % 2), ref[:, LANES * j:LANES * (j + 1)], 0.0)


def _pair_tile(even, odd, lane):
    return jnp.where(lane < HEAD_DIM, even, odd)


def _sigmoid(x):
    return 1.0 / (1.0 + jnp.exp(-x))


def _dot(a, b):
    return jnp.dot(a, b, preferred_element_type=F32)


def _dot_nt(a, b):
    return lax.dot_general(a, b, (((1,), (1,)), ((), ())), preferred_element_type=F32)


def _dot_tn(a, b):
    return lax.dot_general(a, b, (((0,), (0,)), ((), ())), preferred_element_type=F32)


def _dot01(tri, x):
    hi, mid, lo = _split3(x)
    return _dot(tri, hi.astype(BF16)) + _dot(tri, mid.astype(BF16)) + _dot(tri, lo.astype(BF16))


def _rms_bwd(dh, x, g):
    inv = lax.rsqrt(jnp.mean(x * x, axis=-1, keepdims=True) + RMS_EPS)
    xh = x * inv
    dxn = dh * g
    dx = inv * (dxn - xh * jnp.mean(dxn * xh, axis=-1, keepdims=True))
    return dx, jnp.sum(dh * xh, axis=0, keepdims=True)


def _head_rms_bwd(dn, t, g):
    inv = lax.rsqrt(jnp.sum(t * t, axis=1, keepdims=True) * (1.0 / HEAD_DIM) + RMS_EPS)
    th = t * inv
    gd = dn * g
    d = inv * (gd - th * (jnp.sum(gd * th, axis=1, keepdims=True) * (1.0 / HEAD_DIM)))
    return d, jnp.sum(dn * th, axis=0, keepdims=True)


def _params(n_grid):
    return pltpu.CompilerParams(dimension_semantics=("arbitrary",) * n_grid, vmem_limit_bytes=VMEM_LIMIT)


def _rows(tm, cols, rev=None):
    if rev is None:
        return pl.BlockSpec((tm, cols), lambda i: (i, 0))
    return pl.BlockSpec((tm, cols), lambda i: (rev - i, 0))


def _whole(shape):
    return pl.BlockSpec(shape, lambda *_: (0,) * len(shape))


def _conv_fwd(x, g1, w_in, conv_w, w_out):
    S, D = x.shape
    tm = min(ROW_TILE, S)

    def body(x_ref, g_ref, win_ref, cw_ref, wout_ref, proj_ref, h_ref, yc_ref, y_ref, x1_ref, prev_u):
        i = pl.program_id(0)
        xv = x_ref[...]
        inv = lax.rsqrt(jnp.mean(xv * xv, axis=-1, keepdims=True) + RMS_EPS)
        h = (xv * inv * g_ref[...]).astype(BF16)
        h_ref[...] = h
        for j in range(4):
            proj_ref[:, j * D:(j + 1) * D] = _dot(h, win_ref[j])

        @pl.when(i == 0)
        def _():
            prev_u[...] = jnp.zeros((tm, D), F32)

        u = proj_ref[:, D:2 * D] * proj_ref[:, 2 * D:3 * D]
        pu = prev_u[...]
        row = lax.broadcasted_iota(jnp.int32, (tm, 1), 0)
        u1 = jnp.where(row < 1, pltpu.roll(pu, 1, 0), pltpu.roll(u, 1, 0))
        u2 = jnp.where(row < 2, pltpu.roll(pu, 2, 0), pltpu.roll(u, 2, 0))
        prev_u[...] = u
        w = cw_ref[...]
        yc = w[2:3] * u + w[1:2] * u1 + w[0:1] * u2
        yc_ref[...] = yc
        z = proj_ref[:, 3 * D:4 * D]
        y = (proj_ref[:, 0:D] * yc * (z * _sigmoid(z))).astype(BF16)
        y_ref[...] = y
        x1_ref[...] = xv + _dot(y, wout_ref[...])

    return pl.pallas_call(
        body, name="conv_fwd", grid=(S // tm,),
        in_specs=[_rows(tm, D), _whole((1, D)), _whole((4, D, D)), _whole((3, D)), _whole((D, D))],
        out_specs=[_rows(tm, 4 * D), _rows(tm, D), _rows(tm, D), _rows(tm, D), _rows(tm, D)],
        out_shape=[jax.ShapeDtypeStruct((S, 4 * D), F32), jax.ShapeDtypeStruct((S, D), BF16),
                   jax.ShapeDtypeStruct((S, D), F32), jax.ShapeDtypeStruct((S, D), BF16),
                   jax.ShapeDtypeStruct((S, D), F32)],
        scratch_shapes=[pltpu.VMEM((tm, D), F32)],
        compiler_params=_params(1),
    )(x, g1, w_in, conv_w, w_out)


def _conv_bwd(dx1, x, g1, w_in, w_out, conv_w, proj, yc):
    S, D = x.shape
    tm = min(ROW_TILE, S)
    last = S // tm - 1

    def body(dx1_ref, x_ref, g_ref, win_ref, wout_ref, cw_ref, proj_ref, yc_ref,
             dproj_ref, dx_ref, dx1b_ref, dg_ref, dcw_ref, next_d):
        @pl.when(pl.program_id(0) == 0)
        def _():
            dg_ref[...] = jnp.zeros((1, D), F32)
            dcw_ref[...] = jnp.zeros((3, D), F32)
            next_d[...] = jnp.zeros((tm, D), F32)

        dx1v = dx1_ref[...]
        dx1b = dx1v.astype(BF16)
        dx1b_ref[...] = dx1b
        dy = _dot_nt(dx1b, wout_ref[...])
        b = proj_ref[:, 0:D]
        c = proj_ref[:, D:2 * D]
        xin = proj_ref[:, 2 * D:3 * D]
        z = proj_ref[:, 3 * D:4 * D]
        sg = _sigmoid(z)
        sz = z * sg
        ycv = yc_ref[...]
        d0 = dy * b * sz
        dproj_ref[:, 0:D] = (dy * ycv * sz).astype(BF16)
        dproj_ref[:, 3 * D:4 * D] = (dy * b * ycv * (sg * (1.0 + z * (1.0 - sg)))).astype(BF16)
        nd = next_d[...]
        row = lax.broadcasted_iota(jnp.int32, (tm, 1), 0)
        d1 = jnp.where(row >= tm - 1, pltpu.roll(nd, tm - 1, 0), pltpu.roll(d0, tm - 1, 0))
        d2 = jnp.where(row >= tm - 2, pltpu.roll(nd, tm - 2, 0), pltpu.roll(d0, tm - 2, 0))
        next_d[...] = d0
        w = cw_ref[...]
        du = w[2:3] * d0 + w[1:2] * d1 + w[0:1] * d2
        u = c * xin
        dcw_ref[2:3, :] += jnp.sum(d0 * u, axis=0, keepdims=True)
        dcw_ref[1:2, :] += jnp.sum(d1 * u, axis=0, keepdims=True)
        dcw_ref[0:1, :] += jnp.sum(d2 * u, axis=0, keepdims=True)
        dproj_ref[:, D:2 * D] = (du * xin).astype(BF16)
        dproj_ref[:, 2 * D:3 * D] = (du * c).astype(BF16)
        dh = _dot_nt(dproj_ref[:, 0:D], win_ref[0])
        for j in range(1, 4):
            dh = dh + _dot_nt(dproj_ref[:, j * D:(j + 1) * D], win_ref[j])
        dxn, dg = _rms_bwd(dh, x_ref[...], g_ref[...])
        dx_ref[...] = dx1v + dxn
        dg_ref[...] += dg

    return pl.pallas_call(
        body, name="conv_bwd", grid=(S // tm,),
        in_specs=[_rows(tm, D, last), _rows(tm, D, last), _whole((1, D)), _whole((4, D, D)), _whole((D, D)),
                  _whole((3, D)), _rows(tm, 4 * D, last), _rows(tm, D, last)],
        out_specs=[_rows(tm, 4 * D, last), _rows(tm, D, last), _rows(tm, D, last), _whole((1, D)), _whole((3, D))],
        out_shape=[jax.ShapeDtypeStruct((S, 4 * D), BF16), jax.ShapeDtypeStruct((S, D), F32),
                   jax.ShapeDtypeStruct((S, D), BF16), jax.ShapeDtypeStruct((1, D), F32),
                   jax.ShapeDtypeStruct((3, D), F32)],
        scratch_shapes=[pltpu.VMEM((tm, D), F32)],
        compiler_params=_params(1),
    )(dx1, x, g1, w_in, w_out, conv_w, proj, yc)


def _attn_front(x1, g2, w, wf, bf, gq, gk):
    S, D = x1.shape
    H = D // HEAD_DIM
    tm = min(ROW_TILE, S)
    tri = (lax.broadcasted_iota(jnp.int32, (tm, tm), 1) <= lax.broadcasted_iota(jnp.int32, (tm, tm), 0)).astype(BF16)

    def body(x_ref, g_ref, w_ref, wf_ref, bf_ref, gq_ref, gk_ref, tri_ref,
             h_ref, qh_ref, kh_ref, z_ref, f_ref, c_ref, qa_ref, ka_ref, va_ref, vt_ref, carry, v_s, qraw_ref, kraw_ref):
        @pl.when(pl.program_id(0) == 0)
        def _():
            carry[...] = jnp.zeros((8, LANES), F32)

        xv = x_ref[...]
        inv = lax.rsqrt(jnp.mean(xv * xv, axis=-1, keepdims=True) + RMS_EPS)
        h = (xv * inv * g_ref[...]).astype(BF16)
        h_ref[...] = h
        qraw_ref[...] = _dot(h, w_ref[:, 0:D])
        kraw_ref[...] = _dot(h, w_ref[:, D:2 * D])
        v_s[...] = _dot(h, w_ref[:, 2 * D:3 * D])
        z_ref[...] = _dot(h, w_ref[:, 3 * D:4 * D])
        lane = _lane()
        f = _dot(h, wf_ref[...]) + bf_ref[...]
        f_ref[...] = f
        logf = jnp.where(lane < H, jnp.minimum(f, 0.0) - jnp.log(1.0 + jnp.exp(-jnp.abs(f))), 0.0)
        cs = _dot01(tri_ref[...], logf) + carry[0:1, :]
        c_ref[...] = cs
        carry[...] = jnp.broadcast_to(cs[tm - 1:tm, :], (8, LANES))
        for hd in range(H):
            sl = slice(LANES * hd, LANES * (hd + 1))
            a = _aug(hd % 2)
            one_a = (lane >= a) & (lane < a + 3)
            one_b = (lane >= a + 3) & (lane < a + 6)
            ch = _col(cs, lane, hd)
            qt = _head_tile(qraw_ref, hd, lane)
            qh_ref[hd] = qt
            qn = qt * lax.rsqrt(jnp.sum(qt * qt, axis=1, keepdims=True) * (1.0 / HEAD_DIM) + RMS_EPS) * gq_ref[...]
            kt = _head_tile(kraw_ref, hd, lane)
            kh_ref[hd] = kt
            kn = kt * lax.rsqrt(jnp.sum(kt * kt, axis=1, keepdims=True) * (1.0 / HEAD_DIM) + RMS_EPS) * gk_ref[...]
            diag = jnp.sum(qn * kn, axis=1, keepdims=True) * Q_SCALE
            qa = _put(jnp.where(one_b, 1.0, qn * Q_SCALE), lane, a, _split3(ch - diag))
            qa_ref[:, sl] = qa.astype(BF16)
            ka = _put(jnp.where(one_a, 1.0, kn), lane, a + 3, _split3(-ch))
            ka_ref[:, sl] = ka.astype(BF16)
            va = jnp.where(one_a, 1.0, _head_tile(v_s, hd, lane))
            va_ref[:, sl] = va.astype(BF16)
            vt_ref[hd] = va.T.astype(BF16)

    nb = S // tm
    heads = pl.BlockSpec((H, tm, LANES), lambda i: (0, i, 0))
    return pl.pallas_call(
        body, name="attn_front", grid=(nb,),
        in_specs=[_rows(tm, D), _whole((1, D)), _whole((D, 4 * D)), _whole((D, LANES)), _whole((1, LANES)),
                  _whole((1, LANES)), _whole((1, LANES)), _whole((tm, tm))],
        out_specs=[_rows(tm, D), heads, heads, _rows(tm, D), _rows(tm, LANES), _rows(tm, LANES),
                   _rows(tm, H * LANES), _rows(tm, H * LANES), _rows(tm, H * LANES),
                   pl.BlockSpec((H, None, LANES, tm), lambda i: (0, i, 0, 0))],
        out_shape=[jax.ShapeDtypeStruct((S, D), BF16), jax.ShapeDtypeStruct((H, S, LANES), F32),
                   jax.ShapeDtypeStruct((H, S, LANES), F32), jax.ShapeDtypeStruct((S, D), F32),
                   jax.ShapeDtypeStruct((S, LANES), F32), jax.ShapeDtypeStruct((S, LANES), F32),
                   jax.ShapeDtypeStruct((S, H * LANES), BF16), jax.ShapeDtypeStruct((S, H * LANES), BF16),
                   jax.ShapeDtypeStruct((S, H * LANES), BF16), jax.ShapeDtypeStruct((H, nb, LANES, tm), BF16)],
        scratch_shapes=[pltpu.VMEM((8, LANES), F32), pltpu.VMEM((tm, D), F32), pltpu.VMEM((tm, D), F32),
                        pltpu.VMEM((tm, D), F32)],
        compiler_params=_params(1),
    )(x1, g2, w, wf, bf, gq, gk, tri)


def _skip_tables(c, gq, gk, T, G):
    nb = c.shape[0] // T
    bound = 8.0 * jnp.max(jnp.abs(gq)) * jnp.max(jnp.abs(gk))
    first, last = c[0::T, :], c[T - 1::T, :]
    idx = jnp.arange(nb)
    need = (last[None, :, :] <= first[:, None, :] + (SKIP_LOG + 2.0 * bound)) & (idx[None, :, None] < idx[:, None, None])
    need = need | (idx[None, :, None] == idx[:, None, None])
    kstart = jnp.argmax(need, axis=1)
    qend = nb - 1 - jnp.argmax(need[::-1], axis=0)
    kstart = jnp.min(kstart.reshape(2 * nb // G, G // 2, -1), axis=1)
    kstart = kstart - (kstart & 1)
    qend = jnp.max(qend.reshape(2 * nb // G, G // 2, -1), axis=1)
    return kstart.T.astype(jnp.int32), qend.T.astype(jnp.int32), bound


def _attn_fwd(kstart, qa, ka, vt, online_max):
    S = qa.shape[0]
    H = qa.shape[1] // LANES
    nb, T = vt.shape[1], vt.shape[3]
    G = 2 * nb // kstart.shape[1]
    W = G * T

    def finish(acc, shift, o_ref):
        a = _aug(pl.program_id(0) % 2)
        feat = lax.broadcasted_iota(jnp.int32, (LANES, 1), 0)
        l = jnp.sum(jnp.where(feat == a, acc, 0.0), axis=0, keepdims=True)
        o_ref[...] = jnp.where(feat == a + 1, shift + jnp.log(l), acc / l).T

    def causal(st):
        return jnp.where(lax.broadcasted_iota(jnp.int32, st.shape, 0) <= lax.broadcasted_iota(jnp.int32, st.shape, 1),
                         st, NEG)

    def fast_body(ks_ref, q_ref, k_ref, vt_ref, o_ref, acc_ref, sa_ref, sb_ref):
        h, g = pl.program_id(0), pl.program_id(1)
        q = q_ref[...]
        acc_ref[...] = jnp.zeros((LANES, W), F32)

        def scores(ki, lo):
            return _dot_nt(k_ref[pl.ds(pl.multiple_of(ki * T, T), 2 * T), :], q[lo * T:, :])

        def weighted(ki, p):
            return _dot(vt_ref[ki], p[:T]) + _dot(vt_ref[ki + 1], p[T:])

        first = ks_ref[h, 2 * g + 1]
        early = jnp.minimum(ks_ref[h, 2 * g], first)

        def narrow(i, carry):
            ki = early + 2 * i
            st = _dot_nt(k_ref[pl.ds(pl.multiple_of(ki * T, T), 2 * T), :], q[:W // 2, :])
            acc_ref[:, :W // 2] += weighted(ki, jnp.exp(st).astype(BF16))
            return carry

        lax.fori_loop(0, (first - early) // 2, narrow, 0)
        steps = (g * G - first) // 2
        sa_ref[...] = scores(first, 0)

        def advance(ki, cur_ref, next_ref):
            p = jnp.exp(cur_ref[...]).astype(BF16)
            next_ref[...] = scores(ki + 2, 0)
            acc_ref[...] += weighted(ki, p)

        def loop(i, carry):
            advance(first + 4 * i, sa_ref, sb_ref)
            advance(first + 4 * i + 2, sb_ref, sa_ref)
            return carry

        lax.fori_loop(0, steps // 2, loop, 0)

        @pl.when(steps % 2 == 1)
        def _():
            advance(g * G - 2, sa_ref, sb_ref)
            acc_ref[...] += weighted(g * G, jnp.exp(causal(sb_ref[...])).astype(BF16))

        @pl.when(steps % 2 == 0)
        def _():
            acc_ref[...] += weighted(g * G, jnp.exp(causal(sa_ref[...])).astype(BF16))

        for j in range(2, G, 2):
            p = jnp.exp(causal(scores(g * G + j, j))).astype(BF16)
            acc_ref[:, j * T:] += weighted(g * G + j, p)
        finish(acc_ref[...], 0.0, o_ref)

    def online_body(ks_ref, q_ref, k_ref, vt_ref, o_ref, acc_ref, m_ref):
        h, g = pl.program_id(0), pl.program_id(1)
        q = q_ref[...]
        m_ref[...] = jnp.full((8, W), NEG, F32)
        acc_ref[...] = jnp.zeros((LANES, W), F32)

        def update(st, vtb, lo):
            m_old = m_ref[0:1, lo:]
            m_new = jnp.maximum(m_old, jnp.max(st, axis=0, keepdims=True))
            p = jnp.exp(st - m_new).astype(BF16)
            acc_ref[:, lo:] = acc_ref[:, lo:] * jnp.exp(m_old - m_new) + _dot(vtb, p)
            m_ref[:, lo:] = jnp.broadcast_to(m_new, (8, W - lo))

        def loop(ki, carry):
            kb = k_ref[pl.ds(pl.multiple_of(ki * T, T), T), :]
            update(_dot_nt(kb, q), vt_ref[ki], 0)
            return carry

        lax.fori_loop(jnp.minimum(ks_ref[h, 2 * g], ks_ref[h, 2 * g + 1]), g * G, loop, 0)
        for j in range(G):
            ki = g * G + j
            kb = k_ref[pl.ds(pl.multiple_of(ki * T, T), T), :]
            update(causal(_dot_nt(kb, q[j * T:, :])), vt_ref[ki], j * T)
        finish(acc_ref[...], m_ref[0:1, :], o_ref)

    return pl.pallas_call(
        online_body if online_max else fast_body, name="attn_fwd_online" if online_max else "attn_fwd",
        grid_spec=pltpu.PrefetchScalarGridSpec(
            num_scalar_prefetch=1, grid=(H, nb // G),
            in_specs=[pl.BlockSpec((W, LANES), lambda h, i, ks: (i, h)),
                      pl.BlockSpec((S, LANES), lambda h, i, ks: (0, h)),
                      pl.BlockSpec((None, nb, LANES, T), lambda h, i, ks: (h, 0, 0, 0))],
            out_specs=pl.BlockSpec((W, LANES), lambda h, i, ks: (i, h)),
            scratch_shapes=[pltpu.VMEM((LANES, W), F32)] + (
                [pltpu.VMEM((8, W), F32)] if online_max else [pltpu.VMEM((2 * T, W), F32)] * 2)),
        out_shape=jax.ShapeDtypeStruct((S, H * LANES), F32),
        compiler_params=_params(2),
    )(kstart, qa, ka, vt)


def _attn_out(o_aug, z, x1, target, w_out, qa):
    S, D = x1.shape
    H = D // HEAD_DIM
    tm = min(ROW_TILE, S)

    def body(o_ref, z_ref, x1_ref, t_ref, w_ref, q_ref,
             dx2_ref, dx2b_ref, o2b_ref, dz_ref, doa_ref, qa2_ref, loss_ref, oc_s, do_s):
        @pl.when(pl.program_id(0) == 0)
        def _():
            loss_ref[...] = jnp.zeros((1, LANES), F32)

        lane = _lane()
        lses = []
        for j in range(H // 2):
            oe = o_ref[:, 2 * LANES * j:2 * LANES * j + LANES]
            oo = o_ref[:, 2 * LANES * j + LANES:2 * LANES * (j + 1)]
            lses += [_col(oe, lane, _aug(0) + 1), _col(oo, lane, _aug(1) + 1)]
            oc_s[:, LANES * j:LANES * (j + 1)] = _pair_tile(oe, oo, lane)
        oc = oc_s[...]
        zv = z_ref[...]
        sg = _sigmoid(zv)
        sz = zv * sg
        o2 = (oc * sz).astype(BF16)
        o2b_ref[...] = o2
        e = x1_ref[...] + _dot(o2, w_ref[...]) - t_ref[...]
        sq = jnp.sum(jnp.sum(e * e, axis=1, keepdims=True), axis=0, keepdims=True)
        loss_ref[...] += jnp.broadcast_to(sq * (0.5 / D), (1, LANES))
        dx2 = e * (1.0 / D)
        dx2_ref[...] = dx2
        dx2b = dx2.astype(BF16)
        dx2b_ref[...] = dx2b
        do2 = _dot_nt(dx2b, w_ref[...])
        dz_ref[...] = do2 * oc * (sg * (1.0 + zv * (1.0 - sg)))
        do_s[...] = do2 * sz
        for hd in range(H):
            sl = slice(LANES * hd, LANES * (hd + 1))
            a = _aug(hd % 2)
            one_a = (lane >= a) & (lane < a + 3)
            dt = _head_tile(do_s, hd, lane)
            delta = jnp.sum(dt * _head_tile(oc_s, hd, lane), axis=1, keepdims=True)
            doa_ref[:, sl] = _put(dt, lane, a, _split3(-delta)).astype(BF16)
            qt = q_ref[:, sl].astype(F32)
            rq = jnp.sum(jnp.where(one_a, qt, 0.0), axis=1, keepdims=True) - lses[hd]
            qa2_ref[:, sl] = _put(qt, lane, a, _split3(rq)).astype(BF16)

    return pl.pallas_call(
        body, name="attn_out", grid=(S // tm,),
        in_specs=[_rows(tm, H * LANES), _rows(tm, D), _rows(tm, D), _rows(tm, D), _whole((D, D)),
                  _rows(tm, H * LANES)],
        out_specs=[_rows(tm, D), _rows(tm, D), _rows(tm, D), _rows(tm, D), _rows(tm, H * LANES),
                   _rows(tm, H * LANES), _whole((1, LANES))],
        out_shape=[jax.ShapeDtypeStruct((S, D), F32), jax.ShapeDtypeStruct((S, D), BF16),
                   jax.ShapeDtypeStruct((S, D), BF16), jax.ShapeDtypeStruct((S, D), F32),
                   jax.ShapeDtypeStruct((S, H * LANES), BF16), jax.ShapeDtypeStruct((S, H * LANES), BF16),
                   jax.ShapeDtypeStruct((1, LANES), F32)],
        scratch_shapes=[pltpu.VMEM((tm, D), F32), pltpu.VMEM((tm, D), F32)],
        compiler_params=_params(1),
    )(o_aug, z, x1, target, w_out, qa)


def _attn_bwd(qend, qa2, doa, ka, va, T):
    S = qa2.shape[0]
    H = qa2.shape[1] // LANES
    nb = S // T
    G = 2 * nb // qend.shape[1]
    W = G * T

    def body(qe_ref, q_ref, do_ref, k_ref, v_ref, dq_ref, dk_ref, dv_ref, dkt_acc, dvt_acc):
        h, g = pl.program_id(0), pl.program_id(1)

        @pl.when(g == 0)
        def _():
            dq_ref[...] = jnp.zeros((S, LANES), F32)

        kb = k_ref[...]
        vb = v_ref[...]
        dkt_acc[...] = jnp.zeros((LANES, W), F32)
        dvt_acc[...] = jnp.zeros((LANES, W), F32)

        def step(qi, c0, c1, masked):
            rows = pl.ds(pl.multiple_of(qi * T, T), 2 * T)
            qb = q_ref[rows, :]
            dob = do_ref[rows, :]
            s = _dot_nt(qb, kb[c0:c1])
            if masked:
                query = lax.broadcasted_iota(jnp.int32, s.shape, 0) + (c1 - 2 * T)
                s = jnp.where(lax.broadcasted_iota(jnp.int32, s.shape, 1) <= query, s, NEG)
            p = jnp.exp(s)
            ds = (p * _dot_nt(dob, vb[c0:c1])).astype(BF16)
            dvt_acc[:, c0:c1] += _dot(dob.astype(F32).T.astype(BF16), p.astype(BF16))
            dkt_acc[:, c0:c1] += _dot(qb.astype(F32).T.astype(BF16), ds)
            dq_ref[rows, :] += _dot(ds, kb[c0:c1])

        for m in range(G // 2):
            step(g * G + 2 * m, 0, (m + 1) * 2 * T, True)
        first = g * G + G
        n_all = jnp.maximum((qe_ref[h, 2 * g] - first + 2) // 2, 0)
        second = first + 2 * n_all

        def all_keys(i, carry):
            step(first + 2 * i, 0, W, False)
            return carry

        def late_keys(i, carry):
            step(second + 2 * i, W // 2, W, False)
            return carry

        lax.fori_loop(0, n_all, all_keys, 0)
        lax.fori_loop(0, (qe_ref[h, 2 * g + 1] - second + 2) // 2, late_keys, 0)
        dk_ref[...] = dkt_acc[...].T
        dv_ref[...] = dvt_acc[...].T

    heads = pl.BlockSpec((None, W, LANES), lambda h, i, qe: (h, i, 0))
    return pl.pallas_call(
        body, name="attn_bwd",
        grid_spec=pltpu.PrefetchScalarGridSpec(
            num_scalar_prefetch=1, grid=(H, nb // G),
            in_specs=[pl.BlockSpec((S, LANES), lambda h, i, qe: (0, h)), pl.BlockSpec((S, LANES), lambda h, i, qe: (0, h)),
                      pl.BlockSpec((W, LANES), lambda h, i, qe: (i, h)), pl.BlockSpec((W, LANES), lambda h, i, qe: (i, h))],
            out_specs=[pl.BlockSpec((None, S, LANES), lambda h, i, qe: (h, 0, 0)), heads, heads],
            scratch_shapes=[pltpu.VMEM((LANES, W), F32), pltpu.VMEM((LANES, W), F32)]),
        out_shape=[jax.ShapeDtypeStruct((H, S, LANES), F32)] * 3,
        compiler_params=_params(2),
    )(qend, qa2, doa, ka, va)


def _attn_proj_bwd(dqt, dka, dva, qraw, kraw, dz, f, gq, gk, w, wf, x1, g2, dx2):
    S, D = x1.shape
    H = D // HEAD_DIM
    tm = min(ROW_TILE, S)
    last = S // tm - 1
    tri = (lax.broadcasted_iota(jnp.int32, (tm, tm), 1) >= lax.broadcasted_iota(jnp.int32, (tm, tm), 0)).astype(BF16)

    def body(dq_ref, dk_ref, dv_ref, q_ref, k_ref, dz_ref, f_ref, gq_ref, gk_ref, w_ref, wf_ref, x1_ref, g2_ref,
             dx2_ref, tri_ref, dproj_ref, dx1_ref, dg2_ref, small_ref, carry, pairs):
        @pl.when(pl.program_id(0) == 0)
        def _():
            dg2_ref[...] = jnp.zeros((1, D), F32)
            small_ref[...] = jnp.zeros((8, LANES), F32)
            carry[...] = jnp.zeros((8, LANES), F32)

        lane = _lane()

        def head_pair(j, acc):
            dcs, dgq, dgk = acc
            dq2, dk2, dv2 = [], [], []
            for parity in (0, 1):
                hd = 2 * j + parity
                own, a = _own(lane, parity), _aug(parity)
                dqf = dq_ref[hd]
                dqn = jnp.where(own, dqf * Q_SCALE, 0.0)
                d, dg = _head_rms_bwd(dqn, q_ref[hd], gq_ref[...])
                dq2.append(d)
                dgq = dgq + dg
                dkt = dk_ref[hd]
                dcs = dcs + jnp.where(lane == hd, _col(dqf, lane, a) - _col(dkt, lane, a + 3), 0.0)
                d, dg = _head_rms_bwd(jnp.where(own, dkt, 0.0), k_ref[hd], gk_ref[...])
                dk2.append(d)
                dgk = dgk + dg
                dv2.append(dv_ref[hd])
            for part, pair in enumerate((dq2, dk2, dv2)):
                pairs[part, j] = _pair_tile(*pair, lane).astype(BF16)
            return dcs, dgq, dgk

        zero = jnp.zeros((1, LANES), F32)
        dcs, dgq, dgk = lax.fori_loop(0, H // 2, head_pair, (jnp.zeros((tm, LANES), F32), zero, zero))
        for part in range(3):
            for j in range(H // 2):
                dproj_ref[:, part * D + LANES * j:part * D + LANES * (j + 1)] = pairs[part, j]
        dproj_ref[:, 3 * D:4 * D] = dz_ref[...].astype(BF16)
        dlogf = _dot01(tri_ref[...], dcs) + carry[0:1, :]
        carry[...] = jnp.broadcast_to(dlogf[0:1, :], (8, LANES))
        df = dlogf * (1.0 / (1.0 + jnp.exp(f_ref[...])))
        dproj_ref[:, 4 * D:4 * D + LANES] = df.astype(BF16)
        small_ref[0:1, :] += jnp.sum(df, axis=0, keepdims=True)
        small_ref[1:2, :] += dgq
        small_ref[2:3, :] += dgk
        dh = _dot_nt(dproj_ref[:, 0:4 * D], w_ref[...]) + _dot_nt(dproj_ref[:, 4 * D:4 * D + LANES], wf_ref[...])
        dxn, dg = _rms_bwd(dh, x1_ref[...], g2_ref[...])
        dx1_ref[...] = dx2_ref[...] + dxn
        dg2_ref[...] += dg

    W = 4 * D + LANES
    heads = pl.BlockSpec((H, tm, LANES), lambda i: (0, last - i, 0))
    return pl.pallas_call(
        body, name="attn_proj_bwd", grid=(S // tm,),
        in_specs=[heads, heads, heads, heads, heads,
                  _rows(tm, D, last), _rows(tm, LANES, last), _whole((1, LANES)), _whole((1, LANES)),
                  _whole((D, 4 * D)), _whole((D, LANES)), _rows(tm, D, last), _whole((1, D)), _rows(tm, D, last),
                  _whole((tm, tm))],
        out_specs=[_rows(tm, W, last), _rows(tm, D, last), _whole((1, D)), _whole((8, LANES))],
        out_shape=[jax.ShapeDtypeStruct((S, W), BF16), jax.ShapeDtypeStruct((S, D), F32),
                   jax.ShapeDtypeStruct((1, D), F32), jax.ShapeDtypeStruct((8, LANES), F32)],
        scratch_shapes=[pltpu.VMEM((8, LANES), F32), pltpu.VMEM((3, H // 2, tm, LANES), BF16)],
        compiler_params=_params(1),
    )(dqt, dka, dva, qraw, kraw, dz, f, gq, gk, w, wf, x1, g2, dx2, tri)


def _matmul_tn(a, b, col0, n, tn, name, stacked=False):
    S, M = a.shape
    ts = min(TN_ROWS, S)
    off = col0 // tn

    def body(a_ref, b_ref, o_ref):
        @pl.when(pl.program_id(1) == 0)
        def _():
            o_ref[...] = jnp.zeros((M, tn), F32)

        o_ref[...] += _dot_tn(a_ref[...], b_ref[...])

    if stacked:
        out_spec, out_shape = pl.BlockSpec((None, M, tn), lambda j, s: (j, 0, 0)), (n // tn, M, tn)
    else:
        out_spec, out_shape = pl.BlockSpec((M, tn), lambda j, s: (0, j)), (M, n)
    return pl.pallas_call(
        body, name=name, grid=(n // tn, S // ts),
        in_specs=[pl.BlockSpec((ts, M), lambda j, s: (s, 0)), pl.BlockSpec((ts, tn), lambda j, s: (s, off + j))],
        out_specs=out_spec, out_shape=jax.ShapeDtypeStruct(out_shape, F32),
        compiler_params=_params(2),
    )(a, b)


def _adamw(w, g, m, v, name):
    r, c = w.shape
    tr = ROW_TILE if r % ROW_TILE == 0 else r

    def body(w_ref, g_ref, m_ref, v_ref, d_ref, m2_ref, v2_ref):
        gv = g_ref[...]
        m2 = ADAM_B1 * m_ref[...] + (1.0 - ADAM_B1) * gv
        v2 = ADAM_B2 * v_ref[...] + (1.0 - ADAM_B2) * (gv * gv)
        m2_ref[...] = m2
        v2_ref[...] = v2
        m_hat = m2 / (1.0 - ADAM_B1 ** ADAM_STEP)
        v_hat = v2 / (1.0 - ADAM_B2 ** ADAM_STEP)
        d_ref[...] = -ADAM_LR * (m_hat / (jnp.sqrt(v_hat) + ADAM_EPS) + ADAM_WD * w_ref[...])

    spec = _rows(tr, c)
    return pl.pallas_call(
        body, name=name, grid=(r // tr,), in_specs=[spec] * 4, out_specs=[spec] * 3,
        out_shape=[jax.ShapeDtypeStruct((r, c), F32)] * 3, compiler_params=_params(1),
    )(w, g, m, v)


def _local_step(x, target, g1, w_in, conv_w, w_out, g2, wa_in, b_f, gq, gk, wa_out):
    S, D = x.shape
    H = D // HEAD_DIM
    w_qkvz = wa_in[:, :4 * D]
    wf = jnp.pad(wa_in[:, 4 * D:], ((0, 0), (0, LANES - H)))
    bf = jnp.pad(b_f, ((0, 0), (0, LANES - H)))
    gq128 = jnp.concatenate([gq, gq], axis=1)
    gk128 = jnp.concatenate([gk, gk], axis=1)

    proj, h1, yc, y, x1 = _conv_fwd(x, g1, w_in, conv_w, w_out)
    h2, qraw, kraw, z, f, c, qa, ka, va, vt = _attn_front(x1, g2, w_qkvz, wf, bf, gq128, gk128)
    T = vt.shape[3]
    kstart, qend, bound = _skip_tables(c[:, :H], gq, gk, T, min(ATT_GROUP, S // T))
    o_aug = lax.cond(2.0 * bound <= PLAIN_EXP_MAX, functools.partial(_attn_fwd, online_max=False),
                     functools.partial(_attn_fwd, online_max=True), kstart, qa, ka, vt)
    dx2, dx2b, o2b, dz, doa, qa2, loss = _attn_out(o_aug, z, x1, target, wa_out, qa)
    dqt, dka, dva = _attn_bwd(qend, qa2, doa, ka, va, T)
    dproj2, dx1, dg2, small = _attn_proj_bwd(dqt, dka, dva, qraw, kraw, dz, f, gq128, gk128, w_qkvz, wf, x1, g2, dx2)
    dproj1, dx, dx1b, dg1, dcw = _conv_bwd(dx1, x, g1, w_in, w_out, conv_w, proj, yc)

    tn = min(1024, D)
    dwa_out = _matmul_tn(o2b, dx2b, 0, D, tn, "dw_attn_out")
    dwa_in = jnp.concatenate([_matmul_tn(h2, dproj2, 0, 4 * D, tn, "dw_attn_in"),
                              _matmul_tn(h2, dproj2, 4 * D, LANES, LANES, "dw_attn_f")[:, :H]], axis=1)
    dw_out = _matmul_tn(y, dx1b, 0, D, tn, "dw_conv_out")
    dw_in = _matmul_tn(h1, dproj1, 0, 4 * D, D, "dw_conv_in", stacked=True)
    grads = dict(conv_norm_g=dg1, conv_w_in=dw_in, conv_w=dcw, conv_w_out=dw_out, attn_norm_g=dg2,
                 attn_w_in=dwa_in, attn_b_f=small[0:1, :H],
                 attn_q_norm_g=small[1:2, :HEAD_DIM] + small[1:2, HEAD_DIM:],
                 attn_k_norm_g=small[2:3, :HEAD_DIM] + small[2:3, HEAD_DIM:], attn_w_out=dwa_out)
    return loss[0, 0], dx, grads


def _coords():
    return lax.axis_index("x"), lax.axis_index("y"), lax.axis_index("c")


def _at(ref, idx):
    return ref.at[idx] if idx else ref


def _other_chips(x, y):
    return [(1 - x, y), (x, 1 - y), (1 - x, 1 - y)]


def _all_gather(halved, whole):
    nh, nw = len(halved), len(whole)

    def body(*refs):
        src_h, src_w = refs[:nh], refs[nh:nh + nw]
        out_h, out_w = refs[nh + nw:2 * nh + nw], refs[2 * nh + nw:2 * (nh + nw)]
        send_h, recv_h, send_w, recv_w = refs[2 * (nh + nw):]
        x, y, c = _coords()
        mine = 2 * x + y
        sibling = (x, y, 1 - c)
        chips = _other_chips(x, y)

        def copy_h(a, k, chip, half, to, src=None):
            dst = out_h[a].at[chip, half]
            return pltpu.make_async_remote_copy(src_ref=dst if src is None else src, dst_ref=dst,
                                                send_sem=send_h.at[a, k], recv_sem=recv_h.at[a, k],
                                                device_id=to, device_id_type=MESH)

        def copy_w(a, k, chip, to):
            return pltpu.make_async_remote_copy(src_ref=src_w[a], dst_ref=out_w[a].at[chip],
                                                send_sem=send_w.at[a, k], recv_sem=recv_w.at[a, k],
                                                device_id=to, device_id_type=MESH)

        first = [copy_h(a, k, mine, c, (*chip, c), src=src_h[a].at[c]) for a in range(nh) for k, chip in enumerate(chips)]
        first += [copy_w(a, k, mine, (*chip, c)) for a in range(nw) for k, chip in enumerate(chips)]
        for cp in first:
            cp.start()
        passed = []
        for a in range(nh):
            for k, (px, py) in enumerate(chips):
                copy_h(a, k, 2 * px + py, c, (x, y, c)).wait_recv()
                cp = copy_h(a, 3 + k, 2 * px + py, c, sibling)
                cp.start()
                passed.append(cp)
        for a in range(nh):
            for k, (px, py) in enumerate(chips):
                copy_h(a, 3 + k, 2 * px + py, 1 - c, (x, y, c)).wait_recv()
        for a in range(nw):
            for k, (px, py) in enumerate(chips):
                copy_w(a, k, 2 * px + py, (x, y, c)).wait_recv()
        for cp in first + passed:
            cp.wait_send()

    out_shape = [jax.ShapeDtypeStruct((4,) + a.shape, a.dtype) for a in list(halved) + list(whole)]
    gathered = pl.pallas_call(
        body, name="gather_weights", in_specs=[ANY] * (nh + nw), out_specs=[ANY] * (nh + nw), out_shape=out_shape,
        scratch_shapes=[pltpu.SemaphoreType.DMA((nh, 6)), pltpu.SemaphoreType.DMA((nh, 6)),
                        pltpu.SemaphoreType.DMA((nw, 3)), pltpu.SemaphoreType.DMA((nw, 3))],
    )(*halved, *whole)
    x, y, _ = _coords()
    return [lax.dynamic_update_index_in_dim(g, a, 2 * x + y, axis=0) for g, a in zip(gathered, list(halved) + list(whole))]


def _exchange(name, srcs, lands, copies, local_copies):
    ns, nl, n, nloc = len(srcs), len(lands), len(copies), len(local_copies)

    def body(*refs):
        src, land = refs[:ns], refs[ns:ns + nl]
        send, recv, local_sem = refs[ns + nl:]
        me = _coords()
        started = []
        for k, (si, s_at, li, l_at, ci) in enumerate(local_copies):
            cp = pltpu.make_async_copy(_at(src[si], s_at(*me)), _at(land[li], l_at(*me)), local_sem.at[k])
            cp.start()
            started.append(cp)
        remote = []
        for k, (si, s_at, li, l_at, peer) in enumerate(copies):
            cp = pltpu.make_async_remote_copy(src_ref=_at(src[si], s_at(*me)), dst_ref=_at(land[li], l_at(*me)),
                                              send_sem=send.at[k], recv_sem=recv.at[k],
                                              device_id=peer(*me), device_id_type=MESH)
            cp.start()
            remote.append(cp)
        for cp in remote:
            cp.wait()
        for cp in started:
            cp.wait()

    return pl.pallas_call(
        body, name=name, in_specs=[ANY] * ns, out_specs=[ANY] * nl, out_shape=list(lands),
        scratch_shapes=[pltpu.SemaphoreType.DMA((n,)), pltpu.SemaphoreType.DMA((n,)),
                        pltpu.SemaphoreType.DMA((max(nloc, 1),))],
    )(*srcs)


def _add_pairs(a, b, name):
    _, r, cols = a.shape
    tr = ROW_TILE if r % ROW_TILE == 0 else r

    def body(a_ref, b_ref, o_ref, ob_ref):
        s = a_ref[...] + b_ref[...]
        o_ref[...] = s
        ob_ref[...] = s.astype(BF16)

    spec = pl.BlockSpec((None, tr, cols), lambda j, i: (j, i, 0))
    return pl.pallas_call(
        body, name=name, grid=(4, r // tr), in_specs=[spec, spec], out_specs=[spec, spec],
        out_shape=[jax.ShapeDtypeStruct(a.shape, F32), jax.ShapeDtypeStruct(a.shape, BF16)],
        compiler_params=_params(2),
    )(a, b)


def _sum_chips(own, landed, name):
    _, r, cols = landed.shape
    tr = ROW_TILE if r % ROW_TILE == 0 else r

    def body(own_ref, land_ref, o_ref):
        acc = own_ref[...]
        for j in range(3):
            acc = acc + land_ref[j].astype(F32)
        o_ref[...] = acc

    return pl.pallas_call(
        body, name=name, grid=(r // tr,),
        in_specs=[_rows(tr, cols), pl.BlockSpec((3, tr, cols), lambda i: (0, i, 0))], out_specs=_rows(tr, cols),
        out_shape=jax.ShapeDtypeStruct((r, cols), F32), compiler_params=_params(1),
    )(own, landed)


def _sum_devices(landed, name):
    def body(l_ref, o_ref):
        acc = l_ref[0]
        for j in range(1, 8):
            acc = acc + l_ref[j]
        o_ref[...] = acc

    return pl.pallas_call(body, name=name, out_shape=jax.ShapeDtypeStruct(landed.shape[1:], F32))(landed)


def _reduce_gradients(big, small):
    nb = len(big)
    x, y, c = _coords()
    mine = 2 * x + y
    flips = [(fx, fy, fc) for fx in (0, 1) for fy in (0, 1) for fc in (0, 1) if fx or fy or fc]

    def flip(fx, fy, fc):
        return lambda x, y, c: (x ^ fx, y ^ fy, c ^ fc)

    copies = [(a, (lambda j: lambda x, y, c: (j, 1 - c))(j), a, (lambda j: lambda x, y, c: (j,))(j), flip(0, 0, 1))
              for a in range(nb) for j in range(4)]
    copies += [(nb, lambda x, y, c: (), nb, lambda x, y, c: (4 * x + 2 * y + c,), flip(*f)) for f in flips]
    lands = [jax.ShapeDtypeStruct((4,) + g.shape[2:], F32) for g in big] + [jax.ShapeDtypeStruct((8,) + small.shape, F32)]
    local = [(nb, lambda x, y, c: (), nb, lambda x, y, c: (4 * x + 2 * y + c,), None)]
    landed = _exchange("swap_halves", list(big) + [small], lands, copies, local)
    small_sum = _sum_devices(landed[nb], "sum_small")

    chip_f32, chip_bf16 = [], []
    for a in range(nb):
        kept = lax.dynamic_index_in_dim(big[a], c, axis=1, keepdims=False)
        s, sb = _add_pairs(kept, landed[a], f"add_cores_{a}")
        chip_f32.append(s)
        chip_bf16.append(sb)

    chip_flips = [(1, 0), (0, 1), (1, 1)]
    copies = [(a, (lambda f: lambda x, y, c: (2 * (x ^ f[0]) + (y ^ f[1]),))(f), a, (lambda k: lambda x, y, c: (k,))(k),
               flip(f[0], f[1], 0)) for a in range(nb) for k, f in enumerate(chip_flips)]
    lands = [jax.ShapeDtypeStruct((3,) + g.shape[1:], BF16) for g in chip_bf16]
    landed = _exchange("send_chip_sums", chip_bf16, lands, copies, [])
    totals = [_sum_chips(lax.dynamic_index_in_dim(chip_f32[a], mine, axis=0, keepdims=False), landed[a],
                         f"sum_chips_{a}") for a in range(nb)]

    copies = [(a, lambda x, y, c: (), a, lambda x, y, c: (), flip(0, 0, 1)) for a in range(nb)]
    lands = [jax.ShapeDtypeStruct(t.shape, F32) for t in totals]
    landed = _exchange("swap_sums", totals, lands, copies, [])
    return [jnp.stack([jnp.where(c == 0, t, l), jnp.where(c == 0, l, t)]) for t, l in zip(totals, landed)], small_sum


def kernel(x, conv_norm_g, conv_w_in, conv_w, conv_w_out, attn_norm_g, attn_w_in, attn_b_f, attn_q_norm_g, attn_k_norm_g, attn_w_out, loss_target, m_conv_norm_g, m_conv_w_in, m_conv_w, m_conv_w_out, m_attn_norm_g, m_attn_w_in, m_attn_b_f, m_attn_q_norm_g, m_attn_k_norm_g, m_attn_w_out, v_conv_norm_g, v_conv_w_in, v_conv_w, v_conv_w_out, v_attn_norm_g, v_attn_w_in, v_attn_b_f, v_attn_q_norm_g, v_attn_k_norm_g, v_attn_w_out):
    xi, yi, _ = _coords()
    chip = 2 * xi + yi
    D = x.shape[2]
    H = D // HEAD_DIM
    names = ["conv_norm_g", "conv_w_in", "conv_w", "conv_w_out", "attn_norm_g", "attn_w_in", "attn_b_f",
             "attn_q_norm_g", "attn_k_norm_g", "attn_w_out"]
    weights = dict(zip(names, [conv_norm_g, conv_w_in, conv_w, conv_w_out, attn_norm_g, attn_w_in, attn_b_f,
                               attn_q_norm_g, attn_k_norm_g, attn_w_out]))
    m_in = dict(zip(names, [m_conv_norm_g, m_conv_w_in, m_conv_w, m_conv_w_out, m_attn_norm_g, m_attn_w_in,
                            m_attn_b_f, m_attn_q_norm_g, m_attn_k_norm_g, m_attn_w_out]))
    v_in = dict(zip(names, [v_conv_norm_g, v_conv_w_in, v_conv_w, v_conv_w_out, v_attn_norm_g, v_attn_w_in,
                            v_attn_b_f, v_attn_q_norm_g, v_attn_k_norm_g, v_attn_w_out]))
    weights = {k: w[0] for k, w in weights.items()}
    m_in = {k: w[0] for k, w in m_in.items()}
    v_in = {k: w[0] for k, w in v_in.items()}

    big_names = ["conv_w_in", "attn_w_in", "conv_w_out", "attn_w_out"]
    halved = [weights[k].astype(BF16).reshape(2, weights[k].shape[0] // 2, weights[k].shape[1]) for k in big_names]
    q = D // 4
    small_w = jnp.concatenate([weights["conv_w"], weights["attn_norm_g"][None, :], jnp.zeros((4, q), F32)], axis=0)
    g_in, ga_in, g_out, ga_out, g_small = _all_gather(halved, [small_w])
    w_in = g_in.reshape(4, D, D)
    wa_in = ga_in.reshape(4, D, D + H // 4).transpose(1, 0, 2).reshape(D, 4 * D + H)
    w_out = g_out.reshape(D, D)
    wa_out = ga_out.reshape(D, D)
    conv_w_full = g_small[:, 0:3, :].transpose(1, 0, 2).reshape(3, D)
    attn_g_full = g_small[:, 3, :].reshape(1, D)

    loss_part, grad_x, grads = _local_step(x[0], loss_target[0], weights["conv_norm_g"][None, :], w_in, conv_w_full,
                                           w_out, attn_g_full, wa_in, weights["attn_b_f"][None, :],
                                           weights["attn_q_norm_g"][None, :], weights["attn_k_norm_g"][None, :], wa_out)
    loss = lax.psum(loss_part, ("x", "y", "c"))

    big = [grads["conv_w_in"].reshape(4, 2, D // 2, D),
           grads["attn_w_in"].reshape(D, 4, D + H // 4).transpose(1, 0, 2).reshape(4, 2, D // 2, D + H // 4),
           grads["conv_w_out"].reshape(4, 2, D // 8, D), grads["attn_w_out"].reshape(4, 2, D // 8, D)]
    tail = jnp.concatenate([grads["attn_b_f"], grads["attn_q_norm_g"], grads["attn_k_norm_g"],
                            jnp.zeros((1, D - H - 2 * HEAD_DIM), F32)], axis=1)
    small = jnp.concatenate([grads["conv_norm_g"], grads["conv_w"], grads["attn_norm_g"], tail,
                             jnp.zeros((2, D), F32)], axis=0)
    reduced, small_sum = _reduce_gradients(big, small)
    final = {k: r.reshape(weights[k].shape) for k, r in zip(big_names, reduced)}
    final["conv_norm_g"] = small_sum[0]
    final["conv_w"] = lax.dynamic_slice_in_dim(small_sum[1:4], chip * q, q, axis=1)
    final["attn_norm_g"] = lax.dynamic_slice_in_dim(small_sum[4], chip * q, q, axis=0)
    final["attn_b_f"] = small_sum[5, :H]
    final["attn_q_norm_g"] = small_sum[5, H:H + HEAD_DIM]
    final["attn_k_norm_g"] = small_sum[5, H + HEAD_DIM:H + 2 * HEAD_DIM]

    delta, new_m, new_v = {}, {}, {}
    for k in names:
        shape = weights[k].shape
        as2d = (lambda a: a.reshape(1, -1)) if len(shape) == 1 else (lambda a: a)
        d, m2, v2 = _adamw(as2d(weights[k]), as2d(final[k]), as2d(m_in[k]), as2d(v_in[k]), "adamw_" + k)
        delta[k], new_m[k], new_v[k] = d.reshape(shape), m2.reshape(shape), v2.reshape(shape)
    lead = lambda a: a[None]
    return (loss, grad_x[None], *[lead(final[k]) for k in names], *[lead(delta[k]) for k in names],
            *[lead(new_m[k]) for k in names], *[lead(new_v[k]) for k in names])
```

```python
import functools

import jax
import jax.numpy as jnp
from jax import lax
from jax.experimental import pallas as pl
from jax.experimental.pallas import tpu as pltpu

F32 = jnp.float32
BF16 = jnp.bfloat16
HEAD_DIM = 64
LANES = 128
RMS_EPS = 1e-6
NEG = -1e30
Q_SCALE = 0.125
ROW_TILE = 256
ATT_GROUP = 4
SKIP_LOG = 106.0
PLAIN_EXP_MAX = 60.0
TN_ROWS = 1024
VMEM_LIMIT = 56 << 20
ADAM_LR, ADAM_B1, ADAM_B2, ADAM_EPS, ADAM_WD, ADAM_STEP = 0.001, 0.9, 0.999, 1e-08, 0.01, 10
MESH = pl.DeviceIdType.MESH
ANY = pl.BlockSpec(memory_space=pl.ANY)


def _lane():
    return lax.broadcasted_iota(jnp.int32, (1, LANES), 1)


def _split3(x):
    hi = x.astype(BF16).astype(F32)
    r = x - hi
    mid = r.astype(BF16).astype(F32)
    lo = (r - mid).astype(BF16).astype(F32)
    return hi, mid, lo


STAT_STRIDE = 16
ONE_LANE = 3 * STAT_STRIDE


def _pack3(x, lane, one):
    hi, mid, lo = _split3(x)
    packed = hi + pltpu.roll(mid, STAT_STRIDE, 1) + pltpu.roll(lo, 2 * STAT_STRIDE, 1)
    return jnp.where(lane == ONE_LANE, one, packed).astype(BF16)


def _scatter_matrices(H):
    rows = lax.broadcasted_iota(jnp.int32, (LANES, H * LANES), 0)
    cols = lax.broadcasted_iota(jnp.int32, (LANES, H * LANES), 1)
    head, within = cols // LANES, cols % LANES
    extra = within - HEAD_DIM * (1 - head % 2)
    term = (rows < ONE_LANE) & (rows % STAT_STRIDE == head)
    first = ((term & (extra == rows // STAT_STRIDE)) | ((rows == ONE_LANE) & (extra >= 3) & (extra < 6)))
    second = ((term & (extra - 3 == rows // STAT_STRIDE)) | ((rows == ONE_LANE) & (extra >= 0) & (extra < 3)))
    return first.astype(BF16), second.astype(BF16)


def _gather_matrix(H, lo):
    rows = lax.broadcasted_iota(jnp.int32, (H * LANES, LANES), 0)
    cols = lax.broadcasted_iota(jnp.int32, (H * LANES, LANES), 1)
    extra = rows % LANES - HEAD_DIM * (1 - (rows // LANES) % 2)
    return ((rows // LANES == cols) & (extra >= lo) & (extra < lo + 3)).astype(BF16)


def _put(base, lane, start, parts):
    for j, p in enumerate(parts):
        base = jnp.where(lane == start + j, p, base)
    return base


def _col(x, lane, idx):
    return jnp.sum(jnp.where(lane == idx, x, 0.0), axis=1, keepdims=True)


def _feat(parity):
    return HEAD_DIM * parity


def _aug(parity):
    return HEAD_DIM * (1 - parity)


def _own(lane, parity):
    return (lane >= _feat(parity)) & (lane < _feat(parity) + HEAD_DIM)


def _head_tile(ref, hd, lane):
    j = hd // 2
    return jnp.where(_own(lane, hd % 2), ref[:, LANES * j:LANES * (j + 1)], 0.0)


def _pair_tile(even, odd, lane):
    return jnp.where(lane < HEAD_DIM, even, odd)


def _sigmoid(x):
    return 1.0 / (1.0 + jnp.exp(-x))


def _dot(a, b):
    return jnp.dot(a, b, preferred_element_type=F32)


def _dot_nt(a, b):
    return lax.dot_general(a, b, (((1,), (1,)), ((), ())), preferred_element_type=F32)


def _dot_tn(a, b):
    return lax.dot_general(a, b, (((0,), (0,)), ((), ())), preferred_element_type=F32)


def _dot01(tri, x):
    hi, mid, lo = _split3(x)
    return _dot(tri, hi.astype(BF16)) + _dot(tri, mid.astype(BF16)) + _dot(tri, lo.astype(BF16))


def _rms_bwd(dh, x, g):
    inv = lax.rsqrt(jnp.mean(x * x, axis=-1, keepdims=True) + RMS_EPS)
    xh = x * inv
    dxn = dh * g
    dx = inv * (dxn - xh * jnp.mean(dxn * xh, axis=-1, keepdims=True))
    return dx, jnp.sum(dh * xh, axis=0, keepdims=True)


def _head_rms_bwd(dn, t, g):
    inv = lax.rsqrt(jnp.sum(t * t, axis=1, keepdims=True) * (1.0 / HEAD_DIM) + RMS_EPS)
    th = t * inv
    gd = dn * g
    d = inv * (gd - th * (jnp.sum(gd * th, axis=1, keepdims=True) * (1.0 / HEAD_DIM)))
    return d, jnp.sum(dn * th, axis=0, keepdims=True)


def _params(n_grid):
    return pltpu.CompilerParams(dimension_semantics=("arbitrary",) * n_grid, vmem_limit_bytes=VMEM_LIMIT)


def _rows(tm, cols, rev=None):
    if rev is None:
        return pl.BlockSpec((tm, cols), lambda i: (i, 0))
    return pl.BlockSpec((tm, cols), lambda i: (rev - i, 0))


def _whole(shape):
    return pl.BlockSpec(shape, lambda *_: (0,) * len(shape))


def _conv_fwd(x, g1, w_in, conv_w, w_out):
    S, D = x.shape
    tm = min(ROW_TILE, S)

    def body(x_ref, g_ref, win_ref, cw_ref, wout_ref, proj_ref, h_ref, yc_ref, y_ref, x1_ref, prev_u):
        i = pl.program_id(0)
        xv = x_ref[...]
        inv = lax.rsqrt(jnp.mean(xv * xv, axis=-1, keepdims=True) + RMS_EPS)
        h = (xv * inv * g_ref[...]).astype(BF16)
        h_ref[...] = h
        for j in range(4):
            proj_ref[:, j * D:(j + 1) * D] = _dot(h, win_ref[j])

        @pl.when(i == 0)
        def _():
            prev_u[...] = jnp.zeros((tm, D), F32)

        u = proj_ref[:, D:2 * D] * proj_ref[:, 2 * D:3 * D]
        pu = prev_u[...]
        row = lax.broadcasted_iota(jnp.int32, (tm, 1), 0)
        u1 = jnp.where(row < 1, pltpu.roll(pu, 1, 0), pltpu.roll(u, 1, 0))
        u2 = jnp.where(row < 2, pltpu.roll(pu, 2, 0), pltpu.roll(u, 2, 0))
        prev_u[...] = u
        w = cw_ref[...]
        yc = w[2:3] * u + w[1:2] * u1 + w[0:1] * u2
        yc_ref[...] = yc
        z = proj_ref[:, 3 * D:4 * D]
        y = (proj_ref[:, 0:D] * yc * (z * _sigmoid(z))).astype(BF16)
        y_ref[...] = y
        x1_ref[...] = xv + _dot(y, wout_ref[...])

    return pl.pallas_call(
        body, name="conv_fwd", grid=(S // tm,),
        in_specs=[_rows(tm, D), _whole((1, D)), _whole((4, D, D)), _whole((3, D)), _whole((D, D))],
        out_specs=[_rows(tm, 4 * D), _rows(tm, D), _rows(tm, D), _rows(tm, D), _rows(tm, D)],
        out_shape=[jax.ShapeDtypeStruct((S, 4 * D), F32), jax.ShapeDtypeStruct((S, D), BF16),
                   jax.ShapeDtypeStruct((S, D), F32), jax.ShapeDtypeStruct((S, D), BF16),
                   jax.ShapeDtypeStruct((S, D), F32)],
        scratch_shapes=[pltpu.VMEM((tm, D), F32)],
        compiler_params=_params(1),
    )(x, g1, w_in, conv_w, w_out)


def _conv_bwd(dx1, x, g1, w_in, w_out, conv_w, proj, yc):
    S, D = x.shape
    tm = min(ROW_TILE, S)
    last = S // tm - 1

    def body(dx1_ref, x_ref, g_ref, win_ref, wout_ref, cw_ref, proj_ref, yc_ref,
             dproj_ref, dx_ref, dx1b_ref, dg_ref, dcw_ref, next_d):
        @pl.when(pl.program_id(0) == 0)
        def _():
            dg_ref[...] = jnp.zeros((1, D), F32)
            dcw_ref[...] = jnp.zeros((3, D), F32)
            next_d[...] = jnp.zeros((tm, D), F32)

        dx1v = dx1_ref[...]
        dx1b = dx1v.astype(BF16)
        dx1b_ref[...] = dx1b
        dy = _dot_nt(dx1b, wout_ref[...])
        b = proj_ref[:, 0:D]
        c = proj_ref[:, D:2 * D]
        xin = proj_ref[:, 2 * D:3 * D]
        z = proj_ref[:, 3 * D:4 * D]
        sg = _sigmoid(z)
        sz = z * sg
        ycv = yc_ref[...]
        d0 = dy * b * sz
        dproj_ref[:, 0:D] = (dy * ycv * sz).astype(BF16)
        dproj_ref[:, 3 * D:4 * D] = (dy * b * ycv * (sg * (1.0 + z * (1.0 - sg)))).astype(BF16)
        nd = next_d[...]
        row = lax.broadcasted_iota(jnp.int32, (tm, 1), 0)
        d1 = jnp.where(row >= tm - 1, pltpu.roll(nd, tm - 1, 0), pltpu.roll(d0, tm - 1, 0))
        d2 = jnp.where(row >= tm - 2, pltpu.roll(nd, tm - 2, 0), pltpu.roll(d0, tm - 2, 0))
        next_d[...] = d0
        w = cw_ref[...]
        du = w[2:3] * d0 + w[1:2] * d1 + w[0:1] * d2
        u = c * xin
        dcw_ref[2:3, :] += jnp.sum(d0 * u, axis=0, keepdims=True)
        dcw_ref[1:2, :] += jnp.sum(d1 * u, axis=0, keepdims=True)
        dcw_ref[0:1, :] += jnp.sum(d2 * u, axis=0, keepdims=True)
        dproj_ref[:, D:2 * D] = (du * xin).astype(BF16)
        dproj_ref[:, 2 * D:3 * D] = (du * c).astype(BF16)
        dh = _dot_nt(dproj_ref[:, 0:D], win_ref[0])
        for j in range(1, 4):
            dh = dh + _dot_nt(dproj_ref[:, j * D:(j + 1) * D], win_ref[j])
        dxn, dg = _rms_bwd(dh, x_ref[...], g_ref[...])
        dx_ref[...] = dx1v + dxn
        dg_ref[...] += dg

    return pl.pallas_call(
        body, name="conv_bwd", grid=(S // tm,),
        in_specs=[_rows(tm, D, last), _rows(tm, D, last), _whole((1, D)), _whole((4, D, D)), _whole((D, D)),
                  _whole((3, D)), _rows(tm, 4 * D, last), _rows(tm, D, last)],
        out_specs=[_rows(tm, 4 * D, last), _rows(tm, D, last), _rows(tm, D, last), _whole((1, D)), _whole((3, D))],
        out_shape=[jax.ShapeDtypeStruct((S, 4 * D), BF16), jax.ShapeDtypeStruct((S, D), F32),
                   jax.ShapeDtypeStruct((S, D), BF16), jax.ShapeDtypeStruct((1, D), F32),
                   jax.ShapeDtypeStruct((3, D), F32)],
        scratch_shapes=[pltpu.VMEM((tm, D), F32)],
        compiler_params=_params(1),
    )(dx1, x, g1, w_in, w_out, conv_w, proj, yc)


def _attn_front(x1, g2, w, wf, bf, gq, gk):
    S, D = x1.shape
    H = D // HEAD_DIM
    tm = min(ROW_TILE, S)
    tri = (lax.broadcasted_iota(jnp.int32, (tm, tm), 1) <= lax.broadcasted_iota(jnp.int32, (tm, tm), 0)).astype(BF16)

    def body(x_ref, g_ref, w_ref, wf_ref, bf_ref, gq_ref, gk_ref, tri_ref, first_ref, second_ref,
             h_ref, qh_ref, kh_ref, z_ref, f_ref, c_ref, rel_ref, qa_ref, ka_ref, va_ref, vt_ref,
             carry, v_s, qraw_ref, kraw_ref):
        @pl.when(pl.program_id(0) == 0)
        def _():
            carry[...] = jnp.zeros((8, LANES), F32)

        xv = x_ref[...]
        inv = lax.rsqrt(jnp.mean(xv * xv, axis=-1, keepdims=True) + RMS_EPS)
        h = (xv * inv * g_ref[...]).astype(BF16)
        h_ref[...] = h
        qraw_ref[...] = _dot(h, w_ref[:, 0:D])
        kraw_ref[...] = _dot(h, w_ref[:, D:2 * D])
        v_s[...] = _dot(h, w_ref[:, 2 * D:3 * D])
        z_ref[...] = _dot(h, w_ref[:, 3 * D:4 * D])
        lane = _lane()
        f = _dot(h, wf_ref[...]) + bf_ref[...]
        f_ref[...] = f
        logf = jnp.where(lane < H, jnp.minimum(f, 0.0) - jnp.log(1.0 + jnp.exp(-jnp.abs(f))), 0.0)
        cs = _dot01(tri_ref[...], logf) + carry[0:1, :]
        c_ref[...] = cs
        carry[...] = jnp.broadcast_to(cs[tm - 1:tm, :], (8, LANES))
        diags = jnp.zeros((tm, LANES), F32)
        for hd in range(H):
            sl = slice(LANES * hd, LANES * (hd + 1))
            a = _aug(hd % 2)
            qt = _head_tile(qraw_ref, hd, lane)
            qh_ref[hd] = qt
            qn = qt * lax.rsqrt(jnp.sum(qt * qt, axis=1, keepdims=True) * (1.0 / HEAD_DIM) + RMS_EPS) * gq_ref[...]
            kt = _head_tile(kraw_ref, hd, lane)
            kh_ref[hd] = kt
            kn = kt * lax.rsqrt(jnp.sum(kt * kt, axis=1, keepdims=True) * (1.0 / HEAD_DIM) + RMS_EPS) * gk_ref[...]
            diags = diags + jnp.where(lane == hd, jnp.sum(qn * kn, axis=1, keepdims=True) * Q_SCALE, 0.0)
            qa_ref[:, sl] = (qn * Q_SCALE).astype(BF16)
            ka_ref[:, sl] = kn.astype(BF16)
            va = jnp.where((lane >= a) & (lane < a + 3), 1.0, _head_tile(v_s, hd, lane))
            va_ref[:, sl] = va.astype(BF16)
            vt_ref[hd] = va.T.astype(BF16)
        rel = cs - diags
        rel_ref[...] = rel
        qa_ref[...] += _dot(_pack3(rel, lane, 1.0), first_ref[...]).astype(BF16)
        ka_ref[...] += _dot(_pack3(-cs, lane, 1.0), second_ref[...]).astype(BF16)

    nb = S // tm
    heads = pl.BlockSpec((H, tm, LANES), lambda i: (0, i, 0))
    return pl.pallas_call(
        body, name="attn_front", grid=(nb,),
        in_specs=[_rows(tm, D), _whole((1, D)), _whole((D, 4 * D)), _whole((D, LANES)), _whole((1, LANES)),
                  _whole((1, LANES)), _whole((1, LANES)), _whole((tm, tm)), _whole((LANES, H * LANES)),
                  _whole((LANES, H * LANES))],
        out_specs=[_rows(tm, D), heads, heads, _rows(tm, D), _rows(tm, LANES), _rows(tm, LANES), _rows(tm, LANES),
                   _rows(tm, H * LANES), _rows(tm, H * LANES), _rows(tm, H * LANES),
                   pl.BlockSpec((H, None, LANES, tm), lambda i: (0, i, 0, 0))],
        out_shape=[jax.ShapeDtypeStruct((S, D), BF16), jax.ShapeDtypeStruct((H, S, LANES), F32),
                   jax.ShapeDtypeStruct((H, S, LANES), F32), jax.ShapeDtypeStruct((S, D), F32),
                   jax.ShapeDtypeStruct((S, LANES), F32), jax.ShapeDtypeStruct((S, LANES), F32),
                   jax.ShapeDtypeStruct((S, LANES), F32),
                   jax.ShapeDtypeStruct((S, H * LANES), BF16), jax.ShapeDtypeStruct((S, H * LANES), BF16),
                   jax.ShapeDtypeStruct((S, H * LANES), BF16), jax.ShapeDtypeStruct((H, nb, LANES, tm), BF16)],
        scratch_shapes=[pltpu.VMEM((8, LANES), F32), pltpu.VMEM((tm, D), F32), pltpu.VMEM((tm, D), F32),
                        pltpu.VMEM((tm, D), F32)],
        compiler_params=_params(1),
    )(x1, g2, w, wf, bf, gq, gk, tri, *_scatter_matrices(H))


def _skip_tables(c, gq, gk, T, G):
    nb = c.shape[0] // T
    bound = 8.0 * jnp.max(jnp.abs(gq)) * jnp.max(jnp.abs(gk))
    first, last = c[0::T, :], c[T - 1::T, :]
    idx = jnp.arange(nb)
    need = (last[None, :, :] <= first[:, None, :] + (SKIP_LOG + 2.0 * bound)) & (idx[None, :, None] < idx[:, None, None])
    need = need | (idx[None, :, None] == idx[:, None, None])
    kstart = jnp.argmax(need, axis=1)
    qend = nb - 1 - jnp.argmax(need[::-1], axis=0)
    kstart = jnp.min(kstart.reshape(2 * nb // G, G // 2, -1), axis=1)
    kstart = kstart - (kstart & 1)
    qend = jnp.max(qend.reshape(2 * nb // G, G // 2, -1), axis=1)
    return kstart.T.astype(jnp.int32), qend.T.astype(jnp.int32), bound


def _attn_fwd(kstart, qa, ka, vt, online_max):
    S = qa.shape[0]
    H = qa.shape[1] // LANES
    nb, T = vt.shape[1], vt.shape[3]
    G = 2 * nb // kstart.shape[1]
    W = G * T

    def finish(acc, shift, o_ref, lse_ref):
        a = _aug(pl.program_id(0) % 2)
        feat = lax.broadcasted_iota(jnp.int32, (LANES, 1), 0)
        l = jnp.sum(jnp.where(feat == a, acc, 0.0), axis=0, keepdims=True)
        o_ref[...] = (acc / l).T
        lse_ref[...] = shift + jnp.log(l)

    def causal(st):
        return jnp.where(lax.broadcasted_iota(jnp.int32, st.shape, 0) <= lax.broadcasted_iota(jnp.int32, st.shape, 1),
                         st, NEG)

    def fast_body(ks_ref, q_ref, k_ref, vt_ref, o_ref, lse_ref, acc_ref, sa_ref, sb_ref):
        h, g = pl.program_id(0), pl.program_id(1)
        q = q_ref[...]
        acc_ref[...] = jnp.zeros((LANES, W), F32)

        def scores(ki, lo):
            return _dot_nt(k_ref[pl.ds(pl.multiple_of(ki * T, T), 2 * T), :], q[lo * T:, :])

        def weighted(ki, p):
            return _dot(vt_ref[ki], p[:T]) + _dot(vt_ref[ki + 1], p[T:])

        first = ks_ref[h, 2 * g + 1]
        early = jnp.minimum(ks_ref[h, 2 * g], first)

        def narrow(i, carry):
            ki = early + 2 * i
            st = _dot_nt(k_ref[pl.ds(pl.multiple_of(ki * T, T), 2 * T), :], q[:W // 2, :])
            acc_ref[:, :W // 2] += weighted(ki, jnp.exp(st).astype(BF16))
            return carry

        lax.fori_loop(0, (first - early) // 2, narrow, 0)
        steps = (g * G - first) // 2
        sa_ref[...] = scores(first, 0)

        def advance(ki, cur_ref, next_ref):
            p = jnp.exp(cur_ref[...]).astype(BF16)
            next_ref[...] = scores(ki + 2, 0)
            acc_ref[...] += weighted(ki, p)

        def loop(i, carry):
            advance(first + 4 * i, sa_ref, sb_ref)
            advance(first + 4 * i + 2, sb_ref, sa_ref)
            return carry

        lax.fori_loop(0, steps // 2, loop, 0)

        @pl.when(steps % 2 == 1)
        def _():
            advance(g * G - 2, sa_ref, sb_ref)
            acc_ref[...] += weighted(g * G, jnp.exp(causal(sb_ref[...])).astype(BF16))

        @pl.when(steps % 2 == 0)
        def _():
            acc_ref[...] += weighted(g * G, jnp.exp(causal(sa_ref[...])).astype(BF16))

        for j in range(2, G, 2):
            p = jnp.exp(causal(scores(g * G + j, j))).astype(BF16)
            acc_ref[:, j * T:] += weighted(g * G + j, p)
        finish(acc_ref[...], 0.0, o_ref, lse_ref)

    def online_body(ks_ref, q_ref, k_ref, vt_ref, o_ref, lse_ref, acc_ref, m_ref):
        h, g = pl.program_id(0), pl.program_id(1)
        q = q_ref[...]
        m_ref[...] = jnp.full((8, W), NEG, F32)
        acc_ref[...] = jnp.zeros((LANES, W), F32)

        def update(st, vtb, lo):
            m_old = m_ref[0:1, lo:]
            m_new = jnp.maximum(m_old, jnp.max(st, axis=0, keepdims=True))
            p = jnp.exp(st - m_new).astype(BF16)
            acc_ref[:, lo:] = acc_ref[:, lo:] * jnp.exp(m_old - m_new) + _dot(vtb, p)
            m_ref[:, lo:] = jnp.broadcast_to(m_new, (8, W - lo))

        def loop(ki, carry):
            kb = k_ref[pl.ds(pl.multiple_of(ki * T, T), T), :]
            update(_dot_nt(kb, q), vt_ref[ki], 0)
            return carry

        lax.fori_loop(jnp.minimum(ks_ref[h, 2 * g], ks_ref[h, 2 * g + 1]), g * G, loop, 0)
        for j in range(G):
            ki = g * G + j
            kb = k_ref[pl.ds(pl.multiple_of(ki * T, T), T), :]
            update(causal(_dot_nt(kb, q[j * T:, :])), vt_ref[ki], j * T)
        finish(acc_ref[...], m_ref[0:1, :], o_ref, lse_ref)

    return pl.pallas_call(
        online_body if online_max else fast_body, name="attn_fwd_online" if online_max else "attn_fwd",
        grid_spec=pltpu.PrefetchScalarGridSpec(
            num_scalar_prefetch=1, grid=(H, nb // G),
            in_specs=[pl.BlockSpec((W, LANES), lambda h, i, ks: (i, h)),
                      pl.BlockSpec((S, LANES), lambda h, i, ks: (0, h)),
                      pl.BlockSpec((None, nb, LANES, T), lambda h, i, ks: (h, 0, 0, 0))],
            out_specs=[pl.BlockSpec((W, LANES), lambda h, i, ks: (i, h)),
                       pl.BlockSpec((None, 1, W), lambda h, i, ks: (h, 0, i))],
            scratch_shapes=[pltpu.VMEM((LANES, W), F32)] + (
                [pltpu.VMEM((8, W), F32)] if online_max else [pltpu.VMEM((2 * T, W), F32)] * 2)),
        out_shape=[jax.ShapeDtypeStruct((S, H * LANES), F32), jax.ShapeDtypeStruct((H, 1, S), F32)],
        compiler_params=_params(2),
    )(kstart, qa, ka, vt)


def _attn_out(o_aug, lse, rel, z, x1, target, w_out, qa):
    S, D = x1.shape
    H = D // HEAD_DIM
    tm = min(ROW_TILE, S)

    def body(o_ref, z_ref, x1_ref, t_ref, w_ref, q_ref, first_ref, rel_ref, lse_ref,
             dx2_ref, dx2b_ref, o2b_ref, dz_ref, doa_ref, qa2_ref, loss_ref, oc_s, do_s):
        @pl.when(pl.program_id(0) == 0)
        def _():
            loss_ref[...] = jnp.zeros((1, LANES), F32)

        lane = _lane()
        for j in range(H // 2):
            oc_s[:, LANES * j:LANES * (j + 1)] = _pair_tile(o_ref[:, 2 * LANES * j:2 * LANES * j + LANES],
                                                            o_ref[:, 2 * LANES * j + LANES:2 * LANES * (j + 1)], lane)
        oc = oc_s[...]
        zv = z_ref[...]
        sg = _sigmoid(zv)
        sz = zv * sg
        o2 = (oc * sz).astype(BF16)
        o2b_ref[...] = o2
        e = x1_ref[...] + _dot(o2, w_ref[...]) - t_ref[...]
        sq = jnp.sum(jnp.sum(e * e, axis=1, keepdims=True), axis=0, keepdims=True)
        loss_ref[...] += jnp.broadcast_to(sq * (0.5 / D), (1, LANES))
        dx2 = e * (1.0 / D)
        dx2_ref[...] = dx2
        dx2b = dx2.astype(BF16)
        dx2b_ref[...] = dx2b
        do2 = _dot_nt(dx2b, w_ref[...])
        dz_ref[...] = do2 * oc * (sg * (1.0 + zv * (1.0 - sg)))
        do_s[...] = do2 * sz
        deltas = jnp.zeros((tm, LANES), F32)
        for hd in range(H):
            dt = _head_tile(do_s, hd, lane)
            delta = jnp.sum(dt * _head_tile(oc_s, hd, lane), axis=1, keepdims=True)
            deltas = deltas + jnp.where(lane == hd, delta, 0.0)
            doa_ref[:, LANES * hd:LANES * (hd + 1)] = dt.astype(BF16)
        doa_ref[...] += _dot(_pack3(-deltas, lane, 0.0), first_ref[...]).astype(BF16)
        lse = jnp.concatenate([lse_ref[...], jnp.zeros((LANES - H, tm), F32)], axis=0).T
        rq = rel_ref[...] - lse
        tile_lane = lax.broadcasted_iota(jnp.int32, (1, H * LANES), 1)
        extra = tile_lane % LANES - HEAD_DIM * (1 - (tile_lane // LANES) % 2)
        kept = jnp.where((extra >= 0) & (extra < 3), jnp.zeros((), BF16), q_ref[...])
        qa2_ref[...] = kept + _dot(_pack3(rq, lane, 0.0), first_ref[...]).astype(BF16)

    return pl.pallas_call(
        body, name="attn_out", grid=(S // tm,),
        in_specs=[_rows(tm, H * LANES), _rows(tm, D), _rows(tm, D), _rows(tm, D), _whole((D, D)),
                  _rows(tm, H * LANES), _whole((LANES, H * LANES)), _rows(tm, LANES),
                  pl.BlockSpec((H, tm), lambda i: (0, i))],
        out_specs=[_rows(tm, D), _rows(tm, D), _rows(tm, D), _rows(tm, D), _rows(tm, H * LANES),
                   _rows(tm, H * LANES), _whole((1, LANES))],
        out_shape=[jax.ShapeDtypeStruct((S, D), F32), jax.ShapeDtypeStruct((S, D), BF16),
                   jax.ShapeDtypeStruct((S, D), BF16), jax.ShapeDtypeStruct((S, D), F32),
                   jax.ShapeDtypeStruct((S, H * LANES), BF16), jax.ShapeDtypeStruct((S, H * LANES), BF16),
                   jax.ShapeDtypeStruct((1, LANES), F32)],
        scratch_shapes=[pltpu.VMEM((tm, D), F32), pltpu.VMEM((tm, D), F32)],
        compiler_params=_params(1),
    )(o_aug, z, x1, target, w_out, qa, _scatter_matrices(H)[0], rel, lse)


def _attn_bwd(qend, qa2, doa, ka, va, T):
    S = qa2.shape[0]
    H = qa2.shape[1] // LANES
    nb = S // T
    G = 2 * nb // qend.shape[1]
    W = G * T

    def body(qe_ref, q_ref, do_ref, k_ref, v_ref, dq_ref, dk_ref, dv_ref, dkt_acc, dvt_acc):
        h, g = pl.program_id(0), pl.program_id(1)

        @pl.when(g == 0)
        def _():
            dq_ref[...] = jnp.zeros((S, LANES), F32)

        kb = k_ref[...]
        vb = v_ref[...]
        dkt_acc[...] = jnp.zeros((LANES, W), F32)
        dvt_acc[...] = jnp.zeros((LANES, W), F32)

        def step(qi, c0, c1, masked):
            rows = pl.ds(pl.multiple_of(qi * T, T), 2 * T)
            qb = q_ref[rows, :]
            dob = do_ref[rows, :]
            s = _dot_nt(qb, kb[c0:c1])
            if masked:
                query = lax.broadcasted_iota(jnp.int32, s.shape, 0) + (c1 - 2 * T)
                s = jnp.where(lax.broadcasted_iota(jnp.int32, s.shape, 1) <= query, s, NEG)
            p = jnp.exp(s)
            ds = (p * _dot_nt(dob, vb[c0:c1])).astype(BF16)
            dvt_acc[:, c0:c1] += _dot(dob.astype(F32).T.astype(BF16), p.astype(BF16))
            dkt_acc[:, c0:c1] += _dot(qb.astype(F32).T.astype(BF16), ds)
            dq_ref[rows, :] += _dot(ds, kb[c0:c1])

        for m in range(G // 2):
            step(g * G + 2 * m, 0, (m + 1) * 2 * T, True)
        first = g * G + G
        n_all = jnp.maximum((qe_ref[h, 2 * g] - first + 2) // 2, 0)
        second = first + 2 * n_all

        def all_keys(i, carry):
            step(first + 2 * i, 0, W, False)
            return carry

        def late_keys(i, carry):
            step(second + 2 * i, W // 2, W, False)
            return carry

        lax.fori_loop(0, n_all, all_keys, 0)
        lax.fori_loop(0, (qe_ref[h, 2 * g + 1] - second + 2) // 2, late_keys, 0)
        dk_ref[...] = dkt_acc[...].T
        dv_ref[...] = dvt_acc[...].T

    heads = pl.BlockSpec((None, W, LANES), lambda h, i, qe: (h, i, 0))
    return pl.pallas_call(
        body, name="attn_bwd",
        grid_spec=pltpu.PrefetchScalarGridSpec(
            num_scalar_prefetch=1, grid=(H, nb // G),
            in_specs=[pl.BlockSpec((S, LANES), lambda h, i, qe: (0, h)), pl.BlockSpec((S, LANES), lambda h, i, qe: (0, h)),
                      pl.BlockSpec((W, LANES), lambda h, i, qe: (i, h)), pl.BlockSpec((W, LANES), lambda h, i, qe: (i, h))],
            out_specs=[pl.BlockSpec((None, S, LANES), lambda h, i, qe: (h, 0, 0)), heads, heads],
            scratch_shapes=[pltpu.VMEM((LANES, W), F32), pltpu.VMEM((LANES, W), F32)]),
        out_shape=[jax.ShapeDtypeStruct((H, S, LANES), F32)] * 3,
        compiler_params=_params(2),
    )(qend, qa2, doa, ka, va)


def _attn_proj_bwd(dqt, dka, dva, qraw, kraw, dz, f, gq, gk, w, wf, x1, g2, dx2):
    S, D = x1.shape
    H = D // HEAD_DIM
    tm = min(ROW_TILE, S)
    last = S // tm - 1
    tri = (lax.broadcasted_iota(jnp.int32, (tm, tm), 1) >= lax.broadcasted_iota(jnp.int32, (tm, tm), 0)).astype(BF16)

    def body(dq_ref, dk_ref, dv_ref, q_ref, k_ref, dz_ref, f_ref, gq_ref, gk_ref, w_ref, wf_ref, x1_ref, g2_ref,
             dx2_ref, tri_ref, dproj_ref, dx1_ref, dg2_ref, small_ref, carry, pairs):
        @pl.when(pl.program_id(0) == 0)
        def _():
            dg2_ref[...] = jnp.zeros((1, D), F32)
            small_ref[...] = jnp.zeros((8, LANES), F32)
            carry[...] = jnp.zeros((8, LANES), F32)

        lane = _lane()

        def head_pair(j, acc):
            dcs, dgq, dgk = acc
            dq2, dk2, dv2 = [], [], []
            for parity in (0, 1):
                hd = 2 * j + parity
                own, a = _own(lane, parity), _aug(parity)
                dqf = dq_ref[hd]
                dqn = jnp.where(own, dqf * Q_SCALE, 0.0)
                d, dg = _head_rms_bwd(dqn, q_ref[hd], gq_ref[...])
                dq2.append(d)
                dgq = dgq + dg
                dkt = dk_ref[hd]
                dcs = dcs + jnp.where(lane == hd, _col(dqf, lane, a) - _col(dkt, lane, a + 3), 0.0)
                d, dg = _head_rms_bwd(jnp.where(own, dkt, 0.0), k_ref[hd], gk_ref[...])
                dk2.append(d)
                dgk = dgk + dg
                dv2.append(dv_ref[hd])
            for part, pair in enumerate((dq2, dk2, dv2)):
                pairs[part, j] = _pair_tile(*pair, lane).astype(BF16)
            return dcs, dgq, dgk

        zero = jnp.zeros((1, LANES), F32)
        dcs, dgq, dgk = lax.fori_loop(0, H // 2, head_pair, (jnp.zeros((tm, LANES), F32), zero, zero))
        for part in range(3):
            for j in range(H // 2):
                dproj_ref[:, part * D + LANES * j:part * D + LANES * (j + 1)] = pairs[part, j]
        dproj_ref[:, 3 * D:4 * D] = dz_ref[...].astype(BF16)
        dlogf = _dot01(tri_ref[...], dcs) + carry[0:1, :]
        carry[...] = jnp.broadcast_to(dlogf[0:1, :], (8, LANES))
        df = dlogf * (1.0 / (1.0 + jnp.exp(f_ref[...])))
        dproj_ref[:, 4 * D:4 * D + LANES] = df.astype(BF16)
        small_ref[0:1, :] += jnp.sum(df, axis=0, keepdims=True)
        small_ref[1:2, :] += dgq
        small_ref[2:3, :] += dgk
        dh = _dot_nt(dproj_ref[:, 0:4 * D], w_ref[...]) + _dot_nt(dproj_ref[:, 4 * D:4 * D + LANES], wf_ref[...])
        dxn, dg = _rms_bwd(dh, x1_ref[...], g2_ref[...])
        dx1_ref[...] = dx2_ref[...] + dxn
        dg2_ref[...] += dg

    W = 4 * D + LANES
    heads = pl.BlockSpec((H, tm, LANES), lambda i: (0, last - i, 0))
    return pl.pallas_call(
        body, name="attn_proj_bwd", grid=(S // tm,),
        in_specs=[heads, heads, heads, heads, heads,
                  _rows(tm, D, last), _rows(tm, LANES, last), _whole((1, LANES)), _whole((1, LANES)),
                  _whole((D, 4 * D)), _whole((D, LANES)), _rows(tm, D, last), _whole((1, D)), _rows(tm, D, last),
                  _whole((tm, tm))],
        out_specs=[_rows(tm, W, last), _rows(tm, D, last), _whole((1, D)), _whole((8, LANES))],
        out_shape=[jax.ShapeDtypeStruct((S, W), BF16), jax.ShapeDtypeStruct((S, D), F32),
                   jax.ShapeDtypeStruct((1, D), F32), jax.ShapeDtypeStruct((8, LANES), F32)],
        scratch_shapes=[pltpu.VMEM((8, LANES), F32), pltpu.VMEM((3, H // 2, tm, LANES), BF16)],
        compiler_params=_params(1),
    )(dqt, dka, dva, qraw, kraw, dz, f, gq, gk, w, wf, x1, g2, dx2, tri)


def _matmul_tn(a, b, col0, n, tn, name, stacked=False):
    S, M = a.shape
    ts = min(TN_ROWS, S)
    off = col0 // tn

    def body(a_ref, b_ref, o_ref):
        @pl.when(pl.program_id(1) == 0)
        def _():
            o_ref[...] = jnp.zeros((M, tn), F32)

        o_ref[...] += _dot_tn(a_ref[...], b_ref[...])

    if stacked:
        out_spec, out_shape = pl.BlockSpec((None, M, tn), lambda j, s: (j, 0, 0)), (n // tn, M, tn)
    else:
        out_spec, out_shape = pl.BlockSpec((M, tn), lambda j, s: (0, j)), (M, n)
    return pl.pallas_call(
        body, name=name, grid=(n // tn, S // ts),
        in_specs=[pl.BlockSpec((ts, M), lambda j, s: (s, 0)), pl.BlockSpec((ts, tn), lambda j, s: (s, off + j))],
        out_specs=out_spec, out_shape=jax.ShapeDtypeStruct(out_shape, F32),
        compiler_params=_params(2),
    )(a, b)


def _adamw(w, g, m, v, name):
    r, c = w.shape
    tr = ROW_TILE if r % ROW_TILE == 0 else r

    def body(w_ref, g_ref, m_ref, v_ref, d_ref, m2_ref, v2_ref):
        gv = g_ref[...]
        m2 = ADAM_B1 * m_ref[...] + (1.0 - ADAM_B1) * gv
        v2 = ADAM_B2 * v_ref[...] + (1.0 - ADAM_B2) * (gv * gv)
        m2_ref[...] = m2
        v2_ref[...] = v2
        m_hat = m2 / (1.0 - ADAM_B1 ** ADAM_STEP)
        v_hat = v2 / (1.0 - ADAM_B2 ** ADAM_STEP)
        d_ref[...] = -ADAM_LR * (m_hat / (jnp.sqrt(v_hat) + ADAM_EPS) + ADAM_WD * w_ref[...])

    spec = _rows(tr, c)
    return pl.pallas_call(
        body, name=name, grid=(r // tr,), in_specs=[spec] * 4, out_specs=[spec] * 3,
        out_shape=[jax.ShapeDtypeStruct((r, c), F32)] * 3, compiler_params=_params(1),
    )(w, g, m, v)


def _local_step(x, target, g1, w_in, conv_w, w_out, g2, wa_in, b_f, gq, gk, wa_out):
    S, D = x.shape
    H = D // HEAD_DIM
    w_qkvz = wa_in[:, :4 * D]
    wf = jnp.pad(wa_in[:, 4 * D:], ((0, 0), (0, LANES - H)))
    bf = jnp.pad(b_f, ((0, 0), (0, LANES - H)))
    gq128 = jnp.concatenate([gq, gq], axis=1)
    gk128 = jnp.concatenate([gk, gk], axis=1)

    proj, h1, yc, y, x1 = _conv_fwd(x, g1, w_in, conv_w, w_out)
    h2, qraw, kraw, z, f, c, rel, qa, ka, va, vt = _attn_front(x1, g2, w_qkvz, wf, bf, gq128, gk128)
    T = vt.shape[3]
    kstart, qend, bound = _skip_tables(c[:, :H], gq, gk, T, min(ATT_GROUP, S // T))
    o_aug, lse = lax.cond(2.0 * bound <= PLAIN_EXP_MAX, functools.partial(_attn_fwd, online_max=False),
                          functools.partial(_attn_fwd, online_max=True), kstart, qa, ka, vt)
    dx2, dx2b, o2b, dz, doa, qa2, loss = _attn_out(o_aug, lse.reshape(H, S), rel, z, x1, target, wa_out, qa)
    dqt, dka, dva = _attn_bwd(qend, qa2, doa, ka, va, T)
    dproj2, dx1, dg2, small = _attn_proj_bwd(dqt, dka, dva, qraw, kraw, dz, f, gq128, gk128, w_qkvz, wf, x1, g2, dx2)
    dproj1, dx, dx1b, dg1, dcw = _conv_bwd(dx1, x, g1, w_in, w_out, conv_w, proj, yc)

    tn = min(1024, D)
    dwa_out = _matmul_tn(o2b, dx2b, 0, D, tn, "dw_attn_out")
    dwa_in = jnp.concatenate([_matmul_tn(h2, dproj2, 0, 4 * D, tn, "dw_attn_in"),
                              _matmul_tn(h2, dproj2, 4 * D, LANES, LANES, "dw_attn_f")[:, :H]], axis=1)
    dw_out = _matmul_tn(y, dx1b, 0, D, tn, "dw_conv_out")
    dw_in = _matmul_tn(h1, dproj1, 0, 4 * D, D, "dw_conv_in", stacked=True)
    grads = dict(conv_norm_g=dg1, conv_w_in=dw_in, conv_w=dcw, conv_w_out=dw_out, attn_norm_g=dg2,
                 attn_w_in=dwa_in, attn_b_f=small[0:1, :H],
                 attn_q_norm_g=small[1:2, :HEAD_DIM] + small[1:2, HEAD_DIM:],
                 attn_k_norm_g=small[2:3, :HEAD_DIM] + small[2:3, HEAD_DIM:], attn_w_out=dwa_out)
    return loss[0, 0], dx, grads


def _coords():
    return lax.axis_index("x"), lax.axis_index("y"), lax.axis_index("c")


def _at(ref, idx):
    return ref.at[idx] if idx else ref


def _other_chips(x, y):
    return [(1 - x, y), (x, 1 - y), (1 - x, 1 - y)]


def _all_gather(halved, whole):
    nh, nw = len(halved), len(whole)

    def body(*refs):
        src_h, src_w = refs[:nh], refs[nh:nh + nw]
        out_h, out_w = refs[nh + nw:2 * nh + nw], refs[2 * nh + nw:2 * (nh + nw)]
        send_h, recv_h, send_w, recv_w = refs[2 * (nh + nw):]
        x, y, c = _coords()
        mine = 2 * x + y
        sibling = (x, y, 1 - c)
        chips = _other_chips(x, y)

        def copy_h(a, k, chip, half, to, src=None):
            dst = out_h[a].at[chip, half]
            return pltpu.make_async_remote_copy(src_ref=dst if src is None else src, dst_ref=dst,
                                                send_sem=send_h.at[a, k], recv_sem=recv_h.at[a, k],
                                                device_id=to, device_id_type=MESH)

        def copy_w(a, k, chip, to):
            return pltpu.make_async_remote_copy(src_ref=src_w[a], dst_ref=out_w[a].at[chip],
                                                send_sem=send_w.at[a, k], recv_sem=recv_w.at[a, k],
                                                device_id=to, device_id_type=MESH)

        first = [copy_h(a, k, mine, c, (*chip, c), src=src_h[a].at[c]) for a in range(nh) for k, chip in enumerate(chips)]
        first += [copy_w(a, k, mine, (*chip, c)) for a in range(nw) for k, chip in enumerate(chips)]
        for cp in first:
            cp.start()
        passed = []
        for a in range(nh):
            for k, (px, py) in enumerate(chips):
                copy_h(a, k, 2 * px + py, c, (x, y, c)).wait_recv()
                cp = copy_h(a, 3 + k, 2 * px + py, c, sibling)
                cp.start()
                passed.append(cp)
        for a in range(nh):
            for k, (px, py) in enumerate(chips):
                copy_h(a, 3 + k, 2 * px + py, 1 - c, (x, y, c)).wait_recv()
        for a in range(nw):
            for k, (px, py) in enumerate(chips):
                copy_w(a, k, 2 * px + py, (x, y, c)).wait_recv()
        for cp in first + passed:
            cp.wait_send()

    out_shape = [jax.ShapeDtypeStruct((4,) + a.shape, a.dtype) for a in list(halved) + list(whole)]
    gathered = pl.pallas_call(
        body, name="gather_weights", in_specs=[ANY] * (nh + nw), out_specs=[ANY] * (nh + nw), out_shape=out_shape,
        scratch_shapes=[pltpu.SemaphoreType.DMA((nh, 6)), pltpu.SemaphoreType.DMA((nh, 6)),
                        pltpu.SemaphoreType.DMA((nw, 3)), pltpu.SemaphoreType.DMA((nw, 3))],
    )(*halved, *whole)
    x, y, _ = _coords()
    return [lax.dynamic_update_index_in_dim(g, a, 2 * x + y, axis=0) for g, a in zip(gathered, list(halved) + list(whole))]


def _exchange(name, srcs, lands, copies, local_copies):
    ns, nl, n, nloc = len(srcs), len(lands), len(copies), len(local_copies)

    def body(*refs):
        src, land = refs[:ns], refs[ns:ns + nl]
        send, recv, local_sem = refs[ns + nl:]
        me = _coords()
        started = []
        for k, (si, s_at, li, l_at, ci) in enumerate(local_copies):
            cp = pltpu.make_async_copy(_at(src[si], s_at(*me)), _at(land[li], l_at(*me)), local_sem.at[k])
            cp.start()
            started.append(cp)
        remote = []
        for k, (si, s_at, li, l_at, peer) in enumerate(copies):
            cp = pltpu.make_async_remote_copy(src_ref=_at(src[si], s_at(*me)), dst_ref=_at(land[li], l_at(*me)),
                                              send_sem=send.at[k], recv_sem=recv.at[k],
                                              device_id=peer(*me), device_id_type=MESH)
            cp.start()
            remote.append(cp)
        for cp in remote:
            cp.wait()
        for cp in started:
            cp.wait()

    return pl.pallas_call(
        body, name=name, in_specs=[ANY] * ns, out_specs=[ANY] * nl, out_shape=list(lands),
        scratch_shapes=[pltpu.SemaphoreType.DMA((n,)), pltpu.SemaphoreType.DMA((n,)),
                        pltpu.SemaphoreType.DMA((max(nloc, 1),))],
    )(*srcs)


def _add_pairs(a, b, name):
    _, r, cols = a.shape
    tr = ROW_TILE if r % ROW_TILE == 0 else r

    def body(a_ref, b_ref, o_ref, ob_ref):
        s = a_ref[...] + b_ref[...]
        o_ref[...] = s
        ob_ref[...] = s.astype(BF16)

    spec = pl.BlockSpec((None, tr, cols), lambda j, i: (j, i, 0))
    return pl.pallas_call(
        body, name=name, grid=(4, r // tr), in_specs=[spec, spec], out_specs=[spec, spec],
        out_shape=[jax.ShapeDtypeStruct(a.shape, F32), jax.ShapeDtypeStruct(a.shape, BF16)],
        compiler_params=_params(2),
    )(a, b)


def _sum_chips(own, landed, name):
    _, r, cols = landed.shape
    tr = ROW_TILE if r % ROW_TILE == 0 else r

    def body(own_ref, land_ref, o_ref):
        acc = own_ref[...]
        for j in range(3):
            acc = acc + land_ref[j].astype(F32)
        o_ref[...] = acc

    return pl.pallas_call(
        body, name=name, grid=(r // tr,),
        in_specs=[_rows(tr, cols), pl.BlockSpec((3, tr, cols), lambda i: (0, i, 0))], out_specs=_rows(tr, cols),
        out_shape=jax.ShapeDtypeStruct((r, cols), F32), compiler_params=_params(1),
    )(own, landed)


def _sum_devices(landed, name):
    def body(l_ref, o_ref):
        acc = l_ref[0]
        for j in range(1, 8):
            acc = acc + l_ref[j]
        o_ref[...] = acc

    return pl.pallas_call(body, name=name, out_shape=jax.ShapeDtypeStruct(landed.shape[1:], F32))(landed)


def _reduce_gradients(big, small):
    nb = len(big)
    x, y, c = _coords()
    mine = 2 * x + y
    flips = [(fx, fy, fc) for fx in (0, 1) for fy in (0, 1) for fc in (0, 1) if fx or fy or fc]

    def flip(fx, fy, fc):
        return lambda x, y, c: (x ^ fx, y ^ fy, c ^ fc)

    copies = [(a, (lambda j: lambda x, y, c: (j, 1 - c))(j), a, (lambda j: lambda x, y, c: (j,))(j), flip(0, 0, 1))
              for a in range(nb) for j in range(4)]
    copies += [(nb, lambda x, y, c: (), nb, lambda x, y, c: (4 * x + 2 * y + c,), flip(*f)) for f in flips]
    lands = [jax.ShapeDtypeStruct((4,) + g.shape[2:], F32) for g in big] + [jax.ShapeDtypeStruct((8,) + small.shape, F32)]
    local = [(nb, lambda x, y, c: (), nb, lambda x, y, c: (4 * x + 2 * y + c,), None)]
    landed = _exchange("swap_halves", list(big) + [small], lands, copies, local)
    small_sum = _sum_devices(landed[nb], "sum_small")

    chip_f32, chip_bf16 = [], []
    for a in range(nb):
        kept = lax.dynamic_index_in_dim(big[a], c, axis=1, keepdims=False)
        s, sb = _add_pairs(kept, landed[a], f"add_cores_{a}")
        chip_f32.append(s)
        chip_bf16.append(sb)

    chip_flips = [(1, 0), (0, 1), (1, 1)]
    copies = [(a, (lambda f: lambda x, y, c: (2 * (x ^ f[0]) + (y ^ f[1]),))(f), a, (lambda k: lambda x, y, c: (k,))(k),
               flip(f[0], f[1], 0)) for a in range(nb) for k, f in enumerate(chip_flips)]
    lands = [jax.ShapeDtypeStruct((3,) + g.shape[1:], BF16) for g in chip_bf16]
    landed = _exchange("send_chip_sums", chip_bf16, lands, copies, [])
    totals = [_sum_chips(lax.dynamic_index_in_dim(chip_f32[a], mine, axis=0, keepdims=False), landed[a],
                         f"sum_chips_{a}") for a in range(nb)]

    copies = [(a, lambda x, y, c: (), a, lambda x, y, c: (), flip(0, 0, 1)) for a in range(nb)]
    lands = [jax.ShapeDtypeStruct(t.shape, F32) for t in totals]
    landed = _exchange("swap_sums", totals, lands, copies, [])
    return [jnp.stack([jnp.where(c == 0, t, l), jnp.where(c == 0, l, t)]) for t, l in zip(totals, landed)], small_sum


def kernel(x, conv_norm_g, conv_w_in, conv_w, conv_w_out, attn_norm_g, attn_w_in, attn_b_f, attn_q_norm_g, attn_k_norm_g, attn_w_out, loss_target, m_conv_norm_g, m_conv_w_in, m_conv_w, m_conv_w_out, m_attn_norm_g, m_attn_w_in, m_attn_b_f, m_attn_q_norm_g, m_attn_k_norm_g, m_attn_w_out, v_conv_norm_g, v_conv_w_in, v_conv_w, v_conv_w_out, v_attn_norm_g, v_attn_w_in, v_attn_b_f, v_attn_q_norm_g, v_attn_k_norm_g, v_attn_w_out):
    xi, yi, _ = _coords()
    chip = 2 * xi + yi
    D = x.shape[2]
    H = D // HEAD_DIM
    names = ["conv_norm_g", "conv_w_in", "conv_w", "conv_w_out", "attn_norm_g", "attn_w_in", "attn_b_f",
             "attn_q_norm_g", "attn_k_norm_g", "attn_w_out"]
    weights = dict(zip(names, [conv_norm_g, conv_w_in, conv_w, conv_w_out, attn_norm_g, attn_w_in, attn_b_f,
                               attn_q_norm_g, attn_k_norm_g, attn_w_out]))
    m_in = dict(zip(names, [m_conv_norm_g, m_conv_w_in, m_conv_w, m_conv_w_out, m_attn_norm_g, m_attn_w_in,
                            m_attn_b_f, m_attn_q_norm_g, m_attn_k_norm_g, m_attn_w_out]))
    v_in = dict(zip(names, [v_conv_norm_g, v_conv_w_in, v_conv_w, v_conv_w_out, v_attn_norm_g, v_attn_w_in,
                            v_attn_b_f, v_attn_q_norm_g, v_attn_k_norm_g, v_attn_w_out]))
    weights = {k: w[0] for k, w in weights.items()}
    m_in = {k: w[0] for k, w in m_in.items()}
    v_in = {k: w[0] for k, w in v_in.items()}

    big_names = ["conv_w_in", "attn_w_in", "conv_w_out", "attn_w_out"]
    halved = [weights[k].astype(BF16).reshape(2, weights[k].shape[0] // 2, weights[k].shape[1]) for k in big_names]
    q = D // 4
    small_w = jnp.concatenate([weights["conv_w"], weights["attn_norm_g"][None, :], jnp.zeros((4, q), F32)], axis=0)
    g_in, ga_in, g_out, ga_out, g_small = _all_gather(halved, [small_w])
    w_in = g_in.reshape(4, D, D)
    wa_in = ga_in.reshape(4, D, D + H // 4).transpose(1, 0, 2).reshape(D, 4 * D + H)
    w_out = g_out.reshape(D, D)
    wa_out = ga_out.reshape(D, D)
    conv_w_full = g_small[:, 0:3, :].transpose(1, 0, 2).reshape(3, D)
    attn_g_full = g_small[:, 3, :].reshape(1, D)

    loss_part, grad_x, grads = _local_step(x[0], loss_target[0], weights["conv_norm_g"][None, :], w_in, conv_w_full,
                                           w_out, attn_g_full, wa_in, weights["attn_b_f"][None, :],
                                           weights["attn_q_norm_g"][None, :], weights["attn_k_norm_g"][None, :], wa_out)
    loss = lax.psum(loss_part, ("x", "y", "c"))

    big = [grads["conv_w_in"].reshape(4, 2, D // 2, D),
           grads["attn_w_in"].reshape(D, 4, D + H // 4).transpose(1, 0, 2).reshape(4, 2, D // 2, D + H // 4),
           grads["conv_w_out"].reshape(4, 2, D // 8, D), grads["attn_w_out"].reshape(4, 2, D // 8, D)]
    tail = jnp.concatenate([grads["attn_b_f"], grads["attn_q_norm_g"], grads["attn_k_norm_g"],
                            jnp.zeros((1, D - H - 2 * HEAD_DIM), F32)], axis=1)
    small = jnp.concatenate([grads["conv_norm_g"], grads["conv_w"], grads["attn_norm_g"], tail,
                             jnp.zeros((2, D), F32)], axis=0)
    reduced, small_sum = _reduce_gradients(big, small)
    final = {k: r.reshape(weights[k].shape) for k, r in zip(big_names, reduced)}
    final["conv_norm_g"] = small_sum[0]
    final["conv_w"] = lax.dynamic_slice_in_dim(small_sum[1:4], chip * q, q, axis=1)
    final["attn_norm_g"] = lax.dynamic_slice_in_dim(small_sum[4], chip * q, q, axis=0)
    final["attn_b_f"] = small_sum[5, :H]
    final["attn_q_norm_g"] = small_sum[5, H:H + HEAD_DIM]
    final["attn_k_norm_g"] = small_sum[5, H + HEAD_DIM:H + 2 * HEAD_DIM]

    delta, new_m, new_v = {}, {}, {}
    for k in names:
        shape = weights[k].shape
        as2d = (lambda a: a.reshape(1, -1)) if len(shape) == 1 else (lambda a: a)
        d, m2, v2 = _adamw(as2d(weights[k]), as2d(final[k]), as2d(m_in[k]), as2d(v_in[k]), "adamw_" + k)
        delta[k], new_m[k], new_v[k] = d.reshape(shape), m2.reshape(shape), v2.reshape(shape)
    lead = lambda a: a[None]
    return (loss, grad_x[None], *[lead(final[k]) for k in names], *[lead(delta[k]) for k in names],
            *[lead(new_m[k]) for k in names], *[lead(new_v[k]) for k in names])
```

```python
import functools

import jax
import jax.numpy as jnp
from jax import lax
from jax.experimental import pallas as pl
from jax.experimental.pallas import tpu as pltpu

F32 = jnp.float32
BF16 = jnp.bfloat16
HEAD_DIM = 64
LANES = 128
RMS_EPS = 1e-6
NEG = -1e30
Q_SCALE = 0.125
ROW_TILE = 256
ATT_GROUP = 4
SKIP_LOG = 106.0
PLAIN_EXP_MAX = 60.0
TN_ROWS = 1024
VMEM_LIMIT = 56 << 20
ADAM_LR, ADAM_B1, ADAM_B2, ADAM_EPS, ADAM_WD, ADAM_STEP = 0.001, 0.9, 0.999, 1e-08, 0.01, 10
MESH = pl.DeviceIdType.MESH
ANY = pl.BlockSpec(memory_space=pl.ANY)


def _lane():
    return lax.broadcasted_iota(jnp.int32, (1, LANES), 1)


def _split3(x):
    hi = x.astype(BF16).astype(F32)
    r = x - hi
    mid = r.astype(BF16).astype(F32)
    lo = (r - mid).astype(BF16).astype(F32)
    return hi, mid, lo


STAT_STRIDE = 16
ONE_LANE = 3 * STAT_STRIDE


def _pack3(x, lane, one):
    hi, mid, lo = _split3(x)
    packed = hi + pltpu.roll(mid, STAT_STRIDE, 1) + pltpu.roll(lo, 2 * STAT_STRIDE, 1)
    return jnp.where(lane == ONE_LANE, one, packed).astype(BF16)


def _scatter_matrices(H):
    rows = lax.broadcasted_iota(jnp.int32, (LANES, H * LANES), 0)
    cols = lax.broadcasted_iota(jnp.int32, (LANES, H * LANES), 1)
    head, within = cols // LANES, cols % LANES
    extra = within - HEAD_DIM * (1 - head % 2)
    term = (rows < ONE_LANE) & (rows % STAT_STRIDE == head)
    first = ((term & (extra == rows // STAT_STRIDE)) | ((rows == ONE_LANE) & (extra >= 3) & (extra < 6)))
    second = ((term & (extra - 3 == rows // STAT_STRIDE)) | ((rows == ONE_LANE) & (extra >= 0) & (extra < 3)))
    return first.astype(BF16), second.astype(BF16)


def _gather_matrix(H, lo):
    rows = lax.broadcasted_iota(jnp.int32, (H * LANES, LANES), 0)
    cols = lax.broadcasted_iota(jnp.int32, (H * LANES, LANES), 1)
    extra = rows % LANES - HEAD_DIM * (1 - (rows // LANES) % 2)
    return ((rows // LANES == cols) & (extra >= lo) & (extra < lo + 3)).astype(BF16)


def _put(base, lane, start, parts):
    for j, p in enumerate(parts):
        base = jnp.where(lane == start + j, p, base)
    return base


def _col(x, lane, idx):
    return jnp.sum(jnp.where(lane == idx, x, 0.0), axis=1, keepdims=True)


def _feat(parity):
    return HEAD_DIM * parity


def _aug(parity):
    return HEAD_DIM * (1 - parity)


def _own(lane, parity):
    return (lane >= _feat(parity)) & (lane < _feat(parity) + HEAD_DIM)


def _head_tile(ref, hd, lane):
    j = hd // 2
    return jnp.where(_own(lane, hd % 2), ref[:, LANES * j:LANES * (j + 1)], 0.0)


def _pair_tile(even, odd, lane):
    return jnp.where(lane < HEAD_DIM, even, odd)


def _sigmoid(x):
    return 1.0 / (1.0 + jnp.exp(-x))


def _dot(a, b):
    return jnp.dot(a, b, preferred_element_type=F32)


def _dot_nt(a, b):
    return lax.dot_general(a, b, (((1,), (1,)), ((), ())), preferred_element_type=F32)


def _dot_tn(a, b):
    return lax.dot_general(a, b, (((0,), (0,)), ((), ())), preferred_element_type=F32)


def _dot01(tri, x):
    hi, mid, lo = _split3(x)
    return _dot(tri, hi.astype(BF16)) + _dot(tri, mid.astype(BF16)) + _dot(tri, lo.astype(BF16))


def _rms_bwd(dh, x, g):
    inv = lax.rsqrt(jnp.mean(x * x, axis=-1, keepdims=True) + RMS_EPS)
    xh = x * inv
    dxn = dh * g
    dx = inv * (dxn - xh * jnp.mean(dxn * xh, axis=-1, keepdims=True))
    return dx, jnp.sum(dh * xh, axis=0, keepdims=True)


def _head_rms_bwd(dn, t, g):
    inv = lax.rsqrt(jnp.sum(t * t, axis=1, keepdims=True) * (1.0 / HEAD_DIM) + RMS_EPS)
    th = t * inv
    gd = dn * g
    d = inv * (gd - th * (jnp.sum(gd * th, axis=1, keepdims=True) * (1.0 / HEAD_DIM)))
    return d, jnp.sum(dn * th, axis=0, keepdims=True)


def _params(n_grid):
    return pltpu.CompilerParams(dimension_semantics=("arbitrary",) * n_grid, vmem_limit_bytes=VMEM_LIMIT)


def _rows(tm, cols, rev=None):
    if rev is None:
        return pl.BlockSpec((tm, cols), lambda i: (i, 0))
    return pl.BlockSpec((tm, cols), lambda i: (rev - i, 0))


def _whole(shape):
    return pl.BlockSpec(shape, lambda *_: (0,) * len(shape))


def _conv_fwd(x, g1, w_in, conv_w, w_out):
    S, D = x.shape
    tm = min(ROW_TILE, S)

    def body(x_ref, g_ref, win_ref, cw_ref, wout_ref, proj_ref, h_ref, yc_ref, y_ref, x1_ref, prev_u):
        i = pl.program_id(0)
        xv = x_ref[...]
        inv = lax.rsqrt(jnp.mean(xv * xv, axis=-1, keepdims=True) + RMS_EPS)
        h = (xv * inv * g_ref[...]).astype(BF16)
        h_ref[...] = h
        for j in range(4):
            proj_ref[:, j * D:(j + 1) * D] = _dot(h, win_ref[j])

        @pl.when(i == 0)
        def _():
            prev_u[...] = jnp.zeros((tm, D), F32)

        u = proj_ref[:, D:2 * D] * proj_ref[:, 2 * D:3 * D]
        pu = prev_u[...]
        row = lax.broadcasted_iota(jnp.int32, (tm, 1), 0)
        u1 = jnp.where(row < 1, pltpu.roll(pu, 1, 0), pltpu.roll(u, 1, 0))
        u2 = jnp.where(row < 2, pltpu.roll(pu, 2, 0), pltpu.roll(u, 2, 0))
        prev_u[...] = u
        w = cw_ref[...]
        yc = w[2:3] * u + w[1:2] * u1 + w[0:1] * u2
        yc_ref[...] = yc
        z = proj_ref[:, 3 * D:4 * D]
        y = (proj_ref[:, 0:D] * yc * (z * _sigmoid(z))).astype(BF16)
        y_ref[...] = y
        x1_ref[...] = xv + _dot(y, wout_ref[...])

    return pl.pallas_call(
        body, name="conv_fwd", grid=(S // tm,),
        in_specs=[_rows(tm, D), _whole((1, D)), _whole((4, D, D)), _whole((3, D)), _whole((D, D))],
        out_specs=[_rows(tm, 4 * D), _rows(tm, D), _rows(tm, D), _rows(tm, D), _rows(tm, D)],
        out_shape=[jax.ShapeDtypeStruct((S, 4 * D), F32), jax.ShapeDtypeStruct((S, D), BF16),
                   jax.ShapeDtypeStruct((S, D), F32), jax.ShapeDtypeStruct((S, D), BF16),
                   jax.ShapeDtypeStruct((S, D), F32)],
        scratch_shapes=[pltpu.VMEM((tm, D), F32)],
        compiler_params=_params(1),
    )(x, g1, w_in, conv_w, w_out)


def _conv_bwd(dx1, x, g1, w_in, w_out, conv_w, proj, yc):
    S, D = x.shape
    tm = min(ROW_TILE, S)
    last = S // tm - 1

    def body(dx1_ref, x_ref, g_ref, win_ref, wout_ref, cw_ref, proj_ref, yc_ref,
             dproj_ref, dx_ref, dx1b_ref, dg_ref, dcw_ref, next_d):
        @pl.when(pl.program_id(0) == 0)
        def _():
            dg_ref[...] = jnp.zeros((1, D), F32)
            dcw_ref[...] = jnp.zeros((3, D), F32)
            next_d[...] = jnp.zeros((tm, D), F32)

        dx1v = dx1_ref[...]
        dx1b = dx1v.astype(BF16)
        dx1b_ref[...] = dx1b
        dy = _dot_nt(dx1b, wout_ref[...])
        b = proj_ref[:, 0:D]
        c = proj_ref[:, D:2 * D]
        xin = proj_ref[:, 2 * D:3 * D]
        z = proj_ref[:, 3 * D:4 * D]
        sg = _sigmoid(z)
        sz = z * sg
        ycv = yc_ref[...]
        d0 = dy * b * sz
        dproj_ref[:, 0:D] = (dy * ycv * sz).astype(BF16)
        dproj_ref[:, 3 * D:4 * D] = (dy * b * ycv * (sg * (1.0 + z * (1.0 - sg)))).astype(BF16)
        nd = next_d[...]
        row = lax.broadcasted_iota(jnp.int32, (tm, 1), 0)
        d1 = jnp.where(row >= tm - 1, pltpu.roll(nd, tm - 1, 0), pltpu.roll(d0, tm - 1, 0))
        d2 = jnp.where(row >= tm - 2, pltpu.roll(nd, tm - 2, 0), pltpu.roll(d0, tm - 2, 0))
        next_d[...] = d0
        w = cw_ref[...]
        du = w[2:3] * d0 + w[1:2] * d1 + w[0:1] * d2
        u = c * xin
        dcw_ref[2:3, :] += jnp.sum(d0 * u, axis=0, keepdims=True)
        dcw_ref[1:2, :] += jnp.sum(d1 * u, axis=0, keepdims=True)
        dcw_ref[0:1, :] += jnp.sum(d2 * u, axis=0, keepdims=True)
        dproj_ref[:, D:2 * D] = (du * xin).astype(BF16)
        dproj_ref[:, 2 * D:3 * D] = (du * c).astype(BF16)
        dh = _dot_nt(dproj_ref[:, 0:D], win_ref[0])
        for j in range(1, 4):
            dh = dh + _dot_nt(dproj_ref[:, j * D:(j + 1) * D], win_ref[j])
        dxn, dg = _rms_bwd(dh, x_ref[...], g_ref[...])
        dx_ref[...] = dx1v + dxn
        dg_ref[...] += dg

    return pl.pallas_call(
        body, name="conv_bwd", grid=(S // tm,),
        in_specs=[_rows(tm, D, last), _rows(tm, D, last), _whole((1, D)), _whole((4, D, D)), _whole((D, D)),
                  _whole((3, D)), _rows(tm, 4 * D, last), _rows(tm, D, last)],
        out_specs=[_rows(tm, 4 * D, last), _rows(tm, D, last), _rows(tm, D, last), _whole((1, D)), _whole((3, D))],
        out_shape=[jax.ShapeDtypeStruct((S, 4 * D), BF16), jax.ShapeDtypeStruct((S, D), F32),
                   jax.ShapeDtypeStruct((S, D), BF16), jax.ShapeDtypeStruct((1, D), F32),
                   jax.ShapeDtypeStruct((3, D), F32)],
        scratch_shapes=[pltpu.VMEM((tm, D), F32)],
        compiler_params=_params(1),
    )(dx1, x, g1, w_in, w_out, conv_w, proj, yc)


def _attn_front(x1, g2, w, wf, bf, gq, gk):
    S, D = x1.shape
    H = D // HEAD_DIM
    tm = min(ROW_TILE, S)
    tri = (lax.broadcasted_iota(jnp.int32, (tm, tm), 1) <= lax.broadcasted_iota(jnp.int32, (tm, tm), 0)).astype(BF16)

    def body(x_ref, g_ref, w_ref, wf_ref, bf_ref, gq_ref, gk_ref, tri_ref, first_ref, second_ref,
             h_ref, qh_ref, kh_ref, z_ref, f_ref, c_ref, rel_ref, qa_ref, ka_ref, va_ref, vt_ref,
             carry, v_s, qraw_ref, kraw_ref):
        @pl.when(pl.program_id(0) == 0)
        def _():
            carry[...] = jnp.zeros((8, LANES), F32)

        xv = x_ref[...]
        inv = lax.rsqrt(jnp.mean(xv * xv, axis=-1, keepdims=True) + RMS_EPS)
        h = (xv * inv * g_ref[...]).astype(BF16)
        h_ref[...] = h
        qraw_ref[...] = _dot(h, w_ref[:, 0:D])
        kraw_ref[...] = _dot(h, w_ref[:, D:2 * D])
        v_s[...] = _dot(h, w_ref[:, 2 * D:3 * D])
        z_ref[...] = _dot(h, w_ref[:, 3 * D:4 * D])
        lane = _lane()
        f = _dot(h, wf_ref[...]) + bf_ref[...]
        f_ref[...] = f
        logf = jnp.where(lane < H, jnp.minimum(f, 0.0) - jnp.log(1.0 + jnp.exp(-jnp.abs(f))), 0.0)
        cs = _dot01(tri_ref[...], logf) + carry[0:1, :]
        c_ref[...] = cs
        carry[...] = jnp.broadcast_to(cs[tm - 1:tm, :], (8, LANES))
        diags = jnp.zeros((tm, LANES), F32)
        for hd in range(H):
            sl = slice(LANES * hd, LANES * (hd + 1))
            a = _aug(hd % 2)
            if hd % 2 == 0:
                qh_ref[hd // 2] = qraw_ref[:, LANES * (hd // 2):LANES * (hd // 2 + 1)]
                kh_ref[hd // 2] = kraw_ref[:, LANES * (hd // 2):LANES * (hd // 2 + 1)]
            qt = _head_tile(qraw_ref, hd, lane)
            qn = qt * lax.rsqrt(jnp.sum(qt * qt, axis=1, keepdims=True) * (1.0 / HEAD_DIM) + RMS_EPS) * gq_ref[...]
            kt = _head_tile(kraw_ref, hd, lane)
            kn = kt * lax.rsqrt(jnp.sum(kt * kt, axis=1, keepdims=True) * (1.0 / HEAD_DIM) + RMS_EPS) * gk_ref[...]
            diags = diags + jnp.where(lane == hd, jnp.sum(qn * kn, axis=1, keepdims=True) * Q_SCALE, 0.0)
            qa_ref[:, sl] = (qn * Q_SCALE).astype(BF16)
            ka_ref[:, sl] = kn.astype(BF16)
            va = jnp.where((lane >= a) & (lane < a + 3), 1.0, _head_tile(v_s, hd, lane))
            va_ref[:, sl] = va.astype(BF16)
            vt_ref[hd] = va.T.astype(BF16)
        rel = cs - diags
        rel_ref[...] = rel
        qa_ref[...] += _dot(_pack3(rel, lane, 1.0), first_ref[...]).astype(BF16)
        ka_ref[...] += _dot(_pack3(-cs, lane, 1.0), second_ref[...]).astype(BF16)

    nb = S // tm
    heads = pl.BlockSpec((H // 2, tm, LANES), lambda i: (0, i, 0))
    return pl.pallas_call(
        body, name="attn_front", grid=(nb,),
        in_specs=[_rows(tm, D), _whole((1, D)), _whole((D, 4 * D)), _whole((D, LANES)), _whole((1, LANES)),
                  _whole((1, LANES)), _whole((1, LANES)), _whole((tm, tm)), _whole((LANES, H * LANES)),
                  _whole((LANES, H * LANES))],
        out_specs=[_rows(tm, D), heads, heads, _rows(tm, D), _rows(tm, LANES), _rows(tm, LANES), _rows(tm, LANES),
                   _rows(tm, H * LANES), _rows(tm, H * LANES), _rows(tm, H * LANES),
                   pl.BlockSpec((H, None, LANES, tm), lambda i: (0, i, 0, 0))],
        out_shape=[jax.ShapeDtypeStruct((S, D), BF16), jax.ShapeDtypeStruct((H // 2, S, LANES), F32),
                   jax.ShapeDtypeStruct((H // 2, S, LANES), F32), jax.ShapeDtypeStruct((S, D), F32),
                   jax.ShapeDtypeStruct((S, LANES), F32), jax.ShapeDtypeStruct((S, LANES), F32),
                   jax.ShapeDtypeStruct((S, LANES), F32),
                   jax.ShapeDtypeStruct((S, H * LANES), BF16), jax.ShapeDtypeStruct((S, H * LANES), BF16),
                   jax.ShapeDtypeStruct((S, H * LANES), BF16), jax.ShapeDtypeStruct((H, nb, LANES, tm), BF16)],
        scratch_shapes=[pltpu.VMEM((8, LANES), F32), pltpu.VMEM((tm, D), F32), pltpu.VMEM((tm, D), F32),
                        pltpu.VMEM((tm, D), F32)],
        compiler_params=_params(1),
    )(x1, g2, w, wf, bf, gq, gk, tri, *_scatter_matrices(H))


def _skip_tables(c, gq, gk, T, G):
    nb = c.shape[0] // T
    bound = 8.0 * jnp.max(jnp.abs(gq)) * jnp.max(jnp.abs(gk))
    first, last = c[0::T, :], c[T - 1::T, :]
    idx = jnp.arange(nb)
    need = (last[None, :, :] <= first[:, None, :] + (SKIP_LOG + 2.0 * bound)) & (idx[None, :, None] < idx[:, None, None])
    need = need | (idx[None, :, None] == idx[:, None, None])
    kstart = jnp.argmax(need, axis=1)
    qend = nb - 1 - jnp.argmax(need[::-1], axis=0)
    kstart = jnp.min(kstart.reshape(2 * nb // G, G // 2, -1), axis=1)
    kstart = kstart - (kstart & 1)
    qend = jnp.max(qend.reshape(2 * nb // G, G // 2, -1), axis=1)
    return kstart.T.astype(jnp.int32), qend.T.astype(jnp.int32), bound


def _attn_fwd(kstart, qa, ka, vt, online_max):
    S = qa.shape[0]
    H = qa.shape[1] // LANES
    nb, T = vt.shape[1], vt.shape[3]
    G = 2 * nb // kstart.shape[1]
    W = G * T

    def finish(acc, shift, o_ref, lse_ref):
        a = _aug(pl.program_id(0) % 2)
        feat = lax.broadcasted_iota(jnp.int32, (LANES, 1), 0)
        l = jnp.sum(jnp.where(feat == a, acc, 0.0), axis=0, keepdims=True)
        o_ref[...] = (acc / l).T
        lse_ref[...] = shift + jnp.log(l)

    def causal(st):
        return jnp.where(lax.broadcasted_iota(jnp.int32, st.shape, 0) <= lax.broadcasted_iota(jnp.int32, st.shape, 1),
                         st, NEG)

    def fast_body(ks_ref, q_ref, k_ref, vt_ref, o_ref, lse_ref, acc_ref, sa_ref, sb_ref):
        h, g = pl.program_id(0), pl.program_id(1)
        q = q_ref[...]
        acc_ref[...] = jnp.zeros((LANES, W), F32)

        def scores(ki, lo):
            return _dot_nt(k_ref[pl.ds(pl.multiple_of(ki * T, T), 2 * T), :], q[lo * T:, :])

        def weighted(ki, p):
            return _dot(vt_ref[ki], p[:T]) + _dot(vt_ref[ki + 1], p[T:])

        first = ks_ref[h, 2 * g + 1]
        early = jnp.minimum(ks_ref[h, 2 * g], first)

        def narrow(i, carry):
            ki = early + 2 * i
            st = _dot_nt(k_ref[pl.ds(pl.multiple_of(ki * T, T), 2 * T), :], q[:W // 2, :])
            acc_ref[:, :W // 2] += weighted(ki, jnp.exp(st).astype(BF16))
            return carry

        lax.fori_loop(0, (first - early) // 2, narrow, 0)
        steps = (g * G - first) // 2
        sa_ref[...] = scores(first, 0)

        def advance(ki, cur_ref, next_ref):
            p = jnp.exp(cur_ref[...]).astype(BF16)
            next_ref[...] = scores(ki + 2, 0)
            acc_ref[...] += weighted(ki, p)

        def loop(i, carry):
            advance(first + 4 * i, sa_ref, sb_ref)
            advance(first + 4 * i + 2, sb_ref, sa_ref)
            return carry

        lax.fori_loop(0, steps // 2, loop, 0)

        @pl.when(steps % 2 == 1)
        def _():
            advance(g * G - 2, sa_ref, sb_ref)
            acc_ref[...] += weighted(g * G, jnp.exp(causal(sb_ref[...])).astype(BF16))

        @pl.when(steps % 2 == 0)
        def _():
            acc_ref[...] += weighted(g * G, jnp.exp(causal(sa_ref[...])).astype(BF16))

        for j in range(2, G, 2):
            p = jnp.exp(causal(scores(g * G + j, j))).astype(BF16)
            acc_ref[:, j * T:] += weighted(g * G + j, p)
        finish(acc_ref[...], 0.0, o_ref, lse_ref)

    def online_body(ks_ref, q_ref, k_ref, vt_ref, o_ref, lse_ref, acc_ref, m_ref):
        h, g = pl.program_id(0), pl.program_id(1)
        q = q_ref[...]
        m_ref[...] = jnp.full((8, W), NEG, F32)
        acc_ref[...] = jnp.zeros((LANES, W), F32)

        def update(st, vtb, lo):
            m_old = m_ref[0:1, lo:]
            m_new = jnp.maximum(m_old, jnp.max(st, axis=0, keepdims=True))
            p = jnp.exp(st - m_new).astype(BF16)
            acc_ref[:, lo:] = acc_ref[:, lo:] * jnp.exp(m_old - m_new) + _dot(vtb, p)
            m_ref[:, lo:] = jnp.broadcast_to(m_new, (8, W - lo))

        def loop(ki, carry):
            kb = k_ref[pl.ds(pl.multiple_of(ki * T, T), T), :]
            update(_dot_nt(kb, q), vt_ref[ki], 0)
            return carry

        lax.fori_loop(jnp.minimum(ks_ref[h, 2 * g], ks_ref[h, 2 * g + 1]), g * G, loop, 0)
        for j in range(G):
            ki = g * G + j
            kb = k_ref[pl.ds(pl.multiple_of(ki * T, T), T), :]
            update(causal(_dot_nt(kb, q[j * T:, :])), vt_ref[ki], j * T)
        finish(acc_ref[...], m_ref[0:1, :], o_ref, lse_ref)

    return pl.pallas_call(
        online_body if online_max else fast_body, name="attn_fwd_online" if online_max else "attn_fwd",
        grid_spec=pltpu.PrefetchScalarGridSpec(
            num_scalar_prefetch=1, grid=(H, nb // G),
            in_specs=[pl.BlockSpec((W, LANES), lambda h, i, ks: (i, h)),
                      pl.BlockSpec((S, LANES), lambda h, i, ks: (0, h)),
                      pl.BlockSpec((None, nb, LANES, T), lambda h, i, ks: (h, 0, 0, 0))],
            out_specs=[pl.BlockSpec((W, LANES), lambda h, i, ks: (i, h)),
                       pl.BlockSpec((None, 1, W), lambda h, i, ks: (h, 0, i))],
            scratch_shapes=[pltpu.VMEM((LANES, W), F32)] + (
                [pltpu.VMEM((8, W), F32)] if online_max else [pltpu.VMEM((2 * T, W), F32)] * 2)),
        out_shape=[jax.ShapeDtypeStruct((S, H * LANES), F32), jax.ShapeDtypeStruct((H, 1, S), F32)],
        compiler_params=_params(2),
    )(kstart, qa, ka, vt)


def _attn_out(o_aug, lse, rel, z, x1, target, w_out, qa):
    S, D = x1.shape
    H = D // HEAD_DIM
    tm = min(ROW_TILE, S)

    def body(o_ref, z_ref, x1_ref, t_ref, w_ref, q_ref, first_ref, rel_ref, lse_ref,
             dx2_ref, dx2b_ref, o2b_ref, dz_ref, doa_ref, qa2_ref, loss_ref, oc_s, do_s):
        @pl.when(pl.program_id(0) == 0)
        def _():
            loss_ref[...] = jnp.zeros((1, LANES), F32)

        lane = _lane()
        for j in range(H // 2):
            oc_s[:, LANES * j:LANES * (j + 1)] = _pair_tile(o_ref[:, 2 * LANES * j:2 * LANES * j + LANES],
                                                            o_ref[:, 2 * LANES * j + LANES:2 * LANES * (j + 1)], lane)
        oc = oc_s[...]
        zv = z_ref[...]
        sg = _sigmoid(zv)
        sz = zv * sg
        o2 = (oc * sz).astype(BF16)
        o2b_ref[...] = o2
        e = x1_ref[...] + _dot(o2, w_ref[...]) - t_ref[...]
        sq = jnp.sum(jnp.sum(e * e, axis=1, keepdims=True), axis=0, keepdims=True)
        loss_ref[...] += jnp.broadcast_to(sq * (0.5 / D), (1, LANES))
        dx2 = e * (1.0 / D)
        dx2_ref[...] = dx2
        dx2b = dx2.astype(BF16)
        dx2b_ref[...] = dx2b
        do2 = _dot_nt(dx2b, w_ref[...])
        dz_ref[...] = (do2 * oc * (sg * (1.0 + zv * (1.0 - sg)))).astype(BF16)
        do_s[...] = do2 * sz
        deltas = jnp.zeros((tm, LANES), F32)
        for hd in range(H):
            dt = _head_tile(do_s, hd, lane)
            delta = jnp.sum(dt * _head_tile(oc_s, hd, lane), axis=1, keepdims=True)
            deltas = deltas + jnp.where(lane == hd, delta, 0.0)
            doa_ref[:, LANES * hd:LANES * (hd + 1)] = dt.astype(BF16)
        doa_ref[...] += _dot(_pack3(-deltas, lane, 0.0), first_ref[...]).astype(BF16)
        lse = jnp.concatenate([lse_ref[...], jnp.zeros((LANES - H, tm), F32)], axis=0).T
        rq = rel_ref[...] - lse
        tile_lane = lax.broadcasted_iota(jnp.int32, (1, H * LANES), 1)
        extra = tile_lane % LANES - HEAD_DIM * (1 - (tile_lane // LANES) % 2)
        kept = jnp.where((extra >= 0) & (extra < 3), jnp.zeros((), BF16), q_ref[...])
        qa2_ref[...] = kept + _dot(_pack3(rq, lane, 0.0), first_ref[...]).astype(BF16)

    return pl.pallas_call(
        body, name="attn_out", grid=(S // tm,),
        in_specs=[_rows(tm, H * LANES), _rows(tm, D), _rows(tm, D), _rows(tm, D), _whole((D, D)),
                  _rows(tm, H * LANES), _whole((LANES, H * LANES)), _rows(tm, LANES),
                  pl.BlockSpec((H, tm), lambda i: (0, i))],
        out_specs=[_rows(tm, D), _rows(tm, D), _rows(tm, D), _rows(tm, D), _rows(tm, H * LANES),
                   _rows(tm, H * LANES), _whole((1, LANES))],
        out_shape=[jax.ShapeDtypeStruct((S, D), F32), jax.ShapeDtypeStruct((S, D), BF16),
                   jax.ShapeDtypeStruct((S, D), BF16), jax.ShapeDtypeStruct((S, D), BF16),
                   jax.ShapeDtypeStruct((S, H * LANES), BF16), jax.ShapeDtypeStruct((S, H * LANES), BF16),
                   jax.ShapeDtypeStruct((1, LANES), F32)],
        scratch_shapes=[pltpu.VMEM((tm, D), F32), pltpu.VMEM((tm, D), F32)],
        compiler_params=_params(1),
    )(o_aug, z, x1, target, w_out, qa, _scatter_matrices(H)[0], rel, lse)


def _attn_bwd(qend, qa2, doa, ka, va, T):
    S = qa2.shape[0]
    H = qa2.shape[1] // LANES
    nb = S // T
    G = 2 * nb // qend.shape[1]
    W = G * T

    def body(qe_ref, q_ref, do_ref, k_ref, v_ref, dq_ref, dk_ref, dv_ref, dkt_acc, dvt_acc):
        h, g = pl.program_id(0), pl.program_id(1)

        @pl.when(g == 0)
        def _():
            dq_ref[...] = jnp.zeros((S, LANES), F32)

        kb = k_ref[...]
        vb = v_ref[...]
        dkt_acc[...] = jnp.zeros((LANES, W), F32)
        dvt_acc[...] = jnp.zeros((LANES, W), F32)

        def step(qi, c0, c1, masked):
            rows = pl.ds(pl.multiple_of(qi * T, T), 2 * T)
            qb = q_ref[rows, :]
            dob = do_ref[rows, :]
            s = _dot_nt(qb, kb[c0:c1])
            if masked:
                query = lax.broadcasted_iota(jnp.int32, s.shape, 0) + (c1 - 2 * T)
                s = jnp.where(lax.broadcasted_iota(jnp.int32, s.shape, 1) <= query, s, NEG)
            p = jnp.exp(s)
            ds = (p * _dot_nt(dob, vb[c0:c1])).astype(BF16)
            dvt_acc[:, c0:c1] += _dot(dob.astype(F32).T.astype(BF16), p.astype(BF16))
            dkt_acc[:, c0:c1] += _dot(qb.astype(F32).T.astype(BF16), ds)
            dq_ref[rows, :] += _dot(ds, kb[c0:c1])

        for m in range(G // 2):
            step(g * G + 2 * m, 0, (m + 1) * 2 * T, True)
        first = g * G + G
        n_all = jnp.maximum((qe_ref[h, 2 * g] - first + 2) // 2, 0)
        second = first + 2 * n_all

        def all_keys(i, carry):
            step(first + 2 * i, 0, W, False)
            return carry

        def late_keys(i, carry):
            step(second + 2 * i, W // 2, W, False)
            return carry

        lax.fori_loop(0, n_all, all_keys, 0)
        lax.fori_loop(0, (qe_ref[h, 2 * g + 1] - second + 2) // 2, late_keys, 0)
        dk_ref[...] = dkt_acc[...].T
        dv_ref[...] = dvt_acc[...].T.astype(BF16)

    heads = pl.BlockSpec((None, W, LANES), lambda h, i, qe: (h, i, 0))
    return pl.pallas_call(
        body, name="attn_bwd",
        grid_spec=pltpu.PrefetchScalarGridSpec(
            num_scalar_prefetch=1, grid=(H, nb // G),
            in_specs=[pl.BlockSpec((S, LANES), lambda h, i, qe: (0, h)), pl.BlockSpec((S, LANES), lambda h, i, qe: (0, h)),
                      pl.BlockSpec((W, LANES), lambda h, i, qe: (i, h)), pl.BlockSpec((W, LANES), lambda h, i, qe: (i, h))],
            out_specs=[pl.BlockSpec((None, S, LANES), lambda h, i, qe: (h, 0, 0)), heads, heads],
            scratch_shapes=[pltpu.VMEM((LANES, W), F32), pltpu.VMEM((LANES, W), F32)]),
        out_shape=[jax.ShapeDtypeStruct((H, S, LANES), F32), jax.ShapeDtypeStruct((H, S, LANES), F32),
                   jax.ShapeDtypeStruct((H, S, LANES), BF16)],
        compiler_params=_params(2),
    )(qend, qa2, doa, ka, va)


def _attn_proj_bwd(dqt, dka, dva, qraw, kraw, dz, f, gq, gk, w, wf, x1, g2, dx2):
    S, D = x1.shape
    H = D // HEAD_DIM
    tm = min(ROW_TILE, S)
    last = S // tm - 1
    tri = (lax.broadcasted_iota(jnp.int32, (tm, tm), 1) >= lax.broadcasted_iota(jnp.int32, (tm, tm), 0)).astype(BF16)

    def body(dq_ref, dk_ref, dv_ref, q_ref, k_ref, dz_ref, f_ref, gq_ref, gk_ref, w_ref, wf_ref, x1_ref, g2_ref,
             dx2_ref, tri_ref, dproj_ref, dx1_ref, dg2_ref, small_ref, carry, pairs):
        @pl.when(pl.program_id(0) == 0)
        def _():
            dg2_ref[...] = jnp.zeros((1, D), F32)
            small_ref[...] = jnp.zeros((8, LANES), F32)
            carry[...] = jnp.zeros((8, LANES), F32)

        lane = _lane()

        def head_pair(j, acc):
            dcs, dgq, dgk = acc
            dq2, dk2 = [], []
            q_pair, k_pair = q_ref[j], k_ref[j]
            for parity in (0, 1):
                hd = 2 * j + parity
                own, a = _own(lane, parity), _aug(parity)
                dqf = dq_ref[hd]
                dqn = jnp.where(own, dqf * Q_SCALE, 0.0)
                d, dg = _head_rms_bwd(dqn, jnp.where(own, q_pair, 0.0), gq_ref[...])
                dq2.append(d)
                dgq = dgq + dg
                dkt = dk_ref[hd]
                dcs = dcs + jnp.where(lane == hd, _col(dqf, lane, a) - _col(dkt, lane, a + 3), 0.0)
                d, dg = _head_rms_bwd(jnp.where(own, dkt, 0.0), jnp.where(own, k_pair, 0.0), gk_ref[...])
                dk2.append(d)
                dgk = dgk + dg
            pairs[0, j] = _pair_tile(*dq2, lane).astype(BF16)
            pairs[1, j] = _pair_tile(*dk2, lane).astype(BF16)
            pairs[2, j] = _pair_tile(dv_ref[2 * j], dv_ref[2 * j + 1], lane)
            return dcs, dgq, dgk

        zero = jnp.zeros((1, LANES), F32)
        dcs, dgq, dgk = lax.fori_loop(0, H // 2, head_pair, (jnp.zeros((tm, LANES), F32), zero, zero))
        for part in range(3):
            for j in range(H // 2):
                dproj_ref[:, part * D + LANES * j:part * D + LANES * (j + 1)] = pairs[part, j]
        dproj_ref[:, 3 * D:4 * D] = dz_ref[...]
        dlogf = _dot01(tri_ref[...], dcs) + carry[0:1, :]
        carry[...] = jnp.broadcast_to(dlogf[0:1, :], (8, LANES))
        df = dlogf * (1.0 / (1.0 + jnp.exp(f_ref[...])))
        dproj_ref[:, 4 * D:4 * D + LANES] = df.astype(BF16)
        small_ref[0:1, :] += jnp.sum(df, axis=0, keepdims=True)
        small_ref[1:2, :] += dgq
        small_ref[2:3, :] += dgk
        dh = _dot_nt(dproj_ref[:, 0:4 * D], w_ref[...]) + _dot_nt(dproj_ref[:, 4 * D:4 * D + LANES], wf_ref[...])
        dxn, dg = _rms_bwd(dh, x1_ref[...], g2_ref[...])
        dx1_ref[...] = dx2_ref[...] + dxn
        dg2_ref[...] += dg

    W = 4 * D + LANES
    heads = pl.BlockSpec((H, tm, LANES), lambda i: (0, last - i, 0))
    head_pairs = pl.BlockSpec((H // 2, tm, LANES), lambda i: (0, last - i, 0))
    return pl.pallas_call(
        body, name="attn_proj_bwd", grid=(S // tm,),
        in_specs=[heads, heads, heads, head_pairs, head_pairs,
                  _rows(tm, D, last), _rows(tm, LANES, last), _whole((1, LANES)), _whole((1, LANES)),
                  _whole((D, 4 * D)), _whole((D, LANES)), _rows(tm, D, last), _whole((1, D)), _rows(tm, D, last),
                  _whole((tm, tm))],
        out_specs=[_rows(tm, W, last), _rows(tm, D, last), _whole((1, D)), _whole((8, LANES))],
        out_shape=[jax.ShapeDtypeStruct((S, W), BF16), jax.ShapeDtypeStruct((S, D), F32),
                   jax.ShapeDtypeStruct((1, D), F32), jax.ShapeDtypeStruct((8, LANES), F32)],
        scratch_shapes=[pltpu.VMEM((8, LANES), F32), pltpu.VMEM((3, H // 2, tm, LANES), BF16)],
        compiler_params=_params(1),
    )(dqt, dka, dva, qraw, kraw, dz, f, gq, gk, w, wf, x1, g2, dx2, tri)


def _matmul_tn(a, b, col0, n, tn, name, stacked=False):
    S, M = a.shape
    ts = min(TN_ROWS, S)
    off = col0 // tn

    def body(a_ref, b_ref, o_ref):
        @pl.when(pl.program_id(1) == 0)
        def _():
            o_ref[...] = jnp.zeros((M, tn), F32)

        o_ref[...] += _dot_tn(a_ref[...], b_ref[...])

    if stacked:
        out_spec, out_shape = pl.BlockSpec((None, M, tn), lambda j, s: (j, 0, 0)), (n // tn, M, tn)
    else:
        out_spec, out_shape = pl.BlockSpec((M, tn), lambda j, s: (0, j)), (M, n)
    return pl.pallas_call(
        body, name=name, grid=(n // tn, S // ts),
        in_specs=[pl.BlockSpec((ts, M), lambda j, s: (s, 0)), pl.BlockSpec((ts, tn), lambda j, s: (s, off + j))],
        out_specs=out_spec, out_shape=jax.ShapeDtypeStruct(out_shape, F32),
        compiler_params=_params(2),
    )(a, b)


def _adamw(w, g, m, v, name):
    r, c = w.shape
    tr = ROW_TILE if r % ROW_TILE == 0 else r

    def body(w_ref, g_ref, m_ref, v_ref, d_ref, m2_ref, v2_ref):
        gv = g_ref[...]
        m2 = ADAM_B1 * m_ref[...] + (1.0 - ADAM_B1) * gv
        v2 = ADAM_B2 * v_ref[...] + (1.0 - ADAM_B2) * (gv * gv)
        m2_ref[...] = m2
        v2_ref[...] = v2
        m_hat = m2 / (1.0 - ADAM_B1 ** ADAM_STEP)
        v_hat = v2 / (1.0 - ADAM_B2 ** ADAM_STEP)
        d_ref[...] = -ADAM_LR * (m_hat / (jnp.sqrt(v_hat) + ADAM_EPS) + ADAM_WD * w_ref[...])

    spec = _rows(tr, c)
    return pl.pallas_call(
        body, name=name, grid=(r // tr,), in_specs=[spec] * 4, out_specs=[spec] * 3,
        out_shape=[jax.ShapeDtypeStruct((r, c), F32)] * 3, compiler_params=_params(1),
    )(w, g, m, v)


def _local_step(x, target, g1, w_in, conv_w, w_out, g2, wa_in, b_f, gq, gk, wa_out):
    S, D = x.shape
    H = D // HEAD_DIM
    w_qkvz = wa_in[:, :4 * D]
    wf = jnp.pad(wa_in[:, 4 * D:], ((0, 0), (0, LANES - H)))
    bf = jnp.pad(b_f, ((0, 0), (0, LANES - H)))
    gq128 = jnp.concatenate([gq, gq], axis=1)
    gk128 = jnp.concatenate([gk, gk], axis=1)

    proj, h1, yc, y, x1 = _conv_fwd(x, g1, w_in, conv_w, w_out)
    h2, qraw, kraw, z, f, c, rel, qa, ka, va, vt = _attn_front(x1, g2, w_qkvz, wf, bf, gq128, gk128)
    T = vt.shape[3]
    kstart, qend, bound = _skip_tables(c[:, :H], gq, gk, T, min(ATT_GROUP, S // T))
    o_aug, lse = lax.cond(2.0 * bound <= PLAIN_EXP_MAX, functools.partial(_attn_fwd, online_max=False),
                          functools.partial(_attn_fwd, online_max=True), kstart, qa, ka, vt)
    dx2, dx2b, o2b, dz, doa, qa2, loss = _attn_out(o_aug, lse.reshape(H, S), rel, z, x1, target, wa_out, qa)
    dqt, dka, dva = _attn_bwd(qend, qa2, doa, ka, va, T)
    dproj2, dx1, dg2, small = _attn_proj_bwd(dqt, dka, dva, qraw, kraw, dz, f, gq128, gk128, w_qkvz, wf, x1, g2, dx2)
    dproj1, dx, dx1b, dg1, dcw = _conv_bwd(dx1, x, g1, w_in, w_out, conv_w, proj, yc)

    tn = min(1024, D)
    dwa_out = _matmul_tn(o2b, dx2b, 0, D, tn, "dw_attn_out")
    dwa_in = jnp.concatenate([_matmul_tn(h2, dproj2, 0, 4 * D, tn, "dw_attn_in"),
                              _matmul_tn(h2, dproj2, 4 * D, LANES, LANES, "dw_attn_f")[:, :H]], axis=1)
    dw_out = _matmul_tn(y, dx1b, 0, D, tn, "dw_conv_out")
    dw_in = _matmul_tn(h1, dproj1, 0, 4 * D, D, "dw_conv_in", stacked=True)
    grads = dict(conv_norm_g=dg1, conv_w_in=dw_in, conv_w=dcw, conv_w_out=dw_out, attn_norm_g=dg2,
                 attn_w_in=dwa_in, attn_b_f=small[0:1, :H],
                 attn_q_norm_g=small[1:2, :HEAD_DIM] + small[1:2, HEAD_DIM:],
                 attn_k_norm_g=small[2:3, :HEAD_DIM] + small[2:3, HEAD_DIM:], attn_w_out=dwa_out)
    return loss[0, 0], dx, grads


def _coords():
    return lax.axis_index("x"), lax.axis_index("y"), lax.axis_index("c")


def _at(ref, idx):
    return ref.at[idx] if idx else ref


def _other_chips(x, y):
    return [(1 - x, y), (x, 1 - y), (1 - x, 1 - y)]


def _all_gather(halved, whole):
    nh, nw = len(halved), len(whole)

    def body(*refs):
        src_h, src_w = refs[:nh], refs[nh:nh + nw]
        out_h, out_w = refs[nh + nw:2 * nh + nw], refs[2 * nh + nw:2 * (nh + nw)]
        send_h, recv_h, send_w, recv_w = refs[2 * (nh + nw):]
        x, y, c = _coords()
        mine = 2 * x + y
        sibling = (x, y, 1 - c)
        chips = _other_chips(x, y)

        def copy_h(a, k, chip, half, to, src=None):
            dst = out_h[a].at[chip, half]
            return pltpu.make_async_remote_copy(src_ref=dst if src is None else src, dst_ref=dst,
                                                send_sem=send_h.at[a, k], recv_sem=recv_h.at[a, k],
                                                device_id=to, device_id_type=MESH)

        def copy_w(a, k, chip, to):
            return pltpu.make_async_remote_copy(src_ref=src_w[a], dst_ref=out_w[a].at[chip],
                                                send_sem=send_w.at[a, k], recv_sem=recv_w.at[a, k],
                                                device_id=to, device_id_type=MESH)

        first = [copy_h(a, k, mine, c, (*chip, c), src=src_h[a].at[c]) for a in range(nh) for k, chip in enumerate(chips)]
        first += [copy_w(a, k, mine, (*chip, c)) for a in range(nw) for k, chip in enumerate(chips)]
        for cp in first:
            cp.start()
        passed = []
        for a in range(nh):
            for k, (px, py) in enumerate(chips):
                copy_h(a, k, 2 * px + py, c, (x, y, c)).wait_recv()
                cp = copy_h(a, 3 + k, 2 * px + py, c, sibling)
                cp.start()
                passed.append(cp)
        for a in range(nh):
            for k, (px, py) in enumerate(chips):
                copy_h(a, 3 + k, 2 * px + py, 1 - c, (x, y, c)).wait_recv()
        for a in range(nw):
            for k, (px, py) in enumerate(chips):
                copy_w(a, k, 2 * px + py, (x, y, c)).wait_recv()
        for cp in first + passed:
            cp.wait_send()

    out_shape = [jax.ShapeDtypeStruct((4,) + a.shape, a.dtype) for a in list(halved) + list(whole)]
    gathered = pl.pallas_call(
        body, name="gather_weights", in_specs=[ANY] * (nh + nw), out_specs=[ANY] * (nh + nw), out_shape=out_shape,
        scratch_shapes=[pltpu.SemaphoreType.DMA((nh, 6)), pltpu.SemaphoreType.DMA((nh, 6)),
                        pltpu.SemaphoreType.DMA((nw, 3)), pltpu.SemaphoreType.DMA((nw, 3))],
    )(*halved, *whole)
    x, y, _ = _coords()
    return [lax.dynamic_update_index_in_dim(g, a, 2 * x + y, axis=0) for g, a in zip(gathered, list(halved) + list(whole))]


def _exchange(name, srcs, lands, copies, local_copies):
    ns, nl, n, nloc = len(srcs), len(lands), len(copies), len(local_copies)

    def body(*refs):
        src, land = refs[:ns], refs[ns:ns + nl]
        send, recv, local_sem = refs[ns + nl:]
        me = _coords()
        started = []
        for k, (si, s_at, li, l_at, ci) in enumerate(local_copies):
            cp = pltpu.make_async_copy(_at(src[si], s_at(*me)), _at(land[li], l_at(*me)), local_sem.at[k])
            cp.start()
            started.append(cp)
        remote = []
        for k, (si, s_at, li, l_at, peer) in enumerate(copies):
            cp = pltpu.make_async_remote_copy(src_ref=_at(src[si], s_at(*me)), dst_ref=_at(land[li], l_at(*me)),
                                              send_sem=send.at[k], recv_sem=recv.at[k],
                                              device_id=peer(*me), device_id_type=MESH)
            cp.start()
            remote.append(cp)
        for cp in remote:
            cp.wait()
        for cp in started:
            cp.wait()

    return pl.pallas_call(
        body, name=name, in_specs=[ANY] * ns, out_specs=[ANY] * nl, out_shape=list(lands),
        scratch_shapes=[pltpu.SemaphoreType.DMA((n,)), pltpu.SemaphoreType.DMA((n,)),
                        pltpu.SemaphoreType.DMA((max(nloc, 1),))],
    )(*srcs)


def _add_pairs(a, b, name):
    _, r, cols = a.shape
    tr = ROW_TILE if r % ROW_TILE == 0 else r

    def body(a_ref, b_ref, o_ref, ob_ref):
        s = a_ref[...] + b_ref[...]
        o_ref[...] = s
        ob_ref[...] = s.astype(BF16)

    spec = pl.BlockSpec((None, tr, cols), lambda j, i: (j, i, 0))
    return pl.pallas_call(
        body, name=name, grid=(4, r // tr), in_specs=[spec, spec], out_specs=[spec, spec],
        out_shape=[jax.ShapeDtypeStruct(a.shape, F32), jax.ShapeDtypeStruct(a.shape, BF16)],
        compiler_params=_params(2),
    )(a, b)


def _sum_chips(own, landed, name):
    _, r, cols = landed.shape
    tr = ROW_TILE if r % ROW_TILE == 0 else r

    def body(own_ref, land_ref, o_ref):
        acc = own_ref[...]
        for j in range(3):
            acc = acc + land_ref[j].astype(F32)
        o_ref[...] = acc

    return pl.pallas_call(
        body, name=name, grid=(r // tr,),
        in_specs=[_rows(tr, cols), pl.BlockSpec((3, tr, cols), lambda i: (0, i, 0))], out_specs=_rows(tr, cols),
        out_shape=jax.ShapeDtypeStruct((r, cols), F32), compiler_params=_params(1),
    )(own, landed)


def _sum_devices(landed, name):
    def body(l_ref, o_ref):
        acc = l_ref[0]
        for j in range(1, 8):
            acc = acc + l_ref[j]
        o_ref[...] = acc

    return pl.pallas_call(body, name=name, out_shape=jax.ShapeDtypeStruct(landed.shape[1:], F32))(landed)


def _reduce_gradients(big, small):
    nb = len(big)
    x, y, c = _coords()
    mine = 2 * x + y
    flips = [(fx, fy, fc) for fx in (0, 1) for fy in (0, 1) for fc in (0, 1) if fx or fy or fc]

    def flip(fx, fy, fc):
        return lambda x, y, c: (x ^ fx, y ^ fy, c ^ fc)

    copies = [(a, (lambda j: lambda x, y, c: (j, 1 - c))(j), a, (lambda j: lambda x, y, c: (j,))(j), flip(0, 0, 1))
              for a in range(nb) for j in range(4)]
    copies += [(nb, lambda x, y, c: (), nb, lambda x, y, c: (4 * x + 2 * y + c,), flip(*f)) for f in flips]
    lands = [jax.ShapeDtypeStruct((4,) + g.shape[2:], F32) for g in big] + [jax.ShapeDtypeStruct((8,) + small.shape, F32)]
    local = [(nb, lambda x, y, c: (), nb, lambda x, y, c: (4 * x + 2 * y + c,), None)]
    landed = _exchange("swap_halves", list(big) + [small], lands, copies, local)
    small_sum = _sum_devices(landed[nb], "sum_small")

    chip_f32, chip_bf16 = [], []
    for a in range(nb):
        kept = lax.dynamic_index_in_dim(big[a], c, axis=1, keepdims=False)
        s, sb = _add_pairs(kept, landed[a], f"add_cores_{a}")
        chip_f32.append(s)
        chip_bf16.append(sb)

    chip_flips = [(1, 0), (0, 1), (1, 1)]
    copies = [(a, (lambda f: lambda x, y, c: (2 * (x ^ f[0]) + (y ^ f[1]),))(f), a, (lambda k: lambda x, y, c: (k,))(k),
               flip(f[0], f[1], 0)) for a in range(nb) for k, f in enumerate(chip_flips)]
    lands = [jax.ShapeDtypeStruct((3,) + g.shape[1:], BF16) for g in chip_bf16]
    landed = _exchange("send_chip_sums", chip_bf16, lands, copies, [])
    totals = [_sum_chips(lax.dynamic_index_in_dim(chip_f32[a], mine, axis=0, keepdims=False), landed[a],
                         f"sum_chips_{a}") for a in range(nb)]

    copies = [(a, lambda x, y, c: (), a, lambda x, y, c: (), flip(0, 0, 1)) for a in range(nb)]
    lands = [jax.ShapeDtypeStruct(t.shape, F32) for t in totals]
    landed = _exchange("swap_sums", totals, lands, copies, [])
    return [jnp.stack([jnp.where(c == 0, t, l), jnp.where(c == 0, l, t)]) for t, l in zip(totals, landed)], small_sum


def kernel(x, conv_norm_g, conv_w_in, conv_w, conv_w_out, attn_norm_g, attn_w_in, attn_b_f, attn_q_norm_g, attn_k_norm_g, attn_w_out, loss_target, m_conv_norm_g, m_conv_w_in, m_conv_w, m_conv_w_out, m_attn_norm_g, m_attn_w_in, m_attn_b_f, m_attn_q_norm_g, m_attn_k_norm_g, m_attn_w_out, v_conv_norm_g, v_conv_w_in, v_conv_w, v_conv_w_out, v_attn_norm_g, v_attn_w_in, v_attn_b_f, v_attn_q_norm_g, v_attn_k_norm_g, v_attn_w_out):
    xi, yi, _ = _coords()
    chip = 2 * xi + yi
    D = x.shape[2]
    H = D // HEAD_DIM
    names = ["conv_norm_g", "conv_w_in", "conv_w", "conv_w_out", "attn_norm_g", "attn_w_in", "attn_b_f",
             "attn_q_norm_g", "attn_k_norm_g", "attn_w_out"]
    weights = dict(zip(names, [conv_norm_g, conv_w_in, conv_w, conv_w_out, attn_norm_g, attn_w_in, attn_b_f,
                               attn_q_norm_g, attn_k_norm_g, attn_w_out]))
    m_in = dict(zip(names, [m_conv_norm_g, m_conv_w_in, m_conv_w, m_conv_w_out, m_attn_norm_g, m_attn_w_in,
                            m_attn_b_f, m_attn_q_norm_g, m_attn_k_norm_g, m_attn_w_out]))
    v_in = dict(zip(names, [v_conv_norm_g, v_conv_w_in, v_conv_w, v_conv_w_out, v_attn_norm_g, v_attn_w_in,
                            v_attn_b_f, v_attn_q_norm_g, v_attn_k_norm_g, v_attn_w_out]))
    weights = {k: w[0] for k, w in weights.items()}
    m_in = {k: w[0] for k, w in m_in.items()}
    v_in = {k: w[0] for k, w in v_in.items()}

    big_names = ["conv_w_in", "attn_w_in", "conv_w_out", "attn_w_out"]
    halved = [weights[k].astype(BF16).reshape(2, weights[k].shape[0] // 2, weights[k].shape[1]) for k in big_names]
    q = D // 4
    small_w = jnp.concatenate([weights["conv_w"], weights["attn_norm_g"][None, :], jnp.zeros((4, q), F32)], axis=0)
    g_in, ga_in, g_out, ga_out, g_small = _all_gather(halved, [small_w])
    w_in = g_in.reshape(4, D, D)
    wa_in = ga_in.reshape(4, D, D + H // 4).transpose(1, 0, 2).reshape(D, 4 * D + H)
    w_out = g_out.reshape(D, D)
    wa_out = ga_out.reshape(D, D)
    conv_w_full = g_small[:, 0:3, :].transpose(1, 0, 2).reshape(3, D)
    attn_g_full = g_small[:, 3, :].reshape(1, D)

    loss_part, grad_x, grads = _local_step(x[0], loss_target[0], weights["conv_norm_g"][None, :], w_in, conv_w_full,
                                           w_out, attn_g_full, wa_in, weights["attn_b_f"][None, :],
                                           weights["attn_q_norm_g"][None, :], weights["attn_k_norm_g"][None, :], wa_out)

    big = [grads["conv_w_in"].reshape(4, 2, D // 2, D),
           grads["attn_w_in"].reshape(D, 4, D + H // 4).transpose(1, 0, 2).reshape(4, 2, D // 2, D + H // 4),
           grads["conv_w_out"].reshape(4, 2, D // 8, D), grads["attn_w_out"].reshape(4, 2, D // 8, D)]
    tail = jnp.concatenate([grads["attn_b_f"], grads["attn_q_norm_g"], grads["attn_k_norm_g"],
                            jnp.reshape(loss_part, (1, 1)), jnp.zeros((1, D - H - 2 * HEAD_DIM - 1), F32)], axis=1)
    small = jnp.concatenate([grads["conv_norm_g"], grads["conv_w"], grads["attn_norm_g"], tail,
                             jnp.zeros((2, D), F32)], axis=0)
    reduced, small_sum = _reduce_gradients(big, small)
    final = {k: r.reshape(weights[k].shape) for k, r in zip(big_names, reduced)}
    final["conv_norm_g"] = small_sum[0]
    final["conv_w"] = lax.dynamic_slice_in_dim(small_sum[1:4], chip * q, q, axis=1)
    final["attn_norm_g"] = lax.dynamic_slice_in_dim(small_sum[4], chip * q, q, axis=0)
    final["attn_b_f"] = small_sum[5, :H]
    final["attn_q_norm_g"] = small_sum[5, H:H + HEAD_DIM]
    final["attn_k_norm_g"] = small_sum[5, H + HEAD_DIM:H + 2 * HEAD_DIM]
    loss = small_sum[5, H + 2 * HEAD_DIM]

    delta, new_m, new_v = {}, {}, {}
    for k in names:
        shape = weights[k].shape
        as2d = (lambda a: a.reshape(1, -1)) if len(shape) == 1 else (lambda a: a)
        d, m2, v2 = _adamw(as2d(weights[k]), as2d(final[k]), as2d(m_in[k]), as2d(v_in[k]), "adamw_" + k)
        delta[k], new_m[k], new_v[k] = d.reshape(shape), m2.reshape(shape), v2.reshape(shape)
    lead = lambda a: a[None]
    return (loss, grad_x[None], *[lead(final[k]) for k in names], *[lead(delta[k]) for k in names],
            *[lead(new_m[k]) for k in names], *[lead(new_v[k]) for k in names])
```

```python
import functools

import jax
import jax.numpy as jnp
from jax import lax
from jax.experimental import pallas as pl
from jax.experimental.pallas import tpu as pltpu

F32 = jnp.float32
BF16 = jnp.bfloat16
HEAD_DIM = 64
LANES = 128
RMS_EPS = 1e-6
NEG = -1e30
Q_SCALE = 0.125
ROW_TILE = 256
ATT_GROUP = 4
SKIP_LOG = 106.0
PLAIN_EXP_MAX = 60.0
TN_ROWS = 2048
VMEM_LIMIT = 56 << 20
ADAM_LR, ADAM_B1, ADAM_B2, ADAM_EPS, ADAM_WD, ADAM_STEP = 0.001, 0.9, 0.999, 1e-08, 0.01, 10
MESH = pl.DeviceIdType.MESH
ANY = pl.BlockSpec(memory_space=pl.ANY)


def _lane():
    return lax.broadcasted_iota(jnp.int32, (1, LANES), 1)


def _split3(x):
    hi = x.astype(BF16).astype(F32)
    r = x - hi
    mid = r.astype(BF16).astype(F32)
    lo = (r - mid).astype(BF16).astype(F32)
    return hi, mid, lo


STAT_STRIDE = 16
ONE_LANE = 3 * STAT_STRIDE


def _pack3(x, lane, one):
    hi, mid, lo = _split3(x)
    packed = hi + pltpu.roll(mid, STAT_STRIDE, 1) + pltpu.roll(lo, 2 * STAT_STRIDE, 1)
    return jnp.where(lane == ONE_LANE, one, packed).astype(BF16)


def _scatter_matrices(H):
    rows = lax.broadcasted_iota(jnp.int32, (LANES, H * LANES), 0)
    cols = lax.broadcasted_iota(jnp.int32, (LANES, H * LANES), 1)
    head, within = cols // LANES, cols % LANES
    extra = within - HEAD_DIM * (1 - head % 2)
    term = (rows < ONE_LANE) & (rows % STAT_STRIDE == head)
    first = ((term & (extra == rows // STAT_STRIDE)) | ((rows == ONE_LANE) & (extra >= 3) & (extra < 6)))
    second = ((term & (extra - 3 == rows // STAT_STRIDE)) | ((rows == ONE_LANE) & (extra >= 0) & (extra < 3)))
    return first.astype(BF16), second.astype(BF16)


def _gather_matrix(H, lo):
    rows = lax.broadcasted_iota(jnp.int32, (H * LANES, LANES), 0)
    cols = lax.broadcasted_iota(jnp.int32, (H * LANES, LANES), 1)
    extra = rows % LANES - HEAD_DIM * (1 - (rows // LANES) % 2)
    return ((rows // LANES == cols) & (extra >= lo) & (extra < lo + 3)).astype(BF16)


def _put(base, lane, start, parts):
    for j, p in enumerate(parts):
        base = jnp.where(lane == start + j, p, base)
    return base


def _col(x, lane, idx):
    return jnp.sum(jnp.where(lane == idx, x, 0.0), axis=1, keepdims=True)


def _feat(parity):
    return HEAD_DIM * parity


def _aug(parity):
    return HEAD_DIM * (1 - parity)


def _own(lane, parity):
    return (lane >= _feat(parity)) & (lane < _feat(parity) + HEAD_DIM)


def _head_tile(ref, hd, lane):
    j = hd // 2
    return jnp.where(_own(lane, hd % 2), ref[:, LANES * j:LANES * (j + 1)], 0.0)


def _pair_tile(even, odd, lane):
    return jnp.where(lane < HEAD_DIM, even, odd)


def _sigmoid(x):
    return 1.0 / (1.0 + jnp.exp(-x))


def _dot(a, b):
    return jnp.dot(a, b, preferred_element_type=F32)


def _dot_nt(a, b):
    return lax.dot_general(a, b, (((1,), (1,)), ((), ())), preferred_element_type=F32)


def _dot_tn(a, b):
    return lax.dot_general(a, b, (((0,), (0,)), ((), ())), preferred_element_type=F32)


def _dot01(tri, x):
    hi, mid, lo = _split3(x)
    return _dot(tri, hi.astype(BF16)) + _dot(tri, mid.astype(BF16)) + _dot(tri, lo.astype(BF16))


def _rms_bwd(dh, x, g):
    inv = lax.rsqrt(jnp.mean(x * x, axis=-1, keepdims=True) + RMS_EPS)
    xh = x * inv
    dxn = dh * g
    dx = inv * (dxn - xh * jnp.mean(dxn * xh, axis=-1, keepdims=True))
    return dx, jnp.sum(dh * xh, axis=0, keepdims=True)


def _head_rms_bwd(dn, t, g):
    inv = lax.rsqrt(jnp.sum(t * t, axis=1, keepdims=True) * (1.0 / HEAD_DIM) + RMS_EPS)
    th = t * inv
    gd = dn * g
    d = inv * (gd - th * (jnp.sum(gd * th, axis=1, keepdims=True) * (1.0 / HEAD_DIM)))
    return d, jnp.sum(dn * th, axis=0, keepdims=True)


def _params(n_grid):
    return pltpu.CompilerParams(dimension_semantics=("arbitrary",) * n_grid, vmem_limit_bytes=VMEM_LIMIT)


def _rows(tm, cols, rev=None):
    if rev is None:
        return pl.BlockSpec((tm, cols), lambda i: (i, 0))
    return pl.BlockSpec((tm, cols), lambda i: (rev - i, 0))


def _whole(shape):
    return pl.BlockSpec(shape, lambda *_: (0,) * len(shape))


def _conv_fwd(x, g1, w_in, conv_w, w_out):
    S, D = x.shape
    tm = min(ROW_TILE, S)

    def body(x_ref, g_ref, win_ref, cw_ref, wout_ref, proj_ref, h_ref, yc_ref, y_ref, x1_ref, prev_u):
        i = pl.program_id(0)
        xv = x_ref[...]
        inv = lax.rsqrt(jnp.mean(xv * xv, axis=-1, keepdims=True) + RMS_EPS)
        h = (xv * inv * g_ref[...]).astype(BF16)
        h_ref[...] = h
        for j in range(4):
            proj_ref[:, j * D:(j + 1) * D] = _dot(h, win_ref[j])

        @pl.when(i == 0)
        def _():
            prev_u[...] = jnp.zeros((tm, D), F32)

        u = proj_ref[:, D:2 * D] * proj_ref[:, 2 * D:3 * D]
        pu = prev_u[...]
        row = lax.broadcasted_iota(jnp.int32, (tm, 1), 0)
        u1 = jnp.where(row < 1, pltpu.roll(pu, 1, 0), pltpu.roll(u, 1, 0))
        u2 = jnp.where(row < 2, pltpu.roll(pu, 2, 0), pltpu.roll(u, 2, 0))
        prev_u[...] = u
        w = cw_ref[...]
        yc = w[2:3] * u + w[1:2] * u1 + w[0:1] * u2
        yc_ref[...] = yc
        z = proj_ref[:, 3 * D:4 * D]
        y = (proj_ref[:, 0:D] * yc * (z * _sigmoid(z))).astype(BF16)
        y_ref[...] = y
        x1_ref[...] = xv + _dot(y, wout_ref[...])

    return pl.pallas_call(
        body, name="conv_fwd", grid=(S // tm,),
        in_specs=[_rows(tm, D), _whole((1, D)), _whole((4, D, D)), _whole((3, D)), _whole((D, D))],
        out_specs=[_rows(tm, 4 * D), _rows(tm, D), _rows(tm, D), _rows(tm, D), _rows(tm, D)],
        out_shape=[jax.ShapeDtypeStruct((S, 4 * D), F32), jax.ShapeDtypeStruct((S, D), BF16),
                   jax.ShapeDtypeStruct((S, D), F32), jax.ShapeDtypeStruct((S, D), BF16),
                   jax.ShapeDtypeStruct((S, D), F32)],
        scratch_shapes=[pltpu.VMEM((tm, D), F32)],
        compiler_params=_params(1),
    )(x, g1, w_in, conv_w, w_out)


def _conv_bwd(dx1, x, g1, w_in, w_out, conv_w, proj, yc):
    S, D = x.shape
    tm = min(ROW_TILE, S)
    last = S // tm - 1

    def body(dx1_ref, x_ref, g_ref, win_ref, wout_ref, cw_ref, proj_ref, yc_ref,
             dproj_ref, dx_ref, dx1b_ref, dg_ref, dcw_ref, next_d):
        @pl.when(pl.program_id(0) == 0)
        def _():
            dg_ref[...] = jnp.zeros((1, D), F32)
            dcw_ref[...] = jnp.zeros((3, D), F32)
            next_d[...] = jnp.zeros((tm, D), F32)

        dx1v = dx1_ref[...]
        dx1b = dx1v.astype(BF16)
        dx1b_ref[...] = dx1b
        dy = _dot_nt(dx1b, wout_ref[...])
        b = proj_ref[:, 0:D]
        c = proj_ref[:, D:2 * D]
        xin = proj_ref[:, 2 * D:3 * D]
        z = proj_ref[:, 3 * D:4 * D]
        sg = _sigmoid(z)
        sz = z * sg
        ycv = yc_ref[...]
        d0 = dy * b * sz
        dproj_ref[:, 0:D] = (dy * ycv * sz).astype(BF16)
        dproj_ref[:, 3 * D:4 * D] = (dy * b * ycv * (sg * (1.0 + z * (1.0 - sg)))).astype(BF16)
        nd = next_d[...]
        row = lax.broadcasted_iota(jnp.int32, (tm, 1), 0)
        d1 = jnp.where(row >= tm - 1, pltpu.roll(nd, tm - 1, 0), pltpu.roll(d0, tm - 1, 0))
        d2 = jnp.where(row >= tm - 2, pltpu.roll(nd, tm - 2, 0), pltpu.roll(d0, tm - 2, 0))
        next_d[...] = d0
        w = cw_ref[...]
        du = w[2:3] * d0 + w[1:2] * d1 + w[0:1] * d2
        u = c * xin
        dcw_ref[2:3, :] += jnp.sum(d0 * u, axis=0, keepdims=True)
        dcw_ref[1:2, :] += jnp.sum(d1 * u, axis=0, keepdims=True)
        dcw_ref[0:1, :] += jnp.sum(d2 * u, axis=0, keepdims=True)
        dproj_ref[:, D:2 * D] = (du * xin).astype(BF16)
        dproj_ref[:, 2 * D:3 * D] = (du * c).astype(BF16)
        dh = _dot_nt(dproj_ref[:, 0:D], win_ref[0])
        for j in range(1, 4):
            dh = dh + _dot_nt(dproj_ref[:, j * D:(j + 1) * D], win_ref[j])
        dxn, dg = _rms_bwd(dh, x_ref[...], g_ref[...])
        dx_ref[...] = dx1v + dxn
        dg_ref[...] += dg

    return pl.pallas_call(
        body, name="conv_bwd", grid=(S // tm,),
        in_specs=[_rows(tm, D, last), _rows(tm, D, last), _whole((1, D)), _whole((4, D, D)), _whole((D, D)),
                  _whole((3, D)), _rows(tm, 4 * D, last), _rows(tm, D, last)],
        out_specs=[_rows(tm, 4 * D, last), _rows(tm, D, last), _rows(tm, D, last), _whole((1, D)), _whole((3, D))],
        out_shape=[jax.ShapeDtypeStruct((S, 4 * D), BF16), jax.ShapeDtypeStruct((S, D), F32),
                   jax.ShapeDtypeStruct((S, D), BF16), jax.ShapeDtypeStruct((1, D), F32),
                   jax.ShapeDtypeStruct((3, D), F32)],
        scratch_shapes=[pltpu.VMEM((tm, D), F32)],
        compiler_params=_params(1),
    )(dx1, x, g1, w_in, w_out, conv_w, proj, yc)


def _attn_front(x1, g2, w, wf, bf, gq, gk):
    S, D = x1.shape
    H = D // HEAD_DIM
    tm = min(ROW_TILE, S)
    tri = (lax.broadcasted_iota(jnp.int32, (tm, tm), 1) <= lax.broadcasted_iota(jnp.int32, (tm, tm), 0)).astype(BF16)

    def body(x_ref, g_ref, w_ref, wf_ref, bf_ref, gq_ref, gk_ref, tri_ref, first_ref, second_ref,
             h_ref, qh_ref, kh_ref, z_ref, f_ref, c_ref, rel_ref, qa_ref, ka_ref, va_ref, vt_ref,
             carry, v_s, qraw_ref, kraw_ref):
        @pl.when(pl.program_id(0) == 0)
        def _():
            carry[...] = jnp.zeros((8, LANES), F32)

        xv = x_ref[...]
        inv = lax.rsqrt(jnp.mean(xv * xv, axis=-1, keepdims=True) + RMS_EPS)
        h = (xv * inv * g_ref[...]).astype(BF16)
        h_ref[...] = h
        qraw_ref[...] = _dot(h, w_ref[:, 0:D])
        kraw_ref[...] = _dot(h, w_ref[:, D:2 * D])
        v_s[...] = _dot(h, w_ref[:, 2 * D:3 * D])
        z_ref[...] = _dot(h, w_ref[:, 3 * D:4 * D])
        lane = _lane()
        f = _dot(h, wf_ref[...]) + bf_ref[...]
        f_ref[...] = f
        logf = jnp.where(lane < H, jnp.minimum(f, 0.0) - jnp.log(1.0 + jnp.exp(-jnp.abs(f))), 0.0)
        cs = _dot01(tri_ref[...], logf) + carry[0:1, :]
        c_ref[...] = cs
        carry[...] = jnp.broadcast_to(cs[tm - 1:tm, :], (8, LANES))
        diags = jnp.zeros((tm, LANES), F32)
        for hd in range(H):
            sl = slice(LANES * hd, LANES * (hd + 1))
            a = _aug(hd % 2)
            if hd % 2 == 0:
                qh_ref[hd // 2] = qraw_ref[:, LANES * (hd // 2):LANES * (hd // 2 + 1)]
                kh_ref[hd // 2] = kraw_ref[:, LANES * (hd // 2):LANES * (hd // 2 + 1)]
            qt = _head_tile(qraw_ref, hd, lane)
            qn = qt * lax.rsqrt(jnp.sum(qt * qt, axis=1, keepdims=True) * (1.0 / HEAD_DIM) + RMS_EPS) * gq_ref[...]
            kt = _head_tile(kraw_ref, hd, lane)
            kn = kt * lax.rsqrt(jnp.sum(kt * kt, axis=1, keepdims=True) * (1.0 / HEAD_DIM) + RMS_EPS) * gk_ref[...]
            diags = diags + jnp.where(lane == hd, jnp.sum(qn * kn, axis=1, keepdims=True) * Q_SCALE, 0.0)
            qa_ref[:, sl] = (qn * Q_SCALE).astype(BF16)
            ka_ref[:, sl] = kn.astype(BF16)
            va = jnp.where((lane >= a) & (lane < a + 3), 1.0, _head_tile(v_s, hd, lane))
            va_ref[:, sl] = va.astype(BF16)
            vt_ref[hd] = va.T.astype(BF16)
        rel = cs - diags
        rel_ref[...] = rel
        qa_ref[...] += _dot(_pack3(rel, lane, 1.0), first_ref[...]).astype(BF16)
        ka_ref[...] += _dot(_pack3(-cs, lane, 1.0), second_ref[...]).astype(BF16)

    nb = S // tm
    heads = pl.BlockSpec((H // 2, tm, LANES), lambda i: (0, i, 0))
    return pl.pallas_call(
        body, name="attn_front", grid=(nb,),
        in_specs=[_rows(tm, D), _whole((1, D)), _whole((D, 4 * D)), _whole((D, LANES)), _whole((1, LANES)),
                  _whole((1, LANES)), _whole((1, LANES)), _whole((tm, tm)), _whole((LANES, H * LANES)),
                  _whole((LANES, H * LANES))],
        out_specs=[_rows(tm, D), heads, heads, _rows(tm, D), _rows(tm, LANES), _rows(tm, LANES), _rows(tm, LANES),
                   _rows(tm, H * LANES), _rows(tm, H * LANES), _rows(tm, H * LANES),
                   pl.BlockSpec((H, None, LANES, tm), lambda i: (0, i, 0, 0))],
        out_shape=[jax.ShapeDtypeStruct((S, D), BF16), jax.ShapeDtypeStruct((H // 2, S, LANES), F32),
                   jax.ShapeDtypeStruct((H // 2, S, LANES), F32), jax.ShapeDtypeStruct((S, D), F32),
                   jax.ShapeDtypeStruct((S, LANES), F32), jax.ShapeDtypeStruct((S, LANES), F32),
                   jax.ShapeDtypeStruct((S, LANES), F32),
                   jax.ShapeDtypeStruct((S, H * LANES), BF16), jax.ShapeDtypeStruct((S, H * LANES), BF16),
                   jax.ShapeDtypeStruct((S, H * LANES), BF16), jax.ShapeDtypeStruct((H, nb, LANES, tm), BF16)],
        scratch_shapes=[pltpu.VMEM((8, LANES), F32), pltpu.VMEM((tm, D), F32), pltpu.VMEM((tm, D), F32),
                        pltpu.VMEM((tm, D), F32)],
        compiler_params=_params(1),
    )(x1, g2, w, wf, bf, gq, gk, tri, *_scatter_matrices(H))


def _skip_tables(c, diag, gq, gk, T, G):
    nb = c.shape[0] // T
    bound = 8.0 * jnp.max(jnp.abs(gq)) * jnp.max(jnp.abs(gk))
    first, last = c[0::T, :], c[T - 1::T, :]
    lowest = jnp.maximum(jnp.min(diag.reshape(nb, T, -1), axis=1), -bound)
    idx = jnp.arange(nb)
    margin = (SKIP_LOG + bound) - lowest
    need = (last[None, :, :] <= first[:, None, :] + margin[:, None, :]) & (idx[None, :, None] < idx[:, None, None])
    need = need | (idx[None, :, None] == idx[:, None, None])
    kstart = jnp.argmax(need, axis=1)
    qend = nb - 1 - jnp.argmax(need[::-1], axis=0)
    kstart = jnp.min(kstart.reshape(2 * nb // G, G // 2, -1), axis=1)
    kstart = kstart - (kstart & 1)
    qend = jnp.max(qend.reshape(2 * nb // G, G // 2, -1), axis=1)
    return kstart.T.astype(jnp.int32), qend.T.astype(jnp.int32), bound


def _attn_fwd(kstart, qa, ka, vt, online_max):
    S = qa.shape[0]
    H = qa.shape[1] // LANES
    nb, T = vt.shape[1], vt.shape[3]
    G = 2 * nb // kstart.shape[1]
    W = G * T

    def finish(acc, shift, o_ref, lse_ref):
        a = _aug(pl.program_id(0) % 2)
        feat = lax.broadcasted_iota(jnp.int32, (LANES, 1), 0)
        l = jnp.sum(jnp.where(feat == a, acc, 0.0), axis=0, keepdims=True)
        o_ref[...] = (acc * (1.0 / l)).T
        lse_ref[...] = shift + jnp.log(l)

    def causal(st):
        return jnp.where(lax.broadcasted_iota(jnp.int32, st.shape, 0) <= lax.broadcasted_iota(jnp.int32, st.shape, 1),
                         st, NEG)

    def fast_body(ks_ref, q_ref, k_ref, vt_ref, o_ref, lse_ref, acc_ref, sa_ref, sb_ref, sc_ref):
        h, g = pl.program_id(0), pl.program_id(1)
        q = q_ref[...]
        acc_ref[...] = jnp.zeros((LANES, W), F32)

        def scores(ki, lo):
            return _dot_nt(k_ref[pl.ds(pl.multiple_of(ki * T, T), 2 * T), :], q[lo * T:, :])

        def weighted(ki, p):
            return _dot(vt_ref[ki], p[:T]) + _dot(vt_ref[ki + 1], p[T:])

        first = ks_ref[h, 2 * g + 1]
        early = jnp.minimum(ks_ref[h, 2 * g], first)

        def narrow(i, carry):
            ki = early + 2 * i
            st = _dot_nt(k_ref[pl.ds(pl.multiple_of(ki * T, T), 2 * T), :], q[:W // 2, :])
            acc_ref[:, :W // 2] += weighted(ki, jnp.exp(st).astype(BF16))
            return carry

        lax.fori_loop(0, (first - early) // 2, narrow, 0)
        steps = (g * G - first) // 2
        sa_ref[...] = scores(first, 0)

        def advance(ki, cur_ref, next_ref):
            p = jnp.exp(cur_ref[...]).astype(BF16)
            next_ref[...] = scores(ki + 2, 0)
            acc_ref[...] += weighted(ki, p)

        def loop(i, carry):
            advance(first + 4 * i, sa_ref, sb_ref)
            advance(first + 4 * i + 2, sb_ref, sa_ref)
            return carry

        lax.fori_loop(0, steps // 2, loop, 0)

        def first_own(pending_ref):
            p = jnp.exp(causal(pending_ref[...])).astype(BF16)
            if G > 2:
                sc_ref[:, :W - 2 * T] = scores(g * G + 2, 2)
            acc_ref[...] += weighted(g * G, p)

        @pl.when(steps % 2 == 1)
        def _():
            advance(g * G - 2, sa_ref, sb_ref)
            first_own(sb_ref)

        @pl.when(steps % 2 == 0)
        def _():
            first_own(sa_ref)

        if G > 2:
            acc_ref[:, 2 * T:] += weighted(g * G + 2, jnp.exp(causal(sc_ref[:, :W - 2 * T])).astype(BF16))
        for j in range(4, G, 2):
            p = jnp.exp(causal(scores(g * G + j, j))).astype(BF16)
            acc_ref[:, j * T:] += weighted(g * G + j, p)
        finish(acc_ref[...], 0.0, o_ref, lse_ref)

    def online_body(ks_ref, q_ref, k_ref, vt_ref, o_ref, lse_ref, acc_ref, m_ref):
        h, g = pl.program_id(0), pl.program_id(1)
        q = q_ref[...]
        m_ref[...] = jnp.full((8, W), NEG, F32)
        acc_ref[...] = jnp.zeros((LANES, W), F32)

        def update(st, vtb, lo):
            m_old = m_ref[0:1, lo:]
            m_new = jnp.maximum(m_old, jnp.max(st, axis=0, keepdims=True))
            p = jnp.exp(st - m_new).astype(BF16)
            acc_ref[:, lo:] = acc_ref[:, lo:] * jnp.exp(m_old - m_new) + _dot(vtb, p)
            m_ref[:, lo:] = jnp.broadcast_to(m_new, (8, W - lo))

        def loop(ki, carry):
            kb = k_ref[pl.ds(pl.multiple_of(ki * T, T), T), :]
            update(_dot_nt(kb, q), vt_ref[ki], 0)
            return carry

        lax.fori_loop(jnp.minimum(ks_ref[h, 2 * g], ks_ref[h, 2 * g + 1]), g * G, loop, 0)
        for j in range(G):
            ki = g * G + j
            kb = k_ref[pl.ds(pl.multiple_of(ki * T, T), T), :]
            update(causal(_dot_nt(kb, q[j * T:, :])), vt_ref[ki], j * T)
        finish(acc_ref[...], m_ref[0:1, :], o_ref, lse_ref)

    return pl.pallas_call(
        online_body if online_max else fast_body, name="attn_fwd_online" if online_max else "attn_fwd",
        grid_spec=pltpu.PrefetchScalarGridSpec(
            num_scalar_prefetch=1, grid=(H, nb // G),
            in_specs=[pl.BlockSpec((W, LANES), lambda h, i, ks: (i, h)),
                      pl.BlockSpec((S, LANES), lambda h, i, ks: (0, h)),
                      pl.BlockSpec((None, nb, LANES, T), lambda h, i, ks: (h, 0, 0, 0))],
            out_specs=[pl.BlockSpec((W, LANES), lambda h, i, ks: (i, h)),
                       pl.BlockSpec((None, 1, W), lambda h, i, ks: (h, 0, i))],
            scratch_shapes=[pltpu.VMEM((LANES, W), F32)] + (
                [pltpu.VMEM((8, W), F32)] if online_max else [pltpu.VMEM((2 * T, W), F32)] * 3)),
        out_shape=[jax.ShapeDtypeStruct((S, H * LANES), F32), jax.ShapeDtypeStruct((H, 1, S), F32)],
        compiler_params=_params(2),
    )(kstart, qa, ka, vt)


def _attn_out(o_aug, lse, rel, z, x1, target, w_out, qa):
    S, D = x1.shape
    H = D // HEAD_DIM
    tm = min(ROW_TILE, S)

    def body(o_ref, z_ref, x1_ref, t_ref, w_ref, q_ref, first_ref, rel_ref, lse_ref,
             dx2_ref, dx2b_ref, o2b_ref, dz_ref, doa_ref, qa2_ref, loss_ref, oc_s, do_s):
        @pl.when(pl.program_id(0) == 0)
        def _():
            loss_ref[...] = jnp.zeros((1, LANES), F32)

        lane = _lane()
        for j in range(H // 2):
            oc_s[:, LANES * j:LANES * (j + 1)] = _pair_tile(o_ref[:, 2 * LANES * j:2 * LANES * j + LANES],
                                                            o_ref[:, 2 * LANES * j + LANES:2 * LANES * (j + 1)], lane)
        oc = oc_s[...]
        zv = z_ref[...]
        sg = _sigmoid(zv)
        sz = zv * sg
        o2 = (oc * sz).astype(BF16)
        o2b_ref[...] = o2
        e = x1_ref[...] + _dot(o2, w_ref[...]) - t_ref[...]
        sq = jnp.sum(jnp.sum(e * e, axis=1, keepdims=True), axis=0, keepdims=True)
        loss_ref[...] += jnp.broadcast_to(sq * (0.5 / D), (1, LANES))
        dx2 = e * (1.0 / D)
        dx2_ref[...] = dx2
        dx2b = dx2.astype(BF16)
        dx2b_ref[...] = dx2b
        do2 = _dot_nt(dx2b, w_ref[...])
        dz_ref[...] = (do2 * oc * (sg * (1.0 + zv * (1.0 - sg)))).astype(BF16)
        do_s[...] = do2 * sz
        deltas = jnp.zeros((tm, LANES), F32)
        for hd in range(H):
            dt = _head_tile(do_s, hd, lane)
            delta = jnp.sum(dt * _head_tile(oc_s, hd, lane), axis=1, keepdims=True)
            deltas = deltas + jnp.where(lane == hd, delta, 0.0)
            doa_ref[:, LANES * hd:LANES * (hd + 1)] = dt.astype(BF16)
        doa_ref[...] += _dot(_pack3(-deltas, lane, 0.0), first_ref[...]).astype(BF16)
        lse = jnp.concatenate([lse_ref[...], jnp.zeros((LANES - H, tm), F32)], axis=0).T
        rq = rel_ref[...] - lse
        tile_lane = lax.broadcasted_iota(jnp.int32, (1, H * LANES), 1)
        extra = tile_lane % LANES - HEAD_DIM * (1 - (tile_lane // LANES) % 2)
        kept = jnp.where((extra >= 0) & (extra < 3), jnp.zeros((), BF16), q_ref[...])
        qa2_ref[...] = kept + _dot(_pack3(rq, lane, 0.0), first_ref[...]).astype(BF16)

    return pl.pallas_call(
        body, name="attn_out", grid=(S // tm,),
        in_specs=[_rows(tm, H * LANES), _rows(tm, D), _rows(tm, D), _rows(tm, D), _whole((D, D)),
                  _rows(tm, H * LANES), _whole((LANES, H * LANES)), _rows(tm, LANES),
                  pl.BlockSpec((H, tm), lambda i: (0, i))],
        out_specs=[_rows(tm, D), _rows(tm, D), _rows(tm, D), _rows(tm, D), _rows(tm, H * LANES),
                   _rows(tm, H * LANES), _whole((1, LANES))],
        out_shape=[jax.ShapeDtypeStruct((S, D), F32), jax.ShapeDtypeStruct((S, D), BF16),
                   jax.ShapeDtypeStruct((S, D), BF16), jax.ShapeDtypeStruct((S, D), BF16),
                   jax.ShapeDtypeStruct((S, H * LANES), BF16), jax.ShapeDtypeStruct((S, H * LANES), BF16),
                   jax.ShapeDtypeStruct((1, LANES), F32)],
        scratch_shapes=[pltpu.VMEM((tm, D), F32), pltpu.VMEM((tm, D), F32)],
        compiler_params=_params(1),
    )(o_aug, z, x1, target, w_out, qa, _scatter_matrices(H)[0], rel, lse)


def _attn_bwd(qend, qa2, doa, ka, va, T):
    S = qa2.shape[0]
    H = qa2.shape[1] // LANES
    nb = S // T
    G = 2 * nb // qend.shape[1]
    W = G * T

    def body(qe_ref, q_ref, do_ref, k_ref, v_ref, dq_ref, dk_ref, dv_ref, dkt_acc, dvt_acc):
        h, g = pl.program_id(0), pl.program_id(1)

        @pl.when(g == 0)
        def _():
            dq_ref[...] = jnp.zeros((S, LANES), F32)

        kb = k_ref[...]
        vb = v_ref[...]
        dkt_acc[...] = jnp.zeros((LANES, W), F32)
        dvt_acc[...] = jnp.zeros((LANES, W), F32)

        def step(qi, c0, c1, masked):
            rows = pl.ds(pl.multiple_of(qi * T, T), 2 * T)
            qb = q_ref[rows, :]
            dob = do_ref[rows, :]
            s = _dot_nt(qb, kb[c0:c1])
            if masked:
                query = lax.broadcasted_iota(jnp.int32, s.shape, 0) + (c1 - 2 * T)
                s = jnp.where(lax.broadcasted_iota(jnp.int32, s.shape, 1) <= query, s, NEG)
            p = jnp.exp(s)
            ds = (p * _dot_nt(dob, vb[c0:c1])).astype(BF16)
            dvt_acc[:, c0:c1] += _dot(dob.astype(F32).T.astype(BF16), p.astype(BF16))
            dkt_acc[:, c0:c1] += _dot(qb.astype(F32).T.astype(BF16), ds)
            dq_ref[rows, :] += _dot(ds, kb[c0:c1])

        for m in range(G // 2):
            step(g * G + 2 * m, 0, (m + 1) * 2 * T, True)
        first = g * G + G
        n_all = jnp.maximum((qe_ref[h, 2 * g] - first + 2) // 2, 0)
        second = first + 2 * n_all

        def all_keys(i, carry):
            step(first + 2 * i, 0, W, False)
            return carry

        def late_keys(i, carry):
            step(second + 2 * i, W // 2, W, False)
            return carry

        lax.fori_loop(0, n_all, all_keys, 0)
        lax.fori_loop(0, (qe_ref[h, 2 * g + 1] - second + 2) // 2, late_keys, 0)
        dk_ref[...] = dkt_acc[...].T
        dv_ref[...] = dvt_acc[...].T.astype(BF16)

    heads = pl.BlockSpec((None, W, LANES), lambda h, i, qe: (h, i, 0))
    return pl.pallas_call(
        body, name="attn_bwd",
        grid_spec=pltpu.PrefetchScalarGridSpec(
            num_scalar_prefetch=1, grid=(H, nb // G),
            in_specs=[pl.BlockSpec((S, LANES), lambda h, i, qe: (0, h)), pl.BlockSpec((S, LANES), lambda h, i, qe: (0, h)),
                      pl.BlockSpec((W, LANES), lambda h, i, qe: (i, h)), pl.BlockSpec((W, LANES), lambda h, i, qe: (i, h))],
            out_specs=[pl.BlockSpec((None, S, LANES), lambda h, i, qe: (h, 0, 0)), heads, heads],
            scratch_shapes=[pltpu.VMEM((LANES, W), F32), pltpu.VMEM((LANES, W), F32)]),
        out_shape=[jax.ShapeDtypeStruct((H, S, LANES), F32), jax.ShapeDtypeStruct((H, S, LANES), F32),
                   jax.ShapeDtypeStruct((H, S, LANES), BF16)],
        compiler_params=_params(2),
    )(qend, qa2, doa, ka, va)


def _attn_proj_bwd(dqt, dka, dva, qraw, kraw, dz, f, gq, gk, w, wf, x1, g2, dx2):
    S, D = x1.shape
    H = D // HEAD_DIM
    tm = min(ROW_TILE, S)
    last = S // tm - 1
    tri = (lax.broadcasted_iota(jnp.int32, (tm, tm), 1) >= lax.broadcasted_iota(jnp.int32, (tm, tm), 0)).astype(BF16)

    def body(dq_ref, dk_ref, dv_ref, q_ref, k_ref, dz_ref, f_ref, gq_ref, gk_ref, w_ref, wf_ref, x1_ref, g2_ref,
             dx2_ref, tri_ref, dproj_ref, dx1_ref, dg2_ref, small_ref, carry, pairs):
        @pl.when(pl.program_id(0) == 0)
        def _():
            dg2_ref[...] = jnp.zeros((1, D), F32)
            small_ref[...] = jnp.zeros((8, LANES), F32)
            carry[...] = jnp.zeros((8, LANES), F32)

        lane = _lane()

        def head_pair(j, acc):
            dcs, dgq, dgk = acc
            dq2, dk2 = [], []
            q_pair, k_pair = q_ref[j], k_ref[j]
            for parity in (0, 1):
                hd = 2 * j + parity
                own, a = _own(lane, parity), _aug(parity)
                dqf = dq_ref[hd]
                dqn = jnp.where(own, dqf * Q_SCALE, 0.0)
                d, dg = _head_rms_bwd(dqn, jnp.where(own, q_pair, 0.0), gq_ref[...])
                dq2.append(d)
                dgq = dgq + dg
                dkt = dk_ref[hd]
                dcs = dcs + jnp.where(lane == hd, _col(dqf, lane, a) - _col(dkt, lane, a + 3), 0.0)
                d, dg = _head_rms_bwd(jnp.where(own, dkt, 0.0), jnp.where(own, k_pair, 0.0), gk_ref[...])
                dk2.append(d)
                dgk = dgk + dg
            pairs[0, j] = _pair_tile(*dq2, lane).astype(BF16)
            pairs[1, j] = _pair_tile(*dk2, lane).astype(BF16)
            pairs[2, j] = _pair_tile(dv_ref[2 * j], dv_ref[2 * j + 1], lane)
            return dcs, dgq, dgk

        zero = jnp.zeros((1, LANES), F32)
        dcs, dgq, dgk = lax.fori_loop(0, H // 2, head_pair, (jnp.zeros((tm, LANES), F32), zero, zero))
        for part in range(3):
            for j in range(H // 2):
                dproj_ref[:, part * D + LANES * j:part * D + LANES * (j + 1)] = pairs[part, j]
        dproj_ref[:, 3 * D:4 * D] = dz_ref[...]
        dlogf = _dot01(tri_ref[...], dcs) + carry[0:1, :]
        carry[...] = jnp.broadcast_to(dlogf[0:1, :], (8, LANES))
        df = dlogf * (1.0 / (1.0 + jnp.exp(f_ref[...])))
        dproj_ref[:, 4 * D:4 * D + LANES] = df.astype(BF16)
        small_ref[0:1, :] += jnp.sum(df, axis=0, keepdims=True)
        small_ref[1:2, :] += dgq
        small_ref[2:3, :] += dgk
        dh = _dot_nt(dproj_ref[:, 0:4 * D], w_ref[...]) + _dot_nt(dproj_ref[:, 4 * D:4 * D + LANES], wf_ref[...])
        dxn, dg = _rms_bwd(dh, x1_ref[...], g2_ref[...])
        dx1_ref[...] = dx2_ref[...] + dxn
        dg2_ref[...] += dg

    W = 4 * D + LANES
    heads = pl.BlockSpec((H, tm, LANES), lambda i: (0, last - i, 0))
    head_pairs = pl.BlockSpec((H // 2, tm, LANES), lambda i: (0, last - i, 0))
    return pl.pallas_call(
        body, name="attn_proj_bwd", grid=(S // tm,),
        in_specs=[heads, heads, heads, head_pairs, head_pairs,
                  _rows(tm, D, last), _rows(tm, LANES, last), _whole((1, LANES)), _whole((1, LANES)),
                  _whole((D, 4 * D)), _whole((D, LANES)), _rows(tm, D, last), _whole((1, D)), _rows(tm, D, last),
                  _whole((tm, tm))],
        out_specs=[_rows(tm, W, last), _rows(tm, D, last), _whole((1, D)), _whole((8, LANES))],
        out_shape=[jax.ShapeDtypeStruct((S, W), BF16), jax.ShapeDtypeStruct((S, D), F32),
                   jax.ShapeDtypeStruct((1, D), F32), jax.ShapeDtypeStruct((8, LANES), F32)],
        scratch_shapes=[pltpu.VMEM((8, LANES), F32), pltpu.VMEM((3, H // 2, tm, LANES), BF16)],
        compiler_params=_params(1),
    )(dqt, dka, dva, qraw, kraw, dz, f, gq, gk, w, wf, x1, g2, dx2, tri)


def _matmul_tn(a, b, col0, n, tn, name, stacked=False):
    S, M = a.shape
    ts = min(TN_ROWS, S)
    off = col0 // tn

    def body(a_ref, b_ref, o_ref):
        @pl.when(pl.program_id(1) == 0)
        def _():
            o_ref[...] = jnp.zeros((M, tn), F32)

        o_ref[...] += _dot_tn(a_ref[...], b_ref[...])

    if stacked:
        out_spec, out_shape = pl.BlockSpec((None, M, tn), lambda j, s: (j, 0, 0)), (n // tn, M, tn)
    else:
        out_spec, out_shape = pl.BlockSpec((M, tn), lambda j, s: (0, j)), (M, n)
    return pl.pallas_call(
        body, name=name, grid=(n // tn, S // ts),
        in_specs=[pl.BlockSpec((ts, M), lambda j, s: (s, 0)), pl.BlockSpec((ts, tn), lambda j, s: (s, off + j))],
        out_specs=out_spec, out_shape=jax.ShapeDtypeStruct(out_shape, F32),
        compiler_params=_params(2),
    )(a, b)


def _adamw(w, g, m, v, name):
    r, c = w.shape
    tr = ROW_TILE if r % ROW_TILE == 0 else r

    def body(w_ref, g_ref, m_ref, v_ref, d_ref, m2_ref, v2_ref):
        gv = g_ref[...]
        m2 = ADAM_B1 * m_ref[...] + (1.0 - ADAM_B1) * gv
        v2 = ADAM_B2 * v_ref[...] + (1.0 - ADAM_B2) * (gv * gv)
        m2_ref[...] = m2
        v2_ref[...] = v2
        m_hat = m2 / (1.0 - ADAM_B1 ** ADAM_STEP)
        v_hat = v2 / (1.0 - ADAM_B2 ** ADAM_STEP)
        d_ref[...] = -ADAM_LR * (m_hat / (jnp.sqrt(v_hat) + ADAM_EPS) + ADAM_WD * w_ref[...])

    spec = _rows(tr, c)
    return pl.pallas_call(
        body, name=name, grid=(r // tr,), in_specs=[spec] * 4, out_specs=[spec] * 3,
        out_shape=[jax.ShapeDtypeStruct((r, c), F32)] * 3, compiler_params=_params(1),
    )(w, g, m, v)


def _local_step(x, target, g1, w_in, conv_w, w_out, g2, wa_in, b_f, gq, gk, wa_out):
    S, D = x.shape
    H = D // HEAD_DIM
    w_qkvz = wa_in[:, :4 * D]
    wf = jnp.pad(wa_in[:, 4 * D:], ((0, 0), (0, LANES - H)))
    bf = jnp.pad(b_f, ((0, 0), (0, LANES - H)))
    gq128 = jnp.concatenate([gq, gq], axis=1)
    gk128 = jnp.concatenate([gk, gk], axis=1)

    proj, h1, yc, y, x1 = _conv_fwd(x, g1, w_in, conv_w, w_out)
    h2, qraw, kraw, z, f, c, rel, qa, ka, va, vt = _attn_front(x1, g2, w_qkvz, wf, bf, gq128, gk128)
    T = vt.shape[3]
    kstart, qend, bound = _skip_tables(c[:, :H], (c - rel)[:, :H], gq, gk, T, min(ATT_GROUP, S // T))
    o_aug, lse = lax.cond(2.0 * bound <= PLAIN_EXP_MAX, functools.partial(_attn_fwd, online_max=False),
                          functools.partial(_attn_fwd, online_max=True), kstart, qa, ka, vt)
    dx2, dx2b, o2b, dz, doa, qa2, loss = _attn_out(o_aug, lse.reshape(H, S), rel, z, x1, target, wa_out, qa)
    dqt, dka, dva = _attn_bwd(qend, qa2, doa, ka, va, T)
    dproj2, dx1, dg2, small = _attn_proj_bwd(dqt, dka, dva, qraw, kraw, dz, f, gq128, gk128, w_qkvz, wf, x1, g2, dx2)
    dproj1, dx, dx1b, dg1, dcw = _conv_bwd(dx1, x, g1, w_in, w_out, conv_w, proj, yc)

    tn = min(1024, D)
    dwa_out = _matmul_tn(o2b, dx2b, 0, D, tn, "dw_attn_out")
    dwa_in = jnp.concatenate([_matmul_tn(h2, dproj2, 0, 4 * D, tn, "dw_attn_in"),
                              _matmul_tn(h2, dproj2, 4 * D, LANES, LANES, "dw_attn_f")[:, :H]], axis=1)
    dw_out = _matmul_tn(y, dx1b, 0, D, tn, "dw_conv_out")
    dw_in = _matmul_tn(h1, dproj1, 0, 4 * D, D, "dw_conv_in", stacked=True)
    grads = dict(conv_norm_g=dg1, conv_w_in=dw_in, conv_w=dcw, conv_w_out=dw_out, attn_norm_g=dg2,
                 attn_w_in=dwa_in, attn_b_f=small[0:1, :H],
                 attn_q_norm_g=small[1:2, :HEAD_DIM] + small[1:2, HEAD_DIM:],
                 attn_k_norm_g=small[2:3, :HEAD_DIM] + small[2:3, HEAD_DIM:], attn_w_out=dwa_out)
    return loss[0, 0], dx, grads


def _coords():
    return lax.axis_index("x"), lax.axis_index("y"), lax.axis_index("c")


def _at(ref, idx):
    return ref.at[idx] if idx else ref


def _other_chips(x, y):
    return [(1 - x, y), (x, 1 - y), (1 - x, 1 - y)]


def _all_gather(halved, whole):
    nh, nw = len(halved), len(whole)

    def body(*refs):
        src_h, src_w = refs[:nh], refs[nh:nh + nw]
        out_h, out_w = refs[nh + nw:2 * nh + nw], refs[2 * nh + nw:2 * (nh + nw)]
        send_h, recv_h, send_w, recv_w = refs[2 * (nh + nw):]
        x, y, c = _coords()
        mine = 2 * x + y
        sibling = (x, y, 1 - c)
        chips = _other_chips(x, y)

        def copy_h(a, k, chip, half, to, src=None):
            dst = out_h[a].at[chip, half]
            return pltpu.make_async_remote_copy(src_ref=dst if src is None else src, dst_ref=dst,
                                                send_sem=send_h.at[a, k], recv_sem=recv_h.at[a, k],
                                                device_id=to, device_id_type=MESH)

        def copy_w(a, k, chip, to):
            return pltpu.make_async_remote_copy(src_ref=src_w[a], dst_ref=out_w[a].at[chip],
                                                send_sem=send_w.at[a, k], recv_sem=recv_w.at[a, k],
                                                device_id=to, device_id_type=MESH)

        first = [copy_h(a, k, mine, c, (*chip, c), src=src_h[a].at[c]) for a in range(nh) for k, chip in enumerate(chips)]
        first += [copy_w(a, k, mine, (*chip, c)) for a in range(nw) for k, chip in enumerate(chips)]
        for cp in first:
            cp.start()
        passed = []
        for a in range(nh):
            for k, (px, py) in enumerate(chips):
                copy_h(a, k, 2 * px + py, c, (x, y, c)).wait_recv()
                cp = copy_h(a, 3 + k, 2 * px + py, c, sibling)
                cp.start()
                passed.append(cp)
        for a in range(nh):
            for k, (px, py) in enumerate(chips):
                copy_h(a, 3 + k, 2 * px + py, 1 - c, (x, y, c)).wait_recv()
        for a in range(nw):
            for k, (px, py) in enumerate(chips):
                copy_w(a, k, 2 * px + py, (x, y, c)).wait_recv()
        for cp in first + passed:
            cp.wait_send()

    out_shape = [jax.ShapeDtypeStruct((4,) + a.shape, a.dtype) for a in list(halved) + list(whole)]
    gathered = pl.pallas_call(
        body, name="gather_weights", in_specs=[ANY] * (nh + nw), out_specs=[ANY] * (nh + nw), out_shape=out_shape,
        scratch_shapes=[pltpu.SemaphoreType.DMA((nh, 6)), pltpu.SemaphoreType.DMA((nh, 6)),
                        pltpu.SemaphoreType.DMA((nw, 3)), pltpu.SemaphoreType.DMA((nw, 3))],
    )(*halved, *whole)
    x, y, _ = _coords()
    return [lax.dynamic_update_index_in_dim(g, a, 2 * x + y, axis=0) for g, a in zip(gathered, list(halved) + list(whole))]


def _exchange(name, srcs, lands, copies, local_copies):
    ns, nl, n, nloc = len(srcs), len(lands), len(copies), len(local_copies)

    def body(*refs):
        src, land = refs[:ns], refs[ns:ns + nl]
        send, recv, local_sem = refs[ns + nl:]
        me = _coords()
        started = []
        for k, (si, s_at, li, l_at, ci) in enumerate(local_copies):
            cp = pltpu.make_async_copy(_at(src[si], s_at(*me)), _at(land[li], l_at(*me)), local_sem.at[k])
            cp.start()
            started.append(cp)
        remote = []
        for k, (si, s_at, li, l_at, peer) in enumerate(copies):
            cp = pltpu.make_async_remote_copy(src_ref=_at(src[si], s_at(*me)), dst_ref=_at(land[li], l_at(*me)),
                                              send_sem=send.at[k], recv_sem=recv.at[k],
                                              device_id=peer(*me), device_id_type=MESH)
            cp.start()
            remote.append(cp)
        for cp in remote:
            cp.wait()
        for cp in started:
            cp.wait()

    return pl.pallas_call(
        body, name=name, in_specs=[ANY] * ns, out_specs=[ANY] * nl, out_shape=list(lands),
        scratch_shapes=[pltpu.SemaphoreType.DMA((n,)), pltpu.SemaphoreType.DMA((n,)),
                        pltpu.SemaphoreType.DMA((max(nloc, 1),))],
    )(*srcs)


def _add_pairs(a, b, name):
    _, r, cols = a.shape
    tr = ROW_TILE if r % ROW_TILE == 0 else r

    def body(a_ref, b_ref, o_ref, ob_ref):
        s = a_ref[...] + b_ref[...]
        o_ref[...] = s
        ob_ref[...] = s.astype(BF16)

    spec = pl.BlockSpec((None, tr, cols), lambda j, i: (j, i, 0))
    return pl.pallas_call(
        body, name=name, grid=(4, r // tr), in_specs=[spec, spec], out_specs=[spec, spec],
        out_shape=[jax.ShapeDtypeStruct(a.shape, F32), jax.ShapeDtypeStruct(a.shape, BF16)],
        compiler_params=_params(2),
    )(a, b)


def _sum_chips(own, landed, name):
    _, r, cols = landed.shape
    tr = ROW_TILE if r % ROW_TILE == 0 else r

    def body(own_ref, land_ref, o_ref):
        acc = own_ref[...]
        for j in range(3):
            acc = acc + land_ref[j].astype(F32)
        o_ref[...] = acc

    return pl.pallas_call(
        body, name=name, grid=(r // tr,),
        in_specs=[_rows(tr, cols), pl.BlockSpec((3, tr, cols), lambda i: (0, i, 0))], out_specs=_rows(tr, cols),
        out_shape=jax.ShapeDtypeStruct((r, cols), F32), compiler_params=_params(1),
    )(own, landed)


def _sum_devices(landed, name):
    def body(l_ref, o_ref):
        acc = l_ref[0]
        for j in range(1, 8):
            acc = acc + l_ref[j]
        o_ref[...] = acc

    return pl.pallas_call(body, name=name, out_shape=jax.ShapeDtypeStruct(landed.shape[1:], F32))(landed)


def _reduce_gradients(big, small):
    nb = len(big)
    x, y, c = _coords()
    mine = 2 * x + y
    flips = [(fx, fy, fc) for fx in (0, 1) for fy in (0, 1) for fc in (0, 1) if fx or fy or fc]

    def flip(fx, fy, fc):
        return lambda x, y, c: (x ^ fx, y ^ fy, c ^ fc)

    copies = [(a, (lambda j: lambda x, y, c: (j, 1 - c))(j), a, (lambda j: lambda x, y, c: (j,))(j), flip(0, 0, 1))
              for a in range(nb) for j in range(4)]
    copies += [(nb, lambda x, y, c: (), nb, lambda x, y, c: (4 * x + 2 * y + c,), flip(*f)) for f in flips]
    lands = [jax.ShapeDtypeStruct((4,) + g.shape[2:], F32) for g in big] + [jax.ShapeDtypeStruct((8,) + small.shape, F32)]
    local = [(nb, lambda x, y, c: (), nb, lambda x, y, c: (4 * x + 2 * y + c,), None)]
    landed = _exchange("swap_halves", list(big) + [small], lands, copies, local)
    small_sum = _sum_devices(landed[nb], "sum_small")

    chip_f32, chip_bf16 = [], []
    for a in range(nb):
        kept = lax.dynamic_index_in_dim(big[a], c, axis=1, keepdims=False)
        s, sb = _add_pairs(kept, landed[a], f"add_cores_{a}")
        chip_f32.append(s)
        chip_bf16.append(sb)

    chip_flips = [(1, 0), (0, 1), (1, 1)]
    copies = [(a, (lambda f: lambda x, y, c: (2 * (x ^ f[0]) + (y ^ f[1]),))(f), a, (lambda k: lambda x, y, c: (k,))(k),
               flip(f[0], f[1], 0)) for a in range(nb) for k, f in enumerate(chip_flips)]
    lands = [jax.ShapeDtypeStruct((3,) + g.shape[1:], BF16) for g in chip_bf16]
    landed = _exchange("send_chip_sums", chip_bf16, lands, copies, [])
    totals = [_sum_chips(lax.dynamic_index_in_dim(chip_f32[a], mine, axis=0, keepdims=False), landed[a],
                         f"sum_chips_{a}") for a in range(nb)]

    copies = [(a, lambda x, y, c: (), a, lambda x, y, c: (), flip(0, 0, 1)) for a in range(nb)]
    lands = [jax.ShapeDtypeStruct(t.shape, F32) for t in totals]
    landed = _exchange("swap_sums", totals, lands, copies, [])
    return [jnp.stack([jnp.where(c == 0, t, l), jnp.where(c == 0, l, t)]) for t, l in zip(totals, landed)], small_sum


def kernel(x, conv_norm_g, conv_w_in, conv_w, conv_w_out, attn_norm_g, attn_w_in, attn_b_f, attn_q_norm_g, attn_k_norm_g, attn_w_out, loss_target, m_conv_norm_g, m_conv_w_in, m_conv_w, m_conv_w_out, m_attn_norm_g, m_attn_w_in, m_attn_b_f, m_attn_q_norm_g, m_attn_k_norm_g, m_attn_w_out, v_conv_norm_g, v_conv_w_in, v_conv_w, v_conv_w_out, v_attn_norm_g, v_attn_w_in, v_attn_b_f, v_attn_q_norm_g, v_attn_k_norm_g, v_attn_w_out):
    xi, yi, _ = _coords()
    chip = 2 * xi + yi
    D = x.shape[2]
    H = D // HEAD_DIM
    names = ["conv_norm_g", "conv_w_in", "conv_w", "conv_w_out", "attn_norm_g", "attn_w_in", "attn_b_f",
             "attn_q_norm_g", "attn_k_norm_g", "attn_w_out"]
    weights = dict(zip(names, [conv_norm_g, conv_w_in, conv_w, conv_w_out, attn_norm_g, attn_w_in, attn_b_f,
                               attn_q_norm_g, attn_k_norm_g, attn_w_out]))
    m_in = dict(zip(names, [m_conv_norm_g, m_conv_w_in, m_conv_w, m_conv_w_out, m_attn_norm_g, m_attn_w_in,
                            m_attn_b_f, m_attn_q_norm_g, m_attn_k_norm_g, m_attn_w_out]))
    v_in = dict(zip(names, [v_conv_norm_g, v_conv_w_in, v_conv_w, v_conv_w_out, v_attn_norm_g, v_attn_w_in,
                            v_attn_b_f, v_attn_q_norm_g, v_attn_k_norm_g, v_attn_w_out]))
    weights = {k: w[0] for k, w in weights.items()}
    m_in = {k: w[0] for k, w in m_in.items()}
    v_in = {k: w[0] for k, w in v_in.items()}

    big_names = ["conv_w_in", "attn_w_in", "conv_w_out", "attn_w_out"]
    halved = [weights[k].astype(BF16).reshape(2, weights[k].shape[0] // 2, weights[k].shape[1]) for k in big_names]
    q = D // 4
    small_w = jnp.concatenate([weights["conv_w"], weights["attn_norm_g"][None, :], jnp.zeros((4, q), F32)], axis=0)
    g_in, ga_in, g_out, ga_out, g_small = _all_gather(halved, [small_w])
    w_in = g_in.reshape(4, D, D)
    wa_in = ga_in.reshape(4, D, D + H // 4).transpose(1, 0, 2).reshape(D, 4 * D + H)
    w_out = g_out.reshape(D, D)
    wa_out = ga_out.reshape(D, D)
    conv_w_full = g_small[:, 0:3, :].transpose(1, 0, 2).reshape(3, D)
    attn_g_full = g_small[:, 3, :].reshape(1, D)

    loss_part, grad_x, grads = _local_step(x[0], loss_target[0], weights["conv_norm_g"][None, :], w_in, conv_w_full,
                                           w_out, attn_g_full, wa_in, weights["attn_b_f"][None, :],
                                           weights["attn_q_norm_g"][None, :], weights["attn_k_norm_g"][None, :], wa_out)

    big = [grads["conv_w_in"].reshape(4, 2, D // 2, D),
           grads["attn_w_in"].reshape(D, 4, D + H // 4).transpose(1, 0, 2).reshape(4, 2, D // 2, D + H // 4),
           grads["conv_w_out"].reshape(4, 2, D // 8, D), grads["attn_w_out"].reshape(4, 2, D // 8, D)]
    tail = jnp.concatenate([grads["attn_b_f"], grads["attn_q_norm_g"], grads["attn_k_norm_g"],
                            jnp.reshape(loss_part, (1, 1)), jnp.zeros((1, D - H - 2 * HEAD_DIM - 1), F32)], axis=1)
    small = jnp.concatenate([grads["conv_norm_g"], grads["conv_w"], grads["attn_norm_g"], tail,
                             jnp.zeros((2, D), F32)], axis=0)
    reduced, small_sum = _reduce_gradients(big, small)
    final = {k: r.reshape(weights[k].shape) for k, r in zip(big_names, reduced)}
    final["conv_norm_g"] = small_sum[0]
    final["conv_w"] = lax.dynamic_slice_in_dim(small_sum[1:4], chip * q, q, axis=1)
    final["attn_norm_g"] = lax.dynamic_slice_in_dim(small_sum[4], chip * q, q, axis=0)
    final["attn_b_f"] = small_sum[5, :H]
    final["attn_q_norm_g"] = small_sum[5, H:H + HEAD_DIM]
    final["attn_k_norm_g"] = small_sum[5, H + HEAD_DIM:H + 2 * HEAD_DIM]
    loss = small_sum[5, H + 2 * HEAD_DIM]

    delta, new_m, new_v = {}, {}, {}
    for k in names:
        shape = weights[k].shape
        as2d = (lambda a: a.reshape(1, -1)) if len(shape) == 1 else (lambda a: a)
        d, m2, v2 = _adamw(as2d(weights[k]), as2d(final[k]), as2d(m_in[k]), as2d(v_in[k]), "adamw_" + k)
        delta[k], new_m[k], new_v[k] = d.reshape(shape), m2.reshape(shape), v2.reshape(shape)
    lead = lambda a: a[None]
    return (loss, grad_x[None], *[lead(final[k]) for k in names], *[lead(delta[k]) for k in names],
            *[lead(new_m[k]) for k in names], *[lead(new_v[k]) for k in names])
```

```python
import functools

import jax
import jax.numpy as jnp
from jax import lax
from jax.experimental import pallas as pl
from jax.experimental.pallas import tpu as pltpu

F32 = jnp.float32
BF16 = jnp.bfloat16
HEAD_DIM = 64
LANES = 128
RMS_EPS = 1e-6
NEG = -1e30
Q_SCALE = 0.125
ROW_TILE = 256
CONV_TILE = 512
ATT_GROUP = 4
SKIP_LOG = 106.0
PLAIN_EXP_MAX = 60.0
TN_ROWS = 2048
VMEM_LIMIT = 56 << 20
ADAM_LR, ADAM_B1, ADAM_B2, ADAM_EPS, ADAM_WD, ADAM_STEP = 0.001, 0.9, 0.999, 1e-08, 0.01, 10
MESH = pl.DeviceIdType.MESH
ANY = pl.BlockSpec(memory_space=pl.ANY)


def _lane():
    return lax.broadcasted_iota(jnp.int32, (1, LANES), 1)


def _split3(x):
    hi = x.astype(BF16).astype(F32)
    r = x - hi
    mid = r.astype(BF16).astype(F32)
    lo = (r - mid).astype(BF16).astype(F32)
    return hi, mid, lo


STAT_STRIDE = 16
ONE_LANE = 3 * STAT_STRIDE


def _pack3(x, lane, one):
    hi, mid, lo = _split3(x)
    packed = hi + pltpu.roll(mid, STAT_STRIDE, 1) + pltpu.roll(lo, 2 * STAT_STRIDE, 1)
    return jnp.where(lane == ONE_LANE, one, packed).astype(BF16)


def _scatter_matrices(H):
    rows = lax.broadcasted_iota(jnp.int32, (LANES, H * LANES), 0)
    cols = lax.broadcasted_iota(jnp.int32, (LANES, H * LANES), 1)
    head, within = cols // LANES, cols % LANES
    extra = within - HEAD_DIM * (1 - head % 2)
    term = (rows < ONE_LANE) & (rows % STAT_STRIDE == head)
    first = ((term & (extra == rows // STAT_STRIDE)) | ((rows == ONE_LANE) & (extra >= 3) & (extra < 6)))
    second = ((term & (extra - 3 == rows // STAT_STRIDE)) | ((rows == ONE_LANE) & (extra >= 0) & (extra < 3)))
    return first.astype(BF16), second.astype(BF16)


def _gather_matrix(H, lo):
    rows = lax.broadcasted_iota(jnp.int32, (H * LANES, LANES), 0)
    cols = lax.broadcasted_iota(jnp.int32, (H * LANES, LANES), 1)
    extra = rows % LANES - HEAD_DIM * (1 - (rows // LANES) % 2)
    return ((rows // LANES == cols) & (extra >= lo) & (extra < lo + 3)).astype(BF16)


def _put(base, lane, start, parts):
    for j, p in enumerate(parts):
        base = jnp.where(lane == start + j, p, base)
    return base


def _col(x, lane, idx):
    return jnp.sum(jnp.where(lane == idx, x, 0.0), axis=1, keepdims=True)


def _feat(parity):
    return HEAD_DIM * parity


def _aug(parity):
    return HEAD_DIM * (1 - parity)


def _own(lane, parity):
    return (lane >= _feat(parity)) & (lane < _feat(parity) + HEAD_DIM)


def _head_tile(ref, hd, lane):
    j = hd // 2
    return jnp.where(_own(lane, hd % 2), ref[:, LANES * j:LANES * (j + 1)], 0.0)


def _pair_tile(even, odd, lane):
    return jnp.where(lane < HEAD_DIM, even, odd)


def _sigmoid(x):
    return 0.5 * jnp.tanh(0.5 * x) + 0.5


def _dot(a, b):
    return jnp.dot(a, b, preferred_element_type=F32)


def _dot_nt(a, b):
    return lax.dot_general(a, b, (((1,), (1,)), ((), ())), preferred_element_type=F32)


def _dot_tn(a, b):
    return lax.dot_general(a, b, (((0,), (0,)), ((), ())), preferred_element_type=F32)


def _dot01(tri, x):
    hi, mid, lo = _split3(x)
    return _dot(tri, hi.astype(BF16)) + _dot(tri, mid.astype(BF16)) + _dot(tri, lo.astype(BF16))


def _rms_bwd(dh, x, g):
    inv = lax.rsqrt(jnp.mean(x * x, axis=-1, keepdims=True) + RMS_EPS)
    xh = x * inv
    dxn = dh * g
    dx = inv * (dxn - xh * jnp.mean(dxn * xh, axis=-1, keepdims=True))
    return dx, jnp.sum(dh * xh, axis=0, keepdims=True)


def _head_rms_bwd(dn, t, g):
    inv = lax.rsqrt(jnp.sum(t * t, axis=1, keepdims=True) * (1.0 / HEAD_DIM) + RMS_EPS)
    th = t * inv
    gd = dn * g
    d = inv * (gd - th * (jnp.sum(gd * th, axis=1, keepdims=True) * (1.0 / HEAD_DIM)))
    return d, jnp.sum(dn * th, axis=0, keepdims=True)


def _params(n_grid):
    return pltpu.CompilerParams(dimension_semantics=("arbitrary",) * n_grid, vmem_limit_bytes=VMEM_LIMIT)


def _rows(tm, cols, rev=None):
    if rev is None:
        return pl.BlockSpec((tm, cols), lambda i: (i, 0))
    return pl.BlockSpec((tm, cols), lambda i: (rev - i, 0))


def _whole(shape, buffers=None):
    mode = {} if buffers is None else dict(pipeline_mode=pl.Buffered(buffers))
    return pl.BlockSpec(shape, lambda *_: (0,) * len(shape), **mode)


def _conv_fwd(x, g1, w_in, conv_w, w_out):
    S, D = x.shape
    tm = min(CONV_TILE, S)
    sub = min(ROW_TILE, tm)

    def body(x_ref, g_ref, win_ref, cw_ref, wout_ref, proj_ref, h_ref, yc_ref, y_ref, x1_ref, prev_u):
        @pl.when(pl.program_id(0) == 0)
        def _():
            prev_u[...] = jnp.zeros((sub, D), F32)

        for r in range(0, tm, sub):
            rows = slice(r, r + sub)
            xv = x_ref[rows, :]
            inv = lax.rsqrt(jnp.mean(xv * xv, axis=-1, keepdims=True) + RMS_EPS)
            h = (xv * inv * g_ref[...]).astype(BF16)
            h_ref[rows, :] = h
            for j in range(4):
                proj_ref[rows, j * D:(j + 1) * D] = _dot(h, win_ref[j])
            u = proj_ref[rows, D:2 * D] * proj_ref[rows, 2 * D:3 * D]
            pu = prev_u[...]
            row = lax.broadcasted_iota(jnp.int32, (sub, 1), 0)
            u1 = jnp.where(row < 1, pltpu.roll(pu, 1, 0), pltpu.roll(u, 1, 0))
            u2 = jnp.where(row < 2, pltpu.roll(pu, 2, 0), pltpu.roll(u, 2, 0))
            prev_u[...] = u
            w = cw_ref[...]
            yc = w[2:3] * u + w[1:2] * u1 + w[0:1] * u2
            yc_ref[rows, :] = yc
            z = proj_ref[rows, 3 * D:4 * D]
            y = (proj_ref[rows, 0:D] * yc * (z * _sigmoid(z))).astype(BF16)
            y_ref[rows, :] = y
            x1_ref[rows, :] = xv + _dot(y, wout_ref[...])

    return pl.pallas_call(
        body, name="conv_fwd", grid=(S // tm,),
        in_specs=[_rows(tm, D), _whole((1, D)), _whole((4, D, D), 1), _whole((3, D)), _whole((D, D), 1)],
        out_specs=[_rows(tm, 4 * D), _rows(tm, D), _rows(tm, D), _rows(tm, D), _rows(tm, D)],
        out_shape=[jax.ShapeDtypeStruct((S, 4 * D), F32), jax.ShapeDtypeStruct((S, D), BF16),
                   jax.ShapeDtypeStruct((S, D), F32), jax.ShapeDtypeStruct((S, D), BF16),
                   jax.ShapeDtypeStruct((S, D), F32)],
        scratch_shapes=[pltpu.VMEM((sub, D), F32)],
        compiler_params=_params(1),
    )(x, g1, w_in, conv_w, w_out)


def _conv_bwd(dx1, x, g1, w_in, w_out, conv_w, proj, yc):
    S, D = x.shape
    tm = min(ROW_TILE, S)
    sub = min(ROW_TILE, tm)
    last = S // tm - 1

    def body(dx1_ref, x_ref, g_ref, win_ref, wout_ref, cw_ref, proj_ref, yc_ref,
             dproj_ref, dx_ref, dx1b_ref, dg_ref, dcw_ref, next_d):
        @pl.when(pl.program_id(0) == 0)
        def _():
            dg_ref[...] = jnp.zeros((1, D), F32)
            dcw_ref[...] = jnp.zeros((3, D), F32)
            next_d[...] = jnp.zeros((sub, D), F32)

        for r in range(tm - sub, -1, -sub):
            rows = slice(r, r + sub)
            dx1v = dx1_ref[rows, :]
            dx1b = dx1v.astype(BF16)
            dx1b_ref[rows, :] = dx1b
            dy = _dot_nt(dx1b, wout_ref[...])
            b = proj_ref[rows, 0:D]
            c = proj_ref[rows, D:2 * D]
            xin = proj_ref[rows, 2 * D:3 * D]
            z = proj_ref[rows, 3 * D:4 * D]
            sg = _sigmoid(z)
            sz = z * sg
            ycv = yc_ref[rows, :]
            d0 = dy * b * sz
            dproj_ref[rows, 0:D] = (dy * ycv * sz).astype(BF16)
            dproj_ref[rows, 3 * D:4 * D] = (dy * b * ycv * (sg * (1.0 + z * (1.0 - sg)))).astype(BF16)
            nd = next_d[...]
            row = lax.broadcasted_iota(jnp.int32, (sub, 1), 0)
            d1 = jnp.where(row >= sub - 1, pltpu.roll(nd, sub - 1, 0), pltpu.roll(d0, sub - 1, 0))
            d2 = jnp.where(row >= sub - 2, pltpu.roll(nd, sub - 2, 0), pltpu.roll(d0, sub - 2, 0))
            next_d[...] = d0
            w = cw_ref[...]
            du = w[2:3] * d0 + w[1:2] * d1 + w[0:1] * d2
            u = c * xin
            dcw_ref[2:3, :] += jnp.sum(d0 * u, axis=0, keepdims=True)
            dcw_ref[1:2, :] += jnp.sum(d1 * u, axis=0, keepdims=True)
            dcw_ref[0:1, :] += jnp.sum(d2 * u, axis=0, keepdims=True)
            dproj_ref[rows, D:2 * D] = (du * xin).astype(BF16)
            dproj_ref[rows, 2 * D:3 * D] = (du * c).astype(BF16)
            dh = _dot_nt(dproj_ref[rows, 0:D], win_ref[0])
            for j in range(1, 4):
                dh = dh + _dot_nt(dproj_ref[rows, j * D:(j + 1) * D], win_ref[j])
            dxn, dg = _rms_bwd(dh, x_ref[rows, :], g_ref[...])
            dx_ref[rows, :] = dx1v + dxn
            dg_ref[...] += dg

    return pl.pallas_call(
        body, name="conv_bwd", grid=(S // tm,),
        in_specs=[_rows(tm, D, last), _rows(tm, D, last), _whole((1, D)), _whole((4, D, D), 1), _whole((D, D), 1),
                  _whole((3, D)), _rows(tm, 4 * D, last), _rows(tm, D, last)],
        out_specs=[_rows(tm, 4 * D, last), _rows(tm, D, last), _rows(tm, D, last), _whole((1, D)), _whole((3, D))],
        out_shape=[jax.ShapeDtypeStruct((S, 4 * D), BF16), jax.ShapeDtypeStruct((S, D), F32),
                   jax.ShapeDtypeStruct((S, D), BF16), jax.ShapeDtypeStruct((1, D), F32),
                   jax.ShapeDtypeStruct((3, D), F32)],
        scratch_shapes=[pltpu.VMEM((sub, D), F32)],
        compiler_params=_params(1),
    )(dx1, x, g1, w_in, w_out, conv_w, proj, yc)


def _attn_front(x1, g2, w, wf, bf, gq, gk):
    S, D = x1.shape
    H = D // HEAD_DIM
    tm = min(ROW_TILE, S)
    tri = (lax.broadcasted_iota(jnp.int32, (tm, tm), 1) <= lax.broadcasted_iota(jnp.int32, (tm, tm), 0)).astype(BF16)

    def body(x_ref, g_ref, w_ref, wf_ref, bf_ref, gq_ref, gk_ref, tri_ref, first_ref, second_ref,
             h_ref, qh_ref, kh_ref, z_ref, f_ref, c_ref, rel_ref, qa_ref, ka_ref, va_ref, vt_ref,
             carry, v_s, qraw_ref, kraw_ref):
        @pl.when(pl.program_id(0) == 0)
        def _():
            carry[...] = jnp.zeros((8, LANES), F32)

        xv = x_ref[...]
        inv = lax.rsqrt(jnp.mean(xv * xv, axis=-1, keepdims=True) + RMS_EPS)
        h = (xv * inv * g_ref[...]).astype(BF16)
        h_ref[...] = h
        qraw_ref[...] = _dot(h, w_ref[:, 0:D])
        kraw_ref[...] = _dot(h, w_ref[:, D:2 * D])
        v_s[...] = _dot(h, w_ref[:, 2 * D:3 * D])
        z_ref[...] = _dot(h, w_ref[:, 3 * D:4 * D])
        lane = _lane()
        f = _dot(h, wf_ref[...]) + bf_ref[...]
        f_ref[...] = f
        logf = jnp.where(lane < H, jnp.minimum(f, 0.0) - jnp.log(1.0 + jnp.exp(-jnp.abs(f))), 0.0)
        cs = _dot01(tri_ref[...], logf) + carry[0:1, :]
        c_ref[...] = cs
        carry[...] = jnp.broadcast_to(cs[tm - 1:tm, :], (8, LANES))
        diags = jnp.zeros((tm, LANES), F32)
        for hd in range(H):
            sl = slice(LANES * hd, LANES * (hd + 1))
            a = _aug(hd % 2)
            if hd % 2 == 0:
                qh_ref[hd // 2] = qraw_ref[:, LANES * (hd // 2):LANES * (hd // 2 + 1)]
                kh_ref[hd // 2] = kraw_ref[:, LANES * (hd // 2):LANES * (hd // 2 + 1)]
            qt = _head_tile(qraw_ref, hd, lane)
            qn = qt * lax.rsqrt(jnp.sum(qt * qt, axis=1, keepdims=True) * (1.0 / HEAD_DIM) + RMS_EPS) * gq_ref[...]
            kt = _head_tile(kraw_ref, hd, lane)
            kn = kt * lax.rsqrt(jnp.sum(kt * kt, axis=1, keepdims=True) * (1.0 / HEAD_DIM) + RMS_EPS) * gk_ref[...]
            diags = diags + jnp.where(lane == hd, jnp.sum(qn * kn, axis=1, keepdims=True) * Q_SCALE, 0.0)
            qa_ref[:, sl] = (qn * Q_SCALE).astype(BF16)
            ka_ref[:, sl] = kn.astype(BF16)
            va = jnp.where((lane >= a) & (lane < a + 3), 1.0, _head_tile(v_s, hd, lane))
            va_ref[:, sl] = va.astype(BF16)
            vt_ref[hd] = va.T.astype(BF16)
        rel = cs - diags
        rel_ref[...] = rel
        qa_ref[...] += _dot(_pack3(rel, lane, 1.0), first_ref[...]).astype(BF16)
        ka_ref[...] += _dot(_pack3(-cs, lane, 1.0), second_ref[...]).astype(BF16)

    nb = S // tm
    heads = pl.BlockSpec((H // 2, tm, LANES), lambda i: (0, i, 0))
    return pl.pallas_call(
        body, name="attn_front", grid=(nb,),
        in_specs=[_rows(tm, D), _whole((1, D)), _whole((D, 4 * D)), _whole((D, LANES)), _whole((1, LANES)),
                  _whole((1, LANES)), _whole((1, LANES)), _whole((tm, tm)), _whole((LANES, H * LANES)),
                  _whole((LANES, H * LANES))],
        out_specs=[_rows(tm, D), heads, heads, _rows(tm, D), _rows(tm, LANES), _rows(tm, LANES), _rows(tm, LANES),
                   _rows(tm, H * LANES), _rows(tm, H * LANES), _rows(tm, H * LANES),
                   pl.BlockSpec((H, None, LANES, tm), lambda i: (0, i, 0, 0))],
        out_shape=[jax.ShapeDtypeStruct((S, D), BF16), jax.ShapeDtypeStruct((H // 2, S, LANES), F32),
                   jax.ShapeDtypeStruct((H // 2, S, LANES), F32), jax.ShapeDtypeStruct((S, D), F32),
                   jax.ShapeDtypeStruct((S, LANES), F32), jax.ShapeDtypeStruct((S, LANES), F32),
                   jax.ShapeDtypeStruct((S, LANES), F32),
                   jax.ShapeDtypeStruct((S, H * LANES), BF16), jax.ShapeDtypeStruct((S, H * LANES), BF16),
                   jax.ShapeDtypeStruct((S, H * LANES), BF16), jax.ShapeDtypeStruct((H, nb, LANES, tm), BF16)],
        scratch_shapes=[pltpu.VMEM((8, LANES), F32), pltpu.VMEM((tm, D), F32), pltpu.VMEM((tm, D), F32),
                        pltpu.VMEM((tm, D), F32)],
        compiler_params=_params(1),
    )(x1, g2, w, wf, bf, gq, gk, tri, *_scatter_matrices(H))


def _skip_tables(c, diag, gq, gk, T, G):
    nb = c.shape[0] // T
    bound = 8.0 * jnp.max(jnp.abs(gq)) * jnp.max(jnp.abs(gk))
    first, last = c[0::T, :], c[T - 1::T, :]
    lowest = jnp.maximum(jnp.min(diag.reshape(nb, T, -1), axis=1), -bound)
    idx = jnp.arange(nb)
    margin = (SKIP_LOG + bound) - lowest
    need = (last[None, :, :] <= first[:, None, :] + margin[:, None, :]) & (idx[None, :, None] < idx[:, None, None])
    need = need | (idx[None, :, None] == idx[:, None, None])
    kstart = jnp.argmax(need, axis=1)
    qend = nb - 1 - jnp.argmax(need[::-1], axis=0)
    kstart = jnp.min(kstart.reshape(2 * nb // G, G // 2, -1), axis=1)
    kstart = kstart - (kstart & 1)
    qend = jnp.max(qend.reshape(2 * nb // G, G // 2, -1), axis=1)
    return kstart.T.astype(jnp.int32), qend.T.astype(jnp.int32), bound


def _attn_fwd(kstart, qa, ka, vt, online_max):
    S = qa.shape[0]
    H = qa.shape[1] // LANES
    nb, T = vt.shape[1], vt.shape[3]
    G = 2 * nb // kstart.shape[1]
    W = G * T

    def finish(acc, shift, o_ref, lse_ref):
        a = _aug(pl.program_id(0) % 2)
        feat = lax.broadcasted_iota(jnp.int32, (LANES, 1), 0)
        l = jnp.sum(jnp.where(feat == a, acc, 0.0), axis=0, keepdims=True)
        o_ref[...] = (acc * (1.0 / l)).T
        lse_ref[...] = shift + jnp.log(l)

    def causal(st):
        return jnp.where(lax.broadcasted_iota(jnp.int32, st.shape, 0) <= lax.broadcasted_iota(jnp.int32, st.shape, 1),
                         st, NEG)

    def fast_body(ks_ref, q_ref, k_ref, vt_ref, o_ref, lse_ref, acc_ref, sa_ref, sb_ref, sc_ref):
        h, g = pl.program_id(0), pl.program_id(1)
        q = q_ref[...]
        acc_ref[...] = jnp.zeros((LANES, W), F32)

        def scores(ki, lo):
            return _dot_nt(k_ref[pl.ds(pl.multiple_of(ki * T, T), 2 * T), :], q[lo * T:, :])

        def weighted(ki, p):
            return _dot(vt_ref[ki], p[:T]) + _dot(vt_ref[ki + 1], p[T:])

        first = ks_ref[h, 2 * g + 1]
        early = jnp.minimum(ks_ref[h, 2 * g], first)

        def narrow(i, carry):
            ki = early + 2 * i
            st = _dot_nt(k_ref[pl.ds(pl.multiple_of(ki * T, T), 2 * T), :], q[:W // 2, :])
            acc_ref[:, :W // 2] += weighted(ki, jnp.exp(st).astype(BF16))
            return carry

        lax.fori_loop(0, (first - early) // 2, narrow, 0)
        steps = (g * G - first) // 2
        sa_ref[...] = scores(first, 0)

        def advance(ki, cur_ref, next_ref):
            p = jnp.exp(cur_ref[...]).astype(BF16)
            next_ref[...] = scores(ki + 2, 0)
            acc_ref[...] += weighted(ki, p)

        def loop(i, carry):
            advance(first + 4 * i, sa_ref, sb_ref)
            advance(first + 4 * i + 2, sb_ref, sa_ref)
            return carry

        lax.fori_loop(0, steps // 2, loop, 0)

        def first_own(pending_ref):
            p = jnp.exp(causal(pending_ref[...])).astype(BF16)
            if G > 2:
                sc_ref[:, :W - 2 * T] = scores(g * G + 2, 2)
            acc_ref[...] += weighted(g * G, p)

        @pl.when(steps % 2 == 1)
        def _():
            advance(g * G - 2, sa_ref, sb_ref)
            first_own(sb_ref)

        @pl.when(steps % 2 == 0)
        def _():
            first_own(sa_ref)

        if G > 2:
            acc_ref[:, 2 * T:] += weighted(g * G + 2, jnp.exp(causal(sc_ref[:, :W - 2 * T])).astype(BF16))
        for j in range(4, G, 2):
            p = jnp.exp(causal(scores(g * G + j, j))).astype(BF16)
            acc_ref[:, j * T:] += weighted(g * G + j, p)
        finish(acc_ref[...], 0.0, o_ref, lse_ref)

    def online_body(ks_ref, q_ref, k_ref, vt_ref, o_ref, lse_ref, acc_ref, m_ref):
        h, g = pl.program_id(0), pl.program_id(1)
        q = q_ref[...]
        m_ref[...] = jnp.full((8, W), NEG, F32)
        acc_ref[...] = jnp.zeros((LANES, W), F32)

        def update(st, vtb, lo):
            m_old = m_ref[0:1, lo:]
            m_new = jnp.maximum(m_old, jnp.max(st, axis=0, keepdims=True))
            p = jnp.exp(st - m_new).astype(BF16)
            acc_ref[:, lo:] = acc_ref[:, lo:] * jnp.exp(m_old - m_new) + _dot(vtb, p)
            m_ref[:, lo:] = jnp.broadcast_to(m_new, (8, W - lo))

        def loop(ki, carry):
            kb = k_ref[pl.ds(pl.multiple_of(ki * T, T), T), :]
            update(_dot_nt(kb, q), vt_ref[ki], 0)
            return carry

        lax.fori_loop(jnp.minimum(ks_ref[h, 2 * g], ks_ref[h, 2 * g + 1]), g * G, loop, 0)
        for j in range(G):
            ki = g * G + j
            kb = k_ref[pl.ds(pl.multiple_of(ki * T, T), T), :]
            update(causal(_dot_nt(kb, q[j * T:, :])), vt_ref[ki], j * T)
        finish(acc_ref[...], m_ref[0:1, :], o_ref, lse_ref)

    return pl.pallas_call(
        online_body if online_max else fast_body, name="attn_fwd_online" if online_max else "attn_fwd",
        grid_spec=pltpu.PrefetchScalarGridSpec(
            num_scalar_prefetch=1, grid=(H, nb // G),
            in_specs=[pl.BlockSpec((W, LANES), lambda h, i, ks: (i, h)),
                      pl.BlockSpec((S, LANES), lambda h, i, ks: (0, h)),
                      pl.BlockSpec((None, nb, LANES, T), lambda h, i, ks: (h, 0, 0, 0))],
            out_specs=[pl.BlockSpec((W, LANES), lambda h, i, ks: (i, h)),
                       pl.BlockSpec((None, 1, W), lambda h, i, ks: (h, 0, i))],
            scratch_shapes=[pltpu.VMEM((LANES, W), F32)] + (
                [pltpu.VMEM((8, W), F32)] if online_max else [pltpu.VMEM((2 * T, W), F32)] * 3)),
        out_shape=[jax.ShapeDtypeStruct((S, H * LANES), F32), jax.ShapeDtypeStruct((H, 1, S), F32)],
        compiler_params=_params(2),
    )(kstart, qa, ka, vt)


def _attn_out(o_aug, lse, rel, z, x1, target, w_out, qa):
    S, D = x1.shape
    H = D // HEAD_DIM
    tm = min(ROW_TILE, S)

    def body(o_ref, z_ref, x1_ref, t_ref, w_ref, q_ref, first_ref, rel_ref, lse_ref,
             dx2_ref, dx2b_ref, o2b_ref, dz_ref, doa_ref, qa2_ref, loss_ref, oc_s, do_s):
        @pl.when(pl.program_id(0) == 0)
        def _():
            loss_ref[...] = jnp.zeros((1, LANES), F32)

        lane = _lane()
        for j in range(H // 2):
            oc_s[:, LANES * j:LANES * (j + 1)] = _pair_tile(o_ref[:, 2 * LANES * j:2 * LANES * j + LANES],
                                                            o_ref[:, 2 * LANES * j + LANES:2 * LANES * (j + 1)], lane)
        oc = oc_s[...]
        zv = z_ref[...]
        sg = _sigmoid(zv)
        sz = zv * sg
        o2 = (oc * sz).astype(BF16)
        o2b_ref[...] = o2
        e = x1_ref[...] + _dot(o2, w_ref[...]) - t_ref[...]
        sq = jnp.sum(jnp.sum(e * e, axis=1, keepdims=True), axis=0, keepdims=True)
        loss_ref[...] += jnp.broadcast_to(sq * (0.5 / D), (1, LANES))
        dx2 = e * (1.0 / D)
        dx2_ref[...] = dx2
        dx2b = dx2.astype(BF16)
        dx2b_ref[...] = dx2b
        do2 = _dot_nt(dx2b, w_ref[...])
        dz_ref[...] = (do2 * oc * (sg * (1.0 + zv * (1.0 - sg)))).astype(BF16)
        do_s[...] = do2 * sz
        deltas = jnp.zeros((tm, LANES), F32)
        for hd in range(H):
            dt = _head_tile(do_s, hd, lane)
            delta = jnp.sum(dt * _head_tile(oc_s, hd, lane), axis=1, keepdims=True)
            deltas = deltas + jnp.where(lane == hd, delta, 0.0)
            doa_ref[:, LANES * hd:LANES * (hd + 1)] = dt.astype(BF16)
        doa_ref[...] += _dot(_pack3(-deltas, lane, 0.0), first_ref[...]).astype(BF16)
        lse = jnp.concatenate([lse_ref[...], jnp.zeros((LANES - H, tm), F32)], axis=0).T
        rq = rel_ref[...] - lse
        tile_lane = lax.broadcasted_iota(jnp.int32, (1, H * LANES), 1)
        extra = tile_lane % LANES - HEAD_DIM * (1 - (tile_lane // LANES) % 2)
        kept = jnp.where((extra >= 0) & (extra < 3), jnp.zeros((), BF16), q_ref[...])
        qa2_ref[...] = kept + _dot(_pack3(rq, lane, 0.0), first_ref[...]).astype(BF16)

    return pl.pallas_call(
        body, name="attn_out", grid=(S // tm,),
        in_specs=[_rows(tm, H * LANES), _rows(tm, D), _rows(tm, D), _rows(tm, D), _whole((D, D)),
                  _rows(tm, H * LANES), _whole((LANES, H * LANES)), _rows(tm, LANES),
                  pl.BlockSpec((H, tm), lambda i: (0, i))],
        out_specs=[_rows(tm, D), _rows(tm, D), _rows(tm, D), _rows(tm, D), _rows(tm, H * LANES),
                   _rows(tm, H * LANES), _whole((1, LANES))],
        out_shape=[jax.ShapeDtypeStruct((S, D), F32), jax.ShapeDtypeStruct((S, D), BF16),
                   jax.ShapeDtypeStruct((S, D), BF16), jax.ShapeDtypeStruct((S, D), BF16),
                   jax.ShapeDtypeStruct((S, H * LANES), BF16), jax.ShapeDtypeStruct((S, H * LANES), BF16),
                   jax.ShapeDtypeStruct((1, LANES), F32)],
        scratch_shapes=[pltpu.VMEM((tm, D), F32), pltpu.VMEM((tm, D), F32)],
        compiler_params=_params(1),
    )(o_aug, z, x1, target, w_out, qa, _scatter_matrices(H)[0], rel, lse)


def _attn_bwd(qend, qa2, doa, ka, va, T):
    S = qa2.shape[0]
    H = qa2.shape[1] // LANES
    nb = S // T
    G = 2 * nb // qend.shape[1]
    W = G * T

    def body(qe_ref, q_ref, do_ref, k_ref, v_ref, dq_ref, dk_ref, dv_ref, dkt_acc, dvt_acc):
        h, g = pl.program_id(0), pl.program_id(1)

        @pl.when(g == 0)
        def _():
            dq_ref[...] = jnp.zeros((S, LANES), F32)

        kb = k_ref[...]
        vb = v_ref[...]
        dkt_acc[...] = jnp.zeros((LANES, W), F32)
        dvt_acc[...] = jnp.zeros((LANES, W), F32)

        def step(qi, c0, c1, masked):
            rows = pl.ds(pl.multiple_of(qi * T, T), 2 * T)
            qb = q_ref[rows, :]
            dob = do_ref[rows, :]
            s = _dot_nt(qb, kb[c0:c1])
            if masked:
                query = lax.broadcasted_iota(jnp.int32, s.shape, 0) + (c1 - 2 * T)
                s = jnp.where(lax.broadcasted_iota(jnp.int32, s.shape, 1) <= query, s, NEG)
            p = jnp.exp(s)
            ds = (p * _dot_nt(dob, vb[c0:c1])).astype(BF16)
            dvt_acc[:, c0:c1] += _dot(dob.astype(F32).T.astype(BF16), p.astype(BF16))
            dkt_acc[:, c0:c1] += _dot(qb.astype(F32).T.astype(BF16), ds)
            dq_ref[rows, :] += _dot(ds, kb[c0:c1])

        for m in range(G // 2):
            step(g * G + 2 * m, 0, (m + 1) * 2 * T, True)
        first = g * G + G
        n_all = jnp.maximum((qe_ref[h, 2 * g] - first + 2) // 2, 0)
        second = first + 2 * n_all

        def all_keys(i, carry):
            step(first + 2 * i, 0, W, False)
            return carry

        def late_keys(i, carry):
            step(second + 2 * i, W // 2, W, False)
            return carry

        lax.fori_loop(0, n_all, all_keys, 0)
        lax.fori_loop(0, (qe_ref[h, 2 * g + 1] - second + 2) // 2, late_keys, 0)
        dk_ref[...] = dkt_acc[...].T
        dv_ref[...] = dvt_acc[...].T.astype(BF16)

    heads = pl.BlockSpec((None, W, LANES), lambda h, i, qe: (h, i, 0))
    return pl.pallas_call(
        body, name="attn_bwd",
        grid_spec=pltpu.PrefetchScalarGridSpec(
            num_scalar_prefetch=1, grid=(H, nb // G),
            in_specs=[pl.BlockSpec((S, LANES), lambda h, i, qe: (0, h)), pl.BlockSpec((S, LANES), lambda h, i, qe: (0, h)),
                      pl.BlockSpec((W, LANES), lambda h, i, qe: (i, h)), pl.BlockSpec((W, LANES), lambda h, i, qe: (i, h))],
            out_specs=[pl.BlockSpec((None, S, LANES), lambda h, i, qe: (h, 0, 0)), heads, heads],
            scratch_shapes=[pltpu.VMEM((LANES, W), F32), pltpu.VMEM((LANES, W), F32)]),
        out_shape=[jax.ShapeDtypeStruct((H, S, LANES), F32), jax.ShapeDtypeStruct((H, S, LANES), F32),
                   jax.ShapeDtypeStruct((H, S, LANES), BF16)],
        compiler_params=_params(2),
    )(qend, qa2, doa, ka, va)


def _attn_proj_bwd(dqt, dka, dva, qraw, kraw, dz, f, gq, gk, w, wf, x1, g2, dx2):
    S, D = x1.shape
    H = D // HEAD_DIM
    tm = min(ROW_TILE, S)
    last = S // tm - 1
    tri = (lax.broadcasted_iota(jnp.int32, (tm, tm), 1) >= lax.broadcasted_iota(jnp.int32, (tm, tm), 0)).astype(BF16)

    def body(dq_ref, dk_ref, dv_ref, q_ref, k_ref, dz_ref, f_ref, gq_ref, gk_ref, w_ref, wf_ref, x1_ref, g2_ref,
             dx2_ref, tri_ref, dproj_ref, dx1_ref, dg2_ref, small_ref, carry, pairs):
        @pl.when(pl.program_id(0) == 0)
        def _():
            dg2_ref[...] = jnp.zeros((1, D), F32)
            small_ref[...] = jnp.zeros((8, LANES), F32)
            carry[...] = jnp.zeros((8, LANES), F32)

        lane = _lane()

        def head_pair(j, acc):
            dcs, dgq, dgk = acc
            dq2, dk2 = [], []
            q_pair, k_pair = q_ref[j], k_ref[j]
            for parity in (0, 1):
                hd = 2 * j + parity
                own, a = _own(lane, parity), _aug(parity)
                dqf = dq_ref[hd]
                dqn = jnp.where(own, dqf * Q_SCALE, 0.0)
                d, dg = _head_rms_bwd(dqn, jnp.where(own, q_pair, 0.0), gq_ref[...])
                dq2.append(d)
                dgq = dgq + dg
                dkt = dk_ref[hd]
                dcs = dcs + jnp.where(lane == hd, _col(dqf, lane, a) - _col(dkt, lane, a + 3), 0.0)
                d, dg = _head_rms_bwd(jnp.where(own, dkt, 0.0), jnp.where(own, k_pair, 0.0), gk_ref[...])
                dk2.append(d)
                dgk = dgk + dg
            pairs[0, j] = _pair_tile(*dq2, lane).astype(BF16)
            pairs[1, j] = _pair_tile(*dk2, lane).astype(BF16)
            pairs[2, j] = _pair_tile(dv_ref[2 * j], dv_ref[2 * j + 1], lane)
            return dcs, dgq, dgk

        zero = jnp.zeros((1, LANES), F32)
        dcs, dgq, dgk = lax.fori_loop(0, H // 2, head_pair, (jnp.zeros((tm, LANES), F32), zero, zero))
        for part in range(3):
            for j in range(H // 2):
                dproj_ref[:, part * D + LANES * j:part * D + LANES * (j + 1)] = pairs[part, j]
        dproj_ref[:, 3 * D:4 * D] = dz_ref[...]
        dlogf = _dot01(tri_ref[...], dcs) + carry[0:1, :]
        carry[...] = jnp.broadcast_to(dlogf[0:1, :], (8, LANES))
        df = dlogf * (1.0 / (1.0 + jnp.exp(f_ref[...])))
        dproj_ref[:, 4 * D:4 * D + LANES] = df.astype(BF16)
        small_ref[0:1, :] += jnp.sum(df, axis=0, keepdims=True)
        small_ref[1:2, :] += dgq
        small_ref[2:3, :] += dgk
        dh = _dot_nt(dproj_ref[:, 0:4 * D], w_ref[...]) + _dot_nt(dproj_ref[:, 4 * D:4 * D + LANES], wf_ref[...])
        dxn, dg = _rms_bwd(dh, x1_ref[...], g2_ref[...])
        dx1_ref[...] = dx2_ref[...] + dxn
        dg2_ref[...] += dg

    W = 4 * D + LANES
    heads = pl.BlockSpec((H, tm, LANES), lambda i: (0, last - i, 0))
    head_pairs = pl.BlockSpec((H // 2, tm, LANES), lambda i: (0, last - i, 0))
    return pl.pallas_call(
        body, name="attn_proj_bwd", grid=(S // tm,),
        in_specs=[heads, heads, heads, head_pairs, head_pairs,
                  _rows(tm, D, last), _rows(tm, LANES, last), _whole((1, LANES)), _whole((1, LANES)),
                  _whole((D, 4 * D)), _whole((D, LANES)), _rows(tm, D, last), _whole((1, D)), _rows(tm, D, last),
                  _whole((tm, tm))],
        out_specs=[_rows(tm, W, last), _rows(tm, D, last), _whole((1, D)), _whole((8, LANES))],
        out_shape=[jax.ShapeDtypeStruct((S, W), BF16), jax.ShapeDtypeStruct((S, D), F32),
                   jax.ShapeDtypeStruct((1, D), F32), jax.ShapeDtypeStruct((8, LANES), F32)],
        scratch_shapes=[pltpu.VMEM((8, LANES), F32), pltpu.VMEM((3, H // 2, tm, LANES), BF16)],
        compiler_params=_params(1),
    )(dqt, dka, dva, qraw, kraw, dz, f, gq, gk, w, wf, x1, g2, dx2, tri)


def _matmul_tn(a, b, col0, n, tn, name, stacked=False):
    S, M = a.shape
    ts = min(TN_ROWS, S)
    off = col0 // tn

    def body(a_ref, b_ref, o_ref):
        @pl.when(pl.program_id(1) == 0)
        def _():
            o_ref[...] = jnp.zeros((M, tn), F32)

        o_ref[...] += _dot_tn(a_ref[...], b_ref[...])

    if stacked:
        out_spec, out_shape = pl.BlockSpec((None, M, tn), lambda j, s: (j, 0, 0)), (n // tn, M, tn)
    else:
        out_spec, out_shape = pl.BlockSpec((M, tn), lambda j, s: (0, j)), (M, n)
    return pl.pallas_call(
        body, name=name, grid=(n // tn, S // ts),
        in_specs=[pl.BlockSpec((ts, M), lambda j, s: (s, 0)), pl.BlockSpec((ts, tn), lambda j, s: (s, off + j))],
        out_specs=out_spec, out_shape=jax.ShapeDtypeStruct(out_shape, F32),
        compiler_params=_params(2),
    )(a, b)


def _adamw(w, g, m, v, name):
    r, c = w.shape
    tr = ROW_TILE if r % ROW_TILE == 0 else r

    def body(w_ref, g_ref, m_ref, v_ref, d_ref, m2_ref, v2_ref):
        gv = g_ref[...]
        m2 = ADAM_B1 * m_ref[...] + (1.0 - ADAM_B1) * gv
        v2 = ADAM_B2 * v_ref[...] + (1.0 - ADAM_B2) * (gv * gv)
        m2_ref[...] = m2
        v2_ref[...] = v2
        m_hat = m2 / (1.0 - ADAM_B1 ** ADAM_STEP)
        v_hat = v2 / (1.0 - ADAM_B2 ** ADAM_STEP)
        d_ref[...] = -ADAM_LR * (m_hat / (jnp.sqrt(v_hat) + ADAM_EPS) + ADAM_WD * w_ref[...])

    spec = _rows(tr, c)
    return pl.pallas_call(
        body, name=name, grid=(r // tr,), in_specs=[spec] * 4, out_specs=[spec] * 3,
        out_shape=[jax.ShapeDtypeStruct((r, c), F32)] * 3, compiler_params=_params(1),
    )(w, g, m, v)


def _local_step(x, target, g1, w_in, conv_w, w_out, g2, wa_in, b_f, gq, gk, wa_out):
    S, D = x.shape
    H = D // HEAD_DIM
    w_qkvz = wa_in[:, :4 * D]
    wf = jnp.pad(wa_in[:, 4 * D:], ((0, 0), (0, LANES - H)))
    bf = jnp.pad(b_f, ((0, 0), (0, LANES - H)))
    gq128 = jnp.concatenate([gq, gq], axis=1)
    gk128 = jnp.concatenate([gk, gk], axis=1)

    proj, h1, yc, y, x1 = _conv_fwd(x, g1, w_in, conv_w, w_out)
    h2, qraw, kraw, z, f, c, rel, qa, ka, va, vt = _attn_front(x1, g2, w_qkvz, wf, bf, gq128, gk128)
    T = vt.shape[3]
    kstart, qend, bound = _skip_tables(c[:, :H], (c - rel)[:, :H], gq, gk, T, min(ATT_GROUP, S // T))
    o_aug, lse = lax.cond(2.0 * bound <= PLAIN_EXP_MAX, functools.partial(_attn_fwd, online_max=False),
                          functools.partial(_attn_fwd, online_max=True), kstart, qa, ka, vt)
    dx2, dx2b, o2b, dz, doa, qa2, loss = _attn_out(o_aug, lse.reshape(H, S), rel, z, x1, target, wa_out, qa)
    dqt, dka, dva = _attn_bwd(qend, qa2, doa, ka, va, T)
    dproj2, dx1, dg2, small = _attn_proj_bwd(dqt, dka, dva, qraw, kraw, dz, f, gq128, gk128, w_qkvz, wf, x1, g2, dx2)
    dproj1, dx, dx1b, dg1, dcw = _conv_bwd(dx1, x, g1, w_in, w_out, conv_w, proj, yc)

    tn = min(1024, D)
    dwa_out = _matmul_tn(o2b, dx2b, 0, D, tn, "dw_attn_out")
    dwa_in = jnp.concatenate([_matmul_tn(h2, dproj2, 0, 4 * D, tn, "dw_attn_in"),
                              _matmul_tn(h2, dproj2, 4 * D, LANES, LANES, "dw_attn_f")[:, :H]], axis=1)
    dw_out = _matmul_tn(y, dx1b, 0, D, tn, "dw_conv_out")
    dw_in = _matmul_tn(h1, dproj1, 0, 4 * D, D, "dw_conv_in", stacked=True)
    grads = dict(conv_norm_g=dg1, conv_w_in=dw_in, conv_w=dcw, conv_w_out=dw_out, attn_norm_g=dg2,
                 attn_w_in=dwa_in, attn_b_f=small[0:1, :H],
                 attn_q_norm_g=small[1:2, :HEAD_DIM] + small[1:2, HEAD_DIM:],
                 attn_k_norm_g=small[2:3, :HEAD_DIM] + small[2:3, HEAD_DIM:], attn_w_out=dwa_out)
    return loss[0, 0], dx, grads


def _coords():
    return lax.axis_index("x"), lax.axis_index("y"), lax.axis_index("c")


def _at(ref, idx):
    return ref.at[idx] if idx else ref


def _other_chips(x, y):
    return [(1 - x, y), (x, 1 - y), (1 - x, 1 - y)]


def _all_gather(halved, whole):
    nh, nw = len(halved), len(whole)

    def body(*refs):
        src_h, src_w = refs[:nh], refs[nh:nh + nw]
        out_h, out_w = refs[nh + nw:2 * nh + nw], refs[2 * nh + nw:2 * (nh + nw)]
        send_h, recv_h, send_w, recv_w = refs[2 * (nh + nw):]
        x, y, c = _coords()
        mine = 2 * x + y
        sibling = (x, y, 1 - c)
        chips = _other_chips(x, y)

        def copy_h(a, k, chip, half, to, src=None):
            dst = out_h[a].at[chip, half]
            return pltpu.make_async_remote_copy(src_ref=dst if src is None else src, dst_ref=dst,
                                                send_sem=send_h.at[a, k], recv_sem=recv_h.at[a, k],
                                                device_id=to, device_id_type=MESH)

        def copy_w(a, k, chip, to):
            return pltpu.make_async_remote_copy(src_ref=src_w[a], dst_ref=out_w[a].at[chip],
                                                send_sem=send_w.at[a, k], recv_sem=recv_w.at[a, k],
                                                device_id=to, device_id_type=MESH)

        first = [copy_h(a, k, mine, c, (*chip, c), src=src_h[a].at[c]) for a in range(nh) for k, chip in enumerate(chips)]
        first += [copy_w(a, k, mine, (*chip, c)) for a in range(nw) for k, chip in enumerate(chips)]
        for cp in first:
            cp.start()
        passed = []
        for a in range(nh):
            for k, (px, py) in enumerate(chips):
                copy_h(a, k, 2 * px + py, c, (x, y, c)).wait_recv()
                cp = copy_h(a, 3 + k, 2 * px + py, c, sibling)
                cp.start()
                passed.append(cp)
        for a in range(nh):
            for k, (px, py) in enumerate(chips):
                copy_h(a, 3 + k, 2 * px + py, 1 - c, (x, y, c)).wait_recv()
        for a in range(nw):
            for k, (px, py) in enumerate(chips):
                copy_w(a, k, 2 * px + py, (x, y, c)).wait_recv()
        for cp in first + passed:
            cp.wait_send()

    out_shape = [jax.ShapeDtypeStruct((4,) + a.shape, a.dtype) for a in list(halved) + list(whole)]
    gathered = pl.pallas_call(
        body, name="gather_weights", in_specs=[ANY] * (nh + nw), out_specs=[ANY] * (nh + nw), out_shape=out_shape,
        scratch_shapes=[pltpu.SemaphoreType.DMA((nh, 6)), pltpu.SemaphoreType.DMA((nh, 6)),
                        pltpu.SemaphoreType.DMA((nw, 3)), pltpu.SemaphoreType.DMA((nw, 3))],
    )(*halved, *whole)
    x, y, _ = _coords()
    return [lax.dynamic_update_index_in_dim(g, a, 2 * x + y, axis=0) for g, a in zip(gathered, list(halved) + list(whole))]


def _exchange(name, srcs, lands, copies, local_copies):
    ns, nl, n, nloc = len(srcs), len(lands), len(copies), len(local_copies)

    def body(*refs):
        src, land = refs[:ns], refs[ns:ns + nl]
        send, recv, local_sem = refs[ns + nl:]
        me = _coords()
        started = []
        for k, (si, s_at, li, l_at, ci) in enumerate(local_copies):
            cp = pltpu.make_async_copy(_at(src[si], s_at(*me)), _at(land[li], l_at(*me)), local_sem.at[k])
            cp.start()
            started.append(cp)
        remote = []
        for k, (si, s_at, li, l_at, peer) in enumerate(copies):
            cp = pltpu.make_async_remote_copy(src_ref=_at(src[si], s_at(*me)), dst_ref=_at(land[li], l_at(*me)),
                                              send_sem=send.at[k], recv_sem=recv.at[k],
                                              device_id=peer(*me), device_id_type=MESH)
            cp.start()
            remote.append(cp)
        for cp in remote:
            cp.wait()
        for cp in started:
            cp.wait()

    return pl.pallas_call(
        body, name=name, in_specs=[ANY] * ns, out_specs=[ANY] * nl, out_shape=list(lands),
        scratch_shapes=[pltpu.SemaphoreType.DMA((n,)), pltpu.SemaphoreType.DMA((n,)),
                        pltpu.SemaphoreType.DMA((max(nloc, 1),))],
    )(*srcs)


def _add_pairs(a, b, name):
    _, r, cols = a.shape
    tr = ROW_TILE if r % ROW_TILE == 0 else r

    def body(a_ref, b_ref, o_ref, ob_ref):
        s = a_ref[...] + b_ref[...]
        o_ref[...] = s
        ob_ref[...] = s.astype(BF16)

    spec = pl.BlockSpec((None, tr, cols), lambda j, i: (j, i, 0))
    return pl.pallas_call(
        body, name=name, grid=(4, r // tr), in_specs=[spec, spec], out_specs=[spec, spec],
        out_shape=[jax.ShapeDtypeStruct(a.shape, F32), jax.ShapeDtypeStruct(a.shape, BF16)],
        compiler_params=_params(2),
    )(a, b)


def _sum_chips(own, landed, name):
    _, r, cols = landed.shape
    tr = ROW_TILE if r % ROW_TILE == 0 else r

    def body(own_ref, land_ref, o_ref):
        acc = own_ref[...]
        for j in range(3):
            acc = acc + land_ref[j].astype(F32)
        o_ref[...] = acc

    return pl.pallas_call(
        body, name=name, grid=(r // tr,),
        in_specs=[_rows(tr, cols), pl.BlockSpec((3, tr, cols), lambda i: (0, i, 0))], out_specs=_rows(tr, cols),
        out_shape=jax.ShapeDtypeStruct((r, cols), F32), compiler_params=_params(1),
    )(own, landed)


def _sum_devices(landed, name):
    def body(l_ref, o_ref):
        acc = l_ref[0]
        for j in range(1, 8):
            acc = acc + l_ref[j]
        o_ref[...] = acc

    return pl.pallas_call(body, name=name, out_shape=jax.ShapeDtypeStruct(landed.shape[1:], F32))(landed)


def _reduce_gradients(big, small):
    nb = len(big)
    x, y, c = _coords()
    mine = 2 * x + y
    flips = [(fx, fy, fc) for fx in (0, 1) for fy in (0, 1) for fc in (0, 1) if fx or fy or fc]

    def flip(fx, fy, fc):
        return lambda x, y, c: (x ^ fx, y ^ fy, c ^ fc)

    copies = [(a, (lambda j: lambda x, y, c: (j, 1 - c))(j), a, (lambda j: lambda x, y, c: (j,))(j), flip(0, 0, 1))
              for a in range(nb) for j in range(4)]
    copies += [(nb, lambda x, y, c: (), nb, lambda x, y, c: (4 * x + 2 * y + c,), flip(*f)) for f in flips]
    lands = [jax.ShapeDtypeStruct((4,) + g.shape[2:], F32) for g in big] + [jax.ShapeDtypeStruct((8,) + small.shape, F32)]
    local = [(nb, lambda x, y, c: (), nb, lambda x, y, c: (4 * x + 2 * y + c,), None)]
    landed = _exchange("swap_halves", list(big) + [small], lands, copies, local)
    small_sum = _sum_devices(landed[nb], "sum_small")

    chip_f32, chip_bf16 = [], []
    for a in range(nb):
        kept = lax.dynamic_index_in_dim(big[a], c, axis=1, keepdims=False)
        s, sb = _add_pairs(kept, landed[a], f"add_cores_{a}")
        chip_f32.append(s)
        chip_bf16.append(sb)

    chip_flips = [(1, 0), (0, 1), (1, 1)]
    copies = [(a, (lambda f: lambda x, y, c: (2 * (x ^ f[0]) + (y ^ f[1]),))(f), a, (lambda k: lambda x, y, c: (k,))(k),
               flip(f[0], f[1], 0)) for a in range(nb) for k, f in enumerate(chip_flips)]
    lands = [jax.ShapeDtypeStruct((3,) + g.shape[1:], BF16) for g in chip_bf16]
    landed = _exchange("send_chip_sums", chip_bf16, lands, copies, [])
    totals = [_sum_chips(lax.dynamic_index_in_dim(chip_f32[a], mine, axis=0, keepdims=False), landed[a],
                         f"sum_chips_{a}") for a in range(nb)]

    copies = [(a, lambda x, y, c: (), a, lambda x, y, c: (), flip(0, 0, 1)) for a in range(nb)]
    lands = [jax.ShapeDtypeStruct(t.shape, F32) for t in totals]
    landed = _exchange("swap_sums", totals, lands, copies, [])
    return [jnp.stack([jnp.where(c == 0, t, l), jnp.where(c == 0, l, t)]) for t, l in zip(totals, landed)], small_sum


def kernel(x, conv_norm_g, conv_w_in, conv_w, conv_w_out, attn_norm_g, attn_w_in, attn_b_f, attn_q_norm_g, attn_k_norm_g, attn_w_out, loss_target, m_conv_norm_g, m_conv_w_in, m_conv_w, m_conv_w_out, m_attn_norm_g, m_attn_w_in, m_attn_b_f, m_attn_q_norm_g, m_attn_k_norm_g, m_attn_w_out, v_conv_norm_g, v_conv_w_in, v_conv_w, v_conv_w_out, v_attn_norm_g, v_attn_w_in, v_attn_b_f, v_attn_q_norm_g, v_attn_k_norm_g, v_attn_w_out):
    xi, yi, _ = _coords()
    chip = 2 * xi + yi
    D = x.shape[2]
    H = D // HEAD_DIM
    names = ["conv_norm_g", "conv_w_in", "conv_w", "conv_w_out", "attn_norm_g", "attn_w_in", "attn_b_f",
             "attn_q_norm_g", "attn_k_norm_g", "attn_w_out"]
    weights = dict(zip(names, [conv_norm_g, conv_w_in, conv_w, conv_w_out, attn_norm_g, attn_w_in, attn_b_f,
                               attn_q_norm_g, attn_k_norm_g, attn_w_out]))
    m_in = dict(zip(names, [m_conv_norm_g, m_conv_w_in, m_conv_w, m_conv_w_out, m_attn_norm_g, m_attn_w_in,
                            m_attn_b_f, m_attn_q_norm_g, m_attn_k_norm_g, m_attn_w_out]))
    v_in = dict(zip(names, [v_conv_norm_g, v_conv_w_in, v_conv_w, v_conv_w_out, v_attn_norm_g, v_attn_w_in,
                            v_attn_b_f, v_attn_q_norm_g, v_attn_k_norm_g, v_attn_w_out]))
    weights = {k: w[0] for k, w in weights.items()}
    m_in = {k: w[0] for k, w in m_in.items()}
    v_in = {k: w[0] for k, w in v_in.items()}

    big_names = ["conv_w_in", "attn_w_in", "conv_w_out", "attn_w_out"]
    halved = [weights[k].astype(BF16).reshape(2, weights[k].shape[0] // 2, weights[k].shape[1]) for k in big_names]
    q = D // 4
    small_w = jnp.concatenate([weights["conv_w"], weights["attn_norm_g"][None, :], jnp.zeros((4, q), F32)], axis=0)
    g_in, ga_in, g_out, ga_out, g_small = _all_gather(halved, [small_w])
    w_in = g_in.reshape(4, D, D)
    wa_in = ga_in.reshape(4, D, D + H // 4).transpose(1, 0, 2).reshape(D, 4 * D + H)
    w_out = g_out.reshape(D, D)
    wa_out = ga_out.reshape(D, D)
    conv_w_full = g_small[:, 0:3, :].transpose(1, 0, 2).reshape(3, D)
    attn_g_full = g_small[:, 3, :].reshape(1, D)

    loss_part, grad_x, grads = _local_step(x[0], loss_target[0], weights["conv_norm_g"][None, :], w_in, conv_w_full,
                                           w_out, attn_g_full, wa_in, weights["attn_b_f"][None, :],
                                           weights["attn_q_norm_g"][None, :], weights["attn_k_norm_g"][None, :], wa_out)

    big = [grads["conv_w_in"].reshape(4, 2, D // 2, D),
           grads["attn_w_in"].reshape(D, 4, D + H // 4).transpose(1, 0, 2).reshape(4, 2, D // 2, D + H // 4),
           grads["conv_w_out"].reshape(4, 2, D // 8, D), grads["attn_w_out"].reshape(4, 2, D // 8, D)]
    tail = jnp.concatenate([grads["attn_b_f"], grads["attn_q_norm_g"], grads["attn_k_norm_g"],
                            jnp.reshape(loss_part, (1, 1)), jnp.zeros((1, D - H - 2 * HEAD_DIM - 1), F32)], axis=1)
    small = jnp.concatenate([grads["conv_norm_g"], grads["conv_w"], grads["attn_norm_g"], tail,
                             jnp.zeros((2, D), F32)], axis=0)
    reduced, small_sum = _reduce_gradients(big, small)
    final = {k: r.reshape(weights[k].shape) for k, r in zip(big_names, reduced)}
    final["conv_norm_g"] = small_sum[0]
    final["conv_w"] = lax.dynamic_slice_in_dim(small_sum[1:4], chip * q, q, axis=1)
    final["attn_norm_g"] = lax.dynamic_slice_in_dim(small_sum[4], chip * q, q, axis=0)
    final["attn_b_f"] = small_sum[5, :H]
    final["attn_q_norm_g"] = small_sum[5, H:H + HEAD_DIM]
    final["attn_k_norm_g"] = small_sum[5, H + HEAD_DIM:H + 2 * HEAD_DIM]
    loss = small_sum[5, H + 2 * HEAD_DIM]

    delta, new_m, new_v = {}, {}, {}
    for k in names:
        shape = weights[k].shape
        as2d = (lambda a: a.reshape(1, -1)) if len(shape) == 1 else (lambda a: a)
        d, m2, v2 = _adamw(as2d(weights[k]), as2d(final[k]), as2d(m_in[k]), as2d(v_in[k]), "adamw_" + k)
        delta[k], new_m[k], new_v[k] = d.reshape(shape), m2.reshape(shape), v2.reshape(shape)
    lead = lambda a: a[None]
    return (loss, grad_x[None], *[lead(final[k]) for k in names], *[lead(delta[k]) for k in names],
            *[lead(new_m[k]) for k in names], *[lead(new_v[k]) for k in names])
```

```python
import functools

import jax
import jax.numpy as jnp
from jax import lax
from jax.experimental import pallas as pl
from jax.experimental.pallas import tpu as pltpu

F32 = jnp.float32
BF16 = jnp.bfloat16
HEAD_DIM = 64
LANES = 128
RMS_EPS = 1e-6
NEG = -1e30
Q_SCALE = 0.125
ROW_TILE = 256
CONV_TILE = 512
ATT_GROUP = 4
SKIP_LOG = 106.0
PLAIN_EXP_MAX = 60.0
TN_ROWS = 2048
VMEM_LIMIT = 56 << 20
ADAM_LR, ADAM_B1, ADAM_B2, ADAM_EPS, ADAM_WD, ADAM_STEP = 0.001, 0.9, 0.999, 1e-08, 0.01, 10
MESH = pl.DeviceIdType.MESH
ANY = pl.BlockSpec(memory_space=pl.ANY)


def _lane():
    return lax.broadcasted_iota(jnp.int32, (1, LANES), 1)


def _split3(x):
    hi = x.astype(BF16).astype(F32)
    r = x - hi
    mid = r.astype(BF16).astype(F32)
    lo = (r - mid).astype(BF16).astype(F32)
    return hi, mid, lo


STAT_STRIDE = 16
ONE_LANE = 3 * STAT_STRIDE


def _pack3(x, lane, one):
    hi, mid, lo = _split3(x)
    packed = hi + pltpu.roll(mid, STAT_STRIDE, 1) + pltpu.roll(lo, 2 * STAT_STRIDE, 1)
    return jnp.where(lane == ONE_LANE, one, packed).astype(BF16)


def _scatter_matrices(H):
    rows = lax.broadcasted_iota(jnp.int32, (LANES, H * LANES), 0)
    cols = lax.broadcasted_iota(jnp.int32, (LANES, H * LANES), 1)
    head, within = cols // LANES, cols % LANES
    extra = within - HEAD_DIM * (1 - head % 2)
    term = (rows < ONE_LANE) & (rows % STAT_STRIDE == head)
    first = ((term & (extra == rows // STAT_STRIDE)) | ((rows == ONE_LANE) & (extra >= 3) & (extra < 6)))
    second = ((term & (extra - 3 == rows // STAT_STRIDE)) | ((rows == ONE_LANE) & (extra >= 0) & (extra < 3)))
    return first.astype(BF16), second.astype(BF16)


def _gather_matrix(H, lo):
    rows = lax.broadcasted_iota(jnp.int32, (H * LANES, LANES), 0)
    cols = lax.broadcasted_iota(jnp.int32, (H * LANES, LANES), 1)
    extra = rows % LANES - HEAD_DIM * (1 - (rows // LANES) % 2)
    return ((rows // LANES == cols) & (extra >= lo) & (extra < lo + 3)).astype(BF16)


def _put(base, lane, start, parts):
    for j, p in enumerate(parts):
        base = jnp.where(lane == start + j, p, base)
    return base


def _col(x, lane, idx):
    return jnp.sum(jnp.where(lane == idx, x, 0.0), axis=1, keepdims=True)


def _feat(parity):
    return HEAD_DIM * parity


def _aug(parity):
    return HEAD_DIM * (1 - parity)


def _own(lane, parity):
    return (lane >= _feat(parity)) & (lane < _feat(parity) + HEAD_DIM)


def _head_tile(ref, hd, lane):
    j = hd // 2
    return jnp.where(_own(lane, hd % 2), ref[:, LANES * j:LANES * (j + 1)], 0.0)


def _pair_tile(even, odd, lane):
    return jnp.where(lane < HEAD_DIM, even, odd)


def _sigmoid(x):
    return 0.5 * jnp.tanh(0.5 * x) + 0.5


def _dot(a, b):
    return jnp.dot(a, b, preferred_element_type=F32)


def _dot_nt(a, b):
    return lax.dot_general(a, b, (((1,), (1,)), ((), ())), preferred_element_type=F32)


def _dot_tn(a, b):
    return lax.dot_general(a, b, (((0,), (0,)), ((), ())), preferred_element_type=F32)


def _dot01(tri, x):
    hi, mid, lo = _split3(x)
    return _dot(tri, hi.astype(BF16)) + _dot(tri, mid.astype(BF16)) + _dot(tri, lo.astype(BF16))


def _rms_bwd(dh, x, g):
    inv = lax.rsqrt(jnp.mean(x * x, axis=-1, keepdims=True) + RMS_EPS)
    xh = x * inv
    dxn = dh * g
    dx = inv * (dxn - xh * jnp.mean(dxn * xh, axis=-1, keepdims=True))
    return dx, jnp.sum(dh * xh, axis=0, keepdims=True)


def _head_rms_bwd(dn, t, g, ones):
    sq = t * t
    hi = sq.astype(BF16)
    lo = (sq - hi.astype(F32)).astype(BF16)
    inv = lax.rsqrt((_dot(hi, ones) + _dot(lo, ones)) * (1.0 / HEAD_DIM) + RMS_EPS)
    th = t * inv
    gd = dn * g
    d = inv * (gd - th * (jnp.sum(gd * th, axis=1, keepdims=True) * (1.0 / HEAD_DIM)))
    return d, jnp.sum(dn * th, axis=0, keepdims=True)


def _params(n_grid):
    return pltpu.CompilerParams(dimension_semantics=("arbitrary",) * n_grid, vmem_limit_bytes=VMEM_LIMIT)


def _rows(tm, cols, rev=None):
    if rev is None:
        return pl.BlockSpec((tm, cols), lambda i: (i, 0))
    return pl.BlockSpec((tm, cols), lambda i: (rev - i, 0))


def _whole(shape, buffers=None):
    mode = {} if buffers is None else dict(pipeline_mode=pl.Buffered(buffers))
    return pl.BlockSpec(shape, lambda *_: (0,) * len(shape), **mode)


def _conv_fwd(x, g1, w_in, conv_w, w_out):
    S, D = x.shape
    tm = min(CONV_TILE, S)
    sub = min(ROW_TILE, tm)

    def body(x_ref, g_ref, win_ref, cw_ref, wout_ref, proj_ref, h_ref, yc_ref, y_ref, x1_ref, prev_u):
        @pl.when(pl.program_id(0) == 0)
        def _():
            prev_u[...] = jnp.zeros((sub, D), F32)

        for r in range(0, tm, sub):
            rows = slice(r, r + sub)
            xv = x_ref[rows, :]
            inv = lax.rsqrt(jnp.mean(xv * xv, axis=-1, keepdims=True) + RMS_EPS)
            h = (xv * inv * g_ref[...]).astype(BF16)
            h_ref[rows, :] = h
            for j in range(4):
                proj_ref[rows, j * D:(j + 1) * D] = _dot(h, win_ref[j])
            u = proj_ref[rows, D:2 * D] * proj_ref[rows, 2 * D:3 * D]
            pu = prev_u[...]
            row = lax.broadcasted_iota(jnp.int32, (sub, 1), 0)
            u1 = jnp.where(row < 1, pltpu.roll(pu, 1, 0), pltpu.roll(u, 1, 0))
            u2 = jnp.where(row < 2, pltpu.roll(pu, 2, 0), pltpu.roll(u, 2, 0))
            prev_u[...] = u
            w = cw_ref[...]
            yc = w[2:3] * u + w[1:2] * u1 + w[0:1] * u2
            yc_ref[rows, :] = yc
            z = proj_ref[rows, 3 * D:4 * D]
            y = (proj_ref[rows, 0:D] * yc * (z * _sigmoid(z))).astype(BF16)
            y_ref[rows, :] = y
            x1_ref[rows, :] = xv + _dot(y, wout_ref[...])

    return pl.pallas_call(
        body, name="conv_fwd", grid=(S // tm,),
        in_specs=[_rows(tm, D), _whole((1, D)), _whole((4, D, D), 1), _whole((3, D)), _whole((D, D), 1)],
        out_specs=[_rows(tm, 4 * D), _rows(tm, D), _rows(tm, D), _rows(tm, D), _rows(tm, D)],
        out_shape=[jax.ShapeDtypeStruct((S, 4 * D), F32), jax.ShapeDtypeStruct((S, D), BF16),
                   jax.ShapeDtypeStruct((S, D), F32), jax.ShapeDtypeStruct((S, D), BF16),
                   jax.ShapeDtypeStruct((S, D), F32)],
        scratch_shapes=[pltpu.VMEM((sub, D), F32)],
        compiler_params=_params(1),
    )(x, g1, w_in, conv_w, w_out)


def _conv_bwd(dx1, x, g1, w_in, w_out, conv_w, proj, yc):
    S, D = x.shape
    tm = min(ROW_TILE, S)
    sub = min(ROW_TILE, tm)
    last = S // tm - 1

    def body(dx1_ref, x_ref, g_ref, win_ref, wout_ref, cw_ref, proj_ref, yc_ref,
             dproj_ref, dx_ref, dx1b_ref, dg_ref, dcw_ref, next_d):
        @pl.when(pl.program_id(0) == 0)
        def _():
            dg_ref[...] = jnp.zeros((1, D), F32)
            dcw_ref[...] = jnp.zeros((3, D), F32)
            next_d[...] = jnp.zeros((sub, D), F32)

        for r in range(tm - sub, -1, -sub):
            rows = slice(r, r + sub)
            dx1v = dx1_ref[rows, :]
            dx1b = dx1v.astype(BF16)
            dx1b_ref[rows, :] = dx1b
            dy = _dot_nt(dx1b, wout_ref[...])
            b = proj_ref[rows, 0:D]
            c = proj_ref[rows, D:2 * D]
            xin = proj_ref[rows, 2 * D:3 * D]
            z = proj_ref[rows, 3 * D:4 * D]
            sg = _sigmoid(z)
            sz = z * sg
            ycv = yc_ref[rows, :]
            d0 = dy * b * sz
            dproj_ref[rows, 0:D] = (dy * ycv * sz).astype(BF16)
            dproj_ref[rows, 3 * D:4 * D] = (dy * b * ycv * (sg * (1.0 + z * (1.0 - sg)))).astype(BF16)
            nd = next_d[...]
            row = lax.broadcasted_iota(jnp.int32, (sub, 1), 0)
            d1 = jnp.where(row >= sub - 1, pltpu.roll(nd, sub - 1, 0), pltpu.roll(d0, sub - 1, 0))
            d2 = jnp.where(row >= sub - 2, pltpu.roll(nd, sub - 2, 0), pltpu.roll(d0, sub - 2, 0))
            next_d[...] = d0
            w = cw_ref[...]
            du = w[2:3] * d0 + w[1:2] * d1 + w[0:1] * d2
            u = c * xin
            dcw_ref[2:3, :] += jnp.sum(d0 * u, axis=0, keepdims=True)
            dcw_ref[1:2, :] += jnp.sum(d1 * u, axis=0, keepdims=True)
            dcw_ref[0:1, :] += jnp.sum(d2 * u, axis=0, keepdims=True)
            dproj_ref[rows, D:2 * D] = (du * xin).astype(BF16)
            dproj_ref[rows, 2 * D:3 * D] = (du * c).astype(BF16)
            dh = _dot_nt(dproj_ref[rows, 0:D], win_ref[0])
            for j in range(1, 4):
                dh = dh + _dot_nt(dproj_ref[rows, j * D:(j + 1) * D], win_ref[j])
            dxn, dg = _rms_bwd(dh, x_ref[rows, :], g_ref[...])
            dx_ref[rows, :] = dx1v + dxn
            dg_ref[...] += dg

    return pl.pallas_call(
        body, name="conv_bwd", grid=(S // tm,),
        in_specs=[_rows(tm, D, last), _rows(tm, D, last), _whole((1, D)), _whole((4, D, D), 1), _whole((D, D), 1),
                  _whole((3, D)), _rows(tm, 4 * D, last), _rows(tm, D, last)],
        out_specs=[_rows(tm, 4 * D, last), _rows(tm, D, last), _rows(tm, D, last), _whole((1, D)), _whole((3, D))],
        out_shape=[jax.ShapeDtypeStruct((S, 4 * D), BF16), jax.ShapeDtypeStruct((S, D), F32),
                   jax.ShapeDtypeStruct((S, D), BF16), jax.ShapeDtypeStruct((1, D), F32),
                   jax.ShapeDtypeStruct((3, D), F32)],
        scratch_shapes=[pltpu.VMEM((sub, D), F32)],
        compiler_params=_params(1),
    )(dx1, x, g1, w_in, w_out, conv_w, proj, yc)


def _attn_front(x1, g2, w, wf, bf, gq, gk):
    S, D = x1.shape
    H = D // HEAD_DIM
    tm = min(ROW_TILE, S)
    tri = (lax.broadcasted_iota(jnp.int32, (tm, tm), 1) <= lax.broadcasted_iota(jnp.int32, (tm, tm), 0)).astype(BF16)

    def body(x_ref, g_ref, w_ref, wf_ref, bf_ref, gq_ref, gk_ref, tri_ref, first_ref, second_ref,
             h_ref, qh_ref, kh_ref, z_ref, f_ref, c_ref, rel_ref, qa_ref, ka_ref, va_ref, vt_ref,
             carry, v_s, qraw_ref, kraw_ref):
        @pl.when(pl.program_id(0) == 0)
        def _():
            carry[...] = jnp.zeros((8, LANES), F32)

        xv = x_ref[...]
        inv = lax.rsqrt(jnp.mean(xv * xv, axis=-1, keepdims=True) + RMS_EPS)
        h = (xv * inv * g_ref[...]).astype(BF16)
        h_ref[...] = h
        qraw_ref[...] = _dot(h, w_ref[:, 0:D])
        kraw_ref[...] = _dot(h, w_ref[:, D:2 * D])
        v_s[...] = _dot(h, w_ref[:, 2 * D:3 * D])
        z_ref[...] = _dot(h, w_ref[:, 3 * D:4 * D])
        lane = _lane()
        f = _dot(h, wf_ref[...]) + bf_ref[...]
        f_ref[...] = f
        logf = jnp.where(lane < H, jnp.minimum(f, 0.0) - jnp.log(1.0 + jnp.exp(-jnp.abs(f))), 0.0)
        cs = _dot01(tri_ref[...], logf) + carry[0:1, :]
        c_ref[...] = cs
        carry[...] = jnp.broadcast_to(cs[tm - 1:tm, :], (8, LANES))
        diags = jnp.zeros((tm, LANES), F32)
        for hd in range(H):
            sl = slice(LANES * hd, LANES * (hd + 1))
            a = _aug(hd % 2)
            if hd % 2 == 0:
                qh_ref[hd // 2] = qraw_ref[:, LANES * (hd // 2):LANES * (hd // 2 + 1)]
                kh_ref[hd // 2] = kraw_ref[:, LANES * (hd // 2):LANES * (hd // 2 + 1)]
            qt = _head_tile(qraw_ref, hd, lane)
            qn = qt * lax.rsqrt(jnp.sum(qt * qt, axis=1, keepdims=True) * (1.0 / HEAD_DIM) + RMS_EPS) * gq_ref[...]
            kt = _head_tile(kraw_ref, hd, lane)
            kn = kt * lax.rsqrt(jnp.sum(kt * kt, axis=1, keepdims=True) * (1.0 / HEAD_DIM) + RMS_EPS) * gk_ref[...]
            diags = diags + jnp.where(lane == hd, jnp.sum(qn * kn, axis=1, keepdims=True) * Q_SCALE, 0.0)
            qa_ref[:, sl] = (qn * Q_SCALE).astype(BF16)
            ka_ref[:, sl] = kn.astype(BF16)
            va = jnp.where((lane >= a) & (lane < a + 3), 1.0, _head_tile(v_s, hd, lane))
            va_ref[:, sl] = va.astype(BF16)
            vt_ref[hd] = va.T.astype(BF16)
        rel = cs - diags
        rel_ref[...] = rel
        qa_ref[...] += _dot(_pack3(rel, lane, 1.0), first_ref[...]).astype(BF16)
        ka_ref[...] += _dot(_pack3(-cs, lane, 1.0), second_ref[...]).astype(BF16)

    nb = S // tm
    heads = pl.BlockSpec((H // 2, tm, LANES), lambda i: (0, i, 0))
    return pl.pallas_call(
        body, name="attn_front", grid=(nb,),
        in_specs=[_rows(tm, D), _whole((1, D)), _whole((D, 4 * D)), _whole((D, LANES)), _whole((1, LANES)),
                  _whole((1, LANES)), _whole((1, LANES)), _whole((tm, tm)), _whole((LANES, H * LANES)),
                  _whole((LANES, H * LANES))],
        out_specs=[_rows(tm, D), heads, heads, _rows(tm, D), _rows(tm, LANES), _rows(tm, LANES), _rows(tm, LANES),
                   _rows(tm, H * LANES), _rows(tm, H * LANES), _rows(tm, H * LANES),
                   pl.BlockSpec((H, None, LANES, tm), lambda i: (0, i, 0, 0))],
        out_shape=[jax.ShapeDtypeStruct((S, D), BF16), jax.ShapeDtypeStruct((H // 2, S, LANES), F32),
                   jax.ShapeDtypeStruct((H // 2, S, LANES), F32), jax.ShapeDtypeStruct((S, D), F32),
                   jax.ShapeDtypeStruct((S, LANES), F32), jax.ShapeDtypeStruct((S, LANES), F32),
                   jax.ShapeDtypeStruct((S, LANES), F32),
                   jax.ShapeDtypeStruct((S, H * LANES), BF16), jax.ShapeDtypeStruct((S, H * LANES), BF16),
                   jax.ShapeDtypeStruct((S, H * LANES), BF16), jax.ShapeDtypeStruct((H, nb, LANES, tm), BF16)],
        scratch_shapes=[pltpu.VMEM((8, LANES), F32), pltpu.VMEM((tm, D), F32), pltpu.VMEM((tm, D), F32),
                        pltpu.VMEM((tm, D), F32)],
        compiler_params=_params(1),
    )(x1, g2, w, wf, bf, gq, gk, tri, *_scatter_matrices(H))


def _skip_tables(c, diag, gq, gk, T, G):
    nb = c.shape[0] // T
    bound = 8.0 * jnp.max(jnp.abs(gq)) * jnp.max(jnp.abs(gk))
    first, last = c[0::T, :], c[T - 1::T, :]
    lowest = jnp.maximum(jnp.min(diag.reshape(nb, T, -1), axis=1), -bound)
    idx = jnp.arange(nb)
    margin = (SKIP_LOG + bound) - lowest
    need = (last[None, :, :] <= first[:, None, :] + margin[:, None, :]) & (idx[None, :, None] < idx[:, None, None])
    need = need | (idx[None, :, None] == idx[:, None, None])
    kstart = jnp.argmax(need, axis=1)
    qend = nb - 1 - jnp.argmax(need[::-1], axis=0)
    kstart = jnp.min(kstart.reshape(2 * nb // G, G // 2, -1), axis=1)
    kstart = kstart - (kstart & 1)
    qend = jnp.max(qend.reshape(2 * nb // G, G // 2, -1), axis=1)
    return kstart.T.astype(jnp.int32), qend.T.astype(jnp.int32), bound


def _attn_fwd(kstart, qa, ka, vt, online_max):
    S = qa.shape[0]
    H = qa.shape[1] // LANES
    nb, T = vt.shape[1], vt.shape[3]
    G = 2 * nb // kstart.shape[1]
    W = G * T

    def finish(acc, shift, o_ref, lse_ref):
        a = _aug(pl.program_id(0) % 2)
        feat = lax.broadcasted_iota(jnp.int32, (LANES, 1), 0)
        l = jnp.sum(jnp.where(feat == a, acc, 0.0), axis=0, keepdims=True)
        o_ref[...] = (acc * (1.0 / l)).T
        lse_ref[...] = shift + jnp.log(l)

    def causal(st):
        return jnp.where(lax.broadcasted_iota(jnp.int32, st.shape, 0) <= lax.broadcasted_iota(jnp.int32, st.shape, 1),
                         st, NEG)

    def fast_body(ks_ref, q_ref, k_ref, vt_ref, o_ref, lse_ref, acc_ref, sa_ref, sb_ref, sc_ref):
        h, g = pl.program_id(0), pl.program_id(1)
        q = q_ref[...]
        acc_ref[...] = jnp.zeros((LANES, W), F32)

        def scores(ki, lo):
            return _dot_nt(k_ref[pl.ds(pl.multiple_of(ki * T, T), 2 * T), :], q[lo * T:, :])

        def weighted(ki, p):
            return _dot(vt_ref[ki], p[:T]) + _dot(vt_ref[ki + 1], p[T:])

        first = ks_ref[h, 2 * g + 1]
        early = jnp.minimum(ks_ref[h, 2 * g], first)

        def narrow(i, carry):
            ki = early + 2 * i
            st = _dot_nt(k_ref[pl.ds(pl.multiple_of(ki * T, T), 2 * T), :], q[:W // 2, :])
            acc_ref[:, :W // 2] += weighted(ki, jnp.exp(st).astype(BF16))
            return carry

        lax.fori_loop(0, (first - early) // 2, narrow, 0)
        steps = (g * G - first) // 2
        sa_ref[...] = scores(first, 0)

        def advance(ki, cur_ref, next_ref):
            p = jnp.exp(cur_ref[...]).astype(BF16)
            next_ref[...] = scores(ki + 2, 0)
            acc_ref[...] += weighted(ki, p)

        def loop(i, carry):
            advance(first + 4 * i, sa_ref, sb_ref)
            advance(first + 4 * i + 2, sb_ref, sa_ref)
            return carry

        lax.fori_loop(0, steps // 2, loop, 0)

        def first_own(pending_ref):
            p = jnp.exp(causal(pending_ref[...])).astype(BF16)
            if G > 2:
                sc_ref[:, :W - 2 * T] = scores(g * G + 2, 2)
            acc_ref[...] += weighted(g * G, p)

        @pl.when(steps % 2 == 1)
        def _():
            advance(g * G - 2, sa_ref, sb_ref)
            first_own(sb_ref)

        @pl.when(steps % 2 == 0)
        def _():
            first_own(sa_ref)

        if G > 2:
            acc_ref[:, 2 * T:] += weighted(g * G + 2, jnp.exp(causal(sc_ref[:, :W - 2 * T])).astype(BF16))
        for j in range(4, G, 2):
            p = jnp.exp(causal(scores(g * G + j, j))).astype(BF16)
            acc_ref[:, j * T:] += weighted(g * G + j, p)
        finish(acc_ref[...], 0.0, o_ref, lse_ref)

    def online_body(ks_ref, q_ref, k_ref, vt_ref, o_ref, lse_ref, acc_ref, m_ref):
        h, g = pl.program_id(0), pl.program_id(1)
        q = q_ref[...]
        m_ref[...] = jnp.full((8, W), NEG, F32)
        acc_ref[...] = jnp.zeros((LANES, W), F32)

        def update(st, vtb, lo):
            m_old = m_ref[0:1, lo:]
            m_new = jnp.maximum(m_old, jnp.max(st, axis=0, keepdims=True))
            p = jnp.exp(st - m_new).astype(BF16)
            acc_ref[:, lo:] = acc_ref[:, lo:] * jnp.exp(m_old - m_new) + _dot(vtb, p)
            m_ref[:, lo:] = jnp.broadcast_to(m_new, (8, W - lo))

        def loop(ki, carry):
            kb = k_ref[pl.ds(pl.multiple_of(ki * T, T), T), :]
            update(_dot_nt(kb, q), vt_ref[ki], 0)
            return carry

        lax.fori_loop(jnp.minimum(ks_ref[h, 2 * g], ks_ref[h, 2 * g + 1]), g * G, loop, 0)
        for j in range(G):
            ki = g * G + j
            kb = k_ref[pl.ds(pl.multiple_of(ki * T, T), T), :]
            update(causal(_dot_nt(kb, q[j * T:, :])), vt_ref[ki], j * T)
        finish(acc_ref[...], m_ref[0:1, :], o_ref, lse_ref)

    return pl.pallas_call(
        online_body if online_max else fast_body, name="attn_fwd_online" if online_max else "attn_fwd",
        grid_spec=pltpu.PrefetchScalarGridSpec(
            num_scalar_prefetch=1, grid=(H, nb // G),
            in_specs=[pl.BlockSpec((W, LANES), lambda h, i, ks: (i, h)),
                      pl.BlockSpec((S, LANES), lambda h, i, ks: (0, h)),
                      pl.BlockSpec((None, nb, LANES, T), lambda h, i, ks: (h, 0, 0, 0))],
            out_specs=[pl.BlockSpec((W, LANES), lambda h, i, ks: (i, h)),
                       pl.BlockSpec((None, 1, W), lambda h, i, ks: (h, 0, i))],
            scratch_shapes=[pltpu.VMEM((LANES, W), F32)] + (
                [pltpu.VMEM((8, W), F32)] if online_max else [pltpu.VMEM((2 * T, W), F32)] * 3)),
        out_shape=[jax.ShapeDtypeStruct((S, H * LANES), F32), jax.ShapeDtypeStruct((H, 1, S), F32)],
        compiler_params=_params(2),
    )(kstart, qa, ka, vt)


def _attn_out(o_aug, lse, rel, z, x1, target, w_out, qa):
    S, D = x1.shape
    H = D // HEAD_DIM
    tm = min(ROW_TILE, S)

    def body(o_ref, z_ref, x1_ref, t_ref, w_ref, q_ref, first_ref, rel_ref, lse_ref,
             dx2_ref, dx2b_ref, o2b_ref, dz_ref, doa_ref, qa2_ref, loss_ref, oc_s, do_s):
        @pl.when(pl.program_id(0) == 0)
        def _():
            loss_ref[...] = jnp.zeros((1, LANES), F32)

        lane = _lane()
        for j in range(H // 2):
            oc_s[:, LANES * j:LANES * (j + 1)] = _pair_tile(o_ref[:, 2 * LANES * j:2 * LANES * j + LANES],
                                                            o_ref[:, 2 * LANES * j + LANES:2 * LANES * (j + 1)], lane)
        oc = oc_s[...]
        zv = z_ref[...]
        sg = _sigmoid(zv)
        sz = zv * sg
        o2 = (oc * sz).astype(BF16)
        o2b_ref[...] = o2
        e = x1_ref[...] + _dot(o2, w_ref[...]) - t_ref[...]
        sq = jnp.sum(jnp.sum(e * e, axis=1, keepdims=True), axis=0, keepdims=True)
        loss_ref[...] += jnp.broadcast_to(sq * (0.5 / D), (1, LANES))
        dx2 = e * (1.0 / D)
        dx2_ref[...] = dx2
        dx2b = dx2.astype(BF16)
        dx2b_ref[...] = dx2b
        do2 = _dot_nt(dx2b, w_ref[...])
        dz_ref[...] = (do2 * oc * (sg * (1.0 + zv * (1.0 - sg)))).astype(BF16)
        do_s[...] = do2 * sz
        deltas = jnp.zeros((tm, LANES), F32)
        for hd in range(H):
            dt = _head_tile(do_s, hd, lane)
            delta = jnp.sum(dt * _head_tile(oc_s, hd, lane), axis=1, keepdims=True)
            deltas = deltas + jnp.where(lane == hd, delta, 0.0)
            doa_ref[:, LANES * hd:LANES * (hd + 1)] = dt.astype(BF16)
        doa_ref[...] += _dot(_pack3(-deltas, lane, 0.0), first_ref[...]).astype(BF16)
        lse = jnp.concatenate([lse_ref[...], jnp.zeros((LANES - H, tm), F32)], axis=0).T
        rq = rel_ref[...] - lse
        tile_lane = lax.broadcasted_iota(jnp.int32, (1, H * LANES), 1)
        extra = tile_lane % LANES - HEAD_DIM * (1 - (tile_lane // LANES) % 2)
        kept = jnp.where((extra >= 0) & (extra < 3), jnp.zeros((), BF16), q_ref[...])
        qa2_ref[...] = kept + _dot(_pack3(rq, lane, 0.0), first_ref[...]).astype(BF16)

    return pl.pallas_call(
        body, name="attn_out", grid=(S // tm,),
        in_specs=[_rows(tm, H * LANES), _rows(tm, D), _rows(tm, D), _rows(tm, D), _whole((D, D)),
                  _rows(tm, H * LANES), _whole((LANES, H * LANES)), _rows(tm, LANES),
                  pl.BlockSpec((H, tm), lambda i: (0, i))],
        out_specs=[_rows(tm, D), _rows(tm, D), _rows(tm, D), _rows(tm, D), _rows(tm, H * LANES),
                   _rows(tm, H * LANES), _whole((1, LANES))],
        out_shape=[jax.ShapeDtypeStruct((S, D), F32), jax.ShapeDtypeStruct((S, D), BF16),
                   jax.ShapeDtypeStruct((S, D), BF16), jax.ShapeDtypeStruct((S, D), BF16),
                   jax.ShapeDtypeStruct((S, H * LANES), BF16), jax.ShapeDtypeStruct((S, H * LANES), BF16),
                   jax.ShapeDtypeStruct((1, LANES), F32)],
        scratch_shapes=[pltpu.VMEM((tm, D), F32), pltpu.VMEM((tm, D), F32)],
        compiler_params=_params(1),
    )(o_aug, z, x1, target, w_out, qa, _scatter_matrices(H)[0], rel, lse)


def _attn_bwd(qend, qa2, doa, ka, va, T):
    S = qa2.shape[0]
    H = qa2.shape[1] // LANES
    nb = S // T
    G = 2 * nb // qend.shape[1]
    W = G * T

    def body(qe_ref, q_ref, do_ref, k_ref, v_ref, dq_ref, dk_ref, dv_ref, dkt_acc, dvt_acc):
        h, g = pl.program_id(0), pl.program_id(1)

        @pl.when(g == 0)
        def _():
            dq_ref[...] = jnp.zeros((S, LANES), F32)

        kb = k_ref[...]
        vb = v_ref[...]
        dkt_acc[...] = jnp.zeros((LANES, W), F32)
        dvt_acc[...] = jnp.zeros((LANES, W), F32)

        def step(qi, c0, c1, masked):
            rows = pl.ds(pl.multiple_of(qi * T, T), 2 * T)
            qb = q_ref[rows, :]
            dob = do_ref[rows, :]
            s = _dot_nt(qb, kb[c0:c1])
            if masked:
                query = lax.broadcasted_iota(jnp.int32, s.shape, 0) + (c1 - 2 * T)
                s = jnp.where(lax.broadcasted_iota(jnp.int32, s.shape, 1) <= query, s, NEG)
            p = jnp.exp(s)
            ds = (p * _dot_nt(dob, vb[c0:c1])).astype(BF16)
            dvt_acc[:, c0:c1] += _dot(dob.astype(F32).T.astype(BF16), p.astype(BF16))
            dkt_acc[:, c0:c1] += _dot(qb.astype(F32).T.astype(BF16), ds)
            dq_ref[rows, :] += _dot(ds, kb[c0:c1])

        for m in range(G // 2):
            step(g * G + 2 * m, 0, (m + 1) * 2 * T, True)
        first = g * G + G
        n_all = jnp.maximum((qe_ref[h, 2 * g] - first + 2) // 2, 0)
        second = first + 2 * n_all

        def all_keys(i, carry):
            step(first + 2 * i, 0, W, False)
            return carry

        def late_keys(i, carry):
            step(second + 2 * i, W // 2, W, False)
            return carry

        lax.fori_loop(0, n_all, all_keys, 0)
        lax.fori_loop(0, (qe_ref[h, 2 * g + 1] - second + 2) // 2, late_keys, 0)
        dk_ref[...] = dkt_acc[...].T
        dv_ref[...] = dvt_acc[...].T.astype(BF16)

    heads = pl.BlockSpec((None, W, LANES), lambda h, i, qe: (h, i, 0))
    return pl.pallas_call(
        body, name="attn_bwd",
        grid_spec=pltpu.PrefetchScalarGridSpec(
            num_scalar_prefetch=1, grid=(H, nb // G),
            in_specs=[pl.BlockSpec((S, LANES), lambda h, i, qe: (0, h)), pl.BlockSpec((S, LANES), lambda h, i, qe: (0, h)),
                      pl.BlockSpec((W, LANES), lambda h, i, qe: (i, h)), pl.BlockSpec((W, LANES), lambda h, i, qe: (i, h))],
            out_specs=[pl.BlockSpec((None, S, LANES), lambda h, i, qe: (h, 0, 0)), heads, heads],
            scratch_shapes=[pltpu.VMEM((LANES, W), F32), pltpu.VMEM((LANES, W), F32)]),
        out_shape=[jax.ShapeDtypeStruct((H, S, LANES), F32), jax.ShapeDtypeStruct((H, S, LANES), F32),
                   jax.ShapeDtypeStruct((H, S, LANES), BF16)],
        compiler_params=_params(2),
    )(qend, qa2, doa, ka, va)


def _attn_proj_bwd(dqt, dka, dva, qraw, kraw, dz, f, gq, gk, w, wf, x1, g2, dx2):
    S, D = x1.shape
    H = D // HEAD_DIM
    tm = min(ROW_TILE, S)
    last = S // tm - 1
    tri = (lax.broadcasted_iota(jnp.int32, (tm, tm), 1) >= lax.broadcasted_iota(jnp.int32, (tm, tm), 0)).astype(BF16)

    def body(dq_ref, dk_ref, dv_ref, q_ref, k_ref, dz_ref, f_ref, gq_ref, gk_ref, w_ref, wf_ref, x1_ref, g2_ref,
             dx2_ref, tri_ref, ones_ref, dproj_ref, dx1_ref, dg2_ref, small_ref, carry, pairs):
        @pl.when(pl.program_id(0) == 0)
        def _():
            dg2_ref[...] = jnp.zeros((1, D), F32)
            small_ref[...] = jnp.zeros((8, LANES), F32)
            carry[...] = jnp.zeros((8, LANES), F32)

        lane = _lane()

        def head_pair(j, acc):
            dcs, dgq, dgk = acc
            dq2, dk2 = [], []
            q_pair, k_pair = q_ref[j], k_ref[j]
            for parity in (0, 1):
                hd = 2 * j + parity
                own, a = _own(lane, parity), _aug(parity)
                dqf = dq_ref[hd]
                dqn = jnp.where(own, dqf * Q_SCALE, 0.0)
                d, dg = _head_rms_bwd(dqn, jnp.where(own, q_pair, 0.0), gq_ref[...], ones_ref[...])
                dq2.append(d)
                dgq = dgq + dg
                dkt = dk_ref[hd]
                dcs = dcs + jnp.where(lane == hd, _col(dqf, lane, a) - _col(dkt, lane, a + 3), 0.0)
                d, dg = _head_rms_bwd(jnp.where(own, dkt, 0.0), jnp.where(own, k_pair, 0.0), gk_ref[...], ones_ref[...])
                dk2.append(d)
                dgk = dgk + dg
            pairs[0, j] = _pair_tile(*dq2, lane).astype(BF16)
            pairs[1, j] = _pair_tile(*dk2, lane).astype(BF16)
            pairs[2, j] = _pair_tile(dv_ref[2 * j], dv_ref[2 * j + 1], lane)
            return dcs, dgq, dgk

        zero = jnp.zeros((1, LANES), F32)
        dcs, dgq, dgk = lax.fori_loop(0, H // 2, head_pair, (jnp.zeros((tm, LANES), F32), zero, zero))
        for part in range(3):
            for j in range(H // 2):
                dproj_ref[:, part * D + LANES * j:part * D + LANES * (j + 1)] = pairs[part, j]
        dproj_ref[:, 3 * D:4 * D] = dz_ref[...]
        dlogf = _dot01(tri_ref[...], dcs) + carry[0:1, :]
        carry[...] = jnp.broadcast_to(dlogf[0:1, :], (8, LANES))
        df = dlogf * (1.0 / (1.0 + jnp.exp(f_ref[...])))
        dproj_ref[:, 4 * D:4 * D + LANES] = df.astype(BF16)
        small_ref[0:1, :] += jnp.sum(df, axis=0, keepdims=True)
        small_ref[1:2, :] += dgq
        small_ref[2:3, :] += dgk
        dh = _dot_nt(dproj_ref[:, 0:4 * D], w_ref[...]) + _dot_nt(dproj_ref[:, 4 * D:4 * D + LANES], wf_ref[...])
        dxn, dg = _rms_bwd(dh, x1_ref[...], g2_ref[...])
        dx1_ref[...] = dx2_ref[...] + dxn
        dg2_ref[...] += dg

    W = 4 * D + LANES
    heads = pl.BlockSpec((H, tm, LANES), lambda i: (0, last - i, 0))
    head_pairs = pl.BlockSpec((H // 2, tm, LANES), lambda i: (0, last - i, 0))
    return pl.pallas_call(
        body, name="attn_proj_bwd", grid=(S // tm,),
        in_specs=[heads, heads, heads, head_pairs, head_pairs,
                  _rows(tm, D, last), _rows(tm, LANES, last), _whole((1, LANES)), _whole((1, LANES)),
                  _whole((D, 4 * D)), _whole((D, LANES)), _rows(tm, D, last), _whole((1, D)), _rows(tm, D, last),
                  _whole((tm, tm)), _whole((LANES, LANES))],
        out_specs=[_rows(tm, W, last), _rows(tm, D, last), _whole((1, D)), _whole((8, LANES))],
        out_shape=[jax.ShapeDtypeStruct((S, W), BF16), jax.ShapeDtypeStruct((S, D), F32),
                   jax.ShapeDtypeStruct((1, D), F32), jax.ShapeDtypeStruct((8, LANES), F32)],
        scratch_shapes=[pltpu.VMEM((8, LANES), F32), pltpu.VMEM((3, H // 2, tm, LANES), BF16)],
        compiler_params=_params(1),
    )(dqt, dka, dva, qraw, kraw, dz, f, gq, gk, w, wf, x1, g2, dx2, tri, jnp.ones((LANES, LANES), BF16))


def _matmul_tn(a, b, col0, n, tn, name, stacked=False):
    S, M = a.shape
    ts = min(TN_ROWS, S)
    off = col0 // tn

    def body(a_ref, b_ref, o_ref):
        @pl.when(pl.program_id(1) == 0)
        def _():
            o_ref[...] = jnp.zeros((M, tn), F32)

        o_ref[...] += _dot_tn(a_ref[...], b_ref[...])

    if stacked:
        out_spec, out_shape = pl.BlockSpec((None, M, tn), lambda j, s: (j, 0, 0)), (n // tn, M, tn)
    else:
        out_spec, out_shape = pl.BlockSpec((M, tn), lambda j, s: (0, j)), (M, n)
    return pl.pallas_call(
        body, name=name, grid=(n // tn, S // ts),
        in_specs=[pl.BlockSpec((ts, M), lambda j, s: (s, 0)), pl.BlockSpec((ts, tn), lambda j, s: (s, off + j))],
        out_specs=out_spec, out_shape=jax.ShapeDtypeStruct(out_shape, F32),
        compiler_params=_params(2),
    )(a, b)


def _adamw(w, g, m, v, name):
    r, c = w.shape
    tr = ROW_TILE if r % ROW_TILE == 0 else r

    def body(w_ref, g_ref, m_ref, v_ref, d_ref, m2_ref, v2_ref):
        gv = g_ref[...]
        m2 = ADAM_B1 * m_ref[...] + (1.0 - ADAM_B1) * gv
        v2 = ADAM_B2 * v_ref[...] + (1.0 - ADAM_B2) * (gv * gv)
        m2_ref[...] = m2
        v2_ref[...] = v2
        m_hat = m2 / (1.0 - ADAM_B1 ** ADAM_STEP)
        v_hat = v2 / (1.0 - ADAM_B2 ** ADAM_STEP)
        d_ref[...] = -ADAM_LR * (m_hat / (jnp.sqrt(v_hat) + ADAM_EPS) + ADAM_WD * w_ref[...])

    spec = _rows(tr, c)
    return pl.pallas_call(
        body, name=name, grid=(r // tr,), in_specs=[spec] * 4, out_specs=[spec] * 3,
        out_shape=[jax.ShapeDtypeStruct((r, c), F32)] * 3, compiler_params=_params(1),
    )(w, g, m, v)


def _local_step(x, target, g1, w_in, conv_w, w_out, g2, wa_in, b_f, gq, gk, wa_out):
    S, D = x.shape
    H = D // HEAD_DIM
    w_qkvz = wa_in[:, :4 * D]
    wf = jnp.pad(wa_in[:, 4 * D:], ((0, 0), (0, LANES - H)))
    bf = jnp.pad(b_f, ((0, 0), (0, LANES - H)))
    gq128 = jnp.concatenate([gq, gq], axis=1)
    gk128 = jnp.concatenate([gk, gk], axis=1)

    proj, h1, yc, y, x1 = _conv_fwd(x, g1, w_in, conv_w, w_out)
    h2, qraw, kraw, z, f, c, rel, qa, ka, va, vt = _attn_front(x1, g2, w_qkvz, wf, bf, gq128, gk128)
    T = vt.shape[3]
    kstart, qend, bound = _skip_tables(c[:, :H], (c - rel)[:, :H], gq, gk, T, min(ATT_GROUP, S // T))
    o_aug, lse = lax.cond(2.0 * bound <= PLAIN_EXP_MAX, functools.partial(_attn_fwd, online_max=False),
                          functools.partial(_attn_fwd, online_max=True), kstart, qa, ka, vt)
    dx2, dx2b, o2b, dz, doa, qa2, loss = _attn_out(o_aug, lse.reshape(H, S), rel, z, x1, target, wa_out, qa)
    dqt, dka, dva = _attn_bwd(qend, qa2, doa, ka, va, T)
    dproj2, dx1, dg2, small = _attn_proj_bwd(dqt, dka, dva, qraw, kraw, dz, f, gq128, gk128, w_qkvz, wf, x1, g2, dx2)
    dproj1, dx, dx1b, dg1, dcw = _conv_bwd(dx1, x, g1, w_in, w_out, conv_w, proj, yc)

    tn = min(1024, D)
    dwa_out = _matmul_tn(o2b, dx2b, 0, D, tn, "dw_attn_out")
    dwa_in = jnp.concatenate([_matmul_tn(h2, dproj2, 0, 4 * D, tn, "dw_attn_in"),
                              _matmul_tn(h2, dproj2, 4 * D, LANES, LANES, "dw_attn_f")[:, :H]], axis=1)
    dw_out = _matmul_tn(y, dx1b, 0, D, tn, "dw_conv_out")
    dw_in = _matmul_tn(h1, dproj1, 0, 4 * D, D, "dw_conv_in", stacked=True)
    grads = dict(conv_norm_g=dg1, conv_w_in=dw_in, conv_w=dcw, conv_w_out=dw_out, attn_norm_g=dg2,
                 attn_w_in=dwa_in, attn_b_f=small[0:1, :H],
                 attn_q_norm_g=small[1:2, :HEAD_DIM] + small[1:2, HEAD_DIM:],
                 attn_k_norm_g=small[2:3, :HEAD_DIM] + small[2:3, HEAD_DIM:], attn_w_out=dwa_out)
    return loss[0, 0], dx, grads


def _coords():
    return lax.axis_index("x"), lax.axis_index("y"), lax.axis_index("c")


def _at(ref, idx):
    return ref.at[idx] if idx else ref


def _other_chips(x, y):
    return [(1 - x, y), (x, 1 - y), (1 - x, 1 - y)]


def _all_gather(halved, whole):
    nh, nw = len(halved), len(whole)

    def body(*refs):
        src_h, src_w = refs[:nh], refs[nh:nh + nw]
        out_h, out_w = refs[nh + nw:2 * nh + nw], refs[2 * nh + nw:2 * (nh + nw)]
        send_h, recv_h, send_w, recv_w = refs[2 * (nh + nw):]
        x, y, c = _coords()
        mine = 2 * x + y
        sibling = (x, y, 1 - c)
        chips = _other_chips(x, y)

        def copy_h(a, k, chip, half, to, src=None):
            dst = out_h[a].at[chip, half]
            return pltpu.make_async_remote_copy(src_ref=dst if src is None else src, dst_ref=dst,
                                                send_sem=send_h.at[a, k], recv_sem=recv_h.at[a, k],
                                                device_id=to, device_id_type=MESH)

        def copy_w(a, k, chip, to):
            return pltpu.make_async_remote_copy(src_ref=src_w[a], dst_ref=out_w[a].at[chip],
                                                send_sem=send_w.at[a, k], recv_sem=recv_w.at[a, k],
                                                device_id=to, device_id_type=MESH)

        first = [copy_h(a, k, mine, c, (*chip, c), src=src_h[a].at[c]) for a in range(nh) for k, chip in enumerate(chips)]
        first += [copy_w(a, k, mine, (*chip, c)) for a in range(nw) for k, chip in enumerate(chips)]
        own = [pltpu.make_async_remote_copy(src_ref=src_h[a], dst_ref=out_h[a].at[mine], send_sem=send_h.at[a, 6],
                                            recv_sem=recv_h.at[a, 6], device_id=sibling, device_id_type=MESH)
               for a in range(nh)]
        own += [copy_w(a, 3, mine, sibling) for a in range(nw)]
        for cp in first + own:
            cp.start()
        passed = []
        for a in range(nh):
            for k, (px, py) in enumerate(chips):
                copy_h(a, k, 2 * px + py, c, (x, y, c)).wait_recv()
                cp = copy_h(a, 3 + k, 2 * px + py, c, sibling)
                cp.start()
                passed.append(cp)
        for a in range(nh):
            for k, (px, py) in enumerate(chips):
                copy_h(a, 3 + k, 2 * px + py, 1 - c, (x, y, c)).wait_recv()
        for a in range(nw):
            for k, (px, py) in enumerate(chips):
                copy_w(a, k, 2 * px + py, (x, y, c)).wait_recv()
        for cp in own:
            cp.wait_recv()
        for cp in first + passed + own:
            cp.wait_send()

    out_shape = [jax.ShapeDtypeStruct((4,) + a.shape, a.dtype) for a in list(halved) + list(whole)]
    return pl.pallas_call(
        body, name="gather_weights", in_specs=[ANY] * (nh + nw), out_specs=[ANY] * (nh + nw), out_shape=out_shape,
        scratch_shapes=[pltpu.SemaphoreType.DMA((nh, 7)), pltpu.SemaphoreType.DMA((nh, 7)),
                        pltpu.SemaphoreType.DMA((nw, 4)), pltpu.SemaphoreType.DMA((nw, 4))],
    )(*halved, *whole)


def _exchange(name, srcs, lands, copies, local_copies):
    ns, nl, n, nloc = len(srcs), len(lands), len(copies), len(local_copies)

    def body(*refs):
        src, land = refs[:ns], refs[ns:ns + nl]
        send, recv, local_sem = refs[ns + nl:]
        me = _coords()
        started = []
        for k, (si, s_at, li, l_at, ci) in enumerate(local_copies):
            cp = pltpu.make_async_copy(_at(src[si], s_at(*me)), _at(land[li], l_at(*me)), local_sem.at[k])
            cp.start()
            started.append(cp)
        remote = []
        for k, (si, s_at, li, l_at, peer) in enumerate(copies):
            cp = pltpu.make_async_remote_copy(src_ref=_at(src[si], s_at(*me)), dst_ref=_at(land[li], l_at(*me)),
                                              send_sem=send.at[k], recv_sem=recv.at[k],
                                              device_id=peer(*me), device_id_type=MESH)
            cp.start()
            remote.append(cp)
        for cp in remote:
            cp.wait()
        for cp in started:
            cp.wait()

    return pl.pallas_call(
        body, name=name, in_specs=[ANY] * ns, out_specs=[ANY] * nl, out_shape=list(lands),
        scratch_shapes=[pltpu.SemaphoreType.DMA((n,)), pltpu.SemaphoreType.DMA((n,)),
                        pltpu.SemaphoreType.DMA((max(nloc, 1),))],
    )(*srcs)


def _add_pairs(a, b, name):
    _, r, cols = a.shape
    tr = ROW_TILE if r % ROW_TILE == 0 else r

    def body(a_ref, b_ref, o_ref, ob_ref):
        s = a_ref[...] + b_ref[...]
        o_ref[...] = s
        ob_ref[...] = s.astype(BF16)

    spec = pl.BlockSpec((None, tr, cols), lambda j, i: (j, i, 0))
    return pl.pallas_call(
        body, name=name, grid=(4, r // tr), in_specs=[spec, spec], out_specs=[spec, spec],
        out_shape=[jax.ShapeDtypeStruct(a.shape, F32), jax.ShapeDtypeStruct(a.shape, BF16)],
        compiler_params=_params(2),
    )(a, b)


def _sum_chips(own, landed, name):
    _, r, cols = landed.shape
    tr = ROW_TILE if r % ROW_TILE == 0 else r

    def body(own_ref, land_ref, o_ref):
        acc = own_ref[...]
        for j in range(3):
            acc = acc + land_ref[j].astype(F32)
        o_ref[...] = acc

    return pl.pallas_call(
        body, name=name, grid=(r // tr,),
        in_specs=[_rows(tr, cols), pl.BlockSpec((3, tr, cols), lambda i: (0, i, 0))], out_specs=_rows(tr, cols),
        out_shape=jax.ShapeDtypeStruct((r, cols), F32), compiler_params=_params(1),
    )(own, landed)


def _sum_devices(landed, name):
    def body(l_ref, o_ref):
        acc = l_ref[0]
        for j in range(1, 8):
            acc = acc + l_ref[j]
        o_ref[...] = acc

    return pl.pallas_call(body, name=name, out_shape=jax.ShapeDtypeStruct(landed.shape[1:], F32))(landed)


def _reduce_gradients(big, small):
    nb = len(big)
    x, y, c = _coords()
    mine = 2 * x + y
    flips = [(fx, fy, fc) for fx in (0, 1) for fy in (0, 1) for fc in (0, 1) if fx or fy or fc]

    def flip(fx, fy, fc):
        return lambda x, y, c: (x ^ fx, y ^ fy, c ^ fc)

    copies = [(a, (lambda j: lambda x, y, c: (j, 1 - c))(j), a, (lambda j: lambda x, y, c: (j,))(j), flip(0, 0, 1))
              for a in range(nb) for j in range(4)]
    copies += [(nb, lambda x, y, c: (), nb, lambda x, y, c: (4 * x + 2 * y + c,), flip(*f)) for f in flips]
    lands = [jax.ShapeDtypeStruct((4,) + g.shape[2:], F32) for g in big] + [jax.ShapeDtypeStruct((8,) + small.shape, F32)]
    local = [(nb, lambda x, y, c: (), nb, lambda x, y, c: (4 * x + 2 * y + c,), None)]
    landed = _exchange("swap_halves", list(big) + [small], lands, copies, local)
    small_sum = _sum_devices(landed[nb], "sum_small")

    chip_f32, chip_bf16 = [], []
    for a in range(nb):
        kept = lax.dynamic_index_in_dim(big[a], c, axis=1, keepdims=False)
        s, sb = _add_pairs(kept, landed[a], f"add_cores_{a}")
        chip_f32.append(s)
        chip_bf16.append(sb)

    chip_flips = [(1, 0), (0, 1), (1, 1)]
    copies = [(a, (lambda f: lambda x, y, c: (2 * (x ^ f[0]) + (y ^ f[1]),))(f), a, (lambda k: lambda x, y, c: (k,))(k),
               flip(f[0], f[1], 0)) for a in range(nb) for k, f in enumerate(chip_flips)]
    lands = [jax.ShapeDtypeStruct((3,) + g.shape[1:], BF16) for g in chip_bf16]
    landed = _exchange("send_chip_sums", chip_bf16, lands, copies, [])
    totals = [_sum_chips(lax.dynamic_index_in_dim(chip_f32[a], mine, axis=0, keepdims=False), landed[a],
                         f"sum_chips_{a}") for a in range(nb)]

    copies = [(a, lambda x, y, c: (), a, lambda x, y, c: (), flip(0, 0, 1)) for a in range(nb)]
    lands = [jax.ShapeDtypeStruct(t.shape, F32) for t in totals]
    landed = _exchange("swap_sums", totals, lands, copies, [])
    return [jnp.stack([jnp.where(c == 0, t, l), jnp.where(c == 0, l, t)]) for t, l in zip(totals, landed)], small_sum


def kernel(x, conv_norm_g, conv_w_in, conv_w, conv_w_out, attn_norm_g, attn_w_in, attn_b_f, attn_q_norm_g, attn_k_norm_g, attn_w_out, loss_target, m_conv_norm_g, m_conv_w_in, m_conv_w, m_conv_w_out, m_attn_norm_g, m_attn_w_in, m_attn_b_f, m_attn_q_norm_g, m_attn_k_norm_g, m_attn_w_out, v_conv_norm_g, v_conv_w_in, v_conv_w, v_conv_w_out, v_attn_norm_g, v_attn_w_in, v_attn_b_f, v_attn_q_norm_g, v_attn_k_norm_g, v_attn_w_out):
    xi, yi, _ = _coords()
    chip = 2 * xi + yi
    D = x.shape[2]
    H = D // HEAD_DIM
    names = ["conv_norm_g", "conv_w_in", "conv_w", "conv_w_out", "attn_norm_g", "attn_w_in", "attn_b_f",
             "attn_q_norm_g", "attn_k_norm_g", "attn_w_out"]
    weights = dict(zip(names, [conv_norm_g, conv_w_in, conv_w, conv_w_out, attn_norm_g, attn_w_in, attn_b_f,
                               attn_q_norm_g, attn_k_norm_g, attn_w_out]))
    m_in = dict(zip(names, [m_conv_norm_g, m_conv_w_in, m_conv_w, m_conv_w_out, m_attn_norm_g, m_attn_w_in,
                            m_attn_b_f, m_attn_q_norm_g, m_attn_k_norm_g, m_attn_w_out]))
    v_in = dict(zip(names, [v_conv_norm_g, v_conv_w_in, v_conv_w, v_conv_w_out, v_attn_norm_g, v_attn_w_in,
                            v_attn_b_f, v_attn_q_norm_g, v_attn_k_norm_g, v_attn_w_out]))
    weights = {k: w[0] for k, w in weights.items()}
    m_in = {k: w[0] for k, w in m_in.items()}
    v_in = {k: w[0] for k, w in v_in.items()}

    big_names = ["conv_w_in", "attn_w_in", "conv_w_out", "attn_w_out"]
    halved = [weights[k].astype(BF16).reshape(2, weights[k].shape[0] // 2, weights[k].shape[1]) for k in big_names]
    q = D // 4
    small_w = jnp.concatenate([weights["conv_w"], weights["attn_norm_g"][None, :], jnp.zeros((4, q), F32)], axis=0)
    g_in, ga_in, g_out, ga_out, g_small = _all_gather(halved, [small_w])
    w_in = g_in.reshape(4, D, D)
    wa_in = ga_in.reshape(4, D, D + H // 4).transpose(1, 0, 2).reshape(D, 4 * D + H)
    w_out = g_out.reshape(D, D)
    wa_out = ga_out.reshape(D, D)
    conv_w_full = g_small[:, 0:3, :].transpose(1, 0, 2).reshape(3, D)
    attn_g_full = g_small[:, 3, :].reshape(1, D)

    loss_part, grad_x, grads = _local_step(x[0], loss_target[0], weights["conv_norm_g"][None, :], w_in, conv_w_full,
                                           w_out, attn_g_full, wa_in, weights["attn_b_f"][None, :],
                                           weights["attn_q_norm_g"][None, :], weights["attn_k_norm_g"][None, :], wa_out)

    big = [grads["conv_w_in"].reshape(4, 2, D // 2, D),
           grads["attn_w_in"].reshape(D, 4, D + H // 4).transpose(1, 0, 2).reshape(4, 2, D // 2, D + H // 4),
           grads["conv_w_out"].reshape(4, 2, D // 8, D), grads["attn_w_out"].reshape(4, 2, D // 8, D)]
    tail = jnp.concatenate([grads["attn_b_f"], grads["attn_q_norm_g"], grads["attn_k_norm_g"],
                            jnp.reshape(loss_part, (1, 1)), jnp.zeros((1, D - H - 2 * HEAD_DIM - 1), F32)], axis=1)
    small = jnp.concatenate([grads["conv_norm_g"], grads["conv_w"], grads["attn_norm_g"], tail,
                             jnp.zeros((2, D), F32)], axis=0)
    reduced, small_sum = _reduce_gradients(big, small)
    final = {k: r.reshape(weights[k].shape) for k, r in zip(big_names, reduced)}
    final["conv_norm_g"] = small_sum[0]
    final["conv_w"] = lax.dynamic_slice_in_dim(small_sum[1:4], chip * q, q, axis=1)
    final["attn_norm_g"] = lax.dynamic_slice_in_dim(small_sum[4], chip * q, q, axis=0)
    final["attn_b_f"] = small_sum[5, :H]
    final["attn_q_norm_g"] = small_sum[5, H:H + HEAD_DIM]
    final["attn_k_norm_g"] = small_sum[5, H + HEAD_DIM:H + 2 * HEAD_DIM]
    loss = small_sum[5, H + 2 * HEAD_DIM]

    delta, new_m, new_v = {}, {}, {}
    for k in names:
        shape = weights[k].shape
        as2d = (lambda a: a.reshape(1, -1)) if len(shape) == 1 else (lambda a: a)
        d, m2, v2 = _adamw(as2d(weights[k]), as2d(final[k]), as2d(m_in[k]), as2d(v_in[k]), "adamw_" + k)
        delta[k], new_m[k], new_v[k] = d.reshape(shape), m2.reshape(shape), v2.reshape(shape)
    lead = lambda a: a[None]
    return (loss, grad_x[None], *[lead(final[k]) for k in names], *[lead(delta[k]) for k in names],
            *[lead(new_m[k]) for k in names], *[lead(new_v[k]) for k in names])
```

```python
import functools

import jax
import jax.numpy as jnp
from jax import lax
from jax.experimental import pallas as pl
from jax.experimental.pallas import tpu as pltpu

F32 = jnp.float32
BF16 = jnp.bfloat16
HEAD_DIM = 64
LANES = 128
RMS_EPS = 1e-6
NEG = -1e30
Q_SCALE = 0.125
ROW_TILE = 256
CONV_TILE = 512
ATT_GROUP = 4
SKIP_LOG = 106.0
PLAIN_EXP_MAX = 60.0
TN_ROWS = 2048
VMEM_LIMIT = 56 << 20
ADAM_LR, ADAM_B1, ADAM_B2, ADAM_EPS, ADAM_WD, ADAM_STEP = 0.001, 0.9, 0.999, 1e-08, 0.01, 10
MESH = pl.DeviceIdType.MESH
ANY = pl.BlockSpec(memory_space=pl.ANY)


def _lane():
    return lax.broadcasted_iota(jnp.int32, (1, LANES), 1)


def _split3(x):
    hi = x.astype(BF16).astype(F32)
    r = x - hi
    mid = r.astype(BF16).astype(F32)
    lo = (r - mid).astype(BF16).astype(F32)
    return hi, mid, lo


STAT_STRIDE = 16
ONE_LANE = 3 * STAT_STRIDE


def _pack3(x, lane, one):
    hi, mid, lo = _split3(x)
    packed = hi + pltpu.roll(mid, STAT_STRIDE, 1) + pltpu.roll(lo, 2 * STAT_STRIDE, 1)
    return jnp.where(lane == ONE_LANE, one, packed).astype(BF16)


def _scatter_matrices(H):
    rows = lax.broadcasted_iota(jnp.int32, (LANES, H * LANES), 0)
    cols = lax.broadcasted_iota(jnp.int32, (LANES, H * LANES), 1)
    head, within = cols // LANES, cols % LANES
    extra = within - HEAD_DIM * (1 - head % 2)
    term = (rows < ONE_LANE) & (rows % STAT_STRIDE == head)
    first = ((term & (extra == rows // STAT_STRIDE)) | ((rows == ONE_LANE) & (extra >= 3) & (extra < 6)))
    second = ((term & (extra - 3 == rows // STAT_STRIDE)) | ((rows == ONE_LANE) & (extra >= 0) & (extra < 3)))
    return first.astype(BF16), second.astype(BF16)


def _gather_matrix(H, lo):
    rows = lax.broadcasted_iota(jnp.int32, (H * LANES, LANES), 0)
    cols = lax.broadcasted_iota(jnp.int32, (H * LANES, LANES), 1)
    extra = rows % LANES - HEAD_DIM * (1 - (rows // LANES) % 2)
    return ((rows // LANES == cols) & (extra >= lo) & (extra < lo + 3)).astype(BF16)


def _put(base, lane, start, parts):
    for j, p in enumerate(parts):
        base = jnp.where(lane == start + j, p, base)
    return base


def _col(x, lane, idx):
    return jnp.sum(jnp.where(lane == idx, x, 0.0), axis=1, keepdims=True)


def _feat(parity):
    return HEAD_DIM * parity


def _aug(parity):
    return HEAD_DIM * (1 - parity)


def _own(lane, parity):
    return (lane >= _feat(parity)) & (lane < _feat(parity) + HEAD_DIM)


def _head_tile(ref, hd, lane):
    j = hd // 2
    return jnp.where(_own(lane, hd % 2), ref[:, LANES * j:LANES * (j + 1)], 0.0)


def _pair_tile(even, odd, lane):
    return jnp.where(lane < HEAD_DIM, even, odd)


def _sigmoid(x):
    return 0.5 * jnp.tanh(0.5 * x) + 0.5


def _dot(a, b):
    return jnp.dot(a, b, preferred_element_type=F32)


def _dot_nt(a, b):
    return lax.dot_general(a, b, (((1,), (1,)), ((), ())), preferred_element_type=F32)


def _dot_tn(a, b):
    return lax.dot_general(a, b, (((0,), (0,)), ((), ())), preferred_element_type=F32)


def _dot01(tri, x):
    hi, mid, lo = _split3(x)
    return _dot(tri, hi.astype(BF16)) + _dot(tri, mid.astype(BF16)) + _dot(tri, lo.astype(BF16))


def _rms_bwd(dh, x, g):
    inv = lax.rsqrt(jnp.mean(x * x, axis=-1, keepdims=True) + RMS_EPS)
    xh = x * inv
    dxn = dh * g
    dx = inv * (dxn - xh * jnp.mean(dxn * xh, axis=-1, keepdims=True))
    return dx, jnp.sum(dh * xh, axis=0, keepdims=True)


def _head_rms_bwd(dn, t, g, ones):
    sq = t * t
    hi = sq.astype(BF16)
    lo = (sq - hi.astype(F32)).astype(BF16)
    inv = lax.rsqrt((_dot(hi, ones) + _dot(lo, ones)) * (1.0 / HEAD_DIM) + RMS_EPS)
    th = t * inv
    gd = dn * g
    d = inv * (gd - th * (jnp.sum(gd * th, axis=1, keepdims=True) * (1.0 / HEAD_DIM)))
    return d, jnp.sum(dn * th, axis=0, keepdims=True)


def _params(n_grid):
    return pltpu.CompilerParams(dimension_semantics=("arbitrary",) * n_grid, vmem_limit_bytes=VMEM_LIMIT)


def _rows(tm, cols, rev=None):
    if rev is None:
        return pl.BlockSpec((tm, cols), lambda i: (i, 0))
    return pl.BlockSpec((tm, cols), lambda i: (rev - i, 0))


def _whole(shape, buffers=None):
    mode = {} if buffers is None else dict(pipeline_mode=pl.Buffered(buffers))
    return pl.BlockSpec(shape, lambda *_: (0,) * len(shape), **mode)


def _conv_fwd(x, g1, w_in, conv_w, w_out):
    S, D = x.shape
    tm = min(CONV_TILE, S)
    sub = min(ROW_TILE, tm)

    def body(x_ref, g_ref, win_ref, cw_ref, wout_ref, proj_ref, h_ref, yc_ref, y_ref, x1_ref, prev_u):
        @pl.when(pl.program_id(0) == 0)
        def _():
            prev_u[...] = jnp.zeros((sub, D), F32)

        for r in range(0, tm, sub):
            rows = slice(r, r + sub)
            xv = x_ref[rows, :]
            inv = lax.rsqrt(jnp.mean(xv * xv, axis=-1, keepdims=True) + RMS_EPS)
            h = (xv * inv * g_ref[...]).astype(BF16)
            h_ref[rows, :] = h
            for j in range(4):
                proj_ref[rows, j * D:(j + 1) * D] = _dot(h, win_ref[j])
            u = proj_ref[rows, D:2 * D] * proj_ref[rows, 2 * D:3 * D]
            pu = prev_u[...]
            row = lax.broadcasted_iota(jnp.int32, (sub, 1), 0)
            u1 = jnp.where(row < 1, pltpu.roll(pu, 1, 0), pltpu.roll(u, 1, 0))
            u2 = jnp.where(row < 2, pltpu.roll(pu, 2, 0), pltpu.roll(u, 2, 0))
            prev_u[...] = u
            w = cw_ref[...]
            yc = w[2:3] * u + w[1:2] * u1 + w[0:1] * u2
            yc_ref[rows, :] = yc
            z = proj_ref[rows, 3 * D:4 * D]
            y = (proj_ref[rows, 0:D] * yc * (z * _sigmoid(z))).astype(BF16)
            y_ref[rows, :] = y
            x1_ref[rows, :] = xv + _dot(y, wout_ref[...])

    return pl.pallas_call(
        body, name="conv_fwd", grid=(S // tm,),
        in_specs=[_rows(tm, D), _whole((1, D)), _whole((4, D, D), 1), _whole((3, D)), _whole((D, D), 1)],
        out_specs=[_rows(tm, 4 * D), _rows(tm, D), _rows(tm, D), _rows(tm, D), _rows(tm, D)],
        out_shape=[jax.ShapeDtypeStruct((S, 4 * D), F32), jax.ShapeDtypeStruct((S, D), BF16),
                   jax.ShapeDtypeStruct((S, D), F32), jax.ShapeDtypeStruct((S, D), BF16),
                   jax.ShapeDtypeStruct((S, D), F32)],
        scratch_shapes=[pltpu.VMEM((sub, D), F32)],
        compiler_params=_params(1),
    )(x, g1, w_in, conv_w, w_out)


def _conv_bwd(dx1, x, g1, w_in, w_out, conv_w, proj, yc):
    S, D = x.shape
    tm = min(ROW_TILE, S)
    sub = min(ROW_TILE, tm)
    last = S // tm - 1

    def body(dx1_ref, x_ref, g_ref, win_ref, wout_ref, cw_ref, proj_ref, yc_ref,
             dproj_ref, dx_ref, dx1b_ref, dg_ref, dcw_ref, next_d):
        @pl.when(pl.program_id(0) == 0)
        def _():
            dg_ref[...] = jnp.zeros((1, D), F32)
            dcw_ref[...] = jnp.zeros((3, D), F32)
            next_d[...] = jnp.zeros((sub, D), F32)

        for r in range(tm - sub, -1, -sub):
            rows = slice(r, r + sub)
            dx1v = dx1_ref[rows, :]
            dx1b = dx1v.astype(BF16)
            dx1b_ref[rows, :] = dx1b
            dy = _dot_nt(dx1b, wout_ref[...])
            b = proj_ref[rows, 0:D]
            c = proj_ref[rows, D:2 * D]
            xin = proj_ref[rows, 2 * D:3 * D]
            z = proj_ref[rows, 3 * D:4 * D]
            sg = _sigmoid(z)
            sz = z * sg
            ycv = yc_ref[rows, :]
            d0 = dy * b * sz
            dproj_ref[rows, 0:D] = (dy * ycv * sz).astype(BF16)
            dproj_ref[rows, 3 * D:4 * D] = (dy * b * ycv * (sg * (1.0 + z * (1.0 - sg)))).astype(BF16)
            nd = next_d[...]
            row = lax.broadcasted_iota(jnp.int32, (sub, 1), 0)
            d1 = jnp.where(row >= sub - 1, pltpu.roll(nd, sub - 1, 0), pltpu.roll(d0, sub - 1, 0))
            d2 = jnp.where(row >= sub - 2, pltpu.roll(nd, sub - 2, 0), pltpu.roll(d0, sub - 2, 0))
            next_d[...] = d0
            w = cw_ref[...]
            du = w[2:3] * d0 + w[1:2] * d1 + w[0:1] * d2
            u = c * xin
            dcw_ref[2:3, :] += jnp.sum(d0 * u, axis=0, keepdims=True)
            dcw_ref[1:2, :] += jnp.sum(d1 * u, axis=0, keepdims=True)
            dcw_ref[0:1, :] += jnp.sum(d2 * u, axis=0, keepdims=True)
            dproj_ref[rows, D:2 * D] = (du * xin).astype(BF16)
            dproj_ref[rows, 2 * D:3 * D] = (du * c).astype(BF16)
            dh = _dot_nt(dproj_ref[rows, 0:D], win_ref[0])
            for j in range(1, 4):
                dh = dh + _dot_nt(dproj_ref[rows, j * D:(j + 1) * D], win_ref[j])
            dxn, dg = _rms_bwd(dh, x_ref[rows, :], g_ref[...])
            dx_ref[rows, :] = dx1v + dxn
            dg_ref[...] += dg

    return pl.pallas_call(
        body, name="conv_bwd", grid=(S // tm,),
        in_specs=[_rows(tm, D, last), _rows(tm, D, last), _whole((1, D)), _whole((4, D, D), 1), _whole((D, D), 1),
                  _whole((3, D)), _rows(tm, 4 * D, last), _rows(tm, D, last)],
        out_specs=[_rows(tm, 4 * D, last), _rows(tm, D, last), _rows(tm, D, last), _whole((1, D)), _whole((3, D))],
        out_shape=[jax.ShapeDtypeStruct((S, 4 * D), BF16), jax.ShapeDtypeStruct((S, D), F32),
                   jax.ShapeDtypeStruct((S, D), BF16), jax.ShapeDtypeStruct((1, D), F32),
                   jax.ShapeDtypeStruct((3, D), F32)],
        scratch_shapes=[pltpu.VMEM((sub, D), F32)],
        compiler_params=_params(1),
    )(dx1, x, g1, w_in, w_out, conv_w, proj, yc)


def _attn_front(x1, g2, w, wf, bf, gq, gk):
    S, D = x1.shape
    H = D // HEAD_DIM
    tm = min(ROW_TILE, S)
    tri = (lax.broadcasted_iota(jnp.int32, (tm, tm), 1) <= lax.broadcasted_iota(jnp.int32, (tm, tm), 0)).astype(BF16)

    def body(x_ref, g_ref, w_ref, wf_ref, bf_ref, gq_ref, gk_ref, tri_ref, first_ref, second_ref,
             h_ref, qh_ref, kh_ref, z_ref, f_ref, c_ref, rel_ref, qa_ref, ka_ref, va_ref, vt_ref,
             carry, v_s, qraw_ref, kraw_ref):
        @pl.when(pl.program_id(0) == 0)
        def _():
            carry[...] = jnp.zeros((8, LANES), F32)

        xv = x_ref[...]
        inv = lax.rsqrt(jnp.mean(xv * xv, axis=-1, keepdims=True) + RMS_EPS)
        h = (xv * inv * g_ref[...]).astype(BF16)
        h_ref[...] = h
        qraw_ref[...] = _dot(h, w_ref[:, 0:D])
        kraw_ref[...] = _dot(h, w_ref[:, D:2 * D])
        v_s[...] = _dot(h, w_ref[:, 2 * D:3 * D])
        z_ref[...] = _dot(h, w_ref[:, 3 * D:4 * D])
        lane = _lane()
        f = _dot(h, wf_ref[...]) + bf_ref[...]
        f_ref[...] = f
        logf = jnp.where(lane < H, jnp.minimum(f, 0.0) - jnp.log(1.0 + jnp.exp(-jnp.abs(f))), 0.0)
        cs = _dot01(tri_ref[...], logf) + carry[0:1, :]
        c_ref[...] = cs
        carry[...] = jnp.broadcast_to(cs[tm - 1:tm, :], (8, LANES))
        diags = jnp.zeros((tm, LANES), F32)
        for hd in range(H):
            sl = slice(LANES * hd, LANES * (hd + 1))
            a = _aug(hd % 2)
            if hd % 2 == 0:
                qh_ref[hd // 2] = qraw_ref[:, LANES * (hd // 2):LANES * (hd // 2 + 1)]
                kh_ref[hd // 2] = kraw_ref[:, LANES * (hd // 2):LANES * (hd // 2 + 1)]
            qt = _head_tile(qraw_ref, hd, lane)
            qn = qt * lax.rsqrt(jnp.sum(qt * qt, axis=1, keepdims=True) * (1.0 / HEAD_DIM) + RMS_EPS) * gq_ref[...]
            kt = _head_tile(kraw_ref, hd, lane)
            kn = kt * lax.rsqrt(jnp.sum(kt * kt, axis=1, keepdims=True) * (1.0 / HEAD_DIM) + RMS_EPS) * gk_ref[...]
            diags = diags + jnp.where(lane == hd, jnp.sum(qn * kn, axis=1, keepdims=True) * Q_SCALE, 0.0)
            qa_ref[:, sl] = (qn * Q_SCALE).astype(BF16)
            ka_ref[:, sl] = kn.astype(BF16)
            va = jnp.where((lane >= a) & (lane < a + 3), 1.0, _head_tile(v_s, hd, lane))
            va_ref[:, sl] = va.astype(BF16)
            vt_ref[hd] = va.T.astype(BF16)
        rel = cs - diags
        rel_ref[...] = rel
        qa_ref[...] += _dot(_pack3(rel, lane, 1.0), first_ref[...]).astype(BF16)
        ka_ref[...] += _dot(_pack3(-cs, lane, 1.0), second_ref[...]).astype(BF16)

    nb = S // tm
    heads = pl.BlockSpec((H // 2, tm, LANES), lambda i: (0, i, 0))
    return pl.pallas_call(
        body, name="attn_front", grid=(nb,),
        in_specs=[_rows(tm, D), _whole((1, D)), _whole((D, 4 * D)), _whole((D, LANES)), _whole((1, LANES)),
                  _whole((1, LANES)), _whole((1, LANES)), _whole((tm, tm)), _whole((LANES, H * LANES)),
                  _whole((LANES, H * LANES))],
        out_specs=[_rows(tm, D), heads, heads, _rows(tm, D), _rows(tm, LANES), _rows(tm, LANES), _rows(tm, LANES),
                   _rows(tm, H * LANES), _rows(tm, H * LANES), _rows(tm, H * LANES),
                   pl.BlockSpec((H, None, LANES, tm), lambda i: (0, i, 0, 0))],
        out_shape=[jax.ShapeDtypeStruct((S, D), BF16), jax.ShapeDtypeStruct((H // 2, S, LANES), F32),
                   jax.ShapeDtypeStruct((H // 2, S, LANES), F32), jax.ShapeDtypeStruct((S, D), F32),
                   jax.ShapeDtypeStruct((S, LANES), F32), jax.ShapeDtypeStruct((S, LANES), F32),
                   jax.ShapeDtypeStruct((S, LANES), F32),
                   jax.ShapeDtypeStruct((S, H * LANES), BF16), jax.ShapeDtypeStruct((S, H * LANES), BF16),
                   jax.ShapeDtypeStruct((S, H * LANES), BF16), jax.ShapeDtypeStruct((H, nb, LANES, tm), BF16)],
        scratch_shapes=[pltpu.VMEM((8, LANES), F32), pltpu.VMEM((tm, D), F32), pltpu.VMEM((tm, D), F32),
                        pltpu.VMEM((tm, D), F32)],
        compiler_params=_params(1),
    )(x1, g2, w, wf, bf, gq, gk, tri, *_scatter_matrices(H))


def _skip_tables(c, diag, gq, gk, T, G):
    nb = c.shape[0] // T
    bound = 8.0 * jnp.max(jnp.abs(gq)) * jnp.max(jnp.abs(gk))
    first, last = c[0::T, :], c[T - 1::T, :]
    lowest = jnp.maximum(jnp.min(diag.reshape(nb, T, -1), axis=1), -bound)
    idx = jnp.arange(nb)
    margin = (SKIP_LOG + bound) - lowest
    need = (last[None, :, :] <= first[:, None, :] + margin[:, None, :]) & (idx[None, :, None] < idx[:, None, None])
    need = need | (idx[None, :, None] == idx[:, None, None])
    kstart = jnp.argmax(need, axis=1)
    qend = nb - 1 - jnp.argmax(need[::-1], axis=0)
    kstart = jnp.min(kstart.reshape(2 * nb // G, G // 2, -1), axis=1)
    kstart = kstart - (kstart & 1)
    qend = jnp.max(qend.reshape(2 * nb // G, G // 2, -1), axis=1)
    return kstart.T.astype(jnp.int32), qend.T.astype(jnp.int32), bound


def _attn_fwd(kstart, qa, ka, vt, online_max):
    S = qa.shape[0]
    H = qa.shape[1] // LANES
    nb, T = vt.shape[1], vt.shape[3]
    G = 2 * nb // kstart.shape[1]
    W = G * T

    def finish(acc, shift, o_ref, lse_ref):
        a = _aug(pl.program_id(0) % 2)
        feat = lax.broadcasted_iota(jnp.int32, (LANES, 1), 0)
        l = jnp.sum(jnp.where(feat == a, acc, 0.0), axis=0, keepdims=True)
        o_ref[...] = (acc * (1.0 / l)).T
        lse_ref[...] = shift + jnp.log(l)

    def causal(st):
        return jnp.where(lax.broadcasted_iota(jnp.int32, st.shape, 0) <= lax.broadcasted_iota(jnp.int32, st.shape, 1),
                         st, NEG)

    def fast_body(ks_ref, q_ref, k_ref, vt_ref, o_ref, lse_ref, acc_ref, sa_ref, sb_ref, sc_ref):
        h, g = pl.program_id(0), pl.program_id(1)
        q = q_ref[...]
        acc_ref[...] = jnp.zeros((LANES, W), F32)

        def scores(ki, lo):
            return _dot_nt(k_ref[pl.ds(pl.multiple_of(ki * T, T), 2 * T), :], q[lo * T:, :])

        def weighted(ki, p):
            return _dot(vt_ref[ki], p[:T]) + _dot(vt_ref[ki + 1], p[T:])

        first = ks_ref[h, 2 * g + 1]
        early = jnp.minimum(ks_ref[h, 2 * g], first)

        def narrow(i, carry):
            ki = early + 2 * i
            st = _dot_nt(k_ref[pl.ds(pl.multiple_of(ki * T, T), 2 * T), :], q[:W // 2, :])
            acc_ref[:, :W // 2] += weighted(ki, jnp.exp(st).astype(BF16))
            return carry

        lax.fori_loop(0, (first - early) // 2, narrow, 0)
        steps = (g * G - first) // 2
        sa_ref[...] = scores(first, 0)

        def advance(ki, cur_ref, next_ref):
            p = jnp.exp(cur_ref[...]).astype(BF16)
            next_ref[...] = scores(ki + 2, 0)
            acc_ref[...] += weighted(ki, p)

        def loop(i, carry):
            advance(first + 4 * i, sa_ref, sb_ref)
            advance(first + 4 * i + 2, sb_ref, sa_ref)
            return carry

        lax.fori_loop(0, steps // 2, loop, 0)

        def first_own(pending_ref):
            p = jnp.exp(causal(pending_ref[...])).astype(BF16)
            if G > 2:
                sc_ref[:, :W - 2 * T] = scores(g * G + 2, 2)
            acc_ref[...] += weighted(g * G, p)

        @pl.when(steps % 2 == 1)
        def _():
            advance(g * G - 2, sa_ref, sb_ref)
            first_own(sb_ref)

        @pl.when(steps % 2 == 0)
        def _():
            first_own(sa_ref)

        if G > 2:
            acc_ref[:, 2 * T:] += weighted(g * G + 2, jnp.exp(causal(sc_ref[:, :W - 2 * T])).astype(BF16))
        for j in range(4, G, 2):
            p = jnp.exp(causal(scores(g * G + j, j))).astype(BF16)
            acc_ref[:, j * T:] += weighted(g * G + j, p)
        finish(acc_ref[...], 0.0, o_ref, lse_ref)

    def online_body(ks_ref, q_ref, k_ref, vt_ref, o_ref, lse_ref, acc_ref, m_ref):
        h, g = pl.program_id(0), pl.program_id(1)
        q = q_ref[...]
        m_ref[...] = jnp.full((8, W), NEG, F32)
        acc_ref[...] = jnp.zeros((LANES, W), F32)

        def update(st, vtb, lo):
            m_old = m_ref[0:1, lo:]
            m_new = jnp.maximum(m_old, jnp.max(st, axis=0, keepdims=True))
            p = jnp.exp(st - m_new).astype(BF16)
            acc_ref[:, lo:] = acc_ref[:, lo:] * jnp.exp(m_old - m_new) + _dot(vtb, p)
            m_ref[:, lo:] = jnp.broadcast_to(m_new, (8, W - lo))

        def loop(ki, carry):
            kb = k_ref[pl.ds(pl.multiple_of(ki * T, T), T), :]
            update(_dot_nt(kb, q), vt_ref[ki], 0)
            return carry

        lax.fori_loop(jnp.minimum(ks_ref[h, 2 * g], ks_ref[h, 2 * g + 1]), g * G, loop, 0)
        for j in range(G):
            ki = g * G + j
            kb = k_ref[pl.ds(pl.multiple_of(ki * T, T), T), :]
            update(causal(_dot_nt(kb, q[j * T:, :])), vt_ref[ki], j * T)
        finish(acc_ref[...], m_ref[0:1, :], o_ref, lse_ref)

    return pl.pallas_call(
        online_body if online_max else fast_body, name="attn_fwd_online" if online_max else "attn_fwd",
        grid_spec=pltpu.PrefetchScalarGridSpec(
            num_scalar_prefetch=1, grid=(H, nb // G),
            in_specs=[pl.BlockSpec((W, LANES), lambda h, i, ks: (i, h)),
                      pl.BlockSpec((S, LANES), lambda h, i, ks: (0, h)),
                      pl.BlockSpec((None, nb, LANES, T), lambda h, i, ks: (h, 0, 0, 0))],
            out_specs=[pl.BlockSpec((W, LANES), lambda h, i, ks: (i, h)),
                       pl.BlockSpec((None, 1, W), lambda h, i, ks: (h, 0, i))],
            scratch_shapes=[pltpu.VMEM((LANES, W), F32)] + (
                [pltpu.VMEM((8, W), F32)] if online_max else [pltpu.VMEM((2 * T, W), F32)] * 3)),
        out_shape=[jax.ShapeDtypeStruct((S, H * LANES), F32), jax.ShapeDtypeStruct((H, 1, S), F32)],
        compiler_params=_params(2),
    )(kstart, qa, ka, vt)


def _attn_out(o_aug, lse, rel, z, x1, target, w_out, qa):
    S, D = x1.shape
    H = D // HEAD_DIM
    tm = min(ROW_TILE, S)

    def body(o_ref, z_ref, x1_ref, t_ref, w_ref, q_ref, first_ref, rel_ref, lse_ref,
             dx2_ref, dx2b_ref, o2b_ref, dz_ref, doa_ref, qa2_ref, loss_ref, oc_s, do_s):
        @pl.when(pl.program_id(0) == 0)
        def _():
            loss_ref[...] = jnp.zeros((1, LANES), F32)

        lane = _lane()
        for j in range(H // 2):
            oc_s[:, LANES * j:LANES * (j + 1)] = _pair_tile(o_ref[:, 2 * LANES * j:2 * LANES * j + LANES],
                                                            o_ref[:, 2 * LANES * j + LANES:2 * LANES * (j + 1)], lane)
        oc = oc_s[...]
        zv = z_ref[...]
        sg = _sigmoid(zv)
        sz = zv * sg
        o2 = (oc * sz).astype(BF16)
        o2b_ref[...] = o2
        e = x1_ref[...] + _dot(o2, w_ref[...]) - t_ref[...]
        sq = jnp.sum(jnp.sum(e * e, axis=1, keepdims=True), axis=0, keepdims=True)
        loss_ref[...] += jnp.broadcast_to(sq * (0.5 / D), (1, LANES))
        dx2 = e * (1.0 / D)
        dx2_ref[...] = dx2
        dx2b = dx2.astype(BF16)
        dx2b_ref[...] = dx2b
        do2 = _dot_nt(dx2b, w_ref[...])
        dz_ref[...] = (do2 * oc * (sg * (1.0 + zv * (1.0 - sg)))).astype(BF16)
        do_s[...] = do2 * sz
        deltas = jnp.zeros((tm, LANES), F32)
        for hd in range(H):
            dt = _head_tile(do_s, hd, lane)
            delta = jnp.sum(dt * _head_tile(oc_s, hd, lane), axis=1, keepdims=True)
            deltas = deltas + jnp.where(lane == hd, delta, 0.0)
            doa_ref[:, LANES * hd:LANES * (hd + 1)] = dt.astype(BF16)
        doa_ref[...] += _dot(_pack3(-deltas, lane, 0.0), first_ref[...]).astype(BF16)
        lse = jnp.concatenate([lse_ref[...], jnp.zeros((LANES - H, tm), F32)], axis=0).T
        rq = rel_ref[...] - lse
        tile_lane = lax.broadcasted_iota(jnp.int32, (1, H * LANES), 1)
        extra = tile_lane % LANES - HEAD_DIM * (1 - (tile_lane // LANES) % 2)
        kept = jnp.where((extra >= 0) & (extra < 3), jnp.zeros((), BF16), q_ref[...])
        qa2_ref[...] = kept + _dot(_pack3(rq, lane, 0.0), first_ref[...]).astype(BF16)

    return pl.pallas_call(
        body, name="attn_out", grid=(S // tm,),
        in_specs=[_rows(tm, H * LANES), _rows(tm, D), _rows(tm, D), _rows(tm, D), _whole((D, D)),
                  _rows(tm, H * LANES), _whole((LANES, H * LANES)), _rows(tm, LANES),
                  pl.BlockSpec((H, tm), lambda i: (0, i))],
        out_specs=[_rows(tm, D), _rows(tm, D), _rows(tm, D), _rows(tm, D), _rows(tm, H * LANES),
                   _rows(tm, H * LANES), _whole((1, LANES))],
        out_shape=[jax.ShapeDtypeStruct((S, D), F32), jax.ShapeDtypeStruct((S, D), BF16),
                   jax.ShapeDtypeStruct((S, D), BF16), jax.ShapeDtypeStruct((S, D), BF16),
                   jax.ShapeDtypeStruct((S, H * LANES), BF16), jax.ShapeDtypeStruct((S, H * LANES), BF16),
                   jax.ShapeDtypeStruct((1, LANES), F32)],
        scratch_shapes=[pltpu.VMEM((tm, D), F32), pltpu.VMEM((tm, D), F32)],
        compiler_params=_params(1),
    )(o_aug, z, x1, target, w_out, qa, _scatter_matrices(H)[0], rel, lse)


def _attn_bwd(qend, qa2, doa, ka, va, T):
    S = qa2.shape[0]
    H = qa2.shape[1] // LANES
    nb = S // T
    G = 2 * nb // qend.shape[1]
    W = G * T

    def body(qe_ref, q_ref, do_ref, k_ref, v_ref, dq_ref, dk_ref, dv_ref, dkt_acc, dvt_acc):
        h, g = pl.program_id(0), pl.program_id(1)

        @pl.when(g == 0)
        def _():
            dq_ref[...] = jnp.zeros((S, LANES), F32)

        kb = k_ref[...]
        vb = v_ref[...]
        dkt_acc[...] = jnp.zeros((LANES, W), F32)
        dvt_acc[...] = jnp.zeros((LANES, W), F32)

        def step(qi, c0, c1, masked):
            rows = pl.ds(pl.multiple_of(qi * T, T), 2 * T)
            qb = q_ref[rows, :]
            dob = do_ref[rows, :]
            s = _dot_nt(qb, kb[c0:c1])
            if masked:
                query = lax.broadcasted_iota(jnp.int32, s.shape, 0) + (c1 - 2 * T)
                s = jnp.where(lax.broadcasted_iota(jnp.int32, s.shape, 1) <= query, s, NEG)
            p = jnp.exp(s)
            ds = (p * _dot_nt(dob, vb[c0:c1])).astype(BF16)
            dvt_acc[:, c0:c1] += _dot(dob.astype(F32).T.astype(BF16), p.astype(BF16))
            dkt_acc[:, c0:c1] += _dot(qb.astype(F32).T.astype(BF16), ds)
            dq_ref[rows, :] += _dot(ds, kb[c0:c1])

        for m in range(G // 2):
            step(g * G + 2 * m, 0, (m + 1) * 2 * T, True)
        first = g * G + G
        n_all = jnp.maximum((qe_ref[h, 2 * g] - first + 2) // 2, 0)
        second = first + 2 * n_all

        def all_keys(i, carry):
            step(first + 2 * i, 0, W, False)
            return carry

        def late_keys(i, carry):
            step(second + 2 * i, W // 2, W, False)
            return carry

        lax.fori_loop(0, n_all, all_keys, 0)
        lax.fori_loop(0, (qe_ref[h, 2 * g + 1] - second + 2) // 2, late_keys, 0)
        dk_ref[...] = dkt_acc[...].T
        dv_ref[...] = dvt_acc[...].T.astype(BF16)

    heads = pl.BlockSpec((None, W, LANES), lambda h, i, qe: (h, i, 0))
    return pl.pallas_call(
        body, name="attn_bwd",
        grid_spec=pltpu.PrefetchScalarGridSpec(
            num_scalar_prefetch=1, grid=(H, nb // G),
            in_specs=[pl.BlockSpec((S, LANES), lambda h, i, qe: (0, h)), pl.BlockSpec((S, LANES), lambda h, i, qe: (0, h)),
                      pl.BlockSpec((W, LANES), lambda h, i, qe: (i, h)), pl.BlockSpec((W, LANES), lambda h, i, qe: (i, h))],
            out_specs=[pl.BlockSpec((None, S, LANES), lambda h, i, qe: (h, 0, 0)), heads, heads],
            scratch_shapes=[pltpu.VMEM((LANES, W), F32), pltpu.VMEM((LANES, W), F32)]),
        out_shape=[jax.ShapeDtypeStruct((H, S, LANES), F32), jax.ShapeDtypeStruct((H, S, LANES), F32),
                   jax.ShapeDtypeStruct((H, S, LANES), BF16)],
        compiler_params=_params(2),
    )(qend, qa2, doa, ka, va)


def _attn_proj_bwd(dqt, dka, dva, qraw, kraw, dz, f, gq, gk, w, wf, x1, g2, dx2):
    S, D = x1.shape
    H = D // HEAD_DIM
    tm = min(ROW_TILE, S)
    last = S // tm - 1
    tri = (lax.broadcasted_iota(jnp.int32, (tm, tm), 1) >= lax.broadcasted_iota(jnp.int32, (tm, tm), 0)).astype(BF16)

    def body(dq_ref, dk_ref, dv_ref, q_ref, k_ref, dz_ref, f_ref, gq_ref, gk_ref, w_ref, wf_ref, x1_ref, g2_ref,
             dx2_ref, tri_ref, ones_ref, dproj_ref, dx1_ref, dg2_ref, small_ref, carry, pairs):
        @pl.when(pl.program_id(0) == 0)
        def _():
            dg2_ref[...] = jnp.zeros((1, D), F32)
            small_ref[...] = jnp.zeros((8, LANES), F32)
            carry[...] = jnp.zeros((8, LANES), F32)

        lane = _lane()

        def head_pair(j, acc):
            dcs, dgq, dgk = acc
            dq2, dk2 = [], []
            q_pair, k_pair = q_ref[j], k_ref[j]
            for parity in (0, 1):
                hd = 2 * j + parity
                own, a = _own(lane, parity), _aug(parity)
                dqf = dq_ref[hd]
                dqn = jnp.where(own, dqf * Q_SCALE, 0.0)
                d, dg = _head_rms_bwd(dqn, jnp.where(own, q_pair, 0.0), gq_ref[...], ones_ref[...])
                dq2.append(d)
                dgq = dgq + dg
                dkt = dk_ref[hd]
                dcs = dcs + jnp.where(lane == hd, _col(dqf, lane, a) - _col(dkt, lane, a + 3), 0.0)
                d, dg = _head_rms_bwd(jnp.where(own, dkt, 0.0), jnp.where(own, k_pair, 0.0), gk_ref[...], ones_ref[...])
                dk2.append(d)
                dgk = dgk + dg
            pairs[0, j] = _pair_tile(*dq2, lane).astype(BF16)
            pairs[1, j] = _pair_tile(*dk2, lane).astype(BF16)
            pairs[2, j] = _pair_tile(dv_ref[2 * j], dv_ref[2 * j + 1], lane)
            return dcs, dgq, dgk

        zero = jnp.zeros((1, LANES), F32)
        dcs, dgq, dgk = lax.fori_loop(0, H // 2, head_pair, (jnp.zeros((tm, LANES), F32), zero, zero))
        for part in range(3):
            for j in range(H // 2):
                dproj_ref[:, part * D + LANES * j:part * D + LANES * (j + 1)] = pairs[part, j]
        dproj_ref[:, 3 * D:4 * D] = dz_ref[...]
        dlogf = _dot01(tri_ref[...], dcs) + carry[0:1, :]
        carry[...] = jnp.broadcast_to(dlogf[0:1, :], (8, LANES))
        df = dlogf * (1.0 / (1.0 + jnp.exp(f_ref[...])))
        dproj_ref[:, 4 * D:4 * D + LANES] = df.astype(BF16)
        small_ref[0:1, :] += jnp.sum(df, axis=0, keepdims=True)
        small_ref[1:2, :] += dgq
        small_ref[2:3, :] += dgk
        dh = _dot_nt(dproj_ref[:, 0:4 * D], w_ref[...]) + _dot_nt(dproj_ref[:, 4 * D:4 * D + LANES], wf_ref[...])
        dxn, dg = _rms_bwd(dh, x1_ref[...], g2_ref[...])
        dx1_ref[...] = dx2_ref[...] + dxn
        dg2_ref[...] += dg

    W = 4 * D + LANES
    heads = pl.BlockSpec((H, tm, LANES), lambda i: (0, last - i, 0))
    head_pairs = pl.BlockSpec((H // 2, tm, LANES), lambda i: (0, last - i, 0))
    return pl.pallas_call(
        body, name="attn_proj_bwd", grid=(S // tm,),
        in_specs=[heads, heads, heads, head_pairs, head_pairs,
                  _rows(tm, D, last), _rows(tm, LANES, last), _whole((1, LANES)), _whole((1, LANES)),
                  _whole((D, 4 * D)), _whole((D, LANES)), _rows(tm, D, last), _whole((1, D)), _rows(tm, D, last),
                  _whole((tm, tm)), _whole((LANES, LANES))],
        out_specs=[_rows(tm, W, last), _rows(tm, D, last), _whole((1, D)), _whole((8, LANES))],
        out_shape=[jax.ShapeDtypeStruct((S, W), BF16), jax.ShapeDtypeStruct((S, D), F32),
                   jax.ShapeDtypeStruct((1, D), F32), jax.ShapeDtypeStruct((8, LANES), F32)],
        scratch_shapes=[pltpu.VMEM((8, LANES), F32), pltpu.VMEM((3, H // 2, tm, LANES), BF16)],
        compiler_params=_params(1),
    )(dqt, dka, dva, qraw, kraw, dz, f, gq, gk, w, wf, x1, g2, dx2, tri, jnp.ones((LANES, LANES), BF16))


def _matmul_tn(a, b, col0, n, tn, name, stacked=False):
    S, M = a.shape
    ts = min(TN_ROWS, S)
    off = col0 // tn

    def body(a_ref, b_ref, o_ref):
        @pl.when(pl.program_id(1) == 0)
        def _():
            o_ref[...] = jnp.zeros((M, tn), F32)

        o_ref[...] += _dot_tn(a_ref[...], b_ref[...])

    if stacked:
        out_spec, out_shape = pl.BlockSpec((None, M, tn), lambda j, s: (j, 0, 0)), (n // tn, M, tn)
    else:
        out_spec, out_shape = pl.BlockSpec((M, tn), lambda j, s: (0, j)), (M, n)
    return pl.pallas_call(
        body, name=name, grid=(n // tn, S // ts),
        in_specs=[pl.BlockSpec((ts, M), lambda j, s: (s, 0)), pl.BlockSpec((ts, tn), lambda j, s: (s, off + j))],
        out_specs=out_spec, out_shape=jax.ShapeDtypeStruct(out_shape, F32),
        compiler_params=_params(2),
    )(a, b)


def _adam_update(gv, w_ref, m_ref, v_ref, d_ref, m2_ref, v2_ref):
    m2 = ADAM_B1 * m_ref[...] + (1.0 - ADAM_B1) * gv
    v2 = ADAM_B2 * v_ref[...] + (1.0 - ADAM_B2) * (gv * gv)
    m2_ref[...] = m2
    v2_ref[...] = v2
    m_hat = m2 / (1.0 - ADAM_B1 ** ADAM_STEP)
    v_hat = v2 / (1.0 - ADAM_B2 ** ADAM_STEP)
    d_ref[...] = -ADAM_LR * (m_hat / (jnp.sqrt(v_hat) + ADAM_EPS) + ADAM_WD * w_ref[...])


def _adamw(w, g, m, v, name):
    r, c = w.shape
    tr = ROW_TILE if r % ROW_TILE == 0 else r

    def body(w_ref, g_ref, m_ref, v_ref, d_ref, m2_ref, v2_ref):
        _adam_update(g_ref[...], w_ref, m_ref, v_ref, d_ref, m2_ref, v2_ref)

    spec = _rows(tr, c)
    return pl.pallas_call(
        body, name=name, grid=(r // tr,), in_specs=[spec] * 4, out_specs=[spec] * 3,
        out_shape=[jax.ShapeDtypeStruct((r, c), F32)] * 3, compiler_params=_params(1),
    )(w, g, m, v)


def _adamw_halves(w, mine, other, m, v, core, name):
    r, c = mine.shape
    tr = ROW_TILE if r % ROW_TILE == 0 else r
    per = r // tr

    def body(core_ref, w_ref, mine_ref, other_ref, m_ref, v_ref, g_ref, d_ref, m2_ref, v2_ref):
        gv = jnp.where(pl.program_id(0) // per == core_ref[0], mine_ref[...], other_ref[...])
        g_ref[...] = gv
        _adam_update(gv, w_ref, m_ref, v_ref, d_ref, m2_ref, v2_ref)

    full = pl.BlockSpec((tr, c), lambda i, core: (i, 0))
    half = pl.BlockSpec((tr, c), lambda i, core: (i % per, 0))
    return pl.pallas_call(
        body, name=name,
        grid_spec=pltpu.PrefetchScalarGridSpec(num_scalar_prefetch=1, grid=(2 * per,),
                                               in_specs=[full, half, half, full, full], out_specs=[full] * 4),
        out_shape=[jax.ShapeDtypeStruct((2 * r, c), F32)] * 4, compiler_params=_params(1),
    )(core, w, mine, other, m, v)


def _local_step(x, target, g1, w_in, conv_w, w_out, g2, wa_in, b_f, gq, gk, wa_out):
    S, D = x.shape
    H = D // HEAD_DIM
    w_qkvz = wa_in[:, :4 * D]
    wf = jnp.pad(wa_in[:, 4 * D:], ((0, 0), (0, LANES - H)))
    bf = jnp.pad(b_f, ((0, 0), (0, LANES - H)))
    gq128 = jnp.concatenate([gq, gq], axis=1)
    gk128 = jnp.concatenate([gk, gk], axis=1)

    proj, h1, yc, y, x1 = _conv_fwd(x, g1, w_in, conv_w, w_out)
    h2, qraw, kraw, z, f, c, rel, qa, ka, va, vt = _attn_front(x1, g2, w_qkvz, wf, bf, gq128, gk128)
    T = vt.shape[3]
    kstart, qend, bound = _skip_tables(c[:, :H], (c - rel)[:, :H], gq, gk, T, min(ATT_GROUP, S // T))
    o_aug, lse = lax.cond(2.0 * bound <= PLAIN_EXP_MAX, functools.partial(_attn_fwd, online_max=False),
                          functools.partial(_attn_fwd, online_max=True), kstart, qa, ka, vt)
    dx2, dx2b, o2b, dz, doa, qa2, loss = _attn_out(o_aug, lse.reshape(H, S), rel, z, x1, target, wa_out, qa)
    dqt, dka, dva = _attn_bwd(qend, qa2, doa, ka, va, T)
    dproj2, dx1, dg2, small = _attn_proj_bwd(dqt, dka, dva, qraw, kraw, dz, f, gq128, gk128, w_qkvz, wf, x1, g2, dx2)
    dproj1, dx, dx1b, dg1, dcw = _conv_bwd(dx1, x, g1, w_in, w_out, conv_w, proj, yc)

    tn = min(1024, D)
    dwa_out = _matmul_tn(o2b, dx2b, 0, D, tn, "dw_attn_out")
    dwa_in = jnp.concatenate([_matmul_tn(h2, dproj2, 0, 4 * D, tn, "dw_attn_in"),
                              _matmul_tn(h2, dproj2, 4 * D, LANES, LANES, "dw_attn_f")[:, :H]], axis=1)
    dw_out = _matmul_tn(y, dx1b, 0, D, tn, "dw_conv_out")
    dw_in = _matmul_tn(h1, dproj1, 0, 4 * D, D, "dw_conv_in", stacked=True)
    grads = dict(conv_norm_g=dg1, conv_w_in=dw_in, conv_w=dcw, conv_w_out=dw_out, attn_norm_g=dg2,
                 attn_w_in=dwa_in, attn_b_f=small[0:1, :H],
                 attn_q_norm_g=small[1:2, :HEAD_DIM] + small[1:2, HEAD_DIM:],
                 attn_k_norm_g=small[2:3, :HEAD_DIM] + small[2:3, HEAD_DIM:], attn_w_out=dwa_out)
    return loss[0, 0], dx, grads


def _coords():
    return lax.axis_index("x"), lax.axis_index("y"), lax.axis_index("c")


def _at(ref, idx):
    return ref.at[idx] if idx else ref


def _other_chips(x, y):
    return [(1 - x, y), (x, 1 - y), (1 - x, 1 - y)]


def _all_gather(halved, whole):
    nh, nw = len(halved), len(whole)

    def body(*refs):
        src_h, src_w = refs[:nh], refs[nh:nh + nw]
        out_h, out_w = refs[nh + nw:2 * nh + nw], refs[2 * nh + nw:2 * (nh + nw)]
        send_h, recv_h, send_w, recv_w = refs[2 * (nh + nw):]
        x, y, c = _coords()
        mine = 2 * x + y
        sibling = (x, y, 1 - c)
        chips = _other_chips(x, y)

        def copy_h(a, k, chip, half, to, src=None):
            dst = out_h[a].at[chip, half]
            return pltpu.make_async_remote_copy(src_ref=dst if src is None else src, dst_ref=dst,
                                                send_sem=send_h.at[a, k], recv_sem=recv_h.at[a, k],
                                                device_id=to, device_id_type=MESH)

        def copy_w(a, k, chip, to):
            return pltpu.make_async_remote_copy(src_ref=src_w[a], dst_ref=out_w[a].at[chip],
                                                send_sem=send_w.at[a, k], recv_sem=recv_w.at[a, k],
                                                device_id=to, device_id_type=MESH)

        first = [copy_h(a, k, mine, c, (*chip, c), src=src_h[a].at[c]) for a in range(nh) for k, chip in enumerate(chips)]
        first += [copy_w(a, k, mine, (*chip, c)) for a in range(nw) for k, chip in enumerate(chips)]
        own = [pltpu.make_async_remote_copy(src_ref=src_h[a], dst_ref=out_h[a].at[mine], send_sem=send_h.at[a, 6],
                                            recv_sem=recv_h.at[a, 6], device_id=sibling, device_id_type=MESH)
               for a in range(nh)]
        own += [copy_w(a, 3, mine, sibling) for a in range(nw)]
        for cp in first + own:
            cp.start()
        passed = []
        for a in range(nh):
            for k, (px, py) in enumerate(chips):
                copy_h(a, k, 2 * px + py, c, (x, y, c)).wait_recv()
                cp = copy_h(a, 3 + k, 2 * px + py, c, sibling)
                cp.start()
                passed.append(cp)
        for a in range(nh):
            for k, (px, py) in enumerate(chips):
                copy_h(a, 3 + k, 2 * px + py, 1 - c, (x, y, c)).wait_recv()
        for a in range(nw):
            for k, (px, py) in enumerate(chips):
                copy_w(a, k, 2 * px + py, (x, y, c)).wait_recv()
        for cp in own:
            cp.wait_recv()
        for cp in first + passed + own:
            cp.wait_send()

    out_shape = [jax.ShapeDtypeStruct((4,) + a.shape, a.dtype) for a in list(halved) + list(whole)]
    return pl.pallas_call(
        body, name="gather_weights", in_specs=[ANY] * (nh + nw), out_specs=[ANY] * (nh + nw), out_shape=out_shape,
        scratch_shapes=[pltpu.SemaphoreType.DMA((nh, 7)), pltpu.SemaphoreType.DMA((nh, 7)),
                        pltpu.SemaphoreType.DMA((nw, 4)), pltpu.SemaphoreType.DMA((nw, 4))],
    )(*halved, *whole)


def _exchange(name, srcs, lands, copies, local_copies):
    ns, nl, n, nloc = len(srcs), len(lands), len(copies), len(local_copies)

    def body(*refs):
        src, land = refs[:ns], refs[ns:ns + nl]
        send, recv, local_sem = refs[ns + nl:]
        me = _coords()
        started = []
        for k, (si, s_at, li, l_at, ci) in enumerate(local_copies):
            cp = pltpu.make_async_copy(_at(src[si], s_at(*me)), _at(land[li], l_at(*me)), local_sem.at[k])
            cp.start()
            started.append(cp)
        remote = []
        for k, (si, s_at, li, l_at, peer) in enumerate(copies):
            cp = pltpu.make_async_remote_copy(src_ref=_at(src[si], s_at(*me)), dst_ref=_at(land[li], l_at(*me)),
                                              send_sem=send.at[k], recv_sem=recv.at[k],
                                              device_id=peer(*me), device_id_type=MESH)
            cp.start()
            remote.append(cp)
        for cp in remote:
            cp.wait()
        for cp in started:
            cp.wait()

    return pl.pallas_call(
        body, name=name, in_specs=[ANY] * ns, out_specs=[ANY] * nl, out_shape=list(lands),
        scratch_shapes=[pltpu.SemaphoreType.DMA((n,)), pltpu.SemaphoreType.DMA((n,)),
                        pltpu.SemaphoreType.DMA((max(nloc, 1),))],
    )(*srcs)


def _add_pairs(a, b, core, name):
    _, r, cols = b.shape
    tr = ROW_TILE if r % ROW_TILE == 0 else r

    def body(core_ref, a_ref, b_ref, o_ref, ob_ref):
        s = a_ref[...] + b_ref[...]
        o_ref[...] = s
        ob_ref[...] = s.astype(BF16)

    spec = pl.BlockSpec((None, tr, cols), lambda j, i, core: (j, i, 0))
    return pl.pallas_call(
        body, name=name,
        grid_spec=pltpu.PrefetchScalarGridSpec(
            num_scalar_prefetch=1, grid=(4, r // tr),
            in_specs=[pl.BlockSpec((None, None, tr, cols), lambda j, i, core: (j, core[0], i, 0)), spec],
            out_specs=[spec, spec]),
        out_shape=[jax.ShapeDtypeStruct(b.shape, F32), jax.ShapeDtypeStruct(b.shape, BF16)],
        compiler_params=_params(2),
    )(core, a, b)


def _sum_chips(own, landed, name):
    _, r, cols = landed.shape
    tr = ROW_TILE if r % ROW_TILE == 0 else r

    def body(own_ref, land_ref, o_ref):
        acc = own_ref[...]
        for j in range(3):
            acc = acc + land_ref[j].astype(F32)
        o_ref[...] = acc

    return pl.pallas_call(
        body, name=name, grid=(r // tr,),
        in_specs=[_rows(tr, cols), pl.BlockSpec((3, tr, cols), lambda i: (0, i, 0))], out_specs=_rows(tr, cols),
        out_shape=jax.ShapeDtypeStruct((r, cols), F32), compiler_params=_params(1),
    )(own, landed)


def _sum_devices(landed, name):
    def body(l_ref, o_ref):
        acc = l_ref[0]
        for j in range(1, 8):
            acc = acc + l_ref[j]
        o_ref[...] = acc

    return pl.pallas_call(body, name=name, out_shape=jax.ShapeDtypeStruct(landed.shape[1:], F32))(landed)


def _reduce_gradients(big, small):
    nb = len(big)
    x, y, c = _coords()
    mine = 2 * x + y
    flips = [(fx, fy, fc) for fx in (0, 1) for fy in (0, 1) for fc in (0, 1) if fx or fy or fc]

    def flip(fx, fy, fc):
        return lambda x, y, c: (x ^ fx, y ^ fy, c ^ fc)

    copies = [(a, (lambda j: lambda x, y, c: (j, 1 - c))(j), a, (lambda j: lambda x, y, c: (j,))(j), flip(0, 0, 1))
              for a in range(nb) for j in range(4)]
    copies += [(nb, lambda x, y, c: (), nb, lambda x, y, c: (4 * x + 2 * y + c,), flip(*f)) for f in flips]
    lands = [jax.ShapeDtypeStruct((4,) + g.shape[2:], F32) for g in big] + [jax.ShapeDtypeStruct((8,) + small.shape, F32)]
    local = [(nb, lambda x, y, c: (), nb, lambda x, y, c: (4 * x + 2 * y + c,), None)]
    landed = _exchange("swap_halves", list(big) + [small], lands, copies, local)
    small_sum = _sum_devices(landed[nb], "sum_small")

    chip_f32, chip_bf16 = [], []
    for a in range(nb):
        s, sb = _add_pairs(big[a], landed[a], jnp.reshape(c, (1,)).astype(jnp.int32), f"add_cores_{a}")
        chip_f32.append(s)
        chip_bf16.append(sb)

    chip_flips = [(1, 0), (0, 1), (1, 1)]
    copies = [(a, (lambda f: lambda x, y, c: (2 * (x ^ f[0]) + (y ^ f[1]),))(f), a, (lambda k: lambda x, y, c: (k,))(k),
               flip(f[0], f[1], 0)) for a in range(nb) for k, f in enumerate(chip_flips)]
    lands = [jax.ShapeDtypeStruct((3,) + g.shape[1:], BF16) for g in chip_bf16]
    landed = _exchange("send_chip_sums", chip_bf16, lands, copies, [])
    totals = [_sum_chips(lax.dynamic_index_in_dim(chip_f32[a], mine, axis=0, keepdims=False), landed[a],
                         f"sum_chips_{a}") for a in range(nb)]

    copies = [(a, lambda x, y, c: (), a, lambda x, y, c: (), flip(0, 0, 1)) for a in range(nb)]
    lands = [jax.ShapeDtypeStruct(t.shape, F32) for t in totals]
    return list(zip(totals, _exchange("swap_sums", totals, lands, copies, []))), small_sum


def kernel(x, conv_norm_g, conv_w_in, conv_w, conv_w_out, attn_norm_g, attn_w_in, attn_b_f, attn_q_norm_g, attn_k_norm_g, attn_w_out, loss_target, m_conv_norm_g, m_conv_w_in, m_conv_w, m_conv_w_out, m_attn_norm_g, m_attn_w_in, m_attn_b_f, m_attn_q_norm_g, m_attn_k_norm_g, m_attn_w_out, v_conv_norm_g, v_conv_w_in, v_conv_w, v_conv_w_out, v_attn_norm_g, v_attn_w_in, v_attn_b_f, v_attn_q_norm_g, v_attn_k_norm_g, v_attn_w_out):
    xi, yi, _ = _coords()
    chip = 2 * xi + yi
    D = x.shape[2]
    H = D // HEAD_DIM
    names = ["conv_norm_g", "conv_w_in", "conv_w", "conv_w_out", "attn_norm_g", "attn_w_in", "attn_b_f",
             "attn_q_norm_g", "attn_k_norm_g", "attn_w_out"]
    weights = dict(zip(names, [conv_norm_g, conv_w_in, conv_w, conv_w_out, attn_norm_g, attn_w_in, attn_b_f,
                               attn_q_norm_g, attn_k_norm_g, attn_w_out]))
    m_in = dict(zip(names, [m_conv_norm_g, m_conv_w_in, m_conv_w, m_conv_w_out, m_attn_norm_g, m_attn_w_in,
                            m_attn_b_f, m_attn_q_norm_g, m_attn_k_norm_g, m_attn_w_out]))
    v_in = dict(zip(names, [v_conv_norm_g, v_conv_w_in, v_conv_w, v_conv_w_out, v_attn_norm_g, v_attn_w_in,
                            v_attn_b_f, v_attn_q_norm_g, v_attn_k_norm_g, v_attn_w_out]))
    weights = {k: w[0] for k, w in weights.items()}
    m_in = {k: w[0] for k, w in m_in.items()}
    v_in = {k: w[0] for k, w in v_in.items()}

    big_names = ["conv_w_in", "attn_w_in", "conv_w_out", "attn_w_out"]
    halved = [weights[k].astype(BF16).reshape(2, weights[k].shape[0] // 2, weights[k].shape[1]) for k in big_names]
    q = D // 4
    small_w = jnp.concatenate([weights["conv_w"], weights["attn_norm_g"][None, :], jnp.zeros((4, q), F32)], axis=0)
    g_in, ga_in, g_out, ga_out, g_small = _all_gather(halved, [small_w])
    w_in = g_in.reshape(4, D, D)
    wa_in = ga_in.reshape(4, D, D + H // 4).transpose(1, 0, 2).reshape(D, 4 * D + H)
    w_out = g_out.reshape(D, D)
    wa_out = ga_out.reshape(D, D)
    conv_w_full = g_small[:, 0:3, :].transpose(1, 0, 2).reshape(3, D)
    attn_g_full = g_small[:, 3, :].reshape(1, D)

    loss_part, grad_x, grads = _local_step(x[0], loss_target[0], weights["conv_norm_g"][None, :], w_in, conv_w_full,
                                           w_out, attn_g_full, wa_in, weights["attn_b_f"][None, :],
                                           weights["attn_q_norm_g"][None, :], weights["attn_k_norm_g"][None, :], wa_out)

    big = [grads["conv_w_in"].reshape(4, 2, D // 2, D),
           grads["attn_w_in"].reshape(D, 4, D + H // 4).transpose(1, 0, 2).reshape(4, 2, D // 2, D + H // 4),
           grads["conv_w_out"].reshape(4, 2, D // 8, D), grads["attn_w_out"].reshape(4, 2, D // 8, D)]
    tail = jnp.concatenate([grads["attn_b_f"], grads["attn_q_norm_g"], grads["attn_k_norm_g"],
                            jnp.reshape(loss_part, (1, 1)), jnp.zeros((1, D - H - 2 * HEAD_DIM - 1), F32)], axis=1)
    small = jnp.concatenate([grads["conv_norm_g"], grads["conv_w"], grads["attn_norm_g"], tail,
                             jnp.zeros((2, D), F32)], axis=0)
    reduced, small_sum = _reduce_gradients(big, small)
    final = {}
    final["conv_norm_g"] = small_sum[0]
    final["conv_w"] = lax.dynamic_slice_in_dim(small_sum[1:4], chip * q, q, axis=1)
    final["attn_norm_g"] = lax.dynamic_slice_in_dim(small_sum[4], chip * q, q, axis=0)
    final["attn_b_f"] = small_sum[5, :H]
    final["attn_q_norm_g"] = small_sum[5, H:H + HEAD_DIM]
    final["attn_k_norm_g"] = small_sum[5, H + HEAD_DIM:H + 2 * HEAD_DIM]
    loss = small_sum[5, H + 2 * HEAD_DIM]

    delta, new_m, new_v = {}, {}, {}
    core = jnp.reshape(lax.axis_index("c"), (1,)).astype(jnp.int32)
    for k, (mine, other) in zip(big_names, reduced):
        final[k], delta[k], new_m[k], new_v[k] = _adamw_halves(weights[k], mine, other, m_in[k], v_in[k], core,
                                                               "adamw_" + k)
    for k in names:
        if k in big_names:
            continue
        shape = weights[k].shape
        as2d = (lambda a: a.reshape(1, -1)) if len(shape) == 1 else (lambda a: a)
        d, m2, v2 = _adamw(as2d(weights[k]), as2d(final[k]), as2d(m_in[k]), as2d(v_in[k]), "adamw_" + k)
        delta[k], new_m[k], new_v[k] = d.reshape(shape), m2.reshape(shape), v2.reshape(shape)
    lead = lambda a: a[None]
    return (loss, grad_x[None], *[lead(final[k]) for k in names], *[lead(delta[k]) for k in names],
            *[lead(new_m[k]) for k in names], *[lead(new_v[k]) for k in names])
```

```python
import functools

import jax
import jax.numpy as jnp
from jax import lax
from jax.experimental import pallas as pl
from jax.experimental.pallas import tpu as pltpu

F32 = jnp.float32
BF16 = jnp.bfloat16
HEAD_DIM = 64
LANES = 128
RMS_EPS = 1e-6
NEG = -1e30
Q_SCALE = 0.125
ROW_TILE = 256
CONV_TILE = 512
ATT_GROUP = 4
SKIP_LOG = 106.0
PLAIN_EXP_MAX = 60.0
TN_ROWS = 2048
VMEM_LIMIT = 56 << 20
ADAM_LR, ADAM_B1, ADAM_B2, ADAM_EPS, ADAM_WD, ADAM_STEP = 0.001, 0.9, 0.999, 1e-08, 0.01, 10
MESH = pl.DeviceIdType.MESH
ANY = pl.BlockSpec(memory_space=pl.ANY)


def _lane():
    return lax.broadcasted_iota(jnp.int32, (1, LANES), 1)


def _split3(x):
    hi = x.astype(BF16).astype(F32)
    r = x - hi
    mid = r.astype(BF16).astype(F32)
    lo = (r - mid).astype(BF16).astype(F32)
    return hi, mid, lo


STAT_STRIDE = 16
ONE_LANE = 3 * STAT_STRIDE


def _pack3(x, lane, one):
    hi, mid, lo = _split3(x)
    packed = hi + pltpu.roll(mid, STAT_STRIDE, 1) + pltpu.roll(lo, 2 * STAT_STRIDE, 1)
    return jnp.where(lane == ONE_LANE, one, packed).astype(BF16)


def _scatter_matrices(H):
    rows = lax.broadcasted_iota(jnp.int32, (LANES, H * LANES), 0)
    cols = lax.broadcasted_iota(jnp.int32, (LANES, H * LANES), 1)
    head, within = cols // LANES, cols % LANES
    extra = within - _aug(head % 2)
    term = (rows < ONE_LANE) & (rows % STAT_STRIDE == head)
    first = ((term & (extra == rows // STAT_STRIDE)) | ((rows == ONE_LANE) & (extra >= 3) & (extra < 6)))
    second = ((term & (extra - 3 == rows // STAT_STRIDE)) | ((rows == ONE_LANE) & (extra >= 0) & (extra < 3)))
    return first.astype(BF16), second.astype(BF16)


def _put(base, lane, start, parts):
    for j, p in enumerate(parts):
        base = jnp.where(lane == start + j, p, base)
    return base


def _col(x, lane, idx):
    return jnp.sum(jnp.where(lane == idx, x, 0.0), axis=1, keepdims=True)


def _feat(parity):
    return HEAD_DIM * parity


def _aug(parity):
    return HEAD_DIM * (1 - parity)


def _own(lane, parity):
    return (lane >= _feat(parity)) & (lane < _feat(parity) + HEAD_DIM)


def _head_tile(ref, hd, lane):
    j = hd // 2
    return jnp.where(_own(lane, hd % 2), ref[:, LANES * j:LANES * (j + 1)], 0.0)


def _pair_tile(even, odd, lane):
    return jnp.where(lane < HEAD_DIM, even, odd)


def _sigmoid(x):
    return 0.5 * jnp.tanh(0.5 * x) + 0.5


def _dot(a, b):
    return jnp.dot(a, b, preferred_element_type=F32)


def _dot_nt(a, b):
    return lax.dot_general(a, b, (((1,), (1,)), ((), ())), preferred_element_type=F32)


def _dot_tn(a, b):
    return lax.dot_general(a, b, (((0,), (0,)), ((), ())), preferred_element_type=F32)


def _dot01(tri, x):
    hi, mid, lo = _split3(x)
    return _dot(tri, hi.astype(BF16)) + _dot(tri, mid.astype(BF16)) + _dot(tri, lo.astype(BF16))


def _rms_bwd(dh, x, g):
    inv = lax.rsqrt(jnp.mean(x * x, axis=-1, keepdims=True) + RMS_EPS)
    xh = x * inv
    dxn = dh * g
    dx = inv * (dxn - xh * jnp.mean(dxn * xh, axis=-1, keepdims=True))
    return dx, jnp.sum(dh * xh, axis=0, keepdims=True)


def _head_rms_bwd(dn, t, g, ones):
    sq = t * t
    hi = sq.astype(BF16)
    lo = (sq - hi.astype(F32)).astype(BF16)
    inv = lax.rsqrt((_dot(hi, ones) + _dot(lo, ones)) * (1.0 / HEAD_DIM) + RMS_EPS)
    th = t * inv
    gd = dn * g
    d = inv * (gd - th * (jnp.sum(gd * th, axis=1, keepdims=True) * (1.0 / HEAD_DIM)))
    return d, jnp.sum(dn * th, axis=0, keepdims=True)


def _params(n_grid):
    return pltpu.CompilerParams(dimension_semantics=("arbitrary",) * n_grid, vmem_limit_bytes=VMEM_LIMIT)


def _rows(tm, cols, rev=None):
    if rev is None:
        return pl.BlockSpec((tm, cols), lambda i: (i, 0))
    return pl.BlockSpec((tm, cols), lambda i: (rev - i, 0))


def _whole(shape, buffers=None):
    mode = {} if buffers is None else dict(pipeline_mode=pl.Buffered(buffers))
    return pl.BlockSpec(shape, lambda *_: (0,) * len(shape), **mode)


def _conv_fwd(x, g1, w_in, conv_w, w_out):
    S, D = x.shape
    tm = min(CONV_TILE, S)
    sub = min(ROW_TILE, tm)

    def body(x_ref, g_ref, win_ref, cw_ref, wout_ref, proj_ref, h_ref, yc_ref, y_ref, x1_ref, prev_u):
        @pl.when(pl.program_id(0) == 0)
        def _():
            prev_u[...] = jnp.zeros((sub, D), F32)

        for r in range(0, tm, sub):
            rows = slice(r, r + sub)
            xv = x_ref[rows, :]
            inv = lax.rsqrt(jnp.mean(xv * xv, axis=-1, keepdims=True) + RMS_EPS)
            h = (xv * inv * g_ref[...]).astype(BF16)
            h_ref[rows, :] = h
            for j in range(4):
                proj_ref[rows, j * D:(j + 1) * D] = _dot(h, win_ref[j])
            u = proj_ref[rows, D:2 * D] * proj_ref[rows, 2 * D:3 * D]
            pu = prev_u[...]
            row = lax.broadcasted_iota(jnp.int32, (sub, 1), 0)
            u1 = jnp.where(row < 1, pltpu.roll(pu, 1, 0), pltpu.roll(u, 1, 0))
            u2 = jnp.where(row < 2, pltpu.roll(pu, 2, 0), pltpu.roll(u, 2, 0))
            prev_u[...] = u
            w = cw_ref[...]
            yc = w[2:3] * u + w[1:2] * u1 + w[0:1] * u2
            yc_ref[rows, :] = yc
            z = proj_ref[rows, 3 * D:4 * D]
            y = (proj_ref[rows, 0:D] * yc * (z * _sigmoid(z))).astype(BF16)
            y_ref[rows, :] = y
            x1_ref[rows, :] = xv + _dot(y, wout_ref[...])

    return pl.pallas_call(
        body, name="conv_fwd", grid=(S // tm,),
        in_specs=[_rows(tm, D), _whole((1, D)), _whole((4, D, D), 1), _whole((3, D)), _whole((D, D), 1)],
        out_specs=[_rows(tm, 4 * D), _rows(tm, D), _rows(tm, D), _rows(tm, D), _rows(tm, D)],
        out_shape=[jax.ShapeDtypeStruct((S, 4 * D), F32), jax.ShapeDtypeStruct((S, D), BF16),
                   jax.ShapeDtypeStruct((S, D), F32), jax.ShapeDtypeStruct((S, D), BF16),
                   jax.ShapeDtypeStruct((S, D), F32)],
        scratch_shapes=[pltpu.VMEM((sub, D), F32)],
        compiler_params=_params(1),
    )(x, g1, w_in, conv_w, w_out)


def _conv_bwd(dproj2, wa, wf, x1, g2, dx2, x, g1, w_in, w_out, conv_w, proj, yc):
    S, D = x.shape
    tm = min(ROW_TILE, S)
    sub = min(ROW_TILE, tm)
    last = S // tm - 1

    def body(dp2_ref, wa_ref, wf_ref, x1_ref, g2_ref, dx2_ref, x_ref, g_ref, win_ref, wout_ref, cw_ref, proj_ref, yc_ref,
             dproj_ref, dx_ref, dx1b_ref, dg_ref, dcw_ref, dg2_ref, next_d):
        @pl.when(pl.program_id(0) == 0)
        def _():
            dg_ref[...] = jnp.zeros((1, D), F32)
            dcw_ref[...] = jnp.zeros((3, D), F32)
            dg2_ref[...] = jnp.zeros((1, D), F32)
            next_d[...] = jnp.zeros((sub, D), F32)

        for r in range(tm - sub, -1, -sub):
            rows = slice(r, r + sub)
            dh2 = _dot_nt(dp2_ref[rows, 0:4 * D], wa_ref[...]) + _dot_nt(dp2_ref[rows, 4 * D:4 * D + LANES], wf_ref[...])
            dxn2, dg2 = _rms_bwd(dh2, x1_ref[rows, :], g2_ref[...])
            dg2_ref[...] += dg2
            dx1v = dx2_ref[rows, :] + dxn2
            dx1b = dx1v.astype(BF16)
            dx1b_ref[rows, :] = dx1b
            dy = _dot_nt(dx1b, wout_ref[...])
            b = proj_ref[rows, 0:D]
            c = proj_ref[rows, D:2 * D]
            xin = proj_ref[rows, 2 * D:3 * D]
            z = proj_ref[rows, 3 * D:4 * D]
            sg = _sigmoid(z)
            sz = z * sg
            ycv = yc_ref[rows, :]
            d0 = dy * b * sz
            dproj_ref[rows, 0:D] = (dy * ycv * sz).astype(BF16)
            dproj_ref[rows, 3 * D:4 * D] = (dy * b * ycv * (sg * (1.0 + z * (1.0 - sg)))).astype(BF16)
            nd = next_d[...]
            row = lax.broadcasted_iota(jnp.int32, (sub, 1), 0)
            d1 = jnp.where(row >= sub - 1, pltpu.roll(nd, sub - 1, 0), pltpu.roll(d0, sub - 1, 0))
            d2 = jnp.where(row >= sub - 2, pltpu.roll(nd, sub - 2, 0), pltpu.roll(d0, sub - 2, 0))
            next_d[...] = d0
            w = cw_ref[...]
            du = w[2:3] * d0 + w[1:2] * d1 + w[0:1] * d2
            u = c * xin
            dcw_ref[2:3, :] += jnp.sum(d0 * u, axis=0, keepdims=True)
            dcw_ref[1:2, :] += jnp.sum(d1 * u, axis=0, keepdims=True)
            dcw_ref[0:1, :] += jnp.sum(d2 * u, axis=0, keepdims=True)
            dproj_ref[rows, D:2 * D] = (du * xin).astype(BF16)
            dproj_ref[rows, 2 * D:3 * D] = (du * c).astype(BF16)
            dh = _dot_nt(dproj_ref[rows, 0:D], win_ref[0])
            for j in range(1, 4):
                dh = dh + _dot_nt(dproj_ref[rows, j * D:(j + 1) * D], win_ref[j])
            dxn, dg = _rms_bwd(dh, x_ref[rows, :], g_ref[...])
            dx_ref[rows, :] = dx1v + dxn
            dg_ref[...] += dg

    return pl.pallas_call(
        body, name="conv_bwd", grid=(S // tm,),
        in_specs=[_rows(tm, 4 * D + LANES, last), _whole((D, 4 * D), 1), _whole((D, LANES), 1), _rows(tm, D, last),
                  _whole((1, D)), _rows(tm, D, last),
                  _rows(tm, D, last), _whole((1, D)), _whole((4, D, D), 1), _whole((D, D), 1),
                  _whole((3, D)), _rows(tm, 4 * D, last), _rows(tm, D, last)],
        out_specs=[_rows(tm, 4 * D, last), _rows(tm, D, last), _rows(tm, D, last), _whole((1, D)), _whole((3, D)),
                   _whole((1, D))],
        out_shape=[jax.ShapeDtypeStruct((S, 4 * D), BF16), jax.ShapeDtypeStruct((S, D), F32),
                   jax.ShapeDtypeStruct((S, D), BF16), jax.ShapeDtypeStruct((1, D), F32),
                   jax.ShapeDtypeStruct((3, D), F32), jax.ShapeDtypeStruct((1, D), F32)],
        scratch_shapes=[pltpu.VMEM((sub, D), F32)],
        compiler_params=_params(1),
    )(dproj2, wa, wf, x1, g2, dx2, x, g1, w_in, w_out, conv_w, proj, yc)


def _attn_front(x1, g2, w, wf, bf, gq, gk):
    S, D = x1.shape
    H = D // HEAD_DIM
    tm = min(ROW_TILE, S)
    tri = (lax.broadcasted_iota(jnp.int32, (tm, tm), 1) <= lax.broadcasted_iota(jnp.int32, (tm, tm), 0)).astype(BF16)

    def body(x_ref, g_ref, w_ref, wf_ref, bf_ref, gq_ref, gk_ref, tri_ref, first_ref, second_ref,
             h_ref, qh_ref, kh_ref, z_ref, f_ref, c_ref, rel_ref, qa_ref, ka_ref, va_ref, vt_ref,
             carry, v_s, qraw_ref, kraw_ref):
        @pl.when(pl.program_id(0) == 0)
        def _():
            carry[...] = jnp.zeros((8, LANES), F32)

        xv = x_ref[...]
        inv = lax.rsqrt(jnp.mean(xv * xv, axis=-1, keepdims=True) + RMS_EPS)
        h = (xv * inv * g_ref[...]).astype(BF16)
        h_ref[...] = h
        qraw_ref[...] = _dot(h, w_ref[:, 0:D])
        kraw_ref[...] = _dot(h, w_ref[:, D:2 * D])
        v_s[...] = _dot(h, w_ref[:, 2 * D:3 * D])
        z_ref[...] = _dot(h, w_ref[:, 3 * D:4 * D])
        lane = _lane()
        f = _dot(h, wf_ref[...]) + bf_ref[...]
        f_ref[...] = f
        logf = jnp.where(lane < H, jnp.minimum(f, 0.0) - jnp.log(1.0 + jnp.exp(-jnp.abs(f))), 0.0)
        cs = _dot01(tri_ref[...], logf) + carry[0:1, :]
        c_ref[...] = cs
        carry[...] = jnp.broadcast_to(cs[tm - 1:tm, :], (8, LANES))
        diags = jnp.zeros((tm, LANES), F32)
        for hd in range(H):
            sl = slice(LANES * hd, LANES * (hd + 1))
            a = _aug(hd % 2)
            if hd % 2 == 0:
                qh_ref[hd // 2] = qraw_ref[:, LANES * (hd // 2):LANES * (hd // 2 + 1)]
                kh_ref[hd // 2] = kraw_ref[:, LANES * (hd // 2):LANES * (hd // 2 + 1)]
            qt = _head_tile(qraw_ref, hd, lane)
            qn = qt * lax.rsqrt(jnp.sum(qt * qt, axis=1, keepdims=True) * (1.0 / HEAD_DIM) + RMS_EPS) * gq_ref[...]
            kt = _head_tile(kraw_ref, hd, lane)
            kn = kt * lax.rsqrt(jnp.sum(kt * kt, axis=1, keepdims=True) * (1.0 / HEAD_DIM) + RMS_EPS) * gk_ref[...]
            diags = diags + jnp.where(lane == hd, jnp.sum(qn * kn, axis=1, keepdims=True) * Q_SCALE, 0.0)
            qa_ref[:, sl] = (qn * Q_SCALE).astype(BF16)
            ka_ref[:, sl] = kn.astype(BF16)
            va = jnp.where((lane >= a) & (lane < a + 3), 1.0, _head_tile(v_s, hd, lane))
            va_ref[:, sl] = va.astype(BF16)
            vt_ref[hd] = va.T.astype(BF16)
        rel = cs - diags
        rel_ref[...] = rel
        qa_ref[...] += _dot(_pack3(rel, lane, 1.0), first_ref[...]).astype(BF16)
        ka_ref[...] += _dot(_pack3(-cs, lane, 1.0), second_ref[...]).astype(BF16)

    nb = S // tm
    heads = pl.BlockSpec((H // 2, tm, LANES), lambda i: (0, i, 0))
    return pl.pallas_call(
        body, name="attn_front", grid=(nb,),
        in_specs=[_rows(tm, D), _whole((1, D)), _whole((D, 4 * D)), _whole((D, LANES)), _whole((1, LANES)),
                  _whole((1, LANES)), _whole((1, LANES)), _whole((tm, tm)), _whole((LANES, H * LANES)),
                  _whole((LANES, H * LANES))],
        out_specs=[_rows(tm, D), heads, heads, _rows(tm, D), _rows(tm, LANES), _rows(tm, LANES), _rows(tm, LANES),
                   _rows(tm, H * LANES), _rows(tm, H * LANES), _rows(tm, H * LANES),
                   pl.BlockSpec((H, None, LANES, tm), lambda i: (0, i, 0, 0))],
        out_shape=[jax.ShapeDtypeStruct((S, D), BF16), jax.ShapeDtypeStruct((H // 2, S, LANES), F32),
                   jax.ShapeDtypeStruct((H // 2, S, LANES), F32), jax.ShapeDtypeStruct((S, D), F32),
                   jax.ShapeDtypeStruct((S, LANES), F32), jax.ShapeDtypeStruct((S, LANES), F32),
                   jax.ShapeDtypeStruct((S, LANES), F32),
                   jax.ShapeDtypeStruct((S, H * LANES), BF16), jax.ShapeDtypeStruct((S, H * LANES), BF16),
                   jax.ShapeDtypeStruct((S, H * LANES), BF16), jax.ShapeDtypeStruct((H, nb, LANES, tm), BF16)],
        scratch_shapes=[pltpu.VMEM((8, LANES), F32), pltpu.VMEM((tm, D), F32), pltpu.VMEM((tm, D), F32),
                        pltpu.VMEM((tm, D), F32)],
        compiler_params=_params(1),
    )(x1, g2, w, wf, bf, gq, gk, tri, *_scatter_matrices(H))


def _skip_tables(c, diag, gq, gk, T, G):
    nb = c.shape[0] // T
    bound = 8.0 * jnp.max(jnp.abs(gq)) * jnp.max(jnp.abs(gk))
    first, last = c[0::T, :], c[T - 1::T, :]
    lowest = jnp.maximum(jnp.min(diag.reshape(nb, T, -1), axis=1), -bound)
    idx = jnp.arange(nb)
    margin = (SKIP_LOG + bound) - lowest
    need = (last[None, :, :] <= first[:, None, :] + margin[:, None, :]) & (idx[None, :, None] < idx[:, None, None])
    need = need | (idx[None, :, None] == idx[:, None, None])
    kstart = jnp.argmax(need, axis=1)
    qend = nb - 1 - jnp.argmax(need[::-1], axis=0)
    kstart = jnp.min(kstart.reshape(2 * nb // G, G // 2, -1), axis=1)
    kstart = kstart - (kstart & 1)
    qend = jnp.max(qend.reshape(2 * nb // G, G // 2, -1), axis=1)
    return kstart.T.astype(jnp.int32), qend.T.astype(jnp.int32), bound


def _attn_fwd(kstart, qa, ka, vt, online_max):
    S = qa.shape[0]
    H = qa.shape[1] // LANES
    nb, T = vt.shape[1], vt.shape[3]
    G = 2 * nb // kstart.shape[1]
    W = G * T

    def finish(acc, shift, o_ref, lse_ref):
        a = _aug(pl.program_id(0) % 2)
        feat = lax.broadcasted_iota(jnp.int32, (LANES, 1), 0)
        l = jnp.sum(jnp.where(feat == a, acc, 0.0), axis=0, keepdims=True)
        o_ref[...] = (acc * (1.0 / l)).T
        lse_ref[...] = shift + jnp.log(l)

    def causal(st):
        return jnp.where(lax.broadcasted_iota(jnp.int32, st.shape, 0) <= lax.broadcasted_iota(jnp.int32, st.shape, 1),
                         st, NEG)

    def fast_body(ks_ref, q_ref, k_ref, vt_ref, o_ref, lse_ref, acc_ref, sa_ref, sb_ref, sc_ref):
        h, g = pl.program_id(0), pl.program_id(1)
        q = q_ref[...]
        acc_ref[...] = jnp.zeros((LANES, W), F32)

        def scores(ki, lo):
            return _dot_nt(k_ref[pl.ds(pl.multiple_of(ki * T, T), 2 * T), :], q[lo * T:, :])

        def weighted(ki, p):
            return _dot(vt_ref[ki], p[:T]) + _dot(vt_ref[ki + 1], p[T:])

        first = ks_ref[h, 2 * g + 1]
        early = jnp.minimum(ks_ref[h, 2 * g], first)

        def narrow(i, carry):
            ki = early + 2 * i
            st = _dot_nt(k_ref[pl.ds(pl.multiple_of(ki * T, T), 2 * T), :], q[:W // 2, :])
            acc_ref[:, :W // 2] += weighted(ki, jnp.exp(st).astype(BF16))
            return carry

        lax.fori_loop(0, (first - early) // 2, narrow, 0)
        steps = (g * G - first) // 2
        sa_ref[...] = scores(first, 0)

        def advance(ki, cur_ref, next_ref):
            p = jnp.exp(cur_ref[...]).astype(BF16)
            next_ref[...] = scores(ki + 2, 0)
            acc_ref[...] += weighted(ki, p)

        def loop(i, carry):
            advance(first + 4 * i, sa_ref, sb_ref)
            advance(first + 4 * i + 2, sb_ref, sa_ref)
            return carry

        lax.fori_loop(0, steps // 2, loop, 0)

        def first_own(pending_ref):
            p = jnp.exp(causal(pending_ref[...])).astype(BF16)
            if G > 2:
                sc_ref[:, :W - 2 * T] = scores(g * G + 2, 2)
            acc_ref[...] += weighted(g * G, p)

        @pl.when(steps % 2 == 1)
        def _():
            advance(g * G - 2, sa_ref, sb_ref)
            first_own(sb_ref)

        @pl.when(steps % 2 == 0)
        def _():
            first_own(sa_ref)

        if G > 2:
            acc_ref[:, 2 * T:] += weighted(g * G + 2, jnp.exp(causal(sc_ref[:, :W - 2 * T])).astype(BF16))
        for j in range(4, G, 2):
            p = jnp.exp(causal(scores(g * G + j, j))).astype(BF16)
            acc_ref[:, j * T:] += weighted(g * G + j, p)
        finish(acc_ref[...], 0.0, o_ref, lse_ref)

    def online_body(ks_ref, q_ref, k_ref, vt_ref, o_ref, lse_ref, acc_ref, m_ref):
        h, g = pl.program_id(0), pl.program_id(1)
        q = q_ref[...]
        m_ref[...] = jnp.full((8, W), NEG, F32)
        acc_ref[...] = jnp.zeros((LANES, W), F32)

        def update(st, vtb, lo):
            m_old = m_ref[0:1, lo:]
            m_new = jnp.maximum(m_old, jnp.max(st, axis=0, keepdims=True))
            p = jnp.exp(st - m_new).astype(BF16)
            acc_ref[:, lo:] = acc_ref[:, lo:] * jnp.exp(m_old - m_new) + _dot(vtb, p)
            m_ref[:, lo:] = jnp.broadcast_to(m_new, (8, W - lo))

        def loop(ki, carry):
            kb = k_ref[pl.ds(pl.multiple_of(ki * T, T), T), :]
            update(_dot_nt(kb, q), vt_ref[ki], 0)
            return carry

        lax.fori_loop(jnp.minimum(ks_ref[h, 2 * g], ks_ref[h, 2 * g + 1]), g * G, loop, 0)
        for j in range(G):
            ki = g * G + j
            kb = k_ref[pl.ds(pl.multiple_of(ki * T, T), T), :]
            update(causal(_dot_nt(kb, q[j * T:, :])), vt_ref[ki], j * T)
        finish(acc_ref[...], m_ref[0:1, :], o_ref, lse_ref)

    return pl.pallas_call(
        online_body if online_max else fast_body, name="attn_fwd_online" if online_max else "attn_fwd",
        grid_spec=pltpu.PrefetchScalarGridSpec(
            num_scalar_prefetch=1, grid=(H, nb // G),
            in_specs=[pl.BlockSpec((W, LANES), lambda h, i, ks: (i, h)),
                      pl.BlockSpec((S, LANES), lambda h, i, ks: (0, h)),
                      pl.BlockSpec((None, nb, LANES, T), lambda h, i, ks: (h, 0, 0, 0))],
            out_specs=[pl.BlockSpec((W, LANES), lambda h, i, ks: (i, h)),
                       pl.BlockSpec((None, 1, W), lambda h, i, ks: (h, 0, i))],
            scratch_shapes=[pltpu.VMEM((LANES, W), F32)] + (
                [pltpu.VMEM((8, W), F32)] if online_max else [pltpu.VMEM((2 * T, W), F32)] * 3)),
        out_shape=[jax.ShapeDtypeStruct((S, H * LANES), F32), jax.ShapeDtypeStruct((H, 1, S), F32)],
        compiler_params=_params(2),
    )(kstart, qa, ka, vt)


def _attn_out(o_aug, lse, rel, z, x1, target, w_out, qa):
    S, D = x1.shape
    H = D // HEAD_DIM
    tm = min(ROW_TILE, S)

    def body(o_ref, z_ref, x1_ref, t_ref, w_ref, q_ref, first_ref, rel_ref, lse_ref,
             dx2_ref, dx2b_ref, o2b_ref, dz_ref, doa_ref, qa2_ref, loss_ref, oc_s, do_s):
        @pl.when(pl.program_id(0) == 0)
        def _():
            loss_ref[...] = jnp.zeros((1, LANES), F32)

        lane = _lane()
        for j in range(H // 2):
            oc_s[:, LANES * j:LANES * (j + 1)] = _pair_tile(o_ref[:, 2 * LANES * j:2 * LANES * j + LANES],
                                                            o_ref[:, 2 * LANES * j + LANES:2 * LANES * (j + 1)], lane)
        oc = oc_s[...]
        zv = z_ref[...]
        sg = _sigmoid(zv)
        sz = zv * sg
        o2 = (oc * sz).astype(BF16)
        o2b_ref[...] = o2
        e = x1_ref[...] + _dot(o2, w_ref[...]) - t_ref[...]
        sq = jnp.sum(jnp.sum(e * e, axis=1, keepdims=True), axis=0, keepdims=True)
        loss_ref[...] += jnp.broadcast_to(sq * (0.5 / D), (1, LANES))
        dx2 = e * (1.0 / D)
        dx2_ref[...] = dx2
        dx2b = dx2.astype(BF16)
        dx2b_ref[...] = dx2b
        do2 = _dot_nt(dx2b, w_ref[...])
        dz_ref[...] = (do2 * oc * (sg * (1.0 + zv * (1.0 - sg)))).astype(BF16)
        do_s[...] = do2 * sz
        deltas = jnp.zeros((tm, LANES), F32)
        for hd in range(H):
            dt = _head_tile(do_s, hd, lane)
            delta = jnp.sum(dt * _head_tile(oc_s, hd, lane), axis=1, keepdims=True)
            deltas = deltas + jnp.where(lane == hd, delta, 0.0)
            doa_ref[:, LANES * hd:LANES * (hd + 1)] = dt.astype(BF16)
        doa_ref[...] += _dot(_pack3(-deltas, lane, 0.0), first_ref[...]).astype(BF16)
        lse = jnp.concatenate([lse_ref[...], jnp.zeros((LANES - H, tm), F32)], axis=0).T
        rq = rel_ref[...] - lse
        tile_lane = lax.broadcasted_iota(jnp.int32, (1, H * LANES), 1)
        extra = tile_lane % LANES - _aug((tile_lane // LANES) % 2)
        kept = jnp.where((extra >= 0) & (extra < 3), jnp.zeros((), BF16), q_ref[...])
        qa2_ref[...] = kept + _dot(_pack3(rq, lane, 0.0), first_ref[...]).astype(BF16)

    return pl.pallas_call(
        body, name="attn_out", grid=(S // tm,),
        in_specs=[_rows(tm, H * LANES), _rows(tm, D), _rows(tm, D), _rows(tm, D), _whole((D, D)),
                  _rows(tm, H * LANES), _whole((LANES, H * LANES)), _rows(tm, LANES),
                  pl.BlockSpec((H, tm), lambda i: (0, i))],
        out_specs=[_rows(tm, D), _rows(tm, D), _rows(tm, D), _rows(tm, D), _rows(tm, H * LANES),
                   _rows(tm, H * LANES), _whole((1, LANES))],
        out_shape=[jax.ShapeDtypeStruct((S, D), F32), jax.ShapeDtypeStruct((S, D), BF16),
                   jax.ShapeDtypeStruct((S, D), BF16), jax.ShapeDtypeStruct((S, D), BF16),
                   jax.ShapeDtypeStruct((S, H * LANES), BF16), jax.ShapeDtypeStruct((S, H * LANES), BF16),
                   jax.ShapeDtypeStruct((1, LANES), F32)],
        scratch_shapes=[pltpu.VMEM((tm, D), F32), pltpu.VMEM((tm, D), F32)],
        compiler_params=_params(1),
    )(o_aug, z, x1, target, w_out, qa, _scatter_matrices(H)[0], rel, lse)


def _attn_bwd(qend, qa2, doa, ka, va, T):
    S = qa2.shape[0]
    H = qa2.shape[1] // LANES
    nb = S // T
    G = 2 * nb // qend.shape[1]
    W = G * T

    def body(qe_ref, q_ref, do_ref, k_ref, v_ref, dq_ref, dk_ref, dv_ref, dkt_acc, dvt_acc):
        h, g = pl.program_id(0), pl.program_id(1)

        @pl.when(g == 0)
        def _():
            dq_ref[...] = jnp.zeros((S, LANES), F32)

        kb = k_ref[...]
        vb = v_ref[...]
        dkt_acc[...] = jnp.zeros((LANES, W), F32)
        dvt_acc[...] = jnp.zeros((LANES, W), F32)

        def step(qi, c0, c1, masked):
            rows = pl.ds(pl.multiple_of(qi * T, T), 2 * T)
            qb = q_ref[rows, :]
            dob = do_ref[rows, :]
            s = _dot_nt(qb, kb[c0:c1])
            if masked:
                query = lax.broadcasted_iota(jnp.int32, s.shape, 0) + (c1 - 2 * T)
                s = jnp.where(lax.broadcasted_iota(jnp.int32, s.shape, 1) <= query, s, NEG)
            p = jnp.exp(s)
            ds = (p * _dot_nt(dob, vb[c0:c1])).astype(BF16)
            dvt_acc[:, c0:c1] += _dot(dob.astype(F32).T.astype(BF16), p.astype(BF16))
            dkt_acc[:, c0:c1] += _dot(qb.astype(F32).T.astype(BF16), ds)
            dq_ref[rows, :] += _dot(ds, kb[c0:c1])

        for m in range(G // 2):
            step(g * G + 2 * m, 0, (m + 1) * 2 * T, True)
        first = g * G + G
        n_all = jnp.maximum((qe_ref[h, 2 * g] - first + 2) // 2, 0)
        second = first + 2 * n_all

        def all_keys(i, carry):
            step(first + 2 * i, 0, W, False)
            return carry

        def late_keys(i, carry):
            step(second + 2 * i, W // 2, W, False)
            return carry

        lax.fori_loop(0, n_all, all_keys, 0)
        lax.fori_loop(0, (qe_ref[h, 2 * g + 1] - second + 2) // 2, late_keys, 0)
        dk_ref[...] = dkt_acc[...].T
        dv_ref[...] = dvt_acc[...].T.astype(BF16)

    heads = pl.BlockSpec((None, W, LANES), lambda h, i, qe: (h, i, 0))
    return pl.pallas_call(
        body, name="attn_bwd",
        grid_spec=pltpu.PrefetchScalarGridSpec(
            num_scalar_prefetch=1, grid=(H, nb // G),
            in_specs=[pl.BlockSpec((S, LANES), lambda h, i, qe: (0, h)), pl.BlockSpec((S, LANES), lambda h, i, qe: (0, h)),
                      pl.BlockSpec((W, LANES), lambda h, i, qe: (i, h)), pl.BlockSpec((W, LANES), lambda h, i, qe: (i, h))],
            out_specs=[pl.BlockSpec((None, S, LANES), lambda h, i, qe: (h, 0, 0)), heads, heads],
            scratch_shapes=[pltpu.VMEM((LANES, W), F32), pltpu.VMEM((LANES, W), F32)]),
        out_shape=[jax.ShapeDtypeStruct((H, S, LANES), F32), jax.ShapeDtypeStruct((H, S, LANES), F32),
                   jax.ShapeDtypeStruct((H, S, LANES), BF16)],
        compiler_params=_params(2),
    )(qend, qa2, doa, ka, va)


def _attn_proj_bwd(dqt, dka, dva, qraw, kraw, dz, f, gq, gk):
    S, D = dz.shape
    H = D // HEAD_DIM
    tm = min(ROW_TILE, S)
    last = S // tm - 1
    tri = (lax.broadcasted_iota(jnp.int32, (tm, tm), 1) >= lax.broadcasted_iota(jnp.int32, (tm, tm), 0)).astype(BF16)

    def body(dq_ref, dk_ref, dv_ref, q_ref, k_ref, dz_ref, f_ref, gq_ref, gk_ref, tri_ref, ones_ref,
             dproj_ref, small_ref, carry, pairs):
        @pl.when(pl.program_id(0) == 0)
        def _():
            small_ref[...] = jnp.zeros((8, LANES), F32)
            carry[...] = jnp.zeros((8, LANES), F32)

        lane = _lane()

        def head_pair(j, acc):
            dcs, dgq, dgk = acc
            dq2, dk2 = [], []
            q_pair, k_pair = q_ref[j], k_ref[j]
            for parity in (0, 1):
                hd = 2 * j + parity
                own, a = _own(lane, parity), _aug(parity)
                dqf = dq_ref[hd]
                dqn = jnp.where(own, dqf * Q_SCALE, 0.0)
                d, dg = _head_rms_bwd(dqn, jnp.where(own, q_pair, 0.0), gq_ref[...], ones_ref[...])
                dq2.append(d)
                dgq = dgq + dg
                dkt = dk_ref[hd]
                dcs = dcs + jnp.where(lane == hd, _col(dqf, lane, a) - _col(dkt, lane, a + 3), 0.0)
                d, dg = _head_rms_bwd(jnp.where(own, dkt, 0.0), jnp.where(own, k_pair, 0.0), gk_ref[...], ones_ref[...])
                dk2.append(d)
                dgk = dgk + dg
            pairs[0, j] = _pair_tile(*dq2, lane).astype(BF16)
            pairs[1, j] = _pair_tile(*dk2, lane).astype(BF16)
            pairs[2, j] = _pair_tile(dv_ref[2 * j], dv_ref[2 * j + 1], lane)
            return dcs, dgq, dgk

        zero = jnp.zeros((1, LANES), F32)
        dcs, dgq, dgk = lax.fori_loop(0, H // 2, head_pair, (jnp.zeros((tm, LANES), F32), zero, zero))
        for part in range(3):
            for j in range(H // 2):
                dproj_ref[:, part * D + LANES * j:part * D + LANES * (j + 1)] = pairs[part, j]
        dproj_ref[:, 3 * D:4 * D] = dz_ref[...]
        dlogf = _dot01(tri_ref[...], dcs) + carry[0:1, :]
        carry[...] = jnp.broadcast_to(dlogf[0:1, :], (8, LANES))
        df = dlogf * (1.0 / (1.0 + jnp.exp(f_ref[...])))
        dproj_ref[:, 4 * D:4 * D + LANES] = df.astype(BF16)
        small_ref[0:1, :] += jnp.sum(df, axis=0, keepdims=True)
        small_ref[1:2, :] += dgq
        small_ref[2:3, :] += dgk

    W = 4 * D + LANES
    heads = pl.BlockSpec((H, tm, LANES), lambda i: (0, last - i, 0))
    head_pairs = pl.BlockSpec((H // 2, tm, LANES), lambda i: (0, last - i, 0))
    return pl.pallas_call(
        body, name="attn_proj_bwd", grid=(S // tm,),
        in_specs=[heads, heads, heads, head_pairs, head_pairs,
                  _rows(tm, D, last), _rows(tm, LANES, last), _whole((1, LANES)), _whole((1, LANES)),
                  _whole((tm, tm)), _whole((LANES, LANES))],
        out_specs=[_rows(tm, W, last), _whole((8, LANES))],
        out_shape=[jax.ShapeDtypeStruct((S, W), BF16), jax.ShapeDtypeStruct((8, LANES), F32)],
        scratch_shapes=[pltpu.VMEM((8, LANES), F32), pltpu.VMEM((3, H // 2, tm, LANES), BF16)],
        compiler_params=_params(1),
    )(dqt, dka, dva, qraw, kraw, dz, f, gq, gk, tri, jnp.ones((LANES, LANES), BF16))


def _matmul_tn(a, b, col0, n, tn, name, stacked=False):
    S, M = a.shape
    ts = min(TN_ROWS, S)
    off = col0 // tn

    def body(a_ref, b_ref, o_ref):
        @pl.when(pl.program_id(1) == 0)
        def _():
            o_ref[...] = jnp.zeros((M, tn), F32)

        o_ref[...] += _dot_tn(a_ref[...], b_ref[...])

    if stacked:
        out_spec, out_shape = pl.BlockSpec((None, M, tn), lambda j, s: (j, 0, 0)), (n // tn, M, tn)
    else:
        out_spec, out_shape = pl.BlockSpec((M, tn), lambda j, s: (0, j)), (M, n)
    return pl.pallas_call(
        body, name=name, grid=(n // tn, S // ts),
        in_specs=[pl.BlockSpec((ts, M), lambda j, s: (s, 0)), pl.BlockSpec((ts, tn), lambda j, s: (s, off + j))],
        out_specs=out_spec, out_shape=jax.ShapeDtypeStruct(out_shape, F32),
        compiler_params=_params(2),
    )(a, b)


def _adam_update(gv, w_ref, m_ref, v_ref, d_ref, m2_ref, v2_ref):
    m2 = ADAM_B1 * m_ref[...] + (1.0 - ADAM_B1) * gv
    v2 = ADAM_B2 * v_ref[...] + (1.0 - ADAM_B2) * (gv * gv)
    m2_ref[...] = m2
    v2_ref[...] = v2
    m_hat = m2 / (1.0 - ADAM_B1 ** ADAM_STEP)
    v_hat = v2 / (1.0 - ADAM_B2 ** ADAM_STEP)
    d_ref[...] = -ADAM_LR * (m_hat / (jnp.sqrt(v_hat) + ADAM_EPS) + ADAM_WD * w_ref[...])


def _adamw(w, g, m, v, name):
    r, c = w.shape
    tr = ROW_TILE if r % ROW_TILE == 0 else r

    def body(w_ref, g_ref, m_ref, v_ref, d_ref, m2_ref, v2_ref):
        _adam_update(g_ref[...], w_ref, m_ref, v_ref, d_ref, m2_ref, v2_ref)

    spec = _rows(tr, c)
    return pl.pallas_call(
        body, name=name, grid=(r // tr,), in_specs=[spec] * 4, out_specs=[spec] * 3,
        out_shape=[jax.ShapeDtypeStruct((r, c), F32)] * 3, compiler_params=_params(1),
    )(w, g, m, v)


def _adamw_halves(w, mine, other, m, v, core, name):
    r, c = mine.shape
    tr = ROW_TILE if r % ROW_TILE == 0 else r
    per = r // tr

    def body(core_ref, w_ref, mine_ref, other_ref, m_ref, v_ref, g_ref, d_ref, m2_ref, v2_ref):
        gv = jnp.where(pl.program_id(0) // per == core_ref[0], mine_ref[...], other_ref[...])
        g_ref[...] = gv
        _adam_update(gv, w_ref, m_ref, v_ref, d_ref, m2_ref, v2_ref)

    full = pl.BlockSpec((tr, c), lambda i, core: (i, 0))
    half = pl.BlockSpec((tr, c), lambda i, core: (i % per, 0))
    return pl.pallas_call(
        body, name=name,
        grid_spec=pltpu.PrefetchScalarGridSpec(num_scalar_prefetch=1, grid=(2 * per,),
                                               in_specs=[full, half, half, full, full], out_specs=[full] * 4),
        out_shape=[jax.ShapeDtypeStruct((2 * r, c), F32)] * 4, compiler_params=_params(1),
    )(core, w, mine, other, m, v)


def _local_step(x, target, g1, w_in, conv_w, w_out, g2, wa_in, b_f, gq, gk, wa_out):
    S, D = x.shape
    H = D // HEAD_DIM
    w_qkvz = wa_in[:, :4 * D]
    wf = jnp.pad(wa_in[:, 4 * D:], ((0, 0), (0, LANES - H)))
    bf = jnp.pad(b_f, ((0, 0), (0, LANES - H)))
    gq128 = jnp.concatenate([gq, gq], axis=1)
    gk128 = jnp.concatenate([gk, gk], axis=1)

    proj, h1, yc, y, x1 = _conv_fwd(x, g1, w_in, conv_w, w_out)
    h2, qraw, kraw, z, f, c, rel, qa, ka, va, vt = _attn_front(x1, g2, w_qkvz, wf, bf, gq128, gk128)
    T = vt.shape[3]
    kstart, qend, bound = _skip_tables(c[:, :H], (c - rel)[:, :H], gq, gk, T, min(ATT_GROUP, S // T))
    o_aug, lse = lax.cond(2.0 * bound <= PLAIN_EXP_MAX, functools.partial(_attn_fwd, online_max=False),
                          functools.partial(_attn_fwd, online_max=True), kstart, qa, ka, vt)
    dx2, dx2b, o2b, dz, doa, qa2, loss = _attn_out(o_aug, lse.reshape(H, S), rel, z, x1, target, wa_out, qa)
    dqt, dka, dva = _attn_bwd(qend, qa2, doa, ka, va, T)
    dproj2, small = _attn_proj_bwd(dqt, dka, dva, qraw, kraw, dz, f, gq128, gk128)
    dproj1, dx, dx1b, dg1, dcw, dg2 = _conv_bwd(dproj2, w_qkvz, wf, x1, g2, dx2, x, g1, w_in, w_out, conv_w, proj, yc)

    tn = min(1024, D)
    dwa_out = _matmul_tn(o2b, dx2b, 0, D, tn, "dw_attn_out")
    dwa_in = jnp.concatenate([_matmul_tn(h2, dproj2, 0, 4 * D, tn, "dw_attn_in"),
                              _matmul_tn(h2, dproj2, 4 * D, LANES, LANES, "dw_attn_f")[:, :H]], axis=1)
    dw_out = _matmul_tn(y, dx1b, 0, D, tn, "dw_conv_out")
    dw_in = _matmul_tn(h1, dproj1, 0, 4 * D, D, "dw_conv_in", stacked=True)
    grads = dict(conv_norm_g=dg1, conv_w_in=dw_in, conv_w=dcw, conv_w_out=dw_out, attn_norm_g=dg2,
                 attn_w_in=dwa_in, attn_b_f=small[0:1, :H],
                 attn_q_norm_g=small[1:2, :HEAD_DIM] + small[1:2, HEAD_DIM:],
                 attn_k_norm_g=small[2:3, :HEAD_DIM] + small[2:3, HEAD_DIM:], attn_w_out=dwa_out)
    return loss[0, 0], dx, grads


def _coords():
    return lax.axis_index("x"), lax.axis_index("y"), lax.axis_index("c")


def _at(ref, idx):
    return ref.at[idx] if idx else ref


def _other_chips(x, y):
    return [(1 - x, y), (x, 1 - y), (1 - x, 1 - y)]


def _all_gather(halved, whole):
    nh, nw = len(halved), len(whole)

    def body(*refs):
        src_h, src_w = refs[:nh], refs[nh:nh + nw]
        out_h, out_w = refs[nh + nw:2 * nh + nw], refs[2 * nh + nw:2 * (nh + nw)]
        send_h, recv_h, send_w, recv_w = refs[2 * (nh + nw):]
        x, y, c = _coords()
        mine = 2 * x + y
        sibling = (x, y, 1 - c)
        chips = _other_chips(x, y)

        def copy_h(a, k, chip, half, to, src=None):
            dst = out_h[a].at[chip, half]
            return pltpu.make_async_remote_copy(src_ref=dst if src is None else src, dst_ref=dst,
                                                send_sem=send_h.at[a, k], recv_sem=recv_h.at[a, k],
                                                device_id=to, device_id_type=MESH)

        def copy_w(a, k, chip, to):
            return pltpu.make_async_remote_copy(src_ref=src_w[a], dst_ref=out_w[a].at[chip],
                                                send_sem=send_w.at[a, k], recv_sem=recv_w.at[a, k],
                                                device_id=to, device_id_type=MESH)

        first = [copy_h(a, k, mine, c, (*chip, c), src=src_h[a].at[c]) for a in range(nh) for k, chip in enumerate(chips)]
        first += [copy_w(a, k, mine, (*chip, c)) for a in range(nw) for k, chip in enumerate(chips)]
        own = [pltpu.make_async_remote_copy(src_ref=src_h[a], dst_ref=out_h[a].at[mine], send_sem=send_h.at[a, 6],
                                            recv_sem=recv_h.at[a, 6], device_id=sibling, device_id_type=MESH)
               for a in range(nh)]
        own += [copy_w(a, 3, mine, sibling) for a in range(nw)]
        for cp in first + own:
            cp.start()
        passed = []
        for a in range(nh):
            for k, (px, py) in enumerate(chips):
                copy_h(a, k, 2 * px + py, c, (x, y, c)).wait_recv()
                cp = copy_h(a, 3 + k, 2 * px + py, c, sibling)
                cp.start()
                passed.append(cp)
        for a in range(nh):
            for k, (px, py) in enumerate(chips):
                copy_h(a, 3 + k, 2 * px + py, 1 - c, (x, y, c)).wait_recv()
        for a in range(nw):
            for k, (px, py) in enumerate(chips):
                copy_w(a, k, 2 * px + py, (x, y, c)).wait_recv()
        for cp in own:
            cp.wait_recv()
        for cp in first + passed + own:
            cp.wait_send()

    out_shape = [jax.ShapeDtypeStruct((4,) + a.shape, a.dtype) for a in list(halved) + list(whole)]
    return pl.pallas_call(
        body, name="gather_weights", in_specs=[ANY] * (nh + nw), out_specs=[ANY] * (nh + nw), out_shape=out_shape,
        scratch_shapes=[pltpu.SemaphoreType.DMA((nh, 7)), pltpu.SemaphoreType.DMA((nh, 7)),
                        pltpu.SemaphoreType.DMA((nw, 4)), pltpu.SemaphoreType.DMA((nw, 4))],
    )(*halved, *whole)


def _exchange(name, srcs, lands, copies, local_copies):
    ns, nl, n, nloc = len(srcs), len(lands), len(copies), len(local_copies)

    def body(*refs):
        src, land = refs[:ns], refs[ns:ns + nl]
        send, recv, local_sem = refs[ns + nl:]
        me = _coords()
        started = []
        for k, (si, s_at, li, l_at, ci) in enumerate(local_copies):
            cp = pltpu.make_async_copy(_at(src[si], s_at(*me)), _at(land[li], l_at(*me)), local_sem.at[k])
            cp.start()
            started.append(cp)
        remote = []
        for k, (si, s_at, li, l_at, peer) in enumerate(copies):
            cp = pltpu.make_async_remote_copy(src_ref=_at(src[si], s_at(*me)), dst_ref=_at(land[li], l_at(*me)),
                                              send_sem=send.at[k], recv_sem=recv.at[k],
                                              device_id=peer(*me), device_id_type=MESH)
            cp.start()
            remote.append(cp)
        for cp in remote:
            cp.wait()
        for cp in started:
            cp.wait()

    return pl.pallas_call(
        body, name=name, in_specs=[ANY] * ns, out_specs=[ANY] * nl, out_shape=list(lands),
        scratch_shapes=[pltpu.SemaphoreType.DMA((n,)), pltpu.SemaphoreType.DMA((n,)),
                        pltpu.SemaphoreType.DMA((max(nloc, 1),))],
    )(*srcs)


def _add_pairs(a, b, core, name):
    _, r, cols = b.shape
    tr = ROW_TILE if r % ROW_TILE == 0 else r

    def body(core_ref, a_ref, b_ref, o_ref, ob_ref):
        s = a_ref[...] + b_ref[...]
        o_ref[...] = s
        ob_ref[...] = s.astype(BF16)

    spec = pl.BlockSpec((None, tr, cols), lambda j, i, core: (j, i, 0))
    return pl.pallas_call(
        body, name=name,
        grid_spec=pltpu.PrefetchScalarGridSpec(
            num_scalar_prefetch=1, grid=(4, r // tr),
            in_specs=[pl.BlockSpec((None, None, tr, cols), lambda j, i, core: (j, core[0], i, 0)), spec],
            out_specs=[spec, spec]),
        out_shape=[jax.ShapeDtypeStruct(b.shape, F32), jax.ShapeDtypeStruct(b.shape, BF16)],
        compiler_params=_params(2),
    )(core, a, b)


def _sum_chips(own, landed, name):
    _, r, cols = landed.shape
    tr = ROW_TILE if r % ROW_TILE == 0 else r

    def body(own_ref, land_ref, o_ref):
        acc = own_ref[...]
        for j in range(3):
            acc = acc + land_ref[j].astype(F32)
        o_ref[...] = acc

    return pl.pallas_call(
        body, name=name, grid=(r // tr,),
        in_specs=[_rows(tr, cols), pl.BlockSpec((3, tr, cols), lambda i: (0, i, 0))], out_specs=_rows(tr, cols),
        out_shape=jax.ShapeDtypeStruct((r, cols), F32), compiler_params=_params(1),
    )(own, landed)


def _sum_devices(landed, name):
    def body(l_ref, o_ref):
        acc = l_ref[0]
        for j in range(1, 8):
            acc = acc + l_ref[j]
        o_ref[...] = acc

    return pl.pallas_call(body, name=name, out_shape=jax.ShapeDtypeStruct(landed.shape[1:], F32))(landed)


def _reduce_gradients(big, small):
    nb = len(big)
    x, y, c = _coords()
    mine = 2 * x + y
    flips = [(fx, fy, fc) for fx in (0, 1) for fy in (0, 1) for fc in (0, 1) if fx or fy or fc]

    def flip(fx, fy, fc):
        return lambda x, y, c: (x ^ fx, y ^ fy, c ^ fc)

    copies = [(a, (lambda j: lambda x, y, c: (j, 1 - c))(j), a, (lambda j: lambda x, y, c: (j,))(j), flip(0, 0, 1))
              for a in range(nb) for j in range(4)]
    copies += [(nb, lambda x, y, c: (), nb, lambda x, y, c: (4 * x + 2 * y + c,), flip(*f)) for f in flips]
    lands = [jax.ShapeDtypeStruct((4,) + g.shape[2:], F32) for g in big] + [jax.ShapeDtypeStruct((8,) + small.shape, F32)]
    local = [(nb, lambda x, y, c: (), nb, lambda x, y, c: (4 * x + 2 * y + c,), None)]
    landed = _exchange("swap_halves", list(big) + [small], lands, copies, local)
    small_sum = _sum_devices(landed[nb], "sum_small")

    chip_f32, chip_bf16 = [], []
    for a in range(nb):
        s, sb = _add_pairs(big[a], landed[a], jnp.reshape(c, (1,)).astype(jnp.int32), f"add_cores_{a}")
        chip_f32.append(s)
        chip_bf16.append(sb)

    chip_flips = [(1, 0), (0, 1), (1, 1)]
    copies = [(a, (lambda f: lambda x, y, c: (2 * (x ^ f[0]) + (y ^ f[1]),))(f), a, (lambda k: lambda x, y, c: (k,))(k),
               flip(f[0], f[1], 0)) for a in range(nb) for k, f in enumerate(chip_flips)]
    lands = [jax.ShapeDtypeStruct((3,) + g.shape[1:], BF16) for g in chip_bf16]
    landed = _exchange("send_chip_sums", chip_bf16, lands, copies, [])
    totals = [_sum_chips(lax.dynamic_index_in_dim(chip_f32[a], mine, axis=0, keepdims=False), landed[a],
                         f"sum_chips_{a}") for a in range(nb)]

    copies = [(a, lambda x, y, c: (), a, lambda x, y, c: (), flip(0, 0, 1)) for a in range(nb)]
    lands = [jax.ShapeDtypeStruct(t.shape, F32) for t in totals]
    return list(zip(totals, _exchange("swap_sums", totals, lands, copies, []))), small_sum


def kernel(x, conv_norm_g, conv_w_in, conv_w, conv_w_out, attn_norm_g, attn_w_in, attn_b_f, attn_q_norm_g, attn_k_norm_g, attn_w_out, loss_target, m_conv_norm_g, m_conv_w_in, m_conv_w, m_conv_w_out, m_attn_norm_g, m_attn_w_in, m_attn_b_f, m_attn_q_norm_g, m_attn_k_norm_g, m_attn_w_out, v_conv_norm_g, v_conv_w_in, v_conv_w, v_conv_w_out, v_attn_norm_g, v_attn_w_in, v_attn_b_f, v_attn_q_norm_g, v_attn_k_norm_g, v_attn_w_out):
    xi, yi, _ = _coords()
    chip = 2 * xi + yi
    D = x.shape[2]
    H = D // HEAD_DIM
    names = ["conv_norm_g", "conv_w_in", "conv_w", "conv_w_out", "attn_norm_g", "attn_w_in", "attn_b_f",
             "attn_q_norm_g", "attn_k_norm_g", "attn_w_out"]
    weights = dict(zip(names, [conv_norm_g, conv_w_in, conv_w, conv_w_out, attn_norm_g, attn_w_in, attn_b_f,
                               attn_q_norm_g, attn_k_norm_g, attn_w_out]))
    m_in = dict(zip(names, [m_conv_norm_g, m_conv_w_in, m_conv_w, m_conv_w_out, m_attn_norm_g, m_attn_w_in,
                            m_attn_b_f, m_attn_q_norm_g, m_attn_k_norm_g, m_attn_w_out]))
    v_in = dict(zip(names, [v_conv_norm_g, v_conv_w_in, v_conv_w, v_conv_w_out, v_attn_norm_g, v_attn_w_in,
                            v_attn_b_f, v_attn_q_norm_g, v_attn_k_norm_g, v_attn_w_out]))
    weights = {k: w[0] for k, w in weights.items()}
    m_in = {k: w[0] for k, w in m_in.items()}
    v_in = {k: w[0] for k, w in v_in.items()}

    big_names = ["conv_w_in", "attn_w_in", "conv_w_out", "attn_w_out"]
    halved = [weights[k].astype(BF16).reshape(2, weights[k].shape[0] // 2, weights[k].shape[1]) for k in big_names]
    q = D // 4
    small_w = jnp.concatenate([weights["conv_w"], weights["attn_norm_g"][None, :], jnp.zeros((4, q), F32)], axis=0)
    g_in, ga_in, g_out, ga_out, g_small = _all_gather(halved, [small_w])
    w_in = g_in.reshape(4, D, D)
    wa_in = ga_in.reshape(4, D, D + H // 4).transpose(1, 0, 2).reshape(D, 4 * D + H)
    w_out = g_out.reshape(D, D)
    wa_out = ga_out.reshape(D, D)
    conv_w_full = g_small[:, 0:3, :].transpose(1, 0, 2).reshape(3, D)
    attn_g_full = g_small[:, 3, :].reshape(1, D)

    loss_part, grad_x, grads = _local_step(x[0], loss_target[0], weights["conv_norm_g"][None, :], w_in, conv_w_full,
                                           w_out, attn_g_full, wa_in, weights["attn_b_f"][None, :],
                                           weights["attn_q_norm_g"][None, :], weights["attn_k_norm_g"][None, :], wa_out)

    big = [grads["conv_w_in"].reshape(4, 2, D // 2, D),
           grads["attn_w_in"].reshape(D, 4, D + H // 4).transpose(1, 0, 2).reshape(4, 2, D // 2, D + H // 4),
           grads["conv_w_out"].reshape(4, 2, D // 8, D), grads["attn_w_out"].reshape(4, 2, D // 8, D)]
    tail = jnp.concatenate([grads["attn_b_f"], grads["attn_q_norm_g"], grads["attn_k_norm_g"],
                            jnp.reshape(loss_part, (1, 1)), jnp.zeros((1, D - H - 2 * HEAD_DIM - 1), F32)], axis=1)
    small = jnp.concatenate([grads["conv_norm_g"], grads["conv_w"], grads["attn_norm_g"], tail,
                             jnp.zeros((2, D), F32)], axis=0)
    reduced, small_sum = _reduce_gradients(big, small)
    final = {}
    final["conv_norm_g"] = small_sum[0]
    final["conv_w"] = lax.dynamic_slice_in_dim(small_sum[1:4], chip * q, q, axis=1)
    final["attn_norm_g"] = lax.dynamic_slice_in_dim(small_sum[4], chip * q, q, axis=0)
    final["attn_b_f"] = small_sum[5, :H]
    final["attn_q_norm_g"] = small_sum[5, H:H + HEAD_DIM]
    final["attn_k_norm_g"] = small_sum[5, H + HEAD_DIM:H + 2 * HEAD_DIM]
    loss = small_sum[5, H + 2 * HEAD_DIM]

    delta, new_m, new_v = {}, {}, {}
    core = jnp.reshape(lax.axis_index("c"), (1,)).astype(jnp.int32)
    for k, (mine, other) in zip(big_names, reduced):
        final[k], delta[k], new_m[k], new_v[k] = _adamw_halves(weights[k], mine, other, m_in[k], v_in[k], core,
                                                               "adamw_" + k)
    for k in names:
        if k in big_names:
            continue
        shape = weights[k].shape
        as2d = (lambda a: a.reshape(1, -1)) if len(shape) == 1 else (lambda a: a)
        d, m2, v2 = _adamw(as2d(weights[k]), as2d(final[k]), as2d(m_in[k]), as2d(v_in[k]), "adamw_" + k)
        delta[k], new_m[k], new_v[k] = d.reshape(shape), m2.reshape(shape), v2.reshape(shape)
    lead = lambda a: a[None]
    return (loss, grad_x[None], *[lead(final[k]) for k in names], *[lead(delta[k]) for k in names],
            *[lead(new_m[k]) for k in names], *[lead(new_v[k]) for k in names])
```

```python
import functools

import jax
import jax.numpy as jnp
from jax import lax
from jax.experimental import pallas as pl
from jax.experimental.pallas import tpu as pltpu

F32 = jnp.float32
BF16 = jnp.bfloat16
HEAD_DIM = 64
LANES = 128
RMS_EPS = 1e-6
NEG = -1e30
Q_SCALE = 0.125
ROW_TILE = 256
CONV_TILE = 512
ATT_GROUP = 4
SKIP_LOG = 106.0
PLAIN_EXP_MAX = 60.0
TN_ROWS = 2048
VMEM_LIMIT = 56 << 20
ADAM_LR, ADAM_B1, ADAM_B2, ADAM_EPS, ADAM_WD, ADAM_STEP = 0.001, 0.9, 0.999, 1e-08, 0.01, 10
MESH = pl.DeviceIdType.MESH
ANY = pl.BlockSpec(memory_space=pl.ANY)


def _lane():
    return lax.broadcasted_iota(jnp.int32, (1, LANES), 1)


def _split3(x):
    hi = x.astype(BF16).astype(F32)
    r = x - hi
    mid = r.astype(BF16).astype(F32)
    lo = (r - mid).astype(BF16).astype(F32)
    return hi, mid, lo


STAT_STRIDE = 16
ONE_LANE = 3 * STAT_STRIDE


def _pack3(x, lane, one):
    hi, mid, lo = _split3(x)
    packed = hi + pltpu.roll(mid, STAT_STRIDE, 1) + pltpu.roll(lo, 2 * STAT_STRIDE, 1)
    return jnp.where(lane == ONE_LANE, one, packed).astype(BF16)


def _scatter_matrices(H):
    rows = lax.broadcasted_iota(jnp.int32, (LANES, H * LANES), 0)
    cols = lax.broadcasted_iota(jnp.int32, (LANES, H * LANES), 1)
    head, within = cols // LANES, cols % LANES
    extra = within - _aug(head % 2)
    term = (rows < ONE_LANE) & (rows % STAT_STRIDE == head)
    first = ((term & (extra == rows // STAT_STRIDE)) | ((rows == ONE_LANE) & (extra >= 3) & (extra < 6)))
    second = ((term & (extra - 3 == rows // STAT_STRIDE)) | ((rows == ONE_LANE) & (extra >= 0) & (extra < 3)))
    return first.astype(BF16), second.astype(BF16)


def _put(base, lane, start, parts):
    for j, p in enumerate(parts):
        base = jnp.where(lane == start + j, p, base)
    return base


def _col(x, lane, idx):
    return jnp.sum(jnp.where(lane == idx, x, 0.0), axis=1, keepdims=True)


def _feat(parity):
    return HEAD_DIM * parity


def _aug(parity):
    return HEAD_DIM * (1 - parity)


def _own(lane, parity):
    return (lane >= _feat(parity)) & (lane < _feat(parity) + HEAD_DIM)


def _head_tile(ref, hd, lane):
    j = hd // 2
    return jnp.where(_own(lane, hd % 2), ref[:, LANES * j:LANES * (j + 1)], 0.0)


def _pair_tile(even, odd, lane):
    return jnp.where(lane < HEAD_DIM, even, odd)


def _sigmoid(x):
    return 0.5 * jnp.tanh(0.5 * x) + 0.5


def _dot(a, b):
    return jnp.dot(a, b, preferred_element_type=F32)


def _dot_nt(a, b):
    return lax.dot_general(a, b, (((1,), (1,)), ((), ())), preferred_element_type=F32)


def _dot_tn(a, b):
    return lax.dot_general(a, b, (((0,), (0,)), ((), ())), preferred_element_type=F32)


def _dot01(tri, x):
    hi, mid, lo = _split3(x)
    return _dot(tri, hi.astype(BF16)) + _dot(tri, mid.astype(BF16)) + _dot(tri, lo.astype(BF16))


def _rms_bwd(dh, x, g):
    inv = lax.rsqrt(jnp.mean(x * x, axis=-1, keepdims=True) + RMS_EPS)
    xh = x * inv
    dxn = dh * g
    dx = inv * (dxn - xh * jnp.mean(dxn * xh, axis=-1, keepdims=True))
    return dx, jnp.sum(dh * xh, axis=0, keepdims=True)


def _head_rms_bwd(dn, t, g, ones):
    sq = t * t
    hi = sq.astype(BF16)
    lo = (sq - hi.astype(F32)).astype(BF16)
    inv = lax.rsqrt((_dot(hi, ones) + _dot(lo, ones)) * (1.0 / HEAD_DIM) + RMS_EPS)
    th = t * inv
    gd = dn * g
    d = inv * (gd - th * (jnp.sum(gd * th, axis=1, keepdims=True) * (1.0 / HEAD_DIM)))
    return d, jnp.sum(dn * th, axis=0, keepdims=True)


def _params(n_grid):
    return pltpu.CompilerParams(dimension_semantics=("arbitrary",) * n_grid, vmem_limit_bytes=VMEM_LIMIT)


def _rows(tm, cols, rev=None):
    if rev is None:
        return pl.BlockSpec((tm, cols), lambda i: (i, 0))
    return pl.BlockSpec((tm, cols), lambda i: (rev - i, 0))


def _whole(shape, buffers=None):
    mode = {} if buffers is None else dict(pipeline_mode=pl.Buffered(buffers))
    return pl.BlockSpec(shape, lambda *_: (0,) * len(shape), **mode)


def _conv_fwd(x, g1, w_in, conv_w, w_out):
    S, D = x.shape
    tm = min(CONV_TILE, S)
    sub = min(ROW_TILE, tm)

    def body(x_ref, g_ref, win_ref, cw_ref, wout_ref, proj_ref, h_ref, yc_ref, y_ref, x1_ref, prev_u):
        @pl.when(pl.program_id(0) == 0)
        def _():
            prev_u[...] = jnp.zeros((sub, D), F32)

        for r in range(0, tm, sub):
            rows = slice(r, r + sub)
            xv = x_ref[rows, :]
            inv = lax.rsqrt(jnp.mean(xv * xv, axis=-1, keepdims=True) + RMS_EPS)
            h = (xv * inv * g_ref[...]).astype(BF16)
            h_ref[rows, :] = h
            for j in range(4):
                proj_ref[rows, j * D:(j + 1) * D] = _dot(h, win_ref[j])
            u = proj_ref[rows, D:2 * D] * proj_ref[rows, 2 * D:3 * D]
            pu = prev_u[...]
            row = lax.broadcasted_iota(jnp.int32, (sub, 1), 0)
            u1 = jnp.where(row < 1, pltpu.roll(pu, 1, 0), pltpu.roll(u, 1, 0))
            u2 = jnp.where(row < 2, pltpu.roll(pu, 2, 0), pltpu.roll(u, 2, 0))
            prev_u[...] = u
            w = cw_ref[...]
            yc = w[2:3] * u + w[1:2] * u1 + w[0:1] * u2
            yc_ref[rows, :] = yc
            z = proj_ref[rows, 3 * D:4 * D]
            y = (proj_ref[rows, 0:D] * yc * (z * _sigmoid(z))).astype(BF16)
            y_ref[rows, :] = y
            x1_ref[rows, :] = xv + _dot(y, wout_ref[...])

    return pl.pallas_call(
        body, name="conv_fwd", grid=(S // tm,),
        in_specs=[_rows(tm, D), _whole((1, D)), _whole((4, D, D), 1), _whole((3, D)), _whole((D, D), 1)],
        out_specs=[_rows(tm, 4 * D), _rows(tm, D), _rows(tm, D), _rows(tm, D), _rows(tm, D)],
        out_shape=[jax.ShapeDtypeStruct((S, 4 * D), F32), jax.ShapeDtypeStruct((S, D), BF16),
                   jax.ShapeDtypeStruct((S, D), F32), jax.ShapeDtypeStruct((S, D), BF16),
                   jax.ShapeDtypeStruct((S, D), F32)],
        scratch_shapes=[pltpu.VMEM((sub, D), F32)],
        compiler_params=_params(1),
    )(x, g1, w_in, conv_w, w_out)


def _conv_bwd(dproj2, wa, wf, x1, g2, dx2, x, g1, w_in, w_out, conv_w, proj, yc):
    S, D = x.shape
    tm = min(ROW_TILE, S)
    sub = min(ROW_TILE, tm)
    last = S // tm - 1

    def body(dp2_ref, wa_ref, wf_ref, x1_ref, g2_ref, dx2_ref, x_ref, g_ref, win_ref, wout_ref, cw_ref, proj_ref, yc_ref,
             dproj_ref, dx_ref, dx1b_ref, dg_ref, dcw_ref, dg2_ref, next_d):
        @pl.when(pl.program_id(0) == 0)
        def _():
            dg_ref[...] = jnp.zeros((1, D), F32)
            dcw_ref[...] = jnp.zeros((3, D), F32)
            dg2_ref[...] = jnp.zeros((1, D), F32)
            next_d[...] = jnp.zeros((sub, D), F32)

        for r in range(tm - sub, -1, -sub):
            rows = slice(r, r + sub)
            dh2 = _dot_nt(dp2_ref[rows, 0:4 * D], wa_ref[...]) + _dot_nt(dp2_ref[rows, 4 * D:4 * D + LANES], wf_ref[...])
            dxn2, dg2 = _rms_bwd(dh2, x1_ref[rows, :], g2_ref[...])
            dg2_ref[...] += dg2
            dx1v = dx2_ref[rows, :] + dxn2
            dx1b = dx1v.astype(BF16)
            dx1b_ref[rows, :] = dx1b
            dy = _dot_nt(dx1b, wout_ref[...])
            b = proj_ref[rows, 0:D]
            c = proj_ref[rows, D:2 * D]
            xin = proj_ref[rows, 2 * D:3 * D]
            z = proj_ref[rows, 3 * D:4 * D]
            sg = _sigmoid(z)
            sz = z * sg
            ycv = yc_ref[rows, :]
            d0 = dy * b * sz
            dproj_ref[rows, 0:D] = (dy * ycv * sz).astype(BF16)
            dproj_ref[rows, 3 * D:4 * D] = (dy * b * ycv * (sg * (1.0 + z * (1.0 - sg)))).astype(BF16)
            nd = next_d[...]
            row = lax.broadcasted_iota(jnp.int32, (sub, 1), 0)
            d1 = jnp.where(row >= sub - 1, pltpu.roll(nd, sub - 1, 0), pltpu.roll(d0, sub - 1, 0))
            d2 = jnp.where(row >= sub - 2, pltpu.roll(nd, sub - 2, 0), pltpu.roll(d0, sub - 2, 0))
            next_d[...] = d0
            w = cw_ref[...]
            du = w[2:3] * d0 + w[1:2] * d1 + w[0:1] * d2
            u = c * xin
            dcw_ref[2:3, :] += jnp.sum(d0 * u, axis=0, keepdims=True)
            dcw_ref[1:2, :] += jnp.sum(d1 * u, axis=0, keepdims=True)
            dcw_ref[0:1, :] += jnp.sum(d2 * u, axis=0, keepdims=True)
            dproj_ref[rows, D:2 * D] = (du * xin).astype(BF16)
            dproj_ref[rows, 2 * D:3 * D] = (du * c).astype(BF16)
            dh = _dot_nt(dproj_ref[rows, 0:D], win_ref[0])
            for j in range(1, 4):
                dh = dh + _dot_nt(dproj_ref[rows, j * D:(j + 1) * D], win_ref[j])
            dxn, dg = _rms_bwd(dh, x_ref[rows, :], g_ref[...])
            dx_ref[rows, :] = dx1v + dxn
            dg_ref[...] += dg

    return pl.pallas_call(
        body, name="conv_bwd", grid=(S // tm,),
        in_specs=[_rows(tm, 4 * D + LANES, last), _whole((D, 4 * D), 1), _whole((D, LANES), 1), _rows(tm, D, last),
                  _whole((1, D)), _rows(tm, D, last),
                  _rows(tm, D, last), _whole((1, D)), _whole((4, D, D), 1), _whole((D, D), 1),
                  _whole((3, D)), _rows(tm, 4 * D, last), _rows(tm, D, last)],
        out_specs=[_rows(tm, 4 * D, last), _rows(tm, D, last), _rows(tm, D, last), _whole((1, D)), _whole((3, D)),
                   _whole((1, D))],
        out_shape=[jax.ShapeDtypeStruct((S, 4 * D), BF16), jax.ShapeDtypeStruct((S, D), F32),
                   jax.ShapeDtypeStruct((S, D), BF16), jax.ShapeDtypeStruct((1, D), F32),
                   jax.ShapeDtypeStruct((3, D), F32), jax.ShapeDtypeStruct((1, D), F32)],
        scratch_shapes=[pltpu.VMEM((sub, D), F32)],
        compiler_params=_params(1),
    )(dproj2, wa, wf, x1, g2, dx2, x, g1, w_in, w_out, conv_w, proj, yc)


def _attn_front(x1, g2, w, wf, bf, gq, gk):
    S, D = x1.shape
    H = D // HEAD_DIM
    tm = min(ROW_TILE, S)
    tri = (lax.broadcasted_iota(jnp.int32, (tm, tm), 1) <= lax.broadcasted_iota(jnp.int32, (tm, tm), 0)).astype(BF16)

    def body(x_ref, g_ref, w_ref, wf_ref, bf_ref, gq_ref, gk_ref, tri_ref, first_ref, second_ref,
             h_ref, qh_ref, kh_ref, z_ref, f_ref, c_ref, rel_ref, qa_ref, ka_ref, va_ref, vt_ref,
             carry, v_s, qraw_ref, kraw_ref):
        @pl.when(pl.program_id(0) == 0)
        def _():
            carry[...] = jnp.zeros((8, LANES), F32)

        xv = x_ref[...]
        inv = lax.rsqrt(jnp.mean(xv * xv, axis=-1, keepdims=True) + RMS_EPS)
        h = (xv * inv * g_ref[...]).astype(BF16)
        h_ref[...] = h
        qraw_ref[...] = _dot(h, w_ref[:, 0:D])
        kraw_ref[...] = _dot(h, w_ref[:, D:2 * D])
        v_s[...] = _dot(h, w_ref[:, 2 * D:3 * D])
        z_ref[...] = _dot(h, w_ref[:, 3 * D:4 * D])
        lane = _lane()
        f = _dot(h, wf_ref[...]) + bf_ref[...]
        f_ref[...] = f
        logf = jnp.where(lane < H, jnp.minimum(f, 0.0) - jnp.log(1.0 + jnp.exp(-jnp.abs(f))), 0.0)
        cs = _dot01(tri_ref[...], logf) + carry[0:1, :]
        c_ref[...] = cs
        carry[...] = jnp.broadcast_to(cs[tm - 1:tm, :], (8, LANES))
        diags = jnp.zeros((tm, LANES), F32)
        for hd in range(H):
            sl = slice(LANES * hd, LANES * (hd + 1))
            a = _aug(hd % 2)
            if hd % 2 == 0:
                qh_ref[hd // 2] = qraw_ref[:, LANES * (hd // 2):LANES * (hd // 2 + 1)]
                kh_ref[hd // 2] = kraw_ref[:, LANES * (hd // 2):LANES * (hd // 2 + 1)]
            qt = _head_tile(qraw_ref, hd, lane)
            qn = qt * lax.rsqrt(jnp.sum(qt * qt, axis=1, keepdims=True) * (1.0 / HEAD_DIM) + RMS_EPS) * gq_ref[...]
            kt = _head_tile(kraw_ref, hd, lane)
            kn = kt * lax.rsqrt(jnp.sum(kt * kt, axis=1, keepdims=True) * (1.0 / HEAD_DIM) + RMS_EPS) * gk_ref[...]
            diags = diags + jnp.where(lane == hd, jnp.sum(qn * kn, axis=1, keepdims=True) * Q_SCALE, 0.0)
            qa_ref[:, sl] = (qn * Q_SCALE).astype(BF16)
            ka_ref[:, sl] = kn.astype(BF16)
            va = jnp.where((lane >= a) & (lane < a + 3), 1.0, _head_tile(v_s, hd, lane))
            va_ref[:, sl] = va.astype(BF16)
            vt_ref[hd] = va.T.astype(BF16)
        rel = cs - diags
        rel_ref[...] = rel
        qa_ref[...] += _dot(_pack3(rel, lane, 1.0), first_ref[...]).astype(BF16)
        ka_ref[...] += _dot(_pack3(-cs, lane, 1.0), second_ref[...]).astype(BF16)

    nb = S // tm
    heads = pl.BlockSpec((H // 2, tm, LANES), lambda i: (0, i, 0))
    return pl.pallas_call(
        body, name="attn_front", grid=(nb,),
        in_specs=[_rows(tm, D), _whole((1, D)), _whole((D, 4 * D)), _whole((D, LANES)), _whole((1, LANES)),
                  _whole((1, LANES)), _whole((1, LANES)), _whole((tm, tm)), _whole((LANES, H * LANES)),
                  _whole((LANES, H * LANES))],
        out_specs=[_rows(tm, D), heads, heads, _rows(tm, D), _rows(tm, LANES), _rows(tm, LANES), _rows(tm, LANES),
                   _rows(tm, H * LANES), _rows(tm, H * LANES), _rows(tm, H * LANES),
                   pl.BlockSpec((H, None, LANES, tm), lambda i: (0, i, 0, 0))],
        out_shape=[jax.ShapeDtypeStruct((S, D), BF16), jax.ShapeDtypeStruct((H // 2, S, LANES), F32),
                   jax.ShapeDtypeStruct((H // 2, S, LANES), F32), jax.ShapeDtypeStruct((S, D), F32),
                   jax.ShapeDtypeStruct((S, LANES), F32), jax.ShapeDtypeStruct((S, LANES), F32),
                   jax.ShapeDtypeStruct((S, LANES), F32),
                   jax.ShapeDtypeStruct((S, H * LANES), BF16), jax.ShapeDtypeStruct((S, H * LANES), BF16),
                   jax.ShapeDtypeStruct((S, H * LANES), BF16), jax.ShapeDtypeStruct((H, nb, LANES, tm), BF16)],
        scratch_shapes=[pltpu.VMEM((8, LANES), F32), pltpu.VMEM((tm, D), F32), pltpu.VMEM((tm, D), F32),
                        pltpu.VMEM((tm, D), F32)],
        compiler_params=_params(1),
    )(x1, g2, w, wf, bf, gq, gk, tri, *_scatter_matrices(H))


def _skip_tables(c, diag, gq, gk, T, G):
    nb = c.shape[0] // T
    bound = 8.0 * jnp.max(jnp.abs(gq)) * jnp.max(jnp.abs(gk))
    first, last = c[0::T, :], c[T - 1::T, :]
    lowest = jnp.maximum(jnp.min(diag.reshape(nb, T, -1), axis=1), -bound)
    idx = jnp.arange(nb)
    margin = (SKIP_LOG + bound) - lowest
    need = (last[None, :, :] <= first[:, None, :] + margin[:, None, :]) & (idx[None, :, None] < idx[:, None, None])
    need = need | (idx[None, :, None] == idx[:, None, None])
    kstart = jnp.argmax(need, axis=1)
    qend = nb - 1 - jnp.argmax(need[::-1], axis=0)
    kstart = jnp.min(kstart.reshape(2 * nb // G, G // 2, -1), axis=1)
    kstart = kstart - (kstart & 1)
    qend = jnp.max(qend.reshape(2 * nb // G, G // 2, -1), axis=1)
    return kstart.T.astype(jnp.int32), qend.T.astype(jnp.int32), bound


def _attn_fwd(kstart, qa, ka, vt, online_max):
    S = qa.shape[0]
    H = qa.shape[1] // LANES
    nb, T = vt.shape[1], vt.shape[3]
    G = 2 * nb // kstart.shape[1]
    W = G * T

    def finish(acc, shift, o_ref, lse_ref):
        a = _aug(pl.program_id(0) % 2)
        feat = lax.broadcasted_iota(jnp.int32, (LANES, 1), 0)
        l = jnp.sum(jnp.where(feat == a, acc, 0.0), axis=0, keepdims=True)
        o_ref[...] = (acc * (1.0 / l)).T
        lse_ref[...] = shift + jnp.log(l)

    def causal(st):
        return jnp.where(lax.broadcasted_iota(jnp.int32, st.shape, 0) <= lax.broadcasted_iota(jnp.int32, st.shape, 1),
                         st, NEG)

    def fast_body(ks_ref, q_ref, k_ref, vt_ref, o_ref, lse_ref, acc_ref, sa_ref, sb_ref, sc_ref):
        h, g = pl.program_id(0), pl.program_id(1)
        q = q_ref[...]
        acc_ref[...] = jnp.zeros((LANES, W), F32)

        def scores(ki, lo):
            return _dot_nt(k_ref[pl.ds(pl.multiple_of(ki * T, T), 2 * T), :], q[lo * T:, :])

        def weighted(ki, p):
            return _dot(vt_ref[ki], p[:T]) + _dot(vt_ref[ki + 1], p[T:])

        first = ks_ref[h, 2 * g + 1]
        early = jnp.minimum(ks_ref[h, 2 * g], first)

        def narrow(i, carry):
            ki = early + 2 * i
            st = _dot_nt(k_ref[pl.ds(pl.multiple_of(ki * T, T), 2 * T), :], q[:W // 2, :])
            acc_ref[:, :W // 2] += weighted(ki, jnp.exp(st).astype(BF16))
            return carry

        lax.fori_loop(0, (first - early) // 2, narrow, 0)
        steps = (g * G - first) // 2
        sa_ref[...] = scores(first, 0)

        def advance(ki, cur_ref, next_ref):
            p = jnp.exp(cur_ref[...]).astype(BF16)
            next_ref[...] = scores(ki + 2, 0)
            acc_ref[...] += weighted(ki, p)

        def loop(i, carry):
            advance(first + 4 * i, sa_ref, sb_ref)
            advance(first + 4 * i + 2, sb_ref, sa_ref)
            return carry

        lax.fori_loop(0, steps // 2, loop, 0)

        def first_own(pending_ref):
            p = jnp.exp(causal(pending_ref[...])).astype(BF16)
            if G > 2:
                sc_ref[:, :W - 2 * T] = scores(g * G + 2, 2)
            acc_ref[...] += weighted(g * G, p)

        @pl.when(steps % 2 == 1)
        def _():
            advance(g * G - 2, sa_ref, sb_ref)
            first_own(sb_ref)

        @pl.when(steps % 2 == 0)
        def _():
            first_own(sa_ref)

        if G > 2:
            acc_ref[:, 2 * T:] += weighted(g * G + 2, jnp.exp(causal(sc_ref[:, :W - 2 * T])).astype(BF16))
        for j in range(4, G, 2):
            p = jnp.exp(causal(scores(g * G + j, j))).astype(BF16)
            acc_ref[:, j * T:] += weighted(g * G + j, p)
        finish(acc_ref[...], 0.0, o_ref, lse_ref)

    def online_body(ks_ref, q_ref, k_ref, vt_ref, o_ref, lse_ref, acc_ref, m_ref):
        h, g = pl.program_id(0), pl.program_id(1)
        q = q_ref[...]
        m_ref[...] = jnp.full((8, W), NEG, F32)
        acc_ref[...] = jnp.zeros((LANES, W), F32)

        def update(st, vtb, lo):
            m_old = m_ref[0:1, lo:]
            m_new = jnp.maximum(m_old, jnp.max(st, axis=0, keepdims=True))
            p = jnp.exp(st - m_new).astype(BF16)
            acc_ref[:, lo:] = acc_ref[:, lo:] * jnp.exp(m_old - m_new) + _dot(vtb, p)
            m_ref[:, lo:] = jnp.broadcast_to(m_new, (8, W - lo))

        def loop(ki, carry):
            kb = k_ref[pl.ds(pl.multiple_of(ki * T, T), T), :]
            update(_dot_nt(kb, q), vt_ref[ki], 0)
            return carry

        lax.fori_loop(jnp.minimum(ks_ref[h, 2 * g], ks_ref[h, 2 * g + 1]), g * G, loop, 0)
        for j in range(G):
            ki = g * G + j
            kb = k_ref[pl.ds(pl.multiple_of(ki * T, T), T), :]
            update(causal(_dot_nt(kb, q[j * T:, :])), vt_ref[ki], j * T)
        finish(acc_ref[...], m_ref[0:1, :], o_ref, lse_ref)

    return pl.pallas_call(
        online_body if online_max else fast_body, name="attn_fwd_online" if online_max else "attn_fwd",
        grid_spec=pltpu.PrefetchScalarGridSpec(
            num_scalar_prefetch=1, grid=(H, nb // G),
            in_specs=[pl.BlockSpec((W, LANES), lambda h, i, ks: (i, h)),
                      pl.BlockSpec((S, LANES), lambda h, i, ks: (0, h)),
                      pl.BlockSpec((None, nb, LANES, T), lambda h, i, ks: (h, 0, 0, 0))],
            out_specs=[pl.BlockSpec((W, LANES), lambda h, i, ks: (i, h)),
                       pl.BlockSpec((None, 1, W), lambda h, i, ks: (h, 0, i))],
            scratch_shapes=[pltpu.VMEM((LANES, W), F32)] + (
                [pltpu.VMEM((8, W), F32)] if online_max else [pltpu.VMEM((2 * T, W), F32)] * 3)),
        out_shape=[jax.ShapeDtypeStruct((S, H * LANES), F32), jax.ShapeDtypeStruct((H, 1, S), F32)],
        compiler_params=_params(2),
    )(kstart, qa, ka, vt)


def _attn_out(o_aug, lse, rel, z, x1, target, w_out, qa):
    S, D = x1.shape
    H = D // HEAD_DIM
    tm = min(ROW_TILE, S)

    def body(o_ref, z_ref, x1_ref, t_ref, w_ref, q_ref, first_ref, rel_ref, lse_ref,
             dx2_ref, dx2b_ref, o2b_ref, dz_ref, doa_ref, qa2_ref, loss_ref, oc_s, do_s):
        @pl.when(pl.program_id(0) == 0)
        def _():
            loss_ref[...] = jnp.zeros((1, LANES), F32)

        lane = _lane()
        for j in range(H // 2):
            oc_s[:, LANES * j:LANES * (j + 1)] = _pair_tile(o_ref[:, 2 * LANES * j:2 * LANES * j + LANES],
                                                            o_ref[:, 2 * LANES * j + LANES:2 * LANES * (j + 1)], lane)
        oc = oc_s[...]
        zv = z_ref[...]
        sg = _sigmoid(zv)
        sz = zv * sg
        o2 = (oc * sz).astype(BF16)
        o2b_ref[...] = o2
        e = x1_ref[...] + _dot(o2, w_ref[...]) - t_ref[...]
        sq = jnp.sum(jnp.sum(e * e, axis=1, keepdims=True), axis=0, keepdims=True)
        loss_ref[...] += jnp.broadcast_to(sq * (0.5 / D), (1, LANES))
        dx2 = e * (1.0 / D)
        dx2_ref[...] = dx2
        dx2b = dx2.astype(BF16)
        dx2b_ref[...] = dx2b
        do2 = _dot_nt(dx2b, w_ref[...])
        dz_ref[...] = (do2 * oc * (sg * (1.0 + zv * (1.0 - sg)))).astype(BF16)
        do_s[...] = do2 * sz
        deltas = jnp.zeros((tm, LANES), F32)
        for hd in range(H):
            dt = _head_tile(do_s, hd, lane)
            delta = jnp.sum(dt * _head_tile(oc_s, hd, lane), axis=1, keepdims=True)
            deltas = deltas + jnp.where(lane == hd, delta, 0.0)
            doa_ref[:, LANES * hd:LANES * (hd + 1)] = dt.astype(BF16)
        doa_ref[...] += _dot(_pack3(-deltas, lane, 0.0), first_ref[...]).astype(BF16)
        lse = jnp.concatenate([lse_ref[...], jnp.zeros((LANES - H, tm), F32)], axis=0).T
        rq = rel_ref[...] - lse
        tile_lane = lax.broadcasted_iota(jnp.int32, (1, H * LANES), 1)
        extra = tile_lane % LANES - _aug((tile_lane // LANES) % 2)
        kept = jnp.where((extra >= 0) & (extra < 3), jnp.zeros((), BF16), q_ref[...])
        qa2_ref[...] = kept + _dot(_pack3(rq, lane, 0.0), first_ref[...]).astype(BF16)

    return pl.pallas_call(
        body, name="attn_out", grid=(S // tm,),
        in_specs=[_rows(tm, H * LANES), _rows(tm, D), _rows(tm, D), _rows(tm, D), _whole((D, D)),
                  _rows(tm, H * LANES), _whole((LANES, H * LANES)), _rows(tm, LANES),
                  pl.BlockSpec((H, tm), lambda i: (0, i))],
        out_specs=[_rows(tm, D), _rows(tm, D), _rows(tm, D), _rows(tm, D), _rows(tm, H * LANES),
                   _rows(tm, H * LANES), _whole((1, LANES))],
        out_shape=[jax.ShapeDtypeStruct((S, D), F32), jax.ShapeDtypeStruct((S, D), BF16),
                   jax.ShapeDtypeStruct((S, D), BF16), jax.ShapeDtypeStruct((S, D), BF16),
                   jax.ShapeDtypeStruct((S, H * LANES), BF16), jax.ShapeDtypeStruct((S, H * LANES), BF16),
                   jax.ShapeDtypeStruct((1, LANES), F32)],
        scratch_shapes=[pltpu.VMEM((tm, D), F32), pltpu.VMEM((tm, D), F32)],
        compiler_params=_params(1),
    )(o_aug, z, x1, target, w_out, qa, _scatter_matrices(H)[0], rel, lse)


def _attn_bwd(qend, qa2, doa, ka, va, T):
    S = qa2.shape[0]
    H = qa2.shape[1] // LANES
    nb = S // T
    G = 2 * nb // qend.shape[1]
    W = G * T

    def body(qe_ref, q_ref, do_ref, k_ref, v_ref, dq_ref, dk_ref, dv_ref, dkt_acc, dvt_acc):
        h, g = pl.program_id(0), pl.program_id(1)

        @pl.when(g == 0)
        def _():
            dq_ref[...] = jnp.zeros((S, LANES), F32)

        kb = k_ref[...]
        vb = v_ref[...]
        dkt_acc[...] = jnp.zeros((LANES, W), F32)
        dvt_acc[...] = jnp.zeros((LANES, W), F32)

        def step(qi, c0, c1, masked):
            rows = pl.ds(pl.multiple_of(qi * T, T), 2 * T)
            qb = q_ref[rows, :]
            dob = do_ref[rows, :]
            s = _dot_nt(qb, kb[c0:c1])
            if masked:
                query = lax.broadcasted_iota(jnp.int32, s.shape, 0) + (c1 - 2 * T)
                s = jnp.where(lax.broadcasted_iota(jnp.int32, s.shape, 1) <= query, s, NEG)
            p = jnp.exp(s)
            ds = (p * _dot_nt(dob, vb[c0:c1])).astype(BF16)
            dvt_acc[:, c0:c1] += _dot(dob.astype(F32).T.astype(BF16), p.astype(BF16))
            dkt_acc[:, c0:c1] += _dot(qb.astype(F32).T.astype(BF16), ds)
            dq_ref[rows, :] += _dot(ds, kb[c0:c1])

        for m in range(G // 2):
            step(g * G + 2 * m, 0, (m + 1) * 2 * T, True)
        first = g * G + G
        n_all = jnp.maximum((qe_ref[h, 2 * g] - first + 2) // 2, 0)
        second = first + 2 * n_all

        def all_keys(i, carry):
            step(first + 2 * i, 0, W, False)
            return carry

        def late_keys(i, carry):
            step(second + 2 * i, W // 2, W, False)
            return carry

        lax.fori_loop(0, n_all, all_keys, 0)
        lax.fori_loop(0, (qe_ref[h, 2 * g + 1] - second + 2) // 2, late_keys, 0)
        dk_ref[...] = dkt_acc[...].T
        dv_ref[...] = dvt_acc[...].T.astype(BF16)

    heads = pl.BlockSpec((None, W, LANES), lambda h, i, qe: (h, i, 0))
    return pl.pallas_call(
        body, name="attn_bwd",
        grid_spec=pltpu.PrefetchScalarGridSpec(
            num_scalar_prefetch=1, grid=(H, nb // G),
            in_specs=[pl.BlockSpec((S, LANES), lambda h, i, qe: (0, h)), pl.BlockSpec((S, LANES), lambda h, i, qe: (0, h)),
                      pl.BlockSpec((W, LANES), lambda h, i, qe: (i, h)), pl.BlockSpec((W, LANES), lambda h, i, qe: (i, h))],
            out_specs=[pl.BlockSpec((None, S, LANES), lambda h, i, qe: (h, 0, 0)), heads, heads],
            scratch_shapes=[pltpu.VMEM((LANES, W), F32), pltpu.VMEM((LANES, W), F32)]),
        out_shape=[jax.ShapeDtypeStruct((H, S, LANES), F32), jax.ShapeDtypeStruct((H, S, LANES), F32),
                   jax.ShapeDtypeStruct((H, S, LANES), BF16)],
        compiler_params=_params(2),
    )(qend, qa2, doa, ka, va)


def _attn_proj_bwd(dqt, dka, dva, qraw, kraw, dz, f, gq, gk):
    S, D = dz.shape
    H = D // HEAD_DIM
    tm = min(ROW_TILE, S)
    last = S // tm - 1
    tri = (lax.broadcasted_iota(jnp.int32, (tm, tm), 1) >= lax.broadcasted_iota(jnp.int32, (tm, tm), 0)).astype(BF16)

    def body(dq_ref, dk_ref, dv_ref, q_ref, k_ref, dz_ref, f_ref, gq_ref, gk_ref, tri_ref, ones_ref,
             dproj_ref, small_ref, carry, pairs):
        @pl.when(pl.program_id(0) == 0)
        def _():
            small_ref[...] = jnp.zeros((8, LANES), F32)
            carry[...] = jnp.zeros((8, LANES), F32)

        lane = _lane()

        def head_pair(j, acc):
            dcs, dgq, dgk = acc
            dq2, dk2 = [], []
            q_pair, k_pair = q_ref[j], k_ref[j]
            for parity in (0, 1):
                hd = 2 * j + parity
                own, a = _own(lane, parity), _aug(parity)
                dqf = dq_ref[hd]
                dqn = jnp.where(own, dqf * Q_SCALE, 0.0)
                d, dg = _head_rms_bwd(dqn, jnp.where(own, q_pair, 0.0), gq_ref[...], ones_ref[...])
                dq2.append(d)
                dgq = dgq + dg
                dkt = dk_ref[hd]
                dcs = dcs + jnp.where(lane == hd, _col(dqf, lane, a) - _col(dkt, lane, a + 3), 0.0)
                d, dg = _head_rms_bwd(jnp.where(own, dkt, 0.0), jnp.where(own, k_pair, 0.0), gk_ref[...], ones_ref[...])
                dk2.append(d)
                dgk = dgk + dg
            pairs[0, j] = _pair_tile(*dq2, lane).astype(BF16)
            pairs[1, j] = _pair_tile(*dk2, lane).astype(BF16)
            pairs[2, j] = _pair_tile(dv_ref[2 * j], dv_ref[2 * j + 1], lane)
            return dcs, dgq, dgk

        zero = jnp.zeros((1, LANES), F32)
        dcs, dgq, dgk = lax.fori_loop(0, H // 2, head_pair, (jnp.zeros((tm, LANES), F32), zero, zero))
        for part in range(3):
            for j in range(H // 2):
                dproj_ref[:, part * D + LANES * j:part * D + LANES * (j + 1)] = pairs[part, j]
        dproj_ref[:, 3 * D:4 * D] = dz_ref[...]
        dlogf = _dot01(tri_ref[...], dcs) + carry[0:1, :]
        carry[...] = jnp.broadcast_to(dlogf[0:1, :], (8, LANES))
        df = dlogf * (1.0 / (1.0 + jnp.exp(f_ref[...])))
        dproj_ref[:, 4 * D:4 * D + LANES] = df.astype(BF16)
        small_ref[0:1, :] += jnp.sum(df, axis=0, keepdims=True)
        small_ref[1:2, :] += dgq
        small_ref[2:3, :] += dgk

    W = 4 * D + LANES
    heads = pl.BlockSpec((H, tm, LANES), lambda i: (0, last - i, 0))
    head_pairs = pl.BlockSpec((H // 2, tm, LANES), lambda i: (0, last - i, 0))
    return pl.pallas_call(
        body, name="attn_proj_bwd", grid=(S // tm,),
        in_specs=[heads, heads, heads, head_pairs, head_pairs,
                  _rows(tm, D, last), _rows(tm, LANES, last), _whole((1, LANES)), _whole((1, LANES)),
                  _whole((tm, tm)), _whole((LANES, LANES))],
        out_specs=[_rows(tm, W, last), _whole((8, LANES))],
        out_shape=[jax.ShapeDtypeStruct((S, W), BF16), jax.ShapeDtypeStruct((8, LANES), F32)],
        scratch_shapes=[pltpu.VMEM((8, LANES), F32), pltpu.VMEM((3, H // 2, tm, LANES), BF16)],
        compiler_params=_params(1),
    )(dqt, dka, dva, qraw, kraw, dz, f, gq, gk, tri, jnp.ones((LANES, LANES), BF16))


def _matmul_tn(a, b, col0, n, tn, name, stacked=False):
    S, M = a.shape
    ts = min(TN_ROWS, S)
    off = col0 // tn

    def body(a_ref, b_ref, o_ref):
        @pl.when(pl.program_id(1) == 0)
        def _():
            o_ref[...] = jnp.zeros((M, tn), F32)

        o_ref[...] += _dot_tn(a_ref[...], b_ref[...])

    if stacked:
        out_spec, out_shape = pl.BlockSpec((None, M, tn), lambda j, s: (j, 0, 0)), (n // tn, M, tn)
    else:
        out_spec, out_shape = pl.BlockSpec((M, tn), lambda j, s: (0, j)), (M, n)
    return pl.pallas_call(
        body, name=name, grid=(n // tn, S // ts),
        in_specs=[pl.BlockSpec((ts, M), lambda j, s: (s, 0)), pl.BlockSpec((ts, tn), lambda j, s: (s, off + j))],
        out_specs=out_spec, out_shape=jax.ShapeDtypeStruct(out_shape, F32),
        compiler_params=_params(2),
    )(a, b)


def _adam_update(gv, w_ref, m_ref, v_ref, d_ref, m2_ref, v2_ref):
    m2 = ADAM_B1 * m_ref[...] + (1.0 - ADAM_B1) * gv
    v2 = ADAM_B2 * v_ref[...] + (1.0 - ADAM_B2) * (gv * gv)
    m2_ref[...] = m2
    v2_ref[...] = v2
    m_hat = m2 / (1.0 - ADAM_B1 ** ADAM_STEP)
    v_hat = v2 / (1.0 - ADAM_B2 ** ADAM_STEP)
    d_ref[...] = -ADAM_LR * (m_hat / (jnp.sqrt(v_hat) + ADAM_EPS) + ADAM_WD * w_ref[...])


def _adamw(w, g, m, v, name):
    r, c = w.shape
    tr = ROW_TILE if r % ROW_TILE == 0 else r

    def body(w_ref, g_ref, m_ref, v_ref, d_ref, m2_ref, v2_ref):
        _adam_update(g_ref[...], w_ref, m_ref, v_ref, d_ref, m2_ref, v2_ref)

    spec = _rows(tr, c)
    return pl.pallas_call(
        body, name=name, grid=(r // tr,), in_specs=[spec] * 4, out_specs=[spec] * 3,
        out_shape=[jax.ShapeDtypeStruct((r, c), F32)] * 3, compiler_params=_params(1),
    )(w, g, m, v)


def _adamw_halves(w, mine, other, m, v, core, name):
    r, c = mine.shape
    tr = ROW_TILE if r % ROW_TILE == 0 else r
    per = r // tr

    def body(core_ref, w_ref, mine_ref, other_ref, m_ref, v_ref, g_ref, d_ref, m2_ref, v2_ref):
        gv = jnp.where(pl.program_id(0) // per == core_ref[0], mine_ref[...], other_ref[...])
        g_ref[...] = gv
        _adam_update(gv, w_ref, m_ref, v_ref, d_ref, m2_ref, v2_ref)

    full = pl.BlockSpec((tr, c), lambda i, core: (i, 0))
    half = pl.BlockSpec((tr, c), lambda i, core: (i % per, 0))
    return pl.pallas_call(
        body, name=name,
        grid_spec=pltpu.PrefetchScalarGridSpec(num_scalar_prefetch=1, grid=(2 * per,),
                                               in_specs=[full, half, half, full, full], out_specs=[full] * 4),
        out_shape=[jax.ShapeDtypeStruct((2 * r, c), F32)] * 4, compiler_params=_params(1),
    )(core, w, mine, other, m, v)


def _local_step(x, target, g1, w_in, conv_w, w_out, g2, wa_in, b_f, gq, gk, wa_out):
    S, D = x.shape
    H = D // HEAD_DIM
    ws = wa_in.shape[2]
    w_qkvz = jnp.concatenate([wa_in[0], wa_in[1], wa_in[2], wa_in[3][:, :4 * D - 3 * ws]], axis=1)
    wf = jnp.pad(wa_in[3][:, 4 * D - 3 * ws:], ((0, 0), (0, LANES - H)))
    bf = jnp.pad(b_f, ((0, 0), (0, LANES - H)))
    gq128 = jnp.concatenate([gq, gq], axis=1)
    gk128 = jnp.concatenate([gk, gk], axis=1)

    proj, h1, yc, y, x1 = _conv_fwd(x, g1, w_in, conv_w, w_out)
    h2, qraw, kraw, z, f, c, rel, qa, ka, va, vt = _attn_front(x1, g2, w_qkvz, wf, bf, gq128, gk128)
    T = vt.shape[3]
    kstart, qend, bound = _skip_tables(c[:, :H], (c - rel)[:, :H], gq, gk, T, min(ATT_GROUP, S // T))
    o_aug, lse = lax.cond(2.0 * bound <= PLAIN_EXP_MAX, functools.partial(_attn_fwd, online_max=False),
                          functools.partial(_attn_fwd, online_max=True), kstart, qa, ka, vt)
    dx2, dx2b, o2b, dz, doa, qa2, loss = _attn_out(o_aug, lse.reshape(H, S), rel, z, x1, target, wa_out, qa)
    dqt, dka, dva = _attn_bwd(qend, qa2, doa, ka, va, T)
    dproj2, small = _attn_proj_bwd(dqt, dka, dva, qraw, kraw, dz, f, gq128, gk128)
    dproj1, dx, dx1b, dg1, dcw, dg2 = _conv_bwd(dproj2, w_qkvz, wf, x1, g2, dx2, x, g1, w_in, w_out, conv_w, proj, yc)

    tn = min(1024, D)
    dwa_out = _matmul_tn(o2b, dx2b, 0, D, tn, "dw_attn_out")
    dw_main = _matmul_tn(h2, dproj2, 0, 4 * D, tn, "dw_attn_in")
    dw_f = _matmul_tn(h2, dproj2, 4 * D, LANES, LANES, "dw_attn_f")
    dwa_in = jnp.stack([dw_main[:, 0:ws], dw_main[:, ws:2 * ws], dw_main[:, 2 * ws:3 * ws],
                        jnp.concatenate([dw_main[:, 3 * ws:], dw_f[:, :H]], axis=1)])
    dw_out = _matmul_tn(y, dx1b, 0, D, tn, "dw_conv_out")
    dw_in = _matmul_tn(h1, dproj1, 0, 4 * D, D, "dw_conv_in", stacked=True)
    grads = dict(conv_norm_g=dg1, conv_w_in=dw_in, conv_w=dcw, conv_w_out=dw_out, attn_norm_g=dg2,
                 attn_w_in=dwa_in, attn_b_f=small[0:1, :H],
                 attn_q_norm_g=small[1:2, :HEAD_DIM] + small[1:2, HEAD_DIM:],
                 attn_k_norm_g=small[2:3, :HEAD_DIM] + small[2:3, HEAD_DIM:], attn_w_out=dwa_out)
    return loss[0, 0], dx, grads


def _coords():
    return lax.axis_index("x"), lax.axis_index("y"), lax.axis_index("c")


def _at(ref, idx):
    return ref.at[idx] if idx else ref


def _other_chips(x, y):
    return [(1 - x, y), (x, 1 - y), (1 - x, 1 - y)]


def _all_gather(halved, whole):
    nh, nw = len(halved), len(whole)

    def body(*refs):
        src_h, src_w = refs[:nh], refs[nh:nh + nw]
        out_h, out_w = refs[nh + nw:2 * nh + nw], refs[2 * nh + nw:2 * (nh + nw)]
        send_h, recv_h, send_w, recv_w = refs[2 * (nh + nw):]
        x, y, c = _coords()
        mine = 2 * x + y
        sibling = (x, y, 1 - c)
        chips = _other_chips(x, y)

        def copy_h(a, k, chip, half, to, src=None):
            dst = out_h[a].at[chip, half]
            return pltpu.make_async_remote_copy(src_ref=dst if src is None else src, dst_ref=dst,
                                                send_sem=send_h.at[a, k], recv_sem=recv_h.at[a, k],
                                                device_id=to, device_id_type=MESH)

        def copy_w(a, k, chip, to):
            return pltpu.make_async_remote_copy(src_ref=src_w[a], dst_ref=out_w[a].at[chip],
                                                send_sem=send_w.at[a, k], recv_sem=recv_w.at[a, k],
                                                device_id=to, device_id_type=MESH)

        first = [copy_h(a, k, mine, c, (*chip, c), src=src_h[a].at[c]) for a in range(nh) for k, chip in enumerate(chips)]
        first += [copy_w(a, k, mine, (*chip, c)) for a in range(nw) for k, chip in enumerate(chips)]
        own = [pltpu.make_async_remote_copy(src_ref=src_h[a], dst_ref=out_h[a].at[mine], send_sem=send_h.at[a, 6],
                                            recv_sem=recv_h.at[a, 6], device_id=sibling, device_id_type=MESH)
               for a in range(nh)]
        own += [copy_w(a, 3, mine, sibling) for a in range(nw)]
        for cp in first + own:
            cp.start()
        passed = []
        for a in range(nh):
            for k, (px, py) in enumerate(chips):
                copy_h(a, k, 2 * px + py, c, (x, y, c)).wait_recv()
                cp = copy_h(a, 3 + k, 2 * px + py, c, sibling)
                cp.start()
                passed.append(cp)
        for a in range(nh):
            for k, (px, py) in enumerate(chips):
                copy_h(a, 3 + k, 2 * px + py, 1 - c, (x, y, c)).wait_recv()
        for a in range(nw):
            for k, (px, py) in enumerate(chips):
                copy_w(a, k, 2 * px + py, (x, y, c)).wait_recv()
        for cp in own:
            cp.wait_recv()
        for cp in first + passed + own:
            cp.wait_send()

    out_shape = [jax.ShapeDtypeStruct((4,) + a.shape, a.dtype) for a in list(halved) + list(whole)]
    return pl.pallas_call(
        body, name="gather_weights", in_specs=[ANY] * (nh + nw), out_specs=[ANY] * (nh + nw), out_shape=out_shape,
        scratch_shapes=[pltpu.SemaphoreType.DMA((nh, 7)), pltpu.SemaphoreType.DMA((nh, 7)),
                        pltpu.SemaphoreType.DMA((nw, 4)), pltpu.SemaphoreType.DMA((nw, 4))],
    )(*halved, *whole)


def _exchange(name, srcs, lands, copies, local_copies):
    ns, nl, n, nloc = len(srcs), len(lands), len(copies), len(local_copies)

    def body(*refs):
        src, land = refs[:ns], refs[ns:ns + nl]
        send, recv, local_sem = refs[ns + nl:]
        me = _coords()
        started = []
        for k, (si, s_at, li, l_at, ci) in enumerate(local_copies):
            cp = pltpu.make_async_copy(_at(src[si], s_at(*me)), _at(land[li], l_at(*me)), local_sem.at[k])
            cp.start()
            started.append(cp)
        remote = []
        for k, (si, s_at, li, l_at, peer) in enumerate(copies):
            cp = pltpu.make_async_remote_copy(src_ref=_at(src[si], s_at(*me)), dst_ref=_at(land[li], l_at(*me)),
                                              send_sem=send.at[k], recv_sem=recv.at[k],
                                              device_id=peer(*me), device_id_type=MESH)
            cp.start()
            remote.append(cp)
        for cp in remote:
            cp.wait()
        for cp in started:
            cp.wait()

    return pl.pallas_call(
        body, name=name, in_specs=[ANY] * ns, out_specs=[ANY] * nl, out_shape=list(lands),
        scratch_shapes=[pltpu.SemaphoreType.DMA((n,)), pltpu.SemaphoreType.DMA((n,)),
                        pltpu.SemaphoreType.DMA((max(nloc, 1),))],
    )(*srcs)


def _add_pairs(a, b, core, name):
    _, r, cols = b.shape
    tr = ROW_TILE if r % ROW_TILE == 0 else r

    def body(core_ref, a_ref, b_ref, o_ref, ob_ref):
        s = a_ref[...] + b_ref[...]
        o_ref[...] = s
        ob_ref[...] = s.astype(BF16)

    spec = pl.BlockSpec((None, tr, cols), lambda j, i, core: (j, i, 0))
    return pl.pallas_call(
        body, name=name,
        grid_spec=pltpu.PrefetchScalarGridSpec(
            num_scalar_prefetch=1, grid=(4, r // tr),
            in_specs=[pl.BlockSpec((None, None, tr, cols), lambda j, i, core: (j, core[0], i, 0)), spec],
            out_specs=[spec, spec]),
        out_shape=[jax.ShapeDtypeStruct(b.shape, F32), jax.ShapeDtypeStruct(b.shape, BF16)],
        compiler_params=_params(2),
    )(core, a, b)


def _sum_chips(own, landed, name):
    _, r, cols = landed.shape
    tr = ROW_TILE if r % ROW_TILE == 0 else r

    def body(own_ref, land_ref, o_ref):
        acc = own_ref[...]
        for j in range(3):
            acc = acc + land_ref[j].astype(F32)
        o_ref[...] = acc

    return pl.pallas_call(
        body, name=name, grid=(r // tr,),
        in_specs=[_rows(tr, cols), pl.BlockSpec((3, tr, cols), lambda i: (0, i, 0))], out_specs=_rows(tr, cols),
        out_shape=jax.ShapeDtypeStruct((r, cols), F32), compiler_params=_params(1),
    )(own, landed)


def _sum_devices(landed, name):
    def body(l_ref, o_ref):
        acc = l_ref[0]
        for j in range(1, 8):
            acc = acc + l_ref[j]
        o_ref[...] = acc

    return pl.pallas_call(body, name=name, out_shape=jax.ShapeDtypeStruct(landed.shape[1:], F32))(landed)


def _reduce_gradients(big, small):
    nb = len(big)
    x, y, c = _coords()
    mine = 2 * x + y
    flips = [(fx, fy, fc) for fx in (0, 1) for fy in (0, 1) for fc in (0, 1) if fx or fy or fc]

    def flip(fx, fy, fc):
        return lambda x, y, c: (x ^ fx, y ^ fy, c ^ fc)

    copies = [(a, (lambda j: lambda x, y, c: (j, 1 - c))(j), a, (lambda j: lambda x, y, c: (j,))(j), flip(0, 0, 1))
              for a in range(nb) for j in range(4)]
    copies += [(nb, lambda x, y, c: (), nb, lambda x, y, c: (4 * x + 2 * y + c,), flip(*f)) for f in flips]
    lands = [jax.ShapeDtypeStruct((4,) + g.shape[2:], F32) for g in big] + [jax.ShapeDtypeStruct((8,) + small.shape, F32)]
    local = [(nb, lambda x, y, c: (), nb, lambda x, y, c: (4 * x + 2 * y + c,), None)]
    landed = _exchange("swap_halves", list(big) + [small], lands, copies, local)
    small_sum = _sum_devices(landed[nb], "sum_small")

    chip_f32, chip_bf16 = [], []
    for a in range(nb):
        s, sb = _add_pairs(big[a], landed[a], jnp.reshape(c, (1,)).astype(jnp.int32), f"add_cores_{a}")
        chip_f32.append(s)
        chip_bf16.append(sb)

    chip_flips = [(1, 0), (0, 1), (1, 1)]
    copies = [(a, (lambda f: lambda x, y, c: (2 * (x ^ f[0]) + (y ^ f[1]),))(f), a, (lambda k: lambda x, y, c: (k,))(k),
               flip(f[0], f[1], 0)) for a in range(nb) for k, f in enumerate(chip_flips)]
    lands = [jax.ShapeDtypeStruct((3,) + g.shape[1:], BF16) for g in chip_bf16]
    landed = _exchange("send_chip_sums", chip_bf16, lands, copies, [])
    totals = [_sum_chips(lax.dynamic_index_in_dim(chip_f32[a], mine, axis=0, keepdims=False), landed[a],
                         f"sum_chips_{a}") for a in range(nb)]

    copies = [(a, lambda x, y, c: (), a, lambda x, y, c: (), flip(0, 0, 1)) for a in range(nb)]
    lands = [jax.ShapeDtypeStruct(t.shape, F32) for t in totals]
    return list(zip(totals, _exchange("swap_sums", totals, lands, copies, []))), small_sum


def kernel(x, conv_norm_g, conv_w_in, conv_w, conv_w_out, attn_norm_g, attn_w_in, attn_b_f, attn_q_norm_g, attn_k_norm_g, attn_w_out, loss_target, m_conv_norm_g, m_conv_w_in, m_conv_w, m_conv_w_out, m_attn_norm_g, m_attn_w_in, m_attn_b_f, m_attn_q_norm_g, m_attn_k_norm_g, m_attn_w_out, v_conv_norm_g, v_conv_w_in, v_conv_w, v_conv_w_out, v_attn_norm_g, v_attn_w_in, v_attn_b_f, v_attn_q_norm_g, v_attn_k_norm_g, v_attn_w_out):
    xi, yi, _ = _coords()
    chip = 2 * xi + yi
    D = x.shape[2]
    H = D // HEAD_DIM
    names = ["conv_norm_g", "conv_w_in", "conv_w", "conv_w_out", "attn_norm_g", "attn_w_in", "attn_b_f",
             "attn_q_norm_g", "attn_k_norm_g", "attn_w_out"]
    weights = dict(zip(names, [conv_norm_g, conv_w_in, conv_w, conv_w_out, attn_norm_g, attn_w_in, attn_b_f,
                               attn_q_norm_g, attn_k_norm_g, attn_w_out]))
    m_in = dict(zip(names, [m_conv_norm_g, m_conv_w_in, m_conv_w, m_conv_w_out, m_attn_norm_g, m_attn_w_in,
                            m_attn_b_f, m_attn_q_norm_g, m_attn_k_norm_g, m_attn_w_out]))
    v_in = dict(zip(names, [v_conv_norm_g, v_conv_w_in, v_conv_w, v_conv_w_out, v_attn_norm_g, v_attn_w_in,
                            v_attn_b_f, v_attn_q_norm_g, v_attn_k_norm_g, v_attn_w_out]))
    weights = {k: w[0] for k, w in weights.items()}
    m_in = {k: w[0] for k, w in m_in.items()}
    v_in = {k: w[0] for k, w in v_in.items()}

    big_names = ["conv_w_in", "attn_w_in", "conv_w_out", "attn_w_out"]
    halved = [weights[k].astype(BF16).reshape(2, weights[k].shape[0] // 2, weights[k].shape[1]) for k in big_names]
    q = D // 4
    small_w = jnp.concatenate([weights["conv_w"], weights["attn_norm_g"][None, :], jnp.zeros((4, q), F32)], axis=0)
    g_in, ga_in, g_out, ga_out, g_small = _all_gather(halved, [small_w])
    w_in = g_in.reshape(4, D, D)
    wa_in = ga_in.reshape(4, D, D + H // 4)
    w_out = g_out.reshape(D, D)
    wa_out = ga_out.reshape(D, D)
    conv_w_full = g_small[:, 0:3, :].transpose(1, 0, 2).reshape(3, D)
    attn_g_full = g_small[:, 3, :].reshape(1, D)

    loss_part, grad_x, grads = _local_step(x[0], loss_target[0], weights["conv_norm_g"][None, :], w_in, conv_w_full,
                                           w_out, attn_g_full, wa_in, weights["attn_b_f"][None, :],
                                           weights["attn_q_norm_g"][None, :], weights["attn_k_norm_g"][None, :], wa_out)

    big = [grads["conv_w_in"].reshape(4, 2, D // 2, D),
           grads["attn_w_in"].reshape(4, 2, D // 2, D + H // 4),
           grads["conv_w_out"].reshape(4, 2, D // 8, D), grads["attn_w_out"].reshape(4, 2, D // 8, D)]
    tail = jnp.concatenate([grads["attn_b_f"], grads["attn_q_norm_g"], grads["attn_k_norm_g"],
                            jnp.reshape(loss_part, (1, 1)), jnp.zeros((1, D - H - 2 * HEAD_DIM - 1), F32)], axis=1)
    small = jnp.concatenate([grads["conv_norm_g"], grads["conv_w"], grads["attn_norm_g"], tail,
                             jnp.zeros((2, D), F32)], axis=0)
    reduced, small_sum = _reduce_gradients(big, small)
    final = {}
    final["conv_norm_g"] = small_sum[0]
    final["conv_w"] = lax.dynamic_slice_in_dim(small_sum[1:4], chip * q, q, axis=1)
    final["attn_norm_g"] = lax.dynamic_slice_in_dim(small_sum[4], chip * q, q, axis=0)
    final["attn_b_f"] = small_sum[5, :H]
    final["attn_q_norm_g"] = small_sum[5, H:H + HEAD_DIM]
    final["attn_k_norm_g"] = small_sum[5, H + HEAD_DIM:H + 2 * HEAD_DIM]
    loss = small_sum[5, H + 2 * HEAD_DIM]

    delta, new_m, new_v = {}, {}, {}
    core = jnp.reshape(lax.axis_index("c"), (1,)).astype(jnp.int32)
    for k, (mine, other) in zip(big_names, reduced):
        final[k], delta[k], new_m[k], new_v[k] = _adamw_halves(weights[k], mine, other, m_in[k], v_in[k], core,
                                                               "adamw_" + k)
    for k in names:
        if k in big_names:
            continue
        shape = weights[k].shape
        as2d = (lambda a: a.reshape(1, -1)) if len(shape) == 1 else (lambda a: a)
        d, m2, v2 = _adamw(as2d(weights[k]), as2d(final[k]), as2d(m_in[k]), as2d(v_in[k]), "adamw_" + k)
        delta[k], new_m[k], new_v[k] = d.reshape(shape), m2.reshape(shape), v2.reshape(shape)
    lead = lambda a: a[None]
    return (loss, grad_x[None], *[lead(final[k]) for k in names], *[lead(delta[k]) for k in names],
            *[lead(new_m[k]) for k in names], *[lead(new_v[k]) for k in names])
```

```python
import functools

import jax
import jax.numpy as jnp
from jax import lax
from jax.experimental import pallas as pl
from jax.experimental.pallas import tpu as pltpu

F32 = jnp.float32
BF16 = jnp.bfloat16
HEAD_DIM = 64
LANES = 128
RMS_EPS = 1e-6
NEG = -1e30
Q_SCALE = 0.125
ROW_TILE = 256
CONV_TILE = 512
ATT_GROUP = 4
SKIP_LOG = 106.0
PLAIN_EXP_MAX = 60.0
TN_ROWS = 2048
VMEM_LIMIT = 56 << 20
ADAM_LR, ADAM_B1, ADAM_B2, ADAM_EPS, ADAM_WD, ADAM_STEP = 0.001, 0.9, 0.999, 1e-08, 0.01, 10
MESH = pl.DeviceIdType.MESH
ANY = pl.BlockSpec(memory_space=pl.ANY)


def _lane():
    return lax.broadcasted_iota(jnp.int32, (1, LANES), 1)


def _split3(x):
    hi = x.astype(BF16).astype(F32)
    r = x - hi
    mid = r.astype(BF16).astype(F32)
    lo = (r - mid).astype(BF16).astype(F32)
    return hi, mid, lo


STAT_STRIDE = 16
ONE_LANE = 3 * STAT_STRIDE


def _pack3(x, lane, one):
    hi, mid, lo = _split3(x)
    packed = hi + pltpu.roll(mid, STAT_STRIDE, 1) + pltpu.roll(lo, 2 * STAT_STRIDE, 1)
    return jnp.where(lane == ONE_LANE, one, packed).astype(BF16)


def _scatter_matrices(H):
    rows = lax.broadcasted_iota(jnp.int32, (LANES, H * LANES), 0)
    cols = lax.broadcasted_iota(jnp.int32, (LANES, H * LANES), 1)
    head, within = cols // LANES, cols % LANES
    extra = within - _aug(head % 2)
    term = (rows < ONE_LANE) & (rows % STAT_STRIDE == head)
    first = ((term & (extra == rows // STAT_STRIDE)) | ((rows == ONE_LANE) & (extra >= 3) & (extra < 6)))
    second = ((term & (extra - 3 == rows // STAT_STRIDE)) | ((rows == ONE_LANE) & (extra >= 0) & (extra < 3)))
    return first.astype(BF16), second.astype(BF16)


def _put(base, lane, start, parts):
    for j, p in enumerate(parts):
        base = jnp.where(lane == start + j, p, base)
    return base


def _col(x, lane, idx):
    return jnp.sum(jnp.where(lane == idx, x, 0.0), axis=1, keepdims=True)


def _feat(parity):
    return HEAD_DIM * parity


def _aug(parity):
    return HEAD_DIM * (1 - parity)


def _own(lane, parity):
    return (lane >= _feat(parity)) & (lane < _feat(parity) + HEAD_DIM)


def _head_tile(ref, hd, lane):
    j = hd // 2
    return jnp.where(_own(lane, hd % 2), ref[:, LANES * j:LANES * (j + 1)], 0.0)


def _pair_tile(even, odd, lane):
    return jnp.where(lane < HEAD_DIM, even, odd)


def _sigmoid(x):
    return 0.5 * jnp.tanh(0.5 * x) + 0.5


def _dot(a, b):
    return jnp.dot(a, b, preferred_element_type=F32)


def _dot_nt(a, b):
    return lax.dot_general(a, b, (((1,), (1,)), ((), ())), preferred_element_type=F32)


def _dot_tn(a, b):
    return lax.dot_general(a, b, (((0,), (0,)), ((), ())), preferred_element_type=F32)


def _dot01(tri, x):
    hi, mid, lo = _split3(x)
    return _dot(tri, hi.astype(BF16)) + _dot(tri, mid.astype(BF16)) + _dot(tri, lo.astype(BF16))


def _rms_bwd(dh, x, g):
    inv = lax.rsqrt(jnp.mean(x * x, axis=-1, keepdims=True) + RMS_EPS)
    xh = x * inv
    dxn = dh * g
    dx = inv * (dxn - xh * jnp.mean(dxn * xh, axis=-1, keepdims=True))
    return dx, jnp.sum(dh * xh, axis=0, keepdims=True)


def _head_rms_bwd(dn, t, g, ones):
    sq = t * t
    hi = sq.astype(BF16)
    lo = (sq - hi.astype(F32)).astype(BF16)
    inv = lax.rsqrt((_dot(hi, ones) + _dot(lo, ones)) * (1.0 / HEAD_DIM) + RMS_EPS)
    th = t * inv
    gd = dn * g
    d = inv * (gd - th * (jnp.sum(gd * th, axis=1, keepdims=True) * (1.0 / HEAD_DIM)))
    return d, jnp.sum(dn * th, axis=0, keepdims=True)


def _params(n_grid):
    return pltpu.CompilerParams(dimension_semantics=("arbitrary",) * n_grid, vmem_limit_bytes=VMEM_LIMIT)


def _rows(tm, cols, rev=None):
    if rev is None:
        return pl.BlockSpec((tm, cols), lambda i: (i, 0))
    return pl.BlockSpec((tm, cols), lambda i: (rev - i, 0))


def _whole(shape, buffers=None):
    mode = {} if buffers is None else dict(pipeline_mode=pl.Buffered(buffers))
    return pl.BlockSpec(shape, lambda *_: (0,) * len(shape), **mode)


def _conv_fwd(x, g1, w_in, conv_w, w_out):
    S, D = x.shape
    tm = min(CONV_TILE, S)
    sub = min(ROW_TILE, tm)

    def body(x_ref, g_ref, win_ref, cw_ref, wout_ref, saved_ref, h_ref, yc_ref, y_ref, x1_ref, prev_u, proj_ref):
        @pl.when(pl.program_id(0) == 0)
        def _():
            prev_u[...] = jnp.zeros((sub, D), F32)

        for r in range(0, tm, sub):
            rows = slice(r, r + sub)
            xv = x_ref[rows, :]
            inv = lax.rsqrt(jnp.mean(xv * xv, axis=-1, keepdims=True) + RMS_EPS)
            h = (xv * inv * g_ref[...]).astype(BF16)
            h_ref[rows, :] = h
            for j in range(4):
                proj_ref[rows, j * D:(j + 1) * D] = _dot(h, win_ref[j])
            saved_ref[rows, :] = proj_ref[rows, :].astype(BF16)
            u = proj_ref[rows, D:2 * D] * proj_ref[rows, 2 * D:3 * D]
            pu = prev_u[...]
            row = lax.broadcasted_iota(jnp.int32, (sub, 1), 0)
            u1 = jnp.where(row < 1, pltpu.roll(pu, 1, 0), pltpu.roll(u, 1, 0))
            u2 = jnp.where(row < 2, pltpu.roll(pu, 2, 0), pltpu.roll(u, 2, 0))
            prev_u[...] = u
            w = cw_ref[...]
            yc = w[2:3] * u + w[1:2] * u1 + w[0:1] * u2
            yc_ref[rows, :] = yc
            z = proj_ref[rows, 3 * D:4 * D]
            y = (proj_ref[rows, 0:D] * yc * (z * _sigmoid(z))).astype(BF16)
            y_ref[rows, :] = y
            x1_ref[rows, :] = xv + _dot(y, wout_ref[...])

    return pl.pallas_call(
        body, name="conv_fwd", grid=(S // tm,),
        in_specs=[_rows(tm, D), _whole((1, D)), _whole((4, D, D), 1), _whole((3, D)), _whole((D, D), 1)],
        out_specs=[_rows(tm, 4 * D), _rows(tm, D), _rows(tm, D), _rows(tm, D), _rows(tm, D)],
        out_shape=[jax.ShapeDtypeStruct((S, 4 * D), BF16), jax.ShapeDtypeStruct((S, D), BF16),
                   jax.ShapeDtypeStruct((S, D), F32), jax.ShapeDtypeStruct((S, D), BF16),
                   jax.ShapeDtypeStruct((S, D), F32)],
        scratch_shapes=[pltpu.VMEM((sub, D), F32), pltpu.VMEM((tm, 4 * D), F32)],
        compiler_params=_params(1),
    )(x, g1, w_in, conv_w, w_out)


def _conv_bwd(dproj2, wa, wf, x1, g2, dx2, x, g1, w_in, w_out, conv_w, proj, yc):
    S, D = x.shape
    tm = min(ROW_TILE, S)
    sub = min(ROW_TILE, tm)
    last = S // tm - 1

    def body(dp2_ref, wa_ref, wf_ref, x1_ref, g2_ref, dx2_ref, x_ref, g_ref, win_ref, wout_ref, cw_ref, proj_ref, yc_ref,
             dproj_ref, dx_ref, dx1b_ref, dg_ref, dcw_ref, dg2_ref, next_d):
        @pl.when(pl.program_id(0) == 0)
        def _():
            dg_ref[...] = jnp.zeros((1, D), F32)
            dcw_ref[...] = jnp.zeros((3, D), F32)
            dg2_ref[...] = jnp.zeros((1, D), F32)
            next_d[...] = jnp.zeros((sub, D), F32)

        for r in range(tm - sub, -1, -sub):
            rows = slice(r, r + sub)
            dh2 = _dot_nt(dp2_ref[rows, 0:4 * D], wa_ref[...]) + _dot_nt(dp2_ref[rows, 4 * D:4 * D + LANES], wf_ref[...])
            dxn2, dg2 = _rms_bwd(dh2, x1_ref[rows, :], g2_ref[...])
            dg2_ref[...] += dg2
            dx1v = dx2_ref[rows, :] + dxn2
            dx1b = dx1v.astype(BF16)
            dx1b_ref[rows, :] = dx1b
            dy = _dot_nt(dx1b, wout_ref[...])
            b = proj_ref[rows, 0:D].astype(F32)
            c = proj_ref[rows, D:2 * D].astype(F32)
            xin = proj_ref[rows, 2 * D:3 * D].astype(F32)
            z = proj_ref[rows, 3 * D:4 * D].astype(F32)
            sg = _sigmoid(z)
            sz = z * sg
            ycv = yc_ref[rows, :]
            d0 = dy * b * sz
            dproj_ref[rows, 0:D] = (dy * ycv * sz).astype(BF16)
            dproj_ref[rows, 3 * D:4 * D] = (dy * b * ycv * (sg * (1.0 + z * (1.0 - sg)))).astype(BF16)
            nd = next_d[...]
            row = lax.broadcasted_iota(jnp.int32, (sub, 1), 0)
            d1 = jnp.where(row >= sub - 1, pltpu.roll(nd, sub - 1, 0), pltpu.roll(d0, sub - 1, 0))
            d2 = jnp.where(row >= sub - 2, pltpu.roll(nd, sub - 2, 0), pltpu.roll(d0, sub - 2, 0))
            next_d[...] = d0
            w = cw_ref[...]
            du = w[2:3] * d0 + w[1:2] * d1 + w[0:1] * d2
            u = c * xin
            dcw_ref[2:3, :] += jnp.sum(d0 * u, axis=0, keepdims=True)
            dcw_ref[1:2, :] += jnp.sum(d1 * u, axis=0, keepdims=True)
            dcw_ref[0:1, :] += jnp.sum(d2 * u, axis=0, keepdims=True)
            dproj_ref[rows, D:2 * D] = (du * xin).astype(BF16)
            dproj_ref[rows, 2 * D:3 * D] = (du * c).astype(BF16)
            dh = _dot_nt(dproj_ref[rows, 0:D], win_ref[0])
            for j in range(1, 4):
                dh = dh + _dot_nt(dproj_ref[rows, j * D:(j + 1) * D], win_ref[j])
            dxn, dg = _rms_bwd(dh, x_ref[rows, :], g_ref[...])
            dx_ref[rows, :] = dx1v + dxn
            dg_ref[...] += dg

    return pl.pallas_call(
        body, name="conv_bwd", grid=(S // tm,),
        in_specs=[_rows(tm, 4 * D + LANES, last), _whole((D, 4 * D), 1), _whole((D, LANES), 1), _rows(tm, D, last),
                  _whole((1, D)), _rows(tm, D, last),
                  _rows(tm, D, last), _whole((1, D)), _whole((4, D, D), 1), _whole((D, D), 1),
                  _whole((3, D)), _rows(tm, 4 * D, last), _rows(tm, D, last)],
        out_specs=[_rows(tm, 4 * D, last), _rows(tm, D, last), _rows(tm, D, last), _whole((1, D)), _whole((3, D)),
                   _whole((1, D))],
        out_shape=[jax.ShapeDtypeStruct((S, 4 * D), BF16), jax.ShapeDtypeStruct((S, D), F32),
                   jax.ShapeDtypeStruct((S, D), BF16), jax.ShapeDtypeStruct((1, D), F32),
                   jax.ShapeDtypeStruct((3, D), F32), jax.ShapeDtypeStruct((1, D), F32)],
        scratch_shapes=[pltpu.VMEM((sub, D), F32)],
        compiler_params=_params(1),
    )(dproj2, wa, wf, x1, g2, dx2, x, g1, w_in, w_out, conv_w, proj, yc)


def _attn_front(x1, g2, w, wf, bf, gq, gk):
    S, D = x1.shape
    H = D // HEAD_DIM
    tm = min(ROW_TILE, S)
    tri = (lax.broadcasted_iota(jnp.int32, (tm, tm), 1) <= lax.broadcasted_iota(jnp.int32, (tm, tm), 0)).astype(BF16)

    def body(x_ref, g_ref, w_ref, wf_ref, bf_ref, gq_ref, gk_ref, tri_ref, first_ref, second_ref,
             h_ref, qh_ref, kh_ref, z_ref, f_ref, c_ref, rel_ref, qa_ref, ka_ref, va_ref, vt_ref,
             carry, v_s, qraw_ref, kraw_ref):
        @pl.when(pl.program_id(0) == 0)
        def _():
            carry[...] = jnp.zeros((8, LANES), F32)

        xv = x_ref[...]
        inv = lax.rsqrt(jnp.mean(xv * xv, axis=-1, keepdims=True) + RMS_EPS)
        h = (xv * inv * g_ref[...]).astype(BF16)
        h_ref[...] = h
        qraw_ref[...] = _dot(h, w_ref[:, 0:D])
        kraw_ref[...] = _dot(h, w_ref[:, D:2 * D])
        v_s[...] = _dot(h, w_ref[:, 2 * D:3 * D])
        z_ref[...] = _dot(h, w_ref[:, 3 * D:4 * D])
        lane = _lane()
        f = _dot(h, wf_ref[...]) + bf_ref[...]
        f_ref[...] = f
        logf = jnp.where(lane < H, jnp.minimum(f, 0.0) - jnp.log(1.0 + jnp.exp(-jnp.abs(f))), 0.0)
        cs = _dot01(tri_ref[...], logf) + carry[0:1, :]
        c_ref[...] = cs
        carry[...] = jnp.broadcast_to(cs[tm - 1:tm, :], (8, LANES))
        diags = jnp.zeros((tm, LANES), F32)
        for hd in range(H):
            sl = slice(LANES * hd, LANES * (hd + 1))
            a = _aug(hd % 2)
            if hd % 2 == 0:
                qh_ref[hd // 2] = qraw_ref[:, LANES * (hd // 2):LANES * (hd // 2 + 1)]
                kh_ref[hd // 2] = kraw_ref[:, LANES * (hd // 2):LANES * (hd // 2 + 1)]
            qt = _head_tile(qraw_ref, hd, lane)
            qn = qt * lax.rsqrt(jnp.sum(qt * qt, axis=1, keepdims=True) * (1.0 / HEAD_DIM) + RMS_EPS) * gq_ref[...]
            kt = _head_tile(kraw_ref, hd, lane)
            kn = kt * lax.rsqrt(jnp.sum(kt * kt, axis=1, keepdims=True) * (1.0 / HEAD_DIM) + RMS_EPS) * gk_ref[...]
            diags = diags + jnp.where(lane == hd, jnp.sum(qn * kn, axis=1, keepdims=True) * Q_SCALE, 0.0)
            qa_ref[:, sl] = (qn * Q_SCALE).astype(BF16)
            ka_ref[:, sl] = kn.astype(BF16)
            va = jnp.where((lane >= a) & (lane < a + 3), 1.0, _head_tile(v_s, hd, lane))
            va_ref[:, sl] = va.astype(BF16)
            vt_ref[hd] = va.T.astype(BF16)
        rel = cs - diags
        rel_ref[...] = rel
        qa_ref[...] += _dot(_pack3(rel, lane, 1.0), first_ref[...]).astype(BF16)
        ka_ref[...] += _dot(_pack3(-cs, lane, 1.0), second_ref[...]).astype(BF16)

    nb = S // tm
    heads = pl.BlockSpec((H // 2, tm, LANES), lambda i: (0, i, 0))
    return pl.pallas_call(
        body, name="attn_front", grid=(nb,),
        in_specs=[_rows(tm, D), _whole((1, D)), _whole((D, 4 * D)), _whole((D, LANES)), _whole((1, LANES)),
                  _whole((1, LANES)), _whole((1, LANES)), _whole((tm, tm)), _whole((LANES, H * LANES)),
                  _whole((LANES, H * LANES))],
        out_specs=[_rows(tm, D), heads, heads, _rows(tm, D), _rows(tm, LANES), _rows(tm, LANES), _rows(tm, LANES),
                   _rows(tm, H * LANES), _rows(tm, H * LANES), _rows(tm, H * LANES),
                   pl.BlockSpec((H, None, LANES, tm), lambda i: (0, i, 0, 0))],
        out_shape=[jax.ShapeDtypeStruct((S, D), BF16), jax.ShapeDtypeStruct((H // 2, S, LANES), F32),
                   jax.ShapeDtypeStruct((H // 2, S, LANES), F32), jax.ShapeDtypeStruct((S, D), F32),
                   jax.ShapeDtypeStruct((S, LANES), F32), jax.ShapeDtypeStruct((S, LANES), F32),
                   jax.ShapeDtypeStruct((S, LANES), F32),
                   jax.ShapeDtypeStruct((S, H * LANES), BF16), jax.ShapeDtypeStruct((S, H * LANES), BF16),
                   jax.ShapeDtypeStruct((S, H * LANES), BF16), jax.ShapeDtypeStruct((H, nb, LANES, tm), BF16)],
        scratch_shapes=[pltpu.VMEM((8, LANES), F32), pltpu.VMEM((tm, D), F32), pltpu.VMEM((tm, D), F32),
                        pltpu.VMEM((tm, D), F32)],
        compiler_params=_params(1),
    )(x1, g2, w, wf, bf, gq, gk, tri, *_scatter_matrices(H))


def _skip_tables(c, diag, gq, gk, T, G):
    nb = c.shape[0] // T
    bound = 8.0 * jnp.max(jnp.abs(gq)) * jnp.max(jnp.abs(gk))
    first, last = c[0::T, :], c[T - 1::T, :]
    lowest = jnp.maximum(jnp.min(diag.reshape(nb, T, -1), axis=1), -bound)
    idx = jnp.arange(nb)
    margin = (SKIP_LOG + bound) - lowest
    need = (last[None, :, :] <= first[:, None, :] + margin[:, None, :]) & (idx[None, :, None] < idx[:, None, None])
    need = need | (idx[None, :, None] == idx[:, None, None])
    kstart = jnp.argmax(need, axis=1)
    qend = nb - 1 - jnp.argmax(need[::-1], axis=0)
    kstart = jnp.min(kstart.reshape(2 * nb // G, G // 2, -1), axis=1)
    kstart = kstart - (kstart & 1)
    qend = jnp.max(qend.reshape(2 * nb // G, G // 2, -1), axis=1)
    return kstart.T.astype(jnp.int32), qend.T.astype(jnp.int32), bound


def _attn_fwd(kstart, qa, ka, vt, online_max):
    S = qa.shape[0]
    H = qa.shape[1] // LANES
    nb, T = vt.shape[1], vt.shape[3]
    G = 2 * nb // kstart.shape[1]
    W = G * T

    def finish(acc, shift, o_ref, lse_ref):
        a = _aug(pl.program_id(0) % 2)
        feat = lax.broadcasted_iota(jnp.int32, (LANES, 1), 0)
        l = jnp.sum(jnp.where(feat == a, acc, 0.0), axis=0, keepdims=True)
        o_ref[...] = (acc * (1.0 / l)).T
        lse_ref[...] = shift + jnp.log(l)

    def causal(st):
        return jnp.where(lax.broadcasted_iota(jnp.int32, st.shape, 0) <= lax.broadcasted_iota(jnp.int32, st.shape, 1),
                         st, NEG)

    def fast_body(ks_ref, q_ref, k_ref, vt_ref, o_ref, lse_ref, acc_ref, sa_ref, sb_ref, sc_ref):
        h, g = pl.program_id(0), pl.program_id(1)
        q = q_ref[...]
        acc_ref[...] = jnp.zeros((LANES, W), F32)

        def scores(ki, lo):
            return _dot_nt(k_ref[pl.ds(pl.multiple_of(ki * T, T), 2 * T), :], q[lo * T:, :])

        def weighted(ki, p):
            return _dot(vt_ref[ki], p[:T]) + _dot(vt_ref[ki + 1], p[T:])

        first = ks_ref[h, 2 * g + 1]
        early = jnp.minimum(ks_ref[h, 2 * g], first)

        def narrow(i, carry):
            ki = early + 2 * i
            st = _dot_nt(k_ref[pl.ds(pl.multiple_of(ki * T, T), 2 * T), :], q[:W // 2, :])
            acc_ref[:, :W // 2] += weighted(ki, jnp.exp(st).astype(BF16))
            return carry

        lax.fori_loop(0, (first - early) // 2, narrow, 0)
        steps = (g * G - first) // 2
        sa_ref[...] = scores(first, 0)

        def advance(ki, cur_ref, next_ref):
            p = jnp.exp(cur_ref[...]).astype(BF16)
            next_ref[...] = scores(ki + 2, 0)
            acc_ref[...] += weighted(ki, p)

        def loop(i, carry):
            advance(first + 4 * i, sa_ref, sb_ref)
            advance(first + 4 * i + 2, sb_ref, sa_ref)
            return carry

        lax.fori_loop(0, steps // 2, loop, 0)

        def first_own(pending_ref):
            p = jnp.exp(causal(pending_ref[...])).astype(BF16)
            if G > 2:
                sc_ref[:, :W - 2 * T] = scores(g * G + 2, 2)
            acc_ref[...] += weighted(g * G, p)

        @pl.when(steps % 2 == 1)
        def _():
            advance(g * G - 2, sa_ref, sb_ref)
            first_own(sb_ref)

        @pl.when(steps % 2 == 0)
        def _():
            first_own(sa_ref)

        if G > 2:
            acc_ref[:, 2 * T:] += weighted(g * G + 2, jnp.exp(causal(sc_ref[:, :W - 2 * T])).astype(BF16))
        for j in range(4, G, 2):
            p = jnp.exp(causal(scores(g * G + j, j))).astype(BF16)
            acc_ref[:, j * T:] += weighted(g * G + j, p)
        finish(acc_ref[...], 0.0, o_ref, lse_ref)

    def online_body(ks_ref, q_ref, k_ref, vt_ref, o_ref, lse_ref, acc_ref, m_ref):
        h, g = pl.program_id(0), pl.program_id(1)
        q = q_ref[...]
        m_ref[...] = jnp.full((8, W), NEG, F32)
        acc_ref[...] = jnp.zeros((LANES, W), F32)

        def update(st, vtb, lo):
            m_old = m_ref[0:1, lo:]
            m_new = jnp.maximum(m_old, jnp.max(st, axis=0, keepdims=True))
            p = jnp.exp(st - m_new).astype(BF16)
            acc_ref[:, lo:] = acc_ref[:, lo:] * jnp.exp(m_old - m_new) + _dot(vtb, p)
            m_ref[:, lo:] = jnp.broadcast_to(m_new, (8, W - lo))

        def loop(ki, carry):
            kb = k_ref[pl.ds(pl.multiple_of(ki * T, T), T), :]
            update(_dot_nt(kb, q), vt_ref[ki], 0)
            return carry

        lax.fori_loop(jnp.minimum(ks_ref[h, 2 * g], ks_ref[h, 2 * g + 1]), g * G, loop, 0)
        for j in range(G):
            ki = g * G + j
            kb = k_ref[pl.ds(pl.multiple_of(ki * T, T), T), :]
            update(causal(_dot_nt(kb, q[j * T:, :])), vt_ref[ki], j * T)
        finish(acc_ref[...], m_ref[0:1, :], o_ref, lse_ref)

    return pl.pallas_call(
        online_body if online_max else fast_body, name="attn_fwd_online" if online_max else "attn_fwd",
        grid_spec=pltpu.PrefetchScalarGridSpec(
            num_scalar_prefetch=1, grid=(H, nb // G),
            in_specs=[pl.BlockSpec((W, LANES), lambda h, i, ks: (i, h)),
                      pl.BlockSpec((S, LANES), lambda h, i, ks: (0, h)),
                      pl.BlockSpec((None, nb, LANES, T), lambda h, i, ks: (h, 0, 0, 0))],
            out_specs=[pl.BlockSpec((W, LANES), lambda h, i, ks: (i, h)),
                       pl.BlockSpec((None, 1, W), lambda h, i, ks: (h, 0, i))],
            scratch_shapes=[pltpu.VMEM((LANES, W), F32)] + (
                [pltpu.VMEM((8, W), F32)] if online_max else [pltpu.VMEM((2 * T, W), F32)] * 3)),
        out_shape=[jax.ShapeDtypeStruct((S, H * LANES), F32), jax.ShapeDtypeStruct((H, 1, S), F32)],
        compiler_params=_params(2),
    )(kstart, qa, ka, vt)


def _attn_out(o_aug, lse, rel, z, x1, target, w_out, qa):
    S, D = x1.shape
    H = D // HEAD_DIM
    tm = min(ROW_TILE, S)

    def body(o_ref, z_ref, x1_ref, t_ref, w_ref, q_ref, first_ref, rel_ref, lse_ref,
             dx2_ref, dx2b_ref, o2b_ref, dz_ref, doa_ref, qa2_ref, loss_ref, oc_s, do_s):
        @pl.when(pl.program_id(0) == 0)
        def _():
            loss_ref[...] = jnp.zeros((1, LANES), F32)

        lane = _lane()
        for j in range(H // 2):
            oc_s[:, LANES * j:LANES * (j + 1)] = _pair_tile(o_ref[:, 2 * LANES * j:2 * LANES * j + LANES],
                                                            o_ref[:, 2 * LANES * j + LANES:2 * LANES * (j + 1)], lane)
        oc = oc_s[...]
        zv = z_ref[...]
        sg = _sigmoid(zv)
        sz = zv * sg
        o2 = (oc * sz).astype(BF16)
        o2b_ref[...] = o2
        e = x1_ref[...] + _dot(o2, w_ref[...]) - t_ref[...]
        sq = jnp.sum(jnp.sum(e * e, axis=1, keepdims=True), axis=0, keepdims=True)
        loss_ref[...] += jnp.broadcast_to(sq * (0.5 / D), (1, LANES))
        dx2 = e * (1.0 / D)
        dx2_ref[...] = dx2
        dx2b = dx2.astype(BF16)
        dx2b_ref[...] = dx2b
        do2 = _dot_nt(dx2b, w_ref[...])
        dz_ref[...] = (do2 * oc * (sg * (1.0 + zv * (1.0 - sg)))).astype(BF16)
        do_s[...] = do2 * sz
        deltas = jnp.zeros((tm, LANES), F32)
        for hd in range(H):
            dt = _head_tile(do_s, hd, lane)
            delta = jnp.sum(dt * _head_tile(oc_s, hd, lane), axis=1, keepdims=True)
            deltas = deltas + jnp.where(lane == hd, delta, 0.0)
            doa_ref[:, LANES * hd:LANES * (hd + 1)] = dt.astype(BF16)
        doa_ref[...] += _dot(_pack3(-deltas, lane, 0.0), first_ref[...]).astype(BF16)
        lse = jnp.concatenate([lse_ref[...], jnp.zeros((LANES - H, tm), F32)], axis=0).T
        rq = rel_ref[...] - lse
        tile_lane = lax.broadcasted_iota(jnp.int32, (1, H * LANES), 1)
        extra = tile_lane % LANES - _aug((tile_lane // LANES) % 2)
        kept = jnp.where((extra >= 0) & (extra < 3), jnp.zeros((), BF16), q_ref[...])
        qa2_ref[...] = kept + _dot(_pack3(rq, lane, 0.0), first_ref[...]).astype(BF16)

    return pl.pallas_call(
        body, name="attn_out", grid=(S // tm,),
        in_specs=[_rows(tm, H * LANES), _rows(tm, D), _rows(tm, D), _rows(tm, D), _whole((D, D)),
                  _rows(tm, H * LANES), _whole((LANES, H * LANES)), _rows(tm, LANES),
                  pl.BlockSpec((H, tm), lambda i: (0, i))],
        out_specs=[_rows(tm, D), _rows(tm, D), _rows(tm, D), _rows(tm, D), _rows(tm, H * LANES),
                   _rows(tm, H * LANES), _whole((1, LANES))],
        out_shape=[jax.ShapeDtypeStruct((S, D), F32), jax.ShapeDtypeStruct((S, D), BF16),
                   jax.ShapeDtypeStruct((S, D), BF16), jax.ShapeDtypeStruct((S, D), BF16),
                   jax.ShapeDtypeStruct((S, H * LANES), BF16), jax.ShapeDtypeStruct((S, H * LANES), BF16),
                   jax.ShapeDtypeStruct((1, LANES), F32)],
        scratch_shapes=[pltpu.VMEM((tm, D), F32), pltpu.VMEM((tm, D), F32)],
        compiler_params=_params(1),
    )(o_aug, z, x1, target, w_out, qa, _scatter_matrices(H)[0], rel, lse)


def _attn_bwd(qend, qa2, doa, ka, va, T):
    S = qa2.shape[0]
    H = qa2.shape[1] // LANES
    nb = S // T
    G = 2 * nb // qend.shape[1]
    W = G * T

    def body(qe_ref, q_ref, do_ref, k_ref, v_ref, dq_ref, dk_ref, dv_ref, dkt_acc, dvt_acc):
        h, g = pl.program_id(0), pl.program_id(1)

        @pl.when(g == 0)
        def _():
            dq_ref[...] = jnp.zeros((S, LANES), F32)

        kb = k_ref[...]
        vb = v_ref[...]
        dkt_acc[...] = jnp.zeros((LANES, W), F32)
        dvt_acc[...] = jnp.zeros((LANES, W), F32)

        def step(qi, c0, c1, masked):
            rows = pl.ds(pl.multiple_of(qi * T, T), 2 * T)
            qb = q_ref[rows, :]
            dob = do_ref[rows, :]
            s = _dot_nt(qb, kb[c0:c1])
            if masked:
                query = lax.broadcasted_iota(jnp.int32, s.shape, 0) + (c1 - 2 * T)
                s = jnp.where(lax.broadcasted_iota(jnp.int32, s.shape, 1) <= query, s, NEG)
            p = jnp.exp(s)
            ds = (p * _dot_nt(dob, vb[c0:c1])).astype(BF16)
            dvt_acc[:, c0:c1] += _dot(dob.astype(F32).T.astype(BF16), p.astype(BF16))
            dkt_acc[:, c0:c1] += _dot(qb.astype(F32).T.astype(BF16), ds)
            dq_ref[rows, :] += _dot(ds, kb[c0:c1])

        for m in range(G // 2):
            step(g * G + 2 * m, 0, (m + 1) * 2 * T, True)
        first = g * G + G
        n_all = jnp.maximum((qe_ref[h, 2 * g] - first + 2) // 2, 0)
        second = first + 2 * n_all

        def all_keys(i, carry):
            step(first + 2 * i, 0, W, False)
            return carry

        def late_keys(i, carry):
            step(second + 2 * i, W // 2, W, False)
            return carry

        lax.fori_loop(0, n_all, all_keys, 0)
        lax.fori_loop(0, (qe_ref[h, 2 * g + 1] - second + 2) // 2, late_keys, 0)
        dk_ref[...] = dkt_acc[...].T
        dv_ref[...] = dvt_acc[...].T.astype(BF16)

    heads = pl.BlockSpec((None, W, LANES), lambda h, i, qe: (h, i, 0))
    return pl.pallas_call(
        body, name="attn_bwd",
        grid_spec=pltpu.PrefetchScalarGridSpec(
            num_scalar_prefetch=1, grid=(H, nb // G),
            in_specs=[pl.BlockSpec((S, LANES), lambda h, i, qe: (0, h)), pl.BlockSpec((S, LANES), lambda h, i, qe: (0, h)),
                      pl.BlockSpec((W, LANES), lambda h, i, qe: (i, h)), pl.BlockSpec((W, LANES), lambda h, i, qe: (i, h))],
            out_specs=[pl.BlockSpec((None, S, LANES), lambda h, i, qe: (h, 0, 0)), heads, heads],
            scratch_shapes=[pltpu.VMEM((LANES, W), F32), pltpu.VMEM((LANES, W), F32)]),
        out_shape=[jax.ShapeDtypeStruct((H, S, LANES), F32), jax.ShapeDtypeStruct((H, S, LANES), F32),
                   jax.ShapeDtypeStruct((H, S, LANES), BF16)],
        compiler_params=_params(2),
    )(qend, qa2, doa, ka, va)


def _attn_proj_bwd(dqt, dka, dva, qraw, kraw, dz, f, gq, gk):
    S, D = dz.shape
    H = D // HEAD_DIM
    tm = min(ROW_TILE, S)
    last = S // tm - 1
    tri = (lax.broadcasted_iota(jnp.int32, (tm, tm), 1) >= lax.broadcasted_iota(jnp.int32, (tm, tm), 0)).astype(BF16)

    def body(dq_ref, dk_ref, dv_ref, q_ref, k_ref, dz_ref, f_ref, gq_ref, gk_ref, tri_ref, ones_ref,
             dproj_ref, small_ref, carry, pairs):
        @pl.when(pl.program_id(0) == 0)
        def _():
            small_ref[...] = jnp.zeros((8, LANES), F32)
            carry[...] = jnp.zeros((8, LANES), F32)

        lane = _lane()

        def head_pair(j, acc):
            dcs, dgq, dgk = acc
            dq2, dk2 = [], []
            q_pair, k_pair = q_ref[j], k_ref[j]
            for parity in (0, 1):
                hd = 2 * j + parity
                own, a = _own(lane, parity), _aug(parity)
                dqf = dq_ref[hd]
                dqn = jnp.where(own, dqf * Q_SCALE, 0.0)
                d, dg = _head_rms_bwd(dqn, jnp.where(own, q_pair, 0.0), gq_ref[...], ones_ref[...])
                dq2.append(d)
                dgq = dgq + dg
                dkt = dk_ref[hd]
                dcs = dcs + jnp.where(lane == hd, _col(dqf, lane, a) - _col(dkt, lane, a + 3), 0.0)
                d, dg = _head_rms_bwd(jnp.where(own, dkt, 0.0), jnp.where(own, k_pair, 0.0), gk_ref[...], ones_ref[...])
                dk2.append(d)
                dgk = dgk + dg
            pairs[0, j] = _pair_tile(*dq2, lane).astype(BF16)
            pairs[1, j] = _pair_tile(*dk2, lane).astype(BF16)
            pairs[2, j] = _pair_tile(dv_ref[2 * j], dv_ref[2 * j + 1], lane)
            return dcs, dgq, dgk

        zero = jnp.zeros((1, LANES), F32)
        dcs, dgq, dgk = lax.fori_loop(0, H // 2, head_pair, (jnp.zeros((tm, LANES), F32), zero, zero))
        for part in range(3):
            for j in range(H // 2):
                dproj_ref[:, part * D + LANES * j:part * D + LANES * (j + 1)] = pairs[part, j]
        dproj_ref[:, 3 * D:4 * D] = dz_ref[...]
        dlogf = _dot01(tri_ref[...], dcs) + carry[0:1, :]
        carry[...] = jnp.broadcast_to(dlogf[0:1, :], (8, LANES))
        df = dlogf * (1.0 / (1.0 + jnp.exp(f_ref[...])))
        dproj_ref[:, 4 * D:4 * D + LANES] = df.astype(BF16)
        small_ref[0:1, :] += jnp.sum(df, axis=0, keepdims=True)
        small_ref[1:2, :] += dgq
        small_ref[2:3, :] += dgk

    W = 4 * D + LANES
    heads = pl.BlockSpec((H, tm, LANES), lambda i: (0, last - i, 0))
    head_pairs = pl.BlockSpec((H // 2, tm, LANES), lambda i: (0, last - i, 0))
    return pl.pallas_call(
        body, name="attn_proj_bwd", grid=(S // tm,),
        in_specs=[heads, heads, heads, head_pairs, head_pairs,
                  _rows(tm, D, last), _rows(tm, LANES, last), _whole((1, LANES)), _whole((1, LANES)),
                  _whole((tm, tm)), _whole((LANES, LANES))],
        out_specs=[_rows(tm, W, last), _whole((8, LANES))],
        out_shape=[jax.ShapeDtypeStruct((S, W), BF16), jax.ShapeDtypeStruct((8, LANES), F32)],
        scratch_shapes=[pltpu.VMEM((8, LANES), F32), pltpu.VMEM((3, H // 2, tm, LANES), BF16)],
        compiler_params=_params(1),
    )(dqt, dka, dva, qraw, kraw, dz, f, gq, gk, tri, jnp.ones((LANES, LANES), BF16))


def _matmul_tn(a, b, col0, n, tn, name, stacked=False):
    S, M = a.shape
    ts = min(TN_ROWS, S)
    off = col0 // tn

    def body(a_ref, b_ref, o_ref):
        @pl.when(pl.program_id(1) == 0)
        def _():
            o_ref[...] = jnp.zeros((M, tn), F32)

        o_ref[...] += _dot_tn(a_ref[...], b_ref[...])

    if stacked:
        out_spec, out_shape = pl.BlockSpec((None, M, tn), lambda j, s: (j, 0, 0)), (n // tn, M, tn)
    else:
        out_spec, out_shape = pl.BlockSpec((M, tn), lambda j, s: (0, j)), (M, n)
    return pl.pallas_call(
        body, name=name, grid=(n // tn, S // ts),
        in_specs=[pl.BlockSpec((ts, M), lambda j, s: (s, 0)), pl.BlockSpec((ts, tn), lambda j, s: (s, off + j))],
        out_specs=out_spec, out_shape=jax.ShapeDtypeStruct(out_shape, F32),
        compiler_params=_params(2),
    )(a, b)


def _adam_update(gv, w_ref, m_ref, v_ref, d_ref, m2_ref, v2_ref):
    m2 = ADAM_B1 * m_ref[...] + (1.0 - ADAM_B1) * gv
    v2 = ADAM_B2 * v_ref[...] + (1.0 - ADAM_B2) * (gv * gv)
    m2_ref[...] = m2
    v2_ref[...] = v2
    m_hat = m2 / (1.0 - ADAM_B1 ** ADAM_STEP)
    v_hat = v2 / (1.0 - ADAM_B2 ** ADAM_STEP)
    d_ref[...] = -ADAM_LR * (m_hat / (jnp.sqrt(v_hat) + ADAM_EPS) + ADAM_WD * w_ref[...])


def _adamw(w, g, m, v, name):
    r, c = w.shape
    tr = ROW_TILE if r % ROW_TILE == 0 else r

    def body(w_ref, g_ref, m_ref, v_ref, d_ref, m2_ref, v2_ref):
        _adam_update(g_ref[...], w_ref, m_ref, v_ref, d_ref, m2_ref, v2_ref)

    spec = _rows(tr, c)
    return pl.pallas_call(
        body, name=name, grid=(r // tr,), in_specs=[spec] * 4, out_specs=[spec] * 3,
        out_shape=[jax.ShapeDtypeStruct((r, c), F32)] * 3, compiler_params=_params(1),
    )(w, g, m, v)


def _adamw_halves(w, mine, other, m, v, core, name):
    r, c = mine.shape
    tr = ROW_TILE if r % ROW_TILE == 0 else r
    per = r // tr

    def body(core_ref, w_ref, mine_ref, other_ref, m_ref, v_ref, g_ref, d_ref, m2_ref, v2_ref):
        gv = jnp.where(pl.program_id(0) // per == core_ref[0], mine_ref[...], other_ref[...])
        g_ref[...] = gv
        _adam_update(gv, w_ref, m_ref, v_ref, d_ref, m2_ref, v2_ref)

    full = pl.BlockSpec((tr, c), lambda i, core: (i, 0))
    half = pl.BlockSpec((tr, c), lambda i, core: (i % per, 0))
    return pl.pallas_call(
        body, name=name,
        grid_spec=pltpu.PrefetchScalarGridSpec(num_scalar_prefetch=1, grid=(2 * per,),
                                               in_specs=[full, half, half, full, full], out_specs=[full] * 4),
        out_shape=[jax.ShapeDtypeStruct((2 * r, c), F32)] * 4, compiler_params=_params(1),
    )(core, w, mine, other, m, v)


def _local_step(x, target, g1, w_in, conv_w, w_out, g2, wa_in, b_f, gq, gk, wa_out):
    S, D = x.shape
    H = D // HEAD_DIM
    ws = wa_in.shape[2]
    w_qkvz = jnp.concatenate([wa_in[0], wa_in[1], wa_in[2], wa_in[3][:, :4 * D - 3 * ws]], axis=1)
    wf = jnp.pad(wa_in[3][:, 4 * D - 3 * ws:], ((0, 0), (0, LANES - H)))
    bf = jnp.pad(b_f, ((0, 0), (0, LANES - H)))
    gq128 = jnp.concatenate([gq, gq], axis=1)
    gk128 = jnp.concatenate([gk, gk], axis=1)

    proj, h1, yc, y, x1 = _conv_fwd(x, g1, w_in, conv_w, w_out)
    h2, qraw, kraw, z, f, c, rel, qa, ka, va, vt = _attn_front(x1, g2, w_qkvz, wf, bf, gq128, gk128)
    T = vt.shape[3]
    kstart, qend, bound = _skip_tables(c[:, :H], (c - rel)[:, :H], gq, gk, T, min(ATT_GROUP, S // T))
    o_aug, lse = lax.cond(2.0 * bound <= PLAIN_EXP_MAX, functools.partial(_attn_fwd, online_max=False),
                          functools.partial(_attn_fwd, online_max=True), kstart, qa, ka, vt)
    dx2, dx2b, o2b, dz, doa, qa2, loss = _attn_out(o_aug, lse.reshape(H, S), rel, z, x1, target, wa_out, qa)
    dqt, dka, dva = _attn_bwd(qend, qa2, doa, ka, va, T)
    dproj2, small = _attn_proj_bwd(dqt, dka, dva, qraw, kraw, dz, f, gq128, gk128)
    dproj1, dx, dx1b, dg1, dcw, dg2 = _conv_bwd(dproj2, w_qkvz, wf, x1, g2, dx2, x, g1, w_in, w_out, conv_w, proj, yc)

    tn = min(1024, D)
    dwa_out = _matmul_tn(o2b, dx2b, 0, D, tn, "dw_attn_out")
    dw_main = _matmul_tn(h2, dproj2, 0, 4 * D, tn, "dw_attn_in")
    dw_f = _matmul_tn(h2, dproj2, 4 * D, LANES, LANES, "dw_attn_f")
    dwa_in = jnp.stack([dw_main[:, 0:ws], dw_main[:, ws:2 * ws], dw_main[:, 2 * ws:3 * ws],
                        jnp.concatenate([dw_main[:, 3 * ws:], dw_f[:, :H]], axis=1)])
    dw_out = _matmul_tn(y, dx1b, 0, D, tn, "dw_conv_out")
    dw_in = _matmul_tn(h1, dproj1, 0, 4 * D, D, "dw_conv_in", stacked=True)
    grads = dict(conv_norm_g=dg1, conv_w_in=dw_in, conv_w=dcw, conv_w_out=dw_out, attn_norm_g=dg2,
                 attn_w_in=dwa_in, attn_b_f=small[0:1, :H],
                 attn_q_norm_g=small[1:2, :HEAD_DIM] + small[1:2, HEAD_DIM:],
                 attn_k_norm_g=small[2:3, :HEAD_DIM] + small[2:3, HEAD_DIM:], attn_w_out=dwa_out)
    return loss[0, 0], dx, grads


def _coords():
    return lax.axis_index("x"), lax.axis_index("y"), lax.axis_index("c")


def _at(ref, idx):
    return ref.at[idx] if idx else ref


def _other_chips(x, y):
    return [(1 - x, y), (x, 1 - y), (1 - x, 1 - y)]


def _all_gather(halved, whole):
    nh, nw = len(halved), len(whole)

    def body(*refs):
        src_h, src_w = refs[:nh], refs[nh:nh + nw]
        out_h, out_w = refs[nh + nw:2 * nh + nw], refs[2 * nh + nw:2 * (nh + nw)]
        send_h, recv_h, send_w, recv_w = refs[2 * (nh + nw):]
        x, y, c = _coords()
        mine = 2 * x + y
        sibling = (x, y, 1 - c)
        chips = _other_chips(x, y)

        def copy_h(a, k, chip, half, to, src=None):
            dst = out_h[a].at[chip, half]
            return pltpu.make_async_remote_copy(src_ref=dst if src is None else src, dst_ref=dst,
                                                send_sem=send_h.at[a, k], recv_sem=recv_h.at[a, k],
                                                device_id=to, device_id_type=MESH)

        def copy_w(a, k, chip, to):
            return pltpu.make_async_remote_copy(src_ref=src_w[a], dst_ref=out_w[a].at[chip],
                                                send_sem=send_w.at[a, k], recv_sem=recv_w.at[a, k],
                                                device_id=to, device_id_type=MESH)

        first = [copy_h(a, k, mine, c, (*chip, c), src=src_h[a].at[c]) for a in range(nh) for k, chip in enumerate(chips)]
        first += [copy_w(a, k, mine, (*chip, c)) for a in range(nw) for k, chip in enumerate(chips)]
        own = [pltpu.make_async_remote_copy(src_ref=src_h[a], dst_ref=out_h[a].at[mine], send_sem=send_h.at[a, 6],
                                            recv_sem=recv_h.at[a, 6], device_id=sibling, device_id_type=MESH)
               for a in range(nh)]
        own += [copy_w(a, 3, mine, sibling) for a in range(nw)]
        for cp in first + own:
            cp.start()
        passed = []
        for a in range(nh):
            for k, (px, py) in enumerate(chips):
                copy_h(a, k, 2 * px + py, c, (x, y, c)).wait_recv()
                cp = copy_h(a, 3 + k, 2 * px + py, c, sibling)
                cp.start()
                passed.append(cp)
        for a in range(nh):
            for k, (px, py) in enumerate(chips):
                copy_h(a, 3 + k, 2 * px + py, 1 - c, (x, y, c)).wait_recv()
        for a in range(nw):
            for k, (px, py) in enumerate(chips):
                copy_w(a, k, 2 * px + py, (x, y, c)).wait_recv()
        for cp in own:
            cp.wait_recv()
        for cp in first + passed + own:
            cp.wait_send()

    out_shape = [jax.ShapeDtypeStruct((4,) + a.shape, a.dtype) for a in list(halved) + list(whole)]
    return pl.pallas_call(
        body, name="gather_weights", in_specs=[ANY] * (nh + nw), out_specs=[ANY] * (nh + nw), out_shape=out_shape,
        scratch_shapes=[pltpu.SemaphoreType.DMA((nh, 7)), pltpu.SemaphoreType.DMA((nh, 7)),
                        pltpu.SemaphoreType.DMA((nw, 4)), pltpu.SemaphoreType.DMA((nw, 4))],
    )(*halved, *whole)


def _exchange(name, srcs, lands, copies, local_copies):
    ns, nl, n, nloc = len(srcs), len(lands), len(copies), len(local_copies)

    def body(*refs):
        src, land = refs[:ns], refs[ns:ns + nl]
        send, recv, local_sem = refs[ns + nl:]
        me = _coords()
        started = []
        for k, (si, s_at, li, l_at, ci) in enumerate(local_copies):
            cp = pltpu.make_async_copy(_at(src[si], s_at(*me)), _at(land[li], l_at(*me)), local_sem.at[k])
            cp.start()
            started.append(cp)
        remote = []
        for k, (si, s_at, li, l_at, peer) in enumerate(copies):
            cp = pltpu.make_async_remote_copy(src_ref=_at(src[si], s_at(*me)), dst_ref=_at(land[li], l_at(*me)),
                                              send_sem=send.at[k], recv_sem=recv.at[k],
                                              device_id=peer(*me), device_id_type=MESH)
            cp.start()
            remote.append(cp)
        for cp in remote:
            cp.wait()
        for cp in started:
            cp.wait()

    return pl.pallas_call(
        body, name=name, in_specs=[ANY] * ns, out_specs=[ANY] * nl, out_shape=list(lands),
        scratch_shapes=[pltpu.SemaphoreType.DMA((n,)), pltpu.SemaphoreType.DMA((n,)),
                        pltpu.SemaphoreType.DMA((max(nloc, 1),))],
    )(*srcs)


def _add_pairs(a, b, core, name):
    _, r, cols = b.shape
    tr = ROW_TILE if r % ROW_TILE == 0 else r

    def body(core_ref, a_ref, b_ref, o_ref, ob_ref):
        s = a_ref[...] + b_ref[...]
        o_ref[...] = s
        ob_ref[...] = s.astype(BF16)

    spec = pl.BlockSpec((None, tr, cols), lambda j, i, core: (j, i, 0))
    return pl.pallas_call(
        body, name=name,
        grid_spec=pltpu.PrefetchScalarGridSpec(
            num_scalar_prefetch=1, grid=(4, r // tr),
            in_specs=[pl.BlockSpec((None, None, tr, cols), lambda j, i, core: (j, core[0], i, 0)), spec],
            out_specs=[spec, spec]),
        out_shape=[jax.ShapeDtypeStruct(b.shape, F32), jax.ShapeDtypeStruct(b.shape, BF16)],
        compiler_params=_params(2),
    )(core, a, b)


def _sum_chips(own, landed, name):
    _, r, cols = landed.shape
    tr = ROW_TILE if r % ROW_TILE == 0 else r

    def body(own_ref, land_ref, o_ref):
        acc = own_ref[...]
        for j in range(3):
            acc = acc + land_ref[j].astype(F32)
        o_ref[...] = acc

    return pl.pallas_call(
        body, name=name, grid=(r // tr,),
        in_specs=[_rows(tr, cols), pl.BlockSpec((3, tr, cols), lambda i: (0, i, 0))], out_specs=_rows(tr, cols),
        out_shape=jax.ShapeDtypeStruct((r, cols), F32), compiler_params=_params(1),
    )(own, landed)


def _sum_devices(landed, name):
    def body(l_ref, o_ref):
        acc = l_ref[0]
        for j in range(1, 8):
            acc = acc + l_ref[j]
        o_ref[...] = acc

    return pl.pallas_call(body, name=name, out_shape=jax.ShapeDtypeStruct(landed.shape[1:], F32))(landed)


def _reduce_gradients(big, small):
    nb = len(big)
    x, y, c = _coords()
    mine = 2 * x + y
    flips = [(fx, fy, fc) for fx in (0, 1) for fy in (0, 1) for fc in (0, 1) if fx or fy or fc]

    def flip(fx, fy, fc):
        return lambda x, y, c: (x ^ fx, y ^ fy, c ^ fc)

    copies = [(a, (lambda j: lambda x, y, c: (j, 1 - c))(j), a, (lambda j: lambda x, y, c: (j,))(j), flip(0, 0, 1))
              for a in range(nb) for j in range(4)]
    copies += [(nb, lambda x, y, c: (), nb, lambda x, y, c: (4 * x + 2 * y + c,), flip(*f)) for f in flips]
    lands = [jax.ShapeDtypeStruct((4,) + g.shape[2:], F32) for g in big] + [jax.ShapeDtypeStruct((8,) + small.shape, F32)]
    local = [(nb, lambda x, y, c: (), nb, lambda x, y, c: (4 * x + 2 * y + c,), None)]
    landed = _exchange("swap_halves", list(big) + [small], lands, copies, local)
    small_sum = _sum_devices(landed[nb], "sum_small")

    chip_f32, chip_bf16 = [], []
    for a in range(nb):
        s, sb = _add_pairs(big[a], landed[a], jnp.reshape(c, (1,)).astype(jnp.int32), f"add_cores_{a}")
        chip_f32.append(s)
        chip_bf16.append(sb)

    chip_flips = [(1, 0), (0, 1), (1, 1)]
    copies = [(a, (lambda f: lambda x, y, c: (2 * (x ^ f[0]) + (y ^ f[1]),))(f), a, (lambda k: lambda x, y, c: (k,))(k),
               flip(f[0], f[1], 0)) for a in range(nb) for k, f in enumerate(chip_flips)]
    lands = [jax.ShapeDtypeStruct((3,) + g.shape[1:], BF16) for g in chip_bf16]
    landed = _exchange("send_chip_sums", chip_bf16, lands, copies, [])
    totals = [_sum_chips(lax.dynamic_index_in_dim(chip_f32[a], mine, axis=0, keepdims=False), landed[a],
                         f"sum_chips_{a}") for a in range(nb)]

    copies = [(a, lambda x, y, c: (), a, lambda x, y, c: (), flip(0, 0, 1)) for a in range(nb)]
    lands = [jax.ShapeDtypeStruct(t.shape, F32) for t in totals]
    return list(zip(totals, _exchange("swap_sums", totals, lands, copies, []))), small_sum


def kernel(x, conv_norm_g, conv_w_in, conv_w, conv_w_out, attn_norm_g, attn_w_in, attn_b_f, attn_q_norm_g, attn_k_norm_g, attn_w_out, loss_target, m_conv_norm_g, m_conv_w_in, m_conv_w, m_conv_w_out, m_attn_norm_g, m_attn_w_in, m_attn_b_f, m_attn_q_norm_g, m_attn_k_norm_g, m_attn_w_out, v_conv_norm_g, v_conv_w_in, v_conv_w, v_conv_w_out, v_attn_norm_g, v_attn_w_in, v_attn_b_f, v_attn_q_norm_g, v_attn_k_norm_g, v_attn_w_out):
    xi, yi, _ = _coords()
    chip = 2 * xi + yi
    D = x.shape[2]
    H = D // HEAD_DIM
    names = ["conv_norm_g", "conv_w_in", "conv_w", "conv_w_out", "attn_norm_g", "attn_w_in", "attn_b_f",
             "attn_q_norm_g", "attn_k_norm_g", "attn_w_out"]
    weights = dict(zip(names, [conv_norm_g, conv_w_in, conv_w, conv_w_out, attn_norm_g, attn_w_in, attn_b_f,
                               attn_q_norm_g, attn_k_norm_g, attn_w_out]))
    m_in = dict(zip(names, [m_conv_norm_g, m_conv_w_in, m_conv_w, m_conv_w_out, m_attn_norm_g, m_attn_w_in,
                            m_attn_b_f, m_attn_q_norm_g, m_attn_k_norm_g, m_attn_w_out]))
    v_in = dict(zip(names, [v_conv_norm_g, v_conv_w_in, v_conv_w, v_conv_w_out, v_attn_norm_g, v_attn_w_in,
                            v_attn_b_f, v_attn_q_norm_g, v_attn_k_norm_g, v_attn_w_out]))
    weights = {k: w[0] for k, w in weights.items()}
    m_in = {k: w[0] for k, w in m_in.items()}
    v_in = {k: w[0] for k, w in v_in.items()}

    big_names = ["conv_w_in", "attn_w_in", "conv_w_out", "attn_w_out"]
    halved = [weights[k].astype(BF16).reshape(2, weights[k].shape[0] // 2, weights[k].shape[1]) for k in big_names]
    q = D // 4
    small_w = jnp.concatenate([weights["conv_w"], weights["attn_norm_g"][None, :], jnp.zeros((4, q), F32)], axis=0)
    g_in, ga_in, g_out, ga_out, g_small = _all_gather(halved, [small_w])
    w_in = g_in.reshape(4, D, D)
    wa_in = ga_in.reshape(4, D, D + H // 4)
    w_out = g_out.reshape(D, D)
    wa_out = ga_out.reshape(D, D)
    conv_w_full = g_small[:, 0:3, :].transpose(1, 0, 2).reshape(3, D)
    attn_g_full = g_small[:, 3, :].reshape(1, D)

    loss_part, grad_x, grads = _local_step(x[0], loss_target[0], weights["conv_norm_g"][None, :], w_in, conv_w_full,
                                           w_out, attn_g_full, wa_in, weights["attn_b_f"][None, :],
                                           weights["attn_q_norm_g"][None, :], weights["attn_k_norm_g"][None, :], wa_out)

    big = [grads["conv_w_in"].reshape(4, 2, D // 2, D),
           grads["attn_w_in"].reshape(4, 2, D // 2, D + H // 4),
           grads["conv_w_out"].reshape(4, 2, D // 8, D), grads["attn_w_out"].reshape(4, 2, D // 8, D)]
    tail = jnp.concatenate([grads["attn_b_f"], grads["attn_q_norm_g"], grads["attn_k_norm_g"],
                            jnp.reshape(loss_part, (1, 1)), jnp.zeros((1, D - H - 2 * HEAD_DIM - 1), F32)], axis=1)
    small = jnp.concatenate([grads["conv_norm_g"], grads["conv_w"], grads["attn_norm_g"], tail,
                             jnp.zeros((2, D), F32)], axis=0)
    reduced, small_sum = _reduce_gradients(big, small)
    final = {}
    final["conv_norm_g"] = small_sum[0]
    final["conv_w"] = lax.dynamic_slice_in_dim(small_sum[1:4], chip * q, q, axis=1)
    final["attn_norm_g"] = lax.dynamic_slice_in_dim(small_sum[4], chip * q, q, axis=0)
    final["attn_b_f"] = small_sum[5, :H]
    final["attn_q_norm_g"] = small_sum[5, H:H + HEAD_DIM]
    final["attn_k_norm_g"] = small_sum[5, H + HEAD_DIM:H + 2 * HEAD_DIM]
    loss = small_sum[5, H + 2 * HEAD_DIM]

    delta, new_m, new_v = {}, {}, {}
    core = jnp.reshape(lax.axis_index("c"), (1,)).astype(jnp.int32)
    for k, (mine, other) in zip(big_names, reduced):
        final[k], delta[k], new_m[k], new_v[k] = _adamw_halves(weights[k], mine, other, m_in[k], v_in[k], core,
                                                               "adamw_" + k)
    for k in names:
        if k in big_names:
            continue
        shape = weights[k].shape
        as2d = (lambda a: a.reshape(1, -1)) if len(shape) == 1 else (lambda a: a)
        d, m2, v2 = _adamw(as2d(weights[k]), as2d(final[k]), as2d(m_in[k]), as2d(v_in[k]), "adamw_" + k)
        delta[k], new_m[k], new_v[k] = d.reshape(shape), m2.reshape(shape), v2.reshape(shape)
    lead = lambda a: a[None]
    return (loss, grad_x[None], *[lead(final[k]) for k in names], *[lead(delta[k]) for k in names],
            *[lead(new_m[k]) for k in names], *[lead(new_v[k]) for k in names])
```

```python
import functools

import jax
import jax.numpy as jnp
from jax import lax
from jax.experimental import pallas as pl
from jax.experimental.pallas import tpu as pltpu

F32 = jnp.float32
BF16 = jnp.bfloat16
HEAD_DIM = 64
LANES = 128
RMS_EPS = 1e-6
NEG = -1e30
Q_SCALE = 0.125
ROW_TILE = 256
CONV_TILE = 512
ATT_GROUP = 4
SKIP_LOG = 106.0
PLAIN_EXP_MAX = 60.0
TN_ROWS = 2048
VMEM_LIMIT = 56 << 20
ADAM_LR, ADAM_B1, ADAM_B2, ADAM_EPS, ADAM_WD, ADAM_STEP = 0.001, 0.9, 0.999, 1e-08, 0.01, 10
MESH = pl.DeviceIdType.MESH
ANY = pl.BlockSpec(memory_space=pl.ANY)


def _lane():
    return lax.broadcasted_iota(jnp.int32, (1, LANES), 1)


def _split3(x):
    hi = x.astype(BF16).astype(F32)
    r = x - hi
    mid = r.astype(BF16).astype(F32)
    lo = (r - mid).astype(BF16).astype(F32)
    return hi, mid, lo


STAT_STRIDE = 16
ONE_LANE = 3 * STAT_STRIDE


def _pack3(x, lane, one):
    hi, mid, lo = _split3(x)
    packed = hi + pltpu.roll(mid, STAT_STRIDE, 1) + pltpu.roll(lo, 2 * STAT_STRIDE, 1)
    return jnp.where(lane == ONE_LANE, one, packed).astype(BF16)


def _scatter_matrices(H):
    rows = lax.broadcasted_iota(jnp.int32, (LANES, H * LANES), 0)
    cols = lax.broadcasted_iota(jnp.int32, (LANES, H * LANES), 1)
    head, within = cols // LANES, cols % LANES
    extra = within - _aug(head % 2)
    term = (rows < ONE_LANE) & (rows % STAT_STRIDE == head)
    first = ((term & (extra == rows // STAT_STRIDE)) | ((rows == ONE_LANE) & (extra >= 3) & (extra < 6)))
    second = ((term & (extra - 3 == rows // STAT_STRIDE)) | ((rows == ONE_LANE) & (extra >= 0) & (extra < 3)))
    return first.astype(BF16), second.astype(BF16)


def _put(base, lane, start, parts):
    for j, p in enumerate(parts):
        base = jnp.where(lane == start + j, p, base)
    return base


def _col(x, lane, idx):
    return jnp.sum(jnp.where(lane == idx, x, 0.0), axis=1, keepdims=True)


def _feat(parity):
    return HEAD_DIM * parity


def _aug(parity):
    return HEAD_DIM * (1 - parity)


def _own(lane, parity):
    return (lane >= _feat(parity)) & (lane < _feat(parity) + HEAD_DIM)


def _head_tile(ref, hd, lane):
    j = hd // 2
    return jnp.where(_own(lane, hd % 2), ref[:, LANES * j:LANES * (j + 1)], 0.0)


def _pair_tile(even, odd, lane):
    return jnp.where(lane < HEAD_DIM, even, odd)


def _sigmoid(x):
    return 0.5 * jnp.tanh(0.5 * x) + 0.5


def _dot(a, b):
    return jnp.dot(a, b, preferred_element_type=F32)


def _dot_nt(a, b):
    return lax.dot_general(a, b, (((1,), (1,)), ((), ())), preferred_element_type=F32)


def _dot_tn(a, b):
    return lax.dot_general(a, b, (((0,), (0,)), ((), ())), preferred_element_type=F32)


def _dot01(tri, x):
    hi, mid, lo = _split3(x)
    return _dot(tri, hi.astype(BF16)) + _dot(tri, mid.astype(BF16)) + _dot(tri, lo.astype(BF16))


def _rms_bwd(dh, x, g):
    inv = lax.rsqrt(jnp.mean(x * x, axis=-1, keepdims=True) + RMS_EPS)
    xh = x * inv
    dxn = dh * g
    dx = inv * (dxn - xh * jnp.mean(dxn * xh, axis=-1, keepdims=True))
    return dx, jnp.sum(dh * xh, axis=0, keepdims=True)


def _head_rms_bwd(dn, t, g, ones):
    sq = t * t
    hi = sq.astype(BF16)
    lo = (sq - hi.astype(F32)).astype(BF16)
    inv = lax.rsqrt((_dot(hi, ones) + _dot(lo, ones)) * (1.0 / HEAD_DIM) + RMS_EPS)
    th = t * inv
    gd = dn * g
    d = inv * (gd - th * (jnp.sum(gd * th, axis=1, keepdims=True) * (1.0 / HEAD_DIM)))
    return d, jnp.sum(dn * th, axis=0, keepdims=True)


def _params(n_grid):
    return pltpu.CompilerParams(dimension_semantics=("arbitrary",) * n_grid, vmem_limit_bytes=VMEM_LIMIT)


def _rows(tm, cols, rev=None):
    if rev is None:
        return pl.BlockSpec((tm, cols), lambda i: (i, 0))
    return pl.BlockSpec((tm, cols), lambda i: (rev - i, 0))


def _whole(shape, buffers=None):
    mode = {} if buffers is None else dict(pipeline_mode=pl.Buffered(buffers))
    return pl.BlockSpec(shape, lambda *_: (0,) * len(shape), **mode)


def _conv_fwd(x, g1, w_in, conv_w, w_out):
    S, D = x.shape
    tm = min(CONV_TILE, S)
    sub = min(ROW_TILE, tm)

    def body(x_ref, g_ref, win_ref, cw_ref, wout_ref, proj_ref, h_ref, yc_ref, y_ref, x1_ref, prev_u):
        @pl.when(pl.program_id(0) == 0)
        def _():
            prev_u[...] = jnp.zeros((sub, D), F32)

        for r in range(0, tm, sub):
            rows = slice(r, r + sub)
            xv = x_ref[rows, :]
            inv = lax.rsqrt(jnp.mean(xv * xv, axis=-1, keepdims=True) + RMS_EPS)
            h = (xv * inv * g_ref[...]).astype(BF16)
            h_ref[rows, :] = h
            for j in range(4):
                proj_ref[rows, j * D:(j + 1) * D] = _dot(h, win_ref[j])
            u = proj_ref[rows, D:2 * D] * proj_ref[rows, 2 * D:3 * D]
            pu = prev_u[...]
            row = lax.broadcasted_iota(jnp.int32, (sub, 1), 0)
            u1 = jnp.where(row < 1, pltpu.roll(pu, 1, 0), pltpu.roll(u, 1, 0))
            u2 = jnp.where(row < 2, pltpu.roll(pu, 2, 0), pltpu.roll(u, 2, 0))
            prev_u[...] = u
            w = cw_ref[...]
            yc = w[2:3] * u + w[1:2] * u1 + w[0:1] * u2
            yc_ref[rows, :] = yc
            z = proj_ref[rows, 3 * D:4 * D]
            y = (proj_ref[rows, 0:D] * yc * (z * _sigmoid(z))).astype(BF16)
            y_ref[rows, :] = y
            x1_ref[rows, :] = xv + _dot(y, wout_ref[...])

    return pl.pallas_call(
        body, name="conv_fwd", grid=(S // tm,),
        in_specs=[_rows(tm, D), _whole((1, D)), _whole((4, D, D), 1), _whole((3, D)), _whole((D, D), 1)],
        out_specs=[_rows(tm, 4 * D), _rows(tm, D), _rows(tm, D), _rows(tm, D), _rows(tm, D)],
        out_shape=[jax.ShapeDtypeStruct((S, 4 * D), F32), jax.ShapeDtypeStruct((S, D), BF16),
                   jax.ShapeDtypeStruct((S, D), F32), jax.ShapeDtypeStruct((S, D), BF16),
                   jax.ShapeDtypeStruct((S, D), F32)],
        scratch_shapes=[pltpu.VMEM((sub, D), F32)],
        compiler_params=_params(1),
    )(x, g1, w_in, conv_w, w_out)


def _conv_bwd(dproj2, wa, wf, x1, g2, dx2, x, g1, w_in, w_out, conv_w, proj, yc):
    S, D = x.shape
    tm = min(ROW_TILE, S)
    sub = min(ROW_TILE, tm)
    last = S // tm - 1

    def body(dp2_ref, wa_ref, wf_ref, x1_ref, g2_ref, dx2_ref, x_ref, g_ref, win_ref, wout_ref, cw_ref, proj_ref, yc_ref,
             dproj_ref, dx_ref, dx1b_ref, dg_ref, dcw_ref, dg2_ref, next_d):
        @pl.when(pl.program_id(0) == 0)
        def _():
            dg_ref[...] = jnp.zeros((1, D), F32)
            dcw_ref[...] = jnp.zeros((3, D), F32)
            dg2_ref[...] = jnp.zeros((1, D), F32)
            next_d[...] = jnp.zeros((sub, D), F32)

        for r in range(tm - sub, -1, -sub):
            rows = slice(r, r + sub)
            dh2 = _dot_nt(dp2_ref[rows, 0:4 * D], wa_ref[...]) + _dot_nt(dp2_ref[rows, 4 * D:4 * D + LANES], wf_ref[...])
            dxn2, dg2 = _rms_bwd(dh2, x1_ref[rows, :], g2_ref[...])
            dg2_ref[...] += dg2
            dx1v = dx2_ref[rows, :] + dxn2
            dx1b = dx1v.astype(BF16)
            dx1b_ref[rows, :] = dx1b
            dy = _dot_nt(dx1b, wout_ref[...])
            b = proj_ref[rows, 0:D]
            c = proj_ref[rows, D:2 * D]
            xin = proj_ref[rows, 2 * D:3 * D]
            z = proj_ref[rows, 3 * D:4 * D]
            sg = _sigmoid(z)
            sz = z * sg
            ycv = yc_ref[rows, :]
            d0 = dy * b * sz
            dproj_ref[rows, 0:D] = (dy * ycv * sz).astype(BF16)
            dproj_ref[rows, 3 * D:4 * D] = (dy * b * ycv * (sg * (1.0 + z * (1.0 - sg)))).astype(BF16)
            nd = next_d[...]
            row = lax.broadcasted_iota(jnp.int32, (sub, 1), 0)
            d1 = jnp.where(row >= sub - 1, pltpu.roll(nd, sub - 1, 0), pltpu.roll(d0, sub - 1, 0))
            d2 = jnp.where(row >= sub - 2, pltpu.roll(nd, sub - 2, 0), pltpu.roll(d0, sub - 2, 0))
            next_d[...] = d0
            w = cw_ref[...]
            du = w[2:3] * d0 + w[1:2] * d1 + w[0:1] * d2
            u = c * xin
            dcw_ref[2:3, :] += jnp.sum(d0 * u, axis=0, keepdims=True)
            dcw_ref[1:2, :] += jnp.sum(d1 * u, axis=0, keepdims=True)
            dcw_ref[0:1, :] += jnp.sum(d2 * u, axis=0, keepdims=True)
            dproj_ref[rows, D:2 * D] = (du * xin).astype(BF16)
            dproj_ref[rows, 2 * D:3 * D] = (du * c).astype(BF16)
            dh = _dot_nt(dproj_ref[rows, 0:D], win_ref[0])
            for j in range(1, 4):
                dh = dh + _dot_nt(dproj_ref[rows, j * D:(j + 1) * D], win_ref[j])
            dxn, dg = _rms_bwd(dh, x_ref[rows, :], g_ref[...])
            dx_ref[rows, :] = dx1v + dxn
            dg_ref[...] += dg

    return pl.pallas_call(
        body, name="conv_bwd", grid=(S // tm,),
        in_specs=[_rows(tm, 4 * D + LANES, last), _whole((D, 4 * D), 1), _whole((D, LANES), 1), _rows(tm, D, last),
                  _whole((1, D)), _rows(tm, D, last),
                  _rows(tm, D, last), _whole((1, D)), _whole((4, D, D), 1), _whole((D, D), 1),
                  _whole((3, D)), _rows(tm, 4 * D, last), _rows(tm, D, last)],
        out_specs=[_rows(tm, 4 * D, last), _rows(tm, D, last), _rows(tm, D, last), _whole((1, D)), _whole((3, D)),
                   _whole((1, D))],
        out_shape=[jax.ShapeDtypeStruct((S, 4 * D), BF16), jax.ShapeDtypeStruct((S, D), F32),
                   jax.ShapeDtypeStruct((S, D), BF16), jax.ShapeDtypeStruct((1, D), F32),
                   jax.ShapeDtypeStruct((3, D), F32), jax.ShapeDtypeStruct((1, D), F32)],
        scratch_shapes=[pltpu.VMEM((sub, D), F32)],
        compiler_params=_params(1),
    )(dproj2, wa, wf, x1, g2, dx2, x, g1, w_in, w_out, conv_w, proj, yc)


def _attn_front(x1, g2, w, wf, bf, gq, gk):
    S, D = x1.shape
    H = D // HEAD_DIM
    tm = min(ROW_TILE, S)
    tri = (lax.broadcasted_iota(jnp.int32, (tm, tm), 1) <= lax.broadcasted_iota(jnp.int32, (tm, tm), 0)).astype(BF16)

    def body(x_ref, g_ref, w_ref, wf_ref, bf_ref, gq_ref, gk_ref, tri_ref, first_ref, second_ref,
             h_ref, qh_ref, kh_ref, z_ref, f_ref, c_ref, rel_ref, qa_ref, ka_ref, va_ref, vt_ref,
             carry, v_s, qraw_ref, kraw_ref):
        @pl.when(pl.program_id(0) == 0)
        def _():
            carry[...] = jnp.zeros((8, LANES), F32)

        xv = x_ref[...]
        inv = lax.rsqrt(jnp.mean(xv * xv, axis=-1, keepdims=True) + RMS_EPS)
        h = (xv * inv * g_ref[...]).astype(BF16)
        h_ref[...] = h
        qraw_ref[...] = _dot(h, w_ref[:, 0:D])
        kraw_ref[...] = _dot(h, w_ref[:, D:2 * D])
        v_s[...] = _dot(h, w_ref[:, 2 * D:3 * D])
        z_ref[...] = _dot(h, w_ref[:, 3 * D:4 * D])
        lane = _lane()
        f = _dot(h, wf_ref[...]) + bf_ref[...]
        f_ref[...] = f
        logf = jnp.where(lane < H, jnp.minimum(f, 0.0) - jnp.log(1.0 + jnp.exp(-jnp.abs(f))), 0.0)
        cs = _dot01(tri_ref[...], logf) + carry[0:1, :]
        c_ref[...] = cs
        carry[...] = jnp.broadcast_to(cs[tm - 1:tm, :], (8, LANES))
        diags = jnp.zeros((tm, LANES), F32)
        for hd in range(H):
            sl = slice(LANES * hd, LANES * (hd + 1))
            a = _aug(hd % 2)
            if hd % 2 == 0:
                qh_ref[hd // 2] = qraw_ref[:, LANES * (hd // 2):LANES * (hd // 2 + 1)]
                kh_ref[hd // 2] = kraw_ref[:, LANES * (hd // 2):LANES * (hd // 2 + 1)]
            qt = _head_tile(qraw_ref, hd, lane)
            qn = qt * lax.rsqrt(jnp.sum(qt * qt, axis=1, keepdims=True) * (1.0 / HEAD_DIM) + RMS_EPS) * gq_ref[...]
            kt = _head_tile(kraw_ref, hd, lane)
            kn = kt * lax.rsqrt(jnp.sum(kt * kt, axis=1, keepdims=True) * (1.0 / HEAD_DIM) + RMS_EPS) * gk_ref[...]
            diags = diags + jnp.where(lane == hd, jnp.sum(qn * kn, axis=1, keepdims=True) * Q_SCALE, 0.0)
            qa_ref[:, sl] = (qn * Q_SCALE).astype(BF16)
            ka_ref[:, sl] = kn.astype(BF16)
            va = jnp.where((lane >= a) & (lane < a + 3), 1.0, _head_tile(v_s, hd, lane))
            va_ref[:, sl] = va.astype(BF16)
            vt_ref[hd] = va.T.astype(BF16)
        rel = cs - diags
        rel_ref[...] = rel
        qa_ref[...] += _dot(_pack3(rel, lane, 1.0), first_ref[...]).astype(BF16)
        ka_ref[...] += _dot(_pack3(-cs, lane, 1.0), second_ref[...]).astype(BF16)

    nb = S // tm
    heads = pl.BlockSpec((H // 2, tm, LANES), lambda i: (0, i, 0))
    return pl.pallas_call(
        body, name="attn_front", grid=(nb,),
        in_specs=[_rows(tm, D), _whole((1, D)), _whole((D, 4 * D)), _whole((D, LANES)), _whole((1, LANES)),
                  _whole((1, LANES)), _whole((1, LANES)), _whole((tm, tm)), _whole((LANES, H * LANES)),
                  _whole((LANES, H * LANES))],
        out_specs=[_rows(tm, D), heads, heads, _rows(tm, D), _rows(tm, LANES), _rows(tm, LANES), _rows(tm, LANES),
                   _rows(tm, H * LANES), _rows(tm, H * LANES), _rows(tm, H * LANES),
                   pl.BlockSpec((H, None, LANES, tm), lambda i: (0, i, 0, 0))],
        out_shape=[jax.ShapeDtypeStruct((S, D), BF16), jax.ShapeDtypeStruct((H // 2, S, LANES), F32),
                   jax.ShapeDtypeStruct((H // 2, S, LANES), F32), jax.ShapeDtypeStruct((S, D), F32),
                   jax.ShapeDtypeStruct((S, LANES), F32), jax.ShapeDtypeStruct((S, LANES), F32),
                   jax.ShapeDtypeStruct((S, LANES), F32),
                   jax.ShapeDtypeStruct((S, H * LANES), BF16), jax.ShapeDtypeStruct((S, H * LANES), BF16),
                   jax.ShapeDtypeStruct((S, H * LANES), BF16), jax.ShapeDtypeStruct((H, nb, LANES, tm), BF16)],
        scratch_shapes=[pltpu.VMEM((8, LANES), F32), pltpu.VMEM((tm, D), F32), pltpu.VMEM((tm, D), F32),
                        pltpu.VMEM((tm, D), F32)],
        compiler_params=_params(1),
    )(x1, g2, w, wf, bf, gq, gk, tri, *_scatter_matrices(H))


def _skip_tables(c, diag, gq, gk, T, G):
    nb = c.shape[0] // T
    bound = 8.0 * jnp.max(jnp.abs(gq)) * jnp.max(jnp.abs(gk))
    first, last = c[0::T, :], c[T - 1::T, :]
    lowest = jnp.maximum(jnp.min(diag.reshape(nb, T, -1), axis=1), -bound)
    idx = jnp.arange(nb)
    margin = (SKIP_LOG + bound) - lowest
    need = (last[None, :, :] <= first[:, None, :] + margin[:, None, :]) & (idx[None, :, None] < idx[:, None, None])
    need = need | (idx[None, :, None] == idx[:, None, None])
    kstart = jnp.argmax(need, axis=1)
    qend = nb - 1 - jnp.argmax(need[::-1], axis=0)
    kstart = jnp.min(kstart.reshape(2 * nb // G, G // 2, -1), axis=1)
    kstart = kstart - (kstart & 1)
    qend = jnp.max(qend.reshape(2 * nb // G, G // 2, -1), axis=1)
    return kstart.T.astype(jnp.int32), qend.T.astype(jnp.int32), bound


def _attn_fwd(kstart, qa, ka, vt, online_max):
    S = qa.shape[0]
    H = qa.shape[1] // LANES
    nb, T = vt.shape[1], vt.shape[3]
    G = 2 * nb // kstart.shape[1]
    W = G * T

    def finish(acc, shift, o_ref, lse_ref):
        a = _aug(pl.program_id(0) % 2)
        feat = lax.broadcasted_iota(jnp.int32, (LANES, 1), 0)
        l = jnp.sum(jnp.where(feat == a, acc, 0.0), axis=0, keepdims=True)
        o_ref[...] = (acc * (1.0 / l)).T
        lse_ref[...] = shift + jnp.log(l)

    def causal(st):
        return jnp.where(lax.broadcasted_iota(jnp.int32, st.shape, 0) <= lax.broadcasted_iota(jnp.int32, st.shape, 1),
                         st, NEG)

    def fast_body(ks_ref, q_ref, k_ref, vt_ref, o_ref, lse_ref, acc_ref, sa_ref, sb_ref, sc_ref):
        h, g = pl.program_id(0), pl.program_id(1)
        q = q_ref[...]
        acc_ref[...] = jnp.zeros((LANES, W), F32)

        def scores(ki, lo):
            return _dot_nt(k_ref[pl.ds(pl.multiple_of(ki * T, T), 2 * T), :], q[lo * T:, :])

        def weighted(ki, p):
            return _dot(vt_ref[ki], p[:T]) + _dot(vt_ref[ki + 1], p[T:])

        first = ks_ref[h, 2 * g + 1]
        early = jnp.minimum(ks_ref[h, 2 * g], first)

        def narrow(i, carry):
            ki = early + 2 * i
            st = _dot_nt(k_ref[pl.ds(pl.multiple_of(ki * T, T), 2 * T), :], q[:W // 2, :])
            acc_ref[:, :W // 2] += weighted(ki, jnp.exp(st).astype(BF16))
            return carry

        lax.fori_loop(0, (first - early) // 2, narrow, 0)
        steps = (g * G - first) // 2
        sa_ref[...] = scores(first, 0)

        def advance(ki, cur_ref, next_ref):
            p = jnp.exp(cur_ref[...]).astype(BF16)
            next_ref[...] = scores(ki + 2, 0)
            acc_ref[...] += weighted(ki, p)

        def loop(i, carry):
            advance(first + 4 * i, sa_ref, sb_ref)
            advance(first + 4 * i + 2, sb_ref, sa_ref)
            return carry

        lax.fori_loop(0, steps // 2, loop, 0)

        def first_own(pending_ref):
            p = jnp.exp(causal(pending_ref[...])).astype(BF16)
            if G > 2:
                sc_ref[:, :W - 2 * T] = scores(g * G + 2, 2)
            acc_ref[...] += weighted(g * G, p)

        @pl.when(steps % 2 == 1)
        def _():
            advance(g * G - 2, sa_ref, sb_ref)
            first_own(sb_ref)

        @pl.when(steps % 2 == 0)
        def _():
            first_own(sa_ref)

        if G > 2:
            acc_ref[:, 2 * T:] += weighted(g * G + 2, jnp.exp(causal(sc_ref[:, :W - 2 * T])).astype(BF16))
        for j in range(4, G, 2):
            p = jnp.exp(causal(scores(g * G + j, j))).astype(BF16)
            acc_ref[:, j * T:] += weighted(g * G + j, p)
        finish(acc_ref[...], 0.0, o_ref, lse_ref)

    def online_body(ks_ref, q_ref, k_ref, vt_ref, o_ref, lse_ref, acc_ref, m_ref):
        h, g = pl.program_id(0), pl.program_id(1)
        q = q_ref[...]
        m_ref[...] = jnp.full((8, W), NEG, F32)
        acc_ref[...] = jnp.zeros((LANES, W), F32)

        def update(st, vtb, lo):
            m_old = m_ref[0:1, lo:]
            m_new = jnp.maximum(m_old, jnp.max(st, axis=0, keepdims=True))
            p = jnp.exp(st - m_new).astype(BF16)
            acc_ref[:, lo:] = acc_ref[:, lo:] * jnp.exp(m_old - m_new) + _dot(vtb, p)
            m_ref[:, lo:] = jnp.broadcast_to(m_new, (8, W - lo))

        def loop(ki, carry):
            kb = k_ref[pl.ds(pl.multiple_of(ki * T, T), T), :]
            update(_dot_nt(kb, q), vt_ref[ki], 0)
            return carry

        lax.fori_loop(jnp.minimum(ks_ref[h, 2 * g], ks_ref[h, 2 * g + 1]), g * G, loop, 0)
        for j in range(G):
            ki = g * G + j
            kb = k_ref[pl.ds(pl.multiple_of(ki * T, T), T), :]
            update(causal(_dot_nt(kb, q[j * T:, :])), vt_ref[ki], j * T)
        finish(acc_ref[...], m_ref[0:1, :], o_ref, lse_ref)

    return pl.pallas_call(
        online_body if online_max else fast_body, name="attn_fwd_online" if online_max else "attn_fwd",
        grid_spec=pltpu.PrefetchScalarGridSpec(
            num_scalar_prefetch=1, grid=(H, nb // G),
            in_specs=[pl.BlockSpec((W, LANES), lambda h, i, ks: (i, h)),
                      pl.BlockSpec((S, LANES), lambda h, i, ks: (0, h)),
                      pl.BlockSpec((None, nb, LANES, T), lambda h, i, ks: (h, 0, 0, 0))],
            out_specs=[pl.BlockSpec((W, LANES), lambda h, i, ks: (i, h)),
                       pl.BlockSpec((None, 1, W), lambda h, i, ks: (h, 0, i))],
            scratch_shapes=[pltpu.VMEM((LANES, W), F32)] + (
                [pltpu.VMEM((8, W), F32)] if online_max else [pltpu.VMEM((2 * T, W), F32)] * 3)),
        out_shape=[jax.ShapeDtypeStruct((S, H * LANES), F32), jax.ShapeDtypeStruct((H, 1, S), F32)],
        compiler_params=_params(2),
    )(kstart, qa, ka, vt)


def _attn_out(o_aug, lse, rel, z, x1, target, w_out, qa):
    S, D = x1.shape
    H = D // HEAD_DIM
    tm = min(ROW_TILE, S)

    def body(o_ref, z_ref, x1_ref, t_ref, w_ref, q_ref, first_ref, rel_ref, lse_ref,
             dx2_ref, dx2b_ref, o2b_ref, dz_ref, doa_ref, qa2_ref, loss_ref, oc_s, do_s):
        @pl.when(pl.program_id(0) == 0)
        def _():
            loss_ref[...] = jnp.zeros((1, LANES), F32)

        lane = _lane()
        for j in range(H // 2):
            oc_s[:, LANES * j:LANES * (j + 1)] = _pair_tile(o_ref[:, 2 * LANES * j:2 * LANES * j + LANES],
                                                            o_ref[:, 2 * LANES * j + LANES:2 * LANES * (j + 1)], lane)
        oc = oc_s[...]
        zv = z_ref[...]
        sg = _sigmoid(zv)
        sz = zv * sg
        o2 = (oc * sz).astype(BF16)
        o2b_ref[...] = o2
        e = x1_ref[...] + _dot(o2, w_ref[...]) - t_ref[...]
        sq = jnp.sum(jnp.sum(e * e, axis=1, keepdims=True), axis=0, keepdims=True)
        loss_ref[...] += jnp.broadcast_to(sq * (0.5 / D), (1, LANES))
        dx2 = e * (1.0 / D)
        dx2_ref[...] = dx2
        dx2b = dx2.astype(BF16)
        dx2b_ref[...] = dx2b
        do2 = _dot_nt(dx2b, w_ref[...])
        dz_ref[...] = (do2 * oc * (sg * (1.0 + zv * (1.0 - sg)))).astype(BF16)
        do_s[...] = do2 * sz
        deltas = jnp.zeros((tm, LANES), F32)
        for hd in range(H):
            dt = _head_tile(do_s, hd, lane)
            delta = jnp.sum(dt * _head_tile(oc_s, hd, lane), axis=1, keepdims=True)
            deltas = deltas + jnp.where(lane == hd, delta, 0.0)
            doa_ref[:, LANES * hd:LANES * (hd + 1)] = dt.astype(BF16)
        doa_ref[...] += _dot(_pack3(-deltas, lane, 0.0), first_ref[...]).astype(BF16)
        lse = jnp.concatenate([lse_ref[...], jnp.zeros((LANES - H, tm), F32)], axis=0).T
        rq = rel_ref[...] - lse
        tile_lane = lax.broadcasted_iota(jnp.int32, (1, H * LANES), 1)
        extra = tile_lane % LANES - _aug((tile_lane // LANES) % 2)
        kept = jnp.where((extra >= 0) & (extra < 3), jnp.zeros((), BF16), q_ref[...])
        qa2_ref[...] = kept + _dot(_pack3(rq, lane, 0.0), first_ref[...]).astype(BF16)

    return pl.pallas_call(
        body, name="attn_out", grid=(S // tm,),
        in_specs=[_rows(tm, H * LANES), _rows(tm, D), _rows(tm, D), _rows(tm, D), _whole((D, D)),
                  _rows(tm, H * LANES), _whole((LANES, H * LANES)), _rows(tm, LANES),
                  pl.BlockSpec((H, tm), lambda i: (0, i))],
        out_specs=[_rows(tm, D), _rows(tm, D), _rows(tm, D), _rows(tm, D), _rows(tm, H * LANES),
                   _rows(tm, H * LANES), _whole((1, LANES))],
        out_shape=[jax.ShapeDtypeStruct((S, D), F32), jax.ShapeDtypeStruct((S, D), BF16),
                   jax.ShapeDtypeStruct((S, D), BF16), jax.ShapeDtypeStruct((S, D), BF16),
                   jax.ShapeDtypeStruct((S, H * LANES), BF16), jax.ShapeDtypeStruct((S, H * LANES), BF16),
                   jax.ShapeDtypeStruct((1, LANES), F32)],
        scratch_shapes=[pltpu.VMEM((tm, D), F32), pltpu.VMEM((tm, D), F32)],
        compiler_params=_params(1),
    )(o_aug, z, x1, target, w_out, qa, _scatter_matrices(H)[0], rel, lse)


def _attn_bwd(qend, qa2, doa, ka, va, T):
    S = qa2.shape[0]
    H = qa2.shape[1] // LANES
    nb = S // T
    G = 2 * nb // qend.shape[1]
    W = G * T

    def body(qe_ref, q_ref, do_ref, k_ref, v_ref, dq_ref, dk_ref, dv_ref, dkt_acc, dvt_acc):
        h, g = pl.program_id(0), pl.program_id(1)

        @pl.when(g == 0)
        def _():
            dq_ref[...] = jnp.zeros((S, LANES), F32)

        kb = k_ref[...]
        vb = v_ref[...]
        dkt_acc[...] = jnp.zeros((LANES, W), F32)
        dvt_acc[...] = jnp.zeros((LANES, W), F32)

        def step(qi, c0, c1, masked):
            rows = pl.ds(pl.multiple_of(qi * T, T), 2 * T)
            qb = q_ref[rows, :]
            dob = do_ref[rows, :]
            s = _dot_nt(qb, kb[c0:c1])
            if masked:
                query = lax.broadcasted_iota(jnp.int32, s.shape, 0) + (c1 - 2 * T)
                s = jnp.where(lax.broadcasted_iota(jnp.int32, s.shape, 1) <= query, s, NEG)
            p = jnp.exp(s)
            ds = (p * _dot_nt(dob, vb[c0:c1])).astype(BF16)
            dvt_acc[:, c0:c1] += _dot(dob.astype(F32).T.astype(BF16), p.astype(BF16))
            dkt_acc[:, c0:c1] += _dot(qb.astype(F32).T.astype(BF16), ds)
            dq_ref[rows, :] += _dot(ds, kb[c0:c1])

        for m in range(G // 2):
            step(g * G + 2 * m, 0, (m + 1) * 2 * T, True)
        first = g * G + G
        n_all = jnp.maximum((qe_ref[h, 2 * g] - first + 2) // 2, 0)
        second = first + 2 * n_all

        def all_keys(i, carry):
            step(first + 2 * i, 0, W, False)
            return carry

        def late_keys(i, carry):
            step(second + 2 * i, W // 2, W, False)
            return carry

        lax.fori_loop(0, n_all, all_keys, 0)
        lax.fori_loop(0, (qe_ref[h, 2 * g + 1] - second + 2) // 2, late_keys, 0)
        dk_ref[...] = dkt_acc[...].T
        dv_ref[...] = dvt_acc[...].T.astype(BF16)

    heads = pl.BlockSpec((None, W, LANES), lambda h, i, qe: (h, i, 0))
    return pl.pallas_call(
        body, name="attn_bwd",
        grid_spec=pltpu.PrefetchScalarGridSpec(
            num_scalar_prefetch=1, grid=(H, nb // G),
            in_specs=[pl.BlockSpec((S, LANES), lambda h, i, qe: (0, h)), pl.BlockSpec((S, LANES), lambda h, i, qe: (0, h)),
                      pl.BlockSpec((W, LANES), lambda h, i, qe: (i, h)), pl.BlockSpec((W, LANES), lambda h, i, qe: (i, h))],
            out_specs=[pl.BlockSpec((None, S, LANES), lambda h, i, qe: (h, 0, 0)), heads, heads],
            scratch_shapes=[pltpu.VMEM((LANES, W), F32), pltpu.VMEM((LANES, W), F32)]),
        out_shape=[jax.ShapeDtypeStruct((H, S, LANES), F32), jax.ShapeDtypeStruct((H, S, LANES), F32),
                   jax.ShapeDtypeStruct((H, S, LANES), BF16)],
        compiler_params=_params(2),
    )(qend, qa2, doa, ka, va)


def _attn_proj_bwd(dqt, dka, dva, qraw, kraw, dz, f, gq, gk):
    S, D = dz.shape
    H = D // HEAD_DIM
    tm = min(CONV_TILE, S)
    last = S // tm - 1
    tri = (lax.broadcasted_iota(jnp.int32, (tm, tm), 1) >= lax.broadcasted_iota(jnp.int32, (tm, tm), 0)).astype(BF16)

    def body(dq_ref, dk_ref, dv_ref, q_ref, k_ref, dz_ref, f_ref, gq_ref, gk_ref, tri_ref, ones_ref,
             dproj_ref, small_ref, carry, pairs):
        @pl.when(pl.program_id(0) == 0)
        def _():
            small_ref[...] = jnp.zeros((8, LANES), F32)
            carry[...] = jnp.zeros((8, LANES), F32)

        lane = _lane()

        def head_pair(j, acc):
            dcs, dgq, dgk = acc
            dq2, dk2 = [], []
            q_pair, k_pair = q_ref[j], k_ref[j]
            for parity in (0, 1):
                hd = 2 * j + parity
                own, a = _own(lane, parity), _aug(parity)
                dqf = dq_ref[hd]
                dqn = jnp.where(own, dqf * Q_SCALE, 0.0)
                d, dg = _head_rms_bwd(dqn, jnp.where(own, q_pair, 0.0), gq_ref[...], ones_ref[...])
                dq2.append(d)
                dgq = dgq + dg
                dkt = dk_ref[hd]
                dcs = dcs + jnp.where(lane == hd, _col(dqf, lane, a) - _col(dkt, lane, a + 3), 0.0)
                d, dg = _head_rms_bwd(jnp.where(own, dkt, 0.0), jnp.where(own, k_pair, 0.0), gk_ref[...], ones_ref[...])
                dk2.append(d)
                dgk = dgk + dg
            pairs[0, j] = _pair_tile(*dq2, lane).astype(BF16)
            pairs[1, j] = _pair_tile(*dk2, lane).astype(BF16)
            pairs[2, j] = _pair_tile(dv_ref[2 * j], dv_ref[2 * j + 1], lane)
            return dcs, dgq, dgk

        zero = jnp.zeros((1, LANES), F32)
        dcs, dgq, dgk = lax.fori_loop(0, H // 2, head_pair, (jnp.zeros((tm, LANES), F32), zero, zero))
        for part in range(3):
            for j in range(H // 2):
                dproj_ref[:, part * D + LANES * j:part * D + LANES * (j + 1)] = pairs[part, j]
        dproj_ref[:, 3 * D:4 * D] = dz_ref[...]
        dlogf = _dot01(tri_ref[...], dcs) + carry[0:1, :]
        carry[...] = jnp.broadcast_to(dlogf[0:1, :], (8, LANES))
        df = dlogf * (1.0 / (1.0 + jnp.exp(f_ref[...])))
        dproj_ref[:, 4 * D:4 * D + LANES] = df.astype(BF16)
        small_ref[0:1, :] += jnp.sum(df, axis=0, keepdims=True)
        small_ref[1:2, :] += dgq
        small_ref[2:3, :] += dgk

    W = 4 * D + LANES
    heads = pl.BlockSpec((H, tm, LANES), lambda i: (0, last - i, 0))
    head_pairs = pl.BlockSpec((H // 2, tm, LANES), lambda i: (0, last - i, 0))
    return pl.pallas_call(
        body, name="attn_proj_bwd", grid=(S // tm,),
        in_specs=[heads, heads, heads, head_pairs, head_pairs,
                  _rows(tm, D, last), _rows(tm, LANES, last), _whole((1, LANES)), _whole((1, LANES)),
                  _whole((tm, tm)), _whole((LANES, LANES))],
        out_specs=[_rows(tm, W, last), _whole((8, LANES))],
        out_shape=[jax.ShapeDtypeStruct((S, W), BF16), jax.ShapeDtypeStruct((8, LANES), F32)],
        scratch_shapes=[pltpu.VMEM((8, LANES), F32), pltpu.VMEM((3, H // 2, tm, LANES), BF16)],
        compiler_params=_params(1),
    )(dqt, dka, dva, qraw, kraw, dz, f, gq, gk, tri, jnp.ones((LANES, LANES), BF16))


def _matmul_tn(a, b, col0, n, tn, name, stacked=False):
    S, M = a.shape
    ts = min(TN_ROWS, S)
    off = col0 // tn

    def body(a_ref, b_ref, o_ref):
        @pl.when(pl.program_id(1) == 0)
        def _():
            o_ref[...] = jnp.zeros((M, tn), F32)

        o_ref[...] += _dot_tn(a_ref[...], b_ref[...])

    if stacked:
        out_spec, out_shape = pl.BlockSpec((None, M, tn), lambda j, s: (j, 0, 0)), (n // tn, M, tn)
    else:
        out_spec, out_shape = pl.BlockSpec((M, tn), lambda j, s: (0, j)), (M, n)
    return pl.pallas_call(
        body, name=name, grid=(n // tn, S // ts),
        in_specs=[pl.BlockSpec((ts, M), lambda j, s: (s, 0)), pl.BlockSpec((ts, tn), lambda j, s: (s, off + j))],
        out_specs=out_spec, out_shape=jax.ShapeDtypeStruct(out_shape, F32),
        compiler_params=_params(2),
    )(a, b)


def _adam_update(gv, w_ref, m_ref, v_ref, d_ref, m2_ref, v2_ref):
    m2 = ADAM_B1 * m_ref[...] + (1.0 - ADAM_B1) * gv
    v2 = ADAM_B2 * v_ref[...] + (1.0 - ADAM_B2) * (gv * gv)
    m2_ref[...] = m2
    v2_ref[...] = v2
    m_hat = m2 / (1.0 - ADAM_B1 ** ADAM_STEP)
    v_hat = v2 / (1.0 - ADAM_B2 ** ADAM_STEP)
    d_ref[...] = -ADAM_LR * (m_hat / (jnp.sqrt(v_hat) + ADAM_EPS) + ADAM_WD * w_ref[...])


def _adamw(w, g, m, v, name):
    r, c = w.shape
    tr = ROW_TILE if r % ROW_TILE == 0 else r

    def body(w_ref, g_ref, m_ref, v_ref, d_ref, m2_ref, v2_ref):
        _adam_update(g_ref[...], w_ref, m_ref, v_ref, d_ref, m2_ref, v2_ref)

    spec = _rows(tr, c)
    return pl.pallas_call(
        body, name=name, grid=(r // tr,), in_specs=[spec] * 4, out_specs=[spec] * 3,
        out_shape=[jax.ShapeDtypeStruct((r, c), F32)] * 3, compiler_params=_params(1),
    )(w, g, m, v)


def _adamw_halves(w, mine, other, m, v, core, name):
    r, c = mine.shape
    tr = ROW_TILE if r % ROW_TILE == 0 else r
    per = r // tr

    def body(core_ref, w_ref, mine_ref, other_ref, m_ref, v_ref, g_ref, d_ref, m2_ref, v2_ref):
        gv = jnp.where(pl.program_id(0) // per == core_ref[0], mine_ref[...], other_ref[...])
        g_ref[...] = gv
        _adam_update(gv, w_ref, m_ref, v_ref, d_ref, m2_ref, v2_ref)

    full = pl.BlockSpec((tr, c), lambda i, core: (i, 0))
    half = pl.BlockSpec((tr, c), lambda i, core: (i % per, 0))
    return pl.pallas_call(
        body, name=name,
        grid_spec=pltpu.PrefetchScalarGridSpec(num_scalar_prefetch=1, grid=(2 * per,),
                                               in_specs=[full, half, half, full, full], out_specs=[full] * 4),
        out_shape=[jax.ShapeDtypeStruct((2 * r, c), F32)] * 4, compiler_params=_params(1),
    )(core, w, mine, other, m, v)


def _local_step(x, target, g1, w_in, conv_w, w_out, g2, wa_in, b_f, gq, gk, wa_out):
    S, D = x.shape
    H = D // HEAD_DIM
    ws = wa_in.shape[2]
    w_qkvz = jnp.concatenate([wa_in[0], wa_in[1], wa_in[2], wa_in[3][:, :4 * D - 3 * ws]], axis=1)
    wf = jnp.pad(wa_in[3][:, 4 * D - 3 * ws:], ((0, 0), (0, LANES - H)))
    bf = jnp.pad(b_f, ((0, 0), (0, LANES - H)))
    gq128 = jnp.concatenate([gq, gq], axis=1)
    gk128 = jnp.concatenate([gk, gk], axis=1)

    proj, h1, yc, y, x1 = _conv_fwd(x, g1, w_in, conv_w, w_out)
    h2, qraw, kraw, z, f, c, rel, qa, ka, va, vt = _attn_front(x1, g2, w_qkvz, wf, bf, gq128, gk128)
    T = vt.shape[3]
    kstart, qend, bound = _skip_tables(c[:, :H], (c - rel)[:, :H], gq, gk, T, min(ATT_GROUP, S // T))
    o_aug, lse = lax.cond(2.0 * bound <= PLAIN_EXP_MAX, functools.partial(_attn_fwd, online_max=False),
                          functools.partial(_attn_fwd, online_max=True), kstart, qa, ka, vt)
    dx2, dx2b, o2b, dz, doa, qa2, loss = _attn_out(o_aug, lse.reshape(H, S), rel, z, x1, target, wa_out, qa)
    dqt, dka, dva = _attn_bwd(qend, qa2, doa, ka, va, T)
    dproj2, small = _attn_proj_bwd(dqt, dka, dva, qraw, kraw, dz, f, gq128, gk128)
    dproj1, dx, dx1b, dg1, dcw, dg2 = _conv_bwd(dproj2, w_qkvz, wf, x1, g2, dx2, x, g1, w_in, w_out, conv_w, proj, yc)

    tn = min(1024, D)
    dwa_out = _matmul_tn(o2b, dx2b, 0, D, tn, "dw_attn_out")
    dw_main = _matmul_tn(h2, dproj2, 0, 4 * D, 2 * tn, "dw_attn_in")
    dw_f = _matmul_tn(h2, dproj2, 4 * D, LANES, LANES, "dw_attn_f")
    dwa_in = jnp.stack([dw_main[:, 0:ws], dw_main[:, ws:2 * ws], dw_main[:, 2 * ws:3 * ws],
                        jnp.concatenate([dw_main[:, 3 * ws:], dw_f[:, :H]], axis=1)])
    dw_out = _matmul_tn(y, dx1b, 0, D, tn, "dw_conv_out")
    dw_in = _matmul_tn(h1, dproj1, 0, 4 * D, D, "dw_conv_in", stacked=True)
    grads = dict(conv_norm_g=dg1, conv_w_in=dw_in, conv_w=dcw, conv_w_out=dw_out, attn_norm_g=dg2,
                 attn_w_in=dwa_in, attn_b_f=small[0:1, :H],
                 attn_q_norm_g=small[1:2, :HEAD_DIM] + small[1:2, HEAD_DIM:],
                 attn_k_norm_g=small[2:3, :HEAD_DIM] + small[2:3, HEAD_DIM:], attn_w_out=dwa_out)
    return loss[0, 0], dx, grads


def _coords():
    return lax.axis_index("x"), lax.axis_index("y"), lax.axis_index("c")


def _at(ref, idx):
    return ref.at[idx] if idx else ref


def _other_chips(x, y):
    return [(1 - x, y), (x, 1 - y), (1 - x, 1 - y)]


def _all_gather(halved, whole):
    nh, nw = len(halved), len(whole)

    def body(*refs):
        src_h, src_w = refs[:nh], refs[nh:nh + nw]
        out_h, out_w = refs[nh + nw:2 * nh + nw], refs[2 * nh + nw:2 * (nh + nw)]
        send_h, recv_h, send_w, recv_w = refs[2 * (nh + nw):]
        x, y, c = _coords()
        mine = 2 * x + y
        sibling = (x, y, 1 - c)
        chips = _other_chips(x, y)

        def copy_h(a, k, chip, half, to, src=None):
            dst = out_h[a].at[chip, half]
            return pltpu.make_async_remote_copy(src_ref=dst if src is None else src, dst_ref=dst,
                                                send_sem=send_h.at[a, k], recv_sem=recv_h.at[a, k],
                                                device_id=to, device_id_type=MESH)

        def copy_w(a, k, chip, to):
            return pltpu.make_async_remote_copy(src_ref=src_w[a], dst_ref=out_w[a].at[chip],
                                                send_sem=send_w.at[a, k], recv_sem=recv_w.at[a, k],
                                                device_id=to, device_id_type=MESH)

        first = [copy_h(a, k, mine, c, (*chip, c), src=src_h[a].at[c]) for a in range(nh) for k, chip in enumerate(chips)]
        first += [copy_w(a, k, mine, (*chip, c)) for a in range(nw) for k, chip in enumerate(chips)]
        own = [pltpu.make_async_remote_copy(src_ref=src_h[a], dst_ref=out_h[a].at[mine], send_sem=send_h.at[a, 6],
                                            recv_sem=recv_h.at[a, 6], device_id=sibling, device_id_type=MESH)
               for a in range(nh)]
        own += [copy_w(a, 3, mine, sibling) for a in range(nw)]
        for cp in first + own:
            cp.start()
        passed = []
        for a in range(nh):
            for k, (px, py) in enumerate(chips):
                copy_h(a, k, 2 * px + py, c, (x, y, c)).wait_recv()
                cp = copy_h(a, 3 + k, 2 * px + py, c, sibling)
                cp.start()
                passed.append(cp)
        for a in range(nh):
            for k, (px, py) in enumerate(chips):
                copy_h(a, 3 + k, 2 * px + py, 1 - c, (x, y, c)).wait_recv()
        for a in range(nw):
            for k, (px, py) in enumerate(chips):
                copy_w(a, k, 2 * px + py, (x, y, c)).wait_recv()
        for cp in own:
            cp.wait_recv()
        for cp in first + passed + own:
            cp.wait_send()

    out_shape = [jax.ShapeDtypeStruct((4,) + a.shape, a.dtype) for a in list(halved) + list(whole)]
    return pl.pallas_call(
        body, name="gather_weights", in_specs=[ANY] * (nh + nw), out_specs=[ANY] * (nh + nw), out_shape=out_shape,
        scratch_shapes=[pltpu.SemaphoreType.DMA((nh, 7)), pltpu.SemaphoreType.DMA((nh, 7)),
                        pltpu.SemaphoreType.DMA((nw, 4)), pltpu.SemaphoreType.DMA((nw, 4))],
    )(*halved, *whole)


def _exchange(name, srcs, lands, copies, local_copies):
    ns, nl, n, nloc = len(srcs), len(lands), len(copies), len(local_copies)

    def body(*refs):
        src, land = refs[:ns], refs[ns:ns + nl]
        send, recv, local_sem = refs[ns + nl:]
        me = _coords()
        started = []
        for k, (si, s_at, li, l_at, ci) in enumerate(local_copies):
            cp = pltpu.make_async_copy(_at(src[si], s_at(*me)), _at(land[li], l_at(*me)), local_sem.at[k])
            cp.start()
            started.append(cp)
        remote = []
        for k, (si, s_at, li, l_at, peer) in enumerate(copies):
            cp = pltpu.make_async_remote_copy(src_ref=_at(src[si], s_at(*me)), dst_ref=_at(land[li], l_at(*me)),
                                              send_sem=send.at[k], recv_sem=recv.at[k],
                                              device_id=peer(*me), device_id_type=MESH)
            cp.start()
            remote.append(cp)
        for cp in remote:
            cp.wait()
        for cp in started:
            cp.wait()

    return pl.pallas_call(
        body, name=name, in_specs=[ANY] * ns, out_specs=[ANY] * nl, out_shape=list(lands),
        scratch_shapes=[pltpu.SemaphoreType.DMA((n,)), pltpu.SemaphoreType.DMA((n,)),
                        pltpu.SemaphoreType.DMA((max(nloc, 1),))],
    )(*srcs)


def _add_pairs(a, b, core, name):
    _, r, cols = b.shape
    tr = ROW_TILE if r % ROW_TILE == 0 else r

    def body(core_ref, a_ref, b_ref, o_ref, ob_ref):
        s = a_ref[...] + b_ref[...]
        o_ref[...] = s
        ob_ref[...] = s.astype(BF16)

    spec = pl.BlockSpec((None, tr, cols), lambda j, i, core: (j, i, 0))
    return pl.pallas_call(
        body, name=name,
        grid_spec=pltpu.PrefetchScalarGridSpec(
            num_scalar_prefetch=1, grid=(4, r // tr),
            in_specs=[pl.BlockSpec((None, None, tr, cols), lambda j, i, core: (j, core[0], i, 0)), spec],
            out_specs=[spec, spec]),
        out_shape=[jax.ShapeDtypeStruct(b.shape, F32), jax.ShapeDtypeStruct(b.shape, BF16)],
        compiler_params=_params(2),
    )(core, a, b)


def _sum_chips(own, landed, name):
    _, r, cols = landed.shape
    tr = ROW_TILE if r % ROW_TILE == 0 else r

    def body(own_ref, land_ref, o_ref):
        acc = own_ref[...]
        for j in range(3):
            acc = acc + land_ref[j].astype(F32)
        o_ref[...] = acc

    return pl.pallas_call(
        body, name=name, grid=(r // tr,),
        in_specs=[_rows(tr, cols), pl.BlockSpec((3, tr, cols), lambda i: (0, i, 0))], out_specs=_rows(tr, cols),
        out_shape=jax.ShapeDtypeStruct((r, cols), F32), compiler_params=_params(1),
    )(own, landed)


def _sum_devices(landed, name):
    def body(l_ref, o_ref):
        acc = l_ref[0]
        for j in range(1, 8):
            acc = acc + l_ref[j]
        o_ref[...] = acc

    return pl.pallas_call(body, name=name, out_shape=jax.ShapeDtypeStruct(landed.shape[1:], F32))(landed)


def _reduce_gradients(big, small):
    nb = len(big)
    x, y, c = _coords()
    mine = 2 * x + y
    flips = [(fx, fy, fc) for fx in (0, 1) for fy in (0, 1) for fc in (0, 1) if fx or fy or fc]

    def flip(fx, fy, fc):
        return lambda x, y, c: (x ^ fx, y ^ fy, c ^ fc)

    copies = [(a, (lambda j: lambda x, y, c: (j, 1 - c))(j), a, (lambda j: lambda x, y, c: (j,))(j), flip(0, 0, 1))
              for a in range(nb) for j in range(4)]
    copies += [(nb, lambda x, y, c: (), nb, lambda x, y, c: (4 * x + 2 * y + c,), flip(*f)) for f in flips]
    lands = [jax.ShapeDtypeStruct((4,) + g.shape[2:], F32) for g in big] + [jax.ShapeDtypeStruct((8,) + small.shape, F32)]
    local = [(nb, lambda x, y, c: (), nb, lambda x, y, c: (4 * x + 2 * y + c,), None)]
    landed = _exchange("swap_halves", list(big) + [small], lands, copies, local)
    small_sum = _sum_devices(landed[nb], "sum_small")

    chip_f32, chip_bf16 = [], []
    for a in range(nb):
        s, sb = _add_pairs(big[a], landed[a], jnp.reshape(c, (1,)).astype(jnp.int32), f"add_cores_{a}")
        chip_f32.append(s)
        chip_bf16.append(sb)

    chip_flips = [(1, 0), (0, 1), (1, 1)]
    copies = [(a, (lambda f: lambda x, y, c: (2 * (x ^ f[0]) + (y ^ f[1]),))(f), a, (lambda k: lambda x, y, c: (k,))(k),
               flip(f[0], f[1], 0)) for a in range(nb) for k, f in enumerate(chip_flips)]
    lands = [jax.ShapeDtypeStruct((3,) + g.shape[1:], BF16) for g in chip_bf16]
    landed = _exchange("send_chip_sums", chip_bf16, lands, copies, [])
    totals = [_sum_chips(lax.dynamic_index_in_dim(chip_f32[a], mine, axis=0, keepdims=False), landed[a],
                         f"sum_chips_{a}") for a in range(nb)]

    copies = [(a, lambda x, y, c: (), a, lambda x, y, c: (), flip(0, 0, 1)) for a in range(nb)]
    lands = [jax.ShapeDtypeStruct(t.shape, F32) for t in totals]
    return list(zip(totals, _exchange("swap_sums", totals, lands, copies, []))), small_sum


def kernel(x, conv_norm_g, conv_w_in, conv_w, conv_w_out, attn_norm_g, attn_w_in, attn_b_f, attn_q_norm_g, attn_k_norm_g, attn_w_out, loss_target, m_conv_norm_g, m_conv_w_in, m_conv_w, m_conv_w_out, m_attn_norm_g, m_attn_w_in, m_attn_b_f, m_attn_q_norm_g, m_attn_k_norm_g, m_attn_w_out, v_conv_norm_g, v_conv_w_in, v_conv_w, v_conv_w_out, v_attn_norm_g, v_attn_w_in, v_attn_b_f, v_attn_q_norm_g, v_attn_k_norm_g, v_attn_w_out):
    xi, yi, _ = _coords()
    chip = 2 * xi + yi
    D = x.shape[2]
    H = D // HEAD_DIM
    names = ["conv_norm_g", "conv_w_in", "conv_w", "conv_w_out", "attn_norm_g", "attn_w_in", "attn_b_f",
             "attn_q_norm_g", "attn_k_norm_g", "attn_w_out"]
    weights = dict(zip(names, [conv_norm_g, conv_w_in, conv_w, conv_w_out, attn_norm_g, attn_w_in, attn_b_f,
                               attn_q_norm_g, attn_k_norm_g, attn_w_out]))
    m_in = dict(zip(names, [m_conv_norm_g, m_conv_w_in, m_conv_w, m_conv_w_out, m_attn_norm_g, m_attn_w_in,
                            m_attn_b_f, m_attn_q_norm_g, m_attn_k_norm_g, m_attn_w_out]))
    v_in = dict(zip(names, [v_conv_norm_g, v_conv_w_in, v_conv_w, v_conv_w_out, v_attn_norm_g, v_attn_w_in,
                            v_attn_b_f, v_attn_q_norm_g, v_attn_k_norm_g, v_attn_w_out]))
    weights = {k: w[0] for k, w in weights.items()}
    m_in = {k: w[0] for k, w in m_in.items()}
    v_in = {k: w[0] for k, w in v_in.items()}

    big_names = ["conv_w_in", "attn_w_in", "conv_w_out", "attn_w_out"]
    halved = [weights[k].astype(BF16).reshape(2, weights[k].shape[0] // 2, weights[k].shape[1]) for k in big_names]
    q = D // 4
    small_w = jnp.concatenate([weights["conv_w"], weights["attn_norm_g"][None, :], jnp.zeros((4, q), F32)], axis=0)
    g_in, ga_in, g_out, ga_out, g_small = _all_gather(halved, [small_w])
    w_in = g_in.reshape(4, D, D)
    wa_in = ga_in.reshape(4, D, D + H // 4)
    w_out = g_out.reshape(D, D)
    wa_out = ga_out.reshape(D, D)
    conv_w_full = g_small[:, 0:3, :].transpose(1, 0, 2).reshape(3, D)
    attn_g_full = g_small[:, 3, :].reshape(1, D)

    loss_part, grad_x, grads = _local_step(x[0], loss_target[0], weights["conv_norm_g"][None, :], w_in, conv_w_full,
                                           w_out, attn_g_full, wa_in, weights["attn_b_f"][None, :],
                                           weights["attn_q_norm_g"][None, :], weights["attn_k_norm_g"][None, :], wa_out)

    big = [grads["conv_w_in"].reshape(4, 2, D // 2, D),
           grads["attn_w_in"].reshape(4, 2, D // 2, D + H // 4),
           grads["conv_w_out"].reshape(4, 2, D // 8, D), grads["attn_w_out"].reshape(4, 2, D // 8, D)]
    tail = jnp.concatenate([grads["attn_b_f"], grads["attn_q_norm_g"], grads["attn_k_norm_g"],
                            jnp.reshape(loss_part, (1, 1)), jnp.zeros((1, D - H - 2 * HEAD_DIM - 1), F32)], axis=1)
    small = jnp.concatenate([grads["conv_norm_g"], grads["conv_w"], grads["attn_norm_g"], tail,
                             jnp.zeros((2, D), F32)], axis=0)
    reduced, small_sum = _reduce_gradients(big, small)
    final = {}
    final["conv_norm_g"] = small_sum[0]
    final["conv_w"] = lax.dynamic_slice_in_dim(small_sum[1:4], chip * q, q, axis=1)
    final["attn_norm_g"] = lax.dynamic_slice_in_dim(small_sum[4], chip * q, q, axis=0)
    final["attn_b_f"] = small_sum[5, :H]
    final["attn_q_norm_g"] = small_sum[5, H:H + HEAD_DIM]
    final["attn_k_norm_g"] = small_sum[5, H + HEAD_DIM:H + 2 * HEAD_DIM]
    loss = small_sum[5, H + 2 * HEAD_DIM]

    delta, new_m, new_v = {}, {}, {}
    core = jnp.reshape(lax.axis_index("c"), (1,)).astype(jnp.int32)
    for k, (mine, other) in zip(big_names, reduced):
        final[k], delta[k], new_m[k], new_v[k] = _adamw_halves(weights[k], mine, other, m_in[k], v_in[k], core,
                                                               "adamw_" + k)
    for k in names:
        if k in big_names:
            continue
        shape = weights[k].shape
        as2d = (lambda a: a.reshape(1, -1)) if len(shape) == 1 else (lambda a: a)
        d, m2, v2 = _adamw(as2d(weights[k]), as2d(final[k]), as2d(m_in[k]), as2d(v_in[k]), "adamw_" + k)
        delta[k], new_m[k], new_v[k] = d.reshape(shape), m2.reshape(shape), v2.reshape(shape)
    lead = lambda a: a[None]
    return (loss, grad_x[None], *[lead(final[k]) for k in names], *[lead(delta[k]) for k in names],
            *[lead(new_m[k]) for k in names], *[lead(new_v[k]) for k in names])
```

```python
import functools

import jax
import jax.numpy as jnp
from jax import lax
from jax.experimental import pallas as pl
from jax.experimental.pallas import tpu as pltpu

F32 = jnp.float32
BF16 = jnp.bfloat16
HEAD_DIM = 64
LANES = 128
RMS_EPS = 1e-6
NEG = -1e30
Q_SCALE = 0.125
ROW_TILE = 256
CONV_TILE = 512
ATT_GROUP = 4
SKIP_LOG = 106.0
PLAIN_EXP_MAX = 60.0
TN_ROWS = 2048
VMEM_LIMIT = 56 << 20
ADAM_LR, ADAM_B1, ADAM_B2, ADAM_EPS, ADAM_WD, ADAM_STEP = 0.001, 0.9, 0.999, 1e-08, 0.01, 10
MESH = pl.DeviceIdType.MESH
ANY = pl.BlockSpec(memory_space=pl.ANY)


def _lane():
    return lax.broadcasted_iota(jnp.int32, (1, LANES), 1)


def _split3(x):
    hi = x.astype(BF16).astype(F32)
    r = x - hi
    mid = r.astype(BF16).astype(F32)
    lo = (r - mid).astype(BF16).astype(F32)
    return hi, mid, lo


STAT_STRIDE = 16
ONE_LANE = 3 * STAT_STRIDE


def _pack3(x, lane, one):
    hi, mid, lo = _split3(x)
    packed = hi + pltpu.roll(mid, STAT_STRIDE, 1) + pltpu.roll(lo, 2 * STAT_STRIDE, 1)
    return jnp.where(lane == ONE_LANE, one, packed).astype(BF16)


def _scatter_matrices(H):
    rows = lax.broadcasted_iota(jnp.int32, (LANES, H * LANES), 0)
    cols = lax.broadcasted_iota(jnp.int32, (LANES, H * LANES), 1)
    head, within = cols // LANES, cols % LANES
    extra = within - _aug(head % 2)
    term = (rows < ONE_LANE) & (rows % STAT_STRIDE == head)
    first = ((term & (extra == rows // STAT_STRIDE)) | ((rows == ONE_LANE) & (extra >= 3) & (extra < 6)))
    second = ((term & (extra - 3 == rows // STAT_STRIDE)) | ((rows == ONE_LANE) & (extra >= 0) & (extra < 3)))
    return first.astype(BF16), second.astype(BF16)


def _put(base, lane, start, parts):
    for j, p in enumerate(parts):
        base = jnp.where(lane == start + j, p, base)
    return base


def _col(x, lane, idx):
    return jnp.sum(jnp.where(lane == idx, x, 0.0), axis=1, keepdims=True)


def _feat(parity):
    return HEAD_DIM * parity


def _aug(parity):
    return HEAD_DIM * (1 - parity)


def _own(lane, parity):
    return (lane >= _feat(parity)) & (lane < _feat(parity) + HEAD_DIM)


def _head_tile(ref, hd, lane):
    j = hd // 2
    return jnp.where(_own(lane, hd % 2), ref[:, LANES * j:LANES * (j + 1)], 0.0)


def _pair_tile(even, odd, lane):
    return jnp.where(lane < HEAD_DIM, even, odd)


def _sigmoid(x):
    return 0.5 * jnp.tanh(0.5 * x) + 0.5


def _dot(a, b):
    return jnp.dot(a, b, preferred_element_type=F32)


def _dot_nt(a, b):
    return lax.dot_general(a, b, (((1,), (1,)), ((), ())), preferred_element_type=F32)


def _dot_tn(a, b):
    return lax.dot_general(a, b, (((0,), (0,)), ((), ())), preferred_element_type=F32)


def _dot01(tri, x):
    hi, mid, lo = _split3(x)
    return _dot(tri, hi.astype(BF16)) + _dot(tri, mid.astype(BF16)) + _dot(tri, lo.astype(BF16))


def _rms_bwd(dh, x, g):
    inv = lax.rsqrt(jnp.mean(x * x, axis=-1, keepdims=True) + RMS_EPS)
    xh = x * inv
    dxn = dh * g
    dx = inv * (dxn - xh * jnp.mean(dxn * xh, axis=-1, keepdims=True))
    return dx, jnp.sum(dh * xh, axis=0, keepdims=True)


def _head_rms_bwd(dn, t, g, ones):
    sq = t * t
    hi = sq.astype(BF16)
    lo = (sq - hi.astype(F32)).astype(BF16)
    inv = lax.rsqrt((_dot(hi, ones) + _dot(lo, ones)) * (1.0 / HEAD_DIM) + RMS_EPS)
    th = t * inv
    gd = dn * g
    d = inv * (gd - th * (jnp.sum(gd * th, axis=1, keepdims=True) * (1.0 / HEAD_DIM)))
    return d, jnp.sum(dn * th, axis=0, keepdims=True)


def _params(n_grid):
    return pltpu.CompilerParams(dimension_semantics=("arbitrary",) * n_grid, vmem_limit_bytes=VMEM_LIMIT)


def _rows(tm, cols, rev=None):
    if rev is None:
        return pl.BlockSpec((tm, cols), lambda i: (i, 0))
    return pl.BlockSpec((tm, cols), lambda i: (rev - i, 0))


def _whole(shape, buffers=None):
    mode = {} if buffers is None else dict(pipeline_mode=pl.Buffered(buffers))
    return pl.BlockSpec(shape, lambda *_: (0,) * len(shape), **mode)


def _conv_fwd(x, g1, w_in, conv_w, w_out, later):
    S, D = x.shape
    tm = min(CONV_TILE, S)
    sub = min(ROW_TILE, tm)
    steps = S // tm
    n = len(later)

    def body(x_ref, g_ref, win_ref, cw_ref, wout_ref, *rest):
        shard_refs, rest = rest[:n], rest[n:]
        proj_ref, h_ref, yc_ref, y_ref, x1_ref = rest[:5]
        gathered_refs, rest = rest[5:5 + n], rest[5 + n:]
        prev_u = rest[0]
        if n:
            start, forward, finish = _gather_plan(shard_refs, gathered_refs, rest[1], rest[2])
            pl.when(pl.program_id(0) == 0)(start)
            pl.when(pl.program_id(0) == steps // 2)(forward)

        @pl.when(pl.program_id(0) == 0)
        def _():
            prev_u[...] = jnp.zeros((sub, D), F32)

        for r in range(0, tm, sub):
            rows = slice(r, r + sub)
            xv = x_ref[rows, :]
            inv = lax.rsqrt(jnp.mean(xv * xv, axis=-1, keepdims=True) + RMS_EPS)
            h = (xv * inv * g_ref[...]).astype(BF16)
            h_ref[rows, :] = h
            for j in range(4):
                proj_ref[rows, j * D:(j + 1) * D] = _dot(h, win_ref[j])
            u = proj_ref[rows, D:2 * D] * proj_ref[rows, 2 * D:3 * D]
            pu = prev_u[...]
            row = lax.broadcasted_iota(jnp.int32, (sub, 1), 0)
            u1 = jnp.where(row < 1, pltpu.roll(pu, 1, 0), pltpu.roll(u, 1, 0))
            u2 = jnp.where(row < 2, pltpu.roll(pu, 2, 0), pltpu.roll(u, 2, 0))
            prev_u[...] = u
            w = cw_ref[...]
            yc = w[2:3] * u + w[1:2] * u1 + w[0:1] * u2
            yc_ref[rows, :] = yc
            z = proj_ref[rows, 3 * D:4 * D]
            y = (proj_ref[rows, 0:D] * yc * (z * _sigmoid(z))).astype(BF16)
            y_ref[rows, :] = y
            x1_ref[rows, :] = xv + _dot(y, wout_ref[...])

        if n:
            pl.when(pl.program_id(0) == steps - 1)(finish)

    results = pl.pallas_call(
        body, name="conv_fwd", grid=(steps,),
        in_specs=[_rows(tm, D), _whole((1, D)), _whole((4, D, D), 1), _whole((3, D)), _whole((D, D), 1)] + [ANY] * n,
        out_specs=[_rows(tm, 4 * D), _rows(tm, D), _rows(tm, D), _rows(tm, D), _rows(tm, D)] + [ANY] * n,
        out_shape=[jax.ShapeDtypeStruct((S, 4 * D), F32), jax.ShapeDtypeStruct((S, D), BF16),
                   jax.ShapeDtypeStruct((S, D), F32), jax.ShapeDtypeStruct((S, D), BF16),
                   jax.ShapeDtypeStruct((S, D), F32)] + [jax.ShapeDtypeStruct((4,) + a.shape, a.dtype) for a in later],
        scratch_shapes=[pltpu.VMEM((sub, D), F32)] + [pltpu.SemaphoreType.DMA((n, 7))] * (2 if n else 0),
        compiler_params=_params(1),
    )(x, g1, w_in, conv_w, w_out, *later)
    return results[:5], results[5:]


def _conv_bwd(dproj2, wa, wf, x1, g2, dx2, x, g1, w_in, w_out, conv_w, proj, yc):
    S, D = x.shape
    tm = min(ROW_TILE, S)
    sub = min(ROW_TILE, tm)
    last = S // tm - 1

    def body(dp2_ref, wa_ref, wf_ref, x1_ref, g2_ref, dx2_ref, x_ref, g_ref, win_ref, wout_ref, cw_ref, proj_ref, yc_ref,
             dproj_ref, dx_ref, dx1b_ref, dg_ref, dcw_ref, dg2_ref, next_d):
        @pl.when(pl.program_id(0) == 0)
        def _():
            dg_ref[...] = jnp.zeros((1, D), F32)
            dcw_ref[...] = jnp.zeros((3, D), F32)
            dg2_ref[...] = jnp.zeros((1, D), F32)
            next_d[...] = jnp.zeros((sub, D), F32)

        for r in range(tm - sub, -1, -sub):
            rows = slice(r, r + sub)
            dh2 = _dot_nt(dp2_ref[rows, 0:4 * D], wa_ref[...]) + _dot_nt(dp2_ref[rows, 4 * D:4 * D + LANES], wf_ref[...])
            dxn2, dg2 = _rms_bwd(dh2, x1_ref[rows, :], g2_ref[...])
            dg2_ref[...] += dg2
            dx1v = dx2_ref[rows, :] + dxn2
            dx1b = dx1v.astype(BF16)
            dx1b_ref[rows, :] = dx1b
            dy = _dot_nt(dx1b, wout_ref[...])
            b = proj_ref[rows, 0:D]
            c = proj_ref[rows, D:2 * D]
            xin = proj_ref[rows, 2 * D:3 * D]
            z = proj_ref[rows, 3 * D:4 * D]
            sg = _sigmoid(z)
            sz = z * sg
            ycv = yc_ref[rows, :]
            d0 = dy * b * sz
            dproj_ref[rows, 0:D] = (dy * ycv * sz).astype(BF16)
            dproj_ref[rows, 3 * D:4 * D] = (dy * b * ycv * (sg * (1.0 + z * (1.0 - sg)))).astype(BF16)
            nd = next_d[...]
            row = lax.broadcasted_iota(jnp.int32, (sub, 1), 0)
            d1 = jnp.where(row >= sub - 1, pltpu.roll(nd, sub - 1, 0), pltpu.roll(d0, sub - 1, 0))
            d2 = jnp.where(row >= sub - 2, pltpu.roll(nd, sub - 2, 0), pltpu.roll(d0, sub - 2, 0))
            next_d[...] = d0
            w = cw_ref[...]
            du = w[2:3] * d0 + w[1:2] * d1 + w[0:1] * d2
            u = c * xin
            dcw_ref[2:3, :] += jnp.sum(d0 * u, axis=0, keepdims=True)
            dcw_ref[1:2, :] += jnp.sum(d1 * u, axis=0, keepdims=True)
            dcw_ref[0:1, :] += jnp.sum(d2 * u, axis=0, keepdims=True)
            dproj_ref[rows, D:2 * D] = (du * xin).astype(BF16)
            dproj_ref[rows, 2 * D:3 * D] = (du * c).astype(BF16)
            dh = _dot_nt(dproj_ref[rows, 0:D], win_ref[0])
            for j in range(1, 4):
                dh = dh + _dot_nt(dproj_ref[rows, j * D:(j + 1) * D], win_ref[j])
            dxn, dg = _rms_bwd(dh, x_ref[rows, :], g_ref[...])
            dx_ref[rows, :] = dx1v + dxn
            dg_ref[...] += dg

    return pl.pallas_call(
        body, name="conv_bwd", grid=(S // tm,),
        in_specs=[_rows(tm, 4 * D + LANES, last), _whole((D, 4 * D), 1), _whole((D, LANES), 1), _rows(tm, D, last),
                  _whole((1, D)), _rows(tm, D, last),
                  _rows(tm, D, last), _whole((1, D)), _whole((4, D, D), 1), _whole((D, D), 1),
                  _whole((3, D)), _rows(tm, 4 * D, last), _rows(tm, D, last)],
        out_specs=[_rows(tm, 4 * D, last), _rows(tm, D, last), _rows(tm, D, last), _whole((1, D)), _whole((3, D)),
                   _whole((1, D))],
        out_shape=[jax.ShapeDtypeStruct((S, 4 * D), BF16), jax.ShapeDtypeStruct((S, D), F32),
                   jax.ShapeDtypeStruct((S, D), BF16), jax.ShapeDtypeStruct((1, D), F32),
                   jax.ShapeDtypeStruct((3, D), F32), jax.ShapeDtypeStruct((1, D), F32)],
        scratch_shapes=[pltpu.VMEM((sub, D), F32)],
        compiler_params=_params(1),
    )(dproj2, wa, wf, x1, g2, dx2, x, g1, w_in, w_out, conv_w, proj, yc)


def _attn_front(x1, g2, w, wf, bf, gq, gk):
    S, D = x1.shape
    H = D // HEAD_DIM
    tm = min(ROW_TILE, S)
    tri = (lax.broadcasted_iota(jnp.int32, (tm, tm), 1) <= lax.broadcasted_iota(jnp.int32, (tm, tm), 0)).astype(BF16)

    def body(x_ref, g_ref, w_ref, wf_ref, bf_ref, gq_ref, gk_ref, tri_ref, first_ref, second_ref,
             h_ref, qh_ref, kh_ref, z_ref, f_ref, c_ref, rel_ref, qa_ref, ka_ref, va_ref, vt_ref,
             carry, v_s, qraw_ref, kraw_ref):
        @pl.when(pl.program_id(0) == 0)
        def _():
            carry[...] = jnp.zeros((8, LANES), F32)

        xv = x_ref[...]
        inv = lax.rsqrt(jnp.mean(xv * xv, axis=-1, keepdims=True) + RMS_EPS)
        h = (xv * inv * g_ref[...]).astype(BF16)
        h_ref[...] = h
        qraw_ref[...] = _dot(h, w_ref[:, 0:D])
        kraw_ref[...] = _dot(h, w_ref[:, D:2 * D])
        v_s[...] = _dot(h, w_ref[:, 2 * D:3 * D])
        z_ref[...] = _dot(h, w_ref[:, 3 * D:4 * D])
        lane = _lane()
        f = _dot(h, wf_ref[...]) + bf_ref[...]
        f_ref[...] = f
        logf = jnp.where(lane < H, jnp.minimum(f, 0.0) - jnp.log(1.0 + jnp.exp(-jnp.abs(f))), 0.0)
        cs = _dot01(tri_ref[...], logf) + carry[0:1, :]
        c_ref[...] = cs
        carry[...] = jnp.broadcast_to(cs[tm - 1:tm, :], (8, LANES))
        diags = jnp.zeros((tm, LANES), F32)
        for hd in range(H):
            sl = slice(LANES * hd, LANES * (hd + 1))
            a = _aug(hd % 2)
            if hd % 2 == 0:
                qh_ref[hd // 2] = qraw_ref[:, LANES * (hd // 2):LANES * (hd // 2 + 1)]
                kh_ref[hd // 2] = kraw_ref[:, LANES * (hd // 2):LANES * (hd // 2 + 1)]
            qt = _head_tile(qraw_ref, hd, lane)
            qn = qt * lax.rsqrt(jnp.sum(qt * qt, axis=1, keepdims=True) * (1.0 / HEAD_DIM) + RMS_EPS) * gq_ref[...]
            kt = _head_tile(kraw_ref, hd, lane)
            kn = kt * lax.rsqrt(jnp.sum(kt * kt, axis=1, keepdims=True) * (1.0 / HEAD_DIM) + RMS_EPS) * gk_ref[...]
            diags = diags + jnp.where(lane == hd, jnp.sum(qn * kn, axis=1, keepdims=True) * Q_SCALE, 0.0)
            qa_ref[:, sl] = (qn * Q_SCALE).astype(BF16)
            ka_ref[:, sl] = kn.astype(BF16)
            va = jnp.where((lane >= a) & (lane < a + 3), 1.0, _head_tile(v_s, hd, lane))
            va_ref[:, sl] = va.astype(BF16)
            vt_ref[hd] = va.T.astype(BF16)
        rel = cs - diags
        rel_ref[...] = rel
        qa_ref[...] += _dot(_pack3(rel, lane, 1.0), first_ref[...]).astype(BF16)
        ka_ref[...] += _dot(_pack3(-cs, lane, 1.0), second_ref[...]).astype(BF16)

    nb = S // tm
    heads = pl.BlockSpec((H // 2, tm, LANES), lambda i: (0, i, 0))
    return pl.pallas_call(
        body, name="attn_front", grid=(nb,),
        in_specs=[_rows(tm, D), _whole((1, D)), _whole((D, 4 * D)), _whole((D, LANES)), _whole((1, LANES)),
                  _whole((1, LANES)), _whole((1, LANES)), _whole((tm, tm)), _whole((LANES, H * LANES)),
                  _whole((LANES, H * LANES))],
        out_specs=[_rows(tm, D), heads, heads, _rows(tm, D), _rows(tm, LANES), _rows(tm, LANES), _rows(tm, LANES),
                   _rows(tm, H * LANES), _rows(tm, H * LANES), _rows(tm, H * LANES),
                   pl.BlockSpec((H, None, LANES, tm), lambda i: (0, i, 0, 0))],
        out_shape=[jax.ShapeDtypeStruct((S, D), BF16), jax.ShapeDtypeStruct((H // 2, S, LANES), F32),
                   jax.ShapeDtypeStruct((H // 2, S, LANES), F32), jax.ShapeDtypeStruct((S, D), F32),
                   jax.ShapeDtypeStruct((S, LANES), F32), jax.ShapeDtypeStruct((S, LANES), F32),
                   jax.ShapeDtypeStruct((S, LANES), F32),
                   jax.ShapeDtypeStruct((S, H * LANES), BF16), jax.ShapeDtypeStruct((S, H * LANES), BF16),
                   jax.ShapeDtypeStruct((S, H * LANES), BF16), jax.ShapeDtypeStruct((H, nb, LANES, tm), BF16)],
        scratch_shapes=[pltpu.VMEM((8, LANES), F32), pltpu.VMEM((tm, D), F32), pltpu.VMEM((tm, D), F32),
                        pltpu.VMEM((tm, D), F32)],
        compiler_params=_params(1),
    )(x1, g2, w, wf, bf, gq, gk, tri, *_scatter_matrices(H))


def _skip_tables(c, diag, gq, gk, T, G):
    nb = c.shape[0] // T
    bound = 8.0 * jnp.max(jnp.abs(gq)) * jnp.max(jnp.abs(gk))
    first, last = c[0::T, :], c[T - 1::T, :]
    lowest = jnp.maximum(jnp.min(diag.reshape(nb, T, -1), axis=1), -bound)
    idx = jnp.arange(nb)
    margin = (SKIP_LOG + bound) - lowest
    need = (last[None, :, :] <= first[:, None, :] + margin[:, None, :]) & (idx[None, :, None] < idx[:, None, None])
    need = need | (idx[None, :, None] == idx[:, None, None])
    kstart = jnp.argmax(need, axis=1)
    qend = nb - 1 - jnp.argmax(need[::-1], axis=0)
    kstart = jnp.min(kstart.reshape(2 * nb // G, G // 2, -1), axis=1)
    kstart = kstart - (kstart & 1)
    qend = jnp.max(qend.reshape(2 * nb // G, G // 2, -1), axis=1)
    return kstart.T.astype(jnp.int32), qend.T.astype(jnp.int32), bound


def _attn_fwd(kstart, qa, ka, vt, online_max):
    S = qa.shape[0]
    H = qa.shape[1] // LANES
    nb, T = vt.shape[1], vt.shape[3]
    G = 2 * nb // kstart.shape[1]
    W = G * T

    def finish(acc, shift, o_ref, lse_ref):
        a = _aug(pl.program_id(0) % 2)
        feat = lax.broadcasted_iota(jnp.int32, (LANES, 1), 0)
        l = jnp.sum(jnp.where(feat == a, acc, 0.0), axis=0, keepdims=True)
        o_ref[...] = (acc * (1.0 / l)).T
        lse_ref[...] = shift + jnp.log(l)

    def causal(st):
        return jnp.where(lax.broadcasted_iota(jnp.int32, st.shape, 0) <= lax.broadcasted_iota(jnp.int32, st.shape, 1),
                         st, NEG)

    def fast_body(ks_ref, q_ref, k_ref, vt_ref, o_ref, lse_ref, acc_ref, sa_ref, sb_ref, sc_ref):
        h, g = pl.program_id(0), pl.program_id(1)
        q = q_ref[...]
        acc_ref[...] = jnp.zeros((LANES, W), F32)

        def scores(ki, lo):
            return _dot_nt(k_ref[pl.ds(pl.multiple_of(ki * T, T), 2 * T), :], q[lo * T:, :])

        def weighted(ki, p):
            return _dot(vt_ref[ki], p[:T]) + _dot(vt_ref[ki + 1], p[T:])

        first = ks_ref[h, 2 * g + 1]
        early = jnp.minimum(ks_ref[h, 2 * g], first)

        def narrow(i, carry):
            ki = early + 2 * i
            st = _dot_nt(k_ref[pl.ds(pl.multiple_of(ki * T, T), 2 * T), :], q[:W // 2, :])
            acc_ref[:, :W // 2] += weighted(ki, jnp.exp(st).astype(BF16))
            return carry

        lax.fori_loop(0, (first - early) // 2, narrow, 0)
        steps = (g * G - first) // 2
        sa_ref[...] = scores(first, 0)

        def advance(ki, cur_ref, next_ref):
            p = jnp.exp(cur_ref[...]).astype(BF16)
            next_ref[...] = scores(ki + 2, 0)
            acc_ref[...] += weighted(ki, p)

        def loop(i, carry):
            advance(first + 4 * i, sa_ref, sb_ref)
            advance(first + 4 * i + 2, sb_ref, sa_ref)
            return carry

        lax.fori_loop(0, steps // 2, loop, 0)

        def first_own(pending_ref):
            p = jnp.exp(causal(pending_ref[...])).astype(BF16)
            if G > 2:
                sc_ref[:, :W - 2 * T] = scores(g * G + 2, 2)
            acc_ref[...] += weighted(g * G, p)

        @pl.when(steps % 2 == 1)
        def _():
            advance(g * G - 2, sa_ref, sb_ref)
            first_own(sb_ref)

        @pl.when(steps % 2 == 0)
        def _():
            first_own(sa_ref)

        if G > 2:
            acc_ref[:, 2 * T:] += weighted(g * G + 2, jnp.exp(causal(sc_ref[:, :W - 2 * T])).astype(BF16))
        for j in range(4, G, 2):
            p = jnp.exp(causal(scores(g * G + j, j))).astype(BF16)
            acc_ref[:, j * T:] += weighted(g * G + j, p)
        finish(acc_ref[...], 0.0, o_ref, lse_ref)

    def online_body(ks_ref, q_ref, k_ref, vt_ref, o_ref, lse_ref, acc_ref, m_ref):
        h, g = pl.program_id(0), pl.program_id(1)
        q = q_ref[...]
        m_ref[...] = jnp.full((8, W), NEG, F32)
        acc_ref[...] = jnp.zeros((LANES, W), F32)

        def update(st, vtb, lo):
            m_old = m_ref[0:1, lo:]
            m_new = jnp.maximum(m_old, jnp.max(st, axis=0, keepdims=True))
            p = jnp.exp(st - m_new).astype(BF16)
            acc_ref[:, lo:] = acc_ref[:, lo:] * jnp.exp(m_old - m_new) + _dot(vtb, p)
            m_ref[:, lo:] = jnp.broadcast_to(m_new, (8, W - lo))

        def loop(ki, carry):
            kb = k_ref[pl.ds(pl.multiple_of(ki * T, T), T), :]
            update(_dot_nt(kb, q), vt_ref[ki], 0)
            return carry

        lax.fori_loop(jnp.minimum(ks_ref[h, 2 * g], ks_ref[h, 2 * g + 1]), g * G, loop, 0)
        for j in range(G):
            ki = g * G + j
            kb = k_ref[pl.ds(pl.multiple_of(ki * T, T), T), :]
            update(causal(_dot_nt(kb, q[j * T:, :])), vt_ref[ki], j * T)
        finish(acc_ref[...], m_ref[0:1, :], o_ref, lse_ref)

    return pl.pallas_call(
        online_body if online_max else fast_body, name="attn_fwd_online" if online_max else "attn_fwd",
        grid_spec=pltpu.PrefetchScalarGridSpec(
            num_scalar_prefetch=1, grid=(H, nb // G),
            in_specs=[pl.BlockSpec((W, LANES), lambda h, i, ks: (i, h)),
                      pl.BlockSpec((S, LANES), lambda h, i, ks: (0, h)),
                      pl.BlockSpec((None, nb, LANES, T), lambda h, i, ks: (h, 0, 0, 0))],
            out_specs=[pl.BlockSpec((W, LANES), lambda h, i, ks: (i, h)),
                       pl.BlockSpec((None, 1, W), lambda h, i, ks: (h, 0, i))],
            scratch_shapes=[pltpu.VMEM((LANES, W), F32)] + (
                [pltpu.VMEM((8, W), F32)] if online_max else [pltpu.VMEM((2 * T, W), F32)] * 3)),
        out_shape=[jax.ShapeDtypeStruct((S, H * LANES), F32), jax.ShapeDtypeStruct((H, 1, S), F32)],
        compiler_params=_params(2),
    )(kstart, qa, ka, vt)


def _attn_out(o_aug, lse, rel, z, x1, target, w_out, qa):
    S, D = x1.shape
    H = D // HEAD_DIM
    tm = min(ROW_TILE, S)

    def body(o_ref, z_ref, x1_ref, t_ref, w_ref, q_ref, first_ref, rel_ref, lse_ref,
             dx2_ref, dx2b_ref, o2b_ref, dz_ref, doa_ref, qa2_ref, loss_ref, oc_s, do_s):
        @pl.when(pl.program_id(0) == 0)
        def _():
            loss_ref[...] = jnp.zeros((1, LANES), F32)

        lane = _lane()
        for j in range(H // 2):
            oc_s[:, LANES * j:LANES * (j + 1)] = _pair_tile(o_ref[:, 2 * LANES * j:2 * LANES * j + LANES],
                                                            o_ref[:, 2 * LANES * j + LANES:2 * LANES * (j + 1)], lane)
        oc = oc_s[...]
        zv = z_ref[...]
        sg = _sigmoid(zv)
        sz = zv * sg
        o2 = (oc * sz).astype(BF16)
        o2b_ref[...] = o2
        e = x1_ref[...] + _dot(o2, w_ref[...]) - t_ref[...]
        sq = jnp.sum(jnp.sum(e * e, axis=1, keepdims=True), axis=0, keepdims=True)
        loss_ref[...] += jnp.broadcast_to(sq * (0.5 / D), (1, LANES))
        dx2 = e * (1.0 / D)
        dx2_ref[...] = dx2
        dx2b = dx2.astype(BF16)
        dx2b_ref[...] = dx2b
        do2 = _dot_nt(dx2b, w_ref[...])
        dz_ref[...] = (do2 * oc * (sg * (1.0 + zv * (1.0 - sg)))).astype(BF16)
        do_s[...] = do2 * sz
        deltas = jnp.zeros((tm, LANES), F32)
        for hd in range(H):
            dt = _head_tile(do_s, hd, lane)
            delta = jnp.sum(dt * _head_tile(oc_s, hd, lane), axis=1, keepdims=True)
            deltas = deltas + jnp.where(lane == hd, delta, 0.0)
            doa_ref[:, LANES * hd:LANES * (hd + 1)] = dt.astype(BF16)
        doa_ref[...] += _dot(_pack3(-deltas, lane, 0.0), first_ref[...]).astype(BF16)
        lse = jnp.concatenate([lse_ref[...], jnp.zeros((LANES - H, tm), F32)], axis=0).T
        rq = rel_ref[...] - lse
        tile_lane = lax.broadcasted_iota(jnp.int32, (1, H * LANES), 1)
        extra = tile_lane % LANES - _aug((tile_lane // LANES) % 2)
        kept = jnp.where((extra >= 0) & (extra < 3), jnp.zeros((), BF16), q_ref[...])
        qa2_ref[...] = kept + _dot(_pack3(rq, lane, 0.0), first_ref[...]).astype(BF16)

    return pl.pallas_call(
        body, name="attn_out", grid=(S // tm,),
        in_specs=[_rows(tm, H * LANES), _rows(tm, D), _rows(tm, D), _rows(tm, D), _whole((D, D)),
                  _rows(tm, H * LANES), _whole((LANES, H * LANES)), _rows(tm, LANES),
                  pl.BlockSpec((H, tm), lambda i: (0, i))],
        out_specs=[_rows(tm, D), _rows(tm, D), _rows(tm, D), _rows(tm, D), _rows(tm, H * LANES),
                   _rows(tm, H * LANES), _whole((1, LANES))],
        out_shape=[jax.ShapeDtypeStruct((S, D), F32), jax.ShapeDtypeStruct((S, D), BF16),
                   jax.ShapeDtypeStruct((S, D), BF16), jax.ShapeDtypeStruct((S, D), BF16),
                   jax.ShapeDtypeStruct((S, H * LANES), BF16), jax.ShapeDtypeStruct((S, H * LANES), BF16),
                   jax.ShapeDtypeStruct((1, LANES), F32)],
        scratch_shapes=[pltpu.VMEM((tm, D), F32), pltpu.VMEM((tm, D), F32)],
        compiler_params=_params(1),
    )(o_aug, z, x1, target, w_out, qa, _scatter_matrices(H)[0], rel, lse)


def _attn_bwd(qend, qa2, doa, ka, va, T):
    S = qa2.shape[0]
    H = qa2.shape[1] // LANES
    nb = S // T
    G = 2 * nb // qend.shape[1]
    W = G * T

    def body(qe_ref, q_ref, do_ref, k_ref, v_ref, dq_ref, dk_ref, dv_ref, dkt_acc, dvt_acc):
        h, g = pl.program_id(0), pl.program_id(1)

        @pl.when(g == 0)
        def _():
            dq_ref[...] = jnp.zeros((S, LANES), F32)

        kb = k_ref[...]
        vb = v_ref[...]
        dkt_acc[...] = jnp.zeros((LANES, W), F32)
        dvt_acc[...] = jnp.zeros((LANES, W), F32)

        def step(qi, c0, c1, masked):
            rows = pl.ds(pl.multiple_of(qi * T, T), 2 * T)
            qb = q_ref[rows, :]
            dob = do_ref[rows, :]
            s = _dot_nt(qb, kb[c0:c1])
            if masked:
                query = lax.broadcasted_iota(jnp.int32, s.shape, 0) + (c1 - 2 * T)
                s = jnp.where(lax.broadcasted_iota(jnp.int32, s.shape, 1) <= query, s, NEG)
            p = jnp.exp(s)
            ds = (p * _dot_nt(dob, vb[c0:c1])).astype(BF16)
            dvt_acc[:, c0:c1] += _dot(dob.astype(F32).T.astype(BF16), p.astype(BF16))
            dkt_acc[:, c0:c1] += _dot(qb.astype(F32).T.astype(BF16), ds)
            dq_ref[rows, :] += _dot(ds, kb[c0:c1])

        for m in range(G // 2):
            step(g * G + 2 * m, 0, (m + 1) * 2 * T, True)
        first = g * G + G
        n_all = jnp.maximum((qe_ref[h, 2 * g] - first + 2) // 2, 0)
        second = first + 2 * n_all

        def all_keys(i, carry):
            step(first + 2 * i, 0, W, False)
            return carry

        def late_keys(i, carry):
            step(second + 2 * i, W // 2, W, False)
            return carry

        lax.fori_loop(0, n_all, all_keys, 0)
        lax.fori_loop(0, (qe_ref[h, 2 * g + 1] - second + 2) // 2, late_keys, 0)
        dk_ref[...] = dkt_acc[...].T
        dv_ref[...] = dvt_acc[...].T.astype(BF16)

    heads = pl.BlockSpec((None, W, LANES), lambda h, i, qe: (h, i, 0))
    return pl.pallas_call(
        body, name="attn_bwd",
        grid_spec=pltpu.PrefetchScalarGridSpec(
            num_scalar_prefetch=1, grid=(H, nb // G),
            in_specs=[pl.BlockSpec((S, LANES), lambda h, i, qe: (0, h)), pl.BlockSpec((S, LANES), lambda h, i, qe: (0, h)),
                      pl.BlockSpec((W, LANES), lambda h, i, qe: (i, h)), pl.BlockSpec((W, LANES), lambda h, i, qe: (i, h))],
            out_specs=[pl.BlockSpec((None, S, LANES), lambda h, i, qe: (h, 0, 0)), heads, heads],
            scratch_shapes=[pltpu.VMEM((LANES, W), F32), pltpu.VMEM((LANES, W), F32)]),
        out_shape=[jax.ShapeDtypeStruct((H, S, LANES), F32), jax.ShapeDtypeStruct((H, S, LANES), F32),
                   jax.ShapeDtypeStruct((H, S, LANES), BF16)],
        compiler_params=_params(2),
    )(qend, qa2, doa, ka, va)


def _attn_proj_bwd(dqt, dka, dva, qraw, kraw, dz, f, gq, gk):
    S, D = dz.shape
    H = D // HEAD_DIM
    tm = min(CONV_TILE, S)
    last = S // tm - 1
    tri = (lax.broadcasted_iota(jnp.int32, (tm, tm), 1) >= lax.broadcasted_iota(jnp.int32, (tm, tm), 0)).astype(BF16)

    def body(dq_ref, dk_ref, dv_ref, q_ref, k_ref, dz_ref, f_ref, gq_ref, gk_ref, tri_ref, ones_ref,
             dproj_ref, small_ref, carry, pairs):
        @pl.when(pl.program_id(0) == 0)
        def _():
            small_ref[...] = jnp.zeros((8, LANES), F32)
            carry[...] = jnp.zeros((8, LANES), F32)

        lane = _lane()

        def head_pair(j, acc):
            dcs, dgq, dgk = acc
            dq2, dk2 = [], []
            q_pair, k_pair = q_ref[j], k_ref[j]
            for parity in (0, 1):
                hd = 2 * j + parity
                own, a = _own(lane, parity), _aug(parity)
                dqf = dq_ref[hd]
                dqn = jnp.where(own, dqf * Q_SCALE, 0.0)
                d, dg = _head_rms_bwd(dqn, jnp.where(own, q_pair, 0.0), gq_ref[...], ones_ref[...])
                dq2.append(d)
                dgq = dgq + dg
                dkt = dk_ref[hd]
                dcs = dcs + jnp.where(lane == hd, _col(dqf, lane, a) - _col(dkt, lane, a + 3), 0.0)
                d, dg = _head_rms_bwd(jnp.where(own, dkt, 0.0), jnp.where(own, k_pair, 0.0), gk_ref[...], ones_ref[...])
                dk2.append(d)
                dgk = dgk + dg
            pairs[0, j] = _pair_tile(*dq2, lane).astype(BF16)
            pairs[1, j] = _pair_tile(*dk2, lane).astype(BF16)
            pairs[2, j] = _pair_tile(dv_ref[2 * j], dv_ref[2 * j + 1], lane)
            return dcs, dgq, dgk

        zero = jnp.zeros((1, LANES), F32)
        dcs, dgq, dgk = lax.fori_loop(0, H // 2, head_pair, (jnp.zeros((tm, LANES), F32), zero, zero))
        for part in range(3):
            for j in range(H // 2):
                dproj_ref[:, part * D + LANES * j:part * D + LANES * (j + 1)] = pairs[part, j]
        dproj_ref[:, 3 * D:4 * D] = dz_ref[...]
        dlogf = _dot01(tri_ref[...], dcs) + carry[0:1, :]
        carry[...] = jnp.broadcast_to(dlogf[0:1, :], (8, LANES))
        df = dlogf * (1.0 / (1.0 + jnp.exp(f_ref[...])))
        dproj_ref[:, 4 * D:4 * D + LANES] = df.astype(BF16)
        small_ref[0:1, :] += jnp.sum(df, axis=0, keepdims=True)
        small_ref[1:2, :] += dgq
        small_ref[2:3, :] += dgk

    W = 4 * D + LANES
    heads = pl.BlockSpec((H, tm, LANES), lambda i: (0, last - i, 0))
    head_pairs = pl.BlockSpec((H // 2, tm, LANES), lambda i: (0, last - i, 0))
    return pl.pallas_call(
        body, name="attn_proj_bwd", grid=(S // tm,),
        in_specs=[heads, heads, heads, head_pairs, head_pairs,
                  _rows(tm, D, last), _rows(tm, LANES, last), _whole((1, LANES)), _whole((1, LANES)),
                  _whole((tm, tm)), _whole((LANES, LANES))],
        out_specs=[_rows(tm, W, last), _whole((8, LANES))],
        out_shape=[jax.ShapeDtypeStruct((S, W), BF16), jax.ShapeDtypeStruct((8, LANES), F32)],
        scratch_shapes=[pltpu.VMEM((8, LANES), F32), pltpu.VMEM((3, H // 2, tm, LANES), BF16)],
        compiler_params=_params(1),
    )(dqt, dka, dva, qraw, kraw, dz, f, gq, gk, tri, jnp.ones((LANES, LANES), BF16))


def _matmul_tn(a, b, col0, n, tn, name, stacked=False):
    S, M = a.shape
    ts = min(TN_ROWS, S)
    off = col0 // tn

    def body(a_ref, b_ref, o_ref):
        @pl.when(pl.program_id(1) == 0)
        def _():
            o_ref[...] = jnp.zeros((M, tn), F32)

        o_ref[...] += _dot_tn(a_ref[...], b_ref[...])

    if stacked:
        out_spec, out_shape = pl.BlockSpec((None, M, tn), lambda j, s: (j, 0, 0)), (n // tn, M, tn)
    else:
        out_spec, out_shape = pl.BlockSpec((M, tn), lambda j, s: (0, j)), (M, n)
    return pl.pallas_call(
        body, name=name, grid=(n // tn, S // ts),
        in_specs=[pl.BlockSpec((ts, M), lambda j, s: (s, 0)), pl.BlockSpec((ts, tn), lambda j, s: (s, off + j))],
        out_specs=out_spec, out_shape=jax.ShapeDtypeStruct(out_shape, F32),
        compiler_params=_params(2),
    )(a, b)


def _adam_update(gv, w_ref, m_ref, v_ref, d_ref, m2_ref, v2_ref):
    m2 = ADAM_B1 * m_ref[...] + (1.0 - ADAM_B1) * gv
    v2 = ADAM_B2 * v_ref[...] + (1.0 - ADAM_B2) * (gv * gv)
    m2_ref[...] = m2
    v2_ref[...] = v2
    m_hat = m2 / (1.0 - ADAM_B1 ** ADAM_STEP)
    v_hat = v2 / (1.0 - ADAM_B2 ** ADAM_STEP)
    d_ref[...] = -ADAM_LR * (m_hat / (jnp.sqrt(v_hat) + ADAM_EPS) + ADAM_WD * w_ref[...])


def _adamw(w, g, m, v, name):
    r, c = w.shape
    tr = ROW_TILE if r % ROW_TILE == 0 else r

    def body(w_ref, g_ref, m_ref, v_ref, d_ref, m2_ref, v2_ref):
        _adam_update(g_ref[...], w_ref, m_ref, v_ref, d_ref, m2_ref, v2_ref)

    spec = _rows(tr, c)
    return pl.pallas_call(
        body, name=name, grid=(r // tr,), in_specs=[spec] * 4, out_specs=[spec] * 3,
        out_shape=[jax.ShapeDtypeStruct((r, c), F32)] * 3, compiler_params=_params(1),
    )(w, g, m, v)


def _adamw_halves(w, mine, other, m, v, core, name):
    r, c = mine.shape
    tr = ROW_TILE if r % ROW_TILE == 0 else r
    per = r // tr

    def body(core_ref, w_ref, mine_ref, other_ref, m_ref, v_ref, g_ref, d_ref, m2_ref, v2_ref):
        gv = jnp.where(pl.program_id(0) // per == core_ref[0], mine_ref[...], other_ref[...])
        g_ref[...] = gv
        _adam_update(gv, w_ref, m_ref, v_ref, d_ref, m2_ref, v2_ref)

    full = pl.BlockSpec((tr, c), lambda i, core: (i, 0))
    half = pl.BlockSpec((tr, c), lambda i, core: (i % per, 0))
    return pl.pallas_call(
        body, name=name,
        grid_spec=pltpu.PrefetchScalarGridSpec(num_scalar_prefetch=1, grid=(2 * per,),
                                               in_specs=[full, half, half, full, full], out_specs=[full] * 4),
        out_shape=[jax.ShapeDtypeStruct((2 * r, c), F32)] * 4, compiler_params=_params(1),
    )(core, w, mine, other, m, v)


def _after_conv(conv_acts, x, target, g1, w_in, conv_w, w_out, g2, wa_in, b_f, gq, gk, wa_out):
    S, D = x.shape
    H = D // HEAD_DIM
    ws = wa_in.shape[2]
    w_qkvz = jnp.concatenate([wa_in[0], wa_in[1], wa_in[2], wa_in[3][:, :4 * D - 3 * ws]], axis=1)
    wf = jnp.pad(wa_in[3][:, 4 * D - 3 * ws:], ((0, 0), (0, LANES - H)))
    bf = jnp.pad(b_f, ((0, 0), (0, LANES - H)))
    gq128 = jnp.concatenate([gq, gq], axis=1)
    gk128 = jnp.concatenate([gk, gk], axis=1)

    proj, h1, yc, y, x1 = conv_acts
    h2, qraw, kraw, z, f, c, rel, qa, ka, va, vt = _attn_front(x1, g2, w_qkvz, wf, bf, gq128, gk128)
    T = vt.shape[3]
    kstart, qend, bound = _skip_tables(c[:, :H], (c - rel)[:, :H], gq, gk, T, min(ATT_GROUP, S // T))
    o_aug, lse = lax.cond(2.0 * bound <= PLAIN_EXP_MAX, functools.partial(_attn_fwd, online_max=False),
                          functools.partial(_attn_fwd, online_max=True), kstart, qa, ka, vt)
    dx2, dx2b, o2b, dz, doa, qa2, loss = _attn_out(o_aug, lse.reshape(H, S), rel, z, x1, target, wa_out, qa)
    dqt, dka, dva = _attn_bwd(qend, qa2, doa, ka, va, T)
    dproj2, small = _attn_proj_bwd(dqt, dka, dva, qraw, kraw, dz, f, gq128, gk128)
    dproj1, dx, dx1b, dg1, dcw, dg2 = _conv_bwd(dproj2, w_qkvz, wf, x1, g2, dx2, x, g1, w_in, w_out, conv_w, proj, yc)

    tn = min(1024, D)
    dwa_out = _matmul_tn(o2b, dx2b, 0, D, tn, "dw_attn_out")
    dw_main = _matmul_tn(h2, dproj2, 0, 4 * D, 2 * tn, "dw_attn_in")
    dw_f = _matmul_tn(h2, dproj2, 4 * D, LANES, LANES, "dw_attn_f")
    dwa_in = jnp.stack([dw_main[:, 0:ws], dw_main[:, ws:2 * ws], dw_main[:, 2 * ws:3 * ws],
                        jnp.concatenate([dw_main[:, 3 * ws:], dw_f[:, :H]], axis=1)])
    dw_out = _matmul_tn(y, dx1b, 0, D, tn, "dw_conv_out")
    dw_in = _matmul_tn(h1, dproj1, 0, 4 * D, D, "dw_conv_in", stacked=True)
    grads = dict(conv_norm_g=dg1, conv_w_in=dw_in, conv_w=dcw, conv_w_out=dw_out, attn_norm_g=dg2,
                 attn_w_in=dwa_in, attn_b_f=small[0:1, :H],
                 attn_q_norm_g=small[1:2, :HEAD_DIM] + small[1:2, HEAD_DIM:],
                 attn_k_norm_g=small[2:3, :HEAD_DIM] + small[2:3, HEAD_DIM:], attn_w_out=dwa_out)
    return loss[0, 0], dx, grads


def _coords():
    return lax.axis_index("x"), lax.axis_index("y"), lax.axis_index("c")


def _at(ref, idx):
    return ref.at[idx] if idx else ref


def _other_chips(x, y):
    return [(1 - x, y), (x, 1 - y), (1 - x, 1 - y)]


def _gather_plan(src, out, send, recv):
    x, y, c = _coords()
    mine = 2 * x + y
    sibling = (x, y, 1 - c)
    others = [(a, k, 2 * px + py, (px, py)) for a in range(len(src)) for k, (px, py) in enumerate(_other_chips(x, y))]

    def copy(a, k, chip, half, to, source=None):
        dst = out[a].at[chip, half]
        return pltpu.make_async_remote_copy(src_ref=dst if source is None else source, dst_ref=dst,
                                            send_sem=send.at[a, k], recv_sem=recv.at[a, k],
                                            device_id=to, device_id_type=MESH)

    first = [copy(a, k, mine, c, (*chip, c), source=src[a].at[c]) for a, k, _, chip in others]
    own = [pltpu.make_async_remote_copy(src_ref=src[a], dst_ref=out[a].at[mine], send_sem=send.at[a, 6],
                                        recv_sem=recv.at[a, 6], device_id=sibling, device_id_type=MESH)
           for a in range(len(src))]
    passed = [copy(a, 3 + k, slot, c, sibling) for a, k, slot, _ in others]

    def start():
        for cp in first + own:
            cp.start()

    def forward():
        for (a, k, slot, _), cp in zip(others, passed):
            copy(a, k, slot, c, (x, y, c)).wait_recv()
            cp.start()

    def finish():
        for a, k, slot, _ in others:
            copy(a, 3 + k, slot, 1 - c, (x, y, c)).wait_recv()
        for cp in own:
            cp.wait_recv()
        for cp in first + passed + own:
            cp.wait_send()

    return start, forward, finish


def _all_gather(halved, whole):
    nh, nw = len(halved), len(whole)

    def body(*refs):
        src_h, src_w = refs[:nh], refs[nh:nh + nw]
        out_h, out_w = refs[nh + nw:2 * nh + nw], refs[2 * nh + nw:2 * (nh + nw)]
        send_h, recv_h, send_w, recv_w = refs[2 * (nh + nw):]
        x, y, c = _coords()
        mine = 2 * x + y
        chips = _other_chips(x, y)

        def copy_w(a, k, chip, to):
            return pltpu.make_async_remote_copy(src_ref=src_w[a], dst_ref=out_w[a].at[chip],
                                                send_sem=send_w.at[a, k], recv_sem=recv_w.at[a, k],
                                                device_id=to, device_id_type=MESH)

        start, forward, finish = _gather_plan(src_h, out_h, send_h, recv_h)
        small = [copy_w(a, k, mine, (*chip, c)) for a in range(nw) for k, chip in enumerate(chips)]
        small += [copy_w(a, 3, mine, (x, y, 1 - c)) for a in range(nw)]
        start()
        for cp in small:
            cp.start()
        forward()
        finish()
        for a in range(nw):
            for k, (px, py) in enumerate(chips):
                copy_w(a, k, 2 * px + py, (x, y, c)).wait_recv()
            copy_w(a, 3, mine, (x, y, c)).wait_recv()
        for cp in small:
            cp.wait_send()

    out_shape = [jax.ShapeDtypeStruct((4,) + a.shape, a.dtype) for a in list(halved) + list(whole)]
    return pl.pallas_call(
        body, name="gather_weights", in_specs=[ANY] * (nh + nw), out_specs=[ANY] * (nh + nw), out_shape=out_shape,
        scratch_shapes=[pltpu.SemaphoreType.DMA((nh, 7)), pltpu.SemaphoreType.DMA((nh, 7)),
                        pltpu.SemaphoreType.DMA((nw, 4)), pltpu.SemaphoreType.DMA((nw, 4))],
    )(*halved, *whole)


def _exchange(name, srcs, lands, copies, local_copies):
    ns, nl, n, nloc = len(srcs), len(lands), len(copies), len(local_copies)

    def body(*refs):
        src, land = refs[:ns], refs[ns:ns + nl]
        send, recv, local_sem = refs[ns + nl:]
        me = _coords()
        started = []
        for k, (si, s_at, li, l_at, ci) in enumerate(local_copies):
            cp = pltpu.make_async_copy(_at(src[si], s_at(*me)), _at(land[li], l_at(*me)), local_sem.at[k])
            cp.start()
            started.append(cp)
        remote = []
        for k, (si, s_at, li, l_at, peer) in enumerate(copies):
            cp = pltpu.make_async_remote_copy(src_ref=_at(src[si], s_at(*me)), dst_ref=_at(land[li], l_at(*me)),
                                              send_sem=send.at[k], recv_sem=recv.at[k],
                                              device_id=peer(*me), device_id_type=MESH)
            cp.start()
            remote.append(cp)
        for cp in remote:
            cp.wait()
        for cp in started:
            cp.wait()

    return pl.pallas_call(
        body, name=name, in_specs=[ANY] * ns, out_specs=[ANY] * nl, out_shape=list(lands),
        scratch_shapes=[pltpu.SemaphoreType.DMA((n,)), pltpu.SemaphoreType.DMA((n,)),
                        pltpu.SemaphoreType.DMA((max(nloc, 1),))],
    )(*srcs)


def _add_pairs(a, b, core, name):
    _, r, cols = b.shape
    tr = ROW_TILE if r % ROW_TILE == 0 else r

    def body(core_ref, a_ref, b_ref, o_ref, ob_ref):
        s = a_ref[...] + b_ref[...]
        o_ref[...] = s
        ob_ref[...] = s.astype(BF16)

    spec = pl.BlockSpec((None, tr, cols), lambda j, i, core: (j, i, 0))
    return pl.pallas_call(
        body, name=name,
        grid_spec=pltpu.PrefetchScalarGridSpec(
            num_scalar_prefetch=1, grid=(4, r // tr),
            in_specs=[pl.BlockSpec((None, None, tr, cols), lambda j, i, core: (j, core[0], i, 0)), spec],
            out_specs=[spec, spec]),
        out_shape=[jax.ShapeDtypeStruct(b.shape, F32), jax.ShapeDtypeStruct(b.shape, BF16)],
        compiler_params=_params(2),
    )(core, a, b)


def _sum_chips(own, landed, name):
    _, r, cols = landed.shape
    tr = ROW_TILE if r % ROW_TILE == 0 else r

    def body(own_ref, land_ref, o_ref):
        acc = own_ref[...]
        for j in range(3):
            acc = acc + land_ref[j].astype(F32)
        o_ref[...] = acc

    return pl.pallas_call(
        body, name=name, grid=(r // tr,),
        in_specs=[_rows(tr, cols), pl.BlockSpec((3, tr, cols), lambda i: (0, i, 0))], out_specs=_rows(tr, cols),
        out_shape=jax.ShapeDtypeStruct((r, cols), F32), compiler_params=_params(1),
    )(own, landed)


def _sum_devices(landed, name):
    def body(l_ref, o_ref):
        acc = l_ref[0]
        for j in range(1, 8):
            acc = acc + l_ref[j]
        o_ref[...] = acc

    return pl.pallas_call(body, name=name, out_shape=jax.ShapeDtypeStruct(landed.shape[1:], F32))(landed)


def _reduce_gradients(big, small):
    nb = len(big)
    x, y, c = _coords()
    mine = 2 * x + y
    flips = [(fx, fy, fc) for fx in (0, 1) for fy in (0, 1) for fc in (0, 1) if fx or fy or fc]

    def flip(fx, fy, fc):
        return lambda x, y, c: (x ^ fx, y ^ fy, c ^ fc)

    copies = [(a, (lambda j: lambda x, y, c: (j, 1 - c))(j), a, (lambda j: lambda x, y, c: (j,))(j), flip(0, 0, 1))
              for a in range(nb) for j in range(4)]
    copies += [(nb, lambda x, y, c: (), nb, lambda x, y, c: (4 * x + 2 * y + c,), flip(*f)) for f in flips]
    lands = [jax.ShapeDtypeStruct((4,) + g.shape[2:], F32) for g in big] + [jax.ShapeDtypeStruct((8,) + small.shape, F32)]
    local = [(nb, lambda x, y, c: (), nb, lambda x, y, c: (4 * x + 2 * y + c,), None)]
    landed = _exchange("swap_halves", list(big) + [small], lands, copies, local)
    small_sum = _sum_devices(landed[nb], "sum_small")

    chip_f32, chip_bf16 = [], []
    for a in range(nb):
        s, sb = _add_pairs(big[a], landed[a], jnp.reshape(c, (1,)).astype(jnp.int32), f"add_cores_{a}")
        chip_f32.append(s)
        chip_bf16.append(sb)

    chip_flips = [(1, 0), (0, 1), (1, 1)]
    copies = [(a, (lambda f: lambda x, y, c: (2 * (x ^ f[0]) + (y ^ f[1]),))(f), a, (lambda k: lambda x, y, c: (k,))(k),
               flip(f[0], f[1], 0)) for a in range(nb) for k, f in enumerate(chip_flips)]
    lands = [jax.ShapeDtypeStruct((3,) + g.shape[1:], BF16) for g in chip_bf16]
    landed = _exchange("send_chip_sums", chip_bf16, lands, copies, [])
    totals = [_sum_chips(lax.dynamic_index_in_dim(chip_f32[a], mine, axis=0, keepdims=False), landed[a],
                         f"sum_chips_{a}") for a in range(nb)]

    copies = [(a, lambda x, y, c: (), a, lambda x, y, c: (), flip(0, 0, 1)) for a in range(nb)]
    lands = [jax.ShapeDtypeStruct(t.shape, F32) for t in totals]
    return list(zip(totals, _exchange("swap_sums", totals, lands, copies, []))), small_sum


def kernel(x, conv_norm_g, conv_w_in, conv_w, conv_w_out, attn_norm_g, attn_w_in, attn_b_f, attn_q_norm_g, attn_k_norm_g, attn_w_out, loss_target, m_conv_norm_g, m_conv_w_in, m_conv_w, m_conv_w_out, m_attn_norm_g, m_attn_w_in, m_attn_b_f, m_attn_q_norm_g, m_attn_k_norm_g, m_attn_w_out, v_conv_norm_g, v_conv_w_in, v_conv_w, v_conv_w_out, v_attn_norm_g, v_attn_w_in, v_attn_b_f, v_attn_q_norm_g, v_attn_k_norm_g, v_attn_w_out):
    xi, yi, _ = _coords()
    chip = 2 * xi + yi
    D = x.shape[2]
    H = D // HEAD_DIM
    names = ["conv_norm_g", "conv_w_in", "conv_w", "conv_w_out", "attn_norm_g", "attn_w_in", "attn_b_f",
             "attn_q_norm_g", "attn_k_norm_g", "attn_w_out"]
    weights = dict(zip(names, [conv_norm_g, conv_w_in, conv_w, conv_w_out, attn_norm_g, attn_w_in, attn_b_f,
                               attn_q_norm_g, attn_k_norm_g, attn_w_out]))
    m_in = dict(zip(names, [m_conv_norm_g, m_conv_w_in, m_conv_w, m_conv_w_out, m_attn_norm_g, m_attn_w_in,
                            m_attn_b_f, m_attn_q_norm_g, m_attn_k_norm_g, m_attn_w_out]))
    v_in = dict(zip(names, [v_conv_norm_g, v_conv_w_in, v_conv_w, v_conv_w_out, v_attn_norm_g, v_attn_w_in,
                            v_attn_b_f, v_attn_q_norm_g, v_attn_k_norm_g, v_attn_w_out]))
    weights = {k: w[0] for k, w in weights.items()}
    m_in = {k: w[0] for k, w in m_in.items()}
    v_in = {k: w[0] for k, w in v_in.items()}

    big_names = ["conv_w_in", "attn_w_in", "conv_w_out", "attn_w_out"]
    halved = {k: weights[k].astype(BF16).reshape(2, weights[k].shape[0] // 2, weights[k].shape[1]) for k in big_names}
    q = D // 4
    small_w = jnp.concatenate([weights["conv_w"], weights["attn_norm_g"][None, :], jnp.zeros((4, q), F32)], axis=0)
    g_in, g_out, g_small = _all_gather([halved["conv_w_in"], halved["conv_w_out"]], [small_w])
    w_in = g_in.reshape(4, D, D)
    w_out = g_out.reshape(D, D)
    conv_w_full = g_small[:, 0:3, :].transpose(1, 0, 2).reshape(3, D)
    attn_g_full = g_small[:, 3, :].reshape(1, D)
    g1 = weights["conv_norm_g"][None, :]

    conv_acts, (ga_in, ga_out) = _conv_fwd(x[0], g1, w_in, conv_w_full, w_out, [halved["attn_w_in"], halved["attn_w_out"]])
    loss_part, grad_x, grads = _after_conv(conv_acts, x[0], loss_target[0], g1, w_in, conv_w_full, w_out, attn_g_full,
                                           ga_in.reshape(4, D, D + H // 4), weights["attn_b_f"][None, :],
                                           weights["attn_q_norm_g"][None, :], weights["attn_k_norm_g"][None, :],
                                           ga_out.reshape(D, D))

    big = [grads["conv_w_in"].reshape(4, 2, D // 2, D),
           grads["attn_w_in"].reshape(4, 2, D // 2, D + H // 4),
           grads["conv_w_out"].reshape(4, 2, D // 8, D), grads["attn_w_out"].reshape(4, 2, D // 8, D)]
    tail = jnp.concatenate([grads["attn_b_f"], grads["attn_q_norm_g"], grads["attn_k_norm_g"],
                            jnp.reshape(loss_part, (1, 1)), jnp.zeros((1, D - H - 2 * HEAD_DIM - 1), F32)], axis=1)
    small = jnp.concatenate([grads["conv_norm_g"], grads["conv_w"], grads["attn_norm_g"], tail,
                             jnp.zeros((2, D), F32)], axis=0)
    reduced, small_sum = _reduce_gradients(big, small)
    final = {}
    final["conv_norm_g"] = small_sum[0]
    final["conv_w"] = lax.dynamic_slice_in_dim(small_sum[1:4], chip * q, q, axis=1)
    final["attn_norm_g"] = lax.dynamic_slice_in_dim(small_sum[4], chip * q, q, axis=0)
    final["attn_b_f"] = small_sum[5, :H]
    final["attn_q_norm_g"] = small_sum[5, H:H + HEAD_DIM]
    final["attn_k_norm_g"] = small_sum[5, H + HEAD_DIM:H + 2 * HEAD_DIM]
    loss = small_sum[5, H + 2 * HEAD_DIM]

    delta, new_m, new_v = {}, {}, {}
    core = jnp.reshape(lax.axis_index("c"), (1,)).astype(jnp.int32)
    for k, (mine, other) in zip(big_names, reduced):
        final[k], delta[k], new_m[k], new_v[k] = _adamw_halves(weights[k], mine, other, m_in[k], v_in[k], core,
                                                               "adamw_" + k)
    for k in names:
        if k in big_names:
            continue
        shape = weights[k].shape
        as2d = (lambda a: a.reshape(1, -1)) if len(shape) == 1 else (lambda a: a)
        d, m2, v2 = _adamw(as2d(weights[k]), as2d(final[k]), as2d(m_in[k]), as2d(v_in[k]), "adamw_" + k)
        delta[k], new_m[k], new_v[k] = d.reshape(shape), m2.reshape(shape), v2.reshape(shape)
    lead = lambda a: a[None]
    return (loss, grad_x[None], *[lead(final[k]) for k in names], *[lead(delta[k]) for k in names],
            *[lead(new_m[k]) for k in names], *[lead(new_v[k]) for k in names])
```

```python
import functools

import jax
import jax.numpy as jnp
from jax import lax
from jax.experimental import pallas as pl
from jax.experimental.pallas import tpu as pltpu

F32 = jnp.float32
BF16 = jnp.bfloat16
HEAD_DIM = 64
LANES = 128
RMS_EPS = 1e-6
NEG = -1e30
Q_SCALE = 0.125
ROW_TILE = 256
CONV_TILE = 512
ATT_GROUP = 4
SKIP_LOG = 106.0
PLAIN_EXP_MAX = 60.0
TN_ROWS = 2048
VMEM_LIMIT = 56 << 20
ADAM_LR, ADAM_B1, ADAM_B2, ADAM_EPS, ADAM_WD, ADAM_STEP = 0.001, 0.9, 0.999, 1e-08, 0.01, 10
MESH = pl.DeviceIdType.MESH
ANY = pl.BlockSpec(memory_space=pl.ANY)


def _lane():
    return lax.broadcasted_iota(jnp.int32, (1, LANES), 1)


def _split3(x):
    hi = x.astype(BF16).astype(F32)
    r = x - hi
    mid = r.astype(BF16).astype(F32)
    lo = (r - mid).astype(BF16).astype(F32)
    return hi, mid, lo


STAT_STRIDE = 16
ONE_LANE = 3 * STAT_STRIDE


def _pack3(x, lane, one):
    hi, mid, lo = _split3(x)
    packed = hi + pltpu.roll(mid, STAT_STRIDE, 1) + pltpu.roll(lo, 2 * STAT_STRIDE, 1)
    return jnp.where(lane == ONE_LANE, one, packed).astype(BF16)


def _scatter_matrices(H):
    rows = lax.broadcasted_iota(jnp.int32, (LANES, H * LANES), 0)
    cols = lax.broadcasted_iota(jnp.int32, (LANES, H * LANES), 1)
    head, within = cols // LANES, cols % LANES
    extra = within - _aug(head % 2)
    term = (rows < ONE_LANE) & (rows % STAT_STRIDE == head)
    first = ((term & (extra == rows // STAT_STRIDE)) | ((rows == ONE_LANE) & (extra >= 3) & (extra < 6)))
    second = ((term & (extra - 3 == rows // STAT_STRIDE)) | ((rows == ONE_LANE) & (extra >= 0) & (extra < 3)))
    return first.astype(BF16), second.astype(BF16)


def _col(x, lane, idx):
    return jnp.sum(jnp.where(lane == idx, x, 0.0), axis=1, keepdims=True)


def _feat(parity):
    return HEAD_DIM * parity


def _aug(parity):
    return HEAD_DIM * (1 - parity)


def _own(lane, parity):
    return (lane >= _feat(parity)) & (lane < _feat(parity) + HEAD_DIM)


def _head_tile(ref, hd, lane):
    j = hd // 2
    return jnp.where(_own(lane, hd % 2), ref[:, LANES * j:LANES * (j + 1)], 0.0)


def _pair_tile(even, odd, lane):
    return jnp.where(lane < HEAD_DIM, even, odd)


def _sigmoid(x):
    return 0.5 * jnp.tanh(0.5 * x) + 0.5


def _dot(a, b):
    return jnp.dot(a, b, preferred_element_type=F32)


def _dot_nt(a, b):
    return lax.dot_general(a, b, (((1,), (1,)), ((), ())), preferred_element_type=F32)


def _dot_tn(a, b):
    return lax.dot_general(a, b, (((0,), (0,)), ((), ())), preferred_element_type=F32)


def _dot01(tri, x):
    hi, mid, lo = _split3(x)
    return _dot(tri, hi.astype(BF16)) + _dot(tri, mid.astype(BF16)) + _dot(tri, lo.astype(BF16))


def _rms_bwd(dh, x, g):
    inv = lax.rsqrt(jnp.mean(x * x, axis=-1, keepdims=True) + RMS_EPS)
    xh = x * inv
    dxn = dh * g
    dx = inv * (dxn - xh * jnp.mean(dxn * xh, axis=-1, keepdims=True))
    return dx, jnp.sum(dh * xh, axis=0, keepdims=True)


def _head_rms_bwd(dn, t, g, ones):
    sq = t * t
    hi = sq.astype(BF16)
    lo = (sq - hi.astype(F32)).astype(BF16)
    inv = lax.rsqrt((_dot(hi, ones) + _dot(lo, ones)) * (1.0 / HEAD_DIM) + RMS_EPS)
    th = t * inv
    gd = dn * g
    d = inv * (gd - th * (jnp.sum(gd * th, axis=1, keepdims=True) * (1.0 / HEAD_DIM)))
    return d, jnp.sum(dn * th, axis=0, keepdims=True)


def _params(n_grid):
    return pltpu.CompilerParams(dimension_semantics=("arbitrary",) * n_grid, vmem_limit_bytes=VMEM_LIMIT)


def _rows(tm, cols, rev=None):
    if rev is None:
        return pl.BlockSpec((tm, cols), lambda i: (i, 0))
    return pl.BlockSpec((tm, cols), lambda i: (rev - i, 0))


def _whole(shape, buffers=None):
    mode = {} if buffers is None else dict(pipeline_mode=pl.Buffered(buffers))
    return pl.BlockSpec(shape, lambda *_: (0,) * len(shape), **mode)


def _conv_fwd(x, g1, w_in, conv_w, w_out, later):
    S, D = x.shape
    tm = min(CONV_TILE, S)
    sub = min(ROW_TILE, tm)
    steps = S // tm
    n = len(later)

    def body(x_ref, g_ref, win_ref, cw_ref, wout_ref, *rest):
        shard_refs, rest = rest[:n], rest[n:]
        proj_ref, h_ref, yc_ref, y_ref, x1_ref = rest[:5]
        gathered_refs, rest = rest[5:5 + n], rest[5 + n:]
        prev_u = rest[0]
        if n:
            start, forward, finish = _gather_plan(shard_refs, gathered_refs, rest[1], rest[2])
            pl.when(pl.program_id(0) == 0)(start)
            pl.when(pl.program_id(0) == steps // 2)(forward)

        @pl.when(pl.program_id(0) == 0)
        def _():
            prev_u[...] = jnp.zeros((sub, D), F32)

        for r in range(0, tm, sub):
            rows = slice(r, r + sub)
            xv = x_ref[rows, :]
            inv = lax.rsqrt(jnp.mean(xv * xv, axis=-1, keepdims=True) + RMS_EPS)
            h = (xv * inv * g_ref[...]).astype(BF16)
            h_ref[rows, :] = h
            for j in range(4):
                proj_ref[rows, j * D:(j + 1) * D] = _dot(h, win_ref[j])
            u = proj_ref[rows, D:2 * D] * proj_ref[rows, 2 * D:3 * D]
            pu = prev_u[...]
            row = lax.broadcasted_iota(jnp.int32, (sub, 1), 0)
            u1 = jnp.where(row < 1, pltpu.roll(pu, 1, 0), pltpu.roll(u, 1, 0))
            u2 = jnp.where(row < 2, pltpu.roll(pu, 2, 0), pltpu.roll(u, 2, 0))
            prev_u[...] = u
            w = cw_ref[...]
            yc = w[2:3] * u + w[1:2] * u1 + w[0:1] * u2
            yc_ref[rows, :] = yc
            z = proj_ref[rows, 3 * D:4 * D]
            y = (proj_ref[rows, 0:D] * yc * (z * _sigmoid(z))).astype(BF16)
            y_ref[rows, :] = y
            x1_ref[rows, :] = xv + _dot(y, wout_ref[...])

        if n:
            pl.when(pl.program_id(0) == steps - 1)(finish)

    results = pl.pallas_call(
        body, name="conv_fwd", grid=(steps,),
        in_specs=[_rows(tm, D), _whole((1, D)), _whole((4, D, D), 1), _whole((3, D)), _whole((D, D), 1)] + [ANY] * n,
        out_specs=[_rows(tm, 4 * D), _rows(tm, D), _rows(tm, D), _rows(tm, D), _rows(tm, D)] + [ANY] * n,
        out_shape=[jax.ShapeDtypeStruct((S, 4 * D), F32), jax.ShapeDtypeStruct((S, D), BF16),
                   jax.ShapeDtypeStruct((S, D), F32), jax.ShapeDtypeStruct((S, D), BF16),
                   jax.ShapeDtypeStruct((S, D), F32)] + [jax.ShapeDtypeStruct((4,) + a.shape, a.dtype) for a in later],
        scratch_shapes=[pltpu.VMEM((sub, D), F32)] + [pltpu.SemaphoreType.DMA((n, 7))] * (2 if n else 0),
        compiler_params=_params(1),
    )(x, g1, w_in, conv_w, w_out, *later)
    return results[:5], results[5:]


def _conv_bwd(dproj2, wa, wf, x1, g2, dx2, x, g1, w_in, w_out, conv_w, proj, yc, chip_sums):
    S, D = x.shape
    tm = min(ROW_TILE, S)
    sub = min(ROW_TILE, tm)
    steps = S // tm
    last = steps - 1
    n = len(chip_sums)

    def body(dp2_ref, wa_ref, wf_ref, x1_ref, g2_ref, dx2_ref, x_ref, g_ref, win_ref, wout_ref, cw_ref, proj_ref, yc_ref,
             *rest):
        sums_refs, rest = rest[:n], rest[n:]
        dproj_ref, dx_ref, dx1b_ref, dg_ref, dcw_ref, dg2_ref = rest[:6]
        landed_refs, rest = rest[6:6 + n], rest[6 + n:]
        next_d = rest[0]
        if n:
            copies = _chip_sum_copies(sums_refs, landed_refs, rest[1], rest[2])

            @pl.when(pl.program_id(0) == 0)
            def _():
                for cp in copies:
                    cp.start()

        @pl.when(pl.program_id(0) == 0)
        def _():
            dg_ref[...] = jnp.zeros((1, D), F32)
            dcw_ref[...] = jnp.zeros((3, D), F32)
            dg2_ref[...] = jnp.zeros((1, D), F32)
            next_d[...] = jnp.zeros((sub, D), F32)

        for r in range(tm - sub, -1, -sub):
            rows = slice(r, r + sub)
            dh2 = _dot_nt(dp2_ref[rows, 0:4 * D], wa_ref[...]) + _dot_nt(dp2_ref[rows, 4 * D:4 * D + LANES], wf_ref[...])
            dxn2, dg2 = _rms_bwd(dh2, x1_ref[rows, :], g2_ref[...])
            dg2_ref[...] += dg2
            dx1v = dx2_ref[rows, :] + dxn2
            dx1b = dx1v.astype(BF16)
            dx1b_ref[rows, :] = dx1b
            dy = _dot_nt(dx1b, wout_ref[...])
            b = proj_ref[rows, 0:D]
            c = proj_ref[rows, D:2 * D]
            xin = proj_ref[rows, 2 * D:3 * D]
            z = proj_ref[rows, 3 * D:4 * D]
            sg = _sigmoid(z)
            sz = z * sg
            ycv = yc_ref[rows, :]
            d0 = dy * b * sz
            dproj_ref[rows, 0:D] = (dy * ycv * sz).astype(BF16)
            dproj_ref[rows, 3 * D:4 * D] = (dy * b * ycv * (sg * (1.0 + z * (1.0 - sg)))).astype(BF16)
            nd = next_d[...]
            row = lax.broadcasted_iota(jnp.int32, (sub, 1), 0)
            d1 = jnp.where(row >= sub - 1, pltpu.roll(nd, sub - 1, 0), pltpu.roll(d0, sub - 1, 0))
            d2 = jnp.where(row >= sub - 2, pltpu.roll(nd, sub - 2, 0), pltpu.roll(d0, sub - 2, 0))
            next_d[...] = d0
            w = cw_ref[...]
            du = w[2:3] * d0 + w[1:2] * d1 + w[0:1] * d2
            u = c * xin
            dcw_ref[2:3, :] += jnp.sum(d0 * u, axis=0, keepdims=True)
            dcw_ref[1:2, :] += jnp.sum(d1 * u, axis=0, keepdims=True)
            dcw_ref[0:1, :] += jnp.sum(d2 * u, axis=0, keepdims=True)
            dproj_ref[rows, D:2 * D] = (du * xin).astype(BF16)
            dproj_ref[rows, 2 * D:3 * D] = (du * c).astype(BF16)
            dh = _dot_nt(dproj_ref[rows, 0:D], win_ref[0])
            for j in range(1, 4):
                dh = dh + _dot_nt(dproj_ref[rows, j * D:(j + 1) * D], win_ref[j])
            dxn, dg = _rms_bwd(dh, x_ref[rows, :], g_ref[...])
            dx_ref[rows, :] = dx1v + dxn
            dg_ref[...] += dg

        if n:
            @pl.when(pl.program_id(0) == last)
            def _():
                for cp in copies:
                    cp.wait()

    results = pl.pallas_call(
        body, name="conv_bwd", grid=(steps,),
        in_specs=[_rows(tm, 4 * D + LANES, last), _whole((D, 4 * D), 1), _whole((D, LANES), 1), _rows(tm, D, last),
                  _whole((1, D)), _rows(tm, D, last),
                  _rows(tm, D, last), _whole((1, D)), _whole((4, D, D), 1), _whole((D, D), 1),
                  _whole((3, D)), _rows(tm, 4 * D, last), _rows(tm, D, last)] + [ANY] * n,
        out_specs=[_rows(tm, 4 * D, last), _rows(tm, D, last), _rows(tm, D, last), _whole((1, D)), _whole((3, D)),
                   _whole((1, D))] + [ANY] * n,
        out_shape=[jax.ShapeDtypeStruct((S, 4 * D), BF16), jax.ShapeDtypeStruct((S, D), F32),
                   jax.ShapeDtypeStruct((S, D), BF16), jax.ShapeDtypeStruct((1, D), F32),
                   jax.ShapeDtypeStruct((3, D), F32), jax.ShapeDtypeStruct((1, D), F32)]
        + [jax.ShapeDtypeStruct((3,) + a.shape[1:], a.dtype) for a in chip_sums],
        scratch_shapes=[pltpu.VMEM((sub, D), F32)] + [pltpu.SemaphoreType.DMA((n, 3))] * (2 if n else 0),
        compiler_params=_params(1),
    )(dproj2, wa, wf, x1, g2, dx2, x, g1, w_in, w_out, conv_w, proj, yc, *chip_sums)
    return results[:6], results[6:]


def _attn_front(x1, g2, w, wf, bf, gq, gk):
    S, D = x1.shape
    H = D // HEAD_DIM
    tm = min(ROW_TILE, S)
    tri = (lax.broadcasted_iota(jnp.int32, (tm, tm), 1) <= lax.broadcasted_iota(jnp.int32, (tm, tm), 0)).astype(BF16)

    def body(x_ref, g_ref, w_ref, wf_ref, bf_ref, gq_ref, gk_ref, tri_ref, first_ref, second_ref,
             h_ref, qh_ref, kh_ref, z_ref, f_ref, c_ref, rel_ref, qa_ref, ka_ref, va_ref, vt_ref,
             carry, v_s, qraw_ref, kraw_ref):
        @pl.when(pl.program_id(0) == 0)
        def _():
            carry[...] = jnp.zeros((8, LANES), F32)

        xv = x_ref[...]
        inv = lax.rsqrt(jnp.mean(xv * xv, axis=-1, keepdims=True) + RMS_EPS)
        h = (xv * inv * g_ref[...]).astype(BF16)
        h_ref[...] = h
        qraw_ref[...] = _dot(h, w_ref[:, 0:D])
        kraw_ref[...] = _dot(h, w_ref[:, D:2 * D])
        v_s[...] = _dot(h, w_ref[:, 2 * D:3 * D])
        z_ref[...] = _dot(h, w_ref[:, 3 * D:4 * D])
        lane = _lane()
        f = _dot(h, wf_ref[...]) + bf_ref[...]
        f_ref[...] = f
        logf = jnp.where(lane < H, jnp.minimum(f, 0.0) - jnp.log(1.0 + jnp.exp(-jnp.abs(f))), 0.0)
        cs = _dot01(tri_ref[...], logf) + carry[0:1, :]
        c_ref[...] = cs
        carry[...] = jnp.broadcast_to(cs[tm - 1:tm, :], (8, LANES))
        diags = jnp.zeros((tm, LANES), F32)
        for hd in range(H):
            sl = slice(LANES * hd, LANES * (hd + 1))
            a = _aug(hd % 2)
            if hd % 2 == 0:
                qh_ref[hd // 2] = qraw_ref[:, LANES * (hd // 2):LANES * (hd // 2 + 1)]
                kh_ref[hd // 2] = kraw_ref[:, LANES * (hd // 2):LANES * (hd // 2 + 1)]
            qt = _head_tile(qraw_ref, hd, lane)
            qn = qt * lax.rsqrt(jnp.sum(qt * qt, axis=1, keepdims=True) * (1.0 / HEAD_DIM) + RMS_EPS) * gq_ref[...]
            kt = _head_tile(kraw_ref, hd, lane)
            kn = kt * lax.rsqrt(jnp.sum(kt * kt, axis=1, keepdims=True) * (1.0 / HEAD_DIM) + RMS_EPS) * gk_ref[...]
            diags = diags + jnp.where(lane == hd, jnp.sum(qn * kn, axis=1, keepdims=True) * Q_SCALE, 0.0)
            qa_ref[:, sl] = (qn * Q_SCALE).astype(BF16)
            ka_ref[:, sl] = kn.astype(BF16)
            va = jnp.where((lane >= a) & (lane < a + 3), 1.0, _head_tile(v_s, hd, lane))
            va_ref[:, sl] = va.astype(BF16)
            vt_ref[hd] = va.T.astype(BF16)
        rel = cs - diags
        rel_ref[...] = rel
        qa_ref[...] += _dot(_pack3(rel, lane, 1.0), first_ref[...]).astype(BF16)
        ka_ref[...] += _dot(_pack3(-cs, lane, 1.0), second_ref[...]).astype(BF16)

    nb = S // tm
    heads = pl.BlockSpec((H // 2, tm, LANES), lambda i: (0, i, 0))
    return pl.pallas_call(
        body, name="attn_front", grid=(nb,),
        in_specs=[_rows(tm, D), _whole((1, D)), _whole((D, 4 * D)), _whole((D, LANES)), _whole((1, LANES)),
                  _whole((1, LANES)), _whole((1, LANES)), _whole((tm, tm)), _whole((LANES, H * LANES)),
                  _whole((LANES, H * LANES))],
        out_specs=[_rows(tm, D), heads, heads, _rows(tm, D), _rows(tm, LANES), _rows(tm, LANES), _rows(tm, LANES),
                   _rows(tm, H * LANES), _rows(tm, H * LANES), _rows(tm, H * LANES),
                   pl.BlockSpec((H, None, LANES, tm), lambda i: (0, i, 0, 0))],
        out_shape=[jax.ShapeDtypeStruct((S, D), BF16), jax.ShapeDtypeStruct((H // 2, S, LANES), F32),
                   jax.ShapeDtypeStruct((H // 2, S, LANES), F32), jax.ShapeDtypeStruct((S, D), F32),
                   jax.ShapeDtypeStruct((S, LANES), F32), jax.ShapeDtypeStruct((S, LANES), F32),
                   jax.ShapeDtypeStruct((S, LANES), F32),
                   jax.ShapeDtypeStruct((S, H * LANES), BF16), jax.ShapeDtypeStruct((S, H * LANES), BF16),
                   jax.ShapeDtypeStruct((S, H * LANES), BF16), jax.ShapeDtypeStruct((H, nb, LANES, tm), BF16)],
        scratch_shapes=[pltpu.VMEM((8, LANES), F32), pltpu.VMEM((tm, D), F32), pltpu.VMEM((tm, D), F32),
                        pltpu.VMEM((tm, D), F32)],
        compiler_params=_params(1),
    )(x1, g2, w, wf, bf, gq, gk, tri, *_scatter_matrices(H))


def _skip_tables(c, diag, gq, gk, T, G):
    nb = c.shape[0] // T
    bound = HEAD_DIM ** 0.5 * jnp.max(jnp.abs(gq)) * jnp.max(jnp.abs(gk))
    first, last = c[0::T, :], c[T - 1::T, :]
    lowest = jnp.maximum(jnp.min(diag.reshape(nb, T, -1), axis=1), -bound)
    idx = jnp.arange(nb)
    margin = (SKIP_LOG + bound) - lowest
    need = (last[None, :, :] <= first[:, None, :] + margin[:, None, :]) & (idx[None, :, None] < idx[:, None, None])
    need = need | (idx[None, :, None] == idx[:, None, None])
    kstart = jnp.argmax(need, axis=1)
    qend = nb - 1 - jnp.argmax(need[::-1], axis=0)
    kstart = jnp.min(kstart.reshape(2 * nb // G, G // 2, -1), axis=1)
    kstart = kstart - (kstart & 1)
    qend = jnp.max(qend.reshape(2 * nb // G, G // 2, -1), axis=1)
    return kstart.T.astype(jnp.int32), qend.T.astype(jnp.int32), bound


def _attn_fwd(kstart, qa, ka, vt, online_max):
    S = qa.shape[0]
    H = qa.shape[1] // LANES
    nb, T = vt.shape[1], vt.shape[3]
    G = 2 * nb // kstart.shape[1]
    W = G * T

    def finish(acc, shift, o_ref, lse_ref):
        a = _aug(pl.program_id(0) % 2)
        feat = lax.broadcasted_iota(jnp.int32, (LANES, 1), 0)
        l = jnp.sum(jnp.where(feat == a, acc, 0.0), axis=0, keepdims=True)
        o_ref[...] = (acc * (1.0 / l)).T
        lse_ref[...] = shift + jnp.log(l)

    def causal(st):
        return jnp.where(lax.broadcasted_iota(jnp.int32, st.shape, 0) <= lax.broadcasted_iota(jnp.int32, st.shape, 1),
                         st, NEG)

    def fast_body(ks_ref, q_ref, k_ref, vt_ref, o_ref, lse_ref, acc_ref, sa_ref, sb_ref, sc_ref):
        h, g = pl.program_id(0), pl.program_id(1)
        q = q_ref[...]
        acc_ref[...] = jnp.zeros((LANES, W), F32)

        def scores(ki, lo):
            return _dot_nt(k_ref[pl.ds(pl.multiple_of(ki * T, T), 2 * T), :], q[lo * T:, :])

        def weighted(ki, p):
            return _dot(vt_ref[ki], p[:T]) + _dot(vt_ref[ki + 1], p[T:])

        first = ks_ref[h, 2 * g + 1]
        early = jnp.minimum(ks_ref[h, 2 * g], first)

        def narrow(i, carry):
            ki = early + 2 * i
            st = _dot_nt(k_ref[pl.ds(pl.multiple_of(ki * T, T), 2 * T), :], q[:W // 2, :])
            acc_ref[:, :W // 2] += weighted(ki, jnp.exp(st).astype(BF16))
            return carry

        lax.fori_loop(0, (first - early) // 2, narrow, 0)
        steps = (g * G - first) // 2
        sa_ref[...] = scores(first, 0)

        def advance(ki, cur_ref, next_ref):
            p = jnp.exp(cur_ref[...]).astype(BF16)
            next_ref[...] = scores(ki + 2, 0)
            acc_ref[...] += weighted(ki, p)

        def loop(i, carry):
            advance(first + 4 * i, sa_ref, sb_ref)
            advance(first + 4 * i + 2, sb_ref, sa_ref)
            return carry

        lax.fori_loop(0, steps // 2, loop, 0)

        def first_own(pending_ref):
            p = jnp.exp(causal(pending_ref[...])).astype(BF16)
            if G > 2:
                sc_ref[:, :W - 2 * T] = scores(g * G + 2, 2)
            acc_ref[...] += weighted(g * G, p)

        @pl.when(steps % 2 == 1)
        def _():
            advance(g * G - 2, sa_ref, sb_ref)
            first_own(sb_ref)

        @pl.when(steps % 2 == 0)
        def _():
            first_own(sa_ref)

        if G > 2:
            acc_ref[:, 2 * T:] += weighted(g * G + 2, jnp.exp(causal(sc_ref[:, :W - 2 * T])).astype(BF16))
        for j in range(4, G, 2):
            p = jnp.exp(causal(scores(g * G + j, j))).astype(BF16)
            acc_ref[:, j * T:] += weighted(g * G + j, p)
        finish(acc_ref[...], 0.0, o_ref, lse_ref)

    def online_body(ks_ref, q_ref, k_ref, vt_ref, o_ref, lse_ref, acc_ref, m_ref):
        h, g = pl.program_id(0), pl.program_id(1)
        q = q_ref[...]
        m_ref[...] = jnp.full((8, W), NEG, F32)
        acc_ref[...] = jnp.zeros((LANES, W), F32)

        def update(st, vtb, lo):
            m_old = m_ref[0:1, lo:]
            m_new = jnp.maximum(m_old, jnp.max(st, axis=0, keepdims=True))
            p = jnp.exp(st - m_new).astype(BF16)
            acc_ref[:, lo:] = acc_ref[:, lo:] * jnp.exp(m_old - m_new) + _dot(vtb, p)
            m_ref[:, lo:] = jnp.broadcast_to(m_new, (8, W - lo))

        def loop(ki, carry):
            kb = k_ref[pl.ds(pl.multiple_of(ki * T, T), T), :]
            update(_dot_nt(kb, q), vt_ref[ki], 0)
            return carry

        lax.fori_loop(jnp.minimum(ks_ref[h, 2 * g], ks_ref[h, 2 * g + 1]), g * G, loop, 0)
        for j in range(G):
            ki = g * G + j
            kb = k_ref[pl.ds(pl.multiple_of(ki * T, T), T), :]
            update(causal(_dot_nt(kb, q[j * T:, :])), vt_ref[ki], j * T)
        finish(acc_ref[...], m_ref[0:1, :], o_ref, lse_ref)

    return pl.pallas_call(
        online_body if online_max else fast_body, name="attn_fwd_online" if online_max else "attn_fwd",
        grid_spec=pltpu.PrefetchScalarGridSpec(
            num_scalar_prefetch=1, grid=(H, nb // G),
            in_specs=[pl.BlockSpec((W, LANES), lambda h, i, ks: (i, h)),
                      pl.BlockSpec((S, LANES), lambda h, i, ks: (0, h)),
                      pl.BlockSpec((None, nb, LANES, T), lambda h, i, ks: (h, 0, 0, 0))],
            out_specs=[pl.BlockSpec((W, LANES), lambda h, i, ks: (i, h)),
                       pl.BlockSpec((None, 1, W), lambda h, i, ks: (h, 0, i))],
            scratch_shapes=[pltpu.VMEM((LANES, W), F32)] + (
                [pltpu.VMEM((8, W), F32)] if online_max else [pltpu.VMEM((2 * T, W), F32)] * 3)),
        out_shape=[jax.ShapeDtypeStruct((S, H * LANES), F32), jax.ShapeDtypeStruct((H, 1, S), F32)],
        compiler_params=_params(2),
    )(kstart, qa, ka, vt)


def _attn_out(o_aug, lse, rel, z, x1, target, w_out, qa):
    S, D = x1.shape
    H = D // HEAD_DIM
    tm = min(ROW_TILE, S)

    def body(o_ref, z_ref, x1_ref, t_ref, w_ref, q_ref, first_ref, rel_ref, lse_ref,
             dx2_ref, dx2b_ref, o2b_ref, dz_ref, doa_ref, qa2_ref, loss_ref, oc_s, do_s):
        @pl.when(pl.program_id(0) == 0)
        def _():
            loss_ref[...] = jnp.zeros((1, LANES), F32)

        lane = _lane()
        for j in range(H // 2):
            oc_s[:, LANES * j:LANES * (j + 1)] = _pair_tile(o_ref[:, 2 * LANES * j:2 * LANES * j + LANES],
                                                            o_ref[:, 2 * LANES * j + LANES:2 * LANES * (j + 1)], lane)
        oc = oc_s[...]
        zv = z_ref[...]
        sg = _sigmoid(zv)
        sz = zv * sg
        o2 = (oc * sz).astype(BF16)
        o2b_ref[...] = o2
        e = x1_ref[...] + _dot(o2, w_ref[...]) - t_ref[...]
        sq = jnp.sum(jnp.sum(e * e, axis=1, keepdims=True), axis=0, keepdims=True)
        loss_ref[...] += jnp.broadcast_to(sq * (0.5 / D), (1, LANES))
        dx2 = e * (1.0 / D)
        dx2_ref[...] = dx2
        dx2b = dx2.astype(BF16)
        dx2b_ref[...] = dx2b
        do2 = _dot_nt(dx2b, w_ref[...])
        dz_ref[...] = (do2 * oc * (sg * (1.0 + zv * (1.0 - sg)))).astype(BF16)
        do_s[...] = do2 * sz
        deltas = jnp.zeros((tm, LANES), F32)
        for hd in range(H):
            dt = _head_tile(do_s, hd, lane)
            delta = jnp.sum(dt * _head_tile(oc_s, hd, lane), axis=1, keepdims=True)
            deltas = deltas + jnp.where(lane == hd, delta, 0.0)
            doa_ref[:, LANES * hd:LANES * (hd + 1)] = dt.astype(BF16)
        doa_ref[...] += _dot(_pack3(-deltas, lane, 0.0), first_ref[...]).astype(BF16)
        lse = jnp.concatenate([lse_ref[...], jnp.zeros((LANES - H, tm), F32)], axis=0).T
        rq = rel_ref[...] - lse
        tile_lane = lax.broadcasted_iota(jnp.int32, (1, H * LANES), 1)
        extra = tile_lane % LANES - _aug((tile_lane // LANES) % 2)
        kept = jnp.where((extra >= 0) & (extra < 3), jnp.zeros((), BF16), q_ref[...])
        qa2_ref[...] = kept + _dot(_pack3(rq, lane, 0.0), first_ref[...]).astype(BF16)

    return pl.pallas_call(
        body, name="attn_out", grid=(S // tm,),
        in_specs=[_rows(tm, H * LANES), _rows(tm, D), _rows(tm, D), _rows(tm, D), _whole((D, D)),
                  _rows(tm, H * LANES), _whole((LANES, H * LANES)), _rows(tm, LANES),
                  pl.BlockSpec((H, tm), lambda i: (0, i))],
        out_specs=[_rows(tm, D), _rows(tm, D), _rows(tm, D), _rows(tm, D), _rows(tm, H * LANES),
                   _rows(tm, H * LANES), _whole((1, LANES))],
        out_shape=[jax.ShapeDtypeStruct((S, D), F32), jax.ShapeDtypeStruct((S, D), BF16),
                   jax.ShapeDtypeStruct((S, D), BF16), jax.ShapeDtypeStruct((S, D), BF16),
                   jax.ShapeDtypeStruct((S, H * LANES), BF16), jax.ShapeDtypeStruct((S, H * LANES), BF16),
                   jax.ShapeDtypeStruct((1, LANES), F32)],
        scratch_shapes=[pltpu.VMEM((tm, D), F32), pltpu.VMEM((tm, D), F32)],
        compiler_params=_params(1),
    )(o_aug, z, x1, target, w_out, qa, _scatter_matrices(H)[0], rel, lse)


def _attn_bwd(qend, qa2, doa, ka, va, T):
    S = qa2.shape[0]
    H = qa2.shape[1] // LANES
    nb = S // T
    G = 2 * nb // qend.shape[1]
    W = G * T

    def body(qe_ref, q_ref, do_ref, k_ref, v_ref, dq_ref, dk_ref, dv_ref, dkt_acc, dvt_acc):
        h, g = pl.program_id(0), pl.program_id(1)

        @pl.when(g == 0)
        def _():
            dq_ref[...] = jnp.zeros((S, LANES), F32)

        kb = k_ref[...]
        vb = v_ref[...]
        dkt_acc[...] = jnp.zeros((LANES, W), F32)
        dvt_acc[...] = jnp.zeros((LANES, W), F32)

        def step(qi, c0, c1, masked):
            rows = pl.ds(pl.multiple_of(qi * T, T), 2 * T)
            qb = q_ref[rows, :]
            dob = do_ref[rows, :]
            s = _dot_nt(qb, kb[c0:c1])
            if masked:
                query = lax.broadcasted_iota(jnp.int32, s.shape, 0) + (c1 - 2 * T)
                s = jnp.where(lax.broadcasted_iota(jnp.int32, s.shape, 1) <= query, s, NEG)
            p = jnp.exp(s)
            ds = (p * _dot_nt(dob, vb[c0:c1])).astype(BF16)
            dvt_acc[:, c0:c1] += _dot(dob.astype(F32).T.astype(BF16), p.astype(BF16))
            dkt_acc[:, c0:c1] += _dot(qb.astype(F32).T.astype(BF16), ds)
            dq_ref[rows, :] += _dot(ds, kb[c0:c1])

        for m in range(G // 2):
            step(g * G + 2 * m, 0, (m + 1) * 2 * T, True)
        first = g * G + G
        n_all = jnp.maximum((qe_ref[h, 2 * g] - first + 2) // 2, 0)
        second = first + 2 * n_all

        def all_keys(i, carry):
            step(first + 2 * i, 0, W, False)
            return carry

        def late_keys(i, carry):
            step(second + 2 * i, W // 2, W, False)
            return carry

        lax.fori_loop(0, n_all, all_keys, 0)
        lax.fori_loop(0, (qe_ref[h, 2 * g + 1] - second + 2) // 2, late_keys, 0)
        dk_ref[...] = dkt_acc[...].T
        dv_ref[...] = dvt_acc[...].T.astype(BF16)

    heads = pl.BlockSpec((None, W, LANES), lambda h, i, qe: (h, i, 0))
    return pl.pallas_call(
        body, name="attn_bwd",
        grid_spec=pltpu.PrefetchScalarGridSpec(
            num_scalar_prefetch=1, grid=(H, nb // G),
            in_specs=[pl.BlockSpec((S, LANES), lambda h, i, qe: (0, h)), pl.BlockSpec((S, LANES), lambda h, i, qe: (0, h)),
                      pl.BlockSpec((W, LANES), lambda h, i, qe: (i, h)), pl.BlockSpec((W, LANES), lambda h, i, qe: (i, h))],
            out_specs=[pl.BlockSpec((None, S, LANES), lambda h, i, qe: (h, 0, 0)), heads, heads],
            scratch_shapes=[pltpu.VMEM((LANES, W), F32), pltpu.VMEM((LANES, W), F32)]),
        out_shape=[jax.ShapeDtypeStruct((H, S, LANES), F32), jax.ShapeDtypeStruct((H, S, LANES), F32),
                   jax.ShapeDtypeStruct((H, S, LANES), BF16)],
        compiler_params=_params(2),
    )(qend, qa2, doa, ka, va)


def _attn_proj_bwd(dqt, dka, dva, qraw, kraw, dz, f, gq, gk):
    S, D = dz.shape
    H = D // HEAD_DIM
    tm = min(CONV_TILE, S)
    last = S // tm - 1
    tri = (lax.broadcasted_iota(jnp.int32, (tm, tm), 1) >= lax.broadcasted_iota(jnp.int32, (tm, tm), 0)).astype(BF16)

    def body(dq_ref, dk_ref, dv_ref, q_ref, k_ref, dz_ref, f_ref, gq_ref, gk_ref, tri_ref, ones_ref,
             dproj_ref, small_ref, carry, pairs):
        @pl.when(pl.program_id(0) == 0)
        def _():
            small_ref[...] = jnp.zeros((8, LANES), F32)
            carry[...] = jnp.zeros((8, LANES), F32)

        lane = _lane()

        def head_pair(j, acc):
            dcs, dgq, dgk = acc
            dq2, dk2 = [], []
            q_pair, k_pair = q_ref[j], k_ref[j]
            for parity in (0, 1):
                hd = 2 * j + parity
                own, a = _own(lane, parity), _aug(parity)
                dqf = dq_ref[hd]
                dqn = jnp.where(own, dqf * Q_SCALE, 0.0)
                d, dg = _head_rms_bwd(dqn, jnp.where(own, q_pair, 0.0), gq_ref[...], ones_ref[...])
                dq2.append(d)
                dgq = dgq + dg
                dkt = dk_ref[hd]
                dcs = dcs + jnp.where(lane == hd, _col(dqf, lane, a) - _col(dkt, lane, a + 3), 0.0)
                d, dg = _head_rms_bwd(jnp.where(own, dkt, 0.0), jnp.where(own, k_pair, 0.0), gk_ref[...], ones_ref[...])
                dk2.append(d)
                dgk = dgk + dg
            pairs[0, j] = _pair_tile(*dq2, lane).astype(BF16)
            pairs[1, j] = _pair_tile(*dk2, lane).astype(BF16)
            pairs[2, j] = _pair_tile(dv_ref[2 * j], dv_ref[2 * j + 1], lane)
            return dcs, dgq, dgk

        zero = jnp.zeros((1, LANES), F32)
        dcs, dgq, dgk = lax.fori_loop(0, H // 2, head_pair, (jnp.zeros((tm, LANES), F32), zero, zero))
        for part in range(3):
            for j in range(H // 2):
                dproj_ref[:, part * D + LANES * j:part * D + LANES * (j + 1)] = pairs[part, j]
        dproj_ref[:, 3 * D:4 * D] = dz_ref[...]
        dlogf = _dot01(tri_ref[...], dcs) + carry[0:1, :]
        carry[...] = jnp.broadcast_to(dlogf[0:1, :], (8, LANES))
        df = dlogf * (1.0 / (1.0 + jnp.exp(f_ref[...])))
        dproj_ref[:, 4 * D:4 * D + LANES] = df.astype(BF16)
        small_ref[0:1, :] += jnp.sum(df, axis=0, keepdims=True)
        small_ref[1:2, :] += dgq
        small_ref[2:3, :] += dgk

    W = 4 * D + LANES
    heads = pl.BlockSpec((H, tm, LANES), lambda i: (0, last - i, 0))
    head_pairs = pl.BlockSpec((H // 2, tm, LANES), lambda i: (0, last - i, 0))
    return pl.pallas_call(
        body, name="attn_proj_bwd", grid=(S // tm,),
        in_specs=[heads, heads, heads, head_pairs, head_pairs,
                  _rows(tm, D, last), _rows(tm, LANES, last), _whole((1, LANES)), _whole((1, LANES)),
                  _whole((tm, tm)), _whole((LANES, LANES))],
        out_specs=[_rows(tm, W, last), _whole((8, LANES))],
        out_shape=[jax.ShapeDtypeStruct((S, W), BF16), jax.ShapeDtypeStruct((8, LANES), F32)],
        scratch_shapes=[pltpu.VMEM((8, LANES), F32), pltpu.VMEM((3, H // 2, tm, LANES), BF16)],
        compiler_params=_params(1),
    )(dqt, dka, dva, qraw, kraw, dz, f, gq, gk, tri, jnp.ones((LANES, LANES), BF16))


def _matmul_tn(a, b, col0, n, tn, name, stacked=False):
    S, M = a.shape
    ts = min(TN_ROWS, S)
    off = col0 // tn

    def body(a_ref, b_ref, o_ref):
        @pl.when(pl.program_id(1) == 0)
        def _():
            o_ref[...] = jnp.zeros((M, tn), F32)

        o_ref[...] += _dot_tn(a_ref[...], b_ref[...])

    if stacked:
        out_spec, out_shape = pl.BlockSpec((None, M, tn), lambda j, s: (j, 0, 0)), (n // tn, M, tn)
    else:
        out_spec, out_shape = pl.BlockSpec((M, tn), lambda j, s: (0, j)), (M, n)
    return pl.pallas_call(
        body, name=name, grid=(n // tn, S // ts),
        in_specs=[pl.BlockSpec((ts, M), lambda j, s: (s, 0)), pl.BlockSpec((ts, tn), lambda j, s: (s, off + j))],
        out_specs=out_spec, out_shape=jax.ShapeDtypeStruct(out_shape, F32),
        compiler_params=_params(2),
    )(a, b)


def _adam_update(gv, w_ref, m_ref, v_ref, d_ref, m2_ref, v2_ref):
    m2 = ADAM_B1 * m_ref[...] + (1.0 - ADAM_B1) * gv
    v2 = ADAM_B2 * v_ref[...] + (1.0 - ADAM_B2) * (gv * gv)
    m2_ref[...] = m2
    v2_ref[...] = v2
    m_hat = m2 / (1.0 - ADAM_B1 ** ADAM_STEP)
    v_hat = v2 / (1.0 - ADAM_B2 ** ADAM_STEP)
    d_ref[...] = -ADAM_LR * (m_hat / (jnp.sqrt(v_hat) + ADAM_EPS) + ADAM_WD * w_ref[...])


def _adamw(w, g, m, v, name):
    r, c = w.shape
    tr = ROW_TILE if r % ROW_TILE == 0 else r

    def body(w_ref, g_ref, m_ref, v_ref, d_ref, m2_ref, v2_ref):
        _adam_update(g_ref[...], w_ref, m_ref, v_ref, d_ref, m2_ref, v2_ref)

    spec = _rows(tr, c)
    return pl.pallas_call(
        body, name=name, grid=(r // tr,), in_specs=[spec] * 4, out_specs=[spec] * 3,
        out_shape=[jax.ShapeDtypeStruct((r, c), F32)] * 3, compiler_params=_params(1),
    )(w, g, m, v)


def _adamw_halves(w, mine, other, m, v, core, name):
    r, c = mine.shape
    tr = ROW_TILE if r % ROW_TILE == 0 else r
    per = r // tr

    def body(core_ref, w_ref, mine_ref, other_ref, m_ref, v_ref, g_ref, d_ref, m2_ref, v2_ref):
        gv = jnp.where(pl.program_id(0) // per == core_ref[0], mine_ref[...], other_ref[...])
        g_ref[...] = gv
        _adam_update(gv, w_ref, m_ref, v_ref, d_ref, m2_ref, v2_ref)

    full = pl.BlockSpec((tr, c), lambda i, core: (i, 0))
    half = pl.BlockSpec((tr, c), lambda i, core: (i % per, 0))
    return pl.pallas_call(
        body, name=name,
        grid_spec=pltpu.PrefetchScalarGridSpec(num_scalar_prefetch=1, grid=(2 * per,),
                                               in_specs=[full, half, half, full, full], out_specs=[full] * 4),
        out_shape=[jax.ShapeDtypeStruct((2 * r, c), F32)] * 4, compiler_params=_params(1),
    )(core, w, mine, other, m, v)


def _after_conv(conv_acts, x, target, g1, w_in, conv_w, w_out, g2, wa_in, b_f, gq, gk, wa_out, reduce_early=None):
    S, D = x.shape
    H = D // HEAD_DIM
    ws = wa_in.shape[2]
    w_qkvz = jnp.concatenate([wa_in[0], wa_in[1], wa_in[2], wa_in[3][:, :4 * D - 3 * ws]], axis=1)
    wf = jnp.pad(wa_in[3][:, 4 * D - 3 * ws:], ((0, 0), (0, LANES - H)))
    bf = jnp.pad(b_f, ((0, 0), (0, LANES - H)))
    gq128 = jnp.concatenate([gq, gq], axis=1)
    gk128 = jnp.concatenate([gk, gk], axis=1)

    proj, h1, yc, y, x1 = conv_acts
    h2, qraw, kraw, z, f, c, rel, qa, ka, va, vt = _attn_front(x1, g2, w_qkvz, wf, bf, gq128, gk128)
    T = vt.shape[3]
    kstart, qend, bound = _skip_tables(c[:, :H], (c - rel)[:, :H], gq, gk, T, min(ATT_GROUP, S // T))
    o_aug, lse = lax.cond(2.0 * bound <= PLAIN_EXP_MAX, functools.partial(_attn_fwd, online_max=False),
                          functools.partial(_attn_fwd, online_max=True), kstart, qa, ka, vt)
    dx2, dx2b, o2b, dz, doa, qa2, loss = _attn_out(o_aug, lse.reshape(H, S), rel, z, x1, target, wa_out, qa)
    dqt, dka, dva = _attn_bwd(qend, qa2, doa, ka, va, T)
    dproj2, small = _attn_proj_bwd(dqt, dka, dva, qraw, kraw, dz, f, gq128, gk128)

    tn = min(1024, D)
    dwa_out = _matmul_tn(o2b, dx2b, 0, D, tn, "dw_attn_out")
    dw_main = _matmul_tn(h2, dproj2, 0, 4 * D, 2 * tn, "dw_attn_in")
    dw_f = _matmul_tn(h2, dproj2, 4 * D, LANES, LANES, "dw_attn_f")
    dwa_in = jnp.stack([dw_main[:, 0:ws], dw_main[:, ws:2 * ws], dw_main[:, 2 * ws:3 * ws],
                        jnp.concatenate([dw_main[:, 3 * ws:], dw_f[:, :H]], axis=1)])
    early = None if reduce_early is None else reduce_early([dwa_in, dwa_out])
    (dproj1, dx, dx1b, dg1, dcw, dg2), arrived = _conv_bwd(dproj2, w_qkvz, wf, x1, g2, dx2, x, g1, w_in, w_out, conv_w,
                                                          proj, yc, [] if early is None else early[1])
    dw_out = _matmul_tn(y, dx1b, 0, D, tn, "dw_conv_out")
    dw_in = _matmul_tn(h1, dproj1, 0, 4 * D, D, "dw_conv_in", stacked=True)
    grads = dict(conv_norm_g=dg1, conv_w_in=dw_in, conv_w=dcw, conv_w_out=dw_out, attn_norm_g=dg2,
                 attn_w_in=dwa_in, attn_b_f=small[0:1, :H],
                 attn_q_norm_g=small[1:2, :HEAD_DIM] + small[1:2, HEAD_DIM:],
                 attn_k_norm_g=small[2:3, :HEAD_DIM] + small[2:3, HEAD_DIM:], attn_w_out=dwa_out)
    return loss[0, 0], dx, grads, (None if early is None else (early[0], list(arrived)))


def _coords():
    return lax.axis_index("x"), lax.axis_index("y"), lax.axis_index("c")


def _at(ref, idx):
    return ref.at[idx] if idx else ref


def _other_chips(x, y):
    return [(1 - x, y), (x, 1 - y), (1 - x, 1 - y)]


def _gather_plan(src, out, send, recv):
    x, y, c = _coords()
    mine = 2 * x + y
    sibling = (x, y, 1 - c)
    others = [(a, k, 2 * px + py, (px, py)) for a in range(len(src)) for k, (px, py) in enumerate(_other_chips(x, y))]

    def copy(a, k, chip, half, to, source=None):
        dst = out[a].at[chip, half]
        return pltpu.make_async_remote_copy(src_ref=dst if source is None else source, dst_ref=dst,
                                            send_sem=send.at[a, k], recv_sem=recv.at[a, k],
                                            device_id=to, device_id_type=MESH)

    first = [copy(a, k, mine, c, (*chip, c), source=src[a].at[c]) for a, k, _, chip in others]
    own = [pltpu.make_async_remote_copy(src_ref=src[a], dst_ref=out[a].at[mine], send_sem=send.at[a, 6],
                                        recv_sem=recv.at[a, 6], device_id=sibling, device_id_type=MESH)
           for a in range(len(src))]
    passed = [copy(a, 3 + k, slot, c, sibling) for a, k, slot, _ in others]

    def start():
        for cp in first + own:
            cp.start()

    def forward():
        for (a, k, slot, _), cp in zip(others, passed):
            copy(a, k, slot, c, (x, y, c)).wait_recv()
            cp.start()

    def finish():
        for a, k, slot, _ in others:
            copy(a, 3 + k, slot, 1 - c, (x, y, c)).wait_recv()
        for cp in own:
            cp.wait_recv()
        for cp in first + passed + own:
            cp.wait_send()

    return start, forward, finish


def _all_gather(halved, whole):
    nh, nw = len(halved), len(whole)

    def body(*refs):
        src_h, src_w = refs[:nh], refs[nh:nh + nw]
        out_h, out_w = refs[nh + nw:2 * nh + nw], refs[2 * nh + nw:2 * (nh + nw)]
        send_h, recv_h, send_w, recv_w = refs[2 * (nh + nw):]
        x, y, c = _coords()
        mine = 2 * x + y
        chips = _other_chips(x, y)

        def copy_w(a, k, chip, to):
            return pltpu.make_async_remote_copy(src_ref=src_w[a], dst_ref=out_w[a].at[chip],
                                                send_sem=send_w.at[a, k], recv_sem=recv_w.at[a, k],
                                                device_id=to, device_id_type=MESH)

        start, forward, finish = _gather_plan(src_h, out_h, send_h, recv_h)
        small = [copy_w(a, k, mine, (*chip, c)) for a in range(nw) for k, chip in enumerate(chips)]
        small += [copy_w(a, 3, mine, (x, y, 1 - c)) for a in range(nw)]
        start()
        for cp in small:
            cp.start()
        forward()
        finish()
        for a in range(nw):
            for k, (px, py) in enumerate(chips):
                copy_w(a, k, 2 * px + py, (x, y, c)).wait_recv()
            copy_w(a, 3, mine, (x, y, c)).wait_recv()
        for cp in small:
            cp.wait_send()

    out_shape = [jax.ShapeDtypeStruct((4,) + a.shape, a.dtype) for a in list(halved) + list(whole)]
    return pl.pallas_call(
        body, name="gather_weights", in_specs=[ANY] * (nh + nw), out_specs=[ANY] * (nh + nw), out_shape=out_shape,
        scratch_shapes=[pltpu.SemaphoreType.DMA((nh, 7)), pltpu.SemaphoreType.DMA((nh, 7)),
                        pltpu.SemaphoreType.DMA((nw, 4)), pltpu.SemaphoreType.DMA((nw, 4))],
    )(*halved, *whole)


def _exchange(name, srcs, lands, copies, local_copies):
    ns, nl, n, nloc = len(srcs), len(lands), len(copies), len(local_copies)

    def body(*refs):
        src, land = refs[:ns], refs[ns:ns + nl]
        send, recv, local_sem = refs[ns + nl:]
        me = _coords()
        started = []
        for k, (si, s_at, li, l_at, ci) in enumerate(local_copies):
            cp = pltpu.make_async_copy(_at(src[si], s_at(*me)), _at(land[li], l_at(*me)), local_sem.at[k])
            cp.start()
            started.append(cp)
        remote = []
        for k, (si, s_at, li, l_at, peer) in enumerate(copies):
            cp = pltpu.make_async_remote_copy(src_ref=_at(src[si], s_at(*me)), dst_ref=_at(land[li], l_at(*me)),
                                              send_sem=send.at[k], recv_sem=recv.at[k],
                                              device_id=peer(*me), device_id_type=MESH)
            cp.start()
            remote.append(cp)
        for cp in remote:
            cp.wait()
        for cp in started:
            cp.wait()

    return pl.pallas_call(
        body, name=name, in_specs=[ANY] * ns, out_specs=[ANY] * nl, out_shape=list(lands),
        scratch_shapes=[pltpu.SemaphoreType.DMA((n,)), pltpu.SemaphoreType.DMA((n,)),
                        pltpu.SemaphoreType.DMA((max(nloc, 1),))],
    )(*srcs)


def _add_pairs(a, b, core, name):
    _, r, cols = b.shape
    tr = ROW_TILE if r % ROW_TILE == 0 else r

    def body(core_ref, a_ref, b_ref, o_ref, ob_ref):
        s = a_ref[...] + b_ref[...]
        o_ref[...] = s
        ob_ref[...] = s.astype(BF16)

    spec = pl.BlockSpec((None, tr, cols), lambda j, i, core: (j, i, 0))
    return pl.pallas_call(
        body, name=name,
        grid_spec=pltpu.PrefetchScalarGridSpec(
            num_scalar_prefetch=1, grid=(4, r // tr),
            in_specs=[pl.BlockSpec((None, None, tr, cols), lambda j, i, core: (j, core[0], i, 0)), spec],
            out_specs=[spec, spec]),
        out_shape=[jax.ShapeDtypeStruct(b.shape, F32), jax.ShapeDtypeStruct(b.shape, BF16)],
        compiler_params=_params(2),
    )(core, a, b)


def _sum_chips(own, landed, name):
    _, r, cols = landed.shape
    tr = ROW_TILE if r % ROW_TILE == 0 else r

    def body(own_ref, land_ref, o_ref):
        acc = own_ref[...]
        for j in range(3):
            acc = acc + land_ref[j].astype(F32)
        o_ref[...] = acc

    return pl.pallas_call(
        body, name=name, grid=(r // tr,),
        in_specs=[_rows(tr, cols), pl.BlockSpec((3, tr, cols), lambda i: (0, i, 0))], out_specs=_rows(tr, cols),
        out_shape=jax.ShapeDtypeStruct((r, cols), F32), compiler_params=_params(1),
    )(own, landed)


def _sum_devices(landed, name):
    def body(l_ref, o_ref):
        acc = l_ref[0]
        for j in range(1, 8):
            acc = acc + l_ref[j]
        o_ref[...] = acc

    return pl.pallas_call(body, name=name, out_shape=jax.ShapeDtypeStruct(landed.shape[1:], F32))(landed)


CHIP_FLIPS = [(1, 0), (0, 1), (1, 1)]


def _flip(fx, fy, fc):
    return lambda x, y, c: (x ^ fx, y ^ fy, c ^ fc)


def _sum_cores(big, small, tag):
    nb = len(big)
    c = lax.axis_index("c")
    copies = [(a, (lambda j: lambda x, y, c: (j, 1 - c))(j), a, (lambda j: lambda x, y, c: (j,))(j), _flip(0, 0, 1))
              for a in range(nb) for j in range(4)]
    lands = [jax.ShapeDtypeStruct((4,) + g.shape[2:], F32) for g in big]
    srcs, local = list(big), []
    if small is not None:
        flips = [(fx, fy, fc) for fx in (0, 1) for fy in (0, 1) for fc in (0, 1) if fx or fy or fc]
        copies += [(nb, lambda x, y, c: (), nb, lambda x, y, c: (4 * x + 2 * y + c,), _flip(*f)) for f in flips]
        lands.append(jax.ShapeDtypeStruct((8,) + small.shape, F32))
        local = [(nb, lambda x, y, c: (), nb, lambda x, y, c: (4 * x + 2 * y + c,), None)]
        srcs.append(small)
    landed = _exchange("swap_halves_" + tag, srcs, lands, copies, local)
    small_sum = None if small is None else _sum_devices(landed[nb], "sum_small")
    core = jnp.reshape(c, (1,)).astype(jnp.int32)
    sums = [_add_pairs(big[a], landed[a], core, f"add_cores_{tag}_{a}") for a in range(nb)]
    return [s for s, _ in sums], [sb for _, sb in sums], small_sum


def _chip_sum_copies(sums, landed, send, recv):
    x, y, c = _coords()
    return [pltpu.make_async_remote_copy(src_ref=sums[a].at[2 * (x ^ fx) + (y ^ fy)], dst_ref=landed[a].at[k],
                                         send_sem=send.at[a, k], recv_sem=recv.at[a, k],
                                         device_id=(x ^ fx, y ^ fy, c), device_id_type=MESH)
            for a in range(len(sums)) for k, (fx, fy) in enumerate(CHIP_FLIPS)]


def _send_chip_sums(chip_bf16):
    n = len(chip_bf16)

    def body(*refs):
        copies = _chip_sum_copies(refs[:n], refs[n:2 * n], refs[2 * n], refs[2 * n + 1])
        for cp in copies:
            cp.start()
        for cp in copies:
            cp.wait()

    return pl.pallas_call(
        body, name="send_chip_sums", in_specs=[ANY] * n, out_specs=[ANY] * n,
        out_shape=[jax.ShapeDtypeStruct((3,) + g.shape[1:], BF16) for g in chip_bf16],
        scratch_shapes=[pltpu.SemaphoreType.DMA((n, 3)), pltpu.SemaphoreType.DMA((n, 3))],
    )(*chip_bf16)


def _sum_chips_and_share(chip_f32, landed):
    x, y, _ = _coords()
    totals = [_sum_chips(lax.dynamic_index_in_dim(f, 2 * x + y, axis=0, keepdims=False), l, f"sum_chips_{a}")
              for a, (f, l) in enumerate(zip(chip_f32, landed))]
    copies = [(a, lambda x, y, c: (), a, lambda x, y, c: (), _flip(0, 0, 1)) for a in range(len(totals))]
    lands = [jax.ShapeDtypeStruct(t.shape, F32) for t in totals]
    return list(zip(totals, _exchange("swap_sums", totals, lands, copies, [])))


def kernel(x, conv_norm_g, conv_w_in, conv_w, conv_w_out, attn_norm_g, attn_w_in, attn_b_f, attn_q_norm_g, attn_k_norm_g, attn_w_out, loss_target, m_conv_norm_g, m_conv_w_in, m_conv_w, m_conv_w_out, m_attn_norm_g, m_attn_w_in, m_attn_b_f, m_attn_q_norm_g, m_attn_k_norm_g, m_attn_w_out, v_conv_norm_g, v_conv_w_in, v_conv_w, v_conv_w_out, v_attn_norm_g, v_attn_w_in, v_attn_b_f, v_attn_q_norm_g, v_attn_k_norm_g, v_attn_w_out):
    xi, yi, _ = _coords()
    chip = 2 * xi + yi
    D = x.shape[2]
    H = D // HEAD_DIM
    names = ["conv_norm_g", "conv_w_in", "conv_w", "conv_w_out", "attn_norm_g", "attn_w_in", "attn_b_f",
             "attn_q_norm_g", "attn_k_norm_g", "attn_w_out"]
    weights = dict(zip(names, [conv_norm_g, conv_w_in, conv_w, conv_w_out, attn_norm_g, attn_w_in, attn_b_f,
                               attn_q_norm_g, attn_k_norm_g, attn_w_out]))
    m_in = dict(zip(names, [m_conv_norm_g, m_conv_w_in, m_conv_w, m_conv_w_out, m_attn_norm_g, m_attn_w_in,
                            m_attn_b_f, m_attn_q_norm_g, m_attn_k_norm_g, m_attn_w_out]))
    v_in = dict(zip(names, [v_conv_norm_g, v_conv_w_in, v_conv_w, v_conv_w_out, v_attn_norm_g, v_attn_w_in,
                            v_attn_b_f, v_attn_q_norm_g, v_attn_k_norm_g, v_attn_w_out]))
    weights = {k: w[0] for k, w in weights.items()}
    m_in = {k: w[0] for k, w in m_in.items()}
    v_in = {k: w[0] for k, w in v_in.items()}

    big_names = ["conv_w_in", "attn_w_in", "conv_w_out", "attn_w_out"]
    halved = {k: weights[k].astype(BF16).reshape(2, weights[k].shape[0] // 2, weights[k].shape[1]) for k in big_names}
    q = D // 4
    small_w = jnp.concatenate([weights["conv_w"], weights["attn_norm_g"][None, :], jnp.zeros((4, q), F32)], axis=0)
    g_in, g_out, g_small = _all_gather([halved["conv_w_in"], halved["conv_w_out"]], [small_w])
    w_in = g_in.reshape(4, D, D)
    w_out = g_out.reshape(D, D)
    conv_w_full = g_small[:, 0:3, :].transpose(1, 0, 2).reshape(3, D)
    attn_g_full = g_small[:, 3, :].reshape(1, D)
    g1 = weights["conv_norm_g"][None, :]

    conv_acts, (ga_in, ga_out) = _conv_fwd(x[0], g1, w_in, conv_w_full, w_out, [halved["attn_w_in"], halved["attn_w_out"]])
    def attn_core_sums(dw):
        f32, bf16, _ = _sum_cores([dw[0].reshape(4, 2, D // 2, D + H // 4), dw[1].reshape(4, 2, D // 8, D)], None, "attn")
        return f32, bf16

    loss_part, grad_x, grads, (attn_f32, attn_arrived) = _after_conv(
        conv_acts, x[0], loss_target[0], g1, w_in, conv_w_full, w_out, attn_g_full, ga_in.reshape(4, D, D + H // 4),
        weights["attn_b_f"][None, :], weights["attn_q_norm_g"][None, :], weights["attn_k_norm_g"][None, :],
        ga_out.reshape(D, D), reduce_early=attn_core_sums)

    tail = jnp.concatenate([grads["attn_b_f"], grads["attn_q_norm_g"], grads["attn_k_norm_g"],
                            jnp.reshape(loss_part, (1, 1)), jnp.zeros((1, D - H - 2 * HEAD_DIM - 1), F32)], axis=1)
    small = jnp.concatenate([grads["conv_norm_g"], grads["conv_w"], grads["attn_norm_g"], tail,
                             jnp.zeros((2, D), F32)], axis=0)
    conv_f32, conv_bf16, small_sum = _sum_cores([grads["conv_w_in"].reshape(4, 2, D // 2, D),
                                                 grads["conv_w_out"].reshape(4, 2, D // 8, D)], small, "conv")
    big_names = ["attn_w_in", "attn_w_out", "conv_w_in", "conv_w_out"]
    reduced = _sum_chips_and_share(attn_f32 + conv_f32, attn_arrived + list(_send_chip_sums(conv_bf16)))
    final = {}
    final["conv_norm_g"] = small_sum[0]
    final["conv_w"] = lax.dynamic_slice_in_dim(small_sum[1:4], chip * q, q, axis=1)
    final["attn_norm_g"] = lax.dynamic_slice_in_dim(small_sum[4], chip * q, q, axis=0)
    final["attn_b_f"] = small_sum[5, :H]
    final["attn_q_norm_g"] = small_sum[5, H:H + HEAD_DIM]
    final["attn_k_norm_g"] = small_sum[5, H + HEAD_DIM:H + 2 * HEAD_DIM]
    loss = small_sum[5, H + 2 * HEAD_DIM]

    delta, new_m, new_v = {}, {}, {}
    core = jnp.reshape(lax.axis_index("c"), (1,)).astype(jnp.int32)
    for k, (mine, other) in zip(big_names, reduced):
        final[k], delta[k], new_m[k], new_v[k] = _adamw_halves(weights[k], mine, other, m_in[k], v_in[k], core,
                                                               "adamw_" + k)
    for k in names:
        if k in big_names:
            continue
        shape = weights[k].shape
        as2d = (lambda a: a.reshape(1, -1)) if len(shape) == 1 else (lambda a: a)
        d, m2, v2 = _adamw(as2d(weights[k]), as2d(final[k]), as2d(m_in[k]), as2d(v_in[k]), "adamw_" + k)
        delta[k], new_m[k], new_v[k] = d.reshape(shape), m2.reshape(shape), v2.reshape(shape)
    lead = lambda a: a[None]
    return (loss, grad_x[None], *[lead(final[k]) for k in names], *[lead(delta[k]) for k in names],
            *[lead(new_m[k]) for k in names], *[lead(new_v[k]) for k in names])
```

```python
import functools

import jax
import jax.numpy as jnp
from jax import lax
from jax.experimental import pallas as pl
from jax.experimental.pallas import tpu as pltpu

F32 = jnp.float32
BF16 = jnp.bfloat16
HEAD_DIM = 64
LANES = 128
RMS_EPS = 1e-6
NEG = -1e30
Q_SCALE = 0.125
ROW_TILE = 256
CONV_TILE = 512
ATT_GROUP = 4
SKIP_LOG = 106.0
PLAIN_EXP_MAX = 60.0
TN_ROWS = 2048
VMEM_LIMIT = 56 << 20
ADAM_LR, ADAM_B1, ADAM_B2, ADAM_EPS, ADAM_WD, ADAM_STEP = 0.001, 0.9, 0.999, 1e-08, 0.01, 10
MESH = pl.DeviceIdType.MESH
ANY = pl.BlockSpec(memory_space=pl.ANY)


def _lane():
    return lax.broadcasted_iota(jnp.int32, (1, LANES), 1)


def _split3(x):
    hi = x.astype(BF16).astype(F32)
    r = x - hi
    mid = r.astype(BF16).astype(F32)
    lo = (r - mid).astype(BF16).astype(F32)
    return hi, mid, lo


STAT_STRIDE = 16
ONE_LANE = 3 * STAT_STRIDE


def _pack3(x, lane, one):
    hi, mid, lo = _split3(x)
    packed = hi + pltpu.roll(mid, STAT_STRIDE, 1) + pltpu.roll(lo, 2 * STAT_STRIDE, 1)
    return jnp.where(lane == ONE_LANE, one, packed).astype(BF16)


def _scatter_matrices(H):
    rows = lax.broadcasted_iota(jnp.int32, (LANES, H * LANES), 0)
    cols = lax.broadcasted_iota(jnp.int32, (LANES, H * LANES), 1)
    head, within = cols // LANES, cols % LANES
    extra = within - _aug(head % 2)
    term = (rows < ONE_LANE) & (rows % STAT_STRIDE == head)
    first = ((term & (extra == rows // STAT_STRIDE)) | ((rows == ONE_LANE) & (extra >= 3) & (extra < 6)))
    second = ((term & (extra - 3 == rows // STAT_STRIDE)) | ((rows == ONE_LANE) & (extra >= 0) & (extra < 3)))
    return first.astype(BF16), second.astype(BF16)


def _col(x, lane, idx):
    return jnp.sum(jnp.where(lane == idx, x, 0.0), axis=1, keepdims=True)


def _feat(parity):
    return HEAD_DIM * parity


def _aug(parity):
    return HEAD_DIM * (1 - parity)


def _own(lane, parity):
    return (lane >= _feat(parity)) & (lane < _feat(parity) + HEAD_DIM)


def _head_tile(ref, hd, lane):
    j = hd // 2
    return jnp.where(_own(lane, hd % 2), ref[:, LANES * j:LANES * (j + 1)], 0.0)


def _pair_tile(even, odd, lane):
    return jnp.where(lane < HEAD_DIM, even, odd)


def _sigmoid(x):
    return 0.5 * jnp.tanh(0.5 * x) + 0.5


def _dot(a, b):
    return jnp.dot(a, b, preferred_element_type=F32)


def _dot_nt(a, b):
    return lax.dot_general(a, b, (((1,), (1,)), ((), ())), preferred_element_type=F32)


def _dot_tn(a, b):
    return lax.dot_general(a, b, (((0,), (0,)), ((), ())), preferred_element_type=F32)


def _dot01(tri, x):
    hi, mid, lo = _split3(x)
    return _dot(tri, hi.astype(BF16)) + _dot(tri, mid.astype(BF16)) + _dot(tri, lo.astype(BF16))


def _rms_bwd(dh, x, g):
    inv = lax.rsqrt(jnp.mean(x * x, axis=-1, keepdims=True) + RMS_EPS)
    xh = x * inv
    dxn = dh * g
    dx = inv * (dxn - xh * jnp.mean(dxn * xh, axis=-1, keepdims=True))
    return dx, jnp.sum(dh * xh, axis=0, keepdims=True)


def _head_rms_bwd(dn, t, g, ones):
    sq = t * t
    hi = sq.astype(BF16)
    lo = (sq - hi.astype(F32)).astype(BF16)
    inv = lax.rsqrt((_dot(hi, ones) + _dot(lo, ones)) * (1.0 / HEAD_DIM) + RMS_EPS)
    th = t * inv
    gd = dn * g
    d = inv * (gd - th * (jnp.sum(gd * th, axis=1, keepdims=True) * (1.0 / HEAD_DIM)))
    return d, jnp.sum(dn * th, axis=0, keepdims=True)


def _params(n_grid):
    return pltpu.CompilerParams(dimension_semantics=("arbitrary",) * n_grid, vmem_limit_bytes=VMEM_LIMIT)


def _rows(tm, cols, rev=None):
    if rev is None:
        return pl.BlockSpec((tm, cols), lambda i: (i, 0))
    return pl.BlockSpec((tm, cols), lambda i: (rev - i, 0))


def _whole(shape, buffers=None):
    mode = {} if buffers is None else dict(pipeline_mode=pl.Buffered(buffers))
    return pl.BlockSpec(shape, lambda *_: (0,) * len(shape), **mode)


def _conv_fwd(x, g1, w_in, conv_w, w_out, later):
    S, D = x.shape
    tm = min(CONV_TILE, S)
    sub = min(ROW_TILE, tm)
    steps = S // tm
    n = len(later)

    def body(x_ref, g_ref, win_ref, cw_ref, wout_ref, *rest):
        shard_refs, rest = rest[:n], rest[n:]
        proj_ref, h_ref, yc_ref, y_ref, x1_ref = rest[:5]
        gathered_refs, rest = rest[5:5 + n], rest[5 + n:]
        prev_u = rest[0]
        if n:
            start, forward, finish = _gather_plan(shard_refs, gathered_refs, rest[1], rest[2])
            pl.when(pl.program_id(0) == 0)(start)
            pl.when(pl.program_id(0) == steps // 2)(forward)

        @pl.when(pl.program_id(0) == 0)
        def _():
            prev_u[...] = jnp.zeros((sub, D), F32)

        for r in range(0, tm, sub):
            rows = slice(r, r + sub)
            xv = x_ref[rows, :]
            inv = lax.rsqrt(jnp.mean(xv * xv, axis=-1, keepdims=True) + RMS_EPS)
            h = (xv * inv * g_ref[...]).astype(BF16)
            h_ref[rows, :] = h
            for j in range(4):
                proj_ref[rows, j * D:(j + 1) * D] = _dot(h, win_ref[j])
            u = proj_ref[rows, D:2 * D] * proj_ref[rows, 2 * D:3 * D]
            pu = prev_u[...]
            row = lax.broadcasted_iota(jnp.int32, (sub, 1), 0)
            u1 = jnp.where(row < 1, pltpu.roll(pu, 1, 0), pltpu.roll(u, 1, 0))
            u2 = jnp.where(row < 2, pltpu.roll(pu, 2, 0), pltpu.roll(u, 2, 0))
            prev_u[...] = u
            w = cw_ref[...]
            yc = w[2:3] * u + w[1:2] * u1 + w[0:1] * u2
            yc_ref[rows, :] = yc
            z = proj_ref[rows, 3 * D:4 * D]
            y = (proj_ref[rows, 0:D] * yc * (z * _sigmoid(z))).astype(BF16)
            y_ref[rows, :] = y
            x1_ref[rows, :] = xv + _dot(y, wout_ref[...])

        if n:
            pl.when(pl.program_id(0) == steps - 1)(finish)

    results = pl.pallas_call(
        body, name="conv_fwd", grid=(steps,),
        in_specs=[_rows(tm, D), _whole((1, D)), _whole((4, D, D), 1), _whole((3, D)), _whole((D, D), 1)] + [ANY] * n,
        out_specs=[_rows(tm, 4 * D), _rows(tm, D), _rows(tm, D), _rows(tm, D), _rows(tm, D)] + [ANY] * n,
        out_shape=[jax.ShapeDtypeStruct((S, 4 * D), F32), jax.ShapeDtypeStruct((S, D), BF16),
                   jax.ShapeDtypeStruct((S, D), F32), jax.ShapeDtypeStruct((S, D), BF16),
                   jax.ShapeDtypeStruct((S, D), F32)] + [jax.ShapeDtypeStruct((4,) + a.shape, a.dtype) for a in later],
        scratch_shapes=[pltpu.VMEM((sub, D), F32)] + [pltpu.SemaphoreType.DMA((n, 7))] * (2 if n else 0),
        compiler_params=_params(1),
    )(x, g1, w_in, conv_w, w_out, *later)
    return results[:5], results[5:]


def _conv_bwd(dproj2, wa, wf, x1, g2, dx2, x, g1, w_in, w_out, conv_w, proj, yc):
    S, D = x.shape
    tm = min(ROW_TILE, S)
    sub = min(ROW_TILE, tm)
    last = S // tm - 1

    def body(dp2_ref, wa_ref, wf_ref, x1_ref, g2_ref, dx2_ref, x_ref, g_ref, win_ref, wout_ref, cw_ref, proj_ref, yc_ref,
             dproj_ref, dx_ref, dx1b_ref, dg_ref, dcw_ref, dg2_ref, next_d):
        @pl.when(pl.program_id(0) == 0)
        def _():
            dg_ref[...] = jnp.zeros((1, D), F32)
            dcw_ref[...] = jnp.zeros((3, D), F32)
            dg2_ref[...] = jnp.zeros((1, D), F32)
            next_d[...] = jnp.zeros((sub, D), F32)

        for r in range(tm - sub, -1, -sub):
            rows = slice(r, r + sub)
            dh2 = _dot_nt(dp2_ref[rows, 0:4 * D], wa_ref[...]) + _dot_nt(dp2_ref[rows, 4 * D:4 * D + LANES], wf_ref[...])
            dxn2, dg2 = _rms_bwd(dh2, x1_ref[rows, :], g2_ref[...])
            dg2_ref[...] += dg2
            dx1v = dx2_ref[rows, :] + dxn2
            dx1b = dx1v.astype(BF16)
            dx1b_ref[rows, :] = dx1b
            dy = _dot_nt(dx1b, wout_ref[...])
            b = proj_ref[rows, 0:D]
            c = proj_ref[rows, D:2 * D]
            xin = proj_ref[rows, 2 * D:3 * D]
            z = proj_ref[rows, 3 * D:4 * D]
            sg = _sigmoid(z)
            sz = z * sg
            ycv = yc_ref[rows, :]
            d0 = dy * b * sz
            dproj_ref[rows, 0:D] = (dy * ycv * sz).astype(BF16)
            dproj_ref[rows, 3 * D:4 * D] = (dy * b * ycv * (sg * (1.0 + z * (1.0 - sg)))).astype(BF16)
            nd = next_d[...]
            row = lax.broadcasted_iota(jnp.int32, (sub, 1), 0)
            d1 = jnp.where(row >= sub - 1, pltpu.roll(nd, sub - 1, 0), pltpu.roll(d0, sub - 1, 0))
            d2 = jnp.where(row >= sub - 2, pltpu.roll(nd, sub - 2, 0), pltpu.roll(d0, sub - 2, 0))
            next_d[...] = d0
            w = cw_ref[...]
            du = w[2:3] * d0 + w[1:2] * d1 + w[0:1] * d2
            u = c * xin
            dcw_ref[2:3, :] += jnp.sum(d0 * u, axis=0, keepdims=True)
            dcw_ref[1:2, :] += jnp.sum(d1 * u, axis=0, keepdims=True)
            dcw_ref[0:1, :] += jnp.sum(d2 * u, axis=0, keepdims=True)
            dproj_ref[rows, D:2 * D] = (du * xin).astype(BF16)
            dproj_ref[rows, 2 * D:3 * D] = (du * c).astype(BF16)
            dh = _dot_nt(dproj_ref[rows, 0:D], win_ref[0])
            for j in range(1, 4):
                dh = dh + _dot_nt(dproj_ref[rows, j * D:(j + 1) * D], win_ref[j])
            dxn, dg = _rms_bwd(dh, x_ref[rows, :], g_ref[...])
            dx_ref[rows, :] = dx1v + dxn
            dg_ref[...] += dg

    return pl.pallas_call(
        body, name="conv_bwd", grid=(S // tm,),
        in_specs=[_rows(tm, 4 * D + LANES, last), _whole((D, 4 * D), 1), _whole((D, LANES), 1), _rows(tm, D, last),
                  _whole((1, D)), _rows(tm, D, last),
                  _rows(tm, D, last), _whole((1, D)), _whole((4, D, D), 1), _whole((D, D), 1),
                  _whole((3, D)), _rows(tm, 4 * D, last), _rows(tm, D, last)],
        out_specs=[_rows(tm, 4 * D, last), _rows(tm, D, last), _rows(tm, D, last), _whole((1, D)), _whole((3, D)),
                   _whole((1, D))],
        out_shape=[jax.ShapeDtypeStruct((S, 4 * D), BF16), jax.ShapeDtypeStruct((S, D), F32),
                   jax.ShapeDtypeStruct((S, D), BF16), jax.ShapeDtypeStruct((1, D), F32),
                   jax.ShapeDtypeStruct((3, D), F32), jax.ShapeDtypeStruct((1, D), F32)],
        scratch_shapes=[pltpu.VMEM((sub, D), F32)],
        compiler_params=_params(1),
    )(dproj2, wa, wf, x1, g2, dx2, x, g1, w_in, w_out, conv_w, proj, yc)


def _attn_front(x1, g2, w, wf, bf, gq, gk):
    S, D = x1.shape
    H = D // HEAD_DIM
    tm = min(ROW_TILE, S)
    tri = (lax.broadcasted_iota(jnp.int32, (tm, tm), 1) <= lax.broadcasted_iota(jnp.int32, (tm, tm), 0)).astype(BF16)

    def body(x_ref, g_ref, w_ref, wf_ref, bf_ref, gq_ref, gk_ref, tri_ref, first_ref, second_ref,
             h_ref, qh_ref, kh_ref, z_ref, f_ref, c_ref, rel_ref, qa_ref, ka_ref, va_ref, vt_ref,
             carry, v_s, qraw_ref, kraw_ref):
        @pl.when(pl.program_id(0) == 0)
        def _():
            carry[...] = jnp.zeros((8, LANES), F32)

        xv = x_ref[...]
        inv = lax.rsqrt(jnp.mean(xv * xv, axis=-1, keepdims=True) + RMS_EPS)
        h = (xv * inv * g_ref[...]).astype(BF16)
        h_ref[...] = h
        qraw_ref[...] = _dot(h, w_ref[:, 0:D])
        kraw_ref[...] = _dot(h, w_ref[:, D:2 * D])
        v_s[...] = _dot(h, w_ref[:, 2 * D:3 * D])
        z_ref[...] = _dot(h, w_ref[:, 3 * D:4 * D])
        lane = _lane()
        f = _dot(h, wf_ref[...]) + bf_ref[...]
        f_ref[...] = f
        logf = jnp.where(lane < H, jnp.minimum(f, 0.0) - jnp.log(1.0 + jnp.exp(-jnp.abs(f))), 0.0)
        cs = _dot01(tri_ref[...], logf) + carry[0:1, :]
        c_ref[...] = cs
        carry[...] = jnp.broadcast_to(cs[tm - 1:tm, :], (8, LANES))
        diags = jnp.zeros((tm, LANES), F32)
        for hd in range(H):
            sl = slice(LANES * hd, LANES * (hd + 1))
            a = _aug(hd % 2)
            if hd % 2 == 0:
                qh_ref[hd // 2] = qraw_ref[:, LANES * (hd // 2):LANES * (hd // 2 + 1)]
                kh_ref[hd // 2] = kraw_ref[:, LANES * (hd // 2):LANES * (hd // 2 + 1)]
            qt = _head_tile(qraw_ref, hd, lane)
            qn = qt * lax.rsqrt(jnp.sum(qt * qt, axis=1, keepdims=True) * (1.0 / HEAD_DIM) + RMS_EPS) * gq_ref[...]
            kt = _head_tile(kraw_ref, hd, lane)
            kn = kt * lax.rsqrt(jnp.sum(kt * kt, axis=1, keepdims=True) * (1.0 / HEAD_DIM) + RMS_EPS) * gk_ref[...]
            diags = diags + jnp.where(lane == hd, jnp.sum(qn * kn, axis=1, keepdims=True) * Q_SCALE, 0.0)
            qa_ref[:, sl] = (qn * Q_SCALE).astype(BF16)
            ka_ref[:, sl] = kn.astype(BF16)
            va = jnp.where((lane >= a) & (lane < a + 3), 1.0, _head_tile(v_s, hd, lane))
            va_ref[:, sl] = va.astype(BF16)
            vt_ref[hd] = va.T.astype(BF16)
        rel = cs - diags
        rel_ref[...] = rel
        qa_ref[...] += _dot(_pack3(rel, lane, 1.0), first_ref[...]).astype(BF16)
        ka_ref[...] += _dot(_pack3(-cs, lane, 1.0), second_ref[...]).astype(BF16)

    nb = S // tm
    heads = pl.BlockSpec((H // 2, tm, LANES), lambda i: (0, i, 0))
    return pl.pallas_call(
        body, name="attn_front", grid=(nb,),
        in_specs=[_rows(tm, D), _whole((1, D)), _whole((D, 4 * D)), _whole((D, LANES)), _whole((1, LANES)),
                  _whole((1, LANES)), _whole((1, LANES)), _whole((tm, tm)), _whole((LANES, H * LANES)),
                  _whole((LANES, H * LANES))],
        out_specs=[_rows(tm, D), heads, heads, _rows(tm, D), _rows(tm, LANES), _rows(tm, LANES), _rows(tm, LANES),
                   _rows(tm, H * LANES), _rows(tm, H * LANES), _rows(tm, H * LANES),
                   pl.BlockSpec((H, None, LANES, tm), lambda i: (0, i, 0, 0))],
        out_shape=[jax.ShapeDtypeStruct((S, D), BF16), jax.ShapeDtypeStruct((H // 2, S, LANES), F32),
                   jax.ShapeDtypeStruct((H // 2, S, LANES), F32), jax.ShapeDtypeStruct((S, D), F32),
                   jax.ShapeDtypeStruct((S, LANES), F32), jax.ShapeDtypeStruct((S, LANES), F32),
                   jax.ShapeDtypeStruct((S, LANES), F32),
                   jax.ShapeDtypeStruct((S, H * LANES), BF16), jax.ShapeDtypeStruct((S, H * LANES), BF16),
                   jax.ShapeDtypeStruct((S, H * LANES), BF16), jax.ShapeDtypeStruct((H, nb, LANES, tm), BF16)],
        scratch_shapes=[pltpu.VMEM((8, LANES), F32), pltpu.VMEM((tm, D), F32), pltpu.VMEM((tm, D), F32),
                        pltpu.VMEM((tm, D), F32)],
        compiler_params=_params(1),
    )(x1, g2, w, wf, bf, gq, gk, tri, *_scatter_matrices(H))


def _skip_tables(c, diag, gq, gk, T, G):
    nb = c.shape[0] // T
    bound = HEAD_DIM ** 0.5 * jnp.max(jnp.abs(gq)) * jnp.max(jnp.abs(gk))
    first, last = c[0::T, :], c[T - 1::T, :]
    lowest = jnp.maximum(jnp.min(diag.reshape(nb, T, -1), axis=1), -bound)
    idx = jnp.arange(nb)
    margin = (SKIP_LOG + bound) - lowest
    need = (last[None, :, :] <= first[:, None, :] + margin[:, None, :]) & (idx[None, :, None] < idx[:, None, None])
    need = need | (idx[None, :, None] == idx[:, None, None])
    kstart = jnp.argmax(need, axis=1)
    qend = nb - 1 - jnp.argmax(need[::-1], axis=0)
    kstart = jnp.min(kstart.reshape(2 * nb // G, G // 2, -1), axis=1)
    kstart = kstart - (kstart & 1)
    qend = jnp.max(qend.reshape(2 * nb // G, G // 2, -1), axis=1)
    return kstart.T.astype(jnp.int32), qend.T.astype(jnp.int32), bound


def _attn_fwd(kstart, qa, ka, vt, online_max):
    S = qa.shape[0]
    H = qa.shape[1] // LANES
    nb, T = vt.shape[1], vt.shape[3]
    G = 2 * nb // kstart.shape[1]
    W = G * T

    def finish(acc, shift, o_ref, lse_ref):
        a = _aug(pl.program_id(0) % 2)
        feat = lax.broadcasted_iota(jnp.int32, (LANES, 1), 0)
        l = jnp.sum(jnp.where(feat == a, acc, 0.0), axis=0, keepdims=True)
        o_ref[...] = (acc * (1.0 / l)).T
        lse_ref[...] = shift + jnp.log(l)

    def causal(st):
        return jnp.where(lax.broadcasted_iota(jnp.int32, st.shape, 0) <= lax.broadcasted_iota(jnp.int32, st.shape, 1),
                         st, NEG)

    def fast_body(ks_ref, q_ref, k_ref, vt_ref, o_ref, lse_ref, acc_ref, sa_ref, sb_ref, sc_ref):
        h, g = pl.program_id(0), pl.program_id(1)
        q = q_ref[...]
        acc_ref[...] = jnp.zeros((LANES, W), F32)

        def scores(ki, lo):
            return _dot_nt(k_ref[pl.ds(pl.multiple_of(ki * T, T), 2 * T), :], q[lo * T:, :])

        def weighted(ki, p):
            return _dot(vt_ref[ki], p[:T]) + _dot(vt_ref[ki + 1], p[T:])

        first = ks_ref[h, 2 * g + 1]
        early = jnp.minimum(ks_ref[h, 2 * g], first)

        def narrow(i, carry):
            ki = early + 2 * i
            st = _dot_nt(k_ref[pl.ds(pl.multiple_of(ki * T, T), 2 * T), :], q[:W // 2, :])
            acc_ref[:, :W // 2] += weighted(ki, jnp.exp(st).astype(BF16))
            return carry

        lax.fori_loop(0, (first - early) // 2, narrow, 0)
        steps = (g * G - first) // 2
        sa_ref[...] = scores(first, 0)

        def advance(ki, cur_ref, next_ref):
            p = jnp.exp(cur_ref[...]).astype(BF16)
            next_ref[...] = scores(ki + 2, 0)
            acc_ref[...] += weighted(ki, p)

        def loop(i, carry):
            advance(first + 4 * i, sa_ref, sb_ref)
            advance(first + 4 * i + 2, sb_ref, sa_ref)
            return carry

        lax.fori_loop(0, steps // 2, loop, 0)

        def first_own(pending_ref):
            p = jnp.exp(causal(pending_ref[...])).astype(BF16)
            if G > 2:
                sc_ref[:, :W - 2 * T] = scores(g * G + 2, 2)
            acc_ref[...] += weighted(g * G, p)

        @pl.when(steps % 2 == 1)
        def _():
            advance(g * G - 2, sa_ref, sb_ref)
            first_own(sb_ref)

        @pl.when(steps % 2 == 0)
        def _():
            first_own(sa_ref)

        if G > 2:
            acc_ref[:, 2 * T:] += weighted(g * G + 2, jnp.exp(causal(sc_ref[:, :W - 2 * T])).astype(BF16))
        for j in range(4, G, 2):
            p = jnp.exp(causal(scores(g * G + j, j))).astype(BF16)
            acc_ref[:, j * T:] += weighted(g * G + j, p)
        finish(acc_ref[...], 0.0, o_ref, lse_ref)

    def online_body(ks_ref, q_ref, k_ref, vt_ref, o_ref, lse_ref, acc_ref, m_ref):
        h, g = pl.program_id(0), pl.program_id(1)
        q = q_ref[...]
        m_ref[...] = jnp.full((8, W), NEG, F32)
        acc_ref[...] = jnp.zeros((LANES, W), F32)

        def update(st, vtb, lo):
            m_old = m_ref[0:1, lo:]
            m_new = jnp.maximum(m_old, jnp.max(st, axis=0, keepdims=True))
            p = jnp.exp(st - m_new).astype(BF16)
            acc_ref[:, lo:] = acc_ref[:, lo:] * jnp.exp(m_old - m_new) + _dot(vtb, p)
            m_ref[:, lo:] = jnp.broadcast_to(m_new, (8, W - lo))

        def loop(ki, carry):
            kb = k_ref[pl.ds(pl.multiple_of(ki * T, T), T), :]
            update(_dot_nt(kb, q), vt_ref[ki], 0)
            return carry

        lax.fori_loop(jnp.minimum(ks_ref[h, 2 * g], ks_ref[h, 2 * g + 1]), g * G, loop, 0)
        for j in range(G):
            ki = g * G + j
            kb = k_ref[pl.ds(pl.multiple_of(ki * T, T), T), :]
            update(causal(_dot_nt(kb, q[j * T:, :])), vt_ref[ki], j * T)
        finish(acc_ref[...], m_ref[0:1, :], o_ref, lse_ref)

    return pl.pallas_call(
        online_body if online_max else fast_body, name="attn_fwd_online" if online_max else "attn_fwd",
        grid_spec=pltpu.PrefetchScalarGridSpec(
            num_scalar_prefetch=1, grid=(H, nb // G),
            in_specs=[pl.BlockSpec((W, LANES), lambda h, i, ks: (i, h)),
                      pl.BlockSpec((S, LANES), lambda h, i, ks: (0, h)),
                      pl.BlockSpec((None, nb, LANES, T), lambda h, i, ks: (h, 0, 0, 0))],
            out_specs=[pl.BlockSpec((W, LANES), lambda h, i, ks: (i, h)),
                       pl.BlockSpec((None, 1, W), lambda h, i, ks: (h, 0, i))],
            scratch_shapes=[pltpu.VMEM((LANES, W), F32)] + (
                [pltpu.VMEM((8, W), F32)] if online_max else [pltpu.VMEM((2 * T, W), F32)] * 3)),
        out_shape=[jax.ShapeDtypeStruct((S, H * LANES), F32), jax.ShapeDtypeStruct((H, 1, S), F32)],
        compiler_params=_params(2),
    )(kstart, qa, ka, vt)


def _attn_out(o_aug, lse, rel, z, x1, target, w_out, qa):
    S, D = x1.shape
    H = D // HEAD_DIM
    tm = min(ROW_TILE, S)

    def body(o_ref, z_ref, x1_ref, t_ref, w_ref, q_ref, first_ref, rel_ref, lse_ref,
             dx2_ref, dx2b_ref, o2b_ref, dz_ref, doa_ref, qa2_ref, loss_ref, oc_s, do_s):
        @pl.when(pl.program_id(0) == 0)
        def _():
            loss_ref[...] = jnp.zeros((1, LANES), F32)

        lane = _lane()
        for j in range(H // 2):
            oc_s[:, LANES * j:LANES * (j + 1)] = _pair_tile(o_ref[:, 2 * LANES * j:2 * LANES * j + LANES],
                                                            o_ref[:, 2 * LANES * j + LANES:2 * LANES * (j + 1)], lane)
        oc = oc_s[...]
        zv = z_ref[...]
        sg = _sigmoid(zv)
        sz = zv * sg
        o2 = (oc * sz).astype(BF16)
        o2b_ref[...] = o2
        e = x1_ref[...] + _dot(o2, w_ref[...]) - t_ref[...]
        sq = jnp.sum(jnp.sum(e * e, axis=1, keepdims=True), axis=0, keepdims=True)
        loss_ref[...] += jnp.broadcast_to(sq * (0.5 / D), (1, LANES))
        dx2 = e * (1.0 / D)
        dx2_ref[...] = dx2
        dx2b = dx2.astype(BF16)
        dx2b_ref[...] = dx2b
        do2 = _dot_nt(dx2b, w_ref[...])
        dz_ref[...] = (do2 * oc * (sg * (1.0 + zv * (1.0 - sg)))).astype(BF16)
        do_s[...] = do2 * sz
        deltas = jnp.zeros((tm, LANES), F32)
        for hd in range(H):
            dt = _head_tile(do_s, hd, lane)
            delta = jnp.sum(dt * _head_tile(oc_s, hd, lane), axis=1, keepdims=True)
            deltas = deltas + jnp.where(lane == hd, delta, 0.0)
            doa_ref[:, LANES * hd:LANES * (hd + 1)] = dt.astype(BF16)
        doa_ref[...] += _dot(_pack3(-deltas, lane, 0.0), first_ref[...]).astype(BF16)
        lse = jnp.concatenate([lse_ref[...], jnp.zeros((LANES - H, tm), F32)], axis=0).T
        rq = rel_ref[...] - lse
        tile_lane = lax.broadcasted_iota(jnp.int32, (1, H * LANES), 1)
        extra = tile_lane % LANES - _aug((tile_lane // LANES) % 2)
        kept = jnp.where((extra >= 0) & (extra < 3), jnp.zeros((), BF16), q_ref[...])
        qa2_ref[...] = kept + _dot(_pack3(rq, lane, 0.0), first_ref[...]).astype(BF16)

    return pl.pallas_call(
        body, name="attn_out", grid=(S // tm,),
        in_specs=[_rows(tm, H * LANES), _rows(tm, D), _rows(tm, D), _rows(tm, D), _whole((D, D)),
                  _rows(tm, H * LANES), _whole((LANES, H * LANES)), _rows(tm, LANES),
                  pl.BlockSpec((H, tm), lambda i: (0, i))],
        out_specs=[_rows(tm, D), _rows(tm, D), _rows(tm, D), _rows(tm, D), _rows(tm, H * LANES),
                   _rows(tm, H * LANES), _whole((1, LANES))],
        out_shape=[jax.ShapeDtypeStruct((S, D), F32), jax.ShapeDtypeStruct((S, D), BF16),
                   jax.ShapeDtypeStruct((S, D), BF16), jax.ShapeDtypeStruct((S, D), BF16),
                   jax.ShapeDtypeStruct((S, H * LANES), BF16), jax.ShapeDtypeStruct((S, H * LANES), BF16),
                   jax.ShapeDtypeStruct((1, LANES), F32)],
        scratch_shapes=[pltpu.VMEM((tm, D), F32), pltpu.VMEM((tm, D), F32)],
        compiler_params=_params(1),
    )(o_aug, z, x1, target, w_out, qa, _scatter_matrices(H)[0], rel, lse)


def _attn_bwd(qend, qa2, doa, ka, va, T):
    S = qa2.shape[0]
    H = qa2.shape[1] // LANES
    nb = S // T
    G = 2 * nb // qend.shape[1]
    W = G * T

    def body(qe_ref, q_ref, do_ref, k_ref, v_ref, dq_ref, dk_ref, dv_ref, dkt_acc, dvt_acc):
        h, g = pl.program_id(0), pl.program_id(1)

        @pl.when(g == 0)
        def _():
            dq_ref[...] = jnp.zeros((S, LANES), F32)

        kb = k_ref[...]
        vb = v_ref[...]
        dkt_acc[...] = jnp.zeros((LANES, W), F32)
        dvt_acc[...] = jnp.zeros((LANES, W), F32)

        def step(qi, c0, c1, masked):
            rows = pl.ds(pl.multiple_of(qi * T, T), 2 * T)
            qb = q_ref[rows, :]
            dob = do_ref[rows, :]
            s = _dot_nt(qb, kb[c0:c1])
            if masked:
                query = lax.broadcasted_iota(jnp.int32, s.shape, 0) + (c1 - 2 * T)
                s = jnp.where(lax.broadcasted_iota(jnp.int32, s.shape, 1) <= query, s, NEG)
            p = jnp.exp(s)
            ds = (p * _dot_nt(dob, vb[c0:c1])).astype(BF16)
            dvt_acc[:, c0:c1] += _dot(dob.astype(F32).T.astype(BF16), p.astype(BF16))
            dkt_acc[:, c0:c1] += _dot(qb.astype(F32).T.astype(BF16), ds)
            dq_ref[rows, :] += _dot(ds, kb[c0:c1])

        for m in range(G // 2):
            step(g * G + 2 * m, 0, (m + 1) * 2 * T, True)
        first = g * G + G
        n_all = jnp.maximum((qe_ref[h, 2 * g] - first + 2) // 2, 0)
        second = first + 2 * n_all

        def all_keys(i, carry):
            step(first + 2 * i, 0, W, False)
            return carry

        def late_keys(i, carry):
            step(second + 2 * i, W // 2, W, False)
            return carry

        lax.fori_loop(0, n_all, all_keys, 0)
        lax.fori_loop(0, (qe_ref[h, 2 * g + 1] - second + 2) // 2, late_keys, 0)
        dk_ref[...] = dkt_acc[...].T
        dv_ref[...] = dvt_acc[...].T.astype(BF16)

    heads = pl.BlockSpec((None, W, LANES), lambda h, i, qe: (h, i, 0))
    return pl.pallas_call(
        body, name="attn_bwd",
        grid_spec=pltpu.PrefetchScalarGridSpec(
            num_scalar_prefetch=1, grid=(H, nb // G),
            in_specs=[pl.BlockSpec((S, LANES), lambda h, i, qe: (0, h)), pl.BlockSpec((S, LANES), lambda h, i, qe: (0, h)),
                      pl.BlockSpec((W, LANES), lambda h, i, qe: (i, h)), pl.BlockSpec((W, LANES), lambda h, i, qe: (i, h))],
            out_specs=[pl.BlockSpec((None, S, LANES), lambda h, i, qe: (h, 0, 0)), heads, heads],
            scratch_shapes=[pltpu.VMEM((LANES, W), F32), pltpu.VMEM((LANES, W), F32)]),
        out_shape=[jax.ShapeDtypeStruct((H, S, LANES), F32), jax.ShapeDtypeStruct((H, S, LANES), F32),
                   jax.ShapeDtypeStruct((H, S, LANES), BF16)],
        compiler_params=_params(2),
    )(qend, qa2, doa, ka, va)


def _attn_proj_bwd(dqt, dka, dva, qraw, kraw, dz, f, gq, gk):
    S, D = dz.shape
    H = D // HEAD_DIM
    tm = min(CONV_TILE, S)
    last = S // tm - 1
    tri = (lax.broadcasted_iota(jnp.int32, (tm, tm), 1) >= lax.broadcasted_iota(jnp.int32, (tm, tm), 0)).astype(BF16)

    def body(dq_ref, dk_ref, dv_ref, q_ref, k_ref, dz_ref, f_ref, gq_ref, gk_ref, tri_ref, ones_ref,
             dproj_ref, small_ref, carry, pairs):
        @pl.when(pl.program_id(0) == 0)
        def _():
            small_ref[...] = jnp.zeros((8, LANES), F32)
            carry[...] = jnp.zeros((8, LANES), F32)

        lane = _lane()

        def head_pair(j, acc):
            dcs, dgq, dgk = acc
            dq2, dk2 = [], []
            q_pair, k_pair = q_ref[j], k_ref[j]
            for parity in (0, 1):
                hd = 2 * j + parity
                own, a = _own(lane, parity), _aug(parity)
                dqf = dq_ref[hd]
                dqn = jnp.where(own, dqf * Q_SCALE, 0.0)
                d, dg = _head_rms_bwd(dqn, jnp.where(own, q_pair, 0.0), gq_ref[...], ones_ref[...])
                dq2.append(d)
                dgq = dgq + dg
                dkt = dk_ref[hd]
                dcs = dcs + jnp.where(lane == hd, _col(dqf, lane, a) - _col(dkt, lane, a + 3), 0.0)
                d, dg = _head_rms_bwd(jnp.where(own, dkt, 0.0), jnp.where(own, k_pair, 0.0), gk_ref[...], ones_ref[...])
                dk2.append(d)
                dgk = dgk + dg
            pairs[0, j] = _pair_tile(*dq2, lane).astype(BF16)
            pairs[1, j] = _pair_tile(*dk2, lane).astype(BF16)
            pairs[2, j] = _pair_tile(dv_ref[2 * j], dv_ref[2 * j + 1], lane)
            return dcs, dgq, dgk

        zero = jnp.zeros((1, LANES), F32)
        dcs, dgq, dgk = lax.fori_loop(0, H // 2, head_pair, (jnp.zeros((tm, LANES), F32), zero, zero))
        for part in range(3):
            for j in range(H // 2):
                dproj_ref[:, part * D + LANES * j:part * D + LANES * (j + 1)] = pairs[part, j]
        dproj_ref[:, 3 * D:4 * D] = dz_ref[...]
        dlogf = _dot01(tri_ref[...], dcs) + carry[0:1, :]
        carry[...] = jnp.broadcast_to(dlogf[0:1, :], (8, LANES))
        df = dlogf * (1.0 / (1.0 + jnp.exp(f_ref[...])))
        dproj_ref[:, 4 * D:4 * D + LANES] = df.astype(BF16)
        small_ref[0:1, :] += jnp.sum(df, axis=0, keepdims=True)
        small_ref[1:2, :] += dgq
        small_ref[2:3, :] += dgk

    W = 4 * D + LANES
    heads = pl.BlockSpec((H, tm, LANES), lambda i: (0, last - i, 0))
    head_pairs = pl.BlockSpec((H // 2, tm, LANES), lambda i: (0, last - i, 0))
    return pl.pallas_call(
        body, name="attn_proj_bwd", grid=(S // tm,),
        in_specs=[heads, heads, heads, head_pairs, head_pairs,
                  _rows(tm, D, last), _rows(tm, LANES, last), _whole((1, LANES)), _whole((1, LANES)),
                  _whole((tm, tm)), _whole((LANES, LANES))],
        out_specs=[_rows(tm, W, last), _whole((8, LANES))],
        out_shape=[jax.ShapeDtypeStruct((S, W), BF16), jax.ShapeDtypeStruct((8, LANES), F32)],
        scratch_shapes=[pltpu.VMEM((8, LANES), F32), pltpu.VMEM((3, H // 2, tm, LANES), BF16)],
        compiler_params=_params(1),
    )(dqt, dka, dva, qraw, kraw, dz, f, gq, gk, tri, jnp.ones((LANES, LANES), BF16))


def _matmul_tn(a, b, col0, n, tn, name, stacked=False, chip_sums=()):
    S, M = a.shape
    ts = min(TN_ROWS, S)
    off = col0 // tn
    grid = (n // tn, S // ts)
    k = len(chip_sums)

    def body(a_ref, b_ref, *rest):
        o_ref = rest[k]
        j, s = pl.program_id(0), pl.program_id(1)
        if k:
            copies = _chip_sum_copies(rest[:k], rest[k + 1:2 * k + 1], rest[2 * k + 1], rest[2 * k + 2])

            @pl.when((j == 0) & (s == 0))
            def _():
                for cp in copies:
                    cp.start()

        @pl.when(s == 0)
        def _():
            o_ref[...] = jnp.zeros((M, tn), F32)

        o_ref[...] += _dot_tn(a_ref[...], b_ref[...])

        if k:
            @pl.when((j == grid[0] - 1) & (s == grid[1] - 1))
            def _():
                for cp in copies:
                    cp.wait()

    if stacked:
        out_spec, out_shape = pl.BlockSpec((None, M, tn), lambda j, s: (j, 0, 0)), (n // tn, M, tn)
    else:
        out_spec, out_shape = pl.BlockSpec((M, tn), lambda j, s: (0, j)), (M, n)
    results = pl.pallas_call(
        body, name=name, grid=grid,
        in_specs=[pl.BlockSpec((ts, M), lambda j, s: (s, 0)), pl.BlockSpec((ts, tn), lambda j, s: (s, off + j))]
        + [ANY] * k,
        out_specs=[out_spec] + [ANY] * k,
        out_shape=[jax.ShapeDtypeStruct(out_shape, F32)]
        + [jax.ShapeDtypeStruct((3,) + g.shape[1:], g.dtype) for g in chip_sums],
        scratch_shapes=[pltpu.SemaphoreType.DMA((k, 3))] * (2 if k else 0),
        compiler_params=_params(2),
    )(a, b, *chip_sums)
    return (results[0], list(results[1:])) if k else results[0]


def _adam_update(gv, w_ref, m_ref, v_ref, d_ref, m2_ref, v2_ref):
    m2 = ADAM_B1 * m_ref[...] + (1.0 - ADAM_B1) * gv
    v2 = ADAM_B2 * v_ref[...] + (1.0 - ADAM_B2) * (gv * gv)
    m2_ref[...] = m2
    v2_ref[...] = v2
    m_hat = m2 / (1.0 - ADAM_B1 ** ADAM_STEP)
    v_hat = v2 / (1.0 - ADAM_B2 ** ADAM_STEP)
    d_ref[...] = -ADAM_LR * (m_hat / (jnp.sqrt(v_hat) + ADAM_EPS) + ADAM_WD * w_ref[...])


def _adamw(w, g, m, v, name):
    r, c = w.shape
    tr = ROW_TILE if r % ROW_TILE == 0 else r

    def body(w_ref, g_ref, m_ref, v_ref, d_ref, m2_ref, v2_ref):
        _adam_update(g_ref[...], w_ref, m_ref, v_ref, d_ref, m2_ref, v2_ref)

    spec = _rows(tr, c)
    return pl.pallas_call(
        body, name=name, grid=(r // tr,), in_specs=[spec] * 4, out_specs=[spec] * 3,
        out_shape=[jax.ShapeDtypeStruct((r, c), F32)] * 3, compiler_params=_params(1),
    )(w, g, m, v)


def _adamw_halves(w, mine, other, m, v, core, name):
    r, c = mine.shape
    tr = ROW_TILE if r % ROW_TILE == 0 else r
    per = r // tr

    def body(core_ref, w_ref, mine_ref, other_ref, m_ref, v_ref, g_ref, d_ref, m2_ref, v2_ref):
        gv = jnp.where(pl.program_id(0) // per == core_ref[0], mine_ref[...], other_ref[...])
        g_ref[...] = gv
        _adam_update(gv, w_ref, m_ref, v_ref, d_ref, m2_ref, v2_ref)

    full = pl.BlockSpec((tr, c), lambda i, core: (i, 0))
    half = pl.BlockSpec((tr, c), lambda i, core: (i % per, 0))
    return pl.pallas_call(
        body, name=name,
        grid_spec=pltpu.PrefetchScalarGridSpec(num_scalar_prefetch=1, grid=(2 * per,),
                                               in_specs=[full, half, half, full, full], out_specs=[full] * 4),
        out_shape=[jax.ShapeDtypeStruct((2 * r, c), F32)] * 4, compiler_params=_params(1),
    )(core, w, mine, other, m, v)


def _after_conv(conv_acts, x, target, g1, w_in, conv_w, w_out, g2, wa_in, b_f, gq, gk, wa_out, reduce_early=None):
    S, D = x.shape
    H = D // HEAD_DIM
    ws = wa_in.shape[2]
    w_qkvz = jnp.concatenate([wa_in[0], wa_in[1], wa_in[2], wa_in[3][:, :4 * D - 3 * ws]], axis=1)
    wf = jnp.pad(wa_in[3][:, 4 * D - 3 * ws:], ((0, 0), (0, LANES - H)))
    bf = jnp.pad(b_f, ((0, 0), (0, LANES - H)))
    gq128 = jnp.concatenate([gq, gq], axis=1)
    gk128 = jnp.concatenate([gk, gk], axis=1)

    proj, h1, yc, y, x1 = conv_acts
    h2, qraw, kraw, z, f, c, rel, qa, ka, va, vt = _attn_front(x1, g2, w_qkvz, wf, bf, gq128, gk128)
    T = vt.shape[3]
    kstart, qend, bound = _skip_tables(c[:, :H], (c - rel)[:, :H], gq, gk, T, min(ATT_GROUP, S // T))
    o_aug, lse = lax.cond(2.0 * bound <= PLAIN_EXP_MAX, functools.partial(_attn_fwd, online_max=False),
                          functools.partial(_attn_fwd, online_max=True), kstart, qa, ka, vt)
    dx2, dx2b, o2b, dz, doa, qa2, loss = _attn_out(o_aug, lse.reshape(H, S), rel, z, x1, target, wa_out, qa)
    dqt, dka, dva = _attn_bwd(qend, qa2, doa, ka, va, T)
    dproj2, small = _attn_proj_bwd(dqt, dka, dva, qraw, kraw, dz, f, gq128, gk128)

    tn = min(1024, D)
    dwa_out = _matmul_tn(o2b, dx2b, 0, D, tn, "dw_attn_out")
    dw_main = _matmul_tn(h2, dproj2, 0, 4 * D, 2 * tn, "dw_attn_in")
    dw_f = _matmul_tn(h2, dproj2, 4 * D, LANES, LANES, "dw_attn_f")
    dwa_in = jnp.stack([dw_main[:, 0:ws], dw_main[:, ws:2 * ws], dw_main[:, 2 * ws:3 * ws],
                        jnp.concatenate([dw_main[:, 3 * ws:], dw_f[:, :H]], axis=1)])
    early = None if reduce_early is None else reduce_early([dwa_in, dwa_out])
    dproj1, dx, dx1b, dg1, dcw, dg2 = _conv_bwd(dproj2, w_qkvz, wf, x1, g2, dx2, x, g1, w_in, w_out, conv_w, proj, yc)
    dw_out = _matmul_tn(y, dx1b, 0, D, tn, "dw_conv_out")
    dw_in = _matmul_tn(h1, dproj1, 0, 4 * D, D, "dw_conv_in", stacked=True, chip_sums=() if early is None else early[1])
    if early is not None:
        dw_in, arrived = dw_in
    grads = dict(conv_norm_g=dg1, conv_w_in=dw_in, conv_w=dcw, conv_w_out=dw_out, attn_norm_g=dg2,
                 attn_w_in=dwa_in, attn_b_f=small[0:1, :H],
                 attn_q_norm_g=small[1:2, :HEAD_DIM] + small[1:2, HEAD_DIM:],
                 attn_k_norm_g=small[2:3, :HEAD_DIM] + small[2:3, HEAD_DIM:], attn_w_out=dwa_out)
    return loss[0, 0], dx, grads, (None if early is None else (early[0], list(arrived)))


def _coords():
    return lax.axis_index("x"), lax.axis_index("y"), lax.axis_index("c")


def _at(ref, idx):
    return ref.at[idx] if idx else ref


def _other_chips(x, y):
    return [(1 - x, y), (x, 1 - y), (1 - x, 1 - y)]


def _gather_plan(src, out, send, recv):
    x, y, c = _coords()
    mine = 2 * x + y
    sibling = (x, y, 1 - c)
    others = [(a, k, 2 * px + py, (px, py)) for a in range(len(src)) for k, (px, py) in enumerate(_other_chips(x, y))]

    def copy(a, k, chip, half, to, source=None):
        dst = out[a].at[chip, half]
        return pltpu.make_async_remote_copy(src_ref=dst if source is None else source, dst_ref=dst,
                                            send_sem=send.at[a, k], recv_sem=recv.at[a, k],
                                            device_id=to, device_id_type=MESH)

    first = [copy(a, k, mine, c, (*chip, c), source=src[a].at[c]) for a, k, _, chip in others]
    own = [pltpu.make_async_remote_copy(src_ref=src[a], dst_ref=out[a].at[mine], send_sem=send.at[a, 6],
                                        recv_sem=recv.at[a, 6], device_id=sibling, device_id_type=MESH)
           for a in range(len(src))]
    passed = [copy(a, 3 + k, slot, c, sibling) for a, k, slot, _ in others]

    def start():
        for cp in first + own:
            cp.start()

    def forward():
        for (a, k, slot, _), cp in zip(others, passed):
            copy(a, k, slot, c, (x, y, c)).wait_recv()
            cp.start()

    def finish():
        for a, k, slot, _ in others:
            copy(a, 3 + k, slot, 1 - c, (x, y, c)).wait_recv()
        for cp in own:
            cp.wait_recv()
        for cp in first + passed + own:
            cp.wait_send()

    return start, forward, finish


def _all_gather(halved, whole):
    nh, nw = len(halved), len(whole)

    def body(*refs):
        src_h, src_w = refs[:nh], refs[nh:nh + nw]
        out_h, out_w = refs[nh + nw:2 * nh + nw], refs[2 * nh + nw:2 * (nh + nw)]
        send_h, recv_h, send_w, recv_w = refs[2 * (nh + nw):]
        x, y, c = _coords()
        mine = 2 * x + y
        chips = _other_chips(x, y)

        def copy_w(a, k, chip, to):
            return pltpu.make_async_remote_copy(src_ref=src_w[a], dst_ref=out_w[a].at[chip],
                                                send_sem=send_w.at[a, k], recv_sem=recv_w.at[a, k],
                                                device_id=to, device_id_type=MESH)

        start, forward, finish = _gather_plan(src_h, out_h, send_h, recv_h)
        small = [copy_w(a, k, mine, (*chip, c)) for a in range(nw) for k, chip in enumerate(chips)]
        small += [copy_w(a, 3, mine, (x, y, 1 - c)) for a in range(nw)]
        start()
        for cp in small:
            cp.start()
        forward()
        finish()
        for a in range(nw):
            for k, (px, py) in enumerate(chips):
                copy_w(a, k, 2 * px + py, (x, y, c)).wait_recv()
            copy_w(a, 3, mine, (x, y, c)).wait_recv()
        for cp in small:
            cp.wait_send()

    out_shape = [jax.ShapeDtypeStruct((4,) + a.shape, a.dtype) for a in list(halved) + list(whole)]
    return pl.pallas_call(
        body, name="gather_weights", in_specs=[ANY] * (nh + nw), out_specs=[ANY] * (nh + nw), out_shape=out_shape,
        scratch_shapes=[pltpu.SemaphoreType.DMA((nh, 7)), pltpu.SemaphoreType.DMA((nh, 7)),
                        pltpu.SemaphoreType.DMA((nw, 4)), pltpu.SemaphoreType.DMA((nw, 4))],
    )(*halved, *whole)


def _exchange(name, srcs, lands, copies, local_copies):
    ns, nl, n, nloc = len(srcs), len(lands), len(copies), len(local_copies)

    def body(*refs):
        src, land = refs[:ns], refs[ns:ns + nl]
        send, recv, local_sem = refs[ns + nl:]
        me = _coords()
        started = []
        for k, (si, s_at, li, l_at, ci) in enumerate(local_copies):
            cp = pltpu.make_async_copy(_at(src[si], s_at(*me)), _at(land[li], l_at(*me)), local_sem.at[k])
            cp.start()
            started.append(cp)
        remote = []
        for k, (si, s_at, li, l_at, peer) in enumerate(copies):
            cp = pltpu.make_async_remote_copy(src_ref=_at(src[si], s_at(*me)), dst_ref=_at(land[li], l_at(*me)),
                                              send_sem=send.at[k], recv_sem=recv.at[k],
                                              device_id=peer(*me), device_id_type=MESH)
            cp.start()
            remote.append(cp)
        for cp in remote:
            cp.wait()
        for cp in started:
            cp.wait()

    return pl.pallas_call(
        body, name=name, in_specs=[ANY] * ns, out_specs=[ANY] * nl, out_shape=list(lands),
        scratch_shapes=[pltpu.SemaphoreType.DMA((n,)), pltpu.SemaphoreType.DMA((n,)),
                        pltpu.SemaphoreType.DMA((max(nloc, 1),))],
    )(*srcs)


def _add_pairs(a, b, core, name):
    _, r, cols = b.shape
    tr = ROW_TILE if r % ROW_TILE == 0 else r

    def body(core_ref, a_ref, b_ref, o_ref, ob_ref):
        s = a_ref[...] + b_ref[...]
        o_ref[...] = s
        ob_ref[...] = s.astype(BF16)

    spec = pl.BlockSpec((None, tr, cols), lambda j, i, core: (j, i, 0))
    return pl.pallas_call(
        body, name=name,
        grid_spec=pltpu.PrefetchScalarGridSpec(
            num_scalar_prefetch=1, grid=(4, r // tr),
            in_specs=[pl.BlockSpec((None, None, tr, cols), lambda j, i, core: (j, core[0], i, 0)), spec],
            out_specs=[spec, spec]),
        out_shape=[jax.ShapeDtypeStruct(b.shape, F32), jax.ShapeDtypeStruct(b.shape, BF16)],
        compiler_params=_params(2),
    )(core, a, b)


def _sum_chips(own, landed, name):
    _, r, cols = landed.shape
    tr = ROW_TILE if r % ROW_TILE == 0 else r

    def body(own_ref, land_ref, o_ref):
        acc = own_ref[...]
        for j in range(3):
            acc = acc + land_ref[j].astype(F32)
        o_ref[...] = acc

    return pl.pallas_call(
        body, name=name, grid=(r // tr,),
        in_specs=[_rows(tr, cols), pl.BlockSpec((3, tr, cols), lambda i: (0, i, 0))], out_specs=_rows(tr, cols),
        out_shape=jax.ShapeDtypeStruct((r, cols), F32), compiler_params=_params(1),
    )(own, landed)


def _sum_devices(landed, name):
    def body(l_ref, o_ref):
        acc = l_ref[0]
        for j in range(1, 8):
            acc = acc + l_ref[j]
        o_ref[...] = acc

    return pl.pallas_call(body, name=name, out_shape=jax.ShapeDtypeStruct(landed.shape[1:], F32))(landed)


CHIP_FLIPS = [(1, 0), (0, 1), (1, 1)]


def _flip(fx, fy, fc):
    return lambda x, y, c: (x ^ fx, y ^ fy, c ^ fc)


def _sum_cores(big, small, tag):
    nb = len(big)
    c = lax.axis_index("c")
    copies = [(a, (lambda j: lambda x, y, c: (j, 1 - c))(j), a, (lambda j: lambda x, y, c: (j,))(j), _flip(0, 0, 1))
              for a in range(nb) for j in range(4)]
    lands = [jax.ShapeDtypeStruct((4,) + g.shape[2:], F32) for g in big]
    srcs, local = list(big), []
    if small is not None:
        flips = [(fx, fy, fc) for fx in (0, 1) for fy in (0, 1) for fc in (0, 1) if fx or fy or fc]
        copies += [(nb, lambda x, y, c: (), nb, lambda x, y, c: (4 * x + 2 * y + c,), _flip(*f)) for f in flips]
        lands.append(jax.ShapeDtypeStruct((8,) + small.shape, F32))
        local = [(nb, lambda x, y, c: (), nb, lambda x, y, c: (4 * x + 2 * y + c,), None)]
        srcs.append(small)
    landed = _exchange("swap_halves_" + tag, srcs, lands, copies, local)
    small_sum = None if small is None else _sum_devices(landed[nb], "sum_small")
    core = jnp.reshape(c, (1,)).astype(jnp.int32)
    sums = [_add_pairs(big[a], landed[a], core, f"add_cores_{tag}_{a}") for a in range(nb)]
    return [s for s, _ in sums], [sb for _, sb in sums], small_sum


def _chip_sum_copies(sums, landed, send, recv):
    x, y, c = _coords()
    return [pltpu.make_async_remote_copy(src_ref=sums[a].at[2 * (x ^ fx) + (y ^ fy)], dst_ref=landed[a].at[k],
                                         send_sem=send.at[a, k], recv_sem=recv.at[a, k],
                                         device_id=(x ^ fx, y ^ fy, c), device_id_type=MESH)
            for a in range(len(sums)) for k, (fx, fy) in enumerate(CHIP_FLIPS)]


def _send_chip_sums(chip_bf16):
    n = len(chip_bf16)

    def body(*refs):
        copies = _chip_sum_copies(refs[:n], refs[n:2 * n], refs[2 * n], refs[2 * n + 1])
        for cp in copies:
            cp.start()
        for cp in copies:
            cp.wait()

    return pl.pallas_call(
        body, name="send_chip_sums", in_specs=[ANY] * n, out_specs=[ANY] * n,
        out_shape=[jax.ShapeDtypeStruct((3,) + g.shape[1:], BF16) for g in chip_bf16],
        scratch_shapes=[pltpu.SemaphoreType.DMA((n, 3)), pltpu.SemaphoreType.DMA((n, 3))],
    )(*chip_bf16)


def _sum_chips_and_share(chip_f32, landed):
    x, y, _ = _coords()
    totals = [_sum_chips(lax.dynamic_index_in_dim(f, 2 * x + y, axis=0, keepdims=False), l, f"sum_chips_{a}")
              for a, (f, l) in enumerate(zip(chip_f32, landed))]
    copies = [(a, lambda x, y, c: (), a, lambda x, y, c: (), _flip(0, 0, 1)) for a in range(len(totals))]
    lands = [jax.ShapeDtypeStruct(t.shape, F32) for t in totals]
    return list(zip(totals, _exchange("swap_sums", totals, lands, copies, [])))


def kernel(x, conv_norm_g, conv_w_in, conv_w, conv_w_out, attn_norm_g, attn_w_in, attn_b_f, attn_q_norm_g, attn_k_norm_g, attn_w_out, loss_target, m_conv_norm_g, m_conv_w_in, m_conv_w, m_conv_w_out, m_attn_norm_g, m_attn_w_in, m_attn_b_f, m_attn_q_norm_g, m_attn_k_norm_g, m_attn_w_out, v_conv_norm_g, v_conv_w_in, v_conv_w, v_conv_w_out, v_attn_norm_g, v_attn_w_in, v_attn_b_f, v_attn_q_norm_g, v_attn_k_norm_g, v_attn_w_out):
    xi, yi, _ = _coords()
    chip = 2 * xi + yi
    D = x.shape[2]
    H = D // HEAD_DIM
    names = ["conv_norm_g", "conv_w_in", "conv_w", "conv_w_out", "attn_norm_g", "attn_w_in", "attn_b_f",
             "attn_q_norm_g", "attn_k_norm_g", "attn_w_out"]
    weights = dict(zip(names, [conv_norm_g, conv_w_in, conv_w, conv_w_out, attn_norm_g, attn_w_in, attn_b_f,
                               attn_q_norm_g, attn_k_norm_g, attn_w_out]))
    m_in = dict(zip(names, [m_conv_norm_g, m_conv_w_in, m_conv_w, m_conv_w_out, m_attn_norm_g, m_attn_w_in,
                            m_attn_b_f, m_attn_q_norm_g, m_attn_k_norm_g, m_attn_w_out]))
    v_in = dict(zip(names, [v_conv_norm_g, v_conv_w_in, v_conv_w, v_conv_w_out, v_attn_norm_g, v_attn_w_in,
                            v_attn_b_f, v_attn_q_norm_g, v_attn_k_norm_g, v_attn_w_out]))
    weights = {k: w[0] for k, w in weights.items()}
    m_in = {k: w[0] for k, w in m_in.items()}
    v_in = {k: w[0] for k, w in v_in.items()}

    big_names = ["conv_w_in", "attn_w_in", "conv_w_out", "attn_w_out"]
    halved = {k: weights[k].astype(BF16).reshape(2, weights[k].shape[0] // 2, weights[k].shape[1]) for k in big_names}
    q = D // 4
    small_w = jnp.concatenate([weights["conv_w"], weights["attn_norm_g"][None, :], jnp.zeros((4, q), F32)], axis=0)
    g_in, g_out, g_small = _all_gather([halved["conv_w_in"], halved["conv_w_out"]], [small_w])
    w_in = g_in.reshape(4, D, D)
    w_out = g_out.reshape(D, D)
    conv_w_full = g_small[:, 0:3, :].transpose(1, 0, 2).reshape(3, D)
    attn_g_full = g_small[:, 3, :].reshape(1, D)
    g1 = weights["conv_norm_g"][None, :]

    conv_acts, (ga_in, ga_out) = _conv_fwd(x[0], g1, w_in, conv_w_full, w_out, [halved["attn_w_in"], halved["attn_w_out"]])
    def attn_core_sums(dw):
        f32, bf16, _ = _sum_cores([dw[0].reshape(4, 2, D // 2, D + H // 4), dw[1].reshape(4, 2, D // 8, D)], None, "attn")
        return f32, bf16

    loss_part, grad_x, grads, (attn_f32, attn_arrived) = _after_conv(
        conv_acts, x[0], loss_target[0], g1, w_in, conv_w_full, w_out, attn_g_full, ga_in.reshape(4, D, D + H // 4),
        weights["attn_b_f"][None, :], weights["attn_q_norm_g"][None, :], weights["attn_k_norm_g"][None, :],
        ga_out.reshape(D, D), reduce_early=attn_core_sums)

    tail = jnp.concatenate([grads["attn_b_f"], grads["attn_q_norm_g"], grads["attn_k_norm_g"],
                            jnp.reshape(loss_part, (1, 1)), jnp.zeros((1, D - H - 2 * HEAD_DIM - 1), F32)], axis=1)
    small = jnp.concatenate([grads["conv_norm_g"], grads["conv_w"], grads["attn_norm_g"], tail,
                             jnp.zeros((2, D), F32)], axis=0)
    conv_f32, conv_bf16, small_sum = _sum_cores([grads["conv_w_in"].reshape(4, 2, D // 2, D),
                                                 grads["conv_w_out"].reshape(4, 2, D // 8, D)], small, "conv")
    big_names = ["attn_w_in", "attn_w_out", "conv_w_in", "conv_w_out"]
    reduced = _sum_chips_and_share(attn_f32 + conv_f32, attn_arrived + list(_send_chip_sums(conv_bf16)))
    final = {}
    final["conv_norm_g"] = small_sum[0]
    final["conv_w"] = lax.dynamic_slice_in_dim(small_sum[1:4], chip * q, q, axis=1)
    final["attn_norm_g"] = lax.dynamic_slice_in_dim(small_sum[4], chip * q, q, axis=0)
    final["attn_b_f"] = small_sum[5, :H]
    final["attn_q_norm_g"] = small_sum[5, H:H + HEAD_DIM]
    final["attn_k_norm_g"] = small_sum[5, H + HEAD_DIM:H + 2 * HEAD_DIM]
    loss = small_sum[5, H + 2 * HEAD_DIM]

    delta, new_m, new_v = {}, {}, {}
    core = jnp.reshape(lax.axis_index("c"), (1,)).astype(jnp.int32)
    for k, (mine, other) in zip(big_names, reduced):
        final[k], delta[k], new_m[k], new_v[k] = _adamw_halves(weights[k], mine, other, m_in[k], v_in[k], core,
                                                               "adamw_" + k)
    for k in names:
        if k in big_names:
            continue
        shape = weights[k].shape
        as2d = (lambda a: a.reshape(1, -1)) if len(shape) == 1 else (lambda a: a)
        d, m2, v2 = _adamw(as2d(weights[k]), as2d(final[k]), as2d(m_in[k]), as2d(v_in[k]), "adamw_" + k)
        delta[k], new_m[k], new_v[k] = d.reshape(shape), m2.reshape(shape), v2.reshape(shape)
    lead = lambda a: a[None]
    return (loss, grad_x[None], *[lead(final[k]) for k in names], *[lead(delta[k]) for k in names],
            *[lead(new_m[k]) for k in names], *[lead(new_v[k]) for k in names])
```

```python
import functools

import jax
import jax.numpy as jnp
from jax import lax
from jax.experimental import pallas as pl
from jax.experimental.pallas import tpu as pltpu

F32 = jnp.float32
BF16 = jnp.bfloat16
HEAD_DIM = 64
LANES = 128
RMS_EPS = 1e-6
NEG = -1e30
Q_SCALE = 0.125
ROW_TILE = 256
CONV_TILE = 512
ATT_GROUP = 4
SKIP_LOG = 106.0
PLAIN_EXP_MAX = 60.0
TN_ROWS = 2048
VMEM_LIMIT = 56 << 20
ADAM_LR, ADAM_B1, ADAM_B2, ADAM_EPS, ADAM_WD, ADAM_STEP = 0.001, 0.9, 0.999, 1e-08, 0.01, 10
MESH = pl.DeviceIdType.MESH
ANY = pl.BlockSpec(memory_space=pl.ANY)


def _lane():
    return lax.broadcasted_iota(jnp.int32, (1, LANES), 1)


def _split3(x):
    hi = x.astype(BF16).astype(F32)
    r = x - hi
    mid = r.astype(BF16).astype(F32)
    lo = (r - mid).astype(BF16).astype(F32)
    return hi, mid, lo


STAT_STRIDE = 16
ONE_LANE = 3 * STAT_STRIDE


def _pack3(x, lane, one):
    hi, mid, lo = _split3(x)
    packed = hi + pltpu.roll(mid, STAT_STRIDE, 1) + pltpu.roll(lo, 2 * STAT_STRIDE, 1)
    return jnp.where(lane == ONE_LANE, one, packed).astype(BF16)


def _scatter_matrices(H):
    rows = lax.broadcasted_iota(jnp.int32, (LANES, H * LANES), 0)
    cols = lax.broadcasted_iota(jnp.int32, (LANES, H * LANES), 1)
    head, within = cols // LANES, cols % LANES
    extra = within - _aug(head % 2)
    term = (rows < ONE_LANE) & (rows % STAT_STRIDE == head)
    first = ((term & (extra == rows // STAT_STRIDE)) | ((rows == ONE_LANE) & (extra >= 3) & (extra < 6)))
    second = ((term & (extra - 3 == rows // STAT_STRIDE)) | ((rows == ONE_LANE) & (extra >= 0) & (extra < 3)))
    return first.astype(BF16), second.astype(BF16)


def _col(x, lane, idx):
    return jnp.sum(jnp.where(lane == idx, x, 0.0), axis=1, keepdims=True)


def _feat(parity):
    return HEAD_DIM * parity


def _aug(parity):
    return HEAD_DIM * (1 - parity)


def _own(lane, parity):
    return (lane >= _feat(parity)) & (lane < _feat(parity) + HEAD_DIM)


def _head_tile(ref, hd, lane):
    j = hd // 2
    return jnp.where(_own(lane, hd % 2), ref[:, LANES * j:LANES * (j + 1)], 0.0)


def _pair_tile(even, odd, lane):
    return jnp.where(lane < HEAD_DIM, even, odd)


def _sigmoid(x):
    return 0.5 * jnp.tanh(0.5 * x) + 0.5


def _dot(a, b):
    return jnp.dot(a, b, preferred_element_type=F32)


def _dot_nt(a, b):
    return lax.dot_general(a, b, (((1,), (1,)), ((), ())), preferred_element_type=F32)


def _dot_tn(a, b):
    return lax.dot_general(a, b, (((0,), (0,)), ((), ())), preferred_element_type=F32)


def _dot01(tri, x):
    hi, mid, lo = _split3(x)
    return _dot(tri, hi.astype(BF16)) + _dot(tri, mid.astype(BF16)) + _dot(tri, lo.astype(BF16))


def _rms_bwd(dh, x, g):
    inv = lax.rsqrt(jnp.mean(x * x, axis=-1, keepdims=True) + RMS_EPS)
    xh = x * inv
    dxn = dh * g
    dx = inv * (dxn - xh * jnp.mean(dxn * xh, axis=-1, keepdims=True))
    return dx, jnp.sum(dh * xh, axis=0, keepdims=True)


def _head_rms_bwd(dn, t, g, ones):
    sq = t * t
    hi = sq.astype(BF16)
    lo = (sq - hi.astype(F32)).astype(BF16)
    inv = lax.rsqrt((_dot(hi, ones) + _dot(lo, ones)) * (1.0 / HEAD_DIM) + RMS_EPS)
    th = t * inv
    gd = dn * g
    d = inv * (gd - th * (jnp.sum(gd * th, axis=1, keepdims=True) * (1.0 / HEAD_DIM)))
    return d, jnp.sum(dn * th, axis=0, keepdims=True)


def _params(n_grid):
    return pltpu.CompilerParams(dimension_semantics=("arbitrary",) * n_grid, vmem_limit_bytes=VMEM_LIMIT)


def _rows(tm, cols, rev=None):
    if rev is None:
        return pl.BlockSpec((tm, cols), lambda i: (i, 0))
    return pl.BlockSpec((tm, cols), lambda i: (rev - i, 0))


def _whole(shape, buffers=None):
    mode = {} if buffers is None else dict(pipeline_mode=pl.Buffered(buffers))
    return pl.BlockSpec(shape, lambda *_: (0,) * len(shape), **mode)


def _conv_fwd(x, g1, w_in, conv_w, w_out, later):
    S, D = x.shape
    tm = min(CONV_TILE, S)
    sub = min(ROW_TILE, tm)
    steps = S // tm
    n = len(later)

    def body(x_ref, g_ref, win_ref, cw_ref, wout_ref, *rest):
        shard_refs, rest = rest[:n], rest[n:]
        proj_ref, h_ref, yc_ref, y_ref, x1_ref = rest[:5]
        gathered_refs, rest = rest[5:5 + n], rest[5 + n:]
        prev_u = rest[0]
        if n:
            start, forward, finish = _gather_plan(shard_refs, gathered_refs, rest[1], rest[2])
            pl.when(pl.program_id(0) == 0)(start)
            pl.when(pl.program_id(0) == steps // 2)(forward)

        @pl.when(pl.program_id(0) == 0)
        def _():
            prev_u[...] = jnp.zeros((sub, D), F32)

        for r in range(0, tm, sub):
            rows = slice(r, r + sub)
            xv = x_ref[rows, :]
            inv = lax.rsqrt(jnp.mean(xv * xv, axis=-1, keepdims=True) + RMS_EPS)
            h = (xv * inv * g_ref[...]).astype(BF16)
            h_ref[rows, :] = h
            for j in range(4):
                proj_ref[rows, j * D:(j + 1) * D] = _dot(h, win_ref[j])
            u = proj_ref[rows, D:2 * D] * proj_ref[rows, 2 * D:3 * D]
            pu = prev_u[...]
            row = lax.broadcasted_iota(jnp.int32, (sub, 1), 0)
            u1 = jnp.where(row < 1, pltpu.roll(pu, 1, 0), pltpu.roll(u, 1, 0))
            u2 = jnp.where(row < 2, pltpu.roll(pu, 2, 0), pltpu.roll(u, 2, 0))
            prev_u[...] = u
            w = cw_ref[...]
            yc = w[2:3] * u + w[1:2] * u1 + w[0:1] * u2
            yc_ref[rows, :] = yc
            z = proj_ref[rows, 3 * D:4 * D]
            y = (proj_ref[rows, 0:D] * yc * (z * _sigmoid(z))).astype(BF16)
            y_ref[rows, :] = y
            x1_ref[rows, :] = xv + _dot(y, wout_ref[...])

        if n:
            pl.when(pl.program_id(0) == steps - 1)(finish)

    results = pl.pallas_call(
        body, name="conv_fwd", grid=(steps,),
        in_specs=[_rows(tm, D), _whole((1, D)), _whole((4, D, D), 1), _whole((3, D)), _whole((D, D), 1)] + [ANY] * n,
        out_specs=[_rows(tm, 4 * D), _rows(tm, D), _rows(tm, D), _rows(tm, D), _rows(tm, D)] + [ANY] * n,
        out_shape=[jax.ShapeDtypeStruct((S, 4 * D), F32), jax.ShapeDtypeStruct((S, D), BF16),
                   jax.ShapeDtypeStruct((S, D), F32), jax.ShapeDtypeStruct((S, D), BF16),
                   jax.ShapeDtypeStruct((S, D), F32)] + [jax.ShapeDtypeStruct((4,) + a.shape, a.dtype) for a in later],
        scratch_shapes=[pltpu.VMEM((sub, D), F32)] + [pltpu.SemaphoreType.DMA((n, 7))] * (2 if n else 0),
        compiler_params=_params(1),
    )(x, g1, w_in, conv_w, w_out, *later)
    return results[:5], results[5:]


def _conv_bwd(dproj2, wa, wf, x1, g2, dx2, x, g1, w_in, w_out, conv_w, proj, yc):
    S, D = x.shape
    tm = min(ROW_TILE, S)
    sub = min(ROW_TILE, tm)
    last = S // tm - 1

    def body(dp2_ref, wa_ref, wf_ref, x1_ref, g2_ref, dx2_ref, x_ref, g_ref, win_ref, wout_ref, cw_ref, proj_ref, yc_ref,
             dproj_ref, dx_ref, dx1b_ref, dg_ref, dcw_ref, dg2_ref, next_d):
        @pl.when(pl.program_id(0) == 0)
        def _():
            dg_ref[...] = jnp.zeros((1, D), F32)
            dcw_ref[...] = jnp.zeros((3, D), F32)
            dg2_ref[...] = jnp.zeros((1, D), F32)
            next_d[...] = jnp.zeros((sub, D), F32)

        for r in range(tm - sub, -1, -sub):
            rows = slice(r, r + sub)
            dh2 = _dot_nt(dp2_ref[rows, 0:4 * D], wa_ref[...]) + _dot_nt(dp2_ref[rows, 4 * D:4 * D + LANES], wf_ref[...])
            dxn2, dg2 = _rms_bwd(dh2, x1_ref[rows, :], g2_ref[...])
            dg2_ref[...] += dg2
            dx1v = dx2_ref[rows, :] + dxn2
            dx1b = dx1v.astype(BF16)
            dx1b_ref[rows, :] = dx1b
            dy = _dot_nt(dx1b, wout_ref[...])
            b = proj_ref[rows, 0:D]
            c = proj_ref[rows, D:2 * D]
            xin = proj_ref[rows, 2 * D:3 * D]
            z = proj_ref[rows, 3 * D:4 * D]
            sg = _sigmoid(z)
            sz = z * sg
            ycv = yc_ref[rows, :]
            d0 = dy * b * sz
            dproj_ref[rows, 0:D] = (dy * ycv * sz).astype(BF16)
            dproj_ref[rows, 3 * D:4 * D] = (dy * b * ycv * (sg * (1.0 + z * (1.0 - sg)))).astype(BF16)
            nd = next_d[...]
            row = lax.broadcasted_iota(jnp.int32, (sub, 1), 0)
            d1 = jnp.where(row >= sub - 1, pltpu.roll(nd, sub - 1, 0), pltpu.roll(d0, sub - 1, 0))
            d2 = jnp.where(row >= sub - 2, pltpu.roll(nd, sub - 2, 0), pltpu.roll(d0, sub - 2, 0))
            next_d[...] = d0
            w = cw_ref[...]
            du = w[2:3] * d0 + w[1:2] * d1 + w[0:1] * d2
            u = c * xin
            dcw_ref[2:3, :] += jnp.sum(d0 * u, axis=0, keepdims=True)
            dcw_ref[1:2, :] += jnp.sum(d1 * u, axis=0, keepdims=True)
            dcw_ref[0:1, :] += jnp.sum(d2 * u, axis=0, keepdims=True)
            dproj_ref[rows, D:2 * D] = (du * xin).astype(BF16)
            dproj_ref[rows, 2 * D:3 * D] = (du * c).astype(BF16)
            dh = _dot_nt(dproj_ref[rows, 0:D], win_ref[0])
            for j in range(1, 4):
                dh = dh + _dot_nt(dproj_ref[rows, j * D:(j + 1) * D], win_ref[j])
            dxn, dg = _rms_bwd(dh, x_ref[rows, :], g_ref[...])
            dx_ref[rows, :] = dx1v + dxn
            dg_ref[...] += dg

    return pl.pallas_call(
        body, name="conv_bwd", grid=(S // tm,),
        in_specs=[_rows(tm, 4 * D + LANES, last), _whole((D, 4 * D), 1), _whole((D, LANES), 1), _rows(tm, D, last),
                  _whole((1, D)), _rows(tm, D, last),
                  _rows(tm, D, last), _whole((1, D)), _whole((4, D, D), 1), _whole((D, D), 1),
                  _whole((3, D)), _rows(tm, 4 * D, last), _rows(tm, D, last)],
        out_specs=[_rows(tm, 4 * D, last), _rows(tm, D, last), _rows(tm, D, last), _whole((1, D)), _whole((3, D)),
                   _whole((1, D))],
        out_shape=[jax.ShapeDtypeStruct((S, 4 * D), BF16), jax.ShapeDtypeStruct((S, D), F32),
                   jax.ShapeDtypeStruct((S, D), BF16), jax.ShapeDtypeStruct((1, D), F32),
                   jax.ShapeDtypeStruct((3, D), F32), jax.ShapeDtypeStruct((1, D), F32)],
        scratch_shapes=[pltpu.VMEM((sub, D), F32)],
        compiler_params=_params(1),
    )(dproj2, wa, wf, x1, g2, dx2, x, g1, w_in, w_out, conv_w, proj, yc)


def _attn_front(x1, g2, w, wf, bf, gq, gk):
    S, D = x1.shape
    H = D // HEAD_DIM
    tm = min(ROW_TILE, S)
    tri = (lax.broadcasted_iota(jnp.int32, (tm, tm), 1) <= lax.broadcasted_iota(jnp.int32, (tm, tm), 0)).astype(BF16)

    def body(x_ref, g_ref, w_ref, wf_ref, bf_ref, gq_ref, gk_ref, tri_ref, first_ref, second_ref,
             h_ref, qh_ref, kh_ref, z_ref, f_ref, c_ref, rel_ref, qa_ref, ka_ref, va_ref, vt_ref,
             carry, v_s, qraw_ref, kraw_ref):
        @pl.when(pl.program_id(0) == 0)
        def _():
            carry[...] = jnp.zeros((8, LANES), F32)

        xv = x_ref[...]
        inv = lax.rsqrt(jnp.mean(xv * xv, axis=-1, keepdims=True) + RMS_EPS)
        h = (xv * inv * g_ref[...]).astype(BF16)
        h_ref[...] = h
        qraw_ref[...] = _dot(h, w_ref[:, 0:D])
        kraw_ref[...] = _dot(h, w_ref[:, D:2 * D])
        v_s[...] = _dot(h, w_ref[:, 2 * D:3 * D])
        z_ref[...] = _dot(h, w_ref[:, 3 * D:4 * D])
        lane = _lane()
        f = _dot(h, wf_ref[...]) + bf_ref[...]
        f_ref[...] = f
        logf = jnp.where(lane < H, jnp.minimum(f, 0.0) - jnp.log(1.0 + jnp.exp(-jnp.abs(f))), 0.0)
        cs = _dot01(tri_ref[...], logf) + carry[0:1, :]
        c_ref[...] = cs
        carry[...] = jnp.broadcast_to(cs[tm - 1:tm, :], (8, LANES))
        diags = jnp.zeros((tm, LANES), F32)
        for hd in range(H):
            sl = slice(LANES * hd, LANES * (hd + 1))
            a = _aug(hd % 2)
            if hd % 2 == 0:
                qh_ref[hd // 2] = qraw_ref[:, LANES * (hd // 2):LANES * (hd // 2 + 1)]
                kh_ref[hd // 2] = kraw_ref[:, LANES * (hd // 2):LANES * (hd // 2 + 1)]
            qt = _head_tile(qraw_ref, hd, lane)
            qn = qt * lax.rsqrt(jnp.sum(qt * qt, axis=1, keepdims=True) * (1.0 / HEAD_DIM) + RMS_EPS) * gq_ref[...]
            kt = _head_tile(kraw_ref, hd, lane)
            kn = kt * lax.rsqrt(jnp.sum(kt * kt, axis=1, keepdims=True) * (1.0 / HEAD_DIM) + RMS_EPS) * gk_ref[...]
            diags = diags + jnp.where(lane == hd, jnp.sum(qn * kn, axis=1, keepdims=True) * Q_SCALE, 0.0)
            qa_ref[:, sl] = (qn * Q_SCALE).astype(BF16)
            ka_ref[:, sl] = kn.astype(BF16)
            va = jnp.where((lane >= a) & (lane < a + 3), 1.0, _head_tile(v_s, hd, lane))
            va_ref[:, sl] = va.astype(BF16)
            vt_ref[hd] = va.T.astype(BF16)
        rel = cs - diags
        rel_ref[...] = rel
        qa_ref[...] += _dot(_pack3(rel, lane, 1.0), first_ref[...]).astype(BF16)
        ka_ref[...] += _dot(_pack3(-cs, lane, 1.0), second_ref[...]).astype(BF16)

    nb = S // tm
    heads = pl.BlockSpec((H // 2, tm, LANES), lambda i: (0, i, 0))
    return pl.pallas_call(
        body, name="attn_front", grid=(nb,),
        in_specs=[_rows(tm, D), _whole((1, D)), _whole((D, 4 * D)), _whole((D, LANES)), _whole((1, LANES)),
                  _whole((1, LANES)), _whole((1, LANES)), _whole((tm, tm)), _whole((LANES, H * LANES)),
                  _whole((LANES, H * LANES))],
        out_specs=[_rows(tm, D), heads, heads, _rows(tm, D), _rows(tm, LANES), _rows(tm, LANES), _rows(tm, LANES),
                   _rows(tm, H * LANES), _rows(tm, H * LANES), _rows(tm, H * LANES),
                   pl.BlockSpec((H, None, LANES, tm), lambda i: (0, i, 0, 0))],
        out_shape=[jax.ShapeDtypeStruct((S, D), BF16), jax.ShapeDtypeStruct((H // 2, S, LANES), F32),
                   jax.ShapeDtypeStruct((H // 2, S, LANES), F32), jax.ShapeDtypeStruct((S, D), F32),
                   jax.ShapeDtypeStruct((S, LANES), F32), jax.ShapeDtypeStruct((S, LANES), F32),
                   jax.ShapeDtypeStruct((S, LANES), F32),
                   jax.ShapeDtypeStruct((S, H * LANES), BF16), jax.ShapeDtypeStruct((S, H * LANES), BF16),
                   jax.ShapeDtypeStruct((S, H * LANES), BF16), jax.ShapeDtypeStruct((H, nb, LANES, tm), BF16)],
        scratch_shapes=[pltpu.VMEM((8, LANES), F32), pltpu.VMEM((tm, D), F32), pltpu.VMEM((tm, D), F32),
                        pltpu.VMEM((tm, D), F32)],
        compiler_params=_params(1),
    )(x1, g2, w, wf, bf, gq, gk, tri, *_scatter_matrices(H))


def _skip_tables(c, diag, gq, gk, T, G):
    nb = c.shape[0] // T
    bound = HEAD_DIM ** 0.5 * jnp.max(jnp.abs(gq)) * jnp.max(jnp.abs(gk))
    first, last = c[0::T, :], c[T - 1::T, :]
    lowest = jnp.maximum(jnp.min(diag.reshape(nb, T, -1), axis=1), -bound)
    idx = jnp.arange(nb)
    margin = (SKIP_LOG + bound) - lowest
    need = (last[None, :, :] <= first[:, None, :] + margin[:, None, :]) & (idx[None, :, None] < idx[:, None, None])
    need = need | (idx[None, :, None] == idx[:, None, None])
    kstart = jnp.argmax(need, axis=1)
    qend = nb - 1 - jnp.argmax(need[::-1], axis=0)
    kstart = jnp.min(kstart.reshape(2 * nb // G, G // 2, -1), axis=1)
    kstart = kstart - (kstart & 1)
    qend = jnp.max(qend.reshape(2 * nb // G, G // 2, -1), axis=1)
    return kstart.T.astype(jnp.int32), qend.T.astype(jnp.int32), bound


def _attn_fwd(kstart, qa, ka, vt, online_max):
    S = qa.shape[0]
    H = qa.shape[1] // LANES
    nb, T = vt.shape[1], vt.shape[3]
    G = 2 * nb // kstart.shape[1]
    W = G * T

    def finish(acc, shift, o_ref, lse_ref):
        a = _aug(pl.program_id(0) % 2)
        feat = lax.broadcasted_iota(jnp.int32, (LANES, 1), 0)
        l = jnp.sum(jnp.where(feat == a, acc, 0.0), axis=0, keepdims=True)
        o_ref[...] = (acc * (1.0 / l)).T
        lse_ref[...] = shift + jnp.log(l)

    def causal(st):
        return jnp.where(lax.broadcasted_iota(jnp.int32, st.shape, 0) <= lax.broadcasted_iota(jnp.int32, st.shape, 1),
                         st, NEG)

    def fast_body(ks_ref, q_ref, k_ref, vt_ref, o_ref, lse_ref, acc_ref, sa_ref, sb_ref, sc_ref):
        h, g = pl.program_id(0), pl.program_id(1)
        q = q_ref[...]
        acc_ref[...] = jnp.zeros((LANES, W), F32)

        def scores(ki, lo):
            return _dot_nt(k_ref[pl.ds(pl.multiple_of(ki * T, T), 2 * T), :], q[lo * T:, :])

        def weighted(ki, p):
            return _dot(vt_ref[ki], p[:T]) + _dot(vt_ref[ki + 1], p[T:])

        first = ks_ref[h, 2 * g + 1]
        early = jnp.minimum(ks_ref[h, 2 * g], first)

        def narrow(i, carry):
            ki = early + 2 * i
            st = _dot_nt(k_ref[pl.ds(pl.multiple_of(ki * T, T), 2 * T), :], q[:W // 2, :])
            acc_ref[:, :W // 2] += weighted(ki, jnp.exp(st).astype(BF16))
            return carry

        lax.fori_loop(0, (first - early) // 2, narrow, 0)
        steps = (g * G - first) // 2
        sa_ref[...] = scores(first, 0)

        def advance(ki, cur_ref, next_ref):
            p = jnp.exp(cur_ref[...]).astype(BF16)
            next_ref[...] = scores(ki + 2, 0)
            acc_ref[...] += weighted(ki, p)

        def loop(i, carry):
            advance(first + 4 * i, sa_ref, sb_ref)
            advance(first + 4 * i + 2, sb_ref, sa_ref)
            return carry

        lax.fori_loop(0, steps // 2, loop, 0)

        def first_own(pending_ref):
            p = jnp.exp(causal(pending_ref[...])).astype(BF16)
            if G > 2:
                sc_ref[:, :W - 2 * T] = scores(g * G + 2, 2)
            acc_ref[...] += weighted(g * G, p)

        @pl.when(steps % 2 == 1)
        def _():
            advance(g * G - 2, sa_ref, sb_ref)
            first_own(sb_ref)

        @pl.when(steps % 2 == 0)
        def _():
            first_own(sa_ref)

        if G > 2:
            acc_ref[:, 2 * T:] += weighted(g * G + 2, jnp.exp(causal(sc_ref[:, :W - 2 * T])).astype(BF16))
        for j in range(4, G, 2):
            p = jnp.exp(causal(scores(g * G + j, j))).astype(BF16)
            acc_ref[:, j * T:] += weighted(g * G + j, p)
        finish(acc_ref[...], 0.0, o_ref, lse_ref)

    def online_body(ks_ref, q_ref, k_ref, vt_ref, o_ref, lse_ref, acc_ref, m_ref):
        h, g = pl.program_id(0), pl.program_id(1)
        q = q_ref[...]
        m_ref[...] = jnp.full((8, W), NEG, F32)
        acc_ref[...] = jnp.zeros((LANES, W), F32)

        def update(st, vtb, lo):
            m_old = m_ref[0:1, lo:]
            m_new = jnp.maximum(m_old, jnp.max(st, axis=0, keepdims=True))
            p = jnp.exp(st - m_new).astype(BF16)
            acc_ref[:, lo:] = acc_ref[:, lo:] * jnp.exp(m_old - m_new) + _dot(vtb, p)
            m_ref[:, lo:] = jnp.broadcast_to(m_new, (8, W - lo))

        def loop(ki, carry):
            kb = k_ref[pl.ds(pl.multiple_of(ki * T, T), T), :]
            update(_dot_nt(kb, q), vt_ref[ki], 0)
            return carry

        lax.fori_loop(jnp.minimum(ks_ref[h, 2 * g], ks_ref[h, 2 * g + 1]), g * G, loop, 0)
        for j in range(G):
            ki = g * G + j
            kb = k_ref[pl.ds(pl.multiple_of(ki * T, T), T), :]
            update(causal(_dot_nt(kb, q[j * T:, :])), vt_ref[ki], j * T)
        finish(acc_ref[...], m_ref[0:1, :], o_ref, lse_ref)

    return pl.pallas_call(
        online_body if online_max else fast_body, name="attn_fwd_online" if online_max else "attn_fwd",
        grid_spec=pltpu.PrefetchScalarGridSpec(
            num_scalar_prefetch=1, grid=(H, nb // G),
            in_specs=[pl.BlockSpec((W, LANES), lambda h, i, ks: (i, h)),
                      pl.BlockSpec((S, LANES), lambda h, i, ks: (0, h)),
                      pl.BlockSpec((None, nb, LANES, T), lambda h, i, ks: (h, 0, 0, 0))],
            out_specs=[pl.BlockSpec((W, LANES), lambda h, i, ks: (i, h)),
                       pl.BlockSpec((None, 1, W), lambda h, i, ks: (h, 0, i))],
            scratch_shapes=[pltpu.VMEM((LANES, W), F32)] + (
                [pltpu.VMEM((8, W), F32)] if online_max else [pltpu.VMEM((2 * T, W), F32)] * 3)),
        out_shape=[jax.ShapeDtypeStruct((S, H * LANES), F32), jax.ShapeDtypeStruct((H, 1, S), F32)],
        compiler_params=_params(2),
    )(kstart, qa, ka, vt)


def _attn_out(o_aug, lse, rel, z, x1, target, w_out, qa):
    S, D = x1.shape
    H = D // HEAD_DIM
    tm = min(ROW_TILE, S)

    def body(o_ref, z_ref, x1_ref, t_ref, w_ref, q_ref, first_ref, rel_ref, lse_ref,
             dx2_ref, dx2b_ref, o2b_ref, dz_ref, doa_ref, qa2_ref, loss_ref, oc_s, do_s):
        @pl.when(pl.program_id(0) == 0)
        def _():
            loss_ref[...] = jnp.zeros((1, LANES), F32)

        lane = _lane()
        for j in range(H // 2):
            oc_s[:, LANES * j:LANES * (j + 1)] = _pair_tile(o_ref[:, 2 * LANES * j:2 * LANES * j + LANES],
                                                            o_ref[:, 2 * LANES * j + LANES:2 * LANES * (j + 1)], lane)
        oc = oc_s[...]
        zv = z_ref[...]
        sg = _sigmoid(zv)
        sz = zv * sg
        o2 = (oc * sz).astype(BF16)
        o2b_ref[...] = o2
        e = x1_ref[...] + _dot(o2, w_ref[...]) - t_ref[...]
        sq = jnp.sum(jnp.sum(e * e, axis=1, keepdims=True), axis=0, keepdims=True)
        loss_ref[...] += jnp.broadcast_to(sq * (0.5 / D), (1, LANES))
        dx2 = e * (1.0 / D)
        dx2_ref[...] = dx2
        dx2b = dx2.astype(BF16)
        dx2b_ref[...] = dx2b
        do2 = _dot_nt(dx2b, w_ref[...])
        dz_ref[...] = (do2 * oc * (sg * (1.0 + zv * (1.0 - sg)))).astype(BF16)
        do_s[...] = do2 * sz
        deltas = jnp.zeros((tm, LANES), F32)
        for hd in range(H):
            dt = _head_tile(do_s, hd, lane)
            delta = jnp.sum(dt * _head_tile(oc_s, hd, lane), axis=1, keepdims=True)
            deltas = deltas + jnp.where(lane == hd, delta, 0.0)
            doa_ref[:, LANES * hd:LANES * (hd + 1)] = dt.astype(BF16)
        doa_ref[...] += _dot(_pack3(-deltas, lane, 0.0), first_ref[...]).astype(BF16)
        lse = jnp.concatenate([lse_ref[...], jnp.zeros((LANES - H, tm), F32)], axis=0).T
        rq = rel_ref[...] - lse
        tile_lane = lax.broadcasted_iota(jnp.int32, (1, H * LANES), 1)
        extra = tile_lane % LANES - _aug((tile_lane // LANES) % 2)
        kept = jnp.where((extra >= 0) & (extra < 3), jnp.zeros((), BF16), q_ref[...])
        qa2_ref[...] = kept + _dot(_pack3(rq, lane, 0.0), first_ref[...]).astype(BF16)

    return pl.pallas_call(
        body, name="attn_out", grid=(S // tm,),
        in_specs=[_rows(tm, H * LANES), _rows(tm, D), _rows(tm, D), _rows(tm, D), _whole((D, D)),
                  _rows(tm, H * LANES), _whole((LANES, H * LANES)), _rows(tm, LANES),
                  pl.BlockSpec((H, tm), lambda i: (0, i))],
        out_specs=[_rows(tm, D), _rows(tm, D), _rows(tm, D), _rows(tm, D), _rows(tm, H * LANES),
                   _rows(tm, H * LANES), _whole((1, LANES))],
        out_shape=[jax.ShapeDtypeStruct((S, D), F32), jax.ShapeDtypeStruct((S, D), BF16),
                   jax.ShapeDtypeStruct((S, D), BF16), jax.ShapeDtypeStruct((S, D), BF16),
                   jax.ShapeDtypeStruct((S, H * LANES), BF16), jax.ShapeDtypeStruct((S, H * LANES), BF16),
                   jax.ShapeDtypeStruct((1, LANES), F32)],
        scratch_shapes=[pltpu.VMEM((tm, D), F32), pltpu.VMEM((tm, D), F32)],
        compiler_params=_params(1),
    )(o_aug, z, x1, target, w_out, qa, _scatter_matrices(H)[0], rel, lse)


def _attn_bwd(qend, qa2, doa, ka, va, T):
    S = qa2.shape[0]
    H = qa2.shape[1] // LANES
    nb = S // T
    G = 2 * nb // qend.shape[1]
    W = G * T

    def body(qe_ref, q_ref, do_ref, k_ref, v_ref, dq_ref, dk_ref, dv_ref, dkt_acc, dvt_acc):
        h, g = pl.program_id(0), pl.program_id(1)

        @pl.when(g == 0)
        def _():
            dq_ref[...] = jnp.zeros((S, LANES), F32)

        kb = k_ref[...]
        vb = v_ref[...]
        dkt_acc[...] = jnp.zeros((LANES, W), F32)
        dvt_acc[...] = jnp.zeros((LANES, W), F32)

        def step(qi, c0, c1, masked):
            rows = pl.ds(pl.multiple_of(qi * T, T), 2 * T)
            qb = q_ref[rows, :]
            dob = do_ref[rows, :]
            s = _dot_nt(qb, kb[c0:c1])
            if masked:
                query = lax.broadcasted_iota(jnp.int32, s.shape, 0) + (c1 - 2 * T)
                s = jnp.where(lax.broadcasted_iota(jnp.int32, s.shape, 1) <= query, s, NEG)
            p = jnp.exp(s)
            ds = (p * _dot_nt(dob, vb[c0:c1])).astype(BF16)
            dvt_acc[:, c0:c1] += _dot(dob.astype(F32).T.astype(BF16), p.astype(BF16))
            dkt_acc[:, c0:c1] += _dot(qb.astype(F32).T.astype(BF16), ds)
            dq_ref[rows, :] += _dot(ds, kb[c0:c1])

        for m in range(G // 2):
            step(g * G + 2 * m, 0, (m + 1) * 2 * T, True)
        first = g * G + G
        n_all = jnp.maximum((qe_ref[h, 2 * g] - first + 2) // 2, 0)
        second = first + 2 * n_all

        def all_keys(i, carry):
            step(first + 2 * i, 0, W, False)
            return carry

        def late_keys(i, carry):
            step(second + 2 * i, W // 2, W, False)
            return carry

        lax.fori_loop(0, n_all, all_keys, 0)
        lax.fori_loop(0, (qe_ref[h, 2 * g + 1] - second + 2) // 2, late_keys, 0)
        dk_ref[...] = dkt_acc[...].T
        dv_ref[...] = dvt_acc[...].T.astype(BF16)

    heads = pl.BlockSpec((None, W, LANES), lambda h, i, qe: (h, i, 0))
    return pl.pallas_call(
        body, name="attn_bwd",
        grid_spec=pltpu.PrefetchScalarGridSpec(
            num_scalar_prefetch=1, grid=(H, nb // G),
            in_specs=[pl.BlockSpec((S, LANES), lambda h, i, qe: (0, h)), pl.BlockSpec((S, LANES), lambda h, i, qe: (0, h)),
                      pl.BlockSpec((W, LANES), lambda h, i, qe: (i, h)), pl.BlockSpec((W, LANES), lambda h, i, qe: (i, h))],
            out_specs=[pl.BlockSpec((None, S, LANES), lambda h, i, qe: (h, 0, 0)), heads, heads],
            scratch_shapes=[pltpu.VMEM((LANES, W), F32), pltpu.VMEM((LANES, W), F32)]),
        out_shape=[jax.ShapeDtypeStruct((H, S, LANES), F32), jax.ShapeDtypeStruct((H, S, LANES), F32),
                   jax.ShapeDtypeStruct((H, S, LANES), BF16)],
        compiler_params=_params(2),
    )(qend, qa2, doa, ka, va)


def _attn_proj_bwd(dqt, dka, dva, qraw, kraw, dz, f, gq, gk):
    S, D = dz.shape
    H = D // HEAD_DIM
    tm = min(CONV_TILE, S)
    last = S // tm - 1
    tri = (lax.broadcasted_iota(jnp.int32, (tm, tm), 1) >= lax.broadcasted_iota(jnp.int32, (tm, tm), 0)).astype(BF16)

    def body(dq_ref, dk_ref, dv_ref, q_ref, k_ref, dz_ref, f_ref, gq_ref, gk_ref, tri_ref, ones_ref,
             dproj_ref, small_ref, carry, pairs):
        @pl.when(pl.program_id(0) == 0)
        def _():
            small_ref[...] = jnp.zeros((8, LANES), F32)
            carry[...] = jnp.zeros((8, LANES), F32)

        lane = _lane()

        def head_pair(j, acc):
            dcs, dgq, dgk = acc
            dq2, dk2 = [], []
            q_pair, k_pair = q_ref[j], k_ref[j]
            for parity in (0, 1):
                hd = 2 * j + parity
                own, a = _own(lane, parity), _aug(parity)
                dqf = dq_ref[hd]
                dqn = jnp.where(own, dqf * Q_SCALE, 0.0)
                d, dg = _head_rms_bwd(dqn, jnp.where(own, q_pair, 0.0), gq_ref[...], ones_ref[...])
                dq2.append(d)
                dgq = dgq + dg
                dkt = dk_ref[hd]
                dcs = dcs + jnp.where(lane == hd, _col(dqf, lane, a) - _col(dkt, lane, a + 3), 0.0)
                d, dg = _head_rms_bwd(jnp.where(own, dkt, 0.0), jnp.where(own, k_pair, 0.0), gk_ref[...], ones_ref[...])
                dk2.append(d)
                dgk = dgk + dg
            pairs[0, j] = _pair_tile(*dq2, lane).astype(BF16)
            pairs[1, j] = _pair_tile(*dk2, lane).astype(BF16)
            pairs[2, j] = _pair_tile(dv_ref[2 * j], dv_ref[2 * j + 1], lane)
            return dcs, dgq, dgk

        zero = jnp.zeros((1, LANES), F32)
        dcs, dgq, dgk = lax.fori_loop(0, H // 2, head_pair, (jnp.zeros((tm, LANES), F32), zero, zero))
        for part in range(3):
            for j in range(H // 2):
                dproj_ref[:, part * D + LANES * j:part * D + LANES * (j + 1)] = pairs[part, j]
        dproj_ref[:, 3 * D:4 * D] = dz_ref[...]
        dlogf = _dot01(tri_ref[...], dcs) + carry[0:1, :]
        carry[...] = jnp.broadcast_to(dlogf[0:1, :], (8, LANES))
        df = dlogf * (1.0 / (1.0 + jnp.exp(f_ref[...])))
        dproj_ref[:, 4 * D:4 * D + LANES] = df.astype(BF16)
        small_ref[0:1, :] += jnp.sum(df, axis=0, keepdims=True)
        small_ref[1:2, :] += dgq
        small_ref[2:3, :] += dgk

    W = 4 * D + LANES
    heads = pl.BlockSpec((H, tm, LANES), lambda i: (0, last - i, 0))
    head_pairs = pl.BlockSpec((H // 2, tm, LANES), lambda i: (0, last - i, 0))
    return pl.pallas_call(
        body, name="attn_proj_bwd", grid=(S // tm,),
        in_specs=[heads, heads, heads, head_pairs, head_pairs,
                  _rows(tm, D, last), _rows(tm, LANES, last), _whole((1, LANES)), _whole((1, LANES)),
                  _whole((tm, tm)), _whole((LANES, LANES))],
        out_specs=[_rows(tm, W, last), _whole((8, LANES))],
        out_shape=[jax.ShapeDtypeStruct((S, W), BF16), jax.ShapeDtypeStruct((8, LANES), F32)],
        scratch_shapes=[pltpu.VMEM((8, LANES), F32), pltpu.VMEM((3, H // 2, tm, LANES), BF16)],
        compiler_params=_params(1),
    )(dqt, dka, dva, qraw, kraw, dz, f, gq, gk, tri, jnp.ones((LANES, LANES), BF16))


def _matmul_tn(a, b, col0, n, tn, name, stacked=False, chip_sums=None):
    S, M = a.shape
    ts = min(TN_ROWS, S)
    off = col0 // tn
    grid = (n // tn, S // ts)
    k = len(chip_sums or [])

    def body(a_ref, b_ref, *rest):
        o_ref = rest[k]
        j, s = pl.program_id(0), pl.program_id(1)
        if k:
            copies = _chip_sum_copies(rest[:k], rest[k + 1:2 * k + 1], rest[2 * k + 1], rest[2 * k + 2])

            @pl.when((j == 0) & (s == 0))
            def _():
                for cp in copies:
                    cp.start()

        @pl.when(s == 0)
        def _():
            o_ref[...] = jnp.zeros((M, tn), F32)

        o_ref[...] += _dot_tn(a_ref[...], b_ref[...])

        if k:
            @pl.when((j == grid[0] - 1) & (s == grid[1] - 1))
            def _():
                for cp in copies:
                    cp.wait()

    if stacked:
        out_spec, out_shape = pl.BlockSpec((None, M, tn), lambda j, s: (j, 0, 0)), (n // tn, M, tn)
    else:
        out_spec, out_shape = pl.BlockSpec((M, tn), lambda j, s: (0, j)), (M, n)
    results = pl.pallas_call(
        body, name=name, grid=grid,
        in_specs=[pl.BlockSpec((ts, M), lambda j, s: (s, 0)), pl.BlockSpec((ts, tn), lambda j, s: (s, off + j))]
        + [ANY] * k,
        out_specs=[out_spec] + [ANY] * k,
        out_shape=[jax.ShapeDtypeStruct(out_shape, F32)]
        + [jax.ShapeDtypeStruct((3,) + g.shape[1:], g.dtype) for g in chip_sums or []],
        scratch_shapes=[pltpu.SemaphoreType.DMA((k, 3))] * (2 if k else 0),
        compiler_params=_params(2),
    )(a, b, *(chip_sums or []))
    return results[0] if chip_sums is None else (results[0], list(results[1:]))


def _adam_update(gv, w_ref, m_ref, v_ref, d_ref, m2_ref, v2_ref):
    m2 = ADAM_B1 * m_ref[...] + (1.0 - ADAM_B1) * gv
    v2 = ADAM_B2 * v_ref[...] + (1.0 - ADAM_B2) * (gv * gv)
    m2_ref[...] = m2
    v2_ref[...] = v2
    m_hat = m2 / (1.0 - ADAM_B1 ** ADAM_STEP)
    v_hat = v2 / (1.0 - ADAM_B2 ** ADAM_STEP)
    d_ref[...] = -ADAM_LR * (m_hat / (jnp.sqrt(v_hat) + ADAM_EPS) + ADAM_WD * w_ref[...])


def _adamw(w, g, m, v, name):
    r, c = w.shape
    tr = ROW_TILE if r % ROW_TILE == 0 else r

    def body(w_ref, g_ref, m_ref, v_ref, d_ref, m2_ref, v2_ref):
        _adam_update(g_ref[...], w_ref, m_ref, v_ref, d_ref, m2_ref, v2_ref)

    spec = _rows(tr, c)
    return pl.pallas_call(
        body, name=name, grid=(r // tr,), in_specs=[spec] * 4, out_specs=[spec] * 3,
        out_shape=[jax.ShapeDtypeStruct((r, c), F32)] * 3, compiler_params=_params(1),
    )(w, g, m, v)


def _adamw_halves(w, mine, other, m, v, core, name):
    r, c = mine.shape
    tr = ROW_TILE if r % ROW_TILE == 0 else r
    per = r // tr

    def body(core_ref, w_ref, mine_ref, other_ref, m_ref, v_ref, g_ref, d_ref, m2_ref, v2_ref):
        gv = jnp.where(pl.program_id(0) // per == core_ref[0], mine_ref[...], other_ref[...])
        g_ref[...] = gv
        _adam_update(gv, w_ref, m_ref, v_ref, d_ref, m2_ref, v2_ref)

    full = pl.BlockSpec((tr, c), lambda i, core: (i, 0))
    half = pl.BlockSpec((tr, c), lambda i, core: (i % per, 0))
    return pl.pallas_call(
        body, name=name,
        grid_spec=pltpu.PrefetchScalarGridSpec(num_scalar_prefetch=1, grid=(2 * per,),
                                               in_specs=[full, half, half, full, full], out_specs=[full] * 4),
        out_shape=[jax.ShapeDtypeStruct((2 * r, c), F32)] * 4, compiler_params=_params(1),
    )(core, w, mine, other, m, v)


def _no_reduction(by_chip, tag):
    return [], []


def _after_conv(conv_acts, x, target, g1, w_in, conv_w, w_out, g2, wa_in, b_f, gq, gk, wa_out,
                reduce_early=_no_reduction):
    S, D = x.shape
    H = D // HEAD_DIM
    ws = wa_in.shape[2]
    w_qkvz = jnp.concatenate([wa_in[0], wa_in[1], wa_in[2], wa_in[3][:, :4 * D - 3 * ws]], axis=1)
    wf = jnp.pad(wa_in[3][:, 4 * D - 3 * ws:], ((0, 0), (0, LANES - H)))
    bf = jnp.pad(b_f, ((0, 0), (0, LANES - H)))
    gq128 = jnp.concatenate([gq, gq], axis=1)
    gk128 = jnp.concatenate([gk, gk], axis=1)

    proj, h1, yc, y, x1 = conv_acts
    h2, qraw, kraw, z, f, c, rel, qa, ka, va, vt = _attn_front(x1, g2, w_qkvz, wf, bf, gq128, gk128)
    T = vt.shape[3]
    kstart, qend, bound = _skip_tables(c[:, :H], (c - rel)[:, :H], gq, gk, T, min(ATT_GROUP, S // T))
    o_aug, lse = lax.cond(2.0 * bound <= PLAIN_EXP_MAX, functools.partial(_attn_fwd, online_max=False),
                          functools.partial(_attn_fwd, online_max=True), kstart, qa, ka, vt)
    dx2, dx2b, o2b, dz, doa, qa2, loss = _attn_out(o_aug, lse.reshape(H, S), rel, z, x1, target, wa_out, qa)
    dqt, dka, dva = _attn_bwd(qend, qa2, doa, ka, va, T)
    dproj2, small = _attn_proj_bwd(dqt, dka, dva, qraw, kraw, dz, f, gq128, gk128)

    tn = min(1024, D)
    dwa_out = _matmul_tn(o2b, dx2b, 0, D, tn, "dw_attn_out")
    dw_main = _matmul_tn(h2, dproj2, 0, 4 * D, 2 * tn, "dw_attn_in")
    dw_f = _matmul_tn(h2, dproj2, 4 * D, LANES, LANES, "dw_attn_f")
    dwa_in = jnp.stack([dw_main[:, 0:ws], dw_main[:, ws:2 * ws], dw_main[:, 2 * ws:3 * ws],
                        jnp.concatenate([dw_main[:, 3 * ws:], dw_f[:, :H]], axis=1)])
    attn_f32, attn_bf16 = reduce_early([dwa_in, dwa_out.reshape(4, D // 4, D)], "attn")
    dproj1, dx, dx1b, dg1, dcw, dg2 = _conv_bwd(dproj2, w_qkvz, wf, x1, g2, dx2, x, g1, w_in, w_out, conv_w, proj, yc)
    dw_in, attn_arrived = _matmul_tn(h1, dproj1, 0, 4 * D, D, "dw_conv_in", stacked=True, chip_sums=attn_bf16)
    in_f32, in_bf16 = reduce_early([dw_in], "conv_in")
    dw_out, in_arrived = _matmul_tn(y, dx1b, 0, D, tn, "dw_conv_out", chip_sums=in_bf16)
    grads = dict(conv_norm_g=dg1, conv_w_in=dw_in, conv_w=dcw, conv_w_out=dw_out, attn_norm_g=dg2,
                 attn_w_in=dwa_in, attn_b_f=small[0:1, :H],
                 attn_q_norm_g=small[1:2, :HEAD_DIM] + small[1:2, HEAD_DIM:],
                 attn_k_norm_g=small[2:3, :HEAD_DIM] + small[2:3, HEAD_DIM:], attn_w_out=dwa_out)
    return loss[0, 0], dx, grads, (attn_f32 + in_f32, attn_arrived + in_arrived)


def _coords():
    return lax.axis_index("x"), lax.axis_index("y"), lax.axis_index("c")


def _at(ref, idx):
    return ref.at[idx] if idx else ref


def _other_chips(x, y):
    return [(1 - x, y), (x, 1 - y), (1 - x, 1 - y)]


def _gather_plan(src, out, send, recv):
    x, y, c = _coords()
    mine = 2 * x + y
    sibling = (x, y, 1 - c)
    others = [(a, k, 2 * px + py, (px, py)) for a in range(len(src)) for k, (px, py) in enumerate(_other_chips(x, y))]

    def copy(a, k, chip, half, to, source=None):
        dst = out[a].at[chip, half]
        return pltpu.make_async_remote_copy(src_ref=dst if source is None else source, dst_ref=dst,
                                            send_sem=send.at[a, k], recv_sem=recv.at[a, k],
                                            device_id=to, device_id_type=MESH)

    first = [copy(a, k, mine, c, (*chip, c), source=src[a].at[c]) for a, k, _, chip in others]
    own = [pltpu.make_async_remote_copy(src_ref=src[a], dst_ref=out[a].at[mine], send_sem=send.at[a, 6],
                                        recv_sem=recv.at[a, 6], device_id=sibling, device_id_type=MESH)
           for a in range(len(src))]
    passed = [copy(a, 3 + k, slot, c, sibling) for a, k, slot, _ in others]

    def start():
        for cp in first + own:
            cp.start()

    def forward():
        for (a, k, slot, _), cp in zip(others, passed):
            copy(a, k, slot, c, (x, y, c)).wait_recv()
            cp.start()

    def finish():
        for a, k, slot, _ in others:
            copy(a, 3 + k, slot, 1 - c, (x, y, c)).wait_recv()
        for cp in own:
            cp.wait_recv()
        for cp in first + passed + own:
            cp.wait_send()

    return start, forward, finish


def _all_gather(halved, whole):
    nh, nw = len(halved), len(whole)

    def body(*refs):
        src_h, src_w = refs[:nh], refs[nh:nh + nw]
        out_h, out_w = refs[nh + nw:2 * nh + nw], refs[2 * nh + nw:2 * (nh + nw)]
        send_h, recv_h, send_w, recv_w = refs[2 * (nh + nw):]
        x, y, c = _coords()
        mine = 2 * x + y
        chips = _other_chips(x, y)

        def copy_w(a, k, chip, to):
            return pltpu.make_async_remote_copy(src_ref=src_w[a], dst_ref=out_w[a].at[chip],
                                                send_sem=send_w.at[a, k], recv_sem=recv_w.at[a, k],
                                                device_id=to, device_id_type=MESH)

        start, forward, finish = _gather_plan(src_h, out_h, send_h, recv_h)
        small = [copy_w(a, k, mine, (*chip, c)) for a in range(nw) for k, chip in enumerate(chips)]
        small += [copy_w(a, 3, mine, (x, y, 1 - c)) for a in range(nw)]
        start()
        for cp in small:
            cp.start()
        forward()
        finish()
        for a in range(nw):
            for k, (px, py) in enumerate(chips):
                copy_w(a, k, 2 * px + py, (x, y, c)).wait_recv()
            copy_w(a, 3, mine, (x, y, c)).wait_recv()
        for cp in small:
            cp.wait_send()

    out_shape = [jax.ShapeDtypeStruct((4,) + a.shape, a.dtype) for a in list(halved) + list(whole)]
    return pl.pallas_call(
        body, name="gather_weights", in_specs=[ANY] * (nh + nw), out_specs=[ANY] * (nh + nw), out_shape=out_shape,
        scratch_shapes=[pltpu.SemaphoreType.DMA((nh, 7)), pltpu.SemaphoreType.DMA((nh, 7)),
                        pltpu.SemaphoreType.DMA((nw, 4)), pltpu.SemaphoreType.DMA((nw, 4))],
    )(*halved, *whole)


def _exchange(name, srcs, lands, copies, local_copies):
    ns, nl, n, nloc = len(srcs), len(lands), len(copies), len(local_copies)

    def body(*refs):
        src, land = refs[:ns], refs[ns:ns + nl]
        send, recv, local_sem = refs[ns + nl:]
        me = _coords()
        started = []
        for k, (si, s_at, li, l_at, ci) in enumerate(local_copies):
            cp = pltpu.make_async_copy(_at(src[si], s_at(*me)), _at(land[li], l_at(*me)), local_sem.at[k])
            cp.start()
            started.append(cp)
        remote = []
        for k, (si, s_at, li, l_at, peer) in enumerate(copies):
            cp = pltpu.make_async_remote_copy(src_ref=_at(src[si], s_at(*me)), dst_ref=_at(land[li], l_at(*me)),
                                              send_sem=send.at[k], recv_sem=recv.at[k],
                                              device_id=peer(*me), device_id_type=MESH)
            cp.start()
            remote.append(cp)
        for cp in remote:
            cp.wait()
        for cp in started:
            cp.wait()

    return pl.pallas_call(
        body, name=name, in_specs=[ANY] * ns, out_specs=[ANY] * nl, out_shape=list(lands),
        scratch_shapes=[pltpu.SemaphoreType.DMA((n,)), pltpu.SemaphoreType.DMA((n,)),
                        pltpu.SemaphoreType.DMA((max(nloc, 1),))],
    )(*srcs)


def _add_pairs(a, b, core, name):
    _, r, cols = b.shape
    tr = ROW_TILE if r % ROW_TILE == 0 else r

    def body(core_ref, a_ref, b_ref, o_ref, ob_ref):
        s = a_ref[...] + b_ref[...]
        o_ref[...] = s
        ob_ref[...] = s.astype(BF16)

    spec = pl.BlockSpec((None, tr, cols), lambda j, i, core: (j, i, 0))
    return pl.pallas_call(
        body, name=name,
        grid_spec=pltpu.PrefetchScalarGridSpec(
            num_scalar_prefetch=1, grid=(4, r // tr),
            in_specs=[pl.BlockSpec((None, None, tr, cols), lambda j, i, core: (j, core[0], i, 0)), spec],
            out_specs=[spec, spec]),
        out_shape=[jax.ShapeDtypeStruct(b.shape, F32), jax.ShapeDtypeStruct(b.shape, BF16)],
        compiler_params=_params(2),
    )(core, a, b)


def _sum_chips(own, landed, name):
    _, r, cols = landed.shape
    tr = ROW_TILE if r % ROW_TILE == 0 else r

    def body(own_ref, land_ref, o_ref):
        acc = own_ref[...]
        for j in range(3):
            acc = acc + land_ref[j].astype(F32)
        o_ref[...] = acc

    return pl.pallas_call(
        body, name=name, grid=(r // tr,),
        in_specs=[_rows(tr, cols), pl.BlockSpec((3, tr, cols), lambda i: (0, i, 0))], out_specs=_rows(tr, cols),
        out_shape=jax.ShapeDtypeStruct((r, cols), F32), compiler_params=_params(1),
    )(own, landed)


def _sum_devices(landed, name):
    def body(l_ref, o_ref):
        acc = l_ref[0]
        for j in range(1, 8):
            acc = acc + l_ref[j]
        o_ref[...] = acc

    return pl.pallas_call(body, name=name, out_shape=jax.ShapeDtypeStruct(landed.shape[1:], F32))(landed)


CHIP_FLIPS = [(1, 0), (0, 1), (1, 1)]


def _flip(fx, fy, fc):
    return lambda x, y, c: (x ^ fx, y ^ fy, c ^ fc)


def _sum_cores(big, small, tag):
    nb = len(big)
    c = lax.axis_index("c")
    copies = [(a, (lambda j: lambda x, y, c: (j, 1 - c))(j), a, (lambda j: lambda x, y, c: (j,))(j), _flip(0, 0, 1))
              for a in range(nb) for j in range(4)]
    lands = [jax.ShapeDtypeStruct((4,) + g.shape[2:], F32) for g in big]
    srcs, local = list(big), []
    if small is not None:
        flips = [(fx, fy, fc) for fx in (0, 1) for fy in (0, 1) for fc in (0, 1) if fx or fy or fc]
        copies += [(nb, lambda x, y, c: (), nb, lambda x, y, c: (4 * x + 2 * y + c,), _flip(*f)) for f in flips]
        lands.append(jax.ShapeDtypeStruct((8,) + small.shape, F32))
        local = [(nb, lambda x, y, c: (), nb, lambda x, y, c: (4 * x + 2 * y + c,), None)]
        srcs.append(small)
    landed = _exchange("swap_halves_" + tag, srcs, lands, copies, local)
    small_sum = None if small is None else _sum_devices(landed[nb], "sum_small")
    core = jnp.reshape(c, (1,)).astype(jnp.int32)
    sums = [_add_pairs(big[a], landed[a], core, f"add_cores_{tag}_{a}") for a in range(nb)]
    return [s for s, _ in sums], [sb for _, sb in sums], small_sum


def _chip_sum_copies(sums, landed, send, recv):
    x, y, c = _coords()
    return [pltpu.make_async_remote_copy(src_ref=sums[a].at[2 * (x ^ fx) + (y ^ fy)], dst_ref=landed[a].at[k],
                                         send_sem=send.at[a, k], recv_sem=recv.at[a, k],
                                         device_id=(x ^ fx, y ^ fy, c), device_id_type=MESH)
            for a in range(len(sums)) for k, (fx, fy) in enumerate(CHIP_FLIPS)]


def _send_chip_sums(chip_bf16):
    n = len(chip_bf16)

    def body(*refs):
        copies = _chip_sum_copies(refs[:n], refs[n:2 * n], refs[2 * n], refs[2 * n + 1])
        for cp in copies:
            cp.start()
        for cp in copies:
            cp.wait()

    return pl.pallas_call(
        body, name="send_chip_sums", in_specs=[ANY] * n, out_specs=[ANY] * n,
        out_shape=[jax.ShapeDtypeStruct((3,) + g.shape[1:], BF16) for g in chip_bf16],
        scratch_shapes=[pltpu.SemaphoreType.DMA((n, 3)), pltpu.SemaphoreType.DMA((n, 3))],
    )(*chip_bf16)


def _sum_chips_and_share(chip_f32, landed):
    x, y, _ = _coords()
    totals = [_sum_chips(lax.dynamic_index_in_dim(f, 2 * x + y, axis=0, keepdims=False), l, f"sum_chips_{a}")
              for a, (f, l) in enumerate(zip(chip_f32, landed))]
    copies = [(a, lambda x, y, c: (), a, lambda x, y, c: (), _flip(0, 0, 1)) for a in range(len(totals))]
    lands = [jax.ShapeDtypeStruct(t.shape, F32) for t in totals]
    return list(zip(totals, _exchange("swap_sums", totals, lands, copies, [])))


def kernel(x, conv_norm_g, conv_w_in, conv_w, conv_w_out, attn_norm_g, attn_w_in, attn_b_f, attn_q_norm_g, attn_k_norm_g, attn_w_out, loss_target, m_conv_norm_g, m_conv_w_in, m_conv_w, m_conv_w_out, m_attn_norm_g, m_attn_w_in, m_attn_b_f, m_attn_q_norm_g, m_attn_k_norm_g, m_attn_w_out, v_conv_norm_g, v_conv_w_in, v_conv_w, v_conv_w_out, v_attn_norm_g, v_attn_w_in, v_attn_b_f, v_attn_q_norm_g, v_attn_k_norm_g, v_attn_w_out):
    xi, yi, _ = _coords()
    chip = 2 * xi + yi
    D = x.shape[2]
    H = D // HEAD_DIM
    names = ["conv_norm_g", "conv_w_in", "conv_w", "conv_w_out", "attn_norm_g", "attn_w_in", "attn_b_f",
             "attn_q_norm_g", "attn_k_norm_g", "attn_w_out"]
    weights = dict(zip(names, [conv_norm_g, conv_w_in, conv_w, conv_w_out, attn_norm_g, attn_w_in, attn_b_f,
                               attn_q_norm_g, attn_k_norm_g, attn_w_out]))
    m_in = dict(zip(names, [m_conv_norm_g, m_conv_w_in, m_conv_w, m_conv_w_out, m_attn_norm_g, m_attn_w_in,
                            m_attn_b_f, m_attn_q_norm_g, m_attn_k_norm_g, m_attn_w_out]))
    v_in = dict(zip(names, [v_conv_norm_g, v_conv_w_in, v_conv_w, v_conv_w_out, v_attn_norm_g, v_attn_w_in,
                            v_attn_b_f, v_attn_q_norm_g, v_attn_k_norm_g, v_attn_w_out]))
    weights = {k: w[0] for k, w in weights.items()}
    m_in = {k: w[0] for k, w in m_in.items()}
    v_in = {k: w[0] for k, w in v_in.items()}

    big_names = ["conv_w_in", "attn_w_in", "conv_w_out", "attn_w_out"]
    halved = {k: weights[k].astype(BF16).reshape(2, weights[k].shape[0] // 2, weights[k].shape[1]) for k in big_names}
    q = D // 4
    small_w = jnp.concatenate([weights["conv_w"], weights["attn_norm_g"][None, :], jnp.zeros((4, q), F32)], axis=0)
    g_in, g_out, g_small = _all_gather([halved["conv_w_in"], halved["conv_w_out"]], [small_w])
    w_in = g_in.reshape(4, D, D)
    w_out = g_out.reshape(D, D)
    conv_w_full = g_small[:, 0:3, :].transpose(1, 0, 2).reshape(3, D)
    attn_g_full = g_small[:, 3, :].reshape(1, D)
    g1 = weights["conv_norm_g"][None, :]

    conv_acts, (ga_in, ga_out) = _conv_fwd(x[0], g1, w_in, conv_w_full, w_out, [halved["attn_w_in"], halved["attn_w_out"]])
    def core_sums(by_chip, tag):
        f32, bf16, _ = _sum_cores([g.reshape(4, 2, g.shape[1] // 2, g.shape[2]) for g in by_chip], None, tag)
        return f32, bf16

    loss_part, grad_x, grads, (early_f32, early_arrived) = _after_conv(
        conv_acts, x[0], loss_target[0], g1, w_in, conv_w_full, w_out, attn_g_full, ga_in.reshape(4, D, D + H // 4),
        weights["attn_b_f"][None, :], weights["attn_q_norm_g"][None, :], weights["attn_k_norm_g"][None, :],
        ga_out.reshape(D, D), reduce_early=core_sums)

    tail = jnp.concatenate([grads["attn_b_f"], grads["attn_q_norm_g"], grads["attn_k_norm_g"],
                            jnp.reshape(loss_part, (1, 1)), jnp.zeros((1, D - H - 2 * HEAD_DIM - 1), F32)], axis=1)
    small = jnp.concatenate([grads["conv_norm_g"], grads["conv_w"], grads["attn_norm_g"], tail,
                             jnp.zeros((2, D), F32)], axis=0)
    out_f32, out_bf16, small_sum = _sum_cores([grads["conv_w_out"].reshape(4, 2, D // 8, D)], small, "conv_out")
    big_names = ["attn_w_in", "attn_w_out", "conv_w_in", "conv_w_out"]
    reduced = _sum_chips_and_share(early_f32 + out_f32, early_arrived + list(_send_chip_sums(out_bf16)))
    final = {}
    final["conv_norm_g"] = small_sum[0]
    final["conv_w"] = lax.dynamic_slice_in_dim(small_sum[1:4], chip * q, q, axis=1)
    final["attn_norm_g"] = lax.dynamic_slice_in_dim(small_sum[4], chip * q, q, axis=0)
    final["attn_b_f"] = small_sum[5, :H]
    final["attn_q_norm_g"] = small_sum[5, H:H + HEAD_DIM]
    final["attn_k_norm_g"] = small_sum[5, H + HEAD_DIM:H + 2 * HEAD_DIM]
    loss = small_sum[5, H + 2 * HEAD_DIM]

    delta, new_m, new_v = {}, {}, {}
    core = jnp.reshape(lax.axis_index("c"), (1,)).astype(jnp.int32)
    for k, (mine, other) in zip(big_names, reduced):
        final[k], delta[k], new_m[k], new_v[k] = _adamw_halves(weights[k], mine, other, m_in[k], v_in[k], core,
                                                               "adamw_" + k)
    for k in names:
        if k in big_names:
            continue
        shape = weights[k].shape
        as2d = (lambda a: a.reshape(1, -1)) if len(shape) == 1 else (lambda a: a)
        d, m2, v2 = _adamw(as2d(weights[k]), as2d(final[k]), as2d(m_in[k]), as2d(v_in[k]), "adamw_" + k)
        delta[k], new_m[k], new_v[k] = d.reshape(shape), m2.reshape(shape), v2.reshape(shape)
    lead = lambda a: a[None]
    return (loss, grad_x[None], *[lead(final[k]) for k in names], *[lead(delta[k]) for k in names],
            *[lead(new_m[k]) for k in names], *[lead(new_v[k]) for k in names])
```

```python
import functools

import jax
import jax.numpy as jnp
from jax import lax
from jax.experimental import pallas as pl
from jax.experimental.pallas import tpu as pltpu

F32 = jnp.float32
BF16 = jnp.bfloat16
HEAD_DIM = 64
LANES = 128
RMS_EPS = 1e-6
NEG = -1e30
Q_SCALE = 0.125
ROW_TILE = 256
CONV_TILE = 512
ATT_GROUP = 4
SKIP_LOG = 106.0
PLAIN_EXP_MAX = 60.0
TN_ROWS = 2048
VMEM_LIMIT = 56 << 20
ADAM_LR, ADAM_B1, ADAM_B2, ADAM_EPS, ADAM_WD, ADAM_STEP = 0.001, 0.9, 0.999, 1e-08, 0.01, 10
MESH = pl.DeviceIdType.MESH
ANY = pl.BlockSpec(memory_space=pl.ANY)


def _lane():
    return lax.broadcasted_iota(jnp.int32, (1, LANES), 1)


def _split3(x):
    hi = x.astype(BF16).astype(F32)
    r = x - hi
    mid = r.astype(BF16).astype(F32)
    lo = (r - mid).astype(BF16).astype(F32)
    return hi, mid, lo


STAT_STRIDE = 16
ONE_LANE = 3 * STAT_STRIDE


def _pack3(x, lane, one):
    hi, mid, lo = _split3(x)
    packed = hi + pltpu.roll(mid, STAT_STRIDE, 1) + pltpu.roll(lo, 2 * STAT_STRIDE, 1)
    return jnp.where(lane == ONE_LANE, one, packed).astype(BF16)


def _scatter_matrices(H):
    rows = lax.broadcasted_iota(jnp.int32, (LANES, H * LANES), 0)
    cols = lax.broadcasted_iota(jnp.int32, (LANES, H * LANES), 1)
    head, within = cols // LANES, cols % LANES
    extra = within - _aug(head % 2)
    term = (rows < ONE_LANE) & (rows % STAT_STRIDE == head)
    first = ((term & (extra == rows // STAT_STRIDE)) | ((rows == ONE_LANE) & (extra >= 3) & (extra < 6)))
    second = ((term & (extra - 3 == rows // STAT_STRIDE)) | ((rows == ONE_LANE) & (extra >= 0) & (extra < 3)))
    return first.astype(BF16), second.astype(BF16)


def _col(x, lane, idx):
    return jnp.sum(jnp.where(lane == idx, x, 0.0), axis=1, keepdims=True)


def _feat(parity):
    return HEAD_DIM * parity


def _aug(parity):
    return HEAD_DIM * (1 - parity)


def _own(lane, parity):
    return (lane >= _feat(parity)) & (lane < _feat(parity) + HEAD_DIM)


def _head_tile(ref, hd, lane):
    j = hd // 2
    return jnp.where(_own(lane, hd % 2), ref[:, LANES * j:LANES * (j + 1)], 0.0)


def _pair_tile(even, odd, lane):
    return jnp.where(lane < HEAD_DIM, even, odd)


def _sigmoid(x):
    return 0.5 * jnp.tanh(0.5 * x) + 0.5


def _dot(a, b):
    return jnp.dot(a, b, preferred_element_type=F32)


def _dot_nt(a, b):
    return lax.dot_general(a, b, (((1,), (1,)), ((), ())), preferred_element_type=F32)


def _dot_tn(a, b):
    return lax.dot_general(a, b, (((0,), (0,)), ((), ())), preferred_element_type=F32)


def _dot01(tri, x):
    hi, mid, lo = _split3(x)
    return _dot(tri, hi.astype(BF16)) + _dot(tri, mid.astype(BF16)) + _dot(tri, lo.astype(BF16))


def _rms_bwd(dh, x, g):
    inv = lax.rsqrt(jnp.mean(x * x, axis=-1, keepdims=True) + RMS_EPS)
    xh = x * inv
    dxn = dh * g
    dx = inv * (dxn - xh * jnp.mean(dxn * xh, axis=-1, keepdims=True))
    return dx, jnp.sum(dh * xh, axis=0, keepdims=True)


def _head_rms_bwd(dn, t, g, ones):
    sq = t * t
    hi = sq.astype(BF16)
    lo = (sq - hi.astype(F32)).astype(BF16)
    inv = lax.rsqrt((_dot(hi, ones) + _dot(lo, ones)) * (1.0 / HEAD_DIM) + RMS_EPS)
    th = t * inv
    gd = dn * g
    d = inv * (gd - th * (jnp.sum(gd * th, axis=1, keepdims=True) * (1.0 / HEAD_DIM)))
    return d, jnp.sum(dn * th, axis=0, keepdims=True)


def _params(n_grid):
    return pltpu.CompilerParams(dimension_semantics=("arbitrary",) * n_grid, vmem_limit_bytes=VMEM_LIMIT)


def _rows(tm, cols, rev=None):
    if rev is None:
        return pl.BlockSpec((tm, cols), lambda i: (i, 0))
    return pl.BlockSpec((tm, cols), lambda i: (rev - i, 0))


def _whole(shape, buffers=None):
    mode = {} if buffers is None else dict(pipeline_mode=pl.Buffered(buffers))
    return pl.BlockSpec(shape, lambda *_: (0,) * len(shape), **mode)


def _conv_fwd(x, g1, w_in, conv_w, w_out, later):
    S, D = x.shape
    tm = min(CONV_TILE, S)
    sub = min(ROW_TILE, tm)
    steps = S // tm
    n = len(later)

    def body(x_ref, g_ref, win_ref, cw_ref, wout_ref, *rest):
        shard_refs, rest = rest[:n], rest[n:]
        proj_ref, h_ref, yc_ref, y_ref, x1_ref = rest[:5]
        gathered_refs, rest = rest[5:5 + n], rest[5 + n:]
        prev_u = rest[0]
        if n:
            start, forward, finish = _gather_plan(shard_refs, gathered_refs, rest[1], rest[2])
            pl.when(pl.program_id(0) == 0)(start)
            pl.when(pl.program_id(0) == steps // 2)(forward)

        @pl.when(pl.program_id(0) == 0)
        def _():
            prev_u[...] = jnp.zeros((sub, D), F32)

        for r in range(0, tm, sub):
            rows = slice(r, r + sub)
            xv = x_ref[rows, :]
            inv = lax.rsqrt(jnp.mean(xv * xv, axis=-1, keepdims=True) + RMS_EPS)
            h = (xv * inv * g_ref[...]).astype(BF16)
            h_ref[rows, :] = h
            for j in range(4):
                proj_ref[rows, j * D:(j + 1) * D] = _dot(h, win_ref[j])
            u = proj_ref[rows, D:2 * D] * proj_ref[rows, 2 * D:3 * D]
            pu = prev_u[...]
            row = lax.broadcasted_iota(jnp.int32, (sub, 1), 0)
            u1 = jnp.where(row < 1, pltpu.roll(pu, 1, 0), pltpu.roll(u, 1, 0))
            u2 = jnp.where(row < 2, pltpu.roll(pu, 2, 0), pltpu.roll(u, 2, 0))
            prev_u[...] = u
            w = cw_ref[...]
            yc = w[2:3] * u + w[1:2] * u1 + w[0:1] * u2
            yc_ref[rows, :] = yc
            z = proj_ref[rows, 3 * D:4 * D]
            y = (proj_ref[rows, 0:D] * yc * (z * _sigmoid(z))).astype(BF16)
            y_ref[rows, :] = y
            x1_ref[rows, :] = xv + _dot(y, wout_ref[...])

        if n:
            pl.when(pl.program_id(0) == steps - 1)(finish)

    results = pl.pallas_call(
        body, name="conv_fwd", grid=(steps,),
        in_specs=[_rows(tm, D), _whole((1, D)), _whole((4, D, D), 1), _whole((3, D)), _whole((D, D), 1)] + [ANY] * n,
        out_specs=[_rows(tm, 4 * D), _rows(tm, D), _rows(tm, D), _rows(tm, D), _rows(tm, D)] + [ANY] * n,
        out_shape=[jax.ShapeDtypeStruct((S, 4 * D), F32), jax.ShapeDtypeStruct((S, D), BF16),
                   jax.ShapeDtypeStruct((S, D), F32), jax.ShapeDtypeStruct((S, D), BF16),
                   jax.ShapeDtypeStruct((S, D), F32)] + [jax.ShapeDtypeStruct((4,) + a.shape, a.dtype) for a in later],
        scratch_shapes=[pltpu.VMEM((sub, D), F32)] + [pltpu.SemaphoreType.DMA((n, 7))] * (2 if n else 0),
        compiler_params=_params(1),
    )(x, g1, w_in, conv_w, w_out, *later)
    return results[:5], results[5:]


def _conv_bwd(dproj2, wa, wf, x1, g2, dx2, x, g1, w_in, w_out, conv_w, proj, yc):
    S, D = x.shape
    tm = min(ROW_TILE, S)
    sub = min(ROW_TILE, tm)
    last = S // tm - 1

    def body(dp2_ref, wa_ref, wf_ref, x1_ref, g2_ref, dx2_ref, x_ref, g_ref, win_ref, wout_ref, cw_ref, proj_ref, yc_ref,
             dproj_ref, dx_ref, dx1b_ref, dg_ref, dcw_ref, dg2_ref, next_d):
        @pl.when(pl.program_id(0) == 0)
        def _():
            dg_ref[...] = jnp.zeros((1, D), F32)
            dcw_ref[...] = jnp.zeros((3, D), F32)
            dg2_ref[...] = jnp.zeros((1, D), F32)
            next_d[...] = jnp.zeros((sub, D), F32)

        for r in range(tm - sub, -1, -sub):
            rows = slice(r, r + sub)
            dh2 = _dot_nt(dp2_ref[rows, 0:4 * D], wa_ref[...]) + _dot_nt(dp2_ref[rows, 4 * D:4 * D + LANES], wf_ref[...])
            dxn2, dg2 = _rms_bwd(dh2, x1_ref[rows, :], g2_ref[...])
            dg2_ref[...] += dg2
            dx1v = dx2_ref[rows, :] + dxn2
            dx1b = dx1v.astype(BF16)
            dx1b_ref[rows, :] = dx1b
            dy = _dot_nt(dx1b, wout_ref[...])
            b = proj_ref[rows, 0:D]
            c = proj_ref[rows, D:2 * D]
            xin = proj_ref[rows, 2 * D:3 * D]
            z = proj_ref[rows, 3 * D:4 * D]
            sg = _sigmoid(z)
            sz = z * sg
            ycv = yc_ref[rows, :]
            d0 = dy * b * sz
            dproj_ref[rows, 0:D] = (dy * ycv * sz).astype(BF16)
            dproj_ref[rows, 3 * D:4 * D] = (dy * b * ycv * (sg * (1.0 + z * (1.0 - sg)))).astype(BF16)
            nd = next_d[...]
            row = lax.broadcasted_iota(jnp.int32, (sub, 1), 0)
            d1 = jnp.where(row >= sub - 1, pltpu.roll(nd, sub - 1, 0), pltpu.roll(d0, sub - 1, 0))
            d2 = jnp.where(row >= sub - 2, pltpu.roll(nd, sub - 2, 0), pltpu.roll(d0, sub - 2, 0))
            next_d[...] = d0
            w = cw_ref[...]
            du = w[2:3] * d0 + w[1:2] * d1 + w[0:1] * d2
            u = c * xin
            dcw_ref[2:3, :] += jnp.sum(d0 * u, axis=0, keepdims=True)
            dcw_ref[1:2, :] += jnp.sum(d1 * u, axis=0, keepdims=True)
            dcw_ref[0:1, :] += jnp.sum(d2 * u, axis=0, keepdims=True)
            dproj_ref[rows, D:2 * D] = (du * xin).astype(BF16)
            dproj_ref[rows, 2 * D:3 * D] = (du * c).astype(BF16)
            dh = _dot_nt(dproj_ref[rows, 0:D], win_ref[0])
            for j in range(1, 4):
                dh = dh + _dot_nt(dproj_ref[rows, j * D:(j + 1) * D], win_ref[j])
            dxn, dg = _rms_bwd(dh, x_ref[rows, :], g_ref[...])
            dx_ref[rows, :] = dx1v + dxn
            dg_ref[...] += dg

    return pl.pallas_call(
        body, name="conv_bwd", grid=(S // tm,),
        in_specs=[_rows(tm, 4 * D + LANES, last), _whole((D, 4 * D), 1), _whole((D, LANES), 1), _rows(tm, D, last),
                  _whole((1, D)), _rows(tm, D, last),
                  _rows(tm, D, last), _whole((1, D)), _whole((4, D, D), 1), _whole((D, D), 1),
                  _whole((3, D)), _rows(tm, 4 * D, last), _rows(tm, D, last)],
        out_specs=[_rows(tm, 4 * D, last), _rows(tm, D, last), _rows(tm, D, last), _whole((1, D)), _whole((3, D)),
                   _whole((1, D))],
        out_shape=[jax.ShapeDtypeStruct((S, 4 * D), BF16), jax.ShapeDtypeStruct((S, D), F32),
                   jax.ShapeDtypeStruct((S, D), BF16), jax.ShapeDtypeStruct((1, D), F32),
                   jax.ShapeDtypeStruct((3, D), F32), jax.ShapeDtypeStruct((1, D), F32)],
        scratch_shapes=[pltpu.VMEM((sub, D), F32)],
        compiler_params=_params(1),
    )(dproj2, wa, wf, x1, g2, dx2, x, g1, w_in, w_out, conv_w, proj, yc)


def _attn_front(x1, g2, w, wf, bf, gq, gk):
    S, D = x1.shape
    H = D // HEAD_DIM
    tm = min(ROW_TILE, S)
    tri = (lax.broadcasted_iota(jnp.int32, (tm, tm), 1) <= lax.broadcasted_iota(jnp.int32, (tm, tm), 0)).astype(BF16)

    def body(x_ref, g_ref, w_ref, wf_ref, bf_ref, gq_ref, gk_ref, tri_ref, first_ref, second_ref,
             h_ref, qh_ref, kh_ref, z_ref, f_ref, ends_ref, rel_ref, qa_ref, ka_ref, va_ref, vt_ref,
             carry, v_s, qraw_ref, kraw_ref):
        @pl.when(pl.program_id(0) == 0)
        def _():
            carry[...] = jnp.zeros((8, LANES), F32)

        xv = x_ref[...]
        inv = lax.rsqrt(jnp.mean(xv * xv, axis=-1, keepdims=True) + RMS_EPS)
        h = (xv * inv * g_ref[...]).astype(BF16)
        h_ref[...] = h
        qraw_ref[...] = _dot(h, w_ref[:, 0:D])
        kraw_ref[...] = _dot(h, w_ref[:, D:2 * D])
        v_s[...] = _dot(h, w_ref[:, 2 * D:3 * D])
        z_ref[...] = _dot(h, w_ref[:, 3 * D:4 * D])
        lane = _lane()
        f = _dot(h, wf_ref[...]) + bf_ref[...]
        f_ref[...] = f
        logf = jnp.where(lane < H, jnp.minimum(f, 0.0) - jnp.log(1.0 + jnp.exp(-jnp.abs(f))), 0.0)
        cs = _dot01(tri_ref[...], logf) + carry[0:1, :]
        carry[...] = jnp.broadcast_to(cs[tm - 1:tm, :], (8, LANES))
        diags = jnp.zeros((tm, LANES), F32)
        for hd in range(H):
            sl = slice(LANES * hd, LANES * (hd + 1))
            a = _aug(hd % 2)
            if hd % 2 == 0:
                qh_ref[hd // 2] = qraw_ref[:, LANES * (hd // 2):LANES * (hd // 2 + 1)]
                kh_ref[hd // 2] = kraw_ref[:, LANES * (hd // 2):LANES * (hd // 2 + 1)]
            qt = _head_tile(qraw_ref, hd, lane)
            qn = qt * lax.rsqrt(jnp.sum(qt * qt, axis=1, keepdims=True) * (1.0 / HEAD_DIM) + RMS_EPS) * gq_ref[...]
            kt = _head_tile(kraw_ref, hd, lane)
            kn = kt * lax.rsqrt(jnp.sum(kt * kt, axis=1, keepdims=True) * (1.0 / HEAD_DIM) + RMS_EPS) * gk_ref[...]
            diags = diags + jnp.where(lane == hd, jnp.sum(qn * kn, axis=1, keepdims=True) * Q_SCALE, 0.0)
            qa_ref[:, sl] = (qn * Q_SCALE).astype(BF16)
            ka_ref[:, sl] = kn.astype(BF16)
            va = jnp.where((lane >= a) & (lane < a + 3), 1.0, _head_tile(v_s, hd, lane))
            va_ref[:, sl] = va.astype(BF16)
            vt_ref[hd] = va.T.astype(BF16)
        rel = cs - diags
        rel_ref[...] = rel
        row = lax.broadcasted_iota(jnp.int32, (8, LANES), 0)
        ends_ref[...] = jnp.where(row == 0, cs[0:1, :],
                                  jnp.where(row == 1, cs[tm - 1:tm, :], jnp.min(diags, axis=0, keepdims=True)))
        qa_ref[...] += _dot(_pack3(rel, lane, 1.0), first_ref[...]).astype(BF16)
        ka_ref[...] += _dot(_pack3(-cs, lane, 1.0), second_ref[...]).astype(BF16)

    nb = S // tm
    heads = pl.BlockSpec((H // 2, tm, LANES), lambda i: (0, i, 0))
    return pl.pallas_call(
        body, name="attn_front", grid=(nb,),
        in_specs=[_rows(tm, D), _whole((1, D)), _whole((D, 4 * D)), _whole((D, LANES)), _whole((1, LANES)),
                  _whole((1, LANES)), _whole((1, LANES)), _whole((tm, tm)), _whole((LANES, H * LANES)),
                  _whole((LANES, H * LANES))],
        out_specs=[_rows(tm, D), heads, heads, _rows(tm, D), _rows(tm, LANES),
                   pl.BlockSpec((None, 8, LANES), lambda i: (i, 0, 0)), _rows(tm, LANES),
                   _rows(tm, H * LANES), _rows(tm, H * LANES), _rows(tm, H * LANES),
                   pl.BlockSpec((H, None, LANES, tm), lambda i: (0, i, 0, 0))],
        out_shape=[jax.ShapeDtypeStruct((S, D), BF16), jax.ShapeDtypeStruct((H // 2, S, LANES), F32),
                   jax.ShapeDtypeStruct((H // 2, S, LANES), F32), jax.ShapeDtypeStruct((S, D), F32),
                   jax.ShapeDtypeStruct((S, LANES), F32), jax.ShapeDtypeStruct((nb, 8, LANES), F32),
                   jax.ShapeDtypeStruct((S, LANES), F32),
                   jax.ShapeDtypeStruct((S, H * LANES), BF16), jax.ShapeDtypeStruct((S, H * LANES), BF16),
                   jax.ShapeDtypeStruct((S, H * LANES), BF16), jax.ShapeDtypeStruct((H, nb, LANES, tm), BF16)],
        scratch_shapes=[pltpu.VMEM((8, LANES), F32), pltpu.VMEM((tm, D), F32), pltpu.VMEM((tm, D), F32),
                        pltpu.VMEM((tm, D), F32)],
        compiler_params=_params(1),
    )(x1, g2, w, wf, bf, gq, gk, tri, *_scatter_matrices(H))


def _skip_tables(first, last, lowest, gq, gk, G):
    nb = first.shape[0]
    bound = HEAD_DIM ** 0.5 * jnp.max(jnp.abs(gq)) * jnp.max(jnp.abs(gk))
    lowest = jnp.maximum(lowest, -bound)
    idx = jnp.arange(nb)
    margin = (SKIP_LOG + bound) - lowest
    need = (last[None, :, :] <= first[:, None, :] + margin[:, None, :]) & (idx[None, :, None] < idx[:, None, None])
    need = need | (idx[None, :, None] == idx[:, None, None])
    kstart = jnp.argmax(need, axis=1)
    qend = nb - 1 - jnp.argmax(need[::-1], axis=0)
    kstart = jnp.min(kstart.reshape(2 * nb // G, G // 2, -1), axis=1)
    kstart = kstart - (kstart & 1)
    qend = jnp.max(qend.reshape(2 * nb // G, G // 2, -1), axis=1)
    return kstart.T.astype(jnp.int32), qend.T.astype(jnp.int32), bound


def _attn_fwd(kstart, qa, ka, vt, online_max):
    S = qa.shape[0]
    H = qa.shape[1] // LANES
    nb, T = vt.shape[1], vt.shape[3]
    G = 2 * nb // kstart.shape[1]
    W = G * T

    def finish(acc, shift, o_ref, lse_ref):
        a = _aug(pl.program_id(0) % 2)
        feat = lax.broadcasted_iota(jnp.int32, (LANES, 1), 0)
        l = jnp.sum(jnp.where(feat == a, acc, 0.0), axis=0, keepdims=True)
        o_ref[...] = (acc * (1.0 / l)).T
        lse_ref[...] = shift + jnp.log(l)

    def causal(st):
        return jnp.where(lax.broadcasted_iota(jnp.int32, st.shape, 0) <= lax.broadcasted_iota(jnp.int32, st.shape, 1),
                         st, NEG)

    def fast_body(ks_ref, q_ref, k_ref, vt_ref, o_ref, lse_ref, acc_ref, sa_ref, sb_ref, sc_ref):
        h, g = pl.program_id(0), pl.program_id(1)
        q = q_ref[...]
        acc_ref[...] = jnp.zeros((LANES, W), F32)

        def scores(ki, lo):
            return _dot_nt(k_ref[pl.ds(pl.multiple_of(ki * T, T), 2 * T), :], q[lo * T:, :])

        def weighted(ki, p):
            return _dot(vt_ref[ki], p[:T]) + _dot(vt_ref[ki + 1], p[T:])

        first = ks_ref[h, 2 * g + 1]
        early = jnp.minimum(ks_ref[h, 2 * g], first)

        def narrow(i, carry):
            ki = early + 2 * i
            st = _dot_nt(k_ref[pl.ds(pl.multiple_of(ki * T, T), 2 * T), :], q[:W // 2, :])
            acc_ref[:, :W // 2] += weighted(ki, jnp.exp(st).astype(BF16))
            return carry

        lax.fori_loop(0, (first - early) // 2, narrow, 0)
        steps = (g * G - first) // 2
        sa_ref[...] = scores(first, 0)

        def advance(ki, cur_ref, next_ref):
            p = jnp.exp(cur_ref[...]).astype(BF16)
            next_ref[...] = scores(ki + 2, 0)
            acc_ref[...] += weighted(ki, p)

        def loop(i, carry):
            advance(first + 4 * i, sa_ref, sb_ref)
            advance(first + 4 * i + 2, sb_ref, sa_ref)
            return carry

        lax.fori_loop(0, steps // 2, loop, 0)

        def first_own(pending_ref):
            p = jnp.exp(causal(pending_ref[...])).astype(BF16)
            if G > 2:
                sc_ref[:, :W - 2 * T] = scores(g * G + 2, 2)
            acc_ref[...] += weighted(g * G, p)

        @pl.when(steps % 2 == 1)
        def _():
            advance(g * G - 2, sa_ref, sb_ref)
            first_own(sb_ref)

        @pl.when(steps % 2 == 0)
        def _():
            first_own(sa_ref)

        if G > 2:
            acc_ref[:, 2 * T:] += weighted(g * G + 2, jnp.exp(causal(sc_ref[:, :W - 2 * T])).astype(BF16))
        for j in range(4, G, 2):
            p = jnp.exp(causal(scores(g * G + j, j))).astype(BF16)
            acc_ref[:, j * T:] += weighted(g * G + j, p)
        finish(acc_ref[...], 0.0, o_ref, lse_ref)

    def online_body(ks_ref, q_ref, k_ref, vt_ref, o_ref, lse_ref, acc_ref, m_ref):
        h, g = pl.program_id(0), pl.program_id(1)
        q = q_ref[...]
        m_ref[...] = jnp.full((8, W), NEG, F32)
        acc_ref[...] = jnp.zeros((LANES, W), F32)

        def update(st, vtb, lo):
            m_old = m_ref[0:1, lo:]
            m_new = jnp.maximum(m_old, jnp.max(st, axis=0, keepdims=True))
            p = jnp.exp(st - m_new).astype(BF16)
            acc_ref[:, lo:] = acc_ref[:, lo:] * jnp.exp(m_old - m_new) + _dot(vtb, p)
            m_ref[:, lo:] = jnp.broadcast_to(m_new, (8, W - lo))

        def loop(ki, carry):
            kb = k_ref[pl.ds(pl.multiple_of(ki * T, T), T), :]
            update(_dot_nt(kb, q), vt_ref[ki], 0)
            return carry

        lax.fori_loop(jnp.minimum(ks_ref[h, 2 * g], ks_ref[h, 2 * g + 1]), g * G, loop, 0)
        for j in range(G):
            ki = g * G + j
            kb = k_ref[pl.ds(pl.multiple_of(ki * T, T), T), :]
            update(causal(_dot_nt(kb, q[j * T:, :])), vt_ref[ki], j * T)
        finish(acc_ref[...], m_ref[0:1, :], o_ref, lse_ref)

    return pl.pallas_call(
        online_body if online_max else fast_body, name="attn_fwd_online" if online_max else "attn_fwd",
        grid_spec=pltpu.PrefetchScalarGridSpec(
            num_scalar_prefetch=1, grid=(H, nb // G),
            in_specs=[pl.BlockSpec((W, LANES), lambda h, i, ks: (i, h)),
                      pl.BlockSpec((S, LANES), lambda h, i, ks: (0, h)),
                      pl.BlockSpec((None, nb, LANES, T), lambda h, i, ks: (h, 0, 0, 0))],
            out_specs=[pl.BlockSpec((W, LANES), lambda h, i, ks: (i, h)),
                       pl.BlockSpec((None, 1, W), lambda h, i, ks: (h, 0, i))],
            scratch_shapes=[pltpu.VMEM((LANES, W), F32)] + (
                [pltpu.VMEM((8, W), F32)] if online_max else [pltpu.VMEM((2 * T, W), F32)] * 3)),
        out_shape=[jax.ShapeDtypeStruct((S, H * LANES), F32), jax.ShapeDtypeStruct((H, 1, S), F32)],
        compiler_params=_params(2),
    )(kstart, qa, ka, vt)


def _attn_out(o_aug, lse, rel, z, x1, target, w_out, qa):
    S, D = x1.shape
    H = D // HEAD_DIM
    tm = min(ROW_TILE, S)

    def body(o_ref, z_ref, x1_ref, t_ref, w_ref, q_ref, first_ref, rel_ref, lse_ref,
             dx2_ref, dx2b_ref, o2b_ref, dz_ref, doa_ref, qa2_ref, loss_ref, oc_s, do_s):
        @pl.when(pl.program_id(0) == 0)
        def _():
            loss_ref[...] = jnp.zeros((1, LANES), F32)

        lane = _lane()
        for j in range(H // 2):
            oc_s[:, LANES * j:LANES * (j + 1)] = _pair_tile(o_ref[:, 2 * LANES * j:2 * LANES * j + LANES],
                                                            o_ref[:, 2 * LANES * j + LANES:2 * LANES * (j + 1)], lane)
        oc = oc_s[...]
        zv = z_ref[...]
        sg = _sigmoid(zv)
        sz = zv * sg
        o2 = (oc * sz).astype(BF16)
        o2b_ref[...] = o2
        e = x1_ref[...] + _dot(o2, w_ref[...]) - t_ref[...]
        sq = jnp.sum(jnp.sum(e * e, axis=1, keepdims=True), axis=0, keepdims=True)
        loss_ref[...] += jnp.broadcast_to(sq * (0.5 / D), (1, LANES))
        dx2 = e * (1.0 / D)
        dx2_ref[...] = dx2
        dx2b = dx2.astype(BF16)
        dx2b_ref[...] = dx2b
        do2 = _dot_nt(dx2b, w_ref[...])
        dz_ref[...] = (do2 * oc * (sg * (1.0 + zv * (1.0 - sg)))).astype(BF16)
        do_s[...] = do2 * sz
        deltas = jnp.zeros((tm, LANES), F32)
        for hd in range(H):
            dt = _head_tile(do_s, hd, lane)
            delta = jnp.sum(dt * _head_tile(oc_s, hd, lane), axis=1, keepdims=True)
            deltas = deltas + jnp.where(lane == hd, delta, 0.0)
            doa_ref[:, LANES * hd:LANES * (hd + 1)] = dt.astype(BF16)
        doa_ref[...] += _dot(_pack3(-deltas, lane, 0.0), first_ref[...]).astype(BF16)
        lse = jnp.concatenate([lse_ref[...], jnp.zeros((LANES - H, tm), F32)], axis=0).T
        rq = rel_ref[...] - lse
        tile_lane = lax.broadcasted_iota(jnp.int32, (1, H * LANES), 1)
        extra = tile_lane % LANES - _aug((tile_lane // LANES) % 2)
        kept = jnp.where((extra >= 0) & (extra < 3), jnp.zeros((), BF16), q_ref[...])
        qa2_ref[...] = kept + _dot(_pack3(rq, lane, 0.0), first_ref[...]).astype(BF16)

    return pl.pallas_call(
        body, name="attn_out", grid=(S // tm,),
        in_specs=[_rows(tm, H * LANES), _rows(tm, D), _rows(tm, D), _rows(tm, D), _whole((D, D)),
                  _rows(tm, H * LANES), _whole((LANES, H * LANES)), _rows(tm, LANES),
                  pl.BlockSpec((H, tm), lambda i: (0, i))],
        out_specs=[_rows(tm, D), _rows(tm, D), _rows(tm, D), _rows(tm, D), _rows(tm, H * LANES),
                   _rows(tm, H * LANES), _whole((1, LANES))],
        out_shape=[jax.ShapeDtypeStruct((S, D), F32), jax.ShapeDtypeStruct((S, D), BF16),
                   jax.ShapeDtypeStruct((S, D), BF16), jax.ShapeDtypeStruct((S, D), BF16),
                   jax.ShapeDtypeStruct((S, H * LANES), BF16), jax.ShapeDtypeStruct((S, H * LANES), BF16),
                   jax.ShapeDtypeStruct((1, LANES), F32)],
        scratch_shapes=[pltpu.VMEM((tm, D), F32), pltpu.VMEM((tm, D), F32)],
        compiler_params=_params(1),
    )(o_aug, z, x1, target, w_out, qa, _scatter_matrices(H)[0], rel, lse)


def _attn_bwd(qend, qa2, doa, ka, va, T):
    S = qa2.shape[0]
    H = qa2.shape[1] // LANES
    nb = S // T
    G = 2 * nb // qend.shape[1]
    W = G * T

    def body(qe_ref, q_ref, do_ref, k_ref, v_ref, dq_ref, dk_ref, dv_ref, dkt_acc, dvt_acc):
        h, g = pl.program_id(0), pl.program_id(1)

        @pl.when(g == 0)
        def _():
            dq_ref[...] = jnp.zeros((S, LANES), F32)

        kb = k_ref[...]
        vb = v_ref[...]
        dkt_acc[...] = jnp.zeros((LANES, W), F32)
        dvt_acc[...] = jnp.zeros((LANES, W), F32)

        def step(qi, c0, c1, masked):
            rows = pl.ds(pl.multiple_of(qi * T, T), 2 * T)
            qb = q_ref[rows, :]
            dob = do_ref[rows, :]
            s = _dot_nt(qb, kb[c0:c1])
            if masked:
                query = lax.broadcasted_iota(jnp.int32, s.shape, 0) + (c1 - 2 * T)
                s = jnp.where(lax.broadcasted_iota(jnp.int32, s.shape, 1) <= query, s, NEG)
            p = jnp.exp(s)
            ds = (p * _dot_nt(dob, vb[c0:c1])).astype(BF16)
            dvt_acc[:, c0:c1] += _dot(dob.astype(F32).T.astype(BF16), p.astype(BF16))
            dkt_acc[:, c0:c1] += _dot(qb.astype(F32).T.astype(BF16), ds)
            dq_ref[rows, :] += _dot(ds, kb[c0:c1])

        for m in range(G // 2):
            step(g * G + 2 * m, 0, (m + 1) * 2 * T, True)
        first = g * G + G
        n_all = jnp.maximum((qe_ref[h, 2 * g] - first + 2) // 2, 0)
        second = first + 2 * n_all

        def all_keys(i, carry):
            step(first + 2 * i, 0, W, False)
            return carry

        def late_keys(i, carry):
            step(second + 2 * i, W // 2, W, False)
            return carry

        lax.fori_loop(0, n_all, all_keys, 0)
        lax.fori_loop(0, (qe_ref[h, 2 * g + 1] - second + 2) // 2, late_keys, 0)
        dk_ref[...] = dkt_acc[...].T
        dv_ref[...] = dvt_acc[...].T.astype(BF16)

    heads = pl.BlockSpec((None, W, LANES), lambda h, i, qe: (h, i, 0))
    return pl.pallas_call(
        body, name="attn_bwd",
        grid_spec=pltpu.PrefetchScalarGridSpec(
            num_scalar_prefetch=1, grid=(H, nb // G),
            in_specs=[pl.BlockSpec((S, LANES), lambda h, i, qe: (0, h)), pl.BlockSpec((S, LANES), lambda h, i, qe: (0, h)),
                      pl.BlockSpec((W, LANES), lambda h, i, qe: (i, h)), pl.BlockSpec((W, LANES), lambda h, i, qe: (i, h))],
            out_specs=[pl.BlockSpec((None, S, LANES), lambda h, i, qe: (h, 0, 0)), heads, heads],
            scratch_shapes=[pltpu.VMEM((LANES, W), F32), pltpu.VMEM((LANES, W), F32)]),
        out_shape=[jax.ShapeDtypeStruct((H, S, LANES), F32), jax.ShapeDtypeStruct((H, S, LANES), F32),
                   jax.ShapeDtypeStruct((H, S, LANES), BF16)],
        compiler_params=_params(2),
    )(qend, qa2, doa, ka, va)


def _attn_proj_bwd(dqt, dka, dva, qraw, kraw, dz, f, gq, gk):
    S, D = dz.shape
    H = D // HEAD_DIM
    tm = min(CONV_TILE, S)
    last = S // tm - 1
    tri = (lax.broadcasted_iota(jnp.int32, (tm, tm), 1) >= lax.broadcasted_iota(jnp.int32, (tm, tm), 0)).astype(BF16)

    def body(dq_ref, dk_ref, dv_ref, q_ref, k_ref, dz_ref, f_ref, gq_ref, gk_ref, tri_ref, ones_ref,
             dproj_ref, small_ref, carry, pairs):
        @pl.when(pl.program_id(0) == 0)
        def _():
            small_ref[...] = jnp.zeros((8, LANES), F32)
            carry[...] = jnp.zeros((8, LANES), F32)

        lane = _lane()

        def head_pair(j, acc):
            dcs, dgq, dgk = acc
            dq2, dk2 = [], []
            q_pair, k_pair = q_ref[j], k_ref[j]
            for parity in (0, 1):
                hd = 2 * j + parity
                own, a = _own(lane, parity), _aug(parity)
                dqf = dq_ref[hd]
                dqn = jnp.where(own, dqf * Q_SCALE, 0.0)
                d, dg = _head_rms_bwd(dqn, jnp.where(own, q_pair, 0.0), gq_ref[...], ones_ref[...])
                dq2.append(d)
                dgq = dgq + dg
                dkt = dk_ref[hd]
                dcs = dcs + jnp.where(lane == hd, _col(dqf, lane, a) - _col(dkt, lane, a + 3), 0.0)
                d, dg = _head_rms_bwd(jnp.where(own, dkt, 0.0), jnp.where(own, k_pair, 0.0), gk_ref[...], ones_ref[...])
                dk2.append(d)
                dgk = dgk + dg
            pairs[0, j] = _pair_tile(*dq2, lane).astype(BF16)
            pairs[1, j] = _pair_tile(*dk2, lane).astype(BF16)
            pairs[2, j] = _pair_tile(dv_ref[2 * j], dv_ref[2 * j + 1], lane)
            return dcs, dgq, dgk

        zero = jnp.zeros((1, LANES), F32)
        dcs, dgq, dgk = lax.fori_loop(0, H // 2, head_pair, (jnp.zeros((tm, LANES), F32), zero, zero))
        for part in range(3):
            for j in range(H // 2):
                dproj_ref[:, part * D + LANES * j:part * D + LANES * (j + 1)] = pairs[part, j]
        dproj_ref[:, 3 * D:4 * D] = dz_ref[...]
        dlogf = _dot01(tri_ref[...], dcs) + carry[0:1, :]
        carry[...] = jnp.broadcast_to(dlogf[0:1, :], (8, LANES))
        df = dlogf * (1.0 / (1.0 + jnp.exp(f_ref[...])))
        dproj_ref[:, 4 * D:4 * D + LANES] = df.astype(BF16)
        small_ref[0:1, :] += jnp.sum(df, axis=0, keepdims=True)
        small_ref[1:2, :] += dgq
        small_ref[2:3, :] += dgk

    W = 4 * D + LANES
    heads = pl.BlockSpec((H, tm, LANES), lambda i: (0, last - i, 0))
    head_pairs = pl.BlockSpec((H // 2, tm, LANES), lambda i: (0, last - i, 0))
    return pl.pallas_call(
        body, name="attn_proj_bwd", grid=(S // tm,),
        in_specs=[heads, heads, heads, head_pairs, head_pairs,
                  _rows(tm, D, last), _rows(tm, LANES, last), _whole((1, LANES)), _whole((1, LANES)),
                  _whole((tm, tm)), _whole((LANES, LANES))],
        out_specs=[_rows(tm, W, last), _whole((8, LANES))],
        out_shape=[jax.ShapeDtypeStruct((S, W), BF16), jax.ShapeDtypeStruct((8, LANES), F32)],
        scratch_shapes=[pltpu.VMEM((8, LANES), F32), pltpu.VMEM((3, H // 2, tm, LANES), BF16)],
        compiler_params=_params(1),
    )(dqt, dka, dva, qraw, kraw, dz, f, gq, gk, tri, jnp.ones((LANES, LANES), BF16))


def _matmul_tn(a, b, col0, n, tn, name, stacked=False, chip_sums=None):
    S, M = a.shape
    ts = min(TN_ROWS, S)
    off = col0 // tn
    grid = (n // tn, S // ts)
    k = len(chip_sums or [])

    def body(a_ref, b_ref, *rest):
        o_ref = rest[k]
        j, s = pl.program_id(0), pl.program_id(1)
        if k:
            copies = _chip_sum_copies(rest[:k], rest[k + 1:2 * k + 1], rest[2 * k + 1], rest[2 * k + 2])

            @pl.when((j == 0) & (s == 0))
            def _():
                for cp in copies:
                    cp.start()

        @pl.when(s == 0)
        def _():
            o_ref[...] = jnp.zeros((M, tn), F32)

        o_ref[...] += _dot_tn(a_ref[...], b_ref[...])

        if k:
            @pl.when((j == grid[0] - 1) & (s == grid[1] - 1))
            def _():
                for cp in copies:
                    cp.wait()

    if stacked:
        out_spec, out_shape = pl.BlockSpec((None, M, tn), lambda j, s: (j, 0, 0)), (n // tn, M, tn)
    else:
        out_spec, out_shape = pl.BlockSpec((M, tn), lambda j, s: (0, j)), (M, n)
    results = pl.pallas_call(
        body, name=name, grid=grid,
        in_specs=[pl.BlockSpec((ts, M), lambda j, s: (s, 0)), pl.BlockSpec((ts, tn), lambda j, s: (s, off + j))]
        + [ANY] * k,
        out_specs=[out_spec] + [ANY] * k,
        out_shape=[jax.ShapeDtypeStruct(out_shape, F32)]
        + [jax.ShapeDtypeStruct((3,) + g.shape[1:], g.dtype) for g in chip_sums or []],
        scratch_shapes=[pltpu.SemaphoreType.DMA((k, 3))] * (2 if k else 0),
        compiler_params=_params(2),
    )(a, b, *(chip_sums or []))
    return results[0] if chip_sums is None else (results[0], list(results[1:]))


def _adam_update(gv, w_ref, m_ref, v_ref, d_ref, m2_ref, v2_ref):
    m2 = ADAM_B1 * m_ref[...] + (1.0 - ADAM_B1) * gv
    v2 = ADAM_B2 * v_ref[...] + (1.0 - ADAM_B2) * (gv * gv)
    m2_ref[...] = m2
    v2_ref[...] = v2
    m_hat = m2 / (1.0 - ADAM_B1 ** ADAM_STEP)
    v_hat = v2 / (1.0 - ADAM_B2 ** ADAM_STEP)
    d_ref[...] = -ADAM_LR * (m_hat / (jnp.sqrt(v_hat) + ADAM_EPS) + ADAM_WD * w_ref[...])


def _adamw(w, g, m, v, name):
    r, c = w.shape
    tr = ROW_TILE if r % ROW_TILE == 0 else r

    def body(w_ref, g_ref, m_ref, v_ref, d_ref, m2_ref, v2_ref):
        _adam_update(g_ref[...], w_ref, m_ref, v_ref, d_ref, m2_ref, v2_ref)

    spec = _rows(tr, c)
    return pl.pallas_call(
        body, name=name, grid=(r // tr,), in_specs=[spec] * 4, out_specs=[spec] * 3,
        out_shape=[jax.ShapeDtypeStruct((r, c), F32)] * 3, compiler_params=_params(1),
    )(w, g, m, v)


def _adamw_halves(w, mine, other, m, v, core, name):
    r, c = mine.shape
    tr = ROW_TILE if r % ROW_TILE == 0 else r
    per = r // tr

    def body(core_ref, w_ref, mine_ref, other_ref, m_ref, v_ref, g_ref, d_ref, m2_ref, v2_ref):
        gv = jnp.where(pl.program_id(0) // per == core_ref[0], mine_ref[...], other_ref[...])
        g_ref[...] = gv
        _adam_update(gv, w_ref, m_ref, v_ref, d_ref, m2_ref, v2_ref)

    full = pl.BlockSpec((tr, c), lambda i, core: (i, 0))
    half = pl.BlockSpec((tr, c), lambda i, core: (i % per, 0))
    return pl.pallas_call(
        body, name=name,
        grid_spec=pltpu.PrefetchScalarGridSpec(num_scalar_prefetch=1, grid=(2 * per,),
                                               in_specs=[full, half, half, full, full], out_specs=[full] * 4),
        out_shape=[jax.ShapeDtypeStruct((2 * r, c), F32)] * 4, compiler_params=_params(1),
    )(core, w, mine, other, m, v)


def _no_reduction(by_chip, tag):
    return [], []


def _after_conv(conv_acts, x, target, g1, w_in, conv_w, w_out, g2, wa_in, b_f, gq, gk, wa_out,
                reduce_early=_no_reduction):
    S, D = x.shape
    H = D // HEAD_DIM
    ws = wa_in.shape[2]
    w_qkvz = jnp.concatenate([wa_in[0], wa_in[1], wa_in[2], wa_in[3][:, :4 * D - 3 * ws]], axis=1)
    wf = jnp.pad(wa_in[3][:, 4 * D - 3 * ws:], ((0, 0), (0, LANES - H)))
    bf = jnp.pad(b_f, ((0, 0), (0, LANES - H)))
    gq128 = jnp.concatenate([gq, gq], axis=1)
    gk128 = jnp.concatenate([gk, gk], axis=1)

    proj, h1, yc, y, x1 = conv_acts
    h2, qraw, kraw, z, f, ends, rel, qa, ka, va, vt = _attn_front(x1, g2, w_qkvz, wf, bf, gq128, gk128)
    T = vt.shape[3]
    kstart, qend, bound = _skip_tables(ends[:, 0, :H], ends[:, 1, :H], ends[:, 2, :H], gq, gk, min(ATT_GROUP, S // T))
    o_aug, lse = lax.cond(2.0 * bound <= PLAIN_EXP_MAX, functools.partial(_attn_fwd, online_max=False),
                          functools.partial(_attn_fwd, online_max=True), kstart, qa, ka, vt)
    dx2, dx2b, o2b, dz, doa, qa2, loss = _attn_out(o_aug, lse.reshape(H, S), rel, z, x1, target, wa_out, qa)
    dqt, dka, dva = _attn_bwd(qend, qa2, doa, ka, va, T)
    dproj2, small = _attn_proj_bwd(dqt, dka, dva, qraw, kraw, dz, f, gq128, gk128)

    tn = min(1024, D)
    dwa_out = _matmul_tn(o2b, dx2b, 0, D, tn, "dw_attn_out")
    dw_main = _matmul_tn(h2, dproj2, 0, 4 * D, 2 * tn, "dw_attn_in")
    dw_f = _matmul_tn(h2, dproj2, 4 * D, LANES, LANES, "dw_attn_f")
    dwa_in = jnp.stack([dw_main[:, 0:ws], dw_main[:, ws:2 * ws], dw_main[:, 2 * ws:3 * ws],
                        jnp.concatenate([dw_main[:, 3 * ws:], dw_f[:, :H]], axis=1)])
    attn_f32, attn_bf16 = reduce_early([dwa_in, dwa_out.reshape(4, D // 4, D)], "attn")
    dproj1, dx, dx1b, dg1, dcw, dg2 = _conv_bwd(dproj2, w_qkvz, wf, x1, g2, dx2, x, g1, w_in, w_out, conv_w, proj, yc)
    dw_in, attn_arrived = _matmul_tn(h1, dproj1, 0, 4 * D, D, "dw_conv_in", stacked=True, chip_sums=attn_bf16)
    in_f32, in_bf16 = reduce_early([dw_in], "conv_in")
    dw_out, in_arrived = _matmul_tn(y, dx1b, 0, D, tn, "dw_conv_out", chip_sums=in_bf16)
    grads = dict(conv_norm_g=dg1, conv_w_in=dw_in, conv_w=dcw, conv_w_out=dw_out, attn_norm_g=dg2,
                 attn_w_in=dwa_in, attn_b_f=small[0:1, :H],
                 attn_q_norm_g=small[1:2, :HEAD_DIM] + small[1:2, HEAD_DIM:],
                 attn_k_norm_g=small[2:3, :HEAD_DIM] + small[2:3, HEAD_DIM:], attn_w_out=dwa_out)
    return loss[0, 0], dx, grads, (attn_f32 + in_f32, attn_arrived + in_arrived)


def _coords():
    return lax.axis_index("x"), lax.axis_index("y"), lax.axis_index("c")


def _at(ref, idx):
    return ref.at[idx] if idx else ref


def _other_chips(x, y):
    return [(1 - x, y), (x, 1 - y), (1 - x, 1 - y)]


def _gather_plan(src, out, send, recv):
    x, y, c = _coords()
    mine = 2 * x + y
    sibling = (x, y, 1 - c)
    others = [(a, k, 2 * px + py, (px, py)) for a in range(len(src)) for k, (px, py) in enumerate(_other_chips(x, y))]

    def copy(a, k, chip, half, to, source=None):
        dst = out[a].at[chip, half]
        return pltpu.make_async_remote_copy(src_ref=dst if source is None else source, dst_ref=dst,
                                            send_sem=send.at[a, k], recv_sem=recv.at[a, k],
                                            device_id=to, device_id_type=MESH)

    first = [copy(a, k, mine, c, (*chip, c), source=src[a].at[c]) for a, k, _, chip in others]
    own = [pltpu.make_async_remote_copy(src_ref=src[a], dst_ref=out[a].at[mine], send_sem=send.at[a, 6],
                                        recv_sem=recv.at[a, 6], device_id=sibling, device_id_type=MESH)
           for a in range(len(src))]
    passed = [copy(a, 3 + k, slot, c, sibling) for a, k, slot, _ in others]

    def start():
        for cp in first + own:
            cp.start()

    def forward():
        for (a, k, slot, _), cp in zip(others, passed):
            copy(a, k, slot, c, (x, y, c)).wait_recv()
            cp.start()

    def finish():
        for a, k, slot, _ in others:
            copy(a, 3 + k, slot, 1 - c, (x, y, c)).wait_recv()
        for cp in own:
            cp.wait_recv()
        for cp in first + passed + own:
            cp.wait_send()

    return start, forward, finish


def _all_gather(halved, whole):
    nh, nw = len(halved), len(whole)

    def body(*refs):
        src_h, src_w = refs[:nh], refs[nh:nh + nw]
        out_h, out_w = refs[nh + nw:2 * nh + nw], refs[2 * nh + nw:2 * (nh + nw)]
        send_h, recv_h, send_w, recv_w = refs[2 * (nh + nw):]
        x, y, c = _coords()
        mine = 2 * x + y
        chips = _other_chips(x, y)

        def copy_w(a, k, chip, to):
            return pltpu.make_async_remote_copy(src_ref=src_w[a], dst_ref=out_w[a].at[chip],
                                                send_sem=send_w.at[a, k], recv_sem=recv_w.at[a, k],
                                                device_id=to, device_id_type=MESH)

        start, forward, finish = _gather_plan(src_h, out_h, send_h, recv_h)
        small = [copy_w(a, k, mine, (*chip, c)) for a in range(nw) for k, chip in enumerate(chips)]
        small += [copy_w(a, 3, mine, (x, y, 1 - c)) for a in range(nw)]
        start()
        for cp in small:
            cp.start()
        forward()
        finish()
        for a in range(nw):
            for k, (px, py) in enumerate(chips):
                copy_w(a, k, 2 * px + py, (x, y, c)).wait_recv()
            copy_w(a, 3, mine, (x, y, c)).wait_recv()
        for cp in small:
            cp.wait_send()

    out_shape = [jax.ShapeDtypeStruct((4,) + a.shape, a.dtype) for a in list(halved) + list(whole)]
    return pl.pallas_call(
        body, name="gather_weights", in_specs=[ANY] * (nh + nw), out_specs=[ANY] * (nh + nw), out_shape=out_shape,
        scratch_shapes=[pltpu.SemaphoreType.DMA((nh, 7)), pltpu.SemaphoreType.DMA((nh, 7)),
                        pltpu.SemaphoreType.DMA((nw, 4)), pltpu.SemaphoreType.DMA((nw, 4))],
    )(*halved, *whole)


def _exchange(name, srcs, lands, copies, local_copies):
    ns, nl, n, nloc = len(srcs), len(lands), len(copies), len(local_copies)

    def body(*refs):
        src, land = refs[:ns], refs[ns:ns + nl]
        send, recv, local_sem = refs[ns + nl:]
        me = _coords()
        started = []
        for k, (si, s_at, li, l_at, ci) in enumerate(local_copies):
            cp = pltpu.make_async_copy(_at(src[si], s_at(*me)), _at(land[li], l_at(*me)), local_sem.at[k])
            cp.start()
            started.append(cp)
        remote = []
        for k, (si, s_at, li, l_at, peer) in enumerate(copies):
            cp = pltpu.make_async_remote_copy(src_ref=_at(src[si], s_at(*me)), dst_ref=_at(land[li], l_at(*me)),
                                              send_sem=send.at[k], recv_sem=recv.at[k],
                                              device_id=peer(*me), device_id_type=MESH)
            cp.start()
            remote.append(cp)
        for cp in remote:
            cp.wait()
        for cp in started:
            cp.wait()

    return pl.pallas_call(
        body, name=name, in_specs=[ANY] * ns, out_specs=[ANY] * nl, out_shape=list(lands),
        scratch_shapes=[pltpu.SemaphoreType.DMA((n,)), pltpu.SemaphoreType.DMA((n,)),
                        pltpu.SemaphoreType.DMA((max(nloc, 1),))],
    )(*srcs)


def _add_pairs(a, b, core, name):
    _, r, cols = b.shape
    tr = ROW_TILE if r % ROW_TILE == 0 else r

    def body(core_ref, a_ref, b_ref, o_ref, ob_ref):
        s = a_ref[...] + b_ref[...]
        o_ref[...] = s
        ob_ref[...] = s.astype(BF16)

    spec = pl.BlockSpec((None, tr, cols), lambda j, i, core: (j, i, 0))
    return pl.pallas_call(
        body, name=name,
        grid_spec=pltpu.PrefetchScalarGridSpec(
            num_scalar_prefetch=1, grid=(4, r // tr),
            in_specs=[pl.BlockSpec((None, None, tr, cols), lambda j, i, core: (j, core[0], i, 0)), spec],
            out_specs=[spec, spec]),
        out_shape=[jax.ShapeDtypeStruct(b.shape, F32), jax.ShapeDtypeStruct(b.shape, BF16)],
        compiler_params=_params(2),
    )(core, a, b)


def _sum_chips(own, landed, name):
    _, r, cols = landed.shape
    tr = ROW_TILE if r % ROW_TILE == 0 else r

    def body(own_ref, land_ref, o_ref):
        acc = own_ref[...]
        for j in range(3):
            acc = acc + land_ref[j].astype(F32)
        o_ref[...] = acc

    return pl.pallas_call(
        body, name=name, grid=(r // tr,),
        in_specs=[_rows(tr, cols), pl.BlockSpec((3, tr, cols), lambda i: (0, i, 0))], out_specs=_rows(tr, cols),
        out_shape=jax.ShapeDtypeStruct((r, cols), F32), compiler_params=_params(1),
    )(own, landed)


def _sum_devices(landed, name):
    def body(l_ref, o_ref):
        acc = l_ref[0]
        for j in range(1, 8):
            acc = acc + l_ref[j]
        o_ref[...] = acc

    return pl.pallas_call(body, name=name, out_shape=jax.ShapeDtypeStruct(landed.shape[1:], F32))(landed)


CHIP_FLIPS = [(1, 0), (0, 1), (1, 1)]


def _flip(fx, fy, fc):
    return lambda x, y, c: (x ^ fx, y ^ fy, c ^ fc)


def _sum_cores(big, small, tag):
    nb = len(big)
    c = lax.axis_index("c")
    copies = [(a, (lambda j: lambda x, y, c: (j, 1 - c))(j), a, (lambda j: lambda x, y, c: (j,))(j), _flip(0, 0, 1))
              for a in range(nb) for j in range(4)]
    lands = [jax.ShapeDtypeStruct((4,) + g.shape[2:], F32) for g in big]
    srcs, local = list(big), []
    if small is not None:
        flips = [(fx, fy, fc) for fx in (0, 1) for fy in (0, 1) for fc in (0, 1) if fx or fy or fc]
        copies += [(nb, lambda x, y, c: (), nb, lambda x, y, c: (4 * x + 2 * y + c,), _flip(*f)) for f in flips]
        lands.append(jax.ShapeDtypeStruct((8,) + small.shape, F32))
        local = [(nb, lambda x, y, c: (), nb, lambda x, y, c: (4 * x + 2 * y + c,), None)]
        srcs.append(small)
    landed = _exchange("swap_halves_" + tag, srcs, lands, copies, local)
    small_sum = None if small is None else _sum_devices(landed[nb], "sum_small")
    core = jnp.reshape(c, (1,)).astype(jnp.int32)
    sums = [_add_pairs(big[a], landed[a], core, f"add_cores_{tag}_{a}") for a in range(nb)]
    return [s for s, _ in sums], [sb for _, sb in sums], small_sum


def _chip_sum_copies(sums, landed, send, recv):
    x, y, c = _coords()
    return [pltpu.make_async_remote_copy(src_ref=sums[a].at[2 * (x ^ fx) + (y ^ fy)], dst_ref=landed[a].at[k],
                                         send_sem=send.at[a, k], recv_sem=recv.at[a, k],
                                         device_id=(x ^ fx, y ^ fy, c), device_id_type=MESH)
            for a in range(len(sums)) for k, (fx, fy) in enumerate(CHIP_FLIPS)]


def _send_chip_sums(chip_bf16):
    n = len(chip_bf16)

    def body(*refs):
        copies = _chip_sum_copies(refs[:n], refs[n:2 * n], refs[2 * n], refs[2 * n + 1])
        for cp in copies:
            cp.start()
        for cp in copies:
            cp.wait()

    return pl.pallas_call(
        body, name="send_chip_sums", in_specs=[ANY] * n, out_specs=[ANY] * n,
        out_shape=[jax.ShapeDtypeStruct((3,) + g.shape[1:], BF16) for g in chip_bf16],
        scratch_shapes=[pltpu.SemaphoreType.DMA((n, 3)), pltpu.SemaphoreType.DMA((n, 3))],
    )(*chip_bf16)


def _sum_chips_and_share(chip_f32, landed):
    x, y, _ = _coords()
    totals = [_sum_chips(lax.dynamic_index_in_dim(f, 2 * x + y, axis=0, keepdims=False), l, f"sum_chips_{a}")
              for a, (f, l) in enumerate(zip(chip_f32, landed))]
    copies = [(a, lambda x, y, c: (), a, lambda x, y, c: (), _flip(0, 0, 1)) for a in range(len(totals))]
    lands = [jax.ShapeDtypeStruct(t.shape, F32) for t in totals]
    return list(zip(totals, _exchange("swap_sums", totals, lands, copies, [])))


def kernel(x, conv_norm_g, conv_w_in, conv_w, conv_w_out, attn_norm_g, attn_w_in, attn_b_f, attn_q_norm_g, attn_k_norm_g, attn_w_out, loss_target, m_conv_norm_g, m_conv_w_in, m_conv_w, m_conv_w_out, m_attn_norm_g, m_attn_w_in, m_attn_b_f, m_attn_q_norm_g, m_attn_k_norm_g, m_attn_w_out, v_conv_norm_g, v_conv_w_in, v_conv_w, v_conv_w_out, v_attn_norm_g, v_attn_w_in, v_attn_b_f, v_attn_q_norm_g, v_attn_k_norm_g, v_attn_w_out):
    xi, yi, _ = _coords()
    chip = 2 * xi + yi
    D = x.shape[2]
    H = D // HEAD_DIM
    names = ["conv_norm_g", "conv_w_in", "conv_w", "conv_w_out", "attn_norm_g", "attn_w_in", "attn_b_f",
             "attn_q_norm_g", "attn_k_norm_g", "attn_w_out"]
    weights = dict(zip(names, [conv_norm_g, conv_w_in, conv_w, conv_w_out, attn_norm_g, attn_w_in, attn_b_f,
                               attn_q_norm_g, attn_k_norm_g, attn_w_out]))
    m_in = dict(zip(names, [m_conv_norm_g, m_conv_w_in, m_conv_w, m_conv_w_out, m_attn_norm_g, m_attn_w_in,
                            m_attn_b_f, m_attn_q_norm_g, m_attn_k_norm_g, m_attn_w_out]))
    v_in = dict(zip(names, [v_conv_norm_g, v_conv_w_in, v_conv_w, v_conv_w_out, v_attn_norm_g, v_attn_w_in,
                            v_attn_b_f, v_attn_q_norm_g, v_attn_k_norm_g, v_attn_w_out]))
    weights = {k: w[0] for k, w in weights.items()}
    m_in = {k: w[0] for k, w in m_in.items()}
    v_in = {k: w[0] for k, w in v_in.items()}

    big_names = ["conv_w_in", "attn_w_in", "conv_w_out", "attn_w_out"]
    halved = {k: weights[k].astype(BF16).reshape(2, weights[k].shape[0] // 2, weights[k].shape[1]) for k in big_names}
    q = D // 4
    small_w = jnp.concatenate([weights["conv_w"], weights["attn_norm_g"][None, :], jnp.zeros((4, q), F32)], axis=0)
    g_in, g_out, g_small = _all_gather([halved["conv_w_in"], halved["conv_w_out"]], [small_w])
    w_in = g_in.reshape(4, D, D)
    w_out = g_out.reshape(D, D)
    conv_w_full = g_small[:, 0:3, :].transpose(1, 0, 2).reshape(3, D)
    attn_g_full = g_small[:, 3, :].reshape(1, D)
    g1 = weights["conv_norm_g"][None, :]

    conv_acts, (ga_in, ga_out) = _conv_fwd(x[0], g1, w_in, conv_w_full, w_out, [halved["attn_w_in"], halved["attn_w_out"]])
    def core_sums(by_chip, tag):
        f32, bf16, _ = _sum_cores([g.reshape(4, 2, g.shape[1] // 2, g.shape[2]) for g in by_chip], None, tag)
        return f32, bf16

    loss_part, grad_x, grads, (early_f32, early_arrived) = _after_conv(
        conv_acts, x[0], loss_target[0], g1, w_in, conv_w_full, w_out, attn_g_full, ga_in.reshape(4, D, D + H // 4),
        weights["attn_b_f"][None, :], weights["attn_q_norm_g"][None, :], weights["attn_k_norm_g"][None, :],
        ga_out.reshape(D, D), reduce_early=core_sums)

    tail = jnp.concatenate([grads["attn_b_f"], grads["attn_q_norm_g"], grads["attn_k_norm_g"],
                            jnp.reshape(loss_part, (1, 1)), jnp.zeros((1, D - H - 2 * HEAD_DIM - 1), F32)], axis=1)
    small = jnp.concatenate([grads["conv_norm_g"], grads["conv_w"], grads["attn_norm_g"], tail,
                             jnp.zeros((2, D), F32)], axis=0)
    out_f32, out_bf16, small_sum = _sum_cores([grads["conv_w_out"].reshape(4, 2, D // 8, D)], small, "conv_out")
    big_names = ["attn_w_in", "attn_w_out", "conv_w_in", "conv_w_out"]
    reduced = _sum_chips_and_share(early_f32 + out_f32, early_arrived + list(_send_chip_sums(out_bf16)))
    final = {}
    final["conv_norm_g"] = small_sum[0]
    final["conv_w"] = lax.dynamic_slice_in_dim(small_sum[1:4], chip * q, q, axis=1)
    final["attn_norm_g"] = lax.dynamic_slice_in_dim(small_sum[4], chip * q, q, axis=0)
    final["attn_b_f"] = small_sum[5, :H]
    final["attn_q_norm_g"] = small_sum[5, H:H + HEAD_DIM]
    final["attn_k_norm_g"] = small_sum[5, H + HEAD_DIM:H + 2 * HEAD_DIM]
    loss = small_sum[5, H + 2 * HEAD_DIM]

    delta, new_m, new_v = {}, {}, {}
    core = jnp.reshape(lax.axis_index("c"), (1,)).astype(jnp.int32)
    for k, (mine, other) in zip(big_names, reduced):
        final[k], delta[k], new_m[k], new_v[k] = _adamw_halves(weights[k], mine, other, m_in[k], v_in[k], core,
                                                               "adamw_" + k)
    for k in names:
        if k in big_names:
            continue
        shape = weights[k].shape
        as2d = (lambda a: a.reshape(1, -1)) if len(shape) == 1 else (lambda a: a)
        d, m2, v2 = _adamw(as2d(weights[k]), as2d(final[k]), as2d(m_in[k]), as2d(v_in[k]), "adamw_" + k)
        delta[k], new_m[k], new_v[k] = d.reshape(shape), m2.reshape(shape), v2.reshape(shape)
    lead = lambda a: a[None]
    return (loss, grad_x[None], *[lead(final[k]) for k in names], *[lead(delta[k]) for k in names],
            *[lead(new_m[k]) for k in names], *[lead(new_v[k]) for k in names])
```

```python
import functools

import jax
import jax.numpy as jnp
from jax import lax
from jax.experimental import pallas as pl
from jax.experimental.pallas import tpu as pltpu

F32 = jnp.float32
BF16 = jnp.bfloat16
HEAD_DIM = 64
LANES = 128
RMS_EPS = 1e-6
NEG = -1e30
Q_SCALE = 0.125
ROW_TILE = 256
CONV_TILE = 512
ATT_GROUP = 4
SKIP_LOG = 106.0
PLAIN_EXP_MAX = 60.0
TN_ROWS = 2048
VMEM_LIMIT = 56 << 20
ADAM_LR, ADAM_B1, ADAM_B2, ADAM_EPS, ADAM_WD, ADAM_STEP = 0.001, 0.9, 0.999, 1e-08, 0.01, 10
MESH = pl.DeviceIdType.MESH
ANY = pl.BlockSpec(memory_space=pl.ANY)


def _lane():
    return lax.broadcasted_iota(jnp.int32, (1, LANES), 1)


def _split3(x):
    hi = x.astype(BF16).astype(F32)
    r = x - hi
    mid = r.astype(BF16).astype(F32)
    lo = (r - mid).astype(BF16).astype(F32)
    return hi, mid, lo


STAT_STRIDE = 16
ONE_LANE = 3 * STAT_STRIDE


def _pack3(x, lane, one):
    hi, mid, lo = _split3(x)
    packed = hi + pltpu.roll(mid, STAT_STRIDE, 1) + pltpu.roll(lo, 2 * STAT_STRIDE, 1)
    return jnp.where(lane == ONE_LANE, one, packed).astype(BF16)


def _scatter_matrices(H):
    rows = lax.broadcasted_iota(jnp.int32, (LANES, H * LANES), 0)
    cols = lax.broadcasted_iota(jnp.int32, (LANES, H * LANES), 1)
    head, within = cols // LANES, cols % LANES
    extra = within - _aug(head % 2)
    term = (rows < ONE_LANE) & (rows % STAT_STRIDE == head)
    first = ((term & (extra == rows // STAT_STRIDE)) | ((rows == ONE_LANE) & (extra >= 3) & (extra < 6)))
    second = ((term & (extra - 3 == rows // STAT_STRIDE)) | ((rows == ONE_LANE) & (extra >= 0) & (extra < 3)))
    return first.astype(BF16), second.astype(BF16)


def _col(x, lane, idx):
    return jnp.sum(jnp.where(lane == idx, x, 0.0), axis=1, keepdims=True)


def _feat(parity):
    return HEAD_DIM * parity


def _aug(parity):
    return HEAD_DIM * (1 - parity)


def _own(lane, parity):
    return (lane >= _feat(parity)) & (lane < _feat(parity) + HEAD_DIM)


def _head_tile(ref, hd, lane):
    j = hd // 2
    return jnp.where(_own(lane, hd % 2), ref[:, LANES * j:LANES * (j + 1)], 0.0)


def _pair_tile(even, odd, lane):
    return jnp.where(lane < HEAD_DIM, even, odd)


def _sigmoid(x):
    return 0.5 * jnp.tanh(0.5 * x) + 0.5


def _dot(a, b):
    return jnp.dot(a, b, preferred_element_type=F32)


def _dot_nt(a, b):
    return lax.dot_general(a, b, (((1,), (1,)), ((), ())), preferred_element_type=F32)


def _dot_tn(a, b):
    return lax.dot_general(a, b, (((0,), (0,)), ((), ())), preferred_element_type=F32)


def _dot01(tri, x):
    hi, mid, lo = _split3(x)
    return _dot(tri, hi.astype(BF16)) + _dot(tri, mid.astype(BF16)) + _dot(tri, lo.astype(BF16))


def _rms_bwd(dh, x, g):
    inv = lax.rsqrt(jnp.mean(x * x, axis=-1, keepdims=True) + RMS_EPS)
    xh = x * inv
    dxn = dh * g
    dx = inv * (dxn - xh * jnp.mean(dxn * xh, axis=-1, keepdims=True))
    return dx, jnp.sum(dh * xh, axis=0, keepdims=True)


def _head_rms_bwd(dn, t, g, ones):
    sq = t * t
    hi = sq.astype(BF16)
    lo = (sq - hi.astype(F32)).astype(BF16)
    inv = lax.rsqrt((_dot(hi, ones) + _dot(lo, ones)) * (1.0 / HEAD_DIM) + RMS_EPS)
    th = t * inv
    gd = dn * g
    d = inv * (gd - th * (jnp.sum(gd * th, axis=1, keepdims=True) * (1.0 / HEAD_DIM)))
    return d, jnp.sum(dn * th, axis=0, keepdims=True)


def _params(n_grid):
    return pltpu.CompilerParams(dimension_semantics=("arbitrary",) * n_grid, vmem_limit_bytes=VMEM_LIMIT)


def _rows(tm, cols, rev=None):
    if rev is None:
        return pl.BlockSpec((tm, cols), lambda i: (i, 0))
    return pl.BlockSpec((tm, cols), lambda i: (rev - i, 0))


def _whole(shape, buffers=None):
    mode = {} if buffers is None else dict(pipeline_mode=pl.Buffered(buffers))
    return pl.BlockSpec(shape, lambda *_: (0,) * len(shape), **mode)


def _conv_fwd(x, g1, w_in, conv_w, w_out, later):
    S, D = x.shape
    tm = min(CONV_TILE, S)
    sub = min(ROW_TILE, tm)
    steps = S // tm
    n = len(later)

    def body(x_ref, g_ref, win_ref, cw_ref, wout_ref, *rest):
        shard_refs, rest = rest[:n], rest[n:]
        proj_ref, h_ref, yc_ref, y_ref, x1_ref = rest[:5]
        gathered_refs, rest = rest[5:5 + n], rest[5 + n:]
        prev_u = rest[0]
        if n:
            start, forward, finish = _gather_plan(shard_refs, gathered_refs, rest[1], rest[2])
            pl.when(pl.program_id(0) == 0)(start)
            pl.when(pl.program_id(0) == steps // 2)(forward)

        @pl.when(pl.program_id(0) == 0)
        def _():
            prev_u[...] = jnp.zeros((sub, D), F32)

        for r in range(0, tm, sub):
            rows = slice(r, r + sub)
            xv = x_ref[rows, :]
            inv = lax.rsqrt(jnp.mean(xv * xv, axis=-1, keepdims=True) + RMS_EPS)
            h = (xv * inv * g_ref[...]).astype(BF16)
            h_ref[rows, :] = h
            for j in range(4):
                proj_ref[rows, j * D:(j + 1) * D] = _dot(h, win_ref[j])
            u = proj_ref[rows, D:2 * D] * proj_ref[rows, 2 * D:3 * D]
            pu = prev_u[...]
            row = lax.broadcasted_iota(jnp.int32, (sub, 1), 0)
            u1 = jnp.where(row < 1, pltpu.roll(pu, 1, 0), pltpu.roll(u, 1, 0))
            u2 = jnp.where(row < 2, pltpu.roll(pu, 2, 0), pltpu.roll(u, 2, 0))
            prev_u[...] = u
            w = cw_ref[...]
            yc = w[2:3] * u + w[1:2] * u1 + w[0:1] * u2
            yc_ref[rows, :] = yc
            z = proj_ref[rows, 3 * D:4 * D]
            y = (proj_ref[rows, 0:D] * yc * (z * _sigmoid(z))).astype(BF16)
            y_ref[rows, :] = y
            x1_ref[rows, :] = xv + _dot(y, wout_ref[...])

        if n:
            pl.when(pl.program_id(0) == steps - 1)(finish)

    results = pl.pallas_call(
        body, name="conv_fwd", grid=(steps,),
        in_specs=[_rows(tm, D), _whole((1, D)), _whole((4, D, D), 1), _whole((3, D)), _whole((D, D), 1)] + [ANY] * n,
        out_specs=[_rows(tm, 4 * D), _rows(tm, D), _rows(tm, D), _rows(tm, D), _rows(tm, D)] + [ANY] * n,
        out_shape=[jax.ShapeDtypeStruct((S, 4 * D), F32), jax.ShapeDtypeStruct((S, D), BF16),
                   jax.ShapeDtypeStruct((S, D), F32), jax.ShapeDtypeStruct((S, D), BF16),
                   jax.ShapeDtypeStruct((S, D), F32)] + [jax.ShapeDtypeStruct((4,) + a.shape, a.dtype) for a in later],
        scratch_shapes=[pltpu.VMEM((sub, D), F32)] + [pltpu.SemaphoreType.DMA((n, 7))] * (2 if n else 0),
        compiler_params=_params(1),
    )(x, g1, w_in, conv_w, w_out, *later)
    return results[:5], results[5:]


def _conv_bwd(dproj2, wa, wf, x1, g2, dx2, x, g1, w_in, w_out, conv_w, proj, yc):
    S, D = x.shape
    tm = min(ROW_TILE, S)
    sub = min(ROW_TILE, tm)
    last = S // tm - 1

    def body(dp2_ref, wa_ref, wf_ref, x1_ref, g2_ref, dx2_ref, x_ref, g_ref, win_ref, wout_ref, cw_ref, proj_ref, yc_ref,
             dproj_ref, dx_ref, dx1b_ref, dg_ref, dcw_ref, dg2_ref, next_d):
        @pl.when(pl.program_id(0) == 0)
        def _():
            dg_ref[...] = jnp.zeros((1, D), F32)
            dcw_ref[...] = jnp.zeros((3, D), F32)
            dg2_ref[...] = jnp.zeros((1, D), F32)
            next_d[...] = jnp.zeros((sub, D), F32)

        for r in range(tm - sub, -1, -sub):
            rows = slice(r, r + sub)
            dh2 = _dot_nt(dp2_ref[rows, 0:4 * D], wa_ref[...]) + _dot_nt(dp2_ref[rows, 4 * D:4 * D + LANES], wf_ref[...])
            dxn2, dg2 = _rms_bwd(dh2, x1_ref[rows, :], g2_ref[...])
            dg2_ref[...] += dg2
            dx1v = dx2_ref[rows, :] + dxn2
            dx1b = dx1v.astype(BF16)
            dx1b_ref[rows, :] = dx1b
            dy = _dot_nt(dx1b, wout_ref[...])
            b = proj_ref[rows, 0:D]
            c = proj_ref[rows, D:2 * D]
            xin = proj_ref[rows, 2 * D:3 * D]
            z = proj_ref[rows, 3 * D:4 * D]
            sg = _sigmoid(z)
            sz = z * sg
            ycv = yc_ref[rows, :]
            d0 = dy * b * sz
            dproj_ref[rows, 0:D] = (dy * ycv * sz).astype(BF16)
            dproj_ref[rows, 3 * D:4 * D] = (dy * b * ycv * (sg * (1.0 + z * (1.0 - sg)))).astype(BF16)
            nd = next_d[...]
            row = lax.broadcasted_iota(jnp.int32, (sub, 1), 0)
            d1 = jnp.where(row >= sub - 1, pltpu.roll(nd, sub - 1, 0), pltpu.roll(d0, sub - 1, 0))
            d2 = jnp.where(row >= sub - 2, pltpu.roll(nd, sub - 2, 0), pltpu.roll(d0, sub - 2, 0))
            next_d[...] = d0
            w = cw_ref[...]
            du = w[2:3] * d0 + w[1:2] * d1 + w[0:1] * d2
            u = c * xin
            dcw_ref[2:3, :] += jnp.sum(d0 * u, axis=0, keepdims=True)
            dcw_ref[1:2, :] += jnp.sum(d1 * u, axis=0, keepdims=True)
            dcw_ref[0:1, :] += jnp.sum(d2 * u, axis=0, keepdims=True)
            dproj_ref[rows, D:2 * D] = (du * xin).astype(BF16)
            dproj_ref[rows, 2 * D:3 * D] = (du * c).astype(BF16)
            dh = _dot_nt(dproj_ref[rows, 0:D], win_ref[0])
            for j in range(1, 4):
                dh = dh + _dot_nt(dproj_ref[rows, j * D:(j + 1) * D], win_ref[j])
            dxn, dg = _rms_bwd(dh, x_ref[rows, :], g_ref[...])
            dx_ref[rows, :] = dx1v + dxn
            dg_ref[...] += dg

    return pl.pallas_call(
        body, name="conv_bwd", grid=(S // tm,),
        in_specs=[_rows(tm, 4 * D + LANES, last), _whole((D, 4 * D), 1), _whole((D, LANES), 1), _rows(tm, D, last),
                  _whole((1, D)), _rows(tm, D, last),
                  _rows(tm, D, last), _whole((1, D)), _whole((4, D, D), 1), _whole((D, D), 1),
                  _whole((3, D)), _rows(tm, 4 * D, last), _rows(tm, D, last)],
        out_specs=[_rows(tm, 4 * D, last), _rows(tm, D, last), _rows(tm, D, last), _whole((1, D)), _whole((3, D)),
                   _whole((1, D))],
        out_shape=[jax.ShapeDtypeStruct((S, 4 * D), BF16), jax.ShapeDtypeStruct((S, D), F32),
                   jax.ShapeDtypeStruct((S, D), BF16), jax.ShapeDtypeStruct((1, D), F32),
                   jax.ShapeDtypeStruct((3, D), F32), jax.ShapeDtypeStruct((1, D), F32)],
        scratch_shapes=[pltpu.VMEM((sub, D), F32)],
        compiler_params=_params(1),
    )(dproj2, wa, wf, x1, g2, dx2, x, g1, w_in, w_out, conv_w, proj, yc)


def _attn_front(x1, g2, w, wf, bf, gq, gk):
    S, D = x1.shape
    H = D // HEAD_DIM
    tm = min(ROW_TILE, S)
    tri = (lax.broadcasted_iota(jnp.int32, (tm, tm), 1) <= lax.broadcasted_iota(jnp.int32, (tm, tm), 0)).astype(BF16)

    def body(x_ref, g_ref, w_ref, wf_ref, bf_ref, gq_ref, gk_ref, tri_ref, first_ref, second_ref,
             h_ref, qh_ref, kh_ref, z_ref, f_ref, ends_ref, rel_ref, qa_ref, ka_ref, va_ref, vt_ref,
             carry, v_s, qraw_ref, kraw_ref):
        @pl.when(pl.program_id(0) == 0)
        def _():
            carry[...] = jnp.zeros((8, LANES), F32)

        xv = x_ref[...]
        inv = lax.rsqrt(jnp.mean(xv * xv, axis=-1, keepdims=True) + RMS_EPS)
        h = (xv * inv * g_ref[...]).astype(BF16)
        h_ref[...] = h
        qraw_ref[...] = _dot(h, w_ref[:, 0:D])
        kraw_ref[...] = _dot(h, w_ref[:, D:2 * D])
        v_s[...] = _dot(h, w_ref[:, 2 * D:3 * D])
        z_ref[...] = _dot(h, w_ref[:, 3 * D:4 * D])
        lane = _lane()
        f = _dot(h, wf_ref[...]) + bf_ref[...]
        f_ref[...] = f
        logf = jnp.where(lane < H, jnp.minimum(f, 0.0) - jnp.log(1.0 + jnp.exp(-jnp.abs(f))), 0.0)
        cs = _dot01(tri_ref[...], logf) + carry[0:1, :]
        carry[...] = jnp.broadcast_to(cs[tm - 1:tm, :], (8, LANES))
        diags = jnp.zeros((tm, LANES), F32)
        for hd in range(H):
            sl = slice(LANES * hd, LANES * (hd + 1))
            a = _aug(hd % 2)
            if hd % 2 == 0:
                qh_ref[hd // 2] = qraw_ref[:, LANES * (hd // 2):LANES * (hd // 2 + 1)]
                kh_ref[hd // 2] = kraw_ref[:, LANES * (hd // 2):LANES * (hd // 2 + 1)]
            qt = _head_tile(qraw_ref, hd, lane)
            qn = qt * lax.rsqrt(jnp.sum(qt * qt, axis=1, keepdims=True) * (1.0 / HEAD_DIM) + RMS_EPS) * gq_ref[...]
            kt = _head_tile(kraw_ref, hd, lane)
            kn = kt * lax.rsqrt(jnp.sum(kt * kt, axis=1, keepdims=True) * (1.0 / HEAD_DIM) + RMS_EPS) * gk_ref[...]
            diags = diags + jnp.where(lane == hd, jnp.sum(qn * kn, axis=1, keepdims=True) * Q_SCALE, 0.0)
            qa_ref[:, sl] = (qn * Q_SCALE).astype(BF16)
            ka_ref[:, sl] = kn.astype(BF16)
            va = jnp.where((lane >= a) & (lane < a + 3), 1.0, _head_tile(v_s, hd, lane))
            va_ref[:, sl] = va.astype(BF16)
            vt_ref[hd] = va.T.astype(BF16)
        rel = cs - diags
        rel_ref[...] = rel
        row = lax.broadcasted_iota(jnp.int32, (8, LANES), 0)
        ends_ref[...] = jnp.where(row == 0, cs[0:1, :],
                                  jnp.where(row == 1, cs[tm - 1:tm, :], jnp.min(diags, axis=0, keepdims=True)))
        qa_ref[...] += _dot(_pack3(rel, lane, 1.0), first_ref[...]).astype(BF16)
        ka_ref[...] += _dot(_pack3(-cs, lane, 1.0), second_ref[...]).astype(BF16)

    nb = S // tm
    heads = pl.BlockSpec((H // 2, tm, LANES), lambda i: (0, i, 0))
    return pl.pallas_call(
        body, name="attn_front", grid=(nb,),
        in_specs=[_rows(tm, D), _whole((1, D)), _whole((D, 4 * D)), _whole((D, LANES)), _whole((1, LANES)),
                  _whole((1, LANES)), _whole((1, LANES)), _whole((tm, tm)), _whole((LANES, H * LANES)),
                  _whole((LANES, H * LANES))],
        out_specs=[_rows(tm, D), heads, heads, _rows(tm, D), _rows(tm, LANES),
                   pl.BlockSpec((None, 8, LANES), lambda i: (i, 0, 0)), _rows(tm, LANES),
                   _rows(tm, H * LANES), _rows(tm, H * LANES), _rows(tm, H * LANES),
                   pl.BlockSpec((H, None, LANES, tm), lambda i: (0, i, 0, 0))],
        out_shape=[jax.ShapeDtypeStruct((S, D), BF16), jax.ShapeDtypeStruct((H // 2, S, LANES), F32),
                   jax.ShapeDtypeStruct((H // 2, S, LANES), F32), jax.ShapeDtypeStruct((S, D), F32),
                   jax.ShapeDtypeStruct((S, LANES), F32), jax.ShapeDtypeStruct((nb, 8, LANES), F32),
                   jax.ShapeDtypeStruct((S, LANES), F32),
                   jax.ShapeDtypeStruct((S, H * LANES), BF16), jax.ShapeDtypeStruct((S, H * LANES), BF16),
                   jax.ShapeDtypeStruct((S, H * LANES), BF16), jax.ShapeDtypeStruct((H, nb, LANES, tm), BF16)],
        scratch_shapes=[pltpu.VMEM((8, LANES), F32), pltpu.VMEM((tm, D), F32), pltpu.VMEM((tm, D), F32),
                        pltpu.VMEM((tm, D), F32)],
        compiler_params=_params(1),
    )(x1, g2, w, wf, bf, gq, gk, tri, *_scatter_matrices(H))


def _skip_tables(first, last, lowest, gq, gk, G):
    nb = first.shape[0]
    bound = HEAD_DIM ** 0.5 * jnp.max(jnp.abs(gq)) * jnp.max(jnp.abs(gk))
    lowest = jnp.maximum(lowest, -bound)
    idx = jnp.arange(nb)
    margin = (SKIP_LOG + bound) - lowest
    need = (last[None, :, :] <= first[:, None, :] + margin[:, None, :]) & (idx[None, :, None] < idx[:, None, None])
    need = need | (idx[None, :, None] == idx[:, None, None])
    kstart = jnp.argmax(need, axis=1)
    qend = nb - 1 - jnp.argmax(need[::-1], axis=0)
    kstart = jnp.min(kstart.reshape(2 * nb // G, G // 2, -1), axis=1)
    kstart = kstart - (kstart & 1)
    qend = jnp.max(qend.reshape(2 * nb // G, G // 2, -1), axis=1)
    return kstart.T.astype(jnp.int32), qend.T.astype(jnp.int32), bound


def _attn_fwd(kstart, qa, ka, vt, online_max):
    S = qa.shape[0]
    H = qa.shape[1] // LANES
    nb, T = vt.shape[1], vt.shape[3]
    G = 2 * nb // kstart.shape[1]
    W = G * T

    def finish(acc, shift, o_ref, lse_ref):
        a = _aug(pl.program_id(0) % 2)
        feat = lax.broadcasted_iota(jnp.int32, (LANES, 1), 0)
        l = jnp.sum(jnp.where(feat == a, acc, 0.0), axis=0, keepdims=True)
        o_ref[...] = (acc * (1.0 / l)).T
        lse_ref[...] = shift + jnp.log(l)

    def causal(st):
        return jnp.where(lax.broadcasted_iota(jnp.int32, st.shape, 0) <= lax.broadcasted_iota(jnp.int32, st.shape, 1),
                         st, NEG)

    def fast_body(ks_ref, q_ref, k_ref, vt_ref, o_ref, lse_ref, acc_ref, sa_ref, sb_ref, sc_ref):
        h, g = pl.program_id(0), pl.program_id(1)
        q = q_ref[...]
        acc_ref[...] = jnp.zeros((LANES, W), F32)

        def scores(ki, lo):
            return _dot_nt(k_ref[pl.ds(pl.multiple_of(ki * T, T), 2 * T), :], q[lo * T:, :])

        def weighted(ki, p):
            return _dot(vt_ref[ki], p[:T]) + _dot(vt_ref[ki + 1], p[T:])

        first = ks_ref[h, 2 * g + 1]
        early = jnp.minimum(ks_ref[h, 2 * g], first)

        def narrow(i, carry):
            ki = early + 2 * i
            st = _dot_nt(k_ref[pl.ds(pl.multiple_of(ki * T, T), 2 * T), :], q[:W // 2, :])
            acc_ref[:, :W // 2] += weighted(ki, jnp.exp(st).astype(BF16))
            return carry

        lax.fori_loop(0, (first - early) // 2, narrow, 0)
        steps = (g * G - first) // 2
        sa_ref[...] = scores(first, 0)

        def advance(ki, cur_ref, next_ref):
            p = jnp.exp(cur_ref[...]).astype(BF16)
            next_ref[...] = scores(ki + 2, 0)
            acc_ref[...] += weighted(ki, p)

        def loop(i, carry):
            advance(first + 4 * i, sa_ref, sb_ref)
            advance(first + 4 * i + 2, sb_ref, sa_ref)
            return carry

        lax.fori_loop(0, steps // 2, loop, 0)

        def first_own(pending_ref):
            p = jnp.exp(causal(pending_ref[...])).astype(BF16)
            if G > 2:
                sc_ref[:, :W - 2 * T] = scores(g * G + 2, 2)
            acc_ref[...] += weighted(g * G, p)

        @pl.when(steps % 2 == 1)
        def _():
            advance(g * G - 2, sa_ref, sb_ref)
            first_own(sb_ref)

        @pl.when(steps % 2 == 0)
        def _():
            first_own(sa_ref)

        if G > 2:
            acc_ref[:, 2 * T:] += weighted(g * G + 2, jnp.exp(causal(sc_ref[:, :W - 2 * T])).astype(BF16))
        for j in range(4, G, 2):
            p = jnp.exp(causal(scores(g * G + j, j))).astype(BF16)
            acc_ref[:, j * T:] += weighted(g * G + j, p)
        finish(acc_ref[...], 0.0, o_ref, lse_ref)

    def online_body(ks_ref, q_ref, k_ref, vt_ref, o_ref, lse_ref, acc_ref, m_ref):
        h, g = pl.program_id(0), pl.program_id(1)
        q = q_ref[...]
        m_ref[...] = jnp.full((8, W), NEG, F32)
        acc_ref[...] = jnp.zeros((LANES, W), F32)

        def update(st, vtb, lo):
            m_old = m_ref[0:1, lo:]
            m_new = jnp.maximum(m_old, jnp.max(st, axis=0, keepdims=True))
            p = jnp.exp(st - m_new).astype(BF16)
            acc_ref[:, lo:] = acc_ref[:, lo:] * jnp.exp(m_old - m_new) + _dot(vtb, p)
            m_ref[:, lo:] = jnp.broadcast_to(m_new, (8, W - lo))

        def loop(ki, carry):
            kb = k_ref[pl.ds(pl.multiple_of(ki * T, T), T), :]
            update(_dot_nt(kb, q), vt_ref[ki], 0)
            return carry

        lax.fori_loop(jnp.minimum(ks_ref[h, 2 * g], ks_ref[h, 2 * g + 1]), g * G, loop, 0)
        for j in range(G):
            ki = g * G + j
            kb = k_ref[pl.ds(pl.multiple_of(ki * T, T), T), :]
            update(causal(_dot_nt(kb, q[j * T:, :])), vt_ref[ki], j * T)
        finish(acc_ref[...], m_ref[0:1, :], o_ref, lse_ref)

    return pl.pallas_call(
        online_body if online_max else fast_body, name="attn_fwd_online" if online_max else "attn_fwd",
        grid_spec=pltpu.PrefetchScalarGridSpec(
            num_scalar_prefetch=1, grid=(H, nb // G),
            in_specs=[pl.BlockSpec((W, LANES), lambda h, i, ks: (i, h)),
                      pl.BlockSpec((S, LANES), lambda h, i, ks: (0, h)),
                      pl.BlockSpec((None, nb, LANES, T), lambda h, i, ks: (h, 0, 0, 0))],
            out_specs=[pl.BlockSpec((W, LANES), lambda h, i, ks: (i, h)),
                       pl.BlockSpec((None, 1, W), lambda h, i, ks: (h, 0, i))],
            scratch_shapes=[pltpu.VMEM((LANES, W), F32)] + (
                [pltpu.VMEM((8, W), F32)] if online_max else [pltpu.VMEM((2 * T, W), F32)] * 3)),
        out_shape=[jax.ShapeDtypeStruct((S, H * LANES), F32), jax.ShapeDtypeStruct((H, 1, S), F32)],
        compiler_params=_params(2),
    )(kstart, qa, ka, vt)


def _attn_out(o_aug, lse, rel, z, x1, target, w_out, qa):
    S, D = x1.shape
    H = D // HEAD_DIM
    tm = min(ROW_TILE, S)

    def body(o_ref, z_ref, x1_ref, t_ref, w_ref, q_ref, first_ref, rel_ref, lse_ref,
             dx2_ref, dx2b_ref, o2b_ref, dz_ref, doa_ref, qa2_ref, loss_ref, oc_s, do_s):
        @pl.when(pl.program_id(0) == 0)
        def _():
            loss_ref[...] = jnp.zeros((1, LANES), F32)

        lane = _lane()
        for j in range(H // 2):
            oc_s[:, LANES * j:LANES * (j + 1)] = _pair_tile(o_ref[:, 2 * LANES * j:2 * LANES * j + LANES],
                                                            o_ref[:, 2 * LANES * j + LANES:2 * LANES * (j + 1)], lane)
        oc = oc_s[...]
        zv = z_ref[...]
        sg = _sigmoid(zv)
        sz = zv * sg
        o2 = (oc * sz).astype(BF16)
        o2b_ref[...] = o2
        e = x1_ref[...] + _dot(o2, w_ref[...]) - t_ref[...]
        sq = jnp.sum(jnp.sum(e * e, axis=1, keepdims=True), axis=0, keepdims=True)
        loss_ref[...] += jnp.broadcast_to(sq * (0.5 / D), (1, LANES))
        dx2 = e * (1.0 / D)
        dx2_ref[...] = dx2
        dx2b = dx2.astype(BF16)
        dx2b_ref[...] = dx2b
        do2 = _dot_nt(dx2b, w_ref[...])
        dz_ref[...] = (do2 * oc * (sg * (1.0 + zv * (1.0 - sg)))).astype(BF16)
        do_s[...] = do2 * sz
        deltas = jnp.zeros((tm, LANES), F32)
        for hd in range(H):
            dt = _head_tile(do_s, hd, lane)
            delta = jnp.sum(dt * _head_tile(oc_s, hd, lane), axis=1, keepdims=True)
            deltas = deltas + jnp.where(lane == hd, delta, 0.0)
            doa_ref[:, LANES * hd:LANES * (hd + 1)] = dt.astype(BF16)
        doa_ref[...] += _dot(_pack3(-deltas, lane, 0.0), first_ref[...]).astype(BF16)
        lse = jnp.concatenate([lse_ref[...], jnp.zeros((LANES - H, tm), F32)], axis=0).T
        rq = rel_ref[...] - lse
        tile_lane = lax.broadcasted_iota(jnp.int32, (1, H * LANES), 1)
        extra = tile_lane % LANES - _aug((tile_lane // LANES) % 2)
        kept = jnp.where((extra >= 0) & (extra < 3), jnp.zeros((), BF16), q_ref[...])
        qa2_ref[...] = kept + _dot(_pack3(rq, lane, 0.0), first_ref[...]).astype(BF16)

    return pl.pallas_call(
        body, name="attn_out", grid=(S // tm,),
        in_specs=[_rows(tm, H * LANES), _rows(tm, D), _rows(tm, D), _rows(tm, D), _whole((D, D)),
                  _rows(tm, H * LANES), _whole((LANES, H * LANES)), _rows(tm, LANES),
                  pl.BlockSpec((H, tm), lambda i: (0, i))],
        out_specs=[_rows(tm, D), _rows(tm, D), _rows(tm, D), _rows(tm, D), _rows(tm, H * LANES),
                   _rows(tm, H * LANES), _whole((1, LANES))],
        out_shape=[jax.ShapeDtypeStruct((S, D), F32), jax.ShapeDtypeStruct((S, D), BF16),
                   jax.ShapeDtypeStruct((S, D), BF16), jax.ShapeDtypeStruct((S, D), BF16),
                   jax.ShapeDtypeStruct((S, H * LANES), BF16), jax.ShapeDtypeStruct((S, H * LANES), BF16),
                   jax.ShapeDtypeStruct((1, LANES), F32)],
        scratch_shapes=[pltpu.VMEM((tm, D), F32), pltpu.VMEM((tm, D), F32)],
        compiler_params=_params(1),
    )(o_aug, z, x1, target, w_out, qa, _scatter_matrices(H)[0], rel, lse)


def _attn_bwd(qend, qa2, doa, ka, va, T):
    S = qa2.shape[0]
    H = qa2.shape[1] // LANES
    nb = S // T
    G = 2 * nb // qend.shape[1]
    W = G * T

    def body(qe_ref, q_ref, do_ref, k_ref, v_ref, dq_ref, dk_ref, dv_ref, dkt_acc, dvt_acc):
        h, g = pl.program_id(0), pl.program_id(1)

        @pl.when(g == 0)
        def _():
            dq_ref[...] = jnp.zeros((S, LANES), F32)

        kb = k_ref[...]
        vb = v_ref[...]
        dkt_acc[...] = jnp.zeros((LANES, W), F32)
        dvt_acc[...] = jnp.zeros((LANES, W), F32)

        def step(qi, c0, c1, masked):
            rows = pl.ds(pl.multiple_of(qi * T, T), 2 * T)
            qb = q_ref[rows, :]
            dob = do_ref[rows, :]
            s = _dot_nt(qb, kb[c0:c1])
            if masked:
                query = lax.broadcasted_iota(jnp.int32, s.shape, 0) + (c1 - 2 * T)
                s = jnp.where(lax.broadcasted_iota(jnp.int32, s.shape, 1) <= query, s, NEG)
            p = jnp.exp(s)
            ds = (p * _dot_nt(dob, vb[c0:c1])).astype(BF16)
            dvt_acc[:, c0:c1] += _dot(dob.astype(F32).T.astype(BF16), p.astype(BF16))
            dkt_acc[:, c0:c1] += _dot(qb.astype(F32).T.astype(BF16), ds)
            dq_ref[rows, :] += _dot(ds, kb[c0:c1])

        for m in range(G // 2):
            step(g * G + 2 * m, 0, (m + 1) * 2 * T, True)
        first = g * G + G
        n_all = jnp.maximum((qe_ref[h, 2 * g] - first + 2) // 2, 0)
        second = first + 2 * n_all

        def all_keys(i, carry):
            step(first + 2 * i, 0, W, False)
            return carry

        def late_keys(i, carry):
            step(second + 2 * i, W // 2, W, False)
            return carry

        lax.fori_loop(0, n_all, all_keys, 0)
        lax.fori_loop(0, (qe_ref[h, 2 * g + 1] - second + 2) // 2, late_keys, 0)
        dk_ref[...] = dkt_acc[...].T
        dv_ref[...] = dvt_acc[...].T.astype(BF16)

    heads = pl.BlockSpec((None, W, LANES), lambda h, i, qe: (h, i, 0))
    return pl.pallas_call(
        body, name="attn_bwd",
        grid_spec=pltpu.PrefetchScalarGridSpec(
            num_scalar_prefetch=1, grid=(H, nb // G),
            in_specs=[pl.BlockSpec((S, LANES), lambda h, i, qe: (0, h)), pl.BlockSpec((S, LANES), lambda h, i, qe: (0, h)),
                      pl.BlockSpec((W, LANES), lambda h, i, qe: (i, h)), pl.BlockSpec((W, LANES), lambda h, i, qe: (i, h))],
            out_specs=[pl.BlockSpec((None, S, LANES), lambda h, i, qe: (h, 0, 0)), heads, heads],
            scratch_shapes=[pltpu.VMEM((LANES, W), F32), pltpu.VMEM((LANES, W), F32)]),
        out_shape=[jax.ShapeDtypeStruct((H, S, LANES), F32), jax.ShapeDtypeStruct((H, S, LANES), F32),
                   jax.ShapeDtypeStruct((H, S, LANES), BF16)],
        compiler_params=_params(2),
    )(qend, qa2, doa, ka, va)


def _attn_proj_bwd(dqt, dka, dva, qraw, kraw, dz, f, gq, gk):
    S, D = dz.shape
    H = D // HEAD_DIM
    tm = min(CONV_TILE, S)
    last = S // tm - 1
    tri = (lax.broadcasted_iota(jnp.int32, (tm, tm), 1) >= lax.broadcasted_iota(jnp.int32, (tm, tm), 0)).astype(BF16)

    def body(dq_ref, dk_ref, dv_ref, q_ref, k_ref, dz_ref, f_ref, gq_ref, gk_ref, tri_ref, ones_ref,
             dproj_ref, small_ref, carry, pairs):
        @pl.when(pl.program_id(0) == 0)
        def _():
            small_ref[...] = jnp.zeros((8, LANES), F32)
            carry[...] = jnp.zeros((8, LANES), F32)

        lane = _lane()

        def head_pair(j, acc):
            dcs, dgq, dgk = acc
            dq2, dk2 = [], []
            q_pair, k_pair = q_ref[j], k_ref[j]
            for parity in (0, 1):
                hd = 2 * j + parity
                own, a = _own(lane, parity), _aug(parity)
                dqf = dq_ref[hd]
                dqn = jnp.where(own, dqf * Q_SCALE, 0.0)
                d, dg = _head_rms_bwd(dqn, jnp.where(own, q_pair, 0.0), gq_ref[...], ones_ref[...])
                dq2.append(d)
                dgq = dgq + dg
                dkt = dk_ref[hd]
                dcs = dcs + jnp.where(lane == hd, _col(dqf, lane, a) - _col(dkt, lane, a + 3), 0.0)
                d, dg = _head_rms_bwd(jnp.where(own, dkt, 0.0), jnp.where(own, k_pair, 0.0), gk_ref[...], ones_ref[...])
                dk2.append(d)
                dgk = dgk + dg
            pairs[0, j] = _pair_tile(*dq2, lane).astype(BF16)
            pairs[1, j] = _pair_tile(*dk2, lane).astype(BF16)
            pairs[2, j] = _pair_tile(dv_ref[2 * j], dv_ref[2 * j + 1], lane)
            return dcs, dgq, dgk

        zero = jnp.zeros((1, LANES), F32)
        dcs, dgq, dgk = lax.fori_loop(0, H // 2, head_pair, (jnp.zeros((tm, LANES), F32), zero, zero))
        for part in range(3):
            for j in range(H // 2):
                dproj_ref[:, part * D + LANES * j:part * D + LANES * (j + 1)] = pairs[part, j]
        dproj_ref[:, 3 * D:4 * D] = dz_ref[...]
        dlogf = _dot01(tri_ref[...], dcs) + carry[0:1, :]
        carry[...] = jnp.broadcast_to(dlogf[0:1, :], (8, LANES))
        df = dlogf * (1.0 / (1.0 + jnp.exp(f_ref[...])))
        dproj_ref[:, 4 * D:4 * D + LANES] = df.astype(BF16)
        small_ref[0:1, :] += jnp.sum(df, axis=0, keepdims=True)
        small_ref[1:2, :] += dgq
        small_ref[2:3, :] += dgk

    W = 4 * D + LANES
    heads = pl.BlockSpec((H, tm, LANES), lambda i: (0, last - i, 0))
    head_pairs = pl.BlockSpec((H // 2, tm, LANES), lambda i: (0, last - i, 0))
    return pl.pallas_call(
        body, name="attn_proj_bwd", grid=(S // tm,),
        in_specs=[heads, heads, heads, head_pairs, head_pairs,
                  _rows(tm, D, last), _rows(tm, LANES, last), _whole((1, LANES)), _whole((1, LANES)),
                  _whole((tm, tm)), _whole((LANES, LANES))],
        out_specs=[_rows(tm, W, last), _whole((8, LANES))],
        out_shape=[jax.ShapeDtypeStruct((S, W), BF16), jax.ShapeDtypeStruct((8, LANES), F32)],
        scratch_shapes=[pltpu.VMEM((8, LANES), F32), pltpu.VMEM((3, H // 2, tm, LANES), BF16)],
        compiler_params=_params(1),
    )(dqt, dka, dva, qraw, kraw, dz, f, gq, gk, tri, jnp.ones((LANES, LANES), BF16))


def _matmul_tn(a, b, col0, n, tn, name, stacked=False, carried=None):
    S, M = a.shape
    ts = min(TN_ROWS, S)
    off = col0 // tn
    grid = (n // tn, S // ts)
    sent, lands, build = carried or ([], [], None)
    k = len(sent)

    def body(a_ref, b_ref, *rest):
        o_ref = rest[k]
        j, s = pl.program_id(0), pl.program_id(1)
        if k:
            copies = build(rest[:k], rest[k + 1:2 * k + 1], rest[2 * k + 1], rest[2 * k + 2])

            @pl.when((j == 0) & (s == 0))
            def _():
                for cp in copies:
                    cp.start()

        @pl.when(s == 0)
        def _():
            o_ref[...] = jnp.zeros((M, tn), F32)

        o_ref[...] += _dot_tn(a_ref[...], b_ref[...])

        if k:
            @pl.when((j == grid[0] - 1) & (s == grid[1] - 1))
            def _():
                for cp in copies:
                    cp.wait()

    if stacked:
        out_spec, out_shape = pl.BlockSpec((None, M, tn), lambda j, s: (j, 0, 0)), (n // tn, M, tn)
    else:
        out_spec, out_shape = pl.BlockSpec((M, tn), lambda j, s: (0, j)), (M, n)
    results = pl.pallas_call(
        body, name=name, grid=grid,
        in_specs=[pl.BlockSpec((ts, M), lambda j, s: (s, 0)), pl.BlockSpec((ts, tn), lambda j, s: (s, off + j))]
        + [ANY] * k,
        out_specs=[out_spec] + [ANY] * k,
        out_shape=[jax.ShapeDtypeStruct(out_shape, F32)] + list(lands),
        scratch_shapes=[pltpu.SemaphoreType.DMA((k, 4))] * (2 if k else 0),
        compiler_params=_params(2),
    )(a, b, *sent)
    return results[0] if carried is None else (results[0], list(results[1:]))


def _adam_update(gv, w_ref, m_ref, v_ref, d_ref, m2_ref, v2_ref):
    m2 = ADAM_B1 * m_ref[...] + (1.0 - ADAM_B1) * gv
    v2 = ADAM_B2 * v_ref[...] + (1.0 - ADAM_B2) * (gv * gv)
    m2_ref[...] = m2
    v2_ref[...] = v2
    m_hat = m2 / (1.0 - ADAM_B1 ** ADAM_STEP)
    v_hat = v2 / (1.0 - ADAM_B2 ** ADAM_STEP)
    d_ref[...] = -ADAM_LR * (m_hat / (jnp.sqrt(v_hat) + ADAM_EPS) + ADAM_WD * w_ref[...])


def _adamw(w, g, m, v, name):
    r, c = w.shape
    tr = ROW_TILE if r % ROW_TILE == 0 else r

    def body(w_ref, g_ref, m_ref, v_ref, d_ref, m2_ref, v2_ref):
        _adam_update(g_ref[...], w_ref, m_ref, v_ref, d_ref, m2_ref, v2_ref)

    spec = _rows(tr, c)
    return pl.pallas_call(
        body, name=name, grid=(r // tr,), in_specs=[spec] * 4, out_specs=[spec] * 3,
        out_shape=[jax.ShapeDtypeStruct((r, c), F32)] * 3, compiler_params=_params(1),
    )(w, g, m, v)


def _adamw_halves(w, mine, other, m, v, core, name):
    r, c = mine.shape
    tr = ROW_TILE if r % ROW_TILE == 0 else r
    per = r // tr

    def body(core_ref, w_ref, mine_ref, other_ref, m_ref, v_ref, g_ref, d_ref, m2_ref, v2_ref):
        gv = jnp.where(pl.program_id(0) // per == core_ref[0], mine_ref[...], other_ref[...])
        g_ref[...] = gv
        _adam_update(gv, w_ref, m_ref, v_ref, d_ref, m2_ref, v2_ref)

    full = pl.BlockSpec((tr, c), lambda i, core: (i, 0))
    half = pl.BlockSpec((tr, c), lambda i, core: (i % per, 0))
    return pl.pallas_call(
        body, name=name,
        grid_spec=pltpu.PrefetchScalarGridSpec(num_scalar_prefetch=1, grid=(2 * per,),
                                               in_specs=[full, half, half, full, full], out_specs=[full] * 4),
        out_shape=[jax.ShapeDtypeStruct((2 * r, c), F32)] * 4, compiler_params=_params(1),
    )(core, w, mine, other, m, v)


def _after_conv(conv_acts, x, target, g1, w_in, conv_w, w_out, g2, wa_in, b_f, gq, gk, wa_out, reduce_early=None):
    reduces = reduce_early is not None
    if not reduces:
        reduce_early = lambda by_chip, tag, swapped: ([], [])
    S, D = x.shape
    H = D // HEAD_DIM
    ws = wa_in.shape[2]
    w_qkvz = jnp.concatenate([wa_in[0], wa_in[1], wa_in[2], wa_in[3][:, :4 * D - 3 * ws]], axis=1)
    wf = jnp.pad(wa_in[3][:, 4 * D - 3 * ws:], ((0, 0), (0, LANES - H)))
    bf = jnp.pad(b_f, ((0, 0), (0, LANES - H)))
    gq128 = jnp.concatenate([gq, gq], axis=1)
    gk128 = jnp.concatenate([gk, gk], axis=1)

    proj, h1, yc, y, x1 = conv_acts
    h2, qraw, kraw, z, f, ends, rel, qa, ka, va, vt = _attn_front(x1, g2, w_qkvz, wf, bf, gq128, gk128)
    T = vt.shape[3]
    kstart, qend, bound = _skip_tables(ends[:, 0, :H], ends[:, 1, :H], ends[:, 2, :H], gq, gk, min(ATT_GROUP, S // T))
    o_aug, lse = lax.cond(2.0 * bound <= PLAIN_EXP_MAX, functools.partial(_attn_fwd, online_max=False),
                          functools.partial(_attn_fwd, online_max=True), kstart, qa, ka, vt)
    dx2, dx2b, o2b, dz, doa, qa2, loss = _attn_out(o_aug, lse.reshape(H, S), rel, z, x1, target, wa_out, qa)
    dqt, dka, dva = _attn_bwd(qend, qa2, doa, ka, va, T)
    dproj2, small = _attn_proj_bwd(dqt, dka, dva, qraw, kraw, dz, f, gq128, gk128)

    tn = min(1024, D)
    carry = (lambda kind, arrays: kind(arrays if reduces else []))
    dw_main = _matmul_tn(h2, dproj2, 0, 4 * D, 2 * tn, "dw_attn_in")
    dw_f = _matmul_tn(h2, dproj2, 4 * D, LANES, LANES, "dw_attn_f")
    dwa_in = jnp.stack([dw_main[:, 0:ws], dw_main[:, ws:2 * ws], dw_main[:, 2 * ws:3 * ws],
                        jnp.concatenate([dw_main[:, 3 * ws:], dw_f[:, :H]], axis=1)])
    dwa_out, swapped = _matmul_tn(o2b, dx2b, 0, D, tn, "dw_attn_out",
                                  carried=carry(_carried_core_swap, [dwa_in.reshape(4, 2, D // 2, ws)]))
    attn_f32, attn_bf16 = reduce_early([dwa_in, dwa_out.reshape(4, D // 4, D)], "attn", swapped)
    dproj1, dx, dx1b, dg1, dcw, dg2 = _conv_bwd(dproj2, w_qkvz, wf, x1, g2, dx2, x, g1, w_in, w_out, conv_w, proj, yc)
    dw_in, attn_arrived = _matmul_tn(h1, dproj1, 0, 4 * D, D, "dw_conv_in", stacked=True,
                                     carried=carry(_carried_chip_sums, attn_bf16))
    in_f32, in_bf16 = reduce_early([dw_in], "conv_in", [])
    dw_out, in_arrived = _matmul_tn(y, dx1b, 0, D, tn, "dw_conv_out", carried=carry(_carried_chip_sums, in_bf16))
    grads = dict(conv_norm_g=dg1, conv_w_in=dw_in, conv_w=dcw, conv_w_out=dw_out, attn_norm_g=dg2,
                 attn_w_in=dwa_in, attn_b_f=small[0:1, :H],
                 attn_q_norm_g=small[1:2, :HEAD_DIM] + small[1:2, HEAD_DIM:],
                 attn_k_norm_g=small[2:3, :HEAD_DIM] + small[2:3, HEAD_DIM:], attn_w_out=dwa_out)
    return loss[0, 0], dx, grads, (attn_f32 + in_f32, attn_arrived + in_arrived)


def _coords():
    return lax.axis_index("x"), lax.axis_index("y"), lax.axis_index("c")


def _at(ref, idx):
    return ref.at[idx] if idx else ref


def _other_chips(x, y):
    return [(1 - x, y), (x, 1 - y), (1 - x, 1 - y)]


def _gather_plan(src, out, send, recv):
    x, y, c = _coords()
    mine = 2 * x + y
    sibling = (x, y, 1 - c)
    others = [(a, k, 2 * px + py, (px, py)) for a in range(len(src)) for k, (px, py) in enumerate(_other_chips(x, y))]

    def copy(a, k, chip, half, to, source=None):
        dst = out[a].at[chip, half]
        return pltpu.make_async_remote_copy(src_ref=dst if source is None else source, dst_ref=dst,
                                            send_sem=send.at[a, k], recv_sem=recv.at[a, k],
                                            device_id=to, device_id_type=MESH)

    first = [copy(a, k, mine, c, (*chip, c), source=src[a].at[c]) for a, k, _, chip in others]
    own = [pltpu.make_async_remote_copy(src_ref=src[a], dst_ref=out[a].at[mine], send_sem=send.at[a, 6],
                                        recv_sem=recv.at[a, 6], device_id=sibling, device_id_type=MESH)
           for a in range(len(src))]
    passed = [copy(a, 3 + k, slot, c, sibling) for a, k, slot, _ in others]

    def start():
        for cp in first + own:
            cp.start()

    def forward():
        for (a, k, slot, _), cp in zip(others, passed):
            copy(a, k, slot, c, (x, y, c)).wait_recv()
            cp.start()

    def finish():
        for a, k, slot, _ in others:
            copy(a, 3 + k, slot, 1 - c, (x, y, c)).wait_recv()
        for cp in own:
            cp.wait_recv()
        for cp in first + passed + own:
            cp.wait_send()

    return start, forward, finish


def _all_gather(halved, whole):
    nh, nw = len(halved), len(whole)

    def body(*refs):
        src_h, src_w = refs[:nh], refs[nh:nh + nw]
        out_h, out_w = refs[nh + nw:2 * nh + nw], refs[2 * nh + nw:2 * (nh + nw)]
        send_h, recv_h, send_w, recv_w = refs[2 * (nh + nw):]
        x, y, c = _coords()
        mine = 2 * x + y
        chips = _other_chips(x, y)

        def copy_w(a, k, chip, to):
            return pltpu.make_async_remote_copy(src_ref=src_w[a], dst_ref=out_w[a].at[chip],
                                                send_sem=send_w.at[a, k], recv_sem=recv_w.at[a, k],
                                                device_id=to, device_id_type=MESH)

        start, forward, finish = _gather_plan(src_h, out_h, send_h, recv_h)
        small = [copy_w(a, k, mine, (*chip, c)) for a in range(nw) for k, chip in enumerate(chips)]
        small += [copy_w(a, 3, mine, (x, y, 1 - c)) for a in range(nw)]
        start()
        for cp in small:
            cp.start()
        forward()
        finish()
        for a in range(nw):
            for k, (px, py) in enumerate(chips):
                copy_w(a, k, 2 * px + py, (x, y, c)).wait_recv()
            copy_w(a, 3, mine, (x, y, c)).wait_recv()
        for cp in small:
            cp.wait_send()

    out_shape = [jax.ShapeDtypeStruct((4,) + a.shape, a.dtype) for a in list(halved) + list(whole)]
    return pl.pallas_call(
        body, name="gather_weights", in_specs=[ANY] * (nh + nw), out_specs=[ANY] * (nh + nw), out_shape=out_shape,
        scratch_shapes=[pltpu.SemaphoreType.DMA((nh, 7)), pltpu.SemaphoreType.DMA((nh, 7)),
                        pltpu.SemaphoreType.DMA((nw, 4)), pltpu.SemaphoreType.DMA((nw, 4))],
    )(*halved, *whole)


def _exchange(name, srcs, lands, copies, local_copies):
    ns, nl, n, nloc = len(srcs), len(lands), len(copies), len(local_copies)

    def body(*refs):
        src, land = refs[:ns], refs[ns:ns + nl]
        send, recv, local_sem = refs[ns + nl:]
        me = _coords()
        started = []
        for k, (si, s_at, li, l_at, ci) in enumerate(local_copies):
            cp = pltpu.make_async_copy(_at(src[si], s_at(*me)), _at(land[li], l_at(*me)), local_sem.at[k])
            cp.start()
            started.append(cp)
        remote = []
        for k, (si, s_at, li, l_at, peer) in enumerate(copies):
            cp = pltpu.make_async_remote_copy(src_ref=_at(src[si], s_at(*me)), dst_ref=_at(land[li], l_at(*me)),
                                              send_sem=send.at[k], recv_sem=recv.at[k],
                                              device_id=peer(*me), device_id_type=MESH)
            cp.start()
            remote.append(cp)
        for cp in remote:
            cp.wait()
        for cp in started:
            cp.wait()

    return pl.pallas_call(
        body, name=name, in_specs=[ANY] * ns, out_specs=[ANY] * nl, out_shape=list(lands),
        scratch_shapes=[pltpu.SemaphoreType.DMA((n,)), pltpu.SemaphoreType.DMA((n,)),
                        pltpu.SemaphoreType.DMA((max(nloc, 1),))],
    )(*srcs)


def _add_pairs(a, b, core, name):
    _, r, cols = b.shape
    tr = ROW_TILE if r % ROW_TILE == 0 else r

    def body(core_ref, a_ref, b_ref, o_ref, ob_ref):
        s = a_ref[...] + b_ref[...]
        o_ref[...] = s
        ob_ref[...] = s.astype(BF16)

    spec = pl.BlockSpec((None, tr, cols), lambda j, i, core: (j, i, 0))
    return pl.pallas_call(
        body, name=name,
        grid_spec=pltpu.PrefetchScalarGridSpec(
            num_scalar_prefetch=1, grid=(4, r // tr),
            in_specs=[pl.BlockSpec((None, None, tr, cols), lambda j, i, core: (j, core[0], i, 0)), spec],
            out_specs=[spec, spec]),
        out_shape=[jax.ShapeDtypeStruct(b.shape, F32), jax.ShapeDtypeStruct(b.shape, BF16)],
        compiler_params=_params(2),
    )(core, a, b)


def _sum_chips(own, landed, name):
    _, r, cols = landed.shape
    tr = ROW_TILE if r % ROW_TILE == 0 else r

    def body(own_ref, land_ref, o_ref):
        acc = own_ref[...]
        for j in range(3):
            acc = acc + land_ref[j].astype(F32)
        o_ref[...] = acc

    return pl.pallas_call(
        body, name=name, grid=(r // tr,),
        in_specs=[_rows(tr, cols), pl.BlockSpec((3, tr, cols), lambda i: (0, i, 0))], out_specs=_rows(tr, cols),
        out_shape=jax.ShapeDtypeStruct((r, cols), F32), compiler_params=_params(1),
    )(own, landed)


def _sum_devices(landed, name):
    def body(l_ref, o_ref):
        acc = l_ref[0]
        for j in range(1, 8):
            acc = acc + l_ref[j]
        o_ref[...] = acc

    return pl.pallas_call(body, name=name, out_shape=jax.ShapeDtypeStruct(landed.shape[1:], F32))(landed)


CHIP_FLIPS = [(1, 0), (0, 1), (1, 1)]


def _flip(fx, fy, fc):
    return lambda x, y, c: (x ^ fx, y ^ fy, c ^ fc)


def _core_swap_copies(big, landed, send, recv):
    x, y, c = _coords()
    return [pltpu.make_async_remote_copy(src_ref=big[a].at[j, 1 - c], dst_ref=landed[a].at[j],
                                         send_sem=send.at[a, j], recv_sem=recv.at[a, j],
                                         device_id=(x, y, 1 - c), device_id_type=MESH)
            for a in range(len(big)) for j in range(4)]


def _carried_core_swap(big):
    return big, [jax.ShapeDtypeStruct((4,) + g.shape[2:], F32) for g in big], _core_swap_copies


def _carried_chip_sums(sums):
    return sums, [jax.ShapeDtypeStruct((3,) + g.shape[1:], g.dtype) for g in sums], _chip_sum_copies


def _sum_cores(big, small, tag, swapped=()):
    c = lax.axis_index("c")
    rest = big[len(swapped):]
    nb = len(rest)
    copies = [(a, (lambda j: lambda x, y, c: (j, 1 - c))(j), a, (lambda j: lambda x, y, c: (j,))(j), _flip(0, 0, 1))
              for a in range(nb) for j in range(4)]
    lands = [jax.ShapeDtypeStruct((4,) + g.shape[2:], F32) for g in rest]
    srcs, local = list(rest), []
    if small is not None:
        flips = [(fx, fy, fc) for fx in (0, 1) for fy in (0, 1) for fc in (0, 1) if fx or fy or fc]
        copies += [(nb, lambda x, y, c: (), nb, lambda x, y, c: (4 * x + 2 * y + c,), _flip(*f)) for f in flips]
        lands.append(jax.ShapeDtypeStruct((8,) + small.shape, F32))
        local = [(nb, lambda x, y, c: (), nb, lambda x, y, c: (4 * x + 2 * y + c,), None)]
        srcs.append(small)
    landed = _exchange("swap_halves_" + tag, srcs, lands, copies, local)
    small_sum = None if small is None else _sum_devices(landed[nb], "sum_small")
    landed = list(swapped) + list(landed[:nb])
    nb = len(big)
    core = jnp.reshape(c, (1,)).astype(jnp.int32)
    sums = [_add_pairs(big[a], landed[a], core, f"add_cores_{tag}_{a}") for a in range(nb)]
    return [s for s, _ in sums], [sb for _, sb in sums], small_sum


def _chip_sum_copies(sums, landed, send, recv):
    x, y, c = _coords()
    return [pltpu.make_async_remote_copy(src_ref=sums[a].at[2 * (x ^ fx) + (y ^ fy)], dst_ref=landed[a].at[k],
                                         send_sem=send.at[a, k], recv_sem=recv.at[a, k],
                                         device_id=(x ^ fx, y ^ fy, c), device_id_type=MESH)
            for a in range(len(sums)) for k, (fx, fy) in enumerate(CHIP_FLIPS)]


def _send_chip_sums(chip_bf16):
    n = len(chip_bf16)

    def body(*refs):
        copies = _chip_sum_copies(refs[:n], refs[n:2 * n], refs[2 * n], refs[2 * n + 1])
        for cp in copies:
            cp.start()
        for cp in copies:
            cp.wait()

    return pl.pallas_call(
        body, name="send_chip_sums", in_specs=[ANY] * n, out_specs=[ANY] * n,
        out_shape=[jax.ShapeDtypeStruct((3,) + g.shape[1:], BF16) for g in chip_bf16],
        scratch_shapes=[pltpu.SemaphoreType.DMA((n, 3)), pltpu.SemaphoreType.DMA((n, 3))],
    )(*chip_bf16)


def _sum_chips_and_share(chip_f32, landed):
    x, y, _ = _coords()
    totals = [_sum_chips(lax.dynamic_index_in_dim(f, 2 * x + y, axis=0, keepdims=False), l, f"sum_chips_{a}")
              for a, (f, l) in enumerate(zip(chip_f32, landed))]
    copies = [(a, lambda x, y, c: (), a, lambda x, y, c: (), _flip(0, 0, 1)) for a in range(len(totals))]
    lands = [jax.ShapeDtypeStruct(t.shape, F32) for t in totals]
    return list(zip(totals, _exchange("swap_sums", totals, lands, copies, [])))


def kernel(x, conv_norm_g, conv_w_in, conv_w, conv_w_out, attn_norm_g, attn_w_in, attn_b_f, attn_q_norm_g, attn_k_norm_g, attn_w_out, loss_target, m_conv_norm_g, m_conv_w_in, m_conv_w, m_conv_w_out, m_attn_norm_g, m_attn_w_in, m_attn_b_f, m_attn_q_norm_g, m_attn_k_norm_g, m_attn_w_out, v_conv_norm_g, v_conv_w_in, v_conv_w, v_conv_w_out, v_attn_norm_g, v_attn_w_in, v_attn_b_f, v_attn_q_norm_g, v_attn_k_norm_g, v_attn_w_out):
    xi, yi, _ = _coords()
    chip = 2 * xi + yi
    D = x.shape[2]
    H = D // HEAD_DIM
    names = ["conv_norm_g", "conv_w_in", "conv_w", "conv_w_out", "attn_norm_g", "attn_w_in", "attn_b_f",
             "attn_q_norm_g", "attn_k_norm_g", "attn_w_out"]
    weights = dict(zip(names, [conv_norm_g, conv_w_in, conv_w, conv_w_out, attn_norm_g, attn_w_in, attn_b_f,
                               attn_q_norm_g, attn_k_norm_g, attn_w_out]))
    m_in = dict(zip(names, [m_conv_norm_g, m_conv_w_in, m_conv_w, m_conv_w_out, m_attn_norm_g, m_attn_w_in,
                            m_attn_b_f, m_attn_q_norm_g, m_attn_k_norm_g, m_attn_w_out]))
    v_in = dict(zip(names, [v_conv_norm_g, v_conv_w_in, v_conv_w, v_conv_w_out, v_attn_norm_g, v_attn_w_in,
                            v_attn_b_f, v_attn_q_norm_g, v_attn_k_norm_g, v_attn_w_out]))
    weights = {k: w[0] for k, w in weights.items()}
    m_in = {k: w[0] for k, w in m_in.items()}
    v_in = {k: w[0] for k, w in v_in.items()}

    big_names = ["conv_w_in", "attn_w_in", "conv_w_out", "attn_w_out"]
    halved = {k: weights[k].astype(BF16).reshape(2, weights[k].shape[0] // 2, weights[k].shape[1]) for k in big_names}
    q = D // 4
    small_w = jnp.concatenate([weights["conv_w"], weights["attn_norm_g"][None, :], jnp.zeros((4, q), F32)], axis=0)
    g_in, g_out, g_small = _all_gather([halved["conv_w_in"], halved["conv_w_out"]], [small_w])
    w_in = g_in.reshape(4, D, D)
    w_out = g_out.reshape(D, D)
    conv_w_full = g_small[:, 0:3, :].transpose(1, 0, 2).reshape(3, D)
    attn_g_full = g_small[:, 3, :].reshape(1, D)
    g1 = weights["conv_norm_g"][None, :]

    conv_acts, (ga_in, ga_out) = _conv_fwd(x[0], g1, w_in, conv_w_full, w_out, [halved["attn_w_in"], halved["attn_w_out"]])
    def core_sums(by_chip, tag, swapped):
        f32, bf16, _ = _sum_cores([g.reshape(4, 2, g.shape[1] // 2, g.shape[2]) for g in by_chip], None, tag, swapped)
        return f32, bf16

    loss_part, grad_x, grads, (early_f32, early_arrived) = _after_conv(
        conv_acts, x[0], loss_target[0], g1, w_in, conv_w_full, w_out, attn_g_full, ga_in.reshape(4, D, D + H // 4),
        weights["attn_b_f"][None, :], weights["attn_q_norm_g"][None, :], weights["attn_k_norm_g"][None, :],
        ga_out.reshape(D, D), reduce_early=core_sums)

    tail = jnp.concatenate([grads["attn_b_f"], grads["attn_q_norm_g"], grads["attn_k_norm_g"],
                            jnp.reshape(loss_part, (1, 1)), jnp.zeros((1, D - H - 2 * HEAD_DIM - 1), F32)], axis=1)
    small = jnp.concatenate([grads["conv_norm_g"], grads["conv_w"], grads["attn_norm_g"], tail,
                             jnp.zeros((2, D), F32)], axis=0)
    out_f32, out_bf16, small_sum = _sum_cores([grads["conv_w_out"].reshape(4, 2, D // 8, D)], small, "conv_out")
    big_names = ["attn_w_in", "attn_w_out", "conv_w_in", "conv_w_out"]
    reduced = _sum_chips_and_share(early_f32 + out_f32, early_arrived + list(_send_chip_sums(out_bf16)))
    final = {}
    final["conv_norm_g"] = small_sum[0]
    final["conv_w"] = lax.dynamic_slice_in_dim(small_sum[1:4], chip * q, q, axis=1)
    final["attn_norm_g"] = lax.dynamic_slice_in_dim(small_sum[4], chip * q, q, axis=0)
    final["attn_b_f"] = small_sum[5, :H]
    final["attn_q_norm_g"] = small_sum[5, H:H + HEAD_DIM]
    final["attn_k_norm_g"] = small_sum[5, H + HEAD_DIM:H + 2 * HEAD_DIM]
    loss = small_sum[5, H + 2 * HEAD_DIM]

    delta, new_m, new_v = {}, {}, {}
    core = jnp.reshape(lax.axis_index("c"), (1,)).astype(jnp.int32)
    for k, (mine, other) in zip(big_names, reduced):
        final[k], delta[k], new_m[k], new_v[k] = _adamw_halves(weights[k], mine, other, m_in[k], v_in[k], core,
                                                               "adamw_" + k)
    for k in names:
        if k in big_names:
            continue
        shape = weights[k].shape
        as2d = (lambda a: a.reshape(1, -1)) if len(shape) == 1 else (lambda a: a)
        d, m2, v2 = _adamw(as2d(weights[k]), as2d(final[k]), as2d(m_in[k]), as2d(v_in[k]), "adamw_" + k)
        delta[k], new_m[k], new_v[k] = d.reshape(shape), m2.reshape(shape), v2.reshape(shape)
    lead = lambda a: a[None]
    return (loss, grad_x[None], *[lead(final[k]) for k in names], *[lead(delta[k]) for k in names],
            *[lead(new_m[k]) for k in names], *[lead(new_v[k]) for k in names])
```

```python
import functools

import jax
import jax.numpy as jnp
from jax import lax
from jax.experimental import pallas as pl
from jax.experimental.pallas import tpu as pltpu

F32 = jnp.float32
BF16 = jnp.bfloat16
HEAD_DIM = 64
LANES = 128
RMS_EPS = 1e-6
NEG = -1e30
Q_SCALE = 0.125
ROW_TILE = 256
CONV_TILE = 512
ATT_GROUP = 4
SKIP_LOG = 106.0
PLAIN_EXP_MAX = 60.0
TN_ROWS = 2048
VMEM_LIMIT = 56 << 20
ADAM_LR, ADAM_B1, ADAM_B2, ADAM_EPS, ADAM_WD, ADAM_STEP = 0.001, 0.9, 0.999, 1e-08, 0.01, 10
MESH = pl.DeviceIdType.MESH
ANY = pl.BlockSpec(memory_space=pl.ANY)


def _lane():
    return lax.broadcasted_iota(jnp.int32, (1, LANES), 1)


def _split3(x):
    hi = x.astype(BF16).astype(F32)
    r = x - hi
    mid = r.astype(BF16).astype(F32)
    lo = (r - mid).astype(BF16).astype(F32)
    return hi, mid, lo


STAT_STRIDE = 16
ONE_LANE = 3 * STAT_STRIDE


def _pack3(x, lane, one):
    hi, mid, lo = _split3(x)
    packed = hi + pltpu.roll(mid, STAT_STRIDE, 1) + pltpu.roll(lo, 2 * STAT_STRIDE, 1)
    return jnp.where(lane == ONE_LANE, one, packed).astype(BF16)


def _scatter_matrices(H):
    rows = lax.broadcasted_iota(jnp.int32, (LANES, H * LANES), 0)
    cols = lax.broadcasted_iota(jnp.int32, (LANES, H * LANES), 1)
    head, within = cols // LANES, cols % LANES
    extra = within - _aug(head % 2)
    term = (rows < ONE_LANE) & (rows % STAT_STRIDE == head)
    first = ((term & (extra == rows // STAT_STRIDE)) | ((rows == ONE_LANE) & (extra >= 3) & (extra < 6)))
    second = ((term & (extra - 3 == rows // STAT_STRIDE)) | ((rows == ONE_LANE) & (extra >= 0) & (extra < 3)))
    return first.astype(BF16), second.astype(BF16)


def _col(x, lane, idx):
    return jnp.sum(jnp.where(lane == idx, x, 0.0), axis=1, keepdims=True)


def _feat(parity):
    return HEAD_DIM * parity


def _aug(parity):
    return HEAD_DIM * (1 - parity)


def _own(lane, parity):
    return (lane >= _feat(parity)) & (lane < _feat(parity) + HEAD_DIM)


def _head_tile(ref, hd, lane):
    j = hd // 2
    return jnp.where(_own(lane, hd % 2), ref[:, LANES * j:LANES * (j + 1)], 0.0)


def _pair_tile(even, odd, lane):
    return jnp.where(lane < HEAD_DIM, even, odd)


def _sigmoid(x):
    return 0.5 * jnp.tanh(0.5 * x) + 0.5


def _dot(a, b):
    return jnp.dot(a, b, preferred_element_type=F32)


def _dot_nt(a, b):
    return lax.dot_general(a, b, (((1,), (1,)), ((), ())), preferred_element_type=F32)


def _dot_tn(a, b):
    return lax.dot_general(a, b, (((0,), (0,)), ((), ())), preferred_element_type=F32)


def _dot01(tri, x):
    hi, mid, lo = _split3(x)
    return _dot(tri, hi.astype(BF16)) + _dot(tri, mid.astype(BF16)) + _dot(tri, lo.astype(BF16))


def _rms_bwd(dh, x, g):
    inv = lax.rsqrt(jnp.mean(x * x, axis=-1, keepdims=True) + RMS_EPS)
    xh = x * inv
    dxn = dh * g
    dx = inv * (dxn - xh * jnp.mean(dxn * xh, axis=-1, keepdims=True))
    return dx, jnp.sum(dh * xh, axis=0, keepdims=True)


def _head_rms_bwd(dn, t, g, ones):
    sq = t * t
    hi = sq.astype(BF16)
    lo = (sq - hi.astype(F32)).astype(BF16)
    inv = lax.rsqrt((_dot(hi, ones) + _dot(lo, ones)) * (1.0 / HEAD_DIM) + RMS_EPS)
    th = t * inv
    gd = dn * g
    d = inv * (gd - th * (jnp.sum(gd * th, axis=1, keepdims=True) * (1.0 / HEAD_DIM)))
    return d, jnp.sum(dn * th, axis=0, keepdims=True)


def _params(n_grid):
    return pltpu.CompilerParams(dimension_semantics=("arbitrary",) * n_grid, vmem_limit_bytes=VMEM_LIMIT)


def _rows(tm, cols, rev=None):
    if rev is None:
        return pl.BlockSpec((tm, cols), lambda i: (i, 0))
    return pl.BlockSpec((tm, cols), lambda i: (rev - i, 0))


def _whole(shape, buffers=None):
    mode = {} if buffers is None else dict(pipeline_mode=pl.Buffered(buffers))
    return pl.BlockSpec(shape, lambda *_: (0,) * len(shape), **mode)


def _conv_fwd(x, g1, w_in, conv_w, w_out, later):
    S, D = x.shape
    tm = min(CONV_TILE, S)
    sub = min(ROW_TILE, tm)
    steps = S // tm
    n = len(later)

    def body(x_ref, g_ref, win_ref, cw_ref, wout_ref, *rest):
        shard_refs, rest = rest[:n], rest[n:]
        proj_ref, h_ref, yc_ref, y_ref, x1_ref = rest[:5]
        gathered_refs, rest = rest[5:5 + n], rest[5 + n:]
        prev_u = rest[0]
        if n:
            start, forward, finish = _gather_plan(shard_refs, gathered_refs, rest[1], rest[2])
            pl.when(pl.program_id(0) == 0)(start)
            pl.when(pl.program_id(0) == steps // 2)(forward)

        @pl.when(pl.program_id(0) == 0)
        def _():
            prev_u[...] = jnp.zeros((sub, D), F32)

        for r in range(0, tm, sub):
            rows = slice(r, r + sub)
            xv = x_ref[rows, :]
            inv = lax.rsqrt(jnp.mean(xv * xv, axis=-1, keepdims=True) + RMS_EPS)
            h = (xv * inv * g_ref[...]).astype(BF16)
            h_ref[rows, :] = h
            for j in range(4):
                proj_ref[rows, j * D:(j + 1) * D] = _dot(h, win_ref[j])
            u = proj_ref[rows, D:2 * D] * proj_ref[rows, 2 * D:3 * D]
            pu = prev_u[...]
            row = lax.broadcasted_iota(jnp.int32, (sub, 1), 0)
            u1 = jnp.where(row < 1, pltpu.roll(pu, 1, 0), pltpu.roll(u, 1, 0))
            u2 = jnp.where(row < 2, pltpu.roll(pu, 2, 0), pltpu.roll(u, 2, 0))
            prev_u[...] = u
            w = cw_ref[...]
            yc = w[2:3] * u + w[1:2] * u1 + w[0:1] * u2
            yc_ref[rows, :] = yc
            z = proj_ref[rows, 3 * D:4 * D]
            y = (proj_ref[rows, 0:D] * yc * (z * _sigmoid(z))).astype(BF16)
            y_ref[rows, :] = y
            x1_ref[rows, :] = xv + _dot(y, wout_ref[...])

        if n:
            pl.when(pl.program_id(0) == steps - 1)(finish)

    results = pl.pallas_call(
        body, name="conv_fwd", grid=(steps,),
        in_specs=[_rows(tm, D), _whole((1, D)), _whole((4, D, D), 1), _whole((3, D)), _whole((D, D), 1)] + [ANY] * n,
        out_specs=[_rows(tm, 4 * D), _rows(tm, D), _rows(tm, D), _rows(tm, D), _rows(tm, D)] + [ANY] * n,
        out_shape=[jax.ShapeDtypeStruct((S, 4 * D), F32), jax.ShapeDtypeStruct((S, D), BF16),
                   jax.ShapeDtypeStruct((S, D), F32), jax.ShapeDtypeStruct((S, D), BF16),
                   jax.ShapeDtypeStruct((S, D), F32)] + [jax.ShapeDtypeStruct((4,) + a.shape, a.dtype) for a in later],
        scratch_shapes=[pltpu.VMEM((sub, D), F32)] + [pltpu.SemaphoreType.DMA((n, 7))] * (2 if n else 0),
        compiler_params=_params(1),
    )(x, g1, w_in, conv_w, w_out, *later)
    return results[:5], results[5:]


def _conv_bwd(dproj2, wa, wf, x1, g2, dx2, x, g1, w_in, w_out, conv_w, proj, yc):
    S, D = x.shape
    tm = min(ROW_TILE, S)
    sub = min(ROW_TILE, tm)
    last = S // tm - 1

    def body(dp2_ref, wa_ref, wf_ref, x1_ref, g2_ref, dx2_ref, x_ref, g_ref, win_ref, wout_ref, cw_ref, proj_ref, yc_ref,
             dproj_ref, dx_ref, dx1b_ref, dg_ref, dcw_ref, dg2_ref, next_d):
        @pl.when(pl.program_id(0) == 0)
        def _():
            dg_ref[...] = jnp.zeros((1, D), F32)
            dcw_ref[...] = jnp.zeros((3, D), F32)
            dg2_ref[...] = jnp.zeros((1, D), F32)
            next_d[...] = jnp.zeros((sub, D), F32)

        for r in range(tm - sub, -1, -sub):
            rows = slice(r, r + sub)
            dh2 = _dot_nt(dp2_ref[rows, 0:4 * D], wa_ref[...]) + _dot_nt(dp2_ref[rows, 4 * D:4 * D + LANES], wf_ref[...])
            dxn2, dg2 = _rms_bwd(dh2, x1_ref[rows, :], g2_ref[...])
            dg2_ref[...] += dg2
            dx1v = dx2_ref[rows, :] + dxn2
            dx1b = dx1v.astype(BF16)
            dx1b_ref[rows, :] = dx1b
            dy = _dot_nt(dx1b, wout_ref[...])
            b = proj_ref[rows, 0:D]
            c = proj_ref[rows, D:2 * D]
            xin = proj_ref[rows, 2 * D:3 * D]
            z = proj_ref[rows, 3 * D:4 * D]
            sg = _sigmoid(z)
            sz = z * sg
            ycv = yc_ref[rows, :]
            d0 = dy * b * sz
            dproj_ref[rows, 0:D] = (dy * ycv * sz).astype(BF16)
            dproj_ref[rows, 3 * D:4 * D] = (dy * b * ycv * (sg * (1.0 + z * (1.0 - sg)))).astype(BF16)
            nd = next_d[...]
            row = lax.broadcasted_iota(jnp.int32, (sub, 1), 0)
            d1 = jnp.where(row >= sub - 1, pltpu.roll(nd, sub - 1, 0), pltpu.roll(d0, sub - 1, 0))
            d2 = jnp.where(row >= sub - 2, pltpu.roll(nd, sub - 2, 0), pltpu.roll(d0, sub - 2, 0))
            next_d[...] = d0
            w = cw_ref[...]
            du = w[2:3] * d0 + w[1:2] * d1 + w[0:1] * d2
            u = c * xin
            dcw_ref[2:3, :] += jnp.sum(d0 * u, axis=0, keepdims=True)
            dcw_ref[1:2, :] += jnp.sum(d1 * u, axis=0, keepdims=True)
            dcw_ref[0:1, :] += jnp.sum(d2 * u, axis=0, keepdims=True)
            dproj_ref[rows, D:2 * D] = (du * xin).astype(BF16)
            dproj_ref[rows, 2 * D:3 * D] = (du * c).astype(BF16)
            dh = _dot_nt(dproj_ref[rows, 0:D], win_ref[0])
            for j in range(1, 4):
                dh = dh + _dot_nt(dproj_ref[rows, j * D:(j + 1) * D], win_ref[j])
            dxn, dg = _rms_bwd(dh, x_ref[rows, :], g_ref[...])
            dx_ref[rows, :] = dx1v + dxn
            dg_ref[...] += dg

    return pl.pallas_call(
        body, name="conv_bwd", grid=(S // tm,),
        in_specs=[_rows(tm, 4 * D + LANES, last), _whole((D, 4 * D), 1), _whole((D, LANES), 1), _rows(tm, D, last),
                  _whole((1, D)), _rows(tm, D, last),
                  _rows(tm, D, last), _whole((1, D)), _whole((4, D, D), 1), _whole((D, D), 1),
                  _whole((3, D)), _rows(tm, 4 * D, last), _rows(tm, D, last)],
        out_specs=[_rows(tm, 4 * D, last), _rows(tm, D, last), _rows(tm, D, last), _whole((1, D)), _whole((3, D)),
                   _whole((1, D))],
        out_shape=[jax.ShapeDtypeStruct((S, 4 * D), BF16), jax.ShapeDtypeStruct((S, D), F32),
                   jax.ShapeDtypeStruct((S, D), BF16), jax.ShapeDtypeStruct((1, D), F32),
                   jax.ShapeDtypeStruct((3, D), F32), jax.ShapeDtypeStruct((1, D), F32)],
        scratch_shapes=[pltpu.VMEM((sub, D), F32)],
        compiler_params=_params(1),
    )(dproj2, wa, wf, x1, g2, dx2, x, g1, w_in, w_out, conv_w, proj, yc)


def _attn_front(x1, g2, w, wf, bf, gq, gk):
    S, D = x1.shape
    H = D // HEAD_DIM
    tm = min(ROW_TILE, S)
    tri = (lax.broadcasted_iota(jnp.int32, (tm, tm), 1) <= lax.broadcasted_iota(jnp.int32, (tm, tm), 0)).astype(BF16)

    def body(x_ref, g_ref, w_ref, wf_ref, bf_ref, gq_ref, gk_ref, tri_ref, first_ref, second_ref,
             h_ref, qh_ref, kh_ref, z_ref, f_ref, ends_ref, rel_ref, qa_ref, ka_ref, va_ref, vt_ref,
             carry, v_s, qraw_ref, kraw_ref):
        @pl.when(pl.program_id(0) == 0)
        def _():
            carry[...] = jnp.zeros((8, LANES), F32)

        xv = x_ref[...]
        inv = lax.rsqrt(jnp.mean(xv * xv, axis=-1, keepdims=True) + RMS_EPS)
        h = (xv * inv * g_ref[...]).astype(BF16)
        h_ref[...] = h
        qraw_ref[...] = _dot(h, w_ref[:, 0:D])
        kraw_ref[...] = _dot(h, w_ref[:, D:2 * D])
        v_s[...] = _dot(h, w_ref[:, 2 * D:3 * D])
        z_ref[...] = _dot(h, w_ref[:, 3 * D:4 * D])
        lane = _lane()
        f = _dot(h, wf_ref[...]) + bf_ref[...]
        f_ref[...] = f
        logf = jnp.where(lane < H, jnp.minimum(f, 0.0) - jnp.log(1.0 + jnp.exp(-jnp.abs(f))), 0.0)
        cs = _dot01(tri_ref[...], logf) + carry[0:1, :]
        carry[...] = jnp.broadcast_to(cs[tm - 1:tm, :], (8, LANES))
        diags = jnp.zeros((tm, LANES), F32)
        for hd in range(H):
            sl = slice(LANES * hd, LANES * (hd + 1))
            a = _aug(hd % 2)
            if hd % 2 == 0:
                qh_ref[hd // 2] = qraw_ref[:, LANES * (hd // 2):LANES * (hd // 2 + 1)]
                kh_ref[hd // 2] = kraw_ref[:, LANES * (hd // 2):LANES * (hd // 2 + 1)]
            qt = _head_tile(qraw_ref, hd, lane)
            qn = qt * lax.rsqrt(jnp.sum(qt * qt, axis=1, keepdims=True) * (1.0 / HEAD_DIM) + RMS_EPS) * gq_ref[...]
            kt = _head_tile(kraw_ref, hd, lane)
            kn = kt * lax.rsqrt(jnp.sum(kt * kt, axis=1, keepdims=True) * (1.0 / HEAD_DIM) + RMS_EPS) * gk_ref[...]
            diags = diags + jnp.where(lane == hd, jnp.sum(qn * kn, axis=1, keepdims=True) * Q_SCALE, 0.0)
            qa_ref[:, sl] = (qn * Q_SCALE).astype(BF16)
            ka_ref[:, sl] = kn.astype(BF16)
            va = jnp.where((lane >= a) & (lane < a + 3), 1.0, _head_tile(v_s, hd, lane))
            va_ref[:, sl] = va.astype(BF16)
            vt_ref[hd] = va.T.astype(BF16)
        rel = cs - diags
        rel_ref[...] = rel
        row = lax.broadcasted_iota(jnp.int32, (8, LANES), 0)
        ends_ref[...] = jnp.where(row == 0, cs[0:1, :],
                                  jnp.where(row == 1, cs[tm - 1:tm, :], jnp.min(diags, axis=0, keepdims=True)))
        qa_ref[...] += _dot(_pack3(rel, lane, 1.0), first_ref[...]).astype(BF16)
        ka_ref[...] += _dot(_pack3(-cs, lane, 1.0), second_ref[...]).astype(BF16)

    nb = S // tm
    heads = pl.BlockSpec((H // 2, tm, LANES), lambda i: (0, i, 0))
    return pl.pallas_call(
        body, name="attn_front", grid=(nb,),
        in_specs=[_rows(tm, D), _whole((1, D)), _whole((D, 4 * D)), _whole((D, LANES)), _whole((1, LANES)),
                  _whole((1, LANES)), _whole((1, LANES)), _whole((tm, tm)), _whole((LANES, H * LANES)),
                  _whole((LANES, H * LANES))],
        out_specs=[_rows(tm, D), heads, heads, _rows(tm, D), _rows(tm, LANES),
                   pl.BlockSpec((None, 8, LANES), lambda i: (i, 0, 0)), _rows(tm, LANES),
                   _rows(tm, H * LANES), _rows(tm, H * LANES), _rows(tm, H * LANES),
                   pl.BlockSpec((H, None, LANES, tm), lambda i: (0, i, 0, 0))],
        out_shape=[jax.ShapeDtypeStruct((S, D), BF16), jax.ShapeDtypeStruct((H // 2, S, LANES), F32),
                   jax.ShapeDtypeStruct((H // 2, S, LANES), F32), jax.ShapeDtypeStruct((S, D), F32),
                   jax.ShapeDtypeStruct((S, LANES), F32), jax.ShapeDtypeStruct((nb, 8, LANES), F32),
                   jax.ShapeDtypeStruct((S, LANES), F32),
                   jax.ShapeDtypeStruct((S, H * LANES), BF16), jax.ShapeDtypeStruct((S, H * LANES), BF16),
                   jax.ShapeDtypeStruct((S, H * LANES), BF16), jax.ShapeDtypeStruct((H, nb, LANES, tm), BF16)],
        scratch_shapes=[pltpu.VMEM((8, LANES), F32), pltpu.VMEM((tm, D), F32), pltpu.VMEM((tm, D), F32),
                        pltpu.VMEM((tm, D), F32)],
        compiler_params=_params(1),
    )(x1, g2, w, wf, bf, gq, gk, tri, *_scatter_matrices(H))


def _skip_tables(first, last, lowest, gq, gk, G):
    nb = first.shape[0]
    bound = HEAD_DIM ** 0.5 * jnp.max(jnp.abs(gq)) * jnp.max(jnp.abs(gk))
    lowest = jnp.maximum(lowest, -bound)
    idx = jnp.arange(nb)
    margin = (SKIP_LOG + bound) - lowest
    need = (last[None, :, :] <= first[:, None, :] + margin[:, None, :]) & (idx[None, :, None] < idx[:, None, None])
    need = need | (idx[None, :, None] == idx[:, None, None])
    kstart = jnp.argmax(need, axis=1)
    qend = nb - 1 - jnp.argmax(need[::-1], axis=0)
    kstart = jnp.min(kstart.reshape(2 * nb // G, G // 2, -1), axis=1)
    kstart = kstart - (kstart & 1)
    qend = jnp.max(qend.reshape(2 * nb // G, G // 2, -1), axis=1)
    return kstart.T.astype(jnp.int32), qend.T.astype(jnp.int32), bound


def _attn_fwd(kstart, qa, ka, vt, online_max):
    S = qa.shape[0]
    H = qa.shape[1] // LANES
    nb, T = vt.shape[1], vt.shape[3]
    G = 2 * nb // kstart.shape[1]
    W = G * T

    def finish(acc, shift, o_ref, lse_ref):
        a = _aug(pl.program_id(0) % 2)
        feat = lax.broadcasted_iota(jnp.int32, (LANES, 1), 0)
        l = jnp.sum(jnp.where(feat == a, acc, 0.0), axis=0, keepdims=True)
        o_ref[...] = (acc * (1.0 / l)).T
        lse_ref[...] = shift + jnp.log(l)

    def causal(st):
        return jnp.where(lax.broadcasted_iota(jnp.int32, st.shape, 0) <= lax.broadcasted_iota(jnp.int32, st.shape, 1),
                         st, NEG)

    def fast_body(ks_ref, q_ref, k_ref, vt_ref, o_ref, lse_ref, acc_ref, sa_ref, sb_ref, sc_ref):
        h, g = pl.program_id(0), pl.program_id(1)
        q = q_ref[...]
        acc_ref[...] = jnp.zeros((LANES, W), F32)

        def scores(ki, lo):
            return _dot_nt(k_ref[pl.ds(pl.multiple_of(ki * T, T), 2 * T), :], q[lo * T:, :])

        def weighted(ki, p):
            return _dot(vt_ref[ki], p[:T]) + _dot(vt_ref[ki + 1], p[T:])

        first = ks_ref[h, 2 * g + 1]
        early = jnp.minimum(ks_ref[h, 2 * g], first)

        def narrow(i, carry):
            ki = early + 2 * i
            st = _dot_nt(k_ref[pl.ds(pl.multiple_of(ki * T, T), 2 * T), :], q[:W // 2, :])
            acc_ref[:, :W // 2] += weighted(ki, jnp.exp(st).astype(BF16))
            return carry

        lax.fori_loop(0, (first - early) // 2, narrow, 0)
        steps = (g * G - first) // 2
        sa_ref[...] = scores(first, 0)

        def advance(ki, cur_ref, next_ref):
            p = jnp.exp(cur_ref[...]).astype(BF16)
            next_ref[...] = scores(ki + 2, 0)
            acc_ref[...] += weighted(ki, p)

        def loop(i, carry):
            advance(first + 4 * i, sa_ref, sb_ref)
            advance(first + 4 * i + 2, sb_ref, sa_ref)
            return carry

        lax.fori_loop(0, steps // 2, loop, 0)

        def first_own(pending_ref):
            p = jnp.exp(causal(pending_ref[...])).astype(BF16)
            if G > 2:
                sc_ref[:, :W - 2 * T] = scores(g * G + 2, 2)
            acc_ref[...] += weighted(g * G, p)

        @pl.when(steps % 2 == 1)
        def _():
            advance(g * G - 2, sa_ref, sb_ref)
            first_own(sb_ref)

        @pl.when(steps % 2 == 0)
        def _():
            first_own(sa_ref)

        if G > 2:
            acc_ref[:, 2 * T:] += weighted(g * G + 2, jnp.exp(causal(sc_ref[:, :W - 2 * T])).astype(BF16))
        for j in range(4, G, 2):
            p = jnp.exp(causal(scores(g * G + j, j))).astype(BF16)
            acc_ref[:, j * T:] += weighted(g * G + j, p)
        finish(acc_ref[...], 0.0, o_ref, lse_ref)

    def online_body(ks_ref, q_ref, k_ref, vt_ref, o_ref, lse_ref, acc_ref, m_ref):
        h, g = pl.program_id(0), pl.program_id(1)
        q = q_ref[...]
        m_ref[...] = jnp.full((8, W), NEG, F32)
        acc_ref[...] = jnp.zeros((LANES, W), F32)

        def update(st, vtb, lo):
            m_old = m_ref[0:1, lo:]
            m_new = jnp.maximum(m_old, jnp.max(st, axis=0, keepdims=True))
            p = jnp.exp(st - m_new).astype(BF16)
            acc_ref[:, lo:] = acc_ref[:, lo:] * jnp.exp(m_old - m_new) + _dot(vtb, p)
            m_ref[:, lo:] = jnp.broadcast_to(m_new, (8, W - lo))

        def loop(ki, carry):
            kb = k_ref[pl.ds(pl.multiple_of(ki * T, T), T), :]
            update(_dot_nt(kb, q), vt_ref[ki], 0)
            return carry

        lax.fori_loop(jnp.minimum(ks_ref[h, 2 * g], ks_ref[h, 2 * g + 1]), g * G, loop, 0)
        for j in range(G):
            ki = g * G + j
            kb = k_ref[pl.ds(pl.multiple_of(ki * T, T), T), :]
            update(causal(_dot_nt(kb, q[j * T:, :])), vt_ref[ki], j * T)
        finish(acc_ref[...], m_ref[0:1, :], o_ref, lse_ref)

    return pl.pallas_call(
        online_body if online_max else fast_body, name="attn_fwd_online" if online_max else "attn_fwd",
        grid_spec=pltpu.PrefetchScalarGridSpec(
            num_scalar_prefetch=1, grid=(H, nb // G),
            in_specs=[pl.BlockSpec((W, LANES), lambda h, i, ks: (i, h)),
                      pl.BlockSpec((S, LANES), lambda h, i, ks: (0, h)),
                      pl.BlockSpec((None, nb, LANES, T), lambda h, i, ks: (h, 0, 0, 0))],
            out_specs=[pl.BlockSpec((W, LANES), lambda h, i, ks: (i, h)),
                       pl.BlockSpec((None, 1, W), lambda h, i, ks: (h, 0, i))],
            scratch_shapes=[pltpu.VMEM((LANES, W), F32)] + (
                [pltpu.VMEM((8, W), F32)] if online_max else [pltpu.VMEM((2 * T, W), F32)] * 3)),
        out_shape=[jax.ShapeDtypeStruct((S, H * LANES), F32), jax.ShapeDtypeStruct((H, 1, S), F32)],
        compiler_params=_params(2),
    )(kstart, qa, ka, vt)


def _attn_out(o_aug, lse, rel, z, x1, target, w_out, qa):
    S, D = x1.shape
    H = D // HEAD_DIM
    tm = min(ROW_TILE, S)

    def body(o_ref, z_ref, x1_ref, t_ref, w_ref, q_ref, first_ref, rel_ref, lse_ref,
             dx2_ref, dx2b_ref, o2b_ref, dz_ref, doa_ref, qa2_ref, loss_ref, oc_s, do_s):
        @pl.when(pl.program_id(0) == 0)
        def _():
            loss_ref[...] = jnp.zeros((1, LANES), F32)

        lane = _lane()
        for j in range(H // 2):
            oc_s[:, LANES * j:LANES * (j + 1)] = _pair_tile(o_ref[:, 2 * LANES * j:2 * LANES * j + LANES],
                                                            o_ref[:, 2 * LANES * j + LANES:2 * LANES * (j + 1)], lane)
        oc = oc_s[...]
        zv = z_ref[...]
        sg = _sigmoid(zv)
        sz = zv * sg
        o2 = (oc * sz).astype(BF16)
        o2b_ref[...] = o2
        e = x1_ref[...] + _dot(o2, w_ref[...]) - t_ref[...]
        sq = jnp.sum(jnp.sum(e * e, axis=1, keepdims=True), axis=0, keepdims=True)
        loss_ref[...] += jnp.broadcast_to(sq * (0.5 / D), (1, LANES))
        dx2 = e * (1.0 / D)
        dx2_ref[...] = dx2
        dx2b = dx2.astype(BF16)
        dx2b_ref[...] = dx2b
        do2 = _dot_nt(dx2b, w_ref[...])
        dz_ref[...] = (do2 * oc * (sg * (1.0 + zv * (1.0 - sg)))).astype(BF16)
        do_s[...] = do2 * sz
        deltas = jnp.zeros((tm, LANES), F32)
        for hd in range(H):
            dt = _head_tile(do_s, hd, lane)
            delta = jnp.sum(dt * _head_tile(oc_s, hd, lane), axis=1, keepdims=True)
            deltas = deltas + jnp.where(lane == hd, delta, 0.0)
            doa_ref[:, LANES * hd:LANES * (hd + 1)] = dt.astype(BF16)
        doa_ref[...] += _dot(_pack3(-deltas, lane, 0.0), first_ref[...]).astype(BF16)
        lse = jnp.concatenate([lse_ref[...], jnp.zeros((LANES - H, tm), F32)], axis=0).T
        rq = rel_ref[...] - lse
        tile_lane = lax.broadcasted_iota(jnp.int32, (1, H * LANES), 1)
        extra = tile_lane % LANES - _aug((tile_lane // LANES) % 2)
        kept = jnp.where((extra >= 0) & (extra < 3), jnp.zeros((), BF16), q_ref[...])
        qa2_ref[...] = kept + _dot(_pack3(rq, lane, 0.0), first_ref[...]).astype(BF16)

    return pl.pallas_call(
        body, name="attn_out", grid=(S // tm,),
        in_specs=[_rows(tm, H * LANES), _rows(tm, D), _rows(tm, D), _rows(tm, D), _whole((D, D)),
                  _rows(tm, H * LANES), _whole((LANES, H * LANES)), _rows(tm, LANES),
                  pl.BlockSpec((H, tm), lambda i: (0, i))],
        out_specs=[_rows(tm, D), _rows(tm, D), _rows(tm, D), _rows(tm, D), _rows(tm, H * LANES),
                   _rows(tm, H * LANES), _whole((1, LANES))],
        out_shape=[jax.ShapeDtypeStruct((S, D), F32), jax.ShapeDtypeStruct((S, D), BF16),
                   jax.ShapeDtypeStruct((S, D), BF16), jax.ShapeDtypeStruct((S, D), BF16),
                   jax.ShapeDtypeStruct((S, H * LANES), BF16), jax.ShapeDtypeStruct((S, H * LANES), BF16),
                   jax.ShapeDtypeStruct((1, LANES), F32)],
        scratch_shapes=[pltpu.VMEM((tm, D), F32), pltpu.VMEM((tm, D), F32)],
        compiler_params=_params(1),
    )(o_aug, z, x1, target, w_out, qa, _scatter_matrices(H)[0], rel, lse)


def _attn_bwd(qend, qa2, doa, ka, va, T):
    S = qa2.shape[0]
    H = qa2.shape[1] // LANES
    nb = S // T
    G = 2 * nb // qend.shape[1]
    W = G * T

    def body(qe_ref, q_ref, do_ref, k_ref, v_ref, dq_ref, dk_ref, dv_ref, dkt_acc, dvt_acc):
        h, g = pl.program_id(0), pl.program_id(1)

        @pl.when(g == 0)
        def _():
            dq_ref[...] = jnp.zeros((S, LANES), F32)

        kb = k_ref[...]
        vb = v_ref[...]
        dkt_acc[...] = jnp.zeros((LANES, W), F32)
        dvt_acc[...] = jnp.zeros((LANES, W), F32)

        def step(qi, c0, c1, masked):
            rows = pl.ds(pl.multiple_of(qi * T, T), 2 * T)
            qb = q_ref[rows, :]
            dob = do_ref[rows, :]
            s = _dot_nt(qb, kb[c0:c1])
            if masked:
                query = lax.broadcasted_iota(jnp.int32, s.shape, 0) + (c1 - 2 * T)
                s = jnp.where(lax.broadcasted_iota(jnp.int32, s.shape, 1) <= query, s, NEG)
            p = jnp.exp(s)
            ds = (p * _dot_nt(dob, vb[c0:c1])).astype(BF16)
            dvt_acc[:, c0:c1] += _dot(dob.astype(F32).T.astype(BF16), p.astype(BF16))
            dkt_acc[:, c0:c1] += _dot(qb.astype(F32).T.astype(BF16), ds)
            dq_ref[rows, :] += _dot(ds, kb[c0:c1])

        for m in range(G // 2):
            step(g * G + 2 * m, 0, (m + 1) * 2 * T, True)
        first = g * G + G
        n_all = jnp.maximum((qe_ref[h, 2 * g] - first + 2) // 2, 0)
        second = first + 2 * n_all

        def all_keys(i, carry):
            step(first + 2 * i, 0, W, False)
            return carry

        def late_keys(i, carry):
            step(second + 2 * i, W // 2, W, False)
            return carry

        lax.fori_loop(0, n_all, all_keys, 0)
        lax.fori_loop(0, (qe_ref[h, 2 * g + 1] - second + 2) // 2, late_keys, 0)
        dk_ref[...] = dkt_acc[...].T
        dv_ref[...] = dvt_acc[...].T.astype(BF16)

    heads = pl.BlockSpec((None, W, LANES), lambda h, i, qe: (h, i, 0))
    return pl.pallas_call(
        body, name="attn_bwd",
        grid_spec=pltpu.PrefetchScalarGridSpec(
            num_scalar_prefetch=1, grid=(H, nb // G),
            in_specs=[pl.BlockSpec((S, LANES), lambda h, i, qe: (0, h)), pl.BlockSpec((S, LANES), lambda h, i, qe: (0, h)),
                      pl.BlockSpec((W, LANES), lambda h, i, qe: (i, h)), pl.BlockSpec((W, LANES), lambda h, i, qe: (i, h))],
            out_specs=[pl.BlockSpec((None, S, LANES), lambda h, i, qe: (h, 0, 0)), heads, heads],
            scratch_shapes=[pltpu.VMEM((LANES, W), F32), pltpu.VMEM((LANES, W), F32)]),
        out_shape=[jax.ShapeDtypeStruct((H, S, LANES), F32), jax.ShapeDtypeStruct((H, S, LANES), F32),
                   jax.ShapeDtypeStruct((H, S, LANES), BF16)],
        compiler_params=_params(2),
    )(qend, qa2, doa, ka, va)


def _attn_proj_bwd(dqt, dka, dva, qraw, kraw, dz, f, gq, gk):
    S, D = dz.shape
    H = D // HEAD_DIM
    tm = min(CONV_TILE, S)
    last = S // tm - 1
    tri = (lax.broadcasted_iota(jnp.int32, (tm, tm), 1) >= lax.broadcasted_iota(jnp.int32, (tm, tm), 0)).astype(BF16)

    def body(dq_ref, dk_ref, dv_ref, q_ref, k_ref, dz_ref, f_ref, gq_ref, gk_ref, tri_ref, ones_ref,
             dproj_ref, small_ref, carry, pairs):
        @pl.when(pl.program_id(0) == 0)
        def _():
            small_ref[...] = jnp.zeros((8, LANES), F32)
            carry[...] = jnp.zeros((8, LANES), F32)

        lane = _lane()

        def head_pair(j, acc):
            dcs, dgq, dgk = acc
            dq2, dk2 = [], []
            q_pair, k_pair = q_ref[j], k_ref[j]
            for parity in (0, 1):
                hd = 2 * j + parity
                own, a = _own(lane, parity), _aug(parity)
                dqf = dq_ref[hd]
                dqn = jnp.where(own, dqf * Q_SCALE, 0.0)
                d, dg = _head_rms_bwd(dqn, jnp.where(own, q_pair, 0.0), gq_ref[...], ones_ref[...])
                dq2.append(d)
                dgq = dgq + dg
                dkt = dk_ref[hd]
                dcs = dcs + jnp.where(lane == hd, _col(dqf, lane, a) - _col(dkt, lane, a + 3), 0.0)
                d, dg = _head_rms_bwd(jnp.where(own, dkt, 0.0), jnp.where(own, k_pair, 0.0), gk_ref[...], ones_ref[...])
                dk2.append(d)
                dgk = dgk + dg
            pairs[0, j] = _pair_tile(*dq2, lane).astype(BF16)
            pairs[1, j] = _pair_tile(*dk2, lane).astype(BF16)
            pairs[2, j] = _pair_tile(dv_ref[2 * j], dv_ref[2 * j + 1], lane)
            return dcs, dgq, dgk

        zero = jnp.zeros((1, LANES), F32)
        dcs, dgq, dgk = lax.fori_loop(0, H // 2, head_pair, (jnp.zeros((tm, LANES), F32), zero, zero))
        for part in range(3):
            for j in range(H // 2):
                dproj_ref[:, part * D + LANES * j:part * D + LANES * (j + 1)] = pairs[part, j]
        dproj_ref[:, 3 * D:4 * D] = dz_ref[...]
        dlogf = _dot01(tri_ref[...], dcs) + carry[0:1, :]
        carry[...] = jnp.broadcast_to(dlogf[0:1, :], (8, LANES))
        df = dlogf * (1.0 / (1.0 + jnp.exp(f_ref[...])))
        dproj_ref[:, 4 * D:4 * D + LANES] = df.astype(BF16)
        small_ref[0:1, :] += jnp.sum(df, axis=0, keepdims=True)
        small_ref[1:2, :] += dgq
        small_ref[2:3, :] += dgk

    W = 4 * D + LANES
    heads = pl.BlockSpec((H, tm, LANES), lambda i: (0, last - i, 0))
    head_pairs = pl.BlockSpec((H // 2, tm, LANES), lambda i: (0, last - i, 0))
    return pl.pallas_call(
        body, name="attn_proj_bwd", grid=(S // tm,),
        in_specs=[heads, heads, heads, head_pairs, head_pairs,
                  _rows(tm, D, last), _rows(tm, LANES, last), _whole((1, LANES)), _whole((1, LANES)),
                  _whole((tm, tm)), _whole((LANES, LANES))],
        out_specs=[_rows(tm, W, last), _whole((8, LANES))],
        out_shape=[jax.ShapeDtypeStruct((S, W), BF16), jax.ShapeDtypeStruct((8, LANES), F32)],
        scratch_shapes=[pltpu.VMEM((8, LANES), F32), pltpu.VMEM((3, H // 2, tm, LANES), BF16)],
        compiler_params=_params(1),
    )(dqt, dka, dva, qraw, kraw, dz, f, gq, gk, tri, jnp.ones((LANES, LANES), BF16))


def _matmul_tn(a, b, col0, n, tn, name, stacked=False, carried=None):
    S, M = a.shape
    ts = min(TN_ROWS, S)
    off = col0 // tn
    grid = (n // tn, S // ts)
    sent, lands, build = carried or ([], [], None)
    k = len(sent)

    def body(a_ref, b_ref, *rest):
        o_ref = rest[k]
        j, s = pl.program_id(0), pl.program_id(1)
        if k:
            copies = build(rest[:k], rest[k + 1:2 * k + 1], rest[2 * k + 1], rest[2 * k + 2])

            @pl.when((j == 0) & (s == 0))
            def _():
                for cp in copies:
                    cp.start()

        @pl.when(s == 0)
        def _():
            o_ref[...] = jnp.zeros((M, tn), F32)

        o_ref[...] += _dot_tn(a_ref[...], b_ref[...])

        if k:
            @pl.when((j == grid[0] - 1) & (s == grid[1] - 1))
            def _():
                for cp in copies:
                    cp.wait()

    if stacked:
        out_spec, out_shape = pl.BlockSpec((None, M, tn), lambda j, s: (j, 0, 0)), (n // tn, M, tn)
    else:
        out_spec, out_shape = pl.BlockSpec((M, tn), lambda j, s: (0, j)), (M, n)
    results = pl.pallas_call(
        body, name=name, grid=grid,
        in_specs=[pl.BlockSpec((ts, M), lambda j, s: (s, 0)), pl.BlockSpec((ts, tn), lambda j, s: (s, off + j))]
        + [ANY] * k,
        out_specs=[out_spec] + [ANY] * k,
        out_shape=[jax.ShapeDtypeStruct(out_shape, F32)] + list(lands),
        scratch_shapes=[pltpu.SemaphoreType.DMA((k, 4))] * (2 if k else 0),
        compiler_params=_params(2),
    )(a, b, *sent)
    return results[0] if carried is None else (results[0], list(results[1:]))


def _adam_update(gv, w_ref, m_ref, v_ref, d_ref, m2_ref, v2_ref):
    m2 = ADAM_B1 * m_ref[...] + (1.0 - ADAM_B1) * gv
    v2 = ADAM_B2 * v_ref[...] + (1.0 - ADAM_B2) * (gv * gv)
    m2_ref[...] = m2
    v2_ref[...] = v2
    m_hat = m2 / (1.0 - ADAM_B1 ** ADAM_STEP)
    v_hat = v2 / (1.0 - ADAM_B2 ** ADAM_STEP)
    d_ref[...] = -ADAM_LR * (m_hat / (jnp.sqrt(v_hat) + ADAM_EPS) + ADAM_WD * w_ref[...])


def _adamw(w, g, m, v, name):
    r, c = w.shape
    tr = ROW_TILE if r % ROW_TILE == 0 else r

    def body(w_ref, g_ref, m_ref, v_ref, d_ref, m2_ref, v2_ref):
        _adam_update(g_ref[...], w_ref, m_ref, v_ref, d_ref, m2_ref, v2_ref)

    spec = _rows(tr, c)
    return pl.pallas_call(
        body, name=name, grid=(r // tr,), in_specs=[spec] * 4, out_specs=[spec] * 3,
        out_shape=[jax.ShapeDtypeStruct((r, c), F32)] * 3, compiler_params=_params(1),
    )(w, g, m, v)


def _adamw_halves(w, mine, other, m, v, core, name):
    r, c = mine.shape
    tr = ROW_TILE if r % ROW_TILE == 0 else r
    per = r // tr

    def body(core_ref, w_ref, mine_ref, other_ref, m_ref, v_ref, g_ref, d_ref, m2_ref, v2_ref):
        gv = jnp.where(pl.program_id(0) // per == core_ref[0], mine_ref[...], other_ref[...])
        g_ref[...] = gv
        _adam_update(gv, w_ref, m_ref, v_ref, d_ref, m2_ref, v2_ref)

    full = pl.BlockSpec((tr, c), lambda i, core: (i, 0))
    half = pl.BlockSpec((tr, c), lambda i, core: (i % per, 0))
    return pl.pallas_call(
        body, name=name,
        grid_spec=pltpu.PrefetchScalarGridSpec(num_scalar_prefetch=1, grid=(2 * per,),
                                               in_specs=[full, half, half, full, full], out_specs=[full] * 4),
        out_shape=[jax.ShapeDtypeStruct((2 * r, c), F32)] * 4, compiler_params=_params(1),
    )(core, w, mine, other, m, v)


def _after_conv(conv_acts, x, target, g1, w_in, conv_w, w_out, g2, wa_in, b_f, gq, gk, wa_out, reduce_early=None):
    reduces = reduce_early is not None
    if not reduces:
        reduce_early = lambda by_chip, tag, swapped: ([], [])
    S, D = x.shape
    H = D // HEAD_DIM
    ws = wa_in.shape[2]
    w_qkvz = jnp.concatenate([wa_in[0], wa_in[1], wa_in[2], wa_in[3][:, :4 * D - 3 * ws]], axis=1)
    wf = jnp.pad(wa_in[3][:, 4 * D - 3 * ws:], ((0, 0), (0, LANES - H)))
    bf = jnp.pad(b_f, ((0, 0), (0, LANES - H)))
    gq128 = jnp.concatenate([gq, gq], axis=1)
    gk128 = jnp.concatenate([gk, gk], axis=1)

    proj, h1, yc, y, x1 = conv_acts
    h2, qraw, kraw, z, f, ends, rel, qa, ka, va, vt = _attn_front(x1, g2, w_qkvz, wf, bf, gq128, gk128)
    T = vt.shape[3]
    kstart, qend, bound = _skip_tables(ends[:, 0, :H], ends[:, 1, :H], ends[:, 2, :H], gq, gk, min(ATT_GROUP, S // T))
    o_aug, lse = lax.cond(2.0 * bound <= PLAIN_EXP_MAX, functools.partial(_attn_fwd, online_max=False),
                          functools.partial(_attn_fwd, online_max=True), kstart, qa, ka, vt)
    dx2, dx2b, o2b, dz, doa, qa2, loss = _attn_out(o_aug, lse.reshape(H, S), rel, z, x1, target, wa_out, qa)
    dqt, dka, dva = _attn_bwd(qend, qa2, doa, ka, va, T)
    dproj2, small = _attn_proj_bwd(dqt, dka, dva, qraw, kraw, dz, f, gq128, gk128)

    tn = min(1024, D)
    carry = (lambda kind, arrays: kind(arrays if reduces else []))
    dw_main = _matmul_tn(h2, dproj2, 0, 4 * D, 2 * tn, "dw_attn_in")
    dw_f = _matmul_tn(h2, dproj2, 4 * D, LANES, LANES, "dw_attn_f")
    dwa_in = jnp.concatenate([dw_main, dw_f[:, :H]], axis=1).reshape(D, 4, ws).transpose(1, 0, 2)
    dwa_out, swapped = _matmul_tn(o2b, dx2b, 0, D, tn, "dw_attn_out",
                                  carried=carry(_carried_core_swap, [dwa_in.reshape(4, 2, D // 2, ws)]))
    attn_f32, attn_bf16 = reduce_early([dwa_in, dwa_out.reshape(4, D // 4, D)], "attn", swapped)
    dproj1, dx, dx1b, dg1, dcw, dg2 = _conv_bwd(dproj2, w_qkvz, wf, x1, g2, dx2, x, g1, w_in, w_out, conv_w, proj, yc)
    dw_in, attn_arrived = _matmul_tn(h1, dproj1, 0, 4 * D, D, "dw_conv_in", stacked=True,
                                     carried=carry(_carried_chip_sums, attn_bf16))
    in_f32, in_bf16 = reduce_early([dw_in], "conv_in", [])
    dw_out, in_arrived = _matmul_tn(y, dx1b, 0, D, tn, "dw_conv_out", carried=carry(_carried_chip_sums, in_bf16))
    grads = dict(conv_norm_g=dg1, conv_w_in=dw_in, conv_w=dcw, conv_w_out=dw_out, attn_norm_g=dg2,
                 attn_w_in=dwa_in, attn_b_f=small[0:1, :H],
                 attn_q_norm_g=small[1:2, :HEAD_DIM] + small[1:2, HEAD_DIM:],
                 attn_k_norm_g=small[2:3, :HEAD_DIM] + small[2:3, HEAD_DIM:], attn_w_out=dwa_out)
    return loss[0, 0], dx, grads, (attn_f32 + in_f32, attn_arrived + in_arrived)


def _coords():
    return lax.axis_index("x"), lax.axis_index("y"), lax.axis_index("c")


def _at(ref, idx):
    return ref.at[idx] if idx else ref


def _other_chips(x, y):
    return [(1 - x, y), (x, 1 - y), (1 - x, 1 - y)]


def _gather_plan(src, out, send, recv):
    x, y, c = _coords()
    mine = 2 * x + y
    sibling = (x, y, 1 - c)
    others = [(a, k, 2 * px + py, (px, py)) for a in range(len(src)) for k, (px, py) in enumerate(_other_chips(x, y))]

    def copy(a, k, chip, half, to, source=None):
        dst = out[a].at[chip, half]
        return pltpu.make_async_remote_copy(src_ref=dst if source is None else source, dst_ref=dst,
                                            send_sem=send.at[a, k], recv_sem=recv.at[a, k],
                                            device_id=to, device_id_type=MESH)

    first = [copy(a, k, mine, c, (*chip, c), source=src[a].at[c]) for a, k, _, chip in others]
    own = [pltpu.make_async_remote_copy(src_ref=src[a], dst_ref=out[a].at[mine], send_sem=send.at[a, 6],
                                        recv_sem=recv.at[a, 6], device_id=sibling, device_id_type=MESH)
           for a in range(len(src))]
    passed = [copy(a, 3 + k, slot, c, sibling) for a, k, slot, _ in others]

    def start():
        for cp in first + own:
            cp.start()

    def forward():
        for (a, k, slot, _), cp in zip(others, passed):
            copy(a, k, slot, c, (x, y, c)).wait_recv()
            cp.start()

    def finish():
        for a, k, slot, _ in others:
            copy(a, 3 + k, slot, 1 - c, (x, y, c)).wait_recv()
        for cp in own:
            cp.wait_recv()
        for cp in first + passed + own:
            cp.wait_send()

    return start, forward, finish


def _all_gather(halved, whole):
    nh, nw = len(halved), len(whole)

    def body(*refs):
        src_h, src_w = refs[:nh], refs[nh:nh + nw]
        out_h, out_w = refs[nh + nw:2 * nh + nw], refs[2 * nh + nw:2 * (nh + nw)]
        send_h, recv_h, send_w, recv_w = refs[2 * (nh + nw):]
        x, y, c = _coords()
        mine = 2 * x + y
        chips = _other_chips(x, y)

        def copy_w(a, k, chip, to):
            return pltpu.make_async_remote_copy(src_ref=src_w[a], dst_ref=out_w[a].at[chip],
                                                send_sem=send_w.at[a, k], recv_sem=recv_w.at[a, k],
                                                device_id=to, device_id_type=MESH)

        start, forward, finish = _gather_plan(src_h, out_h, send_h, recv_h)
        small = [copy_w(a, k, mine, (*chip, c)) for a in range(nw) for k, chip in enumerate(chips)]
        small += [copy_w(a, 3, mine, (x, y, 1 - c)) for a in range(nw)]
        start()
        for cp in small:
            cp.start()
        forward()
        finish()
        for a in range(nw):
            for k, (px, py) in enumerate(chips):
                copy_w(a, k, 2 * px + py, (x, y, c)).wait_recv()
            copy_w(a, 3, mine, (x, y, c)).wait_recv()
        for cp in small:
            cp.wait_send()

    out_shape = [jax.ShapeDtypeStruct((4,) + a.shape, a.dtype) for a in list(halved) + list(whole)]
    return pl.pallas_call(
        body, name="gather_weights", in_specs=[ANY] * (nh + nw), out_specs=[ANY] * (nh + nw), out_shape=out_shape,
        scratch_shapes=[pltpu.SemaphoreType.DMA((nh, 7)), pltpu.SemaphoreType.DMA((nh, 7)),
                        pltpu.SemaphoreType.DMA((nw, 4)), pltpu.SemaphoreType.DMA((nw, 4))],
    )(*halved, *whole)


def _exchange(name, srcs, lands, copies, local_copies):
    ns, nl, n, nloc = len(srcs), len(lands), len(copies), len(local_copies)

    def body(*refs):
        src, land = refs[:ns], refs[ns:ns + nl]
        send, recv, local_sem = refs[ns + nl:]
        me = _coords()
        started = []
        for k, (si, s_at, li, l_at, ci) in enumerate(local_copies):
            cp = pltpu.make_async_copy(_at(src[si], s_at(*me)), _at(land[li], l_at(*me)), local_sem.at[k])
            cp.start()
            started.append(cp)
        remote = []
        for k, (si, s_at, li, l_at, peer) in enumerate(copies):
            cp = pltpu.make_async_remote_copy(src_ref=_at(src[si], s_at(*me)), dst_ref=_at(land[li], l_at(*me)),
                                              send_sem=send.at[k], recv_sem=recv.at[k],
                                              device_id=peer(*me), device_id_type=MESH)
            cp.start()
            remote.append(cp)
        for cp in remote:
            cp.wait()
        for cp in started:
            cp.wait()

    return pl.pallas_call(
        body, name=name, in_specs=[ANY] * ns, out_specs=[ANY] * nl, out_shape=list(lands),
        scratch_shapes=[pltpu.SemaphoreType.DMA((n,)), pltpu.SemaphoreType.DMA((n,)),
                        pltpu.SemaphoreType.DMA((max(nloc, 1),))],
    )(*srcs)


def _add_pairs(a, b, core, name):
    _, r, cols = b.shape
    tr = ROW_TILE if r % ROW_TILE == 0 else r

    def body(core_ref, a_ref, b_ref, o_ref, ob_ref):
        s = a_ref[...] + b_ref[...]
        o_ref[...] = s
        ob_ref[...] = s.astype(BF16)

    spec = pl.BlockSpec((None, tr, cols), lambda j, i, core: (j, i, 0))
    return pl.pallas_call(
        body, name=name,
        grid_spec=pltpu.PrefetchScalarGridSpec(
            num_scalar_prefetch=1, grid=(4, r // tr),
            in_specs=[pl.BlockSpec((None, None, tr, cols), lambda j, i, core: (j, core[0], i, 0)), spec],
            out_specs=[spec, spec]),
        out_shape=[jax.ShapeDtypeStruct(b.shape, F32), jax.ShapeDtypeStruct(b.shape, BF16)],
        compiler_params=_params(2),
    )(core, a, b)


def _sum_chips(own, landed, name):
    _, r, cols = landed.shape
    tr = ROW_TILE if r % ROW_TILE == 0 else r

    def body(own_ref, land_ref, o_ref):
        acc = own_ref[...]
        for j in range(3):
            acc = acc + land_ref[j].astype(F32)
        o_ref[...] = acc

    return pl.pallas_call(
        body, name=name, grid=(r // tr,),
        in_specs=[_rows(tr, cols), pl.BlockSpec((3, tr, cols), lambda i: (0, i, 0))], out_specs=_rows(tr, cols),
        out_shape=jax.ShapeDtypeStruct((r, cols), F32), compiler_params=_params(1),
    )(own, landed)


def _sum_devices(landed, name):
    def body(l_ref, o_ref):
        acc = l_ref[0]
        for j in range(1, 8):
            acc = acc + l_ref[j]
        o_ref[...] = acc

    return pl.pallas_call(body, name=name, out_shape=jax.ShapeDtypeStruct(landed.shape[1:], F32))(landed)


CHIP_FLIPS = [(1, 0), (0, 1), (1, 1)]


def _flip(fx, fy, fc):
    return lambda x, y, c: (x ^ fx, y ^ fy, c ^ fc)


def _core_swap_copies(big, landed, send, recv):
    x, y, c = _coords()
    return [pltpu.make_async_remote_copy(src_ref=big[a].at[j, 1 - c], dst_ref=landed[a].at[j],
                                         send_sem=send.at[a, j], recv_sem=recv.at[a, j],
                                         device_id=(x, y, 1 - c), device_id_type=MESH)
            for a in range(len(big)) for j in range(4)]


def _carried_core_swap(big):
    return big, [jax.ShapeDtypeStruct((4,) + g.shape[2:], F32) for g in big], _core_swap_copies


def _carried_chip_sums(sums):
    return sums, [jax.ShapeDtypeStruct((3,) + g.shape[1:], g.dtype) for g in sums], _chip_sum_copies


def _sum_cores(big, small, tag, swapped=()):
    c = lax.axis_index("c")
    rest = big[len(swapped):]
    nb = len(rest)
    copies = [(a, (lambda j: lambda x, y, c: (j, 1 - c))(j), a, (lambda j: lambda x, y, c: (j,))(j), _flip(0, 0, 1))
              for a in range(nb) for j in range(4)]
    lands = [jax.ShapeDtypeStruct((4,) + g.shape[2:], F32) for g in rest]
    srcs, local = list(rest), []
    if small is not None:
        flips = [(fx, fy, fc) for fx in (0, 1) for fy in (0, 1) for fc in (0, 1) if fx or fy or fc]
        copies += [(nb, lambda x, y, c: (), nb, lambda x, y, c: (4 * x + 2 * y + c,), _flip(*f)) for f in flips]
        lands.append(jax.ShapeDtypeStruct((8,) + small.shape, F32))
        local = [(nb, lambda x, y, c: (), nb, lambda x, y, c: (4 * x + 2 * y + c,), None)]
        srcs.append(small)
    landed = _exchange("swap_halves_" + tag, srcs, lands, copies, local)
    small_sum = None if small is None else _sum_devices(landed[nb], "sum_small")
    landed = list(swapped) + list(landed[:nb])
    nb = len(big)
    core = jnp.reshape(c, (1,)).astype(jnp.int32)
    sums = [_add_pairs(big[a], landed[a], core, f"add_cores_{tag}_{a}") for a in range(nb)]
    return [s for s, _ in sums], [sb for _, sb in sums], small_sum


def _chip_sum_copies(sums, landed, send, recv):
    x, y, c = _coords()
    return [pltpu.make_async_remote_copy(src_ref=sums[a].at[2 * (x ^ fx) + (y ^ fy)], dst_ref=landed[a].at[k],
                                         send_sem=send.at[a, k], recv_sem=recv.at[a, k],
                                         device_id=(x ^ fx, y ^ fy, c), device_id_type=MESH)
            for a in range(len(sums)) for k, (fx, fy) in enumerate(CHIP_FLIPS)]


def _send_chip_sums(chip_bf16):
    n = len(chip_bf16)

    def body(*refs):
        copies = _chip_sum_copies(refs[:n], refs[n:2 * n], refs[2 * n], refs[2 * n + 1])
        for cp in copies:
            cp.start()
        for cp in copies:
            cp.wait()

    return pl.pallas_call(
        body, name="send_chip_sums", in_specs=[ANY] * n, out_specs=[ANY] * n,
        out_shape=[jax.ShapeDtypeStruct((3,) + g.shape[1:], BF16) for g in chip_bf16],
        scratch_shapes=[pltpu.SemaphoreType.DMA((n, 3)), pltpu.SemaphoreType.DMA((n, 3))],
    )(*chip_bf16)


def _sum_chips_and_share(chip_f32, landed):
    x, y, _ = _coords()
    totals = [_sum_chips(lax.dynamic_index_in_dim(f, 2 * x + y, axis=0, keepdims=False), l, f"sum_chips_{a}")
              for a, (f, l) in enumerate(zip(chip_f32, landed))]
    copies = [(a, lambda x, y, c: (), a, lambda x, y, c: (), _flip(0, 0, 1)) for a in range(len(totals))]
    lands = [jax.ShapeDtypeStruct(t.shape, F32) for t in totals]
    return list(zip(totals, _exchange("swap_sums", totals, lands, copies, [])))


def kernel(x, conv_norm_g, conv_w_in, conv_w, conv_w_out, attn_norm_g, attn_w_in, attn_b_f, attn_q_norm_g, attn_k_norm_g, attn_w_out, loss_target, m_conv_norm_g, m_conv_w_in, m_conv_w, m_conv_w_out, m_attn_norm_g, m_attn_w_in, m_attn_b_f, m_attn_q_norm_g, m_attn_k_norm_g, m_attn_w_out, v_conv_norm_g, v_conv_w_in, v_conv_w, v_conv_w_out, v_attn_norm_g, v_attn_w_in, v_attn_b_f, v_attn_q_norm_g, v_attn_k_norm_g, v_attn_w_out):
    xi, yi, _ = _coords()
    chip = 2 * xi + yi
    D = x.shape[2]
    H = D // HEAD_DIM
    names = ["conv_norm_g", "conv_w_in", "conv_w", "conv_w_out", "attn_norm_g", "attn_w_in", "attn_b_f",
             "attn_q_norm_g", "attn_k_norm_g", "attn_w_out"]
    weights = dict(zip(names, [conv_norm_g, conv_w_in, conv_w, conv_w_out, attn_norm_g, attn_w_in, attn_b_f,
                               attn_q_norm_g, attn_k_norm_g, attn_w_out]))
    m_in = dict(zip(names, [m_conv_norm_g, m_conv_w_in, m_conv_w, m_conv_w_out, m_attn_norm_g, m_attn_w_in,
                            m_attn_b_f, m_attn_q_norm_g, m_attn_k_norm_g, m_attn_w_out]))
    v_in = dict(zip(names, [v_conv_norm_g, v_conv_w_in, v_conv_w, v_conv_w_out, v_attn_norm_g, v_attn_w_in,
                            v_attn_b_f, v_attn_q_norm_g, v_attn_k_norm_g, v_attn_w_out]))
    weights = {k: w[0] for k, w in weights.items()}
    m_in = {k: w[0] for k, w in m_in.items()}
    v_in = {k: w[0] for k, w in v_in.items()}

    big_names = ["conv_w_in", "attn_w_in", "conv_w_out", "attn_w_out"]
    halved = {k: weights[k].astype(BF16).reshape(2, weights[k].shape[0] // 2, weights[k].shape[1]) for k in big_names}
    q = D // 4
    small_w = jnp.concatenate([weights["conv_w"], weights["attn_norm_g"][None, :], jnp.zeros((4, q), F32)], axis=0)
    g_in, g_out, g_small = _all_gather([halved["conv_w_in"], halved["conv_w_out"]], [small_w])
    w_in = g_in.reshape(4, D, D)
    w_out = g_out.reshape(D, D)
    conv_w_full = g_small[:, 0:3, :].transpose(1, 0, 2).reshape(3, D)
    attn_g_full = g_small[:, 3, :].reshape(1, D)
    g1 = weights["conv_norm_g"][None, :]

    conv_acts, (ga_in, ga_out) = _conv_fwd(x[0], g1, w_in, conv_w_full, w_out, [halved["attn_w_in"], halved["attn_w_out"]])
    def core_sums(by_chip, tag, swapped):
        f32, bf16, _ = _sum_cores([g.reshape(4, 2, g.shape[1] // 2, g.shape[2]) for g in by_chip], None, tag, swapped)
        return f32, bf16

    loss_part, grad_x, grads, (early_f32, early_arrived) = _after_conv(
        conv_acts, x[0], loss_target[0], g1, w_in, conv_w_full, w_out, attn_g_full, ga_in.reshape(4, D, D + H // 4),
        weights["attn_b_f"][None, :], weights["attn_q_norm_g"][None, :], weights["attn_k_norm_g"][None, :],
        ga_out.reshape(D, D), reduce_early=core_sums)

    tail = jnp.concatenate([grads["attn_b_f"], grads["attn_q_norm_g"], grads["attn_k_norm_g"],
                            jnp.reshape(loss_part, (1, 1)), jnp.zeros((1, D - H - 2 * HEAD_DIM - 1), F32)], axis=1)
    small = jnp.concatenate([grads["conv_norm_g"], grads["conv_w"], grads["attn_norm_g"], tail,
                             jnp.zeros((2, D), F32)], axis=0)
    out_f32, out_bf16, small_sum = _sum_cores([grads["conv_w_out"].reshape(4, 2, D // 8, D)], small, "conv_out")
    big_names = ["attn_w_in", "attn_w_out", "conv_w_in", "conv_w_out"]
    reduced = _sum_chips_and_share(early_f32 + out_f32, early_arrived + list(_send_chip_sums(out_bf16)))
    final = {}
    final["conv_norm_g"] = small_sum[0]
    final["conv_w"] = lax.dynamic_slice_in_dim(small_sum[1:4], chip * q, q, axis=1)
    final["attn_norm_g"] = lax.dynamic_slice_in_dim(small_sum[4], chip * q, q, axis=0)
    final["attn_b_f"] = small_sum[5, :H]
    final["attn_q_norm_g"] = small_sum[5, H:H + HEAD_DIM]
    final["attn_k_norm_g"] = small_sum[5, H + HEAD_DIM:H + 2 * HEAD_DIM]
    loss = small_sum[5, H + 2 * HEAD_DIM]

    delta, new_m, new_v = {}, {}, {}
    core = jnp.reshape(lax.axis_index("c"), (1,)).astype(jnp.int32)
    for k, (mine, other) in zip(big_names, reduced):
        final[k], delta[k], new_m[k], new_v[k] = _adamw_halves(weights[k], mine, other, m_in[k], v_in[k], core,
                                                               "adamw_" + k)
    for k in names:
        if k in big_names:
            continue
        shape = weights[k].shape
        as2d = (lambda a: a.reshape(1, -1)) if len(shape) == 1 else (lambda a: a)
        d, m2, v2 = _adamw(as2d(weights[k]), as2d(final[k]), as2d(m_in[k]), as2d(v_in[k]), "adamw_" + k)
        delta[k], new_m[k], new_v[k] = d.reshape(shape), m2.reshape(shape), v2.reshape(shape)
    lead = lambda a: a[None]
    return (loss, grad_x[None], *[lead(final[k]) for k in names], *[lead(delta[k]) for k in names],
            *[lead(new_m[k]) for k in names], *[lead(new_v[k]) for k in names])
```

```python
import functools

import jax
import jax.numpy as jnp
from jax import lax
from jax.experimental import pallas as pl
from jax.experimental.pallas import tpu as pltpu

F32 = jnp.float32
BF16 = jnp.bfloat16
HEAD_DIM = 64
LANES = 128
RMS_EPS = 1e-6
NEG = -1e30
Q_SCALE = 0.125
ROW_TILE = 256
CONV_TILE = 512
ATT_GROUP = 4
SKIP_LOG = 106.0
PLAIN_EXP_MAX = 60.0
TN_ROWS = 2048
VMEM_LIMIT = 56 << 20
ADAM_LR, ADAM_B1, ADAM_B2, ADAM_EPS, ADAM_WD, ADAM_STEP = 0.001, 0.9, 0.999, 1e-08, 0.01, 10
MESH = pl.DeviceIdType.MESH
ANY = pl.BlockSpec(memory_space=pl.ANY)


def _lane():
    return lax.broadcasted_iota(jnp.int32, (1, LANES), 1)


def _split3(x):
    hi = x.astype(BF16).astype(F32)
    r = x - hi
    mid = r.astype(BF16).astype(F32)
    lo = (r - mid).astype(BF16).astype(F32)
    return hi, mid, lo


STAT_STRIDE = 16
ONE_LANE = 3 * STAT_STRIDE


def _pack3(x, lane, one):
    hi, mid, lo = _split3(x)
    packed = hi + pltpu.roll(mid, STAT_STRIDE, 1) + pltpu.roll(lo, 2 * STAT_STRIDE, 1)
    return jnp.where(lane == ONE_LANE, one, packed).astype(BF16)


def _scatter_matrices(H):
    rows = lax.broadcasted_iota(jnp.int32, (LANES, H * LANES), 0)
    cols = lax.broadcasted_iota(jnp.int32, (LANES, H * LANES), 1)
    head, within = cols // LANES, cols % LANES
    extra = within - _aug(head % 2)
    term = (rows < ONE_LANE) & (rows % STAT_STRIDE == head)
    first = ((term & (extra == rows // STAT_STRIDE)) | ((rows == ONE_LANE) & (extra >= 3) & (extra < 6)))
    second = ((term & (extra - 3 == rows // STAT_STRIDE)) | ((rows == ONE_LANE) & (extra >= 0) & (extra < 3)))
    return first.astype(BF16), second.astype(BF16)


def _col(x, lane, idx):
    return jnp.sum(jnp.where(lane == idx, x, 0.0), axis=1, keepdims=True)


def _feat(parity):
    return HEAD_DIM * parity


def _aug(parity):
    return HEAD_DIM * (1 - parity)


def _own(lane, parity):
    return (lane >= _feat(parity)) & (lane < _feat(parity) + HEAD_DIM)


def _head_tile(ref, hd, lane):
    j = hd // 2
    return jnp.where(_own(lane, hd % 2), ref[:, LANES * j:LANES * (j + 1)], 0.0)


def _pair_tile(even, odd, lane):
    return jnp.where(lane < HEAD_DIM, even, odd)


def _sigmoid(x):
    return 0.5 * jnp.tanh(0.5 * x) + 0.5


def _dot(a, b):
    return jnp.dot(a, b, preferred_element_type=F32)


def _dot_nt(a, b):
    return lax.dot_general(a, b, (((1,), (1,)), ((), ())), preferred_element_type=F32)


def _dot_tn(a, b):
    return lax.dot_general(a, b, (((0,), (0,)), ((), ())), preferred_element_type=F32)


def _dot01(tri, x):
    hi, mid, lo = _split3(x)
    return _dot(tri, hi.astype(BF16)) + _dot(tri, mid.astype(BF16)) + _dot(tri, lo.astype(BF16))


def _rms_bwd(dh, x, g):
    inv = lax.rsqrt(jnp.mean(x * x, axis=-1, keepdims=True) + RMS_EPS)
    xh = x * inv
    dxn = dh * g
    dx = inv * (dxn - xh * jnp.mean(dxn * xh, axis=-1, keepdims=True))
    return dx, jnp.sum(dh * xh, axis=0, keepdims=True)


def _head_rms_bwd(dn, t, g, ones):
    sq = t * t
    hi = sq.astype(BF16)
    lo = (sq - hi.astype(F32)).astype(BF16)
    inv = lax.rsqrt((_dot(hi, ones) + _dot(lo, ones)) * (1.0 / HEAD_DIM) + RMS_EPS)
    th = t * inv
    gd = dn * g
    d = inv * (gd - th * (jnp.sum(gd * th, axis=1, keepdims=True) * (1.0 / HEAD_DIM)))
    return d, jnp.sum(dn * th, axis=0, keepdims=True)


def _params(n_grid):
    return pltpu.CompilerParams(dimension_semantics=("arbitrary",) * n_grid, vmem_limit_bytes=VMEM_LIMIT)


def _rows(tm, cols, rev=None):
    if rev is None:
        return pl.BlockSpec((tm, cols), lambda i: (i, 0))
    return pl.BlockSpec((tm, cols), lambda i: (rev - i, 0))


def _whole(shape, buffers=None):
    mode = {} if buffers is None else dict(pipeline_mode=pl.Buffered(buffers))
    return pl.BlockSpec(shape, lambda *_: (0,) * len(shape), **mode)


RING_SLOTS = 3


def _ring_fetch(arrays, slots, sems, tm, steps):
    s = pl.program_id(0)

    def copy(k, step):
        slot = step % RING_SLOTS
        rows = pl.ds(pl.multiple_of(step * tm, tm), tm)
        return pltpu.make_async_copy(arrays[k].at[rows, :], slots[k].at[slot], sems.at[k, slot])

    @pl.when(s == 0)
    def _():
        for step in range(min(RING_SLOTS - 1, steps)):
            for k in range(len(arrays)):
                copy(k, step).start()

    @pl.when(s + RING_SLOTS - 1 < steps)
    def _():
        for k in range(len(arrays)):
            copy(k, s + RING_SLOTS - 1).start()

    for k in range(len(arrays)):
        copy(k, s).wait()
    return [slots[k].at[s % RING_SLOTS] for k in range(len(arrays))]


def _conv_fwd(x, g1, w_in, conv_w, w_out, later):
    S, D = x.shape
    tm = min(CONV_TILE, S)
    sub = min(ROW_TILE, tm)
    steps = S // tm
    n = len(later)

    def body(x_ref, g_ref, win_ref, cw_ref, wout_ref, *rest):
        shard_refs, rest = rest[:n], rest[n:]
        proj_ref, h_ref, yc_ref, y_ref, x1_ref = rest[:5]
        gathered_refs, rest = rest[5:5 + n], rest[5 + n:]
        prev_u = rest[0]
        if n:
            start, forward, finish = _gather_plan(shard_refs, gathered_refs, rest[1], rest[2])
            pl.when(pl.program_id(0) == 0)(start)
            pl.when(pl.program_id(0) == steps // 2)(forward)

        @pl.when(pl.program_id(0) == 0)
        def _():
            prev_u[...] = jnp.zeros((sub, D), F32)

        for r in range(0, tm, sub):
            rows = slice(r, r + sub)
            xv = x_ref[rows, :]
            inv = lax.rsqrt(jnp.mean(xv * xv, axis=-1, keepdims=True) + RMS_EPS)
            h = (xv * inv * g_ref[...]).astype(BF16)
            h_ref[rows, :] = h
            for j in range(4):
                proj_ref[rows, j * D:(j + 1) * D] = _dot(h, win_ref[j])
            u = proj_ref[rows, D:2 * D] * proj_ref[rows, 2 * D:3 * D]
            pu = prev_u[...]
            row = lax.broadcasted_iota(jnp.int32, (sub, 1), 0)
            u1 = jnp.where(row < 1, pltpu.roll(pu, 1, 0), pltpu.roll(u, 1, 0))
            u2 = jnp.where(row < 2, pltpu.roll(pu, 2, 0), pltpu.roll(u, 2, 0))
            prev_u[...] = u
            w = cw_ref[...]
            yc = w[2:3] * u + w[1:2] * u1 + w[0:1] * u2
            yc_ref[rows, :] = yc
            z = proj_ref[rows, 3 * D:4 * D]
            y = (proj_ref[rows, 0:D] * yc * (z * _sigmoid(z))).astype(BF16)
            y_ref[rows, :] = y
            x1_ref[rows, :] = xv + _dot(y, wout_ref[...])

        if n:
            pl.when(pl.program_id(0) == steps - 1)(finish)

    results = pl.pallas_call(
        body, name="conv_fwd", grid=(steps,),
        in_specs=[_rows(tm, D), _whole((1, D)), _whole((4, D, D), 1), _whole((3, D)), _whole((D, D), 1)] + [ANY] * n,
        out_specs=[_rows(tm, 4 * D), _rows(tm, D), _rows(tm, D), _rows(tm, D), _rows(tm, D)] + [ANY] * n,
        out_shape=[jax.ShapeDtypeStruct((S, 4 * D), F32), jax.ShapeDtypeStruct((S, D), BF16),
                   jax.ShapeDtypeStruct((S, D), F32), jax.ShapeDtypeStruct((S, D), BF16),
                   jax.ShapeDtypeStruct((S, D), F32)] + [jax.ShapeDtypeStruct((4,) + a.shape, a.dtype) for a in later],
        scratch_shapes=[pltpu.VMEM((sub, D), F32)] + [pltpu.SemaphoreType.DMA((n, 7))] * (2 if n else 0),
        compiler_params=_params(1),
    )(x, g1, w_in, conv_w, w_out, *later)
    return results[:5], results[5:]


def _conv_bwd(dproj2, wa, wf, x1, g2, dx2, x, g1, w_in, w_out, conv_w, proj, yc):
    S, D = x.shape
    tm = min(ROW_TILE, S)
    sub = min(ROW_TILE, tm)
    last = S // tm - 1

    def body(dp2_ref, wa_ref, wf_ref, x1_ref, g2_ref, dx2_ref, x_ref, g_ref, win_ref, wout_ref, cw_ref, proj_ref, yc_ref,
             dproj_ref, dx_ref, dx1b_ref, dg_ref, dcw_ref, dg2_ref, next_d):
        @pl.when(pl.program_id(0) == 0)
        def _():
            dg_ref[...] = jnp.zeros((1, D), F32)
            dcw_ref[...] = jnp.zeros((3, D), F32)
            dg2_ref[...] = jnp.zeros((1, D), F32)
            next_d[...] = jnp.zeros((sub, D), F32)

        for r in range(tm - sub, -1, -sub):
            rows = slice(r, r + sub)
            dh2 = _dot_nt(dp2_ref[rows, 0:4 * D], wa_ref[...]) + _dot_nt(dp2_ref[rows, 4 * D:4 * D + LANES], wf_ref[...])
            dxn2, dg2 = _rms_bwd(dh2, x1_ref[rows, :], g2_ref[...])
            dg2_ref[...] += dg2
            dx1v = dx2_ref[rows, :] + dxn2
            dx1b = dx1v.astype(BF16)
            dx1b_ref[rows, :] = dx1b
            dy = _dot_nt(dx1b, wout_ref[...])
            b = proj_ref[rows, 0:D]
            c = proj_ref[rows, D:2 * D]
            xin = proj_ref[rows, 2 * D:3 * D]
            z = proj_ref[rows, 3 * D:4 * D]
            sg = _sigmoid(z)
            sz = z * sg
            ycv = yc_ref[rows, :]
            d0 = dy * b * sz
            dproj_ref[rows, 0:D] = (dy * ycv * sz).astype(BF16)
            dproj_ref[rows, 3 * D:4 * D] = (dy * b * ycv * (sg * (1.0 + z * (1.0 - sg)))).astype(BF16)
            nd = next_d[...]
            row = lax.broadcasted_iota(jnp.int32, (sub, 1), 0)
            d1 = jnp.where(row >= sub - 1, pltpu.roll(nd, sub - 1, 0), pltpu.roll(d0, sub - 1, 0))
            d2 = jnp.where(row >= sub - 2, pltpu.roll(nd, sub - 2, 0), pltpu.roll(d0, sub - 2, 0))
            next_d[...] = d0
            w = cw_ref[...]
            du = w[2:3] * d0 + w[1:2] * d1 + w[0:1] * d2
            u = c * xin
            dcw_ref[2:3, :] += jnp.sum(d0 * u, axis=0, keepdims=True)
            dcw_ref[1:2, :] += jnp.sum(d1 * u, axis=0, keepdims=True)
            dcw_ref[0:1, :] += jnp.sum(d2 * u, axis=0, keepdims=True)
            dproj_ref[rows, D:2 * D] = (du * xin).astype(BF16)
            dproj_ref[rows, 2 * D:3 * D] = (du * c).astype(BF16)
            dh = _dot_nt(dproj_ref[rows, 0:D], win_ref[0])
            for j in range(1, 4):
                dh = dh + _dot_nt(dproj_ref[rows, j * D:(j + 1) * D], win_ref[j])
            dxn, dg = _rms_bwd(dh, x_ref[rows, :], g_ref[...])
            dx_ref[rows, :] = dx1v + dxn
            dg_ref[...] += dg

    return pl.pallas_call(
        body, name="conv_bwd", grid=(S // tm,),
        in_specs=[_rows(tm, 4 * D + LANES, last), _whole((D, 4 * D), 1), _whole((D, LANES), 1), _rows(tm, D, last),
                  _whole((1, D)), _rows(tm, D, last),
                  _rows(tm, D, last), _whole((1, D)), _whole((4, D, D), 1), _whole((D, D), 1),
                  _whole((3, D)), _rows(tm, 4 * D, last), _rows(tm, D, last)],
        out_specs=[_rows(tm, 4 * D, last), _rows(tm, D, last), _rows(tm, D, last), _whole((1, D)), _whole((3, D)),
                   _whole((1, D))],
        out_shape=[jax.ShapeDtypeStruct((S, 4 * D), BF16), jax.ShapeDtypeStruct((S, D), F32),
                   jax.ShapeDtypeStruct((S, D), BF16), jax.ShapeDtypeStruct((1, D), F32),
                   jax.ShapeDtypeStruct((3, D), F32), jax.ShapeDtypeStruct((1, D), F32)],
        scratch_shapes=[pltpu.VMEM((sub, D), F32)],
        compiler_params=_params(1),
    )(dproj2, wa, wf, x1, g2, dx2, x, g1, w_in, w_out, conv_w, proj, yc)


def _attn_front(x1, g2, w, wf, bf, gq, gk):
    S, D = x1.shape
    H = D // HEAD_DIM
    tm = min(ROW_TILE, S)
    tri = (lax.broadcasted_iota(jnp.int32, (tm, tm), 1) <= lax.broadcasted_iota(jnp.int32, (tm, tm), 0)).astype(BF16)

    def body(x_ref, g_ref, w_ref, wf_ref, bf_ref, gq_ref, gk_ref, tri_ref, first_ref, second_ref,
             h_ref, qh_ref, kh_ref, z_ref, f_ref, ends_ref, rel_ref, qa_ref, ka_ref, va_ref, vt_ref,
             carry, v_s, qraw_ref, kraw_ref):
        @pl.when(pl.program_id(0) == 0)
        def _():
            carry[...] = jnp.zeros((8, LANES), F32)

        xv = x_ref[...]
        inv = lax.rsqrt(jnp.mean(xv * xv, axis=-1, keepdims=True) + RMS_EPS)
        h = (xv * inv * g_ref[...]).astype(BF16)
        h_ref[...] = h
        qraw_ref[...] = _dot(h, w_ref[:, 0:D])
        kraw_ref[...] = _dot(h, w_ref[:, D:2 * D])
        v_s[...] = _dot(h, w_ref[:, 2 * D:3 * D])
        z_ref[...] = _dot(h, w_ref[:, 3 * D:4 * D])
        lane = _lane()
        f = _dot(h, wf_ref[...]) + bf_ref[...]
        f_ref[...] = f
        logf = jnp.where(lane < H, jnp.minimum(f, 0.0) - jnp.log(1.0 + jnp.exp(-jnp.abs(f))), 0.0)
        cs = _dot01(tri_ref[...], logf) + carry[0:1, :]
        carry[...] = jnp.broadcast_to(cs[tm - 1:tm, :], (8, LANES))
        diags = jnp.zeros((tm, LANES), F32)
        for hd in range(H):
            sl = slice(LANES * hd, LANES * (hd + 1))
            a = _aug(hd % 2)
            if hd % 2 == 0:
                qh_ref[hd // 2] = qraw_ref[:, LANES * (hd // 2):LANES * (hd // 2 + 1)]
                kh_ref[hd // 2] = kraw_ref[:, LANES * (hd // 2):LANES * (hd // 2 + 1)]
            qt = _head_tile(qraw_ref, hd, lane)
            qn = qt * lax.rsqrt(jnp.sum(qt * qt, axis=1, keepdims=True) * (1.0 / HEAD_DIM) + RMS_EPS) * gq_ref[...]
            kt = _head_tile(kraw_ref, hd, lane)
            kn = kt * lax.rsqrt(jnp.sum(kt * kt, axis=1, keepdims=True) * (1.0 / HEAD_DIM) + RMS_EPS) * gk_ref[...]
            diags = diags + jnp.where(lane == hd, jnp.sum(qn * kn, axis=1, keepdims=True) * Q_SCALE, 0.0)
            qa_ref[:, sl] = (qn * Q_SCALE).astype(BF16)
            ka_ref[:, sl] = kn.astype(BF16)
            va = jnp.where((lane >= a) & (lane < a + 3), 1.0, _head_tile(v_s, hd, lane))
            va_ref[:, sl] = va.astype(BF16)
            vt_ref[hd] = va.T.astype(BF16)
        rel = cs - diags
        rel_ref[...] = rel
        row = lax.broadcasted_iota(jnp.int32, (8, LANES), 0)
        ends_ref[...] = jnp.where(row == 0, cs[0:1, :],
                                  jnp.where(row == 1, cs[tm - 1:tm, :], jnp.min(diags, axis=0, keepdims=True)))
        qa_ref[...] += _dot(_pack3(rel, lane, 1.0), first_ref[...]).astype(BF16)
        ka_ref[...] += _dot(_pack3(-cs, lane, 1.0), second_ref[...]).astype(BF16)

    nb = S // tm
    heads = pl.BlockSpec((H // 2, tm, LANES), lambda i: (0, i, 0))
    return pl.pallas_call(
        body, name="attn_front", grid=(nb,),
        in_specs=[_rows(tm, D), _whole((1, D)), _whole((D, 4 * D)), _whole((D, LANES)), _whole((1, LANES)),
                  _whole((1, LANES)), _whole((1, LANES)), _whole((tm, tm)), _whole((LANES, H * LANES)),
                  _whole((LANES, H * LANES))],
        out_specs=[_rows(tm, D), heads, heads, _rows(tm, D), _rows(tm, LANES),
                   pl.BlockSpec((None, 8, LANES), lambda i: (i, 0, 0)), _rows(tm, LANES),
                   _rows(tm, H * LANES), _rows(tm, H * LANES), _rows(tm, H * LANES),
                   pl.BlockSpec((H, None, LANES, tm), lambda i: (0, i, 0, 0))],
        out_shape=[jax.ShapeDtypeStruct((S, D), BF16), jax.ShapeDtypeStruct((H // 2, S, LANES), F32),
                   jax.ShapeDtypeStruct((H // 2, S, LANES), F32), jax.ShapeDtypeStruct((S, D), F32),
                   jax.ShapeDtypeStruct((S, LANES), F32), jax.ShapeDtypeStruct((nb, 8, LANES), F32),
                   jax.ShapeDtypeStruct((S, LANES), F32),
                   jax.ShapeDtypeStruct((S, H * LANES), BF16), jax.ShapeDtypeStruct((S, H * LANES), BF16),
                   jax.ShapeDtypeStruct((S, H * LANES), BF16), jax.ShapeDtypeStruct((H, nb, LANES, tm), BF16)],
        scratch_shapes=[pltpu.VMEM((8, LANES), F32), pltpu.VMEM((tm, D), F32), pltpu.VMEM((tm, D), F32),
                        pltpu.VMEM((tm, D), F32)],
        compiler_params=_params(1),
    )(x1, g2, w, wf, bf, gq, gk, tri, *_scatter_matrices(H))


def _skip_tables(first, last, lowest, gq, gk, G):
    nb = first.shape[0]
    bound = HEAD_DIM ** 0.5 * jnp.max(jnp.abs(gq)) * jnp.max(jnp.abs(gk))
    lowest = jnp.maximum(lowest, -bound)
    idx = jnp.arange(nb)
    margin = (SKIP_LOG + bound) - lowest
    need = (last[None, :, :] <= first[:, None, :] + margin[:, None, :]) & (idx[None, :, None] < idx[:, None, None])
    need = need | (idx[None, :, None] == idx[:, None, None])
    kstart = jnp.argmax(need, axis=1)
    qend = nb - 1 - jnp.argmax(need[::-1], axis=0)
    kstart = jnp.min(kstart.reshape(2 * nb // G, G // 2, -1), axis=1)
    kstart = kstart - (kstart & 1)
    qend = jnp.max(qend.reshape(2 * nb // G, G // 2, -1), axis=1)
    return kstart.T.astype(jnp.int32), qend.T.astype(jnp.int32), bound


def _attn_fwd(kstart, qa, ka, vt, online_max):
    S = qa.shape[0]
    H = qa.shape[1] // LANES
    nb, T = vt.shape[1], vt.shape[3]
    G = 2 * nb // kstart.shape[1]
    W = G * T

    def finish(acc, shift, o_ref, lse_ref):
        a = _aug(pl.program_id(0) % 2)
        feat = lax.broadcasted_iota(jnp.int32, (LANES, 1), 0)
        l = jnp.sum(jnp.where(feat == a, acc, 0.0), axis=0, keepdims=True)
        o_ref[...] = (acc * (1.0 / l)).T
        lse_ref[...] = shift + jnp.log(l)

    def causal(st):
        return jnp.where(lax.broadcasted_iota(jnp.int32, st.shape, 0) <= lax.broadcasted_iota(jnp.int32, st.shape, 1),
                         st, NEG)

    def fast_body(ks_ref, q_ref, k_ref, vt_ref, o_ref, lse_ref, acc_ref, sa_ref, sb_ref, sc_ref):
        h, g = pl.program_id(0), pl.program_id(1)
        q = q_ref[...]
        acc_ref[...] = jnp.zeros((LANES, W), F32)

        def scores(ki, lo):
            return _dot_nt(k_ref[pl.ds(pl.multiple_of(ki * T, T), 2 * T), :], q[lo * T:, :])

        def weighted(ki, p):
            return _dot(vt_ref[ki], p[:T]) + _dot(vt_ref[ki + 1], p[T:])

        first = ks_ref[h, 2 * g + 1]
        early = jnp.minimum(ks_ref[h, 2 * g], first)

        def narrow(i, carry):
            ki = early + 2 * i
            st = _dot_nt(k_ref[pl.ds(pl.multiple_of(ki * T, T), 2 * T), :], q[:W // 2, :])
            acc_ref[:, :W // 2] += weighted(ki, jnp.exp(st).astype(BF16))
            return carry

        lax.fori_loop(0, (first - early) // 2, narrow, 0)
        steps = (g * G - first) // 2
        sa_ref[...] = scores(first, 0)

        def advance(ki, cur_ref, next_ref):
            p = jnp.exp(cur_ref[...]).astype(BF16)
            next_ref[...] = scores(ki + 2, 0)
            acc_ref[...] += weighted(ki, p)

        def loop(i, carry):
            advance(first + 4 * i, sa_ref, sb_ref)
            advance(first + 4 * i + 2, sb_ref, sa_ref)
            return carry

        lax.fori_loop(0, steps // 2, loop, 0)

        def first_own(pending_ref):
            p = jnp.exp(causal(pending_ref[...])).astype(BF16)
            if G > 2:
                sc_ref[:, :W - 2 * T] = scores(g * G + 2, 2)
            acc_ref[...] += weighted(g * G, p)

        @pl.when(steps % 2 == 1)
        def _():
            advance(g * G - 2, sa_ref, sb_ref)
            first_own(sb_ref)

        @pl.when(steps % 2 == 0)
        def _():
            first_own(sa_ref)

        if G > 2:
            acc_ref[:, 2 * T:] += weighted(g * G + 2, jnp.exp(causal(sc_ref[:, :W - 2 * T])).astype(BF16))
        for j in range(4, G, 2):
            p = jnp.exp(causal(scores(g * G + j, j))).astype(BF16)
            acc_ref[:, j * T:] += weighted(g * G + j, p)
        finish(acc_ref[...], 0.0, o_ref, lse_ref)

    def online_body(ks_ref, q_ref, k_ref, vt_ref, o_ref, lse_ref, acc_ref, m_ref):
        h, g = pl.program_id(0), pl.program_id(1)
        q = q_ref[...]
        m_ref[...] = jnp.full((8, W), NEG, F32)
        acc_ref[...] = jnp.zeros((LANES, W), F32)

        def update(st, vtb, lo):
            m_old = m_ref[0:1, lo:]
            m_new = jnp.maximum(m_old, jnp.max(st, axis=0, keepdims=True))
            p = jnp.exp(st - m_new).astype(BF16)
            acc_ref[:, lo:] = acc_ref[:, lo:] * jnp.exp(m_old - m_new) + _dot(vtb, p)
            m_ref[:, lo:] = jnp.broadcast_to(m_new, (8, W - lo))

        def loop(ki, carry):
            kb = k_ref[pl.ds(pl.multiple_of(ki * T, T), T), :]
            update(_dot_nt(kb, q), vt_ref[ki], 0)
            return carry

        lax.fori_loop(jnp.minimum(ks_ref[h, 2 * g], ks_ref[h, 2 * g + 1]), g * G, loop, 0)
        for j in range(G):
            ki = g * G + j
            kb = k_ref[pl.ds(pl.multiple_of(ki * T, T), T), :]
            update(causal(_dot_nt(kb, q[j * T:, :])), vt_ref[ki], j * T)
        finish(acc_ref[...], m_ref[0:1, :], o_ref, lse_ref)

    return pl.pallas_call(
        online_body if online_max else fast_body, name="attn_fwd_online" if online_max else "attn_fwd",
        grid_spec=pltpu.PrefetchScalarGridSpec(
            num_scalar_prefetch=1, grid=(H, nb // G),
            in_specs=[pl.BlockSpec((W, LANES), lambda h, i, ks: (i, h)),
                      pl.BlockSpec((S, LANES), lambda h, i, ks: (0, h)),
                      pl.BlockSpec((None, nb, LANES, T), lambda h, i, ks: (h, 0, 0, 0))],
            out_specs=[pl.BlockSpec((W, LANES), lambda h, i, ks: (i, h)),
                       pl.BlockSpec((None, 1, W), lambda h, i, ks: (h, 0, i))],
            scratch_shapes=[pltpu.VMEM((LANES, W), F32)] + (
                [pltpu.VMEM((8, W), F32)] if online_max else [pltpu.VMEM((2 * T, W), F32)] * 3)),
        out_shape=[jax.ShapeDtypeStruct((S, H * LANES), F32), jax.ShapeDtypeStruct((H, 1, S), F32)],
        compiler_params=_params(2),
    )(kstart, qa, ka, vt)


def _attn_out(o_aug, lse, rel, z, x1, target, w_out, qa):
    S, D = x1.shape
    H = D // HEAD_DIM
    tm = min(ROW_TILE, S)
    steps = S // tm
    ringed = [o_aug, z, x1, target, qa]

    def body(*refs):
        n = len(ringed)
        w_ref, first_ref, rel_ref, lse_ref = refs[n:n + 4]
        dx2_ref, dx2b_ref, o2b_ref, dz_ref, doa_ref, qa2_ref, loss_ref, oc_s, do_s = refs[n + 4:n + 13]
        o_ref, z_ref, x1_ref, t_ref, q_ref = _ring_fetch(refs[:n], refs[n + 13:2 * n + 13], refs[2 * n + 13], tm, steps)

        @pl.when(pl.program_id(0) == 0)
        def _():
            loss_ref[...] = jnp.zeros((1, LANES), F32)

        lane = _lane()
        for j in range(H // 2):
            oc_s[:, LANES * j:LANES * (j + 1)] = _pair_tile(o_ref[:, 2 * LANES * j:2 * LANES * j + LANES],
                                                            o_ref[:, 2 * LANES * j + LANES:2 * LANES * (j + 1)], lane)
        oc = oc_s[...]
        zv = z_ref[...]
        sg = _sigmoid(zv)
        sz = zv * sg
        o2 = (oc * sz).astype(BF16)
        o2b_ref[...] = o2
        e = x1_ref[...] + _dot(o2, w_ref[...]) - t_ref[...]
        sq = jnp.sum(jnp.sum(e * e, axis=1, keepdims=True), axis=0, keepdims=True)
        loss_ref[...] += jnp.broadcast_to(sq * (0.5 / D), (1, LANES))
        dx2 = e * (1.0 / D)
        dx2_ref[...] = dx2
        dx2b = dx2.astype(BF16)
        dx2b_ref[...] = dx2b
        do2 = _dot_nt(dx2b, w_ref[...])
        dz_ref[...] = (do2 * oc * (sg * (1.0 + zv * (1.0 - sg)))).astype(BF16)
        do_s[...] = do2 * sz
        deltas = jnp.zeros((tm, LANES), F32)
        for hd in range(H):
            dt = _head_tile(do_s, hd, lane)
            delta = jnp.sum(dt * _head_tile(oc_s, hd, lane), axis=1, keepdims=True)
            deltas = deltas + jnp.where(lane == hd, delta, 0.0)
            doa_ref[:, LANES * hd:LANES * (hd + 1)] = dt.astype(BF16)
        doa_ref[...] += _dot(_pack3(-deltas, lane, 0.0), first_ref[...]).astype(BF16)
        lse = jnp.concatenate([lse_ref[...], jnp.zeros((LANES - H, tm), F32)], axis=0).T
        rq = rel_ref[...] - lse
        tile_lane = lax.broadcasted_iota(jnp.int32, (1, H * LANES), 1)
        extra = tile_lane % LANES - _aug((tile_lane // LANES) % 2)
        kept = jnp.where((extra >= 0) & (extra < 3), jnp.zeros((), BF16), q_ref[...])
        qa2_ref[...] = kept + _dot(_pack3(rq, lane, 0.0), first_ref[...]).astype(BF16)

    return pl.pallas_call(
        body, name="attn_out", grid=(steps,),
        in_specs=[ANY] * len(ringed) + [_whole((D, D)), _whole((LANES, H * LANES)), _rows(tm, LANES),
                                        pl.BlockSpec((H, tm), lambda i: (0, i))],
        out_specs=[_rows(tm, D), _rows(tm, D), _rows(tm, D), _rows(tm, D), _rows(tm, H * LANES),
                   _rows(tm, H * LANES), _whole((1, LANES))],
        out_shape=[jax.ShapeDtypeStruct((S, D), F32), jax.ShapeDtypeStruct((S, D), BF16),
                   jax.ShapeDtypeStruct((S, D), BF16), jax.ShapeDtypeStruct((S, D), BF16),
                   jax.ShapeDtypeStruct((S, H * LANES), BF16), jax.ShapeDtypeStruct((S, H * LANES), BF16),
                   jax.ShapeDtypeStruct((1, LANES), F32)],
        scratch_shapes=[pltpu.VMEM((tm, D), F32), pltpu.VMEM((tm, D), F32)]
        + [pltpu.VMEM((RING_SLOTS, tm, a.shape[1]), a.dtype) for a in ringed]
        + [pltpu.SemaphoreType.DMA((len(ringed), RING_SLOTS))],
        compiler_params=_params(1),
    )(*ringed, w_out, _scatter_matrices(H)[0], rel, lse)


def _attn_bwd(qend, qa2, doa, ka, va, T):
    S = qa2.shape[0]
    H = qa2.shape[1] // LANES
    nb = S // T
    G = 2 * nb // qend.shape[1]
    W = G * T

    def body(qe_ref, q_ref, do_ref, k_ref, v_ref, dq_ref, dk_ref, dv_ref, dkt_acc, dvt_acc):
        h, g = pl.program_id(0), pl.program_id(1)

        @pl.when(g == 0)
        def _():
            dq_ref[...] = jnp.zeros((S, LANES), F32)

        kb = k_ref[...]
        vb = v_ref[...]
        dkt_acc[...] = jnp.zeros((LANES, W), F32)
        dvt_acc[...] = jnp.zeros((LANES, W), F32)

        def step(qi, c0, c1, masked):
            rows = pl.ds(pl.multiple_of(qi * T, T), 2 * T)
            qb = q_ref[rows, :]
            dob = do_ref[rows, :]
            s = _dot_nt(qb, kb[c0:c1])
            if masked:
                query = lax.broadcasted_iota(jnp.int32, s.shape, 0) + (c1 - 2 * T)
                s = jnp.where(lax.broadcasted_iota(jnp.int32, s.shape, 1) <= query, s, NEG)
            p = jnp.exp(s)
            ds = (p * _dot_nt(dob, vb[c0:c1])).astype(BF16)
            dvt_acc[:, c0:c1] += _dot(dob.astype(F32).T.astype(BF16), p.astype(BF16))
            dkt_acc[:, c0:c1] += _dot(qb.astype(F32).T.astype(BF16), ds)
            dq_ref[rows, :] += _dot(ds, kb[c0:c1])

        for m in range(G // 2):
            step(g * G + 2 * m, 0, (m + 1) * 2 * T, True)
        first = g * G + G
        n_all = jnp.maximum((qe_ref[h, 2 * g] - first + 2) // 2, 0)
        second = first + 2 * n_all

        def all_keys(i, carry):
            step(first + 2 * i, 0, W, False)
            return carry

        def late_keys(i, carry):
            step(second + 2 * i, W // 2, W, False)
            return carry

        lax.fori_loop(0, n_all, all_keys, 0)
        lax.fori_loop(0, (qe_ref[h, 2 * g + 1] - second + 2) // 2, late_keys, 0)
        dk_ref[...] = dkt_acc[...].T
        dv_ref[...] = dvt_acc[...].T.astype(BF16)

    heads = pl.BlockSpec((None, W, LANES), lambda h, i, qe: (h, i, 0))
    return pl.pallas_call(
        body, name="attn_bwd",
        grid_spec=pltpu.PrefetchScalarGridSpec(
            num_scalar_prefetch=1, grid=(H, nb // G),
            in_specs=[pl.BlockSpec((S, LANES), lambda h, i, qe: (0, h)), pl.BlockSpec((S, LANES), lambda h, i, qe: (0, h)),
                      pl.BlockSpec((W, LANES), lambda h, i, qe: (i, h)), pl.BlockSpec((W, LANES), lambda h, i, qe: (i, h))],
            out_specs=[pl.BlockSpec((None, S, LANES), lambda h, i, qe: (h, 0, 0)), heads, heads],
            scratch_shapes=[pltpu.VMEM((LANES, W), F32), pltpu.VMEM((LANES, W), F32)]),
        out_shape=[jax.ShapeDtypeStruct((H, S, LANES), F32), jax.ShapeDtypeStruct((H, S, LANES), F32),
                   jax.ShapeDtypeStruct((H, S, LANES), BF16)],
        compiler_params=_params(2),
    )(qend, qa2, doa, ka, va)


def _attn_proj_bwd(dqt, dka, dva, qraw, kraw, dz, f, gq, gk):
    S, D = dz.shape
    H = D // HEAD_DIM
    tm = min(CONV_TILE, S)
    last = S // tm - 1
    tri = (lax.broadcasted_iota(jnp.int32, (tm, tm), 1) >= lax.broadcasted_iota(jnp.int32, (tm, tm), 0)).astype(BF16)

    def body(dq_ref, dk_ref, dv_ref, q_ref, k_ref, dz_ref, f_ref, gq_ref, gk_ref, tri_ref, ones_ref,
             dproj_ref, small_ref, carry, pairs):
        @pl.when(pl.program_id(0) == 0)
        def _():
            small_ref[...] = jnp.zeros((8, LANES), F32)
            carry[...] = jnp.zeros((8, LANES), F32)

        lane = _lane()

        def head_pair(j, acc):
            dcs, dgq, dgk = acc
            dq2, dk2 = [], []
            q_pair, k_pair = q_ref[j], k_ref[j]
            for parity in (0, 1):
                hd = 2 * j + parity
                own, a = _own(lane, parity), _aug(parity)
                dqf = dq_ref[hd]
                dqn = jnp.where(own, dqf * Q_SCALE, 0.0)
                d, dg = _head_rms_bwd(dqn, jnp.where(own, q_pair, 0.0), gq_ref[...], ones_ref[...])
                dq2.append(d)
                dgq = dgq + dg
                dkt = dk_ref[hd]
                dcs = dcs + jnp.where(lane == hd, _col(dqf, lane, a) - _col(dkt, lane, a + 3), 0.0)
                d, dg = _head_rms_bwd(jnp.where(own, dkt, 0.0), jnp.where(own, k_pair, 0.0), gk_ref[...], ones_ref[...])
                dk2.append(d)
                dgk = dgk + dg
            pairs[0, j] = _pair_tile(*dq2, lane).astype(BF16)
            pairs[1, j] = _pair_tile(*dk2, lane).astype(BF16)
            pairs[2, j] = _pair_tile(dv_ref[2 * j], dv_ref[2 * j + 1], lane)
            return dcs, dgq, dgk

        zero = jnp.zeros((1, LANES), F32)
        dcs, dgq, dgk = lax.fori_loop(0, H // 2, head_pair, (jnp.zeros((tm, LANES), F32), zero, zero))
        for part in range(3):
            for j in range(H // 2):
                dproj_ref[:, part * D + LANES * j:part * D + LANES * (j + 1)] = pairs[part, j]
        dproj_ref[:, 3 * D:4 * D] = dz_ref[...]
        dlogf = _dot01(tri_ref[...], dcs) + carry[0:1, :]
        carry[...] = jnp.broadcast_to(dlogf[0:1, :], (8, LANES))
        df = dlogf * (1.0 / (1.0 + jnp.exp(f_ref[...])))
        dproj_ref[:, 4 * D:4 * D + LANES] = df.astype(BF16)
        small_ref[0:1, :] += jnp.sum(df, axis=0, keepdims=True)
        small_ref[1:2, :] += dgq
        small_ref[2:3, :] += dgk

    W = 4 * D + LANES
    heads = pl.BlockSpec((H, tm, LANES), lambda i: (0, last - i, 0))
    head_pairs = pl.BlockSpec((H // 2, tm, LANES), lambda i: (0, last - i, 0))
    return pl.pallas_call(
        body, name="attn_proj_bwd", grid=(S // tm,),
        in_specs=[heads, heads, heads, head_pairs, head_pairs,
                  _rows(tm, D, last), _rows(tm, LANES, last), _whole((1, LANES)), _whole((1, LANES)),
                  _whole((tm, tm)), _whole((LANES, LANES))],
        out_specs=[_rows(tm, W, last), _whole((8, LANES))],
        out_shape=[jax.ShapeDtypeStruct((S, W), BF16), jax.ShapeDtypeStruct((8, LANES), F32)],
        scratch_shapes=[pltpu.VMEM((8, LANES), F32), pltpu.VMEM((3, H // 2, tm, LANES), BF16)],
        compiler_params=_params(1),
    )(dqt, dka, dva, qraw, kraw, dz, f, gq, gk, tri, jnp.ones((LANES, LANES), BF16))


def _matmul_tn(a, b, col0, n, tn, name, stacked=False, carried=None):
    S, M = a.shape
    ts = min(TN_ROWS, S)
    off = col0 // tn
    grid = (n // tn, S // ts)
    sent, lands, build = carried or ([], [], None)
    k = len(sent)

    def body(a_ref, b_ref, *rest):
        o_ref = rest[k]
        j, s = pl.program_id(0), pl.program_id(1)
        if k:
            copies = build(rest[:k], rest[k + 1:2 * k + 1], rest[2 * k + 1], rest[2 * k + 2])

            @pl.when((j == 0) & (s == 0))
            def _():
                for cp in copies:
                    cp.start()

        @pl.when(s == 0)
        def _():
            o_ref[...] = jnp.zeros((M, tn), F32)

        o_ref[...] += _dot_tn(a_ref[...], b_ref[...])

        if k:
            @pl.when((j == grid[0] - 1) & (s == grid[1] - 1))
            def _():
                for cp in copies:
                    cp.wait()

    if stacked:
        out_spec, out_shape = pl.BlockSpec((None, M, tn), lambda j, s: (j, 0, 0)), (n // tn, M, tn)
    else:
        out_spec, out_shape = pl.BlockSpec((M, tn), lambda j, s: (0, j)), (M, n)
    results = pl.pallas_call(
        body, name=name, grid=grid,
        in_specs=[pl.BlockSpec((ts, M), lambda j, s: (s, 0)), pl.BlockSpec((ts, tn), lambda j, s: (s, off + j))]
        + [ANY] * k,
        out_specs=[out_spec] + [ANY] * k,
        out_shape=[jax.ShapeDtypeStruct(out_shape, F32)] + list(lands),
        scratch_shapes=[pltpu.SemaphoreType.DMA((k, 4))] * (2 if k else 0),
        compiler_params=_params(2),
    )(a, b, *sent)
    return results[0] if carried is None else (results[0], list(results[1:]))


def _adam_update(gv, w_ref, m_ref, v_ref, d_ref, m2_ref, v2_ref):
    m2 = ADAM_B1 * m_ref[...] + (1.0 - ADAM_B1) * gv
    v2 = ADAM_B2 * v_ref[...] + (1.0 - ADAM_B2) * (gv * gv)
    m2_ref[...] = m2
    v2_ref[...] = v2
    m_hat = m2 / (1.0 - ADAM_B1 ** ADAM_STEP)
    v_hat = v2 / (1.0 - ADAM_B2 ** ADAM_STEP)
    d_ref[...] = -ADAM_LR * (m_hat / (jnp.sqrt(v_hat) + ADAM_EPS) + ADAM_WD * w_ref[...])


def _adamw(w, g, m, v, name):
    r, c = w.shape
    tr = ROW_TILE if r % ROW_TILE == 0 else r

    def body(w_ref, g_ref, m_ref, v_ref, d_ref, m2_ref, v2_ref):
        _adam_update(g_ref[...], w_ref, m_ref, v_ref, d_ref, m2_ref, v2_ref)

    spec = _rows(tr, c)
    return pl.pallas_call(
        body, name=name, grid=(r // tr,), in_specs=[spec] * 4, out_specs=[spec] * 3,
        out_shape=[jax.ShapeDtypeStruct((r, c), F32)] * 3, compiler_params=_params(1),
    )(w, g, m, v)


def _adamw_halves(w, mine, other, m, v, core, name):
    r, c = mine.shape
    tr = ROW_TILE if r % ROW_TILE == 0 else r
    per = r // tr

    def body(core_ref, w_ref, mine_ref, other_ref, m_ref, v_ref, g_ref, d_ref, m2_ref, v2_ref):
        gv = jnp.where(pl.program_id(0) // per == core_ref[0], mine_ref[...], other_ref[...])
        g_ref[...] = gv
        _adam_update(gv, w_ref, m_ref, v_ref, d_ref, m2_ref, v2_ref)

    full = pl.BlockSpec((tr, c), lambda i, core: (i, 0))
    half = pl.BlockSpec((tr, c), lambda i, core: (i % per, 0))
    return pl.pallas_call(
        body, name=name,
        grid_spec=pltpu.PrefetchScalarGridSpec(num_scalar_prefetch=1, grid=(2 * per,),
                                               in_specs=[full, half, half, full, full], out_specs=[full] * 4),
        out_shape=[jax.ShapeDtypeStruct((2 * r, c), F32)] * 4, compiler_params=_params(1),
    )(core, w, mine, other, m, v)


def _after_conv(conv_acts, x, target, g1, w_in, conv_w, w_out, g2, wa_in, b_f, gq, gk, wa_out, reduce_early=None):
    reduces = reduce_early is not None
    if not reduces:
        reduce_early = lambda by_chip, tag, swapped: ([], [])
    S, D = x.shape
    H = D // HEAD_DIM
    ws = wa_in.shape[2]
    w_qkvz = jnp.concatenate([wa_in[0], wa_in[1], wa_in[2], wa_in[3][:, :4 * D - 3 * ws]], axis=1)
    wf = jnp.pad(wa_in[3][:, 4 * D - 3 * ws:], ((0, 0), (0, LANES - H)))
    bf = jnp.pad(b_f, ((0, 0), (0, LANES - H)))
    gq128 = jnp.concatenate([gq, gq], axis=1)
    gk128 = jnp.concatenate([gk, gk], axis=1)

    proj, h1, yc, y, x1 = conv_acts
    h2, qraw, kraw, z, f, ends, rel, qa, ka, va, vt = _attn_front(x1, g2, w_qkvz, wf, bf, gq128, gk128)
    T = vt.shape[3]
    kstart, qend, bound = _skip_tables(ends[:, 0, :H], ends[:, 1, :H], ends[:, 2, :H], gq, gk, min(ATT_GROUP, S // T))
    o_aug, lse = lax.cond(2.0 * bound <= PLAIN_EXP_MAX, functools.partial(_attn_fwd, online_max=False),
                          functools.partial(_attn_fwd, online_max=True), kstart, qa, ka, vt)
    dx2, dx2b, o2b, dz, doa, qa2, loss = _attn_out(o_aug, lse.reshape(H, S), rel, z, x1, target, wa_out, qa)
    dqt, dka, dva = _attn_bwd(qend, qa2, doa, ka, va, T)
    dproj2, small = _attn_proj_bwd(dqt, dka, dva, qraw, kraw, dz, f, gq128, gk128)

    tn = min(1024, D)
    carry = (lambda kind, arrays: kind(arrays if reduces else []))
    dw_main = _matmul_tn(h2, dproj2, 0, 4 * D, 2 * tn, "dw_attn_in")
    dw_f = _matmul_tn(h2, dproj2, 4 * D, LANES, LANES, "dw_attn_f")
    dwa_in = jnp.stack([dw_main[:, 0:ws], dw_main[:, ws:2 * ws], dw_main[:, 2 * ws:3 * ws],
                        jnp.concatenate([dw_main[:, 3 * ws:], dw_f[:, :H]], axis=1)])
    dwa_out, swapped = _matmul_tn(o2b, dx2b, 0, D, tn, "dw_attn_out",
                                  carried=carry(_carried_core_swap, [dwa_in.reshape(4, 2, D // 2, ws)]))
    attn_f32, attn_bf16 = reduce_early([dwa_in, dwa_out.reshape(4, D // 4, D)], "attn", swapped)
    dproj1, dx, dx1b, dg1, dcw, dg2 = _conv_bwd(dproj2, w_qkvz, wf, x1, g2, dx2, x, g1, w_in, w_out, conv_w, proj, yc)
    dw_in, attn_arrived = _matmul_tn(h1, dproj1, 0, 4 * D, D, "dw_conv_in", stacked=True,
                                     carried=carry(_carried_chip_sums, attn_bf16))
    in_f32, in_bf16 = reduce_early([dw_in], "conv_in", [])
    dw_out, in_arrived = _matmul_tn(y, dx1b, 0, D, tn, "dw_conv_out", carried=carry(_carried_chip_sums, in_bf16))
    grads = dict(conv_norm_g=dg1, conv_w_in=dw_in, conv_w=dcw, conv_w_out=dw_out, attn_norm_g=dg2,
                 attn_w_in=dwa_in, attn_b_f=small[0:1, :H],
                 attn_q_norm_g=small[1:2, :HEAD_DIM] + small[1:2, HEAD_DIM:],
                 attn_k_norm_g=small[2:3, :HEAD_DIM] + small[2:3, HEAD_DIM:], attn_w_out=dwa_out)
    return loss[0, 0], dx, grads, (attn_f32 + in_f32, attn_arrived + in_arrived)


def _coords():
    return lax.axis_index("x"), lax.axis_index("y"), lax.axis_index("c")


def _at(ref, idx):
    return ref.at[idx] if idx else ref


def _other_chips(x, y):
    return [(1 - x, y), (x, 1 - y), (1 - x, 1 - y)]


def _gather_plan(src, out, send, recv):
    x, y, c = _coords()
    mine = 2 * x + y
    sibling = (x, y, 1 - c)
    others = [(a, k, 2 * px + py, (px, py)) for a in range(len(src)) for k, (px, py) in enumerate(_other_chips(x, y))]

    def copy(a, k, chip, half, to, source=None):
        dst = out[a].at[chip, half]
        return pltpu.make_async_remote_copy(src_ref=dst if source is None else source, dst_ref=dst,
                                            send_sem=send.at[a, k], recv_sem=recv.at[a, k],
                                            device_id=to, device_id_type=MESH)

    first = [copy(a, k, mine, c, (*chip, c), source=src[a].at[c]) for a, k, _, chip in others]
    own = [pltpu.make_async_remote_copy(src_ref=src[a], dst_ref=out[a].at[mine], send_sem=send.at[a, 6],
                                        recv_sem=recv.at[a, 6], device_id=sibling, device_id_type=MESH)
           for a in range(len(src))]
    passed = [copy(a, 3 + k, slot, c, sibling) for a, k, slot, _ in others]

    def start():
        for cp in first + own:
            cp.start()

    def forward():
        for (a, k, slot, _), cp in zip(others, passed):
            copy(a, k, slot, c, (x, y, c)).wait_recv()
            cp.start()

    def finish():
        for a, k, slot, _ in others:
            copy(a, 3 + k, slot, 1 - c, (x, y, c)).wait_recv()
        for cp in own:
            cp.wait_recv()
        for cp in first + passed + own:
            cp.wait_send()

    return start, forward, finish


def _all_gather(halved, whole):
    nh, nw = len(halved), len(whole)

    def body(*refs):
        src_h, src_w = refs[:nh], refs[nh:nh + nw]
        out_h, out_w = refs[nh + nw:2 * nh + nw], refs[2 * nh + nw:2 * (nh + nw)]
        send_h, recv_h, send_w, recv_w = refs[2 * (nh + nw):]
        x, y, c = _coords()
        mine = 2 * x + y
        chips = _other_chips(x, y)

        def copy_w(a, k, chip, to):
            return pltpu.make_async_remote_copy(src_ref=src_w[a], dst_ref=out_w[a].at[chip],
                                                send_sem=send_w.at[a, k], recv_sem=recv_w.at[a, k],
                                                device_id=to, device_id_type=MESH)

        start, forward, finish = _gather_plan(src_h, out_h, send_h, recv_h)
        small = [copy_w(a, k, mine, (*chip, c)) for a in range(nw) for k, chip in enumerate(chips)]
        small += [copy_w(a, 3, mine, (x, y, 1 - c)) for a in range(nw)]
        start()
        for cp in small:
            cp.start()
        forward()
        finish()
        for a in range(nw):
            for k, (px, py) in enumerate(chips):
                copy_w(a, k, 2 * px + py, (x, y, c)).wait_recv()
            copy_w(a, 3, mine, (x, y, c)).wait_recv()
        for cp in small:
            cp.wait_send()

    out_shape = [jax.ShapeDtypeStruct((4,) + a.shape, a.dtype) for a in list(halved) + list(whole)]
    return pl.pallas_call(
        body, name="gather_weights", in_specs=[ANY] * (nh + nw), out_specs=[ANY] * (nh + nw), out_shape=out_shape,
        scratch_shapes=[pltpu.SemaphoreType.DMA((nh, 7)), pltpu.SemaphoreType.DMA((nh, 7)),
                        pltpu.SemaphoreType.DMA((nw, 4)), pltpu.SemaphoreType.DMA((nw, 4))],
    )(*halved, *whole)


def _exchange(name, srcs, lands, copies, local_copies):
    ns, nl, n, nloc = len(srcs), len(lands), len(copies), len(local_copies)

    def body(*refs):
        src, land = refs[:ns], refs[ns:ns + nl]
        send, recv, local_sem = refs[ns + nl:]
        me = _coords()
        started = []
        for k, (si, s_at, li, l_at, ci) in enumerate(local_copies):
            cp = pltpu.make_async_copy(_at(src[si], s_at(*me)), _at(land[li], l_at(*me)), local_sem.at[k])
            cp.start()
            started.append(cp)
        remote = []
        for k, (si, s_at, li, l_at, peer) in enumerate(copies):
            cp = pltpu.make_async_remote_copy(src_ref=_at(src[si], s_at(*me)), dst_ref=_at(land[li], l_at(*me)),
                                              send_sem=send.at[k], recv_sem=recv.at[k],
                                              device_id=peer(*me), device_id_type=MESH)
            cp.start()
            remote.append(cp)
        for cp in remote:
            cp.wait()
        for cp in started:
            cp.wait()

    return pl.pallas_call(
        body, name=name, in_specs=[ANY] * ns, out_specs=[ANY] * nl, out_shape=list(lands),
        scratch_shapes=[pltpu.SemaphoreType.DMA((n,)), pltpu.SemaphoreType.DMA((n,)),
                        pltpu.SemaphoreType.DMA((max(nloc, 1),))],
    )(*srcs)


def _add_pairs(a, b, core, name):
    _, r, cols = b.shape
    tr = ROW_TILE if r % ROW_TILE == 0 else r

    def body(core_ref, a_ref, b_ref, o_ref, ob_ref):
        s = a_ref[...] + b_ref[...]
        o_ref[...] = s
        ob_ref[...] = s.astype(BF16)

    spec = pl.BlockSpec((None, tr, cols), lambda j, i, core: (j, i, 0))
    return pl.pallas_call(
        body, name=name,
        grid_spec=pltpu.PrefetchScalarGridSpec(
            num_scalar_prefetch=1, grid=(4, r // tr),
            in_specs=[pl.BlockSpec((None, None, tr, cols), lambda j, i, core: (j, core[0], i, 0)), spec],
            out_specs=[spec, spec]),
        out_shape=[jax.ShapeDtypeStruct(b.shape, F32), jax.ShapeDtypeStruct(b.shape, BF16)],
        compiler_params=_params(2),
    )(core, a, b)


def _sum_chips(own, landed, name):
    _, r, cols = landed.shape
    tr = ROW_TILE if r % ROW_TILE == 0 else r

    def body(own_ref, land_ref, o_ref):
        acc = own_ref[...]
        for j in range(3):
            acc = acc + land_ref[j].astype(F32)
        o_ref[...] = acc

    return pl.pallas_call(
        body, name=name, grid=(r // tr,),
        in_specs=[_rows(tr, cols), pl.BlockSpec((3, tr, cols), lambda i: (0, i, 0))], out_specs=_rows(tr, cols),
        out_shape=jax.ShapeDtypeStruct((r, cols), F32), compiler_params=_params(1),
    )(own, landed)


def _sum_devices(landed, name):
    def body(l_ref, o_ref):
        acc = l_ref[0]
        for j in range(1, 8):
            acc = acc + l_ref[j]
        o_ref[...] = acc

    return pl.pallas_call(body, name=name, out_shape=jax.ShapeDtypeStruct(landed.shape[1:], F32))(landed)


CHIP_FLIPS = [(1, 0), (0, 1), (1, 1)]


def _flip(fx, fy, fc):
    return lambda x, y, c: (x ^ fx, y ^ fy, c ^ fc)


def _core_swap_copies(big, landed, send, recv):
    x, y, c = _coords()
    return [pltpu.make_async_remote_copy(src_ref=big[a].at[j, 1 - c], dst_ref=landed[a].at[j],
                                         send_sem=send.at[a, j], recv_sem=recv.at[a, j],
                                         device_id=(x, y, 1 - c), device_id_type=MESH)
            for a in range(len(big)) for j in range(4)]


def _carried_core_swap(big):
    return big, [jax.ShapeDtypeStruct((4,) + g.shape[2:], F32) for g in big], _core_swap_copies


def _carried_chip_sums(sums):
    return sums, [jax.ShapeDtypeStruct((3,) + g.shape[1:], g.dtype) for g in sums], _chip_sum_copies


def _sum_cores(big, small, tag, swapped=()):
    c = lax.axis_index("c")
    rest = big[len(swapped):]
    nb = len(rest)
    copies = [(a, (lambda j: lambda x, y, c: (j, 1 - c))(j), a, (lambda j: lambda x, y, c: (j,))(j), _flip(0, 0, 1))
              for a in range(nb) for j in range(4)]
    lands = [jax.ShapeDtypeStruct((4,) + g.shape[2:], F32) for g in rest]
    srcs, local = list(rest), []
    if small is not None:
        flips = [(fx, fy, fc) for fx in (0, 1) for fy in (0, 1) for fc in (0, 1) if fx or fy or fc]
        copies += [(nb, lambda x, y, c: (), nb, lambda x, y, c: (4 * x + 2 * y + c,), _flip(*f)) for f in flips]
        lands.append(jax.ShapeDtypeStruct((8,) + small.shape, F32))
        local = [(nb, lambda x, y, c: (), nb, lambda x, y, c: (4 * x + 2 * y + c,), None)]
        srcs.append(small)
    landed = _exchange("swap_halves_" + tag, srcs, lands, copies, local)
    small_sum = None if small is None else _sum_devices(landed[nb], "sum_small")
    landed = list(swapped) + list(landed[:nb])
    nb = len(big)
    core = jnp.reshape(c, (1,)).astype(jnp.int32)
    sums = [_add_pairs(big[a], landed[a], core, f"add_cores_{tag}_{a}") for a in range(nb)]
    return [s for s, _ in sums], [sb for _, sb in sums], small_sum


def _chip_sum_copies(sums, landed, send, recv):
    x, y, c = _coords()
    return [pltpu.make_async_remote_copy(src_ref=sums[a].at[2 * (x ^ fx) + (y ^ fy)], dst_ref=landed[a].at[k],
                                         send_sem=send.at[a, k], recv_sem=recv.at[a, k],
                                         device_id=(x ^ fx, y ^ fy, c), device_id_type=MESH)
            for a in range(len(sums)) for k, (fx, fy) in enumerate(CHIP_FLIPS)]


def _send_chip_sums(chip_bf16):
    n = len(chip_bf16)

    def body(*refs):
        copies = _chip_sum_copies(refs[:n], refs[n:2 * n], refs[2 * n], refs[2 * n + 1])
        for cp in copies:
            cp.start()
        for cp in copies:
            cp.wait()

    return pl.pallas_call(
        body, name="send_chip_sums", in_specs=[ANY] * n, out_specs=[ANY] * n,
        out_shape=[jax.ShapeDtypeStruct((3,) + g.shape[1:], BF16) for g in chip_bf16],
        scratch_shapes=[pltpu.SemaphoreType.DMA((n, 3)), pltpu.SemaphoreType.DMA((n, 3))],
    )(*chip_bf16)


def _sum_chips_and_share(chip_f32, landed):
    x, y, _ = _coords()
    totals = [_sum_chips(lax.dynamic_index_in_dim(f, 2 * x + y, axis=0, keepdims=False), l, f"sum_chips_{a}")
              for a, (f, l) in enumerate(zip(chip_f32, landed))]
    copies = [(a, lambda x, y, c: (), a, lambda x, y, c: (), _flip(0, 0, 1)) for a in range(len(totals))]
    lands = [jax.ShapeDtypeStruct(t.shape, F32) for t in totals]
    return list(zip(totals, _exchange("swap_sums", totals, lands, copies, [])))


def kernel(x, conv_norm_g, conv_w_in, conv_w, conv_w_out, attn_norm_g, attn_w_in, attn_b_f, attn_q_norm_g, attn_k_norm_g, attn_w_out, loss_target, m_conv_norm_g, m_conv_w_in, m_conv_w, m_conv_w_out, m_attn_norm_g, m_attn_w_in, m_attn_b_f, m_attn_q_norm_g, m_attn_k_norm_g, m_attn_w_out, v_conv_norm_g, v_conv_w_in, v_conv_w, v_conv_w_out, v_attn_norm_g, v_attn_w_in, v_attn_b_f, v_attn_q_norm_g, v_attn_k_norm_g, v_attn_w_out):
    xi, yi, _ = _coords()
    chip = 2 * xi + yi
    D = x.shape[2]
    H = D // HEAD_DIM
    names = ["conv_norm_g", "conv_w_in", "conv_w", "conv_w_out", "attn_norm_g", "attn_w_in", "attn_b_f",
             "attn_q_norm_g", "attn_k_norm_g", "attn_w_out"]
    weights = dict(zip(names, [conv_norm_g, conv_w_in, conv_w, conv_w_out, attn_norm_g, attn_w_in, attn_b_f,
                               attn_q_norm_g, attn_k_norm_g, attn_w_out]))
    m_in = dict(zip(names, [m_conv_norm_g, m_conv_w_in, m_conv_w, m_conv_w_out, m_attn_norm_g, m_attn_w_in,
                            m_attn_b_f, m_attn_q_norm_g, m_attn_k_norm_g, m_attn_w_out]))
    v_in = dict(zip(names, [v_conv_norm_g, v_conv_w_in, v_conv_w, v_conv_w_out, v_attn_norm_g, v_attn_w_in,
                            v_attn_b_f, v_attn_q_norm_g, v_attn_k_norm_g, v_attn_w_out]))
    weights = {k: w[0] for k, w in weights.items()}
    m_in = {k: w[0] for k, w in m_in.items()}
    v_in = {k: w[0] for k, w in v_in.items()}

    big_names = ["conv_w_in", "attn_w_in", "conv_w_out", "attn_w_out"]
    halved = {k: weights[k].astype(BF16).reshape(2, weights[k].shape[0] // 2, weights[k].shape[1]) for k in big_names}
    q = D // 4
    small_w = jnp.concatenate([weights["conv_w"], weights["attn_norm_g"][None, :], jnp.zeros((4, q), F32)], axis=0)
    g_in, g_out, g_small = _all_gather([halved["conv_w_in"], halved["conv_w_out"]], [small_w])
    w_in = g_in.reshape(4, D, D)
    w_out = g_out.reshape(D, D)
    conv_w_full = g_small[:, 0:3, :].transpose(1, 0, 2).reshape(3, D)
    attn_g_full = g_small[:, 3, :].reshape(1, D)
    g1 = weights["conv_norm_g"][None, :]

    conv_acts, (ga_in, ga_out) = _conv_fwd(x[0], g1, w_in, conv_w_full, w_out, [halved["attn_w_in"], halved["attn_w_out"]])
    def core_sums(by_chip, tag, swapped):
        f32, bf16, _ = _sum_cores([g.reshape(4, 2, g.shape[1] // 2, g.shape[2]) for g in by_chip], None, tag, swapped)
        return f32, bf16

    loss_part, grad_x, grads, (early_f32, early_arrived) = _after_conv(
        conv_acts, x[0], loss_target[0], g1, w_in, conv_w_full, w_out, attn_g_full, ga_in.reshape(4, D, D + H // 4),
        weights["attn_b_f"][None, :], weights["attn_q_norm_g"][None, :], weights["attn_k_norm_g"][None, :],
        ga_out.reshape(D, D), reduce_early=core_sums)

    tail = jnp.concatenate([grads["attn_b_f"], grads["attn_q_norm_g"], grads["attn_k_norm_g"],
                            jnp.reshape(loss_part, (1, 1)), jnp.zeros((1, D - H - 2 * HEAD_DIM - 1), F32)], axis=1)
    small = jnp.concatenate([grads["conv_norm_g"], grads["conv_w"], grads["attn_norm_g"], tail,
                             jnp.zeros((2, D), F32)], axis=0)
    out_f32, out_bf16, small_sum = _sum_cores([grads["conv_w_out"].reshape(4, 2, D // 8, D)], small, "conv_out")
    big_names = ["attn_w_in", "attn_w_out", "conv_w_in", "conv_w_out"]
    reduced = _sum_chips_and_share(early_f32 + out_f32, early_arrived + list(_send_chip_sums(out_bf16)))
    final = {}
    final["conv_norm_g"] = small_sum[0]
    final["conv_w"] = lax.dynamic_slice_in_dim(small_sum[1:4], chip * q, q, axis=1)
    final["attn_norm_g"] = lax.dynamic_slice_in_dim(small_sum[4], chip * q, q, axis=0)
    final["attn_b_f"] = small_sum[5, :H]
    final["attn_q_norm_g"] = small_sum[5, H:H + HEAD_DIM]
    final["attn_k_norm_g"] = small_sum[5, H + HEAD_DIM:H + 2 * HEAD_DIM]
    loss = small_sum[5, H + 2 * HEAD_DIM]

    delta, new_m, new_v = {}, {}, {}
    core = jnp.reshape(lax.axis_index("c"), (1,)).astype(jnp.int32)
    for k, (mine, other) in zip(big_names, reduced):
        final[k], delta[k], new_m[k], new_v[k] = _adamw_halves(weights[k], mine, other, m_in[k], v_in[k], core,
                                                               "adamw_" + k)
    for k in names:
        if k in big_names:
            continue
        shape = weights[k].shape
        as2d = (lambda a: a.reshape(1, -1)) if len(shape) == 1 else (lambda a: a)
        d, m2, v2 = _adamw(as2d(weights[k]), as2d(final[k]), as2d(m_in[k]), as2d(v_in[k]), "adamw_" + k)
        delta[k], new_m[k], new_v[k] = d.reshape(shape), m2.reshape(shape), v2.reshape(shape)
    lead = lambda a: a[None]
    return (loss, grad_x[None], *[lead(final[k]) for k in names], *[lead(delta[k]) for k in names],
            *[lead(new_m[k]) for k in names], *[lead(new_v[k]) for k in names])
```

```python
import functools

import jax
import jax.numpy as jnp
from jax import lax
from jax.experimental import pallas as pl
from jax.experimental.pallas import tpu as pltpu

F32 = jnp.float32
BF16 = jnp.bfloat16
HEAD_DIM = 64
LANES = 128
RMS_EPS = 1e-6
NEG = -1e30
Q_SCALE = 0.125
ROW_TILE = 256
CONV_TILE = 512
ATT_GROUP = 4
SKIP_LOG = 106.0
PLAIN_EXP_MAX = 60.0
TN_ROWS = 2048
VMEM_LIMIT = 56 << 20
ADAM_LR, ADAM_B1, ADAM_B2, ADAM_EPS, ADAM_WD, ADAM_STEP = 0.001, 0.9, 0.999, 1e-08, 0.01, 10
MESH = pl.DeviceIdType.MESH
ANY = pl.BlockSpec(memory_space=pl.ANY)


def _lane():
    return lax.broadcasted_iota(jnp.int32, (1, LANES), 1)


def _split3(x):
    hi = x.astype(BF16).astype(F32)
    r = x - hi
    mid = r.astype(BF16).astype(F32)
    lo = (r - mid).astype(BF16).astype(F32)
    return hi, mid, lo


STAT_STRIDE = 16
ONE_LANE = 3 * STAT_STRIDE


def _pack3(x, lane, one):
    hi, mid, lo = _split3(x)
    packed = hi + pltpu.roll(mid, STAT_STRIDE, 1) + pltpu.roll(lo, 2 * STAT_STRIDE, 1)
    return jnp.where(lane == ONE_LANE, one, packed).astype(BF16)


def _scatter_matrices(H):
    rows = lax.broadcasted_iota(jnp.int32, (LANES, H * LANES), 0)
    cols = lax.broadcasted_iota(jnp.int32, (LANES, H * LANES), 1)
    head, within = cols // LANES, cols % LANES
    extra = within - _aug(head % 2)
    term = (rows < ONE_LANE) & (rows % STAT_STRIDE == head)
    first = ((term & (extra == rows // STAT_STRIDE)) | ((rows == ONE_LANE) & (extra >= 3) & (extra < 6)))
    second = ((term & (extra - 3 == rows // STAT_STRIDE)) | ((rows == ONE_LANE) & (extra >= 0) & (extra < 3)))
    return first.astype(BF16), second.astype(BF16)


def _col(x, lane, idx):
    return jnp.sum(jnp.where(lane == idx, x, 0.0), axis=1, keepdims=True)


def _feat(parity):
    return HEAD_DIM * parity


def _aug(parity):
    return HEAD_DIM * (1 - parity)


def _own(lane, parity):
    return (lane >= _feat(parity)) & (lane < _feat(parity) + HEAD_DIM)


def _head_tile(ref, hd, lane):
    j = hd // 2
    return jnp.where(_own(lane, hd % 2), ref[:, LANES * j:LANES * (j + 1)], 0.0)


def _pair_tile(even, odd, lane):
    return jnp.where(lane < HEAD_DIM, even, odd)


def _sigmoid(x):
    return 0.5 * jnp.tanh(0.5 * x) + 0.5


def _dot(a, b):
    return jnp.dot(a, b, preferred_element_type=F32)


def _dot_nt(a, b):
    return lax.dot_general(a, b, (((1,), (1,)), ((), ())), preferred_element_type=F32)


def _dot_tn(a, b):
    return lax.dot_general(a, b, (((0,), (0,)), ((), ())), preferred_element_type=F32)


def _dot01(tri, x):
    hi, mid, lo = _split3(x)
    return _dot(tri, hi.astype(BF16)) + _dot(tri, mid.astype(BF16)) + _dot(tri, lo.astype(BF16))


def _rms_bwd(dh, x, g):
    inv = lax.rsqrt(jnp.mean(x * x, axis=-1, keepdims=True) + RMS_EPS)
    xh = x * inv
    dxn = dh * g
    dx = inv * (dxn - xh * jnp.mean(dxn * xh, axis=-1, keepdims=True))
    return dx, jnp.sum(dh * xh, axis=0, keepdims=True)


def _head_rms_bwd(dn, t, g, ones):
    sq = t * t
    hi = sq.astype(BF16)
    lo = (sq - hi.astype(F32)).astype(BF16)
    inv = lax.rsqrt((_dot(hi, ones) + _dot(lo, ones)) * (1.0 / HEAD_DIM) + RMS_EPS)
    th = t * inv
    gd = dn * g
    d = inv * (gd - th * (jnp.sum(gd * th, axis=1, keepdims=True) * (1.0 / HEAD_DIM)))
    return d, jnp.sum(dn * th, axis=0, keepdims=True)


def _params(n_grid):
    return pltpu.CompilerParams(dimension_semantics=("arbitrary",) * n_grid, vmem_limit_bytes=VMEM_LIMIT)


def _rows(tm, cols, rev=None):
    if rev is None:
        return pl.BlockSpec((tm, cols), lambda i: (i, 0))
    return pl.BlockSpec((tm, cols), lambda i: (rev - i, 0))


def _whole(shape, buffers=None):
    mode = {} if buffers is None else dict(pipeline_mode=pl.Buffered(buffers))
    return pl.BlockSpec(shape, lambda *_: (0,) * len(shape), **mode)


RING_SLOTS = 3


def _ring_fetch(arrays, slots, sems, tm, steps, tile=lambda step: step):
    s = pl.program_id(0)

    def copy(k, step):
        slot = step % RING_SLOTS
        rows = pl.ds(pl.multiple_of(tile(step) * tm, tm), tm)
        return pltpu.make_async_copy(arrays[k].at[rows, :], slots[k].at[slot], sems.at[k, slot])

    @pl.when(s == 0)
    def _():
        for step in range(min(RING_SLOTS - 1, steps)):
            for k in range(len(arrays)):
                copy(k, step).start()

    @pl.when(s + RING_SLOTS - 1 < steps)
    def _():
        for k in range(len(arrays)):
            copy(k, s + RING_SLOTS - 1).start()

    for k in range(len(arrays)):
        copy(k, s).wait()
    return [slots[k].at[s % RING_SLOTS] for k in range(len(arrays))]


def _conv_fwd(x, g1, w_in, conv_w, w_out, later):
    S, D = x.shape
    tm = min(CONV_TILE, S)
    sub = min(ROW_TILE, tm)
    steps = S // tm
    n = len(later)

    def body(x_ref, g_ref, win_ref, cw_ref, wout_ref, *rest):
        shard_refs, rest = rest[:n], rest[n:]
        proj_ref, h_ref, yc_ref, y_ref, x1_ref = rest[:5]
        gathered_refs, rest = rest[5:5 + n], rest[5 + n:]
        prev_u = rest[0]
        if n:
            start, forward, finish = _gather_plan(shard_refs, gathered_refs, rest[1], rest[2])
            pl.when(pl.program_id(0) == 0)(start)
            pl.when(pl.program_id(0) == steps // 2)(forward)

        @pl.when(pl.program_id(0) == 0)
        def _():
            prev_u[...] = jnp.zeros((sub, D), F32)

        for r in range(0, tm, sub):
            rows = slice(r, r + sub)
            xv = x_ref[rows, :]
            inv = lax.rsqrt(jnp.mean(xv * xv, axis=-1, keepdims=True) + RMS_EPS)
            h = (xv * inv * g_ref[...]).astype(BF16)
            h_ref[rows, :] = h
            for j in range(4):
                proj_ref[rows, j * D:(j + 1) * D] = _dot(h, win_ref[j])
            u = proj_ref[rows, D:2 * D] * proj_ref[rows, 2 * D:3 * D]
            pu = prev_u[...]
            row = lax.broadcasted_iota(jnp.int32, (sub, 1), 0)
            u1 = jnp.where(row < 1, pltpu.roll(pu, 1, 0), pltpu.roll(u, 1, 0))
            u2 = jnp.where(row < 2, pltpu.roll(pu, 2, 0), pltpu.roll(u, 2, 0))
            prev_u[...] = u
            w = cw_ref[...]
            yc = w[2:3] * u + w[1:2] * u1 + w[0:1] * u2
            yc_ref[rows, :] = yc
            z = proj_ref[rows, 3 * D:4 * D]
            y = (proj_ref[rows, 0:D] * yc * (z * _sigmoid(z))).astype(BF16)
            y_ref[rows, :] = y
            x1_ref[rows, :] = xv + _dot(y, wout_ref[...])

        if n:
            pl.when(pl.program_id(0) == steps - 1)(finish)

    results = pl.pallas_call(
        body, name="conv_fwd", grid=(steps,),
        in_specs=[_rows(tm, D), _whole((1, D)), _whole((4, D, D), 1), _whole((3, D)), _whole((D, D), 1)] + [ANY] * n,
        out_specs=[_rows(tm, 4 * D), _rows(tm, D), _rows(tm, D), _rows(tm, D), _rows(tm, D)] + [ANY] * n,
        out_shape=[jax.ShapeDtypeStruct((S, 4 * D), F32), jax.ShapeDtypeStruct((S, D), BF16),
                   jax.ShapeDtypeStruct((S, D), F32), jax.ShapeDtypeStruct((S, D), BF16),
                   jax.ShapeDtypeStruct((S, D), F32)] + [jax.ShapeDtypeStruct((4,) + a.shape, a.dtype) for a in later],
        scratch_shapes=[pltpu.VMEM((sub, D), F32)] + [pltpu.SemaphoreType.DMA((n, 7))] * (2 if n else 0),
        compiler_params=_params(1),
    )(x, g1, w_in, conv_w, w_out, *later)
    return results[:5], results[5:]


def _conv_bwd(dproj2, wa, wf, x1, g2, dx2, x, g1, w_in, w_out, conv_w, proj, yc):
    S, D = x.shape
    tm = min(ROW_TILE, S)
    sub = min(ROW_TILE, tm)
    last = S // tm - 1

    def body(dp2_ref, wa_ref, wf_ref, x1_ref, g2_ref, dx2_ref, x_ref, g_ref, win_ref, wout_ref, cw_ref, proj_hbm, yc_hbm,
             dproj_ref, dx_ref, dx1b_ref, dg_ref, dcw_ref, dg2_ref, next_d, proj_slots, yc_slots, ring_sems):
        proj_ref, yc_ref = _ring_fetch([proj_hbm, yc_hbm], [proj_slots, yc_slots], ring_sems, tm, last + 1,
                                       lambda step: last - step)

        @pl.when(pl.program_id(0) == 0)
        def _():
            dg_ref[...] = jnp.zeros((1, D), F32)
            dcw_ref[...] = jnp.zeros((3, D), F32)
            dg2_ref[...] = jnp.zeros((1, D), F32)
            next_d[...] = jnp.zeros((sub, D), F32)

        for r in range(tm - sub, -1, -sub):
            rows = slice(r, r + sub)
            dh2 = _dot_nt(dp2_ref[rows, 0:4 * D], wa_ref[...]) + _dot_nt(dp2_ref[rows, 4 * D:4 * D + LANES], wf_ref[...])
            dxn2, dg2 = _rms_bwd(dh2, x1_ref[rows, :], g2_ref[...])
            dg2_ref[...] += dg2
            dx1v = dx2_ref[rows, :] + dxn2
            dx1b = dx1v.astype(BF16)
            dx1b_ref[rows, :] = dx1b
            dy = _dot_nt(dx1b, wout_ref[...])
            b = proj_ref[rows, 0:D]
            c = proj_ref[rows, D:2 * D]
            xin = proj_ref[rows, 2 * D:3 * D]
            z = proj_ref[rows, 3 * D:4 * D]
            sg = _sigmoid(z)
            sz = z * sg
            ycv = yc_ref[rows, :]
            d0 = dy * b * sz
            dproj_ref[rows, 0:D] = (dy * ycv * sz).astype(BF16)
            dproj_ref[rows, 3 * D:4 * D] = (dy * b * ycv * (sg * (1.0 + z * (1.0 - sg)))).astype(BF16)
            nd = next_d[...]
            row = lax.broadcasted_iota(jnp.int32, (sub, 1), 0)
            d1 = jnp.where(row >= sub - 1, pltpu.roll(nd, sub - 1, 0), pltpu.roll(d0, sub - 1, 0))
            d2 = jnp.where(row >= sub - 2, pltpu.roll(nd, sub - 2, 0), pltpu.roll(d0, sub - 2, 0))
            next_d[...] = d0
            w = cw_ref[...]
            du = w[2:3] * d0 + w[1:2] * d1 + w[0:1] * d2
            u = c * xin
            dcw_ref[2:3, :] += jnp.sum(d0 * u, axis=0, keepdims=True)
            dcw_ref[1:2, :] += jnp.sum(d1 * u, axis=0, keepdims=True)
            dcw_ref[0:1, :] += jnp.sum(d2 * u, axis=0, keepdims=True)
            dproj_ref[rows, D:2 * D] = (du * xin).astype(BF16)
            dproj_ref[rows, 2 * D:3 * D] = (du * c).astype(BF16)
            dh = _dot_nt(dproj_ref[rows, 0:D], win_ref[0])
            for j in range(1, 4):
                dh = dh + _dot_nt(dproj_ref[rows, j * D:(j + 1) * D], win_ref[j])
            dxn, dg = _rms_bwd(dh, x_ref[rows, :], g_ref[...])
            dx_ref[rows, :] = dx1v + dxn
            dg_ref[...] += dg

    return pl.pallas_call(
        body, name="conv_bwd", grid=(S // tm,),
        in_specs=[_rows(tm, 4 * D + LANES, last), _whole((D, 4 * D), 1), _whole((D, LANES), 1), _rows(tm, D, last),
                  _whole((1, D)), _rows(tm, D, last),
                  _rows(tm, D, last), _whole((1, D)), _whole((4, D, D), 1), _whole((D, D), 1),
                  _whole((3, D)), ANY, ANY],
        out_specs=[_rows(tm, 4 * D, last), _rows(tm, D, last), _rows(tm, D, last), _whole((1, D)), _whole((3, D)),
                   _whole((1, D))],
        out_shape=[jax.ShapeDtypeStruct((S, 4 * D), BF16), jax.ShapeDtypeStruct((S, D), F32),
                   jax.ShapeDtypeStruct((S, D), BF16), jax.ShapeDtypeStruct((1, D), F32),
                   jax.ShapeDtypeStruct((3, D), F32), jax.ShapeDtypeStruct((1, D), F32)],
        scratch_shapes=[pltpu.VMEM((sub, D), F32), pltpu.VMEM((RING_SLOTS, tm, 4 * D), proj.dtype),
                        pltpu.VMEM((RING_SLOTS, tm, D), yc.dtype), pltpu.SemaphoreType.DMA((2, RING_SLOTS))],
        compiler_params=_params(1),
    )(dproj2, wa, wf, x1, g2, dx2, x, g1, w_in, w_out, conv_w, proj, yc)


def _attn_front(x1, g2, w, wf, bf, gq, gk):
    S, D = x1.shape
    H = D // HEAD_DIM
    tm = min(ROW_TILE, S)
    tri = (lax.broadcasted_iota(jnp.int32, (tm, tm), 1) <= lax.broadcasted_iota(jnp.int32, (tm, tm), 0)).astype(BF16)

    def body(x_hbm, g_ref, w_ref, wf_ref, bf_ref, gq_ref, gk_ref, tri_ref, first_ref, second_ref,
             h_ref, qh_ref, kh_ref, z_ref, f_ref, ends_ref, rel_ref, qa_ref, ka_ref, va_ref, vt_ref,
             carry, v_s, qraw_ref, kraw_ref, x_slots, ring_sems):
        x_ref, = _ring_fetch([x_hbm], [x_slots], ring_sems, tm, S // tm)

        @pl.when(pl.program_id(0) == 0)
        def _():
            carry[...] = jnp.zeros((8, LANES), F32)

        xv = x_ref[...]
        inv = lax.rsqrt(jnp.mean(xv * xv, axis=-1, keepdims=True) + RMS_EPS)
        h = (xv * inv * g_ref[...]).astype(BF16)
        h_ref[...] = h
        qraw_ref[...] = _dot(h, w_ref[:, 0:D])
        kraw_ref[...] = _dot(h, w_ref[:, D:2 * D])
        v_s[...] = _dot(h, w_ref[:, 2 * D:3 * D])
        z_ref[...] = _dot(h, w_ref[:, 3 * D:4 * D])
        lane = _lane()
        f = _dot(h, wf_ref[...]) + bf_ref[...]
        f_ref[...] = f
        logf = jnp.where(lane < H, jnp.minimum(f, 0.0) - jnp.log(1.0 + jnp.exp(-jnp.abs(f))), 0.0)
        cs = _dot01(tri_ref[...], logf) + carry[0:1, :]
        carry[...] = jnp.broadcast_to(cs[tm - 1:tm, :], (8, LANES))
        diags = jnp.zeros((tm, LANES), F32)
        for hd in range(H):
            sl = slice(LANES * hd, LANES * (hd + 1))
            a = _aug(hd % 2)
            if hd % 2 == 0:
                qh_ref[hd // 2] = qraw_ref[:, LANES * (hd // 2):LANES * (hd // 2 + 1)]
                kh_ref[hd // 2] = kraw_ref[:, LANES * (hd // 2):LANES * (hd // 2 + 1)]
            qt = _head_tile(qraw_ref, hd, lane)
            qn = qt * lax.rsqrt(jnp.sum(qt * qt, axis=1, keepdims=True) * (1.0 / HEAD_DIM) + RMS_EPS) * gq_ref[...]
            kt = _head_tile(kraw_ref, hd, lane)
            kn = kt * lax.rsqrt(jnp.sum(kt * kt, axis=1, keepdims=True) * (1.0 / HEAD_DIM) + RMS_EPS) * gk_ref[...]
            diags = diags + jnp.where(lane == hd, jnp.sum(qn * kn, axis=1, keepdims=True) * Q_SCALE, 0.0)
            qa_ref[:, sl] = (qn * Q_SCALE).astype(BF16)
            ka_ref[:, sl] = kn.astype(BF16)
            va = jnp.where((lane >= a) & (lane < a + 3), 1.0, _head_tile(v_s, hd, lane))
            va_ref[:, sl] = va.astype(BF16)
            vt_ref[hd] = va.T.astype(BF16)
        rel = cs - diags
        rel_ref[...] = rel
        row = lax.broadcasted_iota(jnp.int32, (8, LANES), 0)
        ends_ref[...] = jnp.where(row == 0, cs[0:1, :],
                                  jnp.where(row == 1, cs[tm - 1:tm, :], jnp.min(diags, axis=0, keepdims=True)))
        qa_ref[...] += _dot(_pack3(rel, lane, 1.0), first_ref[...]).astype(BF16)
        ka_ref[...] += _dot(_pack3(-cs, lane, 1.0), second_ref[...]).astype(BF16)

    nb = S // tm
    heads = pl.BlockSpec((H // 2, tm, LANES), lambda i: (0, i, 0))
    return pl.pallas_call(
        body, name="attn_front", grid=(nb,),
        in_specs=[ANY, _whole((1, D)), _whole((D, 4 * D)), _whole((D, LANES)), _whole((1, LANES)),
                  _whole((1, LANES)), _whole((1, LANES)), _whole((tm, tm)), _whole((LANES, H * LANES)),
                  _whole((LANES, H * LANES))],
        out_specs=[_rows(tm, D), heads, heads, _rows(tm, D), _rows(tm, LANES),
                   pl.BlockSpec((None, 8, LANES), lambda i: (i, 0, 0)), _rows(tm, LANES),
                   _rows(tm, H * LANES), _rows(tm, H * LANES), _rows(tm, H * LANES),
                   pl.BlockSpec((H, None, LANES, tm), lambda i: (0, i, 0, 0))],
        out_shape=[jax.ShapeDtypeStruct((S, D), BF16), jax.ShapeDtypeStruct((H // 2, S, LANES), F32),
                   jax.ShapeDtypeStruct((H // 2, S, LANES), F32), jax.ShapeDtypeStruct((S, D), F32),
                   jax.ShapeDtypeStruct((S, LANES), F32), jax.ShapeDtypeStruct((nb, 8, LANES), F32),
                   jax.ShapeDtypeStruct((S, LANES), F32),
                   jax.ShapeDtypeStruct((S, H * LANES), BF16), jax.ShapeDtypeStruct((S, H * LANES), BF16),
                   jax.ShapeDtypeStruct((S, H * LANES), BF16), jax.ShapeDtypeStruct((H, nb, LANES, tm), BF16)],
        scratch_shapes=[pltpu.VMEM((8, LANES), F32), pltpu.VMEM((tm, D), F32), pltpu.VMEM((tm, D), F32),
                        pltpu.VMEM((tm, D), F32), pltpu.VMEM((RING_SLOTS, tm, D), F32),
                        pltpu.SemaphoreType.DMA((1, RING_SLOTS))],
        compiler_params=_params(1),
    )(x1, g2, w, wf, bf, gq, gk, tri, *_scatter_matrices(H))


def _skip_tables(first, last, lowest, gq, gk, G):
    nb = first.shape[0]
    bound = HEAD_DIM ** 0.5 * jnp.max(jnp.abs(gq)) * jnp.max(jnp.abs(gk))
    lowest = jnp.maximum(lowest, -bound)
    idx = jnp.arange(nb)
    margin = (SKIP_LOG + bound) - lowest
    need = (last[None, :, :] <= first[:, None, :] + margin[:, None, :]) & (idx[None, :, None] < idx[:, None, None])
    need = need | (idx[None, :, None] == idx[:, None, None])
    kstart = jnp.argmax(need, axis=1)
    qend = nb - 1 - jnp.argmax(need[::-1], axis=0)
    kstart = jnp.min(kstart.reshape(2 * nb // G, G // 2, -1), axis=1)
    kstart = kstart - (kstart & 1)
    qend = jnp.max(qend.reshape(2 * nb // G, G // 2, -1), axis=1)
    return kstart.T.astype(jnp.int32), qend.T.astype(jnp.int32), bound


def _attn_fwd(kstart, qa, ka, vt, online_max):
    S = qa.shape[0]
    H = qa.shape[1] // LANES
    nb, T = vt.shape[1], vt.shape[3]
    G = 2 * nb // kstart.shape[1]
    W = G * T

    def finish(acc, shift, o_ref, lse_ref):
        a = _aug(pl.program_id(0) % 2)
        feat = lax.broadcasted_iota(jnp.int32, (LANES, 1), 0)
        l = jnp.sum(jnp.where(feat == a, acc, 0.0), axis=0, keepdims=True)
        o_ref[...] = (acc * (1.0 / l)).T
        lse_ref[...] = shift + jnp.log(l)

    def causal(st):
        return jnp.where(lax.broadcasted_iota(jnp.int32, st.shape, 0) <= lax.broadcasted_iota(jnp.int32, st.shape, 1),
                         st, NEG)

    def fast_body(ks_ref, q_ref, k_ref, vt_ref, o_ref, lse_ref, acc_ref, sa_ref, sb_ref, sc_ref):
        h, g = pl.program_id(0), pl.program_id(1)
        q = q_ref[...]
        acc_ref[...] = jnp.zeros((LANES, W), F32)

        def scores(ki, lo):
            return _dot_nt(k_ref[pl.ds(pl.multiple_of(ki * T, T), 2 * T), :], q[lo * T:, :])

        def weighted(ki, p):
            return _dot(vt_ref[ki], p[:T]) + _dot(vt_ref[ki + 1], p[T:])

        first = ks_ref[h, 2 * g + 1]
        early = jnp.minimum(ks_ref[h, 2 * g], first)

        def narrow(i, carry):
            ki = early + 2 * i
            st = _dot_nt(k_ref[pl.ds(pl.multiple_of(ki * T, T), 2 * T), :], q[:W // 2, :])
            acc_ref[:, :W // 2] += weighted(ki, jnp.exp(st).astype(BF16))
            return carry

        lax.fori_loop(0, (first - early) // 2, narrow, 0)
        steps = (g * G - first) // 2
        sa_ref[...] = scores(first, 0)

        def advance(ki, cur_ref, next_ref):
            p = jnp.exp(cur_ref[...]).astype(BF16)
            next_ref[...] = scores(ki + 2, 0)
            acc_ref[...] += weighted(ki, p)

        def loop(i, carry):
            advance(first + 4 * i, sa_ref, sb_ref)
            advance(first + 4 * i + 2, sb_ref, sa_ref)
            return carry

        lax.fori_loop(0, steps // 2, loop, 0)

        def first_own(pending_ref):
            p = jnp.exp(causal(pending_ref[...])).astype(BF16)
            if G > 2:
                sc_ref[:, :W - 2 * T] = scores(g * G + 2, 2)
            acc_ref[...] += weighted(g * G, p)

        @pl.when(steps % 2 == 1)
        def _():
            advance(g * G - 2, sa_ref, sb_ref)
            first_own(sb_ref)

        @pl.when(steps % 2 == 0)
        def _():
            first_own(sa_ref)

        if G > 2:
            acc_ref[:, 2 * T:] += weighted(g * G + 2, jnp.exp(causal(sc_ref[:, :W - 2 * T])).astype(BF16))
        for j in range(4, G, 2):
            p = jnp.exp(causal(scores(g * G + j, j))).astype(BF16)
            acc_ref[:, j * T:] += weighted(g * G + j, p)
        finish(acc_ref[...], 0.0, o_ref, lse_ref)

    def online_body(ks_ref, q_ref, k_ref, vt_ref, o_ref, lse_ref, acc_ref, m_ref):
        h, g = pl.program_id(0), pl.program_id(1)
        q = q_ref[...]
        m_ref[...] = jnp.full((8, W), NEG, F32)
        acc_ref[...] = jnp.zeros((LANES, W), F32)

        def update(st, vtb, lo):
            m_old = m_ref[0:1, lo:]
            m_new = jnp.maximum(m_old, jnp.max(st, axis=0, keepdims=True))
            p = jnp.exp(st - m_new).astype(BF16)
            acc_ref[:, lo:] = acc_ref[:, lo:] * jnp.exp(m_old - m_new) + _dot(vtb, p)
            m_ref[:, lo:] = jnp.broadcast_to(m_new, (8, W - lo))

        def loop(ki, carry):
            kb = k_ref[pl.ds(pl.multiple_of(ki * T, T), T), :]
            update(_dot_nt(kb, q), vt_ref[ki], 0)
            return carry

        lax.fori_loop(jnp.minimum(ks_ref[h, 2 * g], ks_ref[h, 2 * g + 1]), g * G, loop, 0)
        for j in range(G):
            ki = g * G + j
            kb = k_ref[pl.ds(pl.multiple_of(ki * T, T), T), :]
            update(causal(_dot_nt(kb, q[j * T:, :])), vt_ref[ki], j * T)
        finish(acc_ref[...], m_ref[0:1, :], o_ref, lse_ref)

    return pl.pallas_call(
        online_body if online_max else fast_body, name="attn_fwd_online" if online_max else "attn_fwd",
        grid_spec=pltpu.PrefetchScalarGridSpec(
            num_scalar_prefetch=1, grid=(H, nb // G),
            in_specs=[pl.BlockSpec((W, LANES), lambda h, i, ks: (i, h)),
                      pl.BlockSpec((S, LANES), lambda h, i, ks: (0, h)),
                      pl.BlockSpec((None, nb, LANES, T), lambda h, i, ks: (h, 0, 0, 0))],
            out_specs=[pl.BlockSpec((W, LANES), lambda h, i, ks: (i, h)),
                       pl.BlockSpec((None, 1, W), lambda h, i, ks: (h, 0, i))],
            scratch_shapes=[pltpu.VMEM((LANES, W), F32)] + (
                [pltpu.VMEM((8, W), F32)] if online_max else [pltpu.VMEM((2 * T, W), F32)] * 3)),
        out_shape=[jax.ShapeDtypeStruct((S, H * LANES), F32), jax.ShapeDtypeStruct((H, 1, S), F32)],
        compiler_params=_params(2),
    )(kstart, qa, ka, vt)


def _attn_out(o_aug, lse, rel, z, x1, target, w_out, qa):
    S, D = x1.shape
    H = D // HEAD_DIM
    tm = min(ROW_TILE, S)
    steps = S // tm
    ringed = [o_aug, z, x1, target, qa]

    def body(*refs):
        n = len(ringed)
        w_ref, first_ref, rel_ref, lse_ref = refs[n:n + 4]
        dx2_ref, dx2b_ref, o2b_ref, dz_ref, doa_ref, qa2_ref, loss_ref, oc_s, do_s = refs[n + 4:n + 13]
        o_ref, z_ref, x1_ref, t_ref, q_ref = _ring_fetch(refs[:n], refs[n + 13:2 * n + 13], refs[2 * n + 13], tm, steps)

        @pl.when(pl.program_id(0) == 0)
        def _():
            loss_ref[...] = jnp.zeros((1, LANES), F32)

        lane = _lane()
        for j in range(H // 2):
            oc_s[:, LANES * j:LANES * (j + 1)] = _pair_tile(o_ref[:, 2 * LANES * j:2 * LANES * j + LANES],
                                                            o_ref[:, 2 * LANES * j + LANES:2 * LANES * (j + 1)], lane)
        oc = oc_s[...]
        zv = z_ref[...]
        sg = _sigmoid(zv)
        sz = zv * sg
        o2 = (oc * sz).astype(BF16)
        o2b_ref[...] = o2
        e = x1_ref[...] + _dot(o2, w_ref[...]) - t_ref[...]
        sq = jnp.sum(jnp.sum(e * e, axis=1, keepdims=True), axis=0, keepdims=True)
        loss_ref[...] += jnp.broadcast_to(sq * (0.5 / D), (1, LANES))
        dx2 = e * (1.0 / D)
        dx2_ref[...] = dx2
        dx2b = dx2.astype(BF16)
        dx2b_ref[...] = dx2b
        do2 = _dot_nt(dx2b, w_ref[...])
        dz_ref[...] = (do2 * oc * (sg * (1.0 + zv * (1.0 - sg)))).astype(BF16)
        do_s[...] = do2 * sz
        deltas = jnp.zeros((tm, LANES), F32)
        for hd in range(H):
            dt = _head_tile(do_s, hd, lane)
            delta = jnp.sum(dt * _head_tile(oc_s, hd, lane), axis=1, keepdims=True)
            deltas = deltas + jnp.where(lane == hd, delta, 0.0)
            doa_ref[:, LANES * hd:LANES * (hd + 1)] = dt.astype(BF16)
        doa_ref[...] += _dot(_pack3(-deltas, lane, 0.0), first_ref[...]).astype(BF16)
        lse = jnp.concatenate([lse_ref[...], jnp.zeros((LANES - H, tm), F32)], axis=0).T
        rq = rel_ref[...] - lse
        tile_lane = lax.broadcasted_iota(jnp.int32, (1, H * LANES), 1)
        extra = tile_lane % LANES - _aug((tile_lane // LANES) % 2)
        kept = jnp.where((extra >= 0) & (extra < 3), jnp.zeros((), BF16), q_ref[...])
        qa2_ref[...] = kept + _dot(_pack3(rq, lane, 0.0), first_ref[...]).astype(BF16)

    return pl.pallas_call(
        body, name="attn_out", grid=(steps,),
        in_specs=[ANY] * len(ringed) + [_whole((D, D)), _whole((LANES, H * LANES)), _rows(tm, LANES),
                                        pl.BlockSpec((H, tm), lambda i: (0, i))],
        out_specs=[_rows(tm, D), _rows(tm, D), _rows(tm, D), _rows(tm, D), _rows(tm, H * LANES),
                   _rows(tm, H * LANES), _whole((1, LANES))],
        out_shape=[jax.ShapeDtypeStruct((S, D), F32), jax.ShapeDtypeStruct((S, D), BF16),
                   jax.ShapeDtypeStruct((S, D), BF16), jax.ShapeDtypeStruct((S, D), BF16),
                   jax.ShapeDtypeStruct((S, H * LANES), BF16), jax.ShapeDtypeStruct((S, H * LANES), BF16),
                   jax.ShapeDtypeStruct((1, LANES), F32)],
        scratch_shapes=[pltpu.VMEM((tm, D), F32), pltpu.VMEM((tm, D), F32)]
        + [pltpu.VMEM((RING_SLOTS, tm, a.shape[1]), a.dtype) for a in ringed]
        + [pltpu.SemaphoreType.DMA((len(ringed), RING_SLOTS))],
        compiler_params=_params(1),
    )(*ringed, w_out, _scatter_matrices(H)[0], rel, lse)


def _attn_bwd(qend, qa2, doa, ka, va, T):
    S = qa2.shape[0]
    H = qa2.shape[1] // LANES
    nb = S // T
    G = 2 * nb // qend.shape[1]
    W = G * T

    def body(qe_ref, q_ref, do_ref, k_ref, v_ref, dq_ref, dk_ref, dv_ref, dkt_acc, dvt_acc):
        h, g = pl.program_id(0), pl.program_id(1)

        @pl.when(g == 0)
        def _():
            dq_ref[...] = jnp.zeros((S, LANES), F32)

        kb = k_ref[...]
        vb = v_ref[...]
        dkt_acc[...] = jnp.zeros((LANES, W), F32)
        dvt_acc[...] = jnp.zeros((LANES, W), F32)

        def step(qi, c0, c1, masked):
            rows = pl.ds(pl.multiple_of(qi * T, T), 2 * T)
            qb = q_ref[rows, :]
            dob = do_ref[rows, :]
            s = _dot_nt(qb, kb[c0:c1])
            if masked:
                query = lax.broadcasted_iota(jnp.int32, s.shape, 0) + (c1 - 2 * T)
                s = jnp.where(lax.broadcasted_iota(jnp.int32, s.shape, 1) <= query, s, NEG)
            p = jnp.exp(s)
            ds = (p * _dot_nt(dob, vb[c0:c1])).astype(BF16)
            dvt_acc[:, c0:c1] += _dot(dob.astype(F32).T.astype(BF16), p.astype(BF16))
            dkt_acc[:, c0:c1] += _dot(qb.astype(F32).T.astype(BF16), ds)
            dq_ref[rows, :] += _dot(ds, kb[c0:c1])

        for m in range(G // 2):
            step(g * G + 2 * m, 0, (m + 1) * 2 * T, True)
        first = g * G + G
        n_all = jnp.maximum((qe_ref[h, 2 * g] - first + 2) // 2, 0)
        second = first + 2 * n_all

        def all_keys(i, carry):
            step(first + 2 * i, 0, W, False)
            return carry

        def late_keys(i, carry):
            step(second + 2 * i, W // 2, W, False)
            return carry

        lax.fori_loop(0, n_all, all_keys, 0)
        lax.fori_loop(0, (qe_ref[h, 2 * g + 1] - second + 2) // 2, late_keys, 0)
        dk_ref[...] = dkt_acc[...].T
        dv_ref[...] = dvt_acc[...].T.astype(BF16)

    heads = pl.BlockSpec((None, W, LANES), lambda h, i, qe: (h, i, 0))
    return pl.pallas_call(
        body, name="attn_bwd",
        grid_spec=pltpu.PrefetchScalarGridSpec(
            num_scalar_prefetch=1, grid=(H, nb // G),
            in_specs=[pl.BlockSpec((S, LANES), lambda h, i, qe: (0, h)), pl.BlockSpec((S, LANES), lambda h, i, qe: (0, h)),
                      pl.BlockSpec((W, LANES), lambda h, i, qe: (i, h)), pl.BlockSpec((W, LANES), lambda h, i, qe: (i, h))],
            out_specs=[pl.BlockSpec((None, S, LANES), lambda h, i, qe: (h, 0, 0)), heads, heads],
            scratch_shapes=[pltpu.VMEM((LANES, W), F32), pltpu.VMEM((LANES, W), F32)]),
        out_shape=[jax.ShapeDtypeStruct((H, S, LANES), F32), jax.ShapeDtypeStruct((H, S, LANES), F32),
                   jax.ShapeDtypeStruct((H, S, LANES), BF16)],
        compiler_params=_params(2),
    )(qend, qa2, doa, ka, va)


def _attn_proj_bwd(dqt, dka, dva, qraw, kraw, dz, f, gq, gk):
    S, D = dz.shape
    H = D // HEAD_DIM
    tm = min(CONV_TILE, S)
    last = S // tm - 1
    tri = (lax.broadcasted_iota(jnp.int32, (tm, tm), 1) >= lax.broadcasted_iota(jnp.int32, (tm, tm), 0)).astype(BF16)

    def body(dq_ref, dk_ref, dv_ref, q_ref, k_ref, dz_ref, f_ref, gq_ref, gk_ref, tri_ref, ones_ref,
             dproj_ref, small_ref, carry, pairs):
        @pl.when(pl.program_id(0) == 0)
        def _():
            small_ref[...] = jnp.zeros((8, LANES), F32)
            carry[...] = jnp.zeros((8, LANES), F32)

        lane = _lane()

        def head_pair(j, acc):
            dcs, dgq, dgk = acc
            dq2, dk2 = [], []
            q_pair, k_pair = q_ref[j], k_ref[j]
            for parity in (0, 1):
                hd = 2 * j + parity
                own, a = _own(lane, parity), _aug(parity)
                dqf = dq_ref[hd]
                dqn = jnp.where(own, dqf * Q_SCALE, 0.0)
                d, dg = _head_rms_bwd(dqn, jnp.where(own, q_pair, 0.0), gq_ref[...], ones_ref[...])
                dq2.append(d)
                dgq = dgq + dg
                dkt = dk_ref[hd]
                dcs = dcs + jnp.where(lane == hd, _col(dqf, lane, a) - _col(dkt, lane, a + 3), 0.0)
                d, dg = _head_rms_bwd(jnp.where(own, dkt, 0.0), jnp.where(own, k_pair, 0.0), gk_ref[...], ones_ref[...])
                dk2.append(d)
                dgk = dgk + dg
            pairs[0, j] = _pair_tile(*dq2, lane).astype(BF16)
            pairs[1, j] = _pair_tile(*dk2, lane).astype(BF16)
            pairs[2, j] = _pair_tile(dv_ref[2 * j], dv_ref[2 * j + 1], lane)
            return dcs, dgq, dgk

        zero = jnp.zeros((1, LANES), F32)
        dcs, dgq, dgk = lax.fori_loop(0, H // 2, head_pair, (jnp.zeros((tm, LANES), F32), zero, zero))
        for part in range(3):
            for j in range(H // 2):
                dproj_ref[:, part * D + LANES * j:part * D + LANES * (j + 1)] = pairs[part, j]
        dproj_ref[:, 3 * D:4 * D] = dz_ref[...]
        dlogf = _dot01(tri_ref[...], dcs) + carry[0:1, :]
        carry[...] = jnp.broadcast_to(dlogf[0:1, :], (8, LANES))
        df = dlogf * (1.0 / (1.0 + jnp.exp(f_ref[...])))
        dproj_ref[:, 4 * D:4 * D + LANES] = df.astype(BF16)
        small_ref[0:1, :] += jnp.sum(df, axis=0, keepdims=True)
        small_ref[1:2, :] += dgq
        small_ref[2:3, :] += dgk

    W = 4 * D + LANES
    heads = pl.BlockSpec((H, tm, LANES), lambda i: (0, last - i, 0))
    head_pairs = pl.BlockSpec((H // 2, tm, LANES), lambda i: (0, last - i, 0))
    return pl.pallas_call(
        body, name="attn_proj_bwd", grid=(S // tm,),
        in_specs=[heads, heads, heads, head_pairs, head_pairs,
                  _rows(tm, D, last), _rows(tm, LANES, last), _whole((1, LANES)), _whole((1, LANES)),
                  _whole((tm, tm)), _whole((LANES, LANES))],
        out_specs=[_rows(tm, W, last), _whole((8, LANES))],
        out_shape=[jax.ShapeDtypeStruct((S, W), BF16), jax.ShapeDtypeStruct((8, LANES), F32)],
        scratch_shapes=[pltpu.VMEM((8, LANES), F32), pltpu.VMEM((3, H // 2, tm, LANES), BF16)],
        compiler_params=_params(1),
    )(dqt, dka, dva, qraw, kraw, dz, f, gq, gk, tri, jnp.ones((LANES, LANES), BF16))


def _matmul_tn(a, b, col0, n, tn, name, stacked=False, carried=None):
    S, M = a.shape
    ts = min(TN_ROWS, S)
    off = col0 // tn
    grid = (n // tn, S // ts)
    sent, lands, build = carried or ([], [], None)
    k = len(sent)

    def body(a_ref, b_ref, *rest):
        o_ref = rest[k]
        j, s = pl.program_id(0), pl.program_id(1)
        if k:
            copies = build(rest[:k], rest[k + 1:2 * k + 1], rest[2 * k + 1], rest[2 * k + 2])

            @pl.when((j == 0) & (s == 0))
            def _():
                for cp in copies:
                    cp.start()

        @pl.when(s == 0)
        def _():
            o_ref[...] = jnp.zeros((M, tn), F32)

        o_ref[...] += _dot_tn(a_ref[...], b_ref[...])

        if k:
            @pl.when((j == grid[0] - 1) & (s == grid[1] - 1))
            def _():
                for cp in copies:
                    cp.wait()

    if stacked:
        out_spec, out_shape = pl.BlockSpec((None, M, tn), lambda j, s: (j, 0, 0)), (n // tn, M, tn)
    else:
        out_spec, out_shape = pl.BlockSpec((M, tn), lambda j, s: (0, j)), (M, n)
    results = pl.pallas_call(
        body, name=name, grid=grid,
        in_specs=[pl.BlockSpec((ts, M), lambda j, s: (s, 0)), pl.BlockSpec((ts, tn), lambda j, s: (s, off + j))]
        + [ANY] * k,
        out_specs=[out_spec] + [ANY] * k,
        out_shape=[jax.ShapeDtypeStruct(out_shape, F32)] + list(lands),
        scratch_shapes=[pltpu.SemaphoreType.DMA((k, 4))] * (2 if k else 0),
        compiler_params=_params(2),
    )(a, b, *sent)
    return results[0] if carried is None else (results[0], list(results[1:]))


def _adam_update(gv, w_ref, m_ref, v_ref, d_ref, m2_ref, v2_ref):
    m2 = ADAM_B1 * m_ref[...] + (1.0 - ADAM_B1) * gv
    v2 = ADAM_B2 * v_ref[...] + (1.0 - ADAM_B2) * (gv * gv)
    m2_ref[...] = m2
    v2_ref[...] = v2
    m_hat = m2 / (1.0 - ADAM_B1 ** ADAM_STEP)
    v_hat = v2 / (1.0 - ADAM_B2 ** ADAM_STEP)
    d_ref[...] = -ADAM_LR * (m_hat / (jnp.sqrt(v_hat) + ADAM_EPS) + ADAM_WD * w_ref[...])


def _adamw(w, g, m, v, name):
    r, c = w.shape
    tr = ROW_TILE if r % ROW_TILE == 0 else r

    def body(w_ref, g_ref, m_ref, v_ref, d_ref, m2_ref, v2_ref):
        _adam_update(g_ref[...], w_ref, m_ref, v_ref, d_ref, m2_ref, v2_ref)

    spec = _rows(tr, c)
    return pl.pallas_call(
        body, name=name, grid=(r // tr,), in_specs=[spec] * 4, out_specs=[spec] * 3,
        out_shape=[jax.ShapeDtypeStruct((r, c), F32)] * 3, compiler_params=_params(1),
    )(w, g, m, v)


def _adamw_halves(w, mine, other, m, v, core, name):
    r, c = mine.shape
    tr = ROW_TILE if r % ROW_TILE == 0 else r
    per = r // tr

    def body(core_ref, w_ref, mine_ref, other_ref, m_ref, v_ref, g_ref, d_ref, m2_ref, v2_ref):
        gv = jnp.where(pl.program_id(0) // per == core_ref[0], mine_ref[...], other_ref[...])
        g_ref[...] = gv
        _adam_update(gv, w_ref, m_ref, v_ref, d_ref, m2_ref, v2_ref)

    full = pl.BlockSpec((tr, c), lambda i, core: (i, 0))
    half = pl.BlockSpec((tr, c), lambda i, core: (i % per, 0))
    return pl.pallas_call(
        body, name=name,
        grid_spec=pltpu.PrefetchScalarGridSpec(num_scalar_prefetch=1, grid=(2 * per,),
                                               in_specs=[full, half, half, full, full], out_specs=[full] * 4),
        out_shape=[jax.ShapeDtypeStruct((2 * r, c), F32)] * 4, compiler_params=_params(1),
    )(core, w, mine, other, m, v)


def _after_conv(conv_acts, x, target, g1, w_in, conv_w, w_out, g2, wa_in, b_f, gq, gk, wa_out, reduce_early=None):
    reduces = reduce_early is not None
    if not reduces:
        reduce_early = lambda by_chip, tag, swapped: ([], [])
    S, D = x.shape
    H = D // HEAD_DIM
    ws = wa_in.shape[2]
    w_qkvz = jnp.concatenate([wa_in[0], wa_in[1], wa_in[2], wa_in[3][:, :4 * D - 3 * ws]], axis=1)
    wf = jnp.pad(wa_in[3][:, 4 * D - 3 * ws:], ((0, 0), (0, LANES - H)))
    bf = jnp.pad(b_f, ((0, 0), (0, LANES - H)))
    gq128 = jnp.concatenate([gq, gq], axis=1)
    gk128 = jnp.concatenate([gk, gk], axis=1)

    proj, h1, yc, y, x1 = conv_acts
    h2, qraw, kraw, z, f, ends, rel, qa, ka, va, vt = _attn_front(x1, g2, w_qkvz, wf, bf, gq128, gk128)
    T = vt.shape[3]
    kstart, qend, bound = _skip_tables(ends[:, 0, :H], ends[:, 1, :H], ends[:, 2, :H], gq, gk, min(ATT_GROUP, S // T))
    o_aug, lse = lax.cond(2.0 * bound <= PLAIN_EXP_MAX, functools.partial(_attn_fwd, online_max=False),
                          functools.partial(_attn_fwd, online_max=True), kstart, qa, ka, vt)
    dx2, dx2b, o2b, dz, doa, qa2, loss = _attn_out(o_aug, lse.reshape(H, S), rel, z, x1, target, wa_out, qa)
    dqt, dka, dva = _attn_bwd(qend, qa2, doa, ka, va, T)
    dproj2, small = _attn_proj_bwd(dqt, dka, dva, qraw, kraw, dz, f, gq128, gk128)

    tn = min(1024, D)
    carry = (lambda kind, arrays: kind(arrays if reduces else []))
    dw_main = _matmul_tn(h2, dproj2, 0, 4 * D, 2 * tn, "dw_attn_in")
    dw_f = _matmul_tn(h2, dproj2, 4 * D, LANES, LANES, "dw_attn_f")
    dwa_in = jnp.stack([dw_main[:, 0:ws], dw_main[:, ws:2 * ws], dw_main[:, 2 * ws:3 * ws],
                        jnp.concatenate([dw_main[:, 3 * ws:], dw_f[:, :H]], axis=1)])
    dwa_out, swapped = _matmul_tn(o2b, dx2b, 0, D, tn, "dw_attn_out",
                                  carried=carry(_carried_core_swap, [dwa_in.reshape(4, 2, D // 2, ws)]))
    attn_f32, attn_bf16 = reduce_early([dwa_in, dwa_out.reshape(4, D // 4, D)], "attn", swapped)
    dproj1, dx, dx1b, dg1, dcw, dg2 = _conv_bwd(dproj2, w_qkvz, wf, x1, g2, dx2, x, g1, w_in, w_out, conv_w, proj, yc)
    dw_in, attn_arrived = _matmul_tn(h1, dproj1, 0, 4 * D, D, "dw_conv_in", stacked=True,
                                     carried=carry(_carried_chip_sums, attn_bf16))
    in_f32, in_bf16 = reduce_early([dw_in], "conv_in", [])
    dw_out, in_arrived = _matmul_tn(y, dx1b, 0, D, tn, "dw_conv_out", carried=carry(_carried_chip_sums, in_bf16))
    grads = dict(conv_norm_g=dg1, conv_w_in=dw_in, conv_w=dcw, conv_w_out=dw_out, attn_norm_g=dg2,
                 attn_w_in=dwa_in, attn_b_f=small[0:1, :H],
                 attn_q_norm_g=small[1:2, :HEAD_DIM] + small[1:2, HEAD_DIM:],
                 attn_k_norm_g=small[2:3, :HEAD_DIM] + small[2:3, HEAD_DIM:], attn_w_out=dwa_out)
    return loss[0, 0], dx, grads, (attn_f32 + in_f32, attn_arrived + in_arrived)


def _coords():
    return lax.axis_index("x"), lax.axis_index("y"), lax.axis_index("c")


def _at(ref, idx):
    return ref.at[idx] if idx else ref


def _other_chips(x, y):
    return [(1 - x, y), (x, 1 - y), (1 - x, 1 - y)]


def _gather_plan(src, out, send, recv):
    x, y, c = _coords()
    mine = 2 * x + y
    sibling = (x, y, 1 - c)
    others = [(a, k, 2 * px + py, (px, py)) for a in range(len(src)) for k, (px, py) in enumerate(_other_chips(x, y))]

    def copy(a, k, chip, half, to, source=None):
        dst = out[a].at[chip, half]
        return pltpu.make_async_remote_copy(src_ref=dst if source is None else source, dst_ref=dst,
                                            send_sem=send.at[a, k], recv_sem=recv.at[a, k],
                                            device_id=to, device_id_type=MESH)

    first = [copy(a, k, mine, c, (*chip, c), source=src[a].at[c]) for a, k, _, chip in others]
    own = [pltpu.make_async_remote_copy(src_ref=src[a], dst_ref=out[a].at[mine], send_sem=send.at[a, 6],
                                        recv_sem=recv.at[a, 6], device_id=sibling, device_id_type=MESH)
           for a in range(len(src))]
    passed = [copy(a, 3 + k, slot, c, sibling) for a, k, slot, _ in others]

    def start():
        for cp in first + own:
            cp.start()

    def forward():
        for (a, k, slot, _), cp in zip(others, passed):
            copy(a, k, slot, c, (x, y, c)).wait_recv()
            cp.start()

    def finish():
        for a, k, slot, _ in others:
            copy(a, 3 + k, slot, 1 - c, (x, y, c)).wait_recv()
        for cp in own:
            cp.wait_recv()
        for cp in first + passed + own:
            cp.wait_send()

    return start, forward, finish


def _all_gather(halved, whole):
    nh, nw = len(halved), len(whole)

    def body(*refs):
        src_h, src_w = refs[:nh], refs[nh:nh + nw]
        out_h, out_w = refs[nh + nw:2 * nh + nw], refs[2 * nh + nw:2 * (nh + nw)]
        send_h, recv_h, send_w, recv_w = refs[2 * (nh + nw):]
        x, y, c = _coords()
        mine = 2 * x + y
        chips = _other_chips(x, y)

        def copy_w(a, k, chip, to):
            return pltpu.make_async_remote_copy(src_ref=src_w[a], dst_ref=out_w[a].at[chip],
                                                send_sem=send_w.at[a, k], recv_sem=recv_w.at[a, k],
                                                device_id=to, device_id_type=MESH)

        start, forward, finish = _gather_plan(src_h, out_h, send_h, recv_h)
        small = [copy_w(a, k, mine, (*chip, c)) for a in range(nw) for k, chip in enumerate(chips)]
        small += [copy_w(a, 3, mine, (x, y, 1 - c)) for a in range(nw)]
        start()
        for cp in small:
            cp.start()
        forward()
        finish()
        for a in range(nw):
            for k, (px, py) in enumerate(chips):
                copy_w(a, k, 2 * px + py, (x, y, c)).wait_recv()
            copy_w(a, 3, mine, (x, y, c)).wait_recv()
        for cp in small:
            cp.wait_send()

    out_shape = [jax.ShapeDtypeStruct((4,) + a.shape, a.dtype) for a in list(halved) + list(whole)]
    return pl.pallas_call(
        body, name="gather_weights", in_specs=[ANY] * (nh + nw), out_specs=[ANY] * (nh + nw), out_shape=out_shape,
        scratch_shapes=[pltpu.SemaphoreType.DMA((nh, 7)), pltpu.SemaphoreType.DMA((nh, 7)),
                        pltpu.SemaphoreType.DMA((nw, 4)), pltpu.SemaphoreType.DMA((nw, 4))],
    )(*halved, *whole)


def _exchange(name, srcs, lands, copies, local_copies):
    ns, nl, n, nloc = len(srcs), len(lands), len(copies), len(local_copies)

    def body(*refs):
        src, land = refs[:ns], refs[ns:ns + nl]
        send, recv, local_sem = refs[ns + nl:]
        me = _coords()
        started = []
        for k, (si, s_at, li, l_at, ci) in enumerate(local_copies):
            cp = pltpu.make_async_copy(_at(src[si], s_at(*me)), _at(land[li], l_at(*me)), local_sem.at[k])
            cp.start()
            started.append(cp)
        remote = []
        for k, (si, s_at, li, l_at, peer) in enumerate(copies):
            cp = pltpu.make_async_remote_copy(src_ref=_at(src[si], s_at(*me)), dst_ref=_at(land[li], l_at(*me)),
                                              send_sem=send.at[k], recv_sem=recv.at[k],
                                              device_id=peer(*me), device_id_type=MESH)
            cp.start()
            remote.append(cp)
        for cp in remote:
            cp.wait()
        for cp in started:
            cp.wait()

    return pl.pallas_call(
        body, name=name, in_specs=[ANY] * ns, out_specs=[ANY] * nl, out_shape=list(lands),
        scratch_shapes=[pltpu.SemaphoreType.DMA((n,)), pltpu.SemaphoreType.DMA((n,)),
                        pltpu.SemaphoreType.DMA((max(nloc, 1),))],
    )(*srcs)


def _add_pairs(a, b, core, name):
    _, r, cols = b.shape
    tr = ROW_TILE if r % ROW_TILE == 0 else r

    def body(core_ref, a_ref, b_ref, o_ref, ob_ref):
        s = a_ref[...] + b_ref[...]
        o_ref[...] = s
        ob_ref[...] = s.astype(BF16)

    spec = pl.BlockSpec((None, tr, cols), lambda j, i, core: (j, i, 0))
    return pl.pallas_call(
        body, name=name,
        grid_spec=pltpu.PrefetchScalarGridSpec(
            num_scalar_prefetch=1, grid=(4, r // tr),
            in_specs=[pl.BlockSpec((None, None, tr, cols), lambda j, i, core: (j, core[0], i, 0)), spec],
            out_specs=[spec, spec]),
        out_shape=[jax.ShapeDtypeStruct(b.shape, F32), jax.ShapeDtypeStruct(b.shape, BF16)],
        compiler_params=_params(2),
    )(core, a, b)


def _sum_chips(own, landed, name):
    _, r, cols = landed.shape
    tr = ROW_TILE if r % ROW_TILE == 0 else r

    def body(own_ref, land_ref, o_ref):
        acc = own_ref[...]
        for j in range(3):
            acc = acc + land_ref[j].astype(F32)
        o_ref[...] = acc

    return pl.pallas_call(
        body, name=name, grid=(r // tr,),
        in_specs=[_rows(tr, cols), pl.BlockSpec((3, tr, cols), lambda i: (0, i, 0))], out_specs=_rows(tr, cols),
        out_shape=jax.ShapeDtypeStruct((r, cols), F32), compiler_params=_params(1),
    )(own, landed)


def _sum_devices(landed, name):
    def body(l_ref, o_ref):
        acc = l_ref[0]
        for j in range(1, 8):
            acc = acc + l_ref[j]
        o_ref[...] = acc

    return pl.pallas_call(body, name=name, out_shape=jax.ShapeDtypeStruct(landed.shape[1:], F32))(landed)


CHIP_FLIPS = [(1, 0), (0, 1), (1, 1)]


def _flip(fx, fy, fc):
    return lambda x, y, c: (x ^ fx, y ^ fy, c ^ fc)


def _core_swap_copies(big, landed, send, recv):
    x, y, c = _coords()
    return [pltpu.make_async_remote_copy(src_ref=big[a].at[j, 1 - c], dst_ref=landed[a].at[j],
                                         send_sem=send.at[a, j], recv_sem=recv.at[a, j],
                                         device_id=(x, y, 1 - c), device_id_type=MESH)
            for a in range(len(big)) for j in range(4)]


def _carried_core_swap(big):
    return big, [jax.ShapeDtypeStruct((4,) + g.shape[2:], F32) for g in big], _core_swap_copies


def _carried_chip_sums(sums):
    return sums, [jax.ShapeDtypeStruct((3,) + g.shape[1:], g.dtype) for g in sums], _chip_sum_copies


def _sum_cores(big, small, tag, swapped=()):
    c = lax.axis_index("c")
    rest = big[len(swapped):]
    nb = len(rest)
    copies = [(a, (lambda j: lambda x, y, c: (j, 1 - c))(j), a, (lambda j: lambda x, y, c: (j,))(j), _flip(0, 0, 1))
              for a in range(nb) for j in range(4)]
    lands = [jax.ShapeDtypeStruct((4,) + g.shape[2:], F32) for g in rest]
    srcs, local = list(rest), []
    if small is not None:
        flips = [(fx, fy, fc) for fx in (0, 1) for fy in (0, 1) for fc in (0, 1) if fx or fy or fc]
        copies += [(nb, lambda x, y, c: (), nb, lambda x, y, c: (4 * x + 2 * y + c,), _flip(*f)) for f in flips]
        lands.append(jax.ShapeDtypeStruct((8,) + small.shape, F32))
        local = [(nb, lambda x, y, c: (), nb, lambda x, y, c: (4 * x + 2 * y + c,), None)]
        srcs.append(small)
    landed = _exchange("swap_halves_" + tag, srcs, lands, copies, local)
    small_sum = None if small is None else _sum_devices(landed[nb], "sum_small")
    landed = list(swapped) + list(landed[:nb])
    nb = len(big)
    core = jnp.reshape(c, (1,)).astype(jnp.int32)
    sums = [_add_pairs(big[a], landed[a], core, f"add_cores_{tag}_{a}") for a in range(nb)]
    return [s for s, _ in sums], [sb for _, sb in sums], small_sum


def _chip_sum_copies(sums, landed, send, recv):
    x, y, c = _coords()
    return [pltpu.make_async_remote_copy(src_ref=sums[a].at[2 * (x ^ fx) + (y ^ fy)], dst_ref=landed[a].at[k],
                                         send_sem=send.at[a, k], recv_sem=recv.at[a, k],
                                         device_id=(x ^ fx, y ^ fy, c), device_id_type=MESH)
            for a in range(len(sums)) for k, (fx, fy) in enumerate(CHIP_FLIPS)]


def _send_chip_sums(chip_bf16):
    n = len(chip_bf16)

    def body(*refs):
        copies = _chip_sum_copies(refs[:n], refs[n:2 * n], refs[2 * n], refs[2 * n + 1])
        for cp in copies:
            cp.start()
        for cp in copies:
            cp.wait()

    return pl.pallas_call(
        body, name="send_chip_sums", in_specs=[ANY] * n, out_specs=[ANY] * n,
        out_shape=[jax.ShapeDtypeStruct((3,) + g.shape[1:], BF16) for g in chip_bf16],
        scratch_shapes=[pltpu.SemaphoreType.DMA((n, 3)), pltpu.SemaphoreType.DMA((n, 3))],
    )(*chip_bf16)


def _sum_chips_and_share(chip_f32, landed):
    x, y, _ = _coords()
    totals = [_sum_chips(lax.dynamic_index_in_dim(f, 2 * x + y, axis=0, keepdims=False), l, f"sum_chips_{a}")
              for a, (f, l) in enumerate(zip(chip_f32, landed))]
    copies = [(a, lambda x, y, c: (), a, lambda x, y, c: (), _flip(0, 0, 1)) for a in range(len(totals))]
    lands = [jax.ShapeDtypeStruct(t.shape, F32) for t in totals]
    return list(zip(totals, _exchange("swap_sums", totals, lands, copies, [])))


def kernel(x, conv_norm_g, conv_w_in, conv_w, conv_w_out, attn_norm_g, attn_w_in, attn_b_f, attn_q_norm_g, attn_k_norm_g, attn_w_out, loss_target, m_conv_norm_g, m_conv_w_in, m_conv_w, m_conv_w_out, m_attn_norm_g, m_attn_w_in, m_attn_b_f, m_attn_q_norm_g, m_attn_k_norm_g, m_attn_w_out, v_conv_norm_g, v_conv_w_in, v_conv_w, v_conv_w_out, v_attn_norm_g, v_attn_w_in, v_attn_b_f, v_attn_q_norm_g, v_attn_k_norm_g, v_attn_w_out):
    xi, yi, _ = _coords()
    chip = 2 * xi + yi
    D = x.shape[2]
    H = D // HEAD_DIM
    names = ["conv_norm_g", "conv_w_in", "conv_w", "conv_w_out", "attn_norm_g", "attn_w_in", "attn_b_f",
             "attn_q_norm_g", "attn_k_norm_g", "attn_w_out"]
    weights = dict(zip(names, [conv_norm_g, conv_w_in, conv_w, conv_w_out, attn_norm_g, attn_w_in, attn_b_f,
                               attn_q_norm_g, attn_k_norm_g, attn_w_out]))
    m_in = dict(zip(names, [m_conv_norm_g, m_conv_w_in, m_conv_w, m_conv_w_out, m_attn_norm_g, m_attn_w_in,
                            m_attn_b_f, m_attn_q_norm_g, m_attn_k_norm_g, m_attn_w_out]))
    v_in = dict(zip(names, [v_conv_norm_g, v_conv_w_in, v_conv_w, v_conv_w_out, v_attn_norm_g, v_attn_w_in,
                            v_attn_b_f, v_attn_q_norm_g, v_attn_k_norm_g, v_attn_w_out]))
    weights = {k: w[0] for k, w in weights.items()}
    m_in = {k: w[0] for k, w in m_in.items()}
    v_in = {k: w[0] for k, w in v_in.items()}

    big_names = ["conv_w_in", "attn_w_in", "conv_w_out", "attn_w_out"]
    halved = {k: weights[k].astype(BF16).reshape(2, weights[k].shape[0] // 2, weights[k].shape[1]) for k in big_names}
    q = D // 4
    small_w = jnp.concatenate([weights["conv_w"], weights["attn_norm_g"][None, :], jnp.zeros((4, q), F32)], axis=0)
    g_in, g_out, g_small = _all_gather([halved["conv_w_in"], halved["conv_w_out"]], [small_w])
    w_in = g_in.reshape(4, D, D)
    w_out = g_out.reshape(D, D)
    conv_w_full = g_small[:, 0:3, :].transpose(1, 0, 2).reshape(3, D)
    attn_g_full = g_small[:, 3, :].reshape(1, D)
    g1 = weights["conv_norm_g"][None, :]

    conv_acts, (ga_in, ga_out) = _conv_fwd(x[0], g1, w_in, conv_w_full, w_out, [halved["attn_w_in"], halved["attn_w_out"]])
    def core_sums(by_chip, tag, swapped):
        f32, bf16, _ = _sum_cores([g.reshape(4, 2, g.shape[1] // 2, g.shape[2]) for g in by_chip], None, tag, swapped)
        return f32, bf16

    loss_part, grad_x, grads, (early_f32, early_arrived) = _after_conv(
        conv_acts, x[0], loss_target[0], g1, w_in, conv_w_full, w_out, attn_g_full, ga_in.reshape(4, D, D + H // 4),
        weights["attn_b_f"][None, :], weights["attn_q_norm_g"][None, :], weights["attn_k_norm_g"][None, :],
        ga_out.reshape(D, D), reduce_early=core_sums)

    tail = jnp.concatenate([grads["attn_b_f"], grads["attn_q_norm_g"], grads["attn_k_norm_g"],
                            jnp.reshape(loss_part, (1, 1)), jnp.zeros((1, D - H - 2 * HEAD_DIM - 1), F32)], axis=1)
    small = jnp.concatenate([grads["conv_norm_g"], grads["conv_w"], grads["attn_norm_g"], tail,
                             jnp.zeros((2, D), F32)], axis=0)
    out_f32, out_bf16, small_sum = _sum_cores([grads["conv_w_out"].reshape(4, 2, D // 8, D)], small, "conv_out")
    big_names = ["attn_w_in", "attn_w_out", "conv_w_in", "conv_w_out"]
    reduced = _sum_chips_and_share(early_f32 + out_f32, early_arrived + list(_send_chip_sums(out_bf16)))
    final = {}
    final["conv_norm_g"] = small_sum[0]
    final["conv_w"] = lax.dynamic_slice_in_dim(small_sum[1:4], chip * q, q, axis=1)
    final["attn_norm_g"] = lax.dynamic_slice_in_dim(small_sum[4], chip * q, q, axis=0)
    final["attn_b_f"] = small_sum[5, :H]
    final["attn_q_norm_g"] = small_sum[5, H:H + HEAD_DIM]
    final["attn_k_norm_g"] = small_sum[5, H + HEAD_DIM:H + 2 * HEAD_DIM]
    loss = small_sum[5, H + 2 * HEAD_DIM]

    delta, new_m, new_v = {}, {}, {}
    core = jnp.reshape(lax.axis_index("c"), (1,)).astype(jnp.int32)
    for k, (mine, other) in zip(big_names, reduced):
        final[k], delta[k], new_m[k], new_v[k] = _adamw_halves(weights[k], mine, other, m_in[k], v_in[k], core,
                                                               "adamw_" + k)
    for k in names:
        if k in big_names:
            continue
        shape = weights[k].shape
        as2d = (lambda a: a.reshape(1, -1)) if len(shape) == 1 else (lambda a: a)
        d, m2, v2 = _adamw(as2d(weights[k]), as2d(final[k]), as2d(m_in[k]), as2d(v_in[k]), "adamw_" + k)
        delta[k], new_m[k], new_v[k] = d.reshape(shape), m2.reshape(shape), v2.reshape(shape)
    lead = lambda a: a[None]
    return (loss, grad_x[None], *[lead(final[k]) for k in names], *[lead(delta[k]) for k in names],
            *[lead(new_m[k]) for k in names], *[lead(new_v[k]) for k in names])
```

```python
import functools

import jax
import jax.numpy as jnp
from jax import lax
from jax.experimental import pallas as pl
from jax.experimental.pallas import tpu as pltpu

F32 = jnp.float32
BF16 = jnp.bfloat16
HEAD_DIM = 64
LANES = 128
RMS_EPS = 1e-6
NEG = -1e30
Q_SCALE = 0.125
ROW_TILE = 256
CONV_TILE = 512
ATT_GROUP = 4
SKIP_LOG = 106.0
PLAIN_EXP_MAX = 60.0
TN_ROWS = 2048
VMEM_LIMIT = 56 << 20
ADAM_LR, ADAM_B1, ADAM_B2, ADAM_EPS, ADAM_WD, ADAM_STEP = 0.001, 0.9, 0.999, 1e-08, 0.01, 10
MESH = pl.DeviceIdType.MESH
ANY = pl.BlockSpec(memory_space=pl.ANY)


def _lane():
    return lax.broadcasted_iota(jnp.int32, (1, LANES), 1)


def _split3(x):
    hi = x.astype(BF16).astype(F32)
    r = x - hi
    mid = r.astype(BF16).astype(F32)
    lo = (r - mid).astype(BF16).astype(F32)
    return hi, mid, lo


STAT_STRIDE = 16
ONE_LANE = 3 * STAT_STRIDE


def _pack3(x, lane, one):
    hi, mid, lo = _split3(x)
    packed = hi + pltpu.roll(mid, STAT_STRIDE, 1) + pltpu.roll(lo, 2 * STAT_STRIDE, 1)
    return jnp.where(lane == ONE_LANE, one, packed).astype(BF16)


def _scatter_matrices(H):
    rows = lax.broadcasted_iota(jnp.int32, (LANES, H * LANES), 0)
    cols = lax.broadcasted_iota(jnp.int32, (LANES, H * LANES), 1)
    head, within = cols // LANES, cols % LANES
    extra = within - _aug(head % 2)
    term = (rows < ONE_LANE) & (rows % STAT_STRIDE == head)
    first = ((term & (extra == rows // STAT_STRIDE)) | ((rows == ONE_LANE) & (extra >= 3) & (extra < 6)))
    second = ((term & (extra - 3 == rows // STAT_STRIDE)) | ((rows == ONE_LANE) & (extra >= 0) & (extra < 3)))
    return first.astype(BF16), second.astype(BF16)


def _col(x, lane, idx):
    return jnp.sum(jnp.where(lane == idx, x, 0.0), axis=1, keepdims=True)


def _feat(parity):
    return HEAD_DIM * parity


def _aug(parity):
    return HEAD_DIM * (1 - parity)


def _own(lane, parity):
    return (lane >= _feat(parity)) & (lane < _feat(parity) + HEAD_DIM)


def _head_tile(ref, hd, lane):
    j = hd // 2
    return jnp.where(_own(lane, hd % 2), ref[:, LANES * j:LANES * (j + 1)], 0.0)


def _pair_tile(even, odd, lane):
    return jnp.where(lane < HEAD_DIM, even, odd)


def _sigmoid(x):
    return 0.5 * jnp.tanh(0.5 * x) + 0.5


def _dot(a, b):
    return jnp.dot(a, b, preferred_element_type=F32)


def _dot_nt(a, b):
    return lax.dot_general(a, b, (((1,), (1,)), ((), ())), preferred_element_type=F32)


def _dot_tn(a, b):
    return lax.dot_general(a, b, (((0,), (0,)), ((), ())), preferred_element_type=F32)


def _dot01(tri, x):
    hi, mid, lo = _split3(x)
    return _dot(tri, hi.astype(BF16)) + _dot(tri, mid.astype(BF16)) + _dot(tri, lo.astype(BF16))


def _rms_bwd(dh, x, g):
    inv = lax.rsqrt(jnp.mean(x * x, axis=-1, keepdims=True) + RMS_EPS)
    xh = x * inv
    dxn = dh * g
    dx = inv * (dxn - xh * jnp.mean(dxn * xh, axis=-1, keepdims=True))
    return dx, jnp.sum(dh * xh, axis=0, keepdims=True)


def _head_rms_bwd(dn, t, g, ones):
    sq = t * t
    hi = sq.astype(BF16)
    lo = (sq - hi.astype(F32)).astype(BF16)
    inv = lax.rsqrt((_dot(hi, ones) + _dot(lo, ones)) * (1.0 / HEAD_DIM) + RMS_EPS)
    th = t * inv
    gd = dn * g
    d = inv * (gd - th * (jnp.sum(gd * th, axis=1, keepdims=True) * (1.0 / HEAD_DIM)))
    return d, jnp.sum(dn * th, axis=0, keepdims=True)


def _params(n_grid):
    return pltpu.CompilerParams(dimension_semantics=("arbitrary",) * n_grid, vmem_limit_bytes=VMEM_LIMIT)


def _rows(tm, cols, rev=None):
    if rev is None:
        return pl.BlockSpec((tm, cols), lambda i: (i, 0))
    return pl.BlockSpec((tm, cols), lambda i: (rev - i, 0))


def _whole(shape, buffers=None):
    mode = {} if buffers is None else dict(pipeline_mode=pl.Buffered(buffers))
    return pl.BlockSpec(shape, lambda *_: (0,) * len(shape), **mode)


RING_SLOTS = 3


def _ring_fetch(arrays, slots, sems, tm, steps):
    s = pl.program_id(0)

    def copy(k, step):
        slot = step % RING_SLOTS
        rows = pl.ds(pl.multiple_of(step * tm, tm), tm)
        return pltpu.make_async_copy(arrays[k].at[rows, :], slots[k].at[slot], sems.at[k, slot])

    @pl.when(s == 0)
    def _():
        for step in range(min(RING_SLOTS - 1, steps)):
            for k in range(len(arrays)):
                copy(k, step).start(priority=k % 2)

    @pl.when(s + RING_SLOTS - 1 < steps)
    def _():
        for k in range(len(arrays)):
            copy(k, s + RING_SLOTS - 1).start(priority=k % 2)

    for k in range(len(arrays)):
        copy(k, s).wait()
    return [slots[k].at[s % RING_SLOTS] for k in range(len(arrays))]


def _conv_fwd(x, g1, w_in, conv_w, w_out, later):
    S, D = x.shape
    tm = min(CONV_TILE, S)
    sub = min(ROW_TILE, tm)
    steps = S // tm
    n = len(later)

    def body(x_ref, g_ref, win_ref, cw_ref, wout_ref, *rest):
        shard_refs, rest = rest[:n], rest[n:]
        proj_ref, h_ref, yc_ref, y_ref, x1_ref = rest[:5]
        gathered_refs, rest = rest[5:5 + n], rest[5 + n:]
        prev_u = rest[0]
        if n:
            start, forward, finish = _gather_plan(shard_refs, gathered_refs, rest[1], rest[2])
            pl.when(pl.program_id(0) == 0)(start)
            pl.when(pl.program_id(0) == steps // 2)(forward)

        @pl.when(pl.program_id(0) == 0)
        def _():
            prev_u[...] = jnp.zeros((sub, D), F32)

        for r in range(0, tm, sub):
            rows = slice(r, r + sub)
            xv = x_ref[rows, :]
            inv = lax.rsqrt(jnp.mean(xv * xv, axis=-1, keepdims=True) + RMS_EPS)
            h = (xv * inv * g_ref[...]).astype(BF16)
            h_ref[rows, :] = h
            for j in range(4):
                proj_ref[rows, j * D:(j + 1) * D] = _dot(h, win_ref[j])
            u = proj_ref[rows, D:2 * D] * proj_ref[rows, 2 * D:3 * D]
            pu = prev_u[...]
            row = lax.broadcasted_iota(jnp.int32, (sub, 1), 0)
            u1 = jnp.where(row < 1, pltpu.roll(pu, 1, 0), pltpu.roll(u, 1, 0))
            u2 = jnp.where(row < 2, pltpu.roll(pu, 2, 0), pltpu.roll(u, 2, 0))
            prev_u[...] = u
            w = cw_ref[...]
            yc = w[2:3] * u + w[1:2] * u1 + w[0:1] * u2
            yc_ref[rows, :] = yc
            z = proj_ref[rows, 3 * D:4 * D]
            y = (proj_ref[rows, 0:D] * yc * (z * _sigmoid(z))).astype(BF16)
            y_ref[rows, :] = y
            x1_ref[rows, :] = xv + _dot(y, wout_ref[...])

        if n:
            pl.when(pl.program_id(0) == steps - 1)(finish)

    results = pl.pallas_call(
        body, name="conv_fwd", grid=(steps,),
        in_specs=[_rows(tm, D), _whole((1, D)), _whole((4, D, D), 1), _whole((3, D)), _whole((D, D), 1)] + [ANY] * n,
        out_specs=[_rows(tm, 4 * D), _rows(tm, D), _rows(tm, D), _rows(tm, D), _rows(tm, D)] + [ANY] * n,
        out_shape=[jax.ShapeDtypeStruct((S, 4 * D), F32), jax.ShapeDtypeStruct((S, D), BF16),
                   jax.ShapeDtypeStruct((S, D), F32), jax.ShapeDtypeStruct((S, D), BF16),
                   jax.ShapeDtypeStruct((S, D), F32)] + [jax.ShapeDtypeStruct((4,) + a.shape, a.dtype) for a in later],
        scratch_shapes=[pltpu.VMEM((sub, D), F32)] + [pltpu.SemaphoreType.DMA((n, 7))] * (2 if n else 0),
        compiler_params=_params(1),
    )(x, g1, w_in, conv_w, w_out, *later)
    return results[:5], results[5:]


def _conv_bwd(dproj2, wa, wf, x1, g2, dx2, x, g1, w_in, w_out, conv_w, proj, yc):
    S, D = x.shape
    tm = min(ROW_TILE, S)
    sub = min(ROW_TILE, tm)
    last = S // tm - 1

    def body(dp2_ref, wa_ref, wf_ref, x1_ref, g2_ref, dx2_ref, x_ref, g_ref, win_ref, wout_ref, cw_ref, proj_ref, yc_ref,
             dproj_ref, dx_ref, dx1b_ref, dg_ref, dcw_ref, dg2_ref, next_d):
        @pl.when(pl.program_id(0) == 0)
        def _():
            dg_ref[...] = jnp.zeros((1, D), F32)
            dcw_ref[...] = jnp.zeros((3, D), F32)
            dg2_ref[...] = jnp.zeros((1, D), F32)
            next_d[...] = jnp.zeros((sub, D), F32)

        for r in range(tm - sub, -1, -sub):
            rows = slice(r, r + sub)
            dh2 = _dot_nt(dp2_ref[rows, 0:4 * D], wa_ref[...]) + _dot_nt(dp2_ref[rows, 4 * D:4 * D + LANES], wf_ref[...])
            dxn2, dg2 = _rms_bwd(dh2, x1_ref[rows, :], g2_ref[...])
            dg2_ref[...] += dg2
            dx1v = dx2_ref[rows, :] + dxn2
            dx1b = dx1v.astype(BF16)
            dx1b_ref[rows, :] = dx1b
            dy = _dot_nt(dx1b, wout_ref[...])
            b = proj_ref[rows, 0:D]
            c = proj_ref[rows, D:2 * D]
            xin = proj_ref[rows, 2 * D:3 * D]
            z = proj_ref[rows, 3 * D:4 * D]
            sg = _sigmoid(z)
            sz = z * sg
            ycv = yc_ref[rows, :]
            d0 = dy * b * sz
            dproj_ref[rows, 0:D] = (dy * ycv * sz).astype(BF16)
            dproj_ref[rows, 3 * D:4 * D] = (dy * b * ycv * (sg * (1.0 + z * (1.0 - sg)))).astype(BF16)
            nd = next_d[...]
            row = lax.broadcasted_iota(jnp.int32, (sub, 1), 0)
            d1 = jnp.where(row >= sub - 1, pltpu.roll(nd, sub - 1, 0), pltpu.roll(d0, sub - 1, 0))
            d2 = jnp.where(row >= sub - 2, pltpu.roll(nd, sub - 2, 0), pltpu.roll(d0, sub - 2, 0))
            next_d[...] = d0
            w = cw_ref[...]
            du = w[2:3] * d0 + w[1:2] * d1 + w[0:1] * d2
            u = c * xin
            dcw_ref[2:3, :] += jnp.sum(d0 * u, axis=0, keepdims=True)
            dcw_ref[1:2, :] += jnp.sum(d1 * u, axis=0, keepdims=True)
            dcw_ref[0:1, :] += jnp.sum(d2 * u, axis=0, keepdims=True)
            dproj_ref[rows, D:2 * D] = (du * xin).astype(BF16)
            dproj_ref[rows, 2 * D:3 * D] = (du * c).astype(BF16)
            dh = _dot_nt(dproj_ref[rows, 0:D], win_ref[0])
            for j in range(1, 4):
                dh = dh + _dot_nt(dproj_ref[rows, j * D:(j + 1) * D], win_ref[j])
            dxn, dg = _rms_bwd(dh, x_ref[rows, :], g_ref[...])
            dx_ref[rows, :] = dx1v + dxn
            dg_ref[...] += dg

    return pl.pallas_call(
        body, name="conv_bwd", grid=(S // tm,),
        in_specs=[_rows(tm, 4 * D + LANES, last), _whole((D, 4 * D), 1), _whole((D, LANES), 1), _rows(tm, D, last),
                  _whole((1, D)), _rows(tm, D, last),
                  _rows(tm, D, last), _whole((1, D)), _whole((4, D, D), 1), _whole((D, D), 1),
                  _whole((3, D)), _rows(tm, 4 * D, last), _rows(tm, D, last)],
        out_specs=[_rows(tm, 4 * D, last), _rows(tm, D, last), _rows(tm, D, last), _whole((1, D)), _whole((3, D)),
                   _whole((1, D))],
        out_shape=[jax.ShapeDtypeStruct((S, 4 * D), BF16), jax.ShapeDtypeStruct((S, D), F32),
                   jax.ShapeDtypeStruct((S, D), BF16), jax.ShapeDtypeStruct((1, D), F32),
                   jax.ShapeDtypeStruct((3, D), F32), jax.ShapeDtypeStruct((1, D), F32)],
        scratch_shapes=[pltpu.VMEM((sub, D), F32)],
        compiler_params=_params(1),
    )(dproj2, wa, wf, x1, g2, dx2, x, g1, w_in, w_out, conv_w, proj, yc)


def _attn_front(x1, g2, w, wf, bf, gq, gk):
    S, D = x1.shape
    H = D // HEAD_DIM
    tm = min(ROW_TILE, S)
    tri = (lax.broadcasted_iota(jnp.int32, (tm, tm), 1) <= lax.broadcasted_iota(jnp.int32, (tm, tm), 0)).astype(BF16)

    def body(x_ref, g_ref, w_ref, wf_ref, bf_ref, gq_ref, gk_ref, tri_ref, first_ref, second_ref,
             h_ref, qh_ref, kh_ref, z_ref, f_ref, ends_ref, rel_ref, qa_ref, ka_ref, va_ref, vt_ref,
             carry, v_s, qraw_ref, kraw_ref):
        @pl.when(pl.program_id(0) == 0)
        def _():
            carry[...] = jnp.zeros((8, LANES), F32)

        xv = x_ref[...]
        inv = lax.rsqrt(jnp.mean(xv * xv, axis=-1, keepdims=True) + RMS_EPS)
        h = (xv * inv * g_ref[...]).astype(BF16)
        h_ref[...] = h
        qraw_ref[...] = _dot(h, w_ref[:, 0:D])
        kraw_ref[...] = _dot(h, w_ref[:, D:2 * D])
        v_s[...] = _dot(h, w_ref[:, 2 * D:3 * D])
        z_ref[...] = _dot(h, w_ref[:, 3 * D:4 * D])
        lane = _lane()
        f = _dot(h, wf_ref[...]) + bf_ref[...]
        f_ref[...] = f
        logf = jnp.where(lane < H, jnp.minimum(f, 0.0) - jnp.log(1.0 + jnp.exp(-jnp.abs(f))), 0.0)
        cs = _dot01(tri_ref[...], logf) + carry[0:1, :]
        carry[...] = jnp.broadcast_to(cs[tm - 1:tm, :], (8, LANES))
        diags = jnp.zeros((tm, LANES), F32)
        for hd in range(H):
            sl = slice(LANES * hd, LANES * (hd + 1))
            a = _aug(hd % 2)
            if hd % 2 == 0:
                qh_ref[hd // 2] = qraw_ref[:, LANES * (hd // 2):LANES * (hd // 2 + 1)]
                kh_ref[hd // 2] = kraw_ref[:, LANES * (hd // 2):LANES * (hd // 2 + 1)]
            qt = _head_tile(qraw_ref, hd, lane)
            qn = qt * lax.rsqrt(jnp.sum(qt * qt, axis=1, keepdims=True) * (1.0 / HEAD_DIM) + RMS_EPS) * gq_ref[...]
            kt = _head_tile(kraw_ref, hd, lane)
            kn = kt * lax.rsqrt(jnp.sum(kt * kt, axis=1, keepdims=True) * (1.0 / HEAD_DIM) + RMS_EPS) * gk_ref[...]
            diags = diags + jnp.where(lane == hd, jnp.sum(qn * kn, axis=1, keepdims=True) * Q_SCALE, 0.0)
            qa_ref[:, sl] = (qn * Q_SCALE).astype(BF16)
            ka_ref[:, sl] = kn.astype(BF16)
            va = jnp.where((lane >= a) & (lane < a + 3), 1.0, _head_tile(v_s, hd, lane))
            va_ref[:, sl] = va.astype(BF16)
            vt_ref[hd] = va.T.astype(BF16)
        rel = cs - diags
        rel_ref[...] = rel
        row = lax.broadcasted_iota(jnp.int32, (8, LANES), 0)
        ends_ref[...] = jnp.where(row == 0, cs[0:1, :],
                                  jnp.where(row == 1, cs[tm - 1:tm, :], jnp.min(diags, axis=0, keepdims=True)))
        qa_ref[...] += _dot(_pack3(rel, lane, 1.0), first_ref[...]).astype(BF16)
        ka_ref[...] += _dot(_pack3(-cs, lane, 1.0), second_ref[...]).astype(BF16)

    nb = S // tm
    heads = pl.BlockSpec((H // 2, tm, LANES), lambda i: (0, i, 0))
    return pl.pallas_call(
        body, name="attn_front", grid=(nb,),
        in_specs=[_rows(tm, D), _whole((1, D)), _whole((D, 4 * D)), _whole((D, LANES)), _whole((1, LANES)),
                  _whole((1, LANES)), _whole((1, LANES)), _whole((tm, tm)), _whole((LANES, H * LANES)),
                  _whole((LANES, H * LANES))],
        out_specs=[_rows(tm, D), heads, heads, _rows(tm, D), _rows(tm, LANES),
                   pl.BlockSpec((None, 8, LANES), lambda i: (i, 0, 0)), _rows(tm, LANES),
                   _rows(tm, H * LANES), _rows(tm, H * LANES), _rows(tm, H * LANES),
                   pl.BlockSpec((H, None, LANES, tm), lambda i: (0, i, 0, 0))],
        out_shape=[jax.ShapeDtypeStruct((S, D), BF16), jax.ShapeDtypeStruct((H // 2, S, LANES), F32),
                   jax.ShapeDtypeStruct((H // 2, S, LANES), F32), jax.ShapeDtypeStruct((S, D), F32),
                   jax.ShapeDtypeStruct((S, LANES), F32), jax.ShapeDtypeStruct((nb, 8, LANES), F32),
                   jax.ShapeDtypeStruct((S, LANES), F32),
                   jax.ShapeDtypeStruct((S, H * LANES), BF16), jax.ShapeDtypeStruct((S, H * LANES), BF16),
                   jax.ShapeDtypeStruct((S, H * LANES), BF16), jax.ShapeDtypeStruct((H, nb, LANES, tm), BF16)],
        scratch_shapes=[pltpu.VMEM((8, LANES), F32), pltpu.VMEM((tm, D), F32), pltpu.VMEM((tm, D), F32),
                        pltpu.VMEM((tm, D), F32)],
        compiler_params=_params(1),
    )(x1, g2, w, wf, bf, gq, gk, tri, *_scatter_matrices(H))


def _skip_tables(first, last, lowest, gq, gk, G):
    nb = first.shape[0]
    bound = HEAD_DIM ** 0.5 * jnp.max(jnp.abs(gq)) * jnp.max(jnp.abs(gk))
    lowest = jnp.maximum(lowest, -bound)
    idx = jnp.arange(nb)
    margin = (SKIP_LOG + bound) - lowest
    need = (last[None, :, :] <= first[:, None, :] + margin[:, None, :]) & (idx[None, :, None] < idx[:, None, None])
    need = need | (idx[None, :, None] == idx[:, None, None])
    kstart = jnp.argmax(need, axis=1)
    qend = nb - 1 - jnp.argmax(need[::-1], axis=0)
    kstart = jnp.min(kstart.reshape(2 * nb // G, G // 2, -1), axis=1)
    kstart = kstart - (kstart & 1)
    qend = jnp.max(qend.reshape(2 * nb // G, G // 2, -1), axis=1)
    return kstart.T.astype(jnp.int32), qend.T.astype(jnp.int32), bound


def _attn_fwd(kstart, qa, ka, vt, online_max):
    S = qa.shape[0]
    H = qa.shape[1] // LANES
    nb, T = vt.shape[1], vt.shape[3]
    G = 2 * nb // kstart.shape[1]
    W = G * T

    def finish(acc, shift, o_ref, lse_ref):
        a = _aug(pl.program_id(0) % 2)
        feat = lax.broadcasted_iota(jnp.int32, (LANES, 1), 0)
        l = jnp.sum(jnp.where(feat == a, acc, 0.0), axis=0, keepdims=True)
        o_ref[...] = (acc * (1.0 / l)).T
        lse_ref[...] = shift + jnp.log(l)

    def causal(st):
        return jnp.where(lax.broadcasted_iota(jnp.int32, st.shape, 0) <= lax.broadcasted_iota(jnp.int32, st.shape, 1),
                         st, NEG)

    def fast_body(ks_ref, q_ref, k_ref, vt_ref, o_ref, lse_ref, acc_ref, sa_ref, sb_ref, sc_ref):
        h, g = pl.program_id(0), pl.program_id(1)
        q = q_ref[...]
        acc_ref[...] = jnp.zeros((LANES, W), F32)

        def scores(ki, lo):
            return _dot_nt(k_ref[pl.ds(pl.multiple_of(ki * T, T), 2 * T), :], q[lo * T:, :])

        def weighted(ki, p):
            return _dot(vt_ref[ki], p[:T]) + _dot(vt_ref[ki + 1], p[T:])

        first = ks_ref[h, 2 * g + 1]
        early = jnp.minimum(ks_ref[h, 2 * g], first)

        def narrow(i, carry):
            ki = early + 2 * i
            st = _dot_nt(k_ref[pl.ds(pl.multiple_of(ki * T, T), 2 * T), :], q[:W // 2, :])
            acc_ref[:, :W // 2] += weighted(ki, jnp.exp(st).astype(BF16))
            return carry

        lax.fori_loop(0, (first - early) // 2, narrow, 0)
        steps = (g * G - first) // 2
        sa_ref[...] = scores(first, 0)

        def advance(ki, cur_ref, next_ref):
            p = jnp.exp(cur_ref[...]).astype(BF16)
            next_ref[...] = scores(ki + 2, 0)
            acc_ref[...] += weighted(ki, p)

        def loop(i, carry):
            advance(first + 4 * i, sa_ref, sb_ref)
            advance(first + 4 * i + 2, sb_ref, sa_ref)
            return carry

        lax.fori_loop(0, steps // 2, loop, 0)

        def first_own(pending_ref):
            p = jnp.exp(causal(pending_ref[...])).astype(BF16)
            if G > 2:
                sc_ref[:, :W - 2 * T] = scores(g * G + 2, 2)
            acc_ref[...] += weighted(g * G, p)

        @pl.when(steps % 2 == 1)
        def _():
            advance(g * G - 2, sa_ref, sb_ref)
            first_own(sb_ref)

        @pl.when(steps % 2 == 0)
        def _():
            first_own(sa_ref)

        if G > 2:
            acc_ref[:, 2 * T:] += weighted(g * G + 2, jnp.exp(causal(sc_ref[:, :W - 2 * T])).astype(BF16))
        for j in range(4, G, 2):
            p = jnp.exp(causal(scores(g * G + j, j))).astype(BF16)
            acc_ref[:, j * T:] += weighted(g * G + j, p)
        finish(acc_ref[...], 0.0, o_ref, lse_ref)

    def online_body(ks_ref, q_ref, k_ref, vt_ref, o_ref, lse_ref, acc_ref, m_ref):
        h, g = pl.program_id(0), pl.program_id(1)
        q = q_ref[...]
        m_ref[...] = jnp.full((8, W), NEG, F32)
        acc_ref[...] = jnp.zeros((LANES, W), F32)

        def update(st, vtb, lo):
            m_old = m_ref[0:1, lo:]
            m_new = jnp.maximum(m_old, jnp.max(st, axis=0, keepdims=True))
            p = jnp.exp(st - m_new).astype(BF16)
            acc_ref[:, lo:] = acc_ref[:, lo:] * jnp.exp(m_old - m_new) + _dot(vtb, p)
            m_ref[:, lo:] = jnp.broadcast_to(m_new, (8, W - lo))

        def loop(ki, carry):
            kb = k_ref[pl.ds(pl.multiple_of(ki * T, T), T), :]
            update(_dot_nt(kb, q), vt_ref[ki], 0)
            return carry

        lax.fori_loop(jnp.minimum(ks_ref[h, 2 * g], ks_ref[h, 2 * g + 1]), g * G, loop, 0)
        for j in range(G):
            ki = g * G + j
            kb = k_ref[pl.ds(pl.multiple_of(ki * T, T), T), :]
            update(causal(_dot_nt(kb, q[j * T:, :])), vt_ref[ki], j * T)
        finish(acc_ref[...], m_ref[0:1, :], o_ref, lse_ref)

    return pl.pallas_call(
        online_body if online_max else fast_body, name="attn_fwd_online" if online_max else "attn_fwd",
        grid_spec=pltpu.PrefetchScalarGridSpec(
            num_scalar_prefetch=1, grid=(H, nb // G),
            in_specs=[pl.BlockSpec((W, LANES), lambda h, i, ks: (i, h)),
                      pl.BlockSpec((S, LANES), lambda h, i, ks: (0, h)),
                      pl.BlockSpec((None, nb, LANES, T), lambda h, i, ks: (h, 0, 0, 0))],
            out_specs=[pl.BlockSpec((W, LANES), lambda h, i, ks: (i, h)),
                       pl.BlockSpec((None, 1, W), lambda h, i, ks: (h, 0, i))],
            scratch_shapes=[pltpu.VMEM((LANES, W), F32)] + (
                [pltpu.VMEM((8, W), F32)] if online_max else [pltpu.VMEM((2 * T, W), F32)] * 3)),
        out_shape=[jax.ShapeDtypeStruct((S, H * LANES), F32), jax.ShapeDtypeStruct((H, 1, S), F32)],
        compiler_params=_params(2),
    )(kstart, qa, ka, vt)


def _attn_out(o_aug, lse, rel, z, x1, target, w_out, qa):
    S, D = x1.shape
    H = D // HEAD_DIM
    tm = min(ROW_TILE, S)
    steps = S // tm
    ringed = [o_aug, z, x1, target, qa]

    def body(*refs):
        n = len(ringed)
        w_ref, first_ref, rel_ref, lse_ref = refs[n:n + 4]
        dx2_ref, dx2b_ref, o2b_ref, dz_ref, doa_ref, qa2_ref, loss_ref, oc_s, do_s = refs[n + 4:n + 13]
        o_ref, z_ref, x1_ref, t_ref, q_ref = _ring_fetch(refs[:n], refs[n + 13:2 * n + 13], refs[2 * n + 13], tm, steps)

        @pl.when(pl.program_id(0) == 0)
        def _():
            loss_ref[...] = jnp.zeros((1, LANES), F32)

        lane = _lane()
        for j in range(H // 2):
            oc_s[:, LANES * j:LANES * (j + 1)] = _pair_tile(o_ref[:, 2 * LANES * j:2 * LANES * j + LANES],
                                                            o_ref[:, 2 * LANES * j + LANES:2 * LANES * (j + 1)], lane)
        oc = oc_s[...]
        zv = z_ref[...]
        sg = _sigmoid(zv)
        sz = zv * sg
        o2 = (oc * sz).astype(BF16)
        o2b_ref[...] = o2
        e = x1_ref[...] + _dot(o2, w_ref[...]) - t_ref[...]
        sq = jnp.sum(jnp.sum(e * e, axis=1, keepdims=True), axis=0, keepdims=True)
        loss_ref[...] += jnp.broadcast_to(sq * (0.5 / D), (1, LANES))
        dx2 = e * (1.0 / D)
        dx2_ref[...] = dx2
        dx2b = dx2.astype(BF16)
        dx2b_ref[...] = dx2b
        do2 = _dot_nt(dx2b, w_ref[...])
        dz_ref[...] = (do2 * oc * (sg * (1.0 + zv * (1.0 - sg)))).astype(BF16)
        do_s[...] = do2 * sz
        deltas = jnp.zeros((tm, LANES), F32)
        for hd in range(H):
            dt = _head_tile(do_s, hd, lane)
            delta = jnp.sum(dt * _head_tile(oc_s, hd, lane), axis=1, keepdims=True)
            deltas = deltas + jnp.where(lane == hd, delta, 0.0)
            doa_ref[:, LANES * hd:LANES * (hd + 1)] = dt.astype(BF16)
        doa_ref[...] += _dot(_pack3(-deltas, lane, 0.0), first_ref[...]).astype(BF16)
        lse = jnp.concatenate([lse_ref[...], jnp.zeros((LANES - H, tm), F32)], axis=0).T
        rq = rel_ref[...] - lse
        tile_lane = lax.broadcasted_iota(jnp.int32, (1, H * LANES), 1)
        extra = tile_lane % LANES - _aug((tile_lane // LANES) % 2)
        kept = jnp.where((extra >= 0) & (extra < 3), jnp.zeros((), BF16), q_ref[...])
        qa2_ref[...] = kept + _dot(_pack3(rq, lane, 0.0), first_ref[...]).astype(BF16)

    return pl.pallas_call(
        body, name="attn_out", grid=(steps,),
        in_specs=[ANY] * len(ringed) + [_whole((D, D)), _whole((LANES, H * LANES)), _rows(tm, LANES),
                                        pl.BlockSpec((H, tm), lambda i: (0, i))],
        out_specs=[_rows(tm, D), _rows(tm, D), _rows(tm, D), _rows(tm, D), _rows(tm, H * LANES),
                   _rows(tm, H * LANES), _whole((1, LANES))],
        out_shape=[jax.ShapeDtypeStruct((S, D), F32), jax.ShapeDtypeStruct((S, D), BF16),
                   jax.ShapeDtypeStruct((S, D), BF16), jax.ShapeDtypeStruct((S, D), BF16),
                   jax.ShapeDtypeStruct((S, H * LANES), BF16), jax.ShapeDtypeStruct((S, H * LANES), BF16),
                   jax.ShapeDtypeStruct((1, LANES), F32)],
        scratch_shapes=[pltpu.VMEM((tm, D), F32), pltpu.VMEM((tm, D), F32)]
        + [pltpu.VMEM((RING_SLOTS, tm, a.shape[1]), a.dtype) for a in ringed]
        + [pltpu.SemaphoreType.DMA((len(ringed), RING_SLOTS))],
        compiler_params=_params(1),
    )(*ringed, w_out, _scatter_matrices(H)[0], rel, lse)


def _attn_bwd(qend, qa2, doa, ka, va, T):
    S = qa2.shape[0]
    H = qa2.shape[1] // LANES
    nb = S // T
    G = 2 * nb // qend.shape[1]
    W = G * T

    def body(qe_ref, q_ref, do_ref, k_ref, v_ref, dq_ref, dk_ref, dv_ref, dkt_acc, dvt_acc):
        h, g = pl.program_id(0), pl.program_id(1)

        @pl.when(g == 0)
        def _():
            dq_ref[...] = jnp.zeros((S, LANES), F32)

        kb = k_ref[...]
        vb = v_ref[...]
        dkt_acc[...] = jnp.zeros((LANES, W), F32)
        dvt_acc[...] = jnp.zeros((LANES, W), F32)

        def step(qi, c0, c1, masked):
            rows = pl.ds(pl.multiple_of(qi * T, T), 2 * T)
            qb = q_ref[rows, :]
            dob = do_ref[rows, :]
            s = _dot_nt(qb, kb[c0:c1])
            if masked:
                query = lax.broadcasted_iota(jnp.int32, s.shape, 0) + (c1 - 2 * T)
                s = jnp.where(lax.broadcasted_iota(jnp.int32, s.shape, 1) <= query, s, NEG)
            p = jnp.exp(s)
            ds = (p * _dot_nt(dob, vb[c0:c1])).astype(BF16)
            dvt_acc[:, c0:c1] += _dot(dob.astype(F32).T.astype(BF16), p.astype(BF16))
            dkt_acc[:, c0:c1] += _dot(qb.astype(F32).T.astype(BF16), ds)
            dq_ref[rows, :] += _dot(ds, kb[c0:c1])

        for m in range(G // 2):
            step(g * G + 2 * m, 0, (m + 1) * 2 * T, True)
        first = g * G + G
        n_all = jnp.maximum((qe_ref[h, 2 * g] - first + 2) // 2, 0)
        second = first + 2 * n_all

        def all_keys(i, carry):
            step(first + 2 * i, 0, W, False)
            return carry

        def late_keys(i, carry):
            step(second + 2 * i, W // 2, W, False)
            return carry

        lax.fori_loop(0, n_all, all_keys, 0)
        lax.fori_loop(0, (qe_ref[h, 2 * g + 1] - second + 2) // 2, late_keys, 0)
        dk_ref[...] = dkt_acc[...].T
        dv_ref[...] = dvt_acc[...].T.astype(BF16)

    heads = pl.BlockSpec((None, W, LANES), lambda h, i, qe: (h, i, 0))
    return pl.pallas_call(
        body, name="attn_bwd",
        grid_spec=pltpu.PrefetchScalarGridSpec(
            num_scalar_prefetch=1, grid=(H, nb // G),
            in_specs=[pl.BlockSpec((S, LANES), lambda h, i, qe: (0, h)), pl.BlockSpec((S, LANES), lambda h, i, qe: (0, h)),
                      pl.BlockSpec((W, LANES), lambda h, i, qe: (i, h)), pl.BlockSpec((W, LANES), lambda h, i, qe: (i, h))],
            out_specs=[pl.BlockSpec((None, S, LANES), lambda h, i, qe: (h, 0, 0)), heads, heads],
            scratch_shapes=[pltpu.VMEM((LANES, W), F32), pltpu.VMEM((LANES, W), F32)]),
        out_shape=[jax.ShapeDtypeStruct((H, S, LANES), F32), jax.ShapeDtypeStruct((H, S, LANES), F32),
                   jax.ShapeDtypeStruct((H, S, LANES), BF16)],
        compiler_params=_params(2),
    )(qend, qa2, doa, ka, va)


def _attn_proj_bwd(dqt, dka, dva, qraw, kraw, dz, f, gq, gk):
    S, D = dz.shape
    H = D // HEAD_DIM
    tm = min(CONV_TILE, S)
    last = S // tm - 1
    tri = (lax.broadcasted_iota(jnp.int32, (tm, tm), 1) >= lax.broadcasted_iota(jnp.int32, (tm, tm), 0)).astype(BF16)

    def body(dq_ref, dk_ref, dv_ref, q_ref, k_ref, dz_ref, f_ref, gq_ref, gk_ref, tri_ref, ones_ref,
             dproj_ref, small_ref, carry, pairs):
        @pl.when(pl.program_id(0) == 0)
        def _():
            small_ref[...] = jnp.zeros((8, LANES), F32)
            carry[...] = jnp.zeros((8, LANES), F32)

        lane = _lane()

        def head_pair(j, acc):
            dcs, dgq, dgk = acc
            dq2, dk2 = [], []
            q_pair, k_pair = q_ref[j], k_ref[j]
            for parity in (0, 1):
                hd = 2 * j + parity
                own, a = _own(lane, parity), _aug(parity)
                dqf = dq_ref[hd]
                dqn = jnp.where(own, dqf * Q_SCALE, 0.0)
                d, dg = _head_rms_bwd(dqn, jnp.where(own, q_pair, 0.0), gq_ref[...], ones_ref[...])
                dq2.append(d)
                dgq = dgq + dg
                dkt = dk_ref[hd]
                dcs = dcs + jnp.where(lane == hd, _col(dqf, lane, a) - _col(dkt, lane, a + 3), 0.0)
                d, dg = _head_rms_bwd(jnp.where(own, dkt, 0.0), jnp.where(own, k_pair, 0.0), gk_ref[...], ones_ref[...])
                dk2.append(d)
                dgk = dgk + dg
            pairs[0, j] = _pair_tile(*dq2, lane).astype(BF16)
            pairs[1, j] = _pair_tile(*dk2, lane).astype(BF16)
            pairs[2, j] = _pair_tile(dv_ref[2 * j], dv_ref[2 * j + 1], lane)
            return dcs, dgq, dgk

        zero = jnp.zeros((1, LANES), F32)
        dcs, dgq, dgk = lax.fori_loop(0, H // 2, head_pair, (jnp.zeros((tm, LANES), F32), zero, zero))
        for part in range(3):
            for j in range(H // 2):
                dproj_ref[:, part * D + LANES * j:part * D + LANES * (j + 1)] = pairs[part, j]
        dproj_ref[:, 3 * D:4 * D] = dz_ref[...]
        dlogf = _dot01(tri_ref[...], dcs) + carry[0:1, :]
        carry[...] = jnp.broadcast_to(dlogf[0:1, :], (8, LANES))
        df = dlogf * (1.0 / (1.0 + jnp.exp(f_ref[...])))
        dproj_ref[:, 4 * D:4 * D + LANES] = df.astype(BF16)
        small_ref[0:1, :] += jnp.sum(df, axis=0, keepdims=True)
        small_ref[1:2, :] += dgq
        small_ref[2:3, :] += dgk

    W = 4 * D + LANES
    heads = pl.BlockSpec((H, tm, LANES), lambda i: (0, last - i, 0))
    head_pairs = pl.BlockSpec((H // 2, tm, LANES), lambda i: (0, last - i, 0))
    return pl.pallas_call(
        body, name="attn_proj_bwd", grid=(S // tm,),
        in_specs=[heads, heads, heads, head_pairs, head_pairs,
                  _rows(tm, D, last), _rows(tm, LANES, last), _whole((1, LANES)), _whole((1, LANES)),
                  _whole((tm, tm)), _whole((LANES, LANES))],
        out_specs=[_rows(tm, W, last), _whole((8, LANES))],
        out_shape=[jax.ShapeDtypeStruct((S, W), BF16), jax.ShapeDtypeStruct((8, LANES), F32)],
        scratch_shapes=[pltpu.VMEM((8, LANES), F32), pltpu.VMEM((3, H // 2, tm, LANES), BF16)],
        compiler_params=_params(1),
    )(dqt, dka, dva, qraw, kraw, dz, f, gq, gk, tri, jnp.ones((LANES, LANES), BF16))


def _matmul_tn(a, b, col0, n, tn, name, stacked=False, carried=None):
    S, M = a.shape
    ts = min(TN_ROWS, S)
    off = col0 // tn
    grid = (n // tn, S // ts)
    sent, lands, build = carried or ([], [], None)
    k = len(sent)

    def body(a_ref, b_ref, *rest):
        o_ref = rest[k]
        j, s = pl.program_id(0), pl.program_id(1)
        if k:
            copies = build(rest[:k], rest[k + 1:2 * k + 1], rest[2 * k + 1], rest[2 * k + 2])

            @pl.when((j == 0) & (s == 0))
            def _():
                for cp in copies:
                    cp.start()

        @pl.when(s == 0)
        def _():
            o_ref[...] = jnp.zeros((M, tn), F32)

        o_ref[...] += _dot_tn(a_ref[...], b_ref[...])

        if k:
            @pl.when((j == grid[0] - 1) & (s == grid[1] - 1))
            def _():
                for cp in copies:
                    cp.wait()

    if stacked:
        out_spec, out_shape = pl.BlockSpec((None, M, tn), lambda j, s: (j, 0, 0)), (n // tn, M, tn)
    else:
        out_spec, out_shape = pl.BlockSpec((M, tn), lambda j, s: (0, j)), (M, n)
    results = pl.pallas_call(
        body, name=name, grid=grid,
        in_specs=[pl.BlockSpec((ts, M), lambda j, s: (s, 0)), pl.BlockSpec((ts, tn), lambda j, s: (s, off + j))]
        + [ANY] * k,
        out_specs=[out_spec] + [ANY] * k,
        out_shape=[jax.ShapeDtypeStruct(out_shape, F32)] + list(lands),
        scratch_shapes=[pltpu.SemaphoreType.DMA((k, 4))] * (2 if k else 0),
        compiler_params=_params(2),
    )(a, b, *sent)
    return results[0] if carried is None else (results[0], list(results[1:]))


def _adam_update(gv, w_ref, m_ref, v_ref, d_ref, m2_ref, v2_ref):
    m2 = ADAM_B1 * m_ref[...] + (1.0 - ADAM_B1) * gv
    v2 = ADAM_B2 * v_ref[...] + (1.0 - ADAM_B2) * (gv * gv)
    m2_ref[...] = m2
    v2_ref[...] = v2
    m_hat = m2 / (1.0 - ADAM_B1 ** ADAM_STEP)
    v_hat = v2 / (1.0 - ADAM_B2 ** ADAM_STEP)
    d_ref[...] = -ADAM_LR * (m_hat / (jnp.sqrt(v_hat) + ADAM_EPS) + ADAM_WD * w_ref[...])


def _adamw(w, g, m, v, name):
    r, c = w.shape
    tr = ROW_TILE if r % ROW_TILE == 0 else r

    def body(w_ref, g_ref, m_ref, v_ref, d_ref, m2_ref, v2_ref):
        _adam_update(g_ref[...], w_ref, m_ref, v_ref, d_ref, m2_ref, v2_ref)

    spec = _rows(tr, c)
    return pl.pallas_call(
        body, name=name, grid=(r // tr,), in_specs=[spec] * 4, out_specs=[spec] * 3,
        out_shape=[jax.ShapeDtypeStruct((r, c), F32)] * 3, compiler_params=_params(1),
    )(w, g, m, v)


def _adamw_halves(w, mine, other, m, v, core, name):
    r, c = mine.shape
    tr = ROW_TILE if r % ROW_TILE == 0 else r
    per = r // tr

    def body(core_ref, w_ref, mine_ref, other_ref, m_ref, v_ref, g_ref, d_ref, m2_ref, v2_ref):
        gv = jnp.where(pl.program_id(0) // per == core_ref[0], mine_ref[...], other_ref[...])
        g_ref[...] = gv
        _adam_update(gv, w_ref, m_ref, v_ref, d_ref, m2_ref, v2_ref)

    full = pl.BlockSpec((tr, c), lambda i, core: (i, 0))
    half = pl.BlockSpec((tr, c), lambda i, core: (i % per, 0))
    return pl.pallas_call(
        body, name=name,
        grid_spec=pltpu.PrefetchScalarGridSpec(num_scalar_prefetch=1, grid=(2 * per,),
                                               in_specs=[full, half, half, full, full], out_specs=[full] * 4),
        out_shape=[jax.ShapeDtypeStruct((2 * r, c), F32)] * 4, compiler_params=_params(1),
    )(core, w, mine, other, m, v)


def _after_conv(conv_acts, x, target, g1, w_in, conv_w, w_out, g2, wa_in, b_f, gq, gk, wa_out, reduce_early=None):
    reduces = reduce_early is not None
    if not reduces:
        reduce_early = lambda by_chip, tag, swapped: ([], [])
    S, D = x.shape
    H = D // HEAD_DIM
    ws = wa_in.shape[2]
    w_qkvz = jnp.concatenate([wa_in[0], wa_in[1], wa_in[2], wa_in[3][:, :4 * D - 3 * ws]], axis=1)
    wf = jnp.pad(wa_in[3][:, 4 * D - 3 * ws:], ((0, 0), (0, LANES - H)))
    bf = jnp.pad(b_f, ((0, 0), (0, LANES - H)))
    gq128 = jnp.concatenate([gq, gq], axis=1)
    gk128 = jnp.concatenate([gk, gk], axis=1)

    proj, h1, yc, y, x1 = conv_acts
    h2, qraw, kraw, z, f, ends, rel, qa, ka, va, vt = _attn_front(x1, g2, w_qkvz, wf, bf, gq128, gk128)
    T = vt.shape[3]
    kstart, qend, bound = _skip_tables(ends[:, 0, :H], ends[:, 1, :H], ends[:, 2, :H], gq, gk, min(ATT_GROUP, S // T))
    o_aug, lse = lax.cond(2.0 * bound <= PLAIN_EXP_MAX, functools.partial(_attn_fwd, online_max=False),
                          functools.partial(_attn_fwd, online_max=True), kstart, qa, ka, vt)
    dx2, dx2b, o2b, dz, doa, qa2, loss = _attn_out(o_aug, lse.reshape(H, S), rel, z, x1, target, wa_out, qa)
    dqt, dka, dva = _attn_bwd(qend, qa2, doa, ka, va, T)
    dproj2, small = _attn_proj_bwd(dqt, dka, dva, qraw, kraw, dz, f, gq128, gk128)

    tn = min(1024, D)
    carry = (lambda kind, arrays: kind(arrays if reduces else []))
    dw_main = _matmul_tn(h2, dproj2, 0, 4 * D, 2 * tn, "dw_attn_in")
    dw_f = _matmul_tn(h2, dproj2, 4 * D, LANES, LANES, "dw_attn_f")
    dwa_in = jnp.stack([dw_main[:, 0:ws], dw_main[:, ws:2 * ws], dw_main[:, 2 * ws:3 * ws],
                        jnp.concatenate([dw_main[:, 3 * ws:], dw_f[:, :H]], axis=1)])
    dwa_out, swapped = _matmul_tn(o2b, dx2b, 0, D, tn, "dw_attn_out",
                                  carried=carry(_carried_core_swap, [dwa_in.reshape(4, 2, D // 2, ws)]))
    attn_f32, attn_bf16 = reduce_early([dwa_in, dwa_out.reshape(4, D // 4, D)], "attn", swapped)
    dproj1, dx, dx1b, dg1, dcw, dg2 = _conv_bwd(dproj2, w_qkvz, wf, x1, g2, dx2, x, g1, w_in, w_out, conv_w, proj, yc)
    dw_in, attn_arrived = _matmul_tn(h1, dproj1, 0, 4 * D, D, "dw_conv_in", stacked=True,
                                     carried=carry(_carried_chip_sums, attn_bf16))
    in_f32, in_bf16 = reduce_early([dw_in], "conv_in", [])
    dw_out, in_arrived = _matmul_tn(y, dx1b, 0, D, tn, "dw_conv_out", carried=carry(_carried_chip_sums, in_bf16))
    grads = dict(conv_norm_g=dg1, conv_w_in=dw_in, conv_w=dcw, conv_w_out=dw_out, attn_norm_g=dg2,
                 attn_w_in=dwa_in, attn_b_f=small[0:1, :H],
                 attn_q_norm_g=small[1:2, :HEAD_DIM] + small[1:2, HEAD_DIM:],
                 attn_k_norm_g=small[2:3, :HEAD_DIM] + small[2:3, HEAD_DIM:], attn_w_out=dwa_out)
    return loss[0, 0], dx, grads, (attn_f32 + in_f32, attn_arrived + in_arrived)


def _coords():
    return lax.axis_index("x"), lax.axis_index("y"), lax.axis_index("c")


def _at(ref, idx):
    return ref.at[idx] if idx else ref


def _other_chips(x, y):
    return [(1 - x, y), (x, 1 - y), (1 - x, 1 - y)]


def _gather_plan(src, out, send, recv):
    x, y, c = _coords()
    mine = 2 * x + y
    sibling = (x, y, 1 - c)
    others = [(a, k, 2 * px + py, (px, py)) for a in range(len(src)) for k, (px, py) in enumerate(_other_chips(x, y))]

    def copy(a, k, chip, half, to, source=None):
        dst = out[a].at[chip, half]
        return pltpu.make_async_remote_copy(src_ref=dst if source is None else source, dst_ref=dst,
                                            send_sem=send.at[a, k], recv_sem=recv.at[a, k],
                                            device_id=to, device_id_type=MESH)

    first = [copy(a, k, mine, c, (*chip, c), source=src[a].at[c]) for a, k, _, chip in others]
    own = [pltpu.make_async_remote_copy(src_ref=src[a], dst_ref=out[a].at[mine], send_sem=send.at[a, 6],
                                        recv_sem=recv.at[a, 6], device_id=sibling, device_id_type=MESH)
           for a in range(len(src))]
    passed = [copy(a, 3 + k, slot, c, sibling) for a, k, slot, _ in others]

    def start():
        for cp in first + own:
            cp.start()

    def forward():
        for (a, k, slot, _), cp in zip(others, passed):
            copy(a, k, slot, c, (x, y, c)).wait_recv()
            cp.start()

    def finish():
        for a, k, slot, _ in others:
            copy(a, 3 + k, slot, 1 - c, (x, y, c)).wait_recv()
        for cp in own:
            cp.wait_recv()
        for cp in first + passed + own:
            cp.wait_send()

    return start, forward, finish


def _all_gather(halved, whole):
    nh, nw = len(halved), len(whole)

    def body(*refs):
        src_h, src_w = refs[:nh], refs[nh:nh + nw]
        out_h, out_w = refs[nh + nw:2 * nh + nw], refs[2 * nh + nw:2 * (nh + nw)]
        send_h, recv_h, send_w, recv_w = refs[2 * (nh + nw):]
        x, y, c = _coords()
        mine = 2 * x + y
        chips = _other_chips(x, y)

        def copy_w(a, k, chip, to):
            return pltpu.make_async_remote_copy(src_ref=src_w[a], dst_ref=out_w[a].at[chip],
                                                send_sem=send_w.at[a, k], recv_sem=recv_w.at[a, k],
                                                device_id=to, device_id_type=MESH)

        start, forward, finish = _gather_plan(src_h, out_h, send_h, recv_h)
        small = [copy_w(a, k, mine, (*chip, c)) for a in range(nw) for k, chip in enumerate(chips)]
        small += [copy_w(a, 3, mine, (x, y, 1 - c)) for a in range(nw)]
        start()
        for cp in small:
            cp.start()
        forward()
        finish()
        for a in range(nw):
            for k, (px, py) in enumerate(chips):
                copy_w(a, k, 2 * px + py, (x, y, c)).wait_recv()
            copy_w(a, 3, mine, (x, y, c)).wait_recv()
        for cp in small:
            cp.wait_send()

    out_shape = [jax.ShapeDtypeStruct((4,) + a.shape, a.dtype) for a in list(halved) + list(whole)]
    return pl.pallas_call(
        body, name="gather_weights", in_specs=[ANY] * (nh + nw), out_specs=[ANY] * (nh + nw), out_shape=out_shape,
        scratch_shapes=[pltpu.SemaphoreType.DMA((nh, 7)), pltpu.SemaphoreType.DMA((nh, 7)),
                        pltpu.SemaphoreType.DMA((nw, 4)), pltpu.SemaphoreType.DMA((nw, 4))],
    )(*halved, *whole)


def _exchange(name, srcs, lands, copies, local_copies):
    ns, nl, n, nloc = len(srcs), len(lands), len(copies), len(local_copies)

    def body(*refs):
        src, land = refs[:ns], refs[ns:ns + nl]
        send, recv, local_sem = refs[ns + nl:]
        me = _coords()
        started = []
        for k, (si, s_at, li, l_at, ci) in enumerate(local_copies):
            cp = pltpu.make_async_copy(_at(src[si], s_at(*me)), _at(land[li], l_at(*me)), local_sem.at[k])
            cp.start()
            started.append(cp)
        remote = []
        for k, (si, s_at, li, l_at, peer) in enumerate(copies):
            cp = pltpu.make_async_remote_copy(src_ref=_at(src[si], s_at(*me)), dst_ref=_at(land[li], l_at(*me)),
                                              send_sem=send.at[k], recv_sem=recv.at[k],
                                              device_id=peer(*me), device_id_type=MESH)
            cp.start()
            remote.append(cp)
        for cp in remote:
            cp.wait()
        for cp in started:
            cp.wait()

    return pl.pallas_call(
        body, name=name, in_specs=[ANY] * ns, out_specs=[ANY] * nl, out_shape=list(lands),
        scratch_shapes=[pltpu.SemaphoreType.DMA((n,)), pltpu.SemaphoreType.DMA((n,)),
                        pltpu.SemaphoreType.DMA((max(nloc, 1),))],
    )(*srcs)


def _add_pairs(a, b, core, name):
    _, r, cols = b.shape
    tr = ROW_TILE if r % ROW_TILE == 0 else r

    def body(core_ref, a_ref, b_ref, o_ref, ob_ref):
        s = a_ref[...] + b_ref[...]
        o_ref[...] = s
        ob_ref[...] = s.astype(BF16)

    spec = pl.BlockSpec((None, tr, cols), lambda j, i, core: (j, i, 0))
    return pl.pallas_call(
        body, name=name,
        grid_spec=pltpu.PrefetchScalarGridSpec(
            num_scalar_prefetch=1, grid=(4, r // tr),
            in_specs=[pl.BlockSpec((None, None, tr, cols), lambda j, i, core: (j, core[0], i, 0)), spec],
            out_specs=[spec, spec]),
        out_shape=[jax.ShapeDtypeStruct(b.shape, F32), jax.ShapeDtypeStruct(b.shape, BF16)],
        compiler_params=_params(2),
    )(core, a, b)


def _sum_chips(own, landed, name):
    _, r, cols = landed.shape
    tr = ROW_TILE if r % ROW_TILE == 0 else r

    def body(own_ref, land_ref, o_ref):
        acc = own_ref[...]
        for j in range(3):
            acc = acc + land_ref[j].astype(F32)
        o_ref[...] = acc

    return pl.pallas_call(
        body, name=name, grid=(r // tr,),
        in_specs=[_rows(tr, cols), pl.BlockSpec((3, tr, cols), lambda i: (0, i, 0))], out_specs=_rows(tr, cols),
        out_shape=jax.ShapeDtypeStruct((r, cols), F32), compiler_params=_params(1),
    )(own, landed)


def _sum_devices(landed, name):
    def body(l_ref, o_ref):
        acc = l_ref[0]
        for j in range(1, 8):
            acc = acc + l_ref[j]
        o_ref[...] = acc

    return pl.pallas_call(body, name=name, out_shape=jax.ShapeDtypeStruct(landed.shape[1:], F32))(landed)


CHIP_FLIPS = [(1, 0), (0, 1), (1, 1)]


def _flip(fx, fy, fc):
    return lambda x, y, c: (x ^ fx, y ^ fy, c ^ fc)


def _core_swap_copies(big, landed, send, recv):
    x, y, c = _coords()
    return [pltpu.make_async_remote_copy(src_ref=big[a].at[j, 1 - c], dst_ref=landed[a].at[j],
                                         send_sem=send.at[a, j], recv_sem=recv.at[a, j],
                                         device_id=(x, y, 1 - c), device_id_type=MESH)
            for a in range(len(big)) for j in range(4)]


def _carried_core_swap(big):
    return big, [jax.ShapeDtypeStruct((4,) + g.shape[2:], F32) for g in big], _core_swap_copies


def _carried_chip_sums(sums):
    return sums, [jax.ShapeDtypeStruct((3,) + g.shape[1:], g.dtype) for g in sums], _chip_sum_copies


def _sum_cores(big, small, tag, swapped=()):
    c = lax.axis_index("c")
    rest = big[len(swapped):]
    nb = len(rest)
    copies = [(a, (lambda j: lambda x, y, c: (j, 1 - c))(j), a, (lambda j: lambda x, y, c: (j,))(j), _flip(0, 0, 1))
              for a in range(nb) for j in range(4)]
    lands = [jax.ShapeDtypeStruct((4,) + g.shape[2:], F32) for g in rest]
    srcs, local = list(rest), []
    if small is not None:
        flips = [(fx, fy, fc) for fx in (0, 1) for fy in (0, 1) for fc in (0, 1) if fx or fy or fc]
        copies += [(nb, lambda x, y, c: (), nb, lambda x, y, c: (4 * x + 2 * y + c,), _flip(*f)) for f in flips]
        lands.append(jax.ShapeDtypeStruct((8,) + small.shape, F32))
        local = [(nb, lambda x, y, c: (), nb, lambda x, y, c: (4 * x + 2 * y + c,), None)]
        srcs.append(small)
    landed = _exchange("swap_halves_" + tag, srcs, lands, copies, local)
    small_sum = None if small is None else _sum_devices(landed[nb], "sum_small")
    landed = list(swapped) + list(landed[:nb])
    nb = len(big)
    core = jnp.reshape(c, (1,)).astype(jnp.int32)
    sums = [_add_pairs(big[a], landed[a], core, f"add_cores_{tag}_{a}") for a in range(nb)]
    return [s for s, _ in sums], [sb for _, sb in sums], small_sum


def _chip_sum_copies(sums, landed, send, recv):
    x, y, c = _coords()
    return [pltpu.make_async_remote_copy(src_ref=sums[a].at[2 * (x ^ fx) + (y ^ fy)], dst_ref=landed[a].at[k],
                                         send_sem=send.at[a, k], recv_sem=recv.at[a, k],
                                         device_id=(x ^ fx, y ^ fy, c), device_id_type=MESH)
            for a in range(len(sums)) for k, (fx, fy) in enumerate(CHIP_FLIPS)]


def _send_chip_sums(chip_bf16):
    n = len(chip_bf16)

    def body(*refs):
        copies = _chip_sum_copies(refs[:n], refs[n:2 * n], refs[2 * n], refs[2 * n + 1])
        for cp in copies:
            cp.start()
        for cp in copies:
            cp.wait()

    return pl.pallas_call(
        body, name="send_chip_sums", in_specs=[ANY] * n, out_specs=[ANY] * n,
        out_shape=[jax.ShapeDtypeStruct((3,) + g.shape[1:], BF16) for g in chip_bf16],
        scratch_shapes=[pltpu.SemaphoreType.DMA((n, 3)), pltpu.SemaphoreType.DMA((n, 3))],
    )(*chip_bf16)


def _sum_chips_and_share(chip_f32, landed):
    x, y, _ = _coords()
    totals = [_sum_chips(lax.dynamic_index_in_dim(f, 2 * x + y, axis=0, keepdims=False), l, f"sum_chips_{a}")
              for a, (f, l) in enumerate(zip(chip_f32, landed))]
    copies = [(a, lambda x, y, c: (), a, lambda x, y, c: (), _flip(0, 0, 1)) for a in range(len(totals))]
    lands = [jax.ShapeDtypeStruct(t.shape, F32) for t in totals]
    return list(zip(totals, _exchange("swap_sums", totals, lands, copies, [])))


def kernel(x, conv_norm_g, conv_w_in, conv_w, conv_w_out, attn_norm_g, attn_w_in, attn_b_f, attn_q_norm_g, attn_k_norm_g, attn_w_out, loss_target, m_conv_norm_g, m_conv_w_in, m_conv_w, m_conv_w_out, m_attn_norm_g, m_attn_w_in, m_attn_b_f, m_attn_q_norm_g, m_attn_k_norm_g, m_attn_w_out, v_conv_norm_g, v_conv_w_in, v_conv_w, v_conv_w_out, v_attn_norm_g, v_attn_w_in, v_attn_b_f, v_attn_q_norm_g, v_attn_k_norm_g, v_attn_w_out):
    xi, yi, _ = _coords()
    chip = 2 * xi + yi
    D = x.shape[2]
    H = D // HEAD_DIM
    names = ["conv_norm_g", "conv_w_in", "conv_w", "conv_w_out", "attn_norm_g", "attn_w_in", "attn_b_f",
             "attn_q_norm_g", "attn_k_norm_g", "attn_w_out"]
    weights = dict(zip(names, [conv_norm_g, conv_w_in, conv_w, conv_w_out, attn_norm_g, attn_w_in, attn_b_f,
                               attn_q_norm_g, attn_k_norm_g, attn_w_out]))
    m_in = dict(zip(names, [m_conv_norm_g, m_conv_w_in, m_conv_w, m_conv_w_out, m_attn_norm_g, m_attn_w_in,
                            m_attn_b_f, m_attn_q_norm_g, m_attn_k_norm_g, m_attn_w_out]))
    v_in = dict(zip(names, [v_conv_norm_g, v_conv_w_in, v_conv_w, v_conv_w_out, v_attn_norm_g, v_attn_w_in,
                            v_attn_b_f, v_attn_q_norm_g, v_attn_k_norm_g, v_attn_w_out]))
    weights = {k: w[0] for k, w in weights.items()}
    m_in = {k: w[0] for k, w in m_in.items()}
    v_in = {k: w[0] for k, w in v_in.items()}

    big_names = ["conv_w_in", "attn_w_in", "conv_w_out", "attn_w_out"]
    halved = {k: weights[k].astype(BF16).reshape(2, weights[k].shape[0] // 2, weights[k].shape[1]) for k in big_names}
    q = D // 4
    small_w = jnp.concatenate([weights["conv_w"], weights["attn_norm_g"][None, :], jnp.zeros((4, q), F32)], axis=0)
    g_in, g_out, g_small = _all_gather([halved["conv_w_in"], halved["conv_w_out"]], [small_w])
    w_in = g_in.reshape(4, D, D)
    w_out = g_out.reshape(D, D)
    conv_w_full = g_small[:, 0:3, :].transpose(1, 0, 2).reshape(3, D)
    attn_g_full = g_small[:, 3, :].reshape(1, D)
    g1 = weights["conv_norm_g"][None, :]

    conv_acts, (ga_in, ga_out) = _conv_fwd(x[0], g1, w_in, conv_w_full, w_out, [halved["attn_w_in"], halved["attn_w_out"]])
    def core_sums(by_chip, tag, swapped):
        f32, bf16, _ = _sum_cores([g.reshape(4, 2, g.shape[1] // 2, g.shape[2]) for g in by_chip], None, tag, swapped)
        return f32, bf16

    loss_part, grad_x, grads, (early_f32, early_arrived) = _after_conv(
        conv_acts, x[0], loss_target[0], g1, w_in, conv_w_full, w_out, attn_g_full, ga_in.reshape(4, D, D + H // 4),
        weights["attn_b_f"][None, :], weights["attn_q_norm_g"][None, :], weights["attn_k_norm_g"][None, :],
        ga_out.reshape(D, D), reduce_early=core_sums)

    tail = jnp.concatenate([grads["attn_b_f"], grads["attn_q_norm_g"], grads["attn_k_norm_g"],
                            jnp.reshape(loss_part, (1, 1)), jnp.zeros((1, D - H - 2 * HEAD_DIM - 1), F32)], axis=1)
    small = jnp.concatenate([grads["conv_norm_g"], grads["conv_w"], grads["attn_norm_g"], tail,
                             jnp.zeros((2, D), F32)], axis=0)
    out_f32, out_bf16, small_sum = _sum_cores([grads["conv_w_out"].reshape(4, 2, D // 8, D)], small, "conv_out")
    big_names = ["attn_w_in", "attn_w_out", "conv_w_in", "conv_w_out"]
    reduced = _sum_chips_and_share(early_f32 + out_f32, early_arrived + list(_send_chip_sums(out_bf16)))
    final = {}
    final["conv_norm_g"] = small_sum[0]
    final["conv_w"] = lax.dynamic_slice_in_dim(small_sum[1:4], chip * q, q, axis=1)
    final["attn_norm_g"] = lax.dynamic_slice_in_dim(small_sum[4], chip * q, q, axis=0)
    final["attn_b_f"] = small_sum[5, :H]
    final["attn_q_norm_g"] = small_sum[5, H:H + HEAD_DIM]
    final["attn_k_norm_g"] = small_sum[5, H + HEAD_DIM:H + 2 * HEAD_DIM]
    loss = small_sum[5, H + 2 * HEAD_DIM]

    delta, new_m, new_v = {}, {}, {}
    core = jnp.reshape(lax.axis_index("c"), (1,)).astype(jnp.int32)
    for k, (mine, other) in zip(big_names, reduced):
        final[k], delta[k], new_m[k], new_v[k] = _adamw_halves(weights[k], mine, other, m_in[k], v_in[k], core,
                                                               "adamw_" + k)
    for k in names:
        if k in big_names:
            continue
        shape = weights[k].shape
        as2d = (lambda a: a.reshape(1, -1)) if len(shape) == 1 else (lambda a: a)
        d, m2, v2 = _adamw(as2d(weights[k]), as2d(final[k]), as2d(m_in[k]), as2d(v_in[k]), "adamw_" + k)
        delta[k], new_m[k], new_v[k] = d.reshape(shape), m2.reshape(shape), v2.reshape(shape)
    lead = lambda a: a[None]
    return (loss, grad_x[None], *[lead(final[k]) for k in names], *[lead(delta[k]) for k in names],
            *[lead(new_m[k]) for k in names], *[lead(new_v[k]) for k in names])
```
